```python
import functools
import jax, jax.numpy as jnp
from jax import lax
import numpy as np

D_MODEL = 1024
BATCH = 2
SEQ = 8192
DEPTH = 1
DEC_BATCH = 32
DEC_SEQ = 16
PAST_LEN = 4096

CHUNK = 64
N_LEFT_CHUNKS = 8
ATT_WINDOW = N_LEFT_CHUNKS * CHUNK
N_HEADS_A = 8
HEAD_DIM_A = 64
D_ATT = N_HEADS_A * HEAD_DIM_A
REL_CLIP = 128
N_HEADS_R = 8
KEY_DIM_R = 64
VAL_DIM_R = 128
D_RET_K = N_HEADS_R * KEY_DIM_R
D_RET_V = N_HEADS_R * VAL_DIM_R
ROPE_BASE = 10000.0
N_EXPERTS = 256
TOP_K = 8
N_GROUPS = 8
TOPK_GROUPS = 4
D_EXPERT = 256
D_SHARED = 256
ROUTED_SCALE = 2.5
EXPERT_BLOCK = 64
EPS = 1e-6

IN_WIDTHS = (D_ATT, D_ATT, D_ATT, D_RET_K, D_RET_K, D_RET_V, D_RET_V, D_MODEL, D_MODEL)
D_IN = sum(IN_WIDTHS)
SPLIT_POINTS = tuple(int(v) for v in np.cumsum(IN_WIDTHS)[:-1])

kernel_name = 'hybrid_chunkband_retention_moe_stream_step'


def _rmsnorm(x, g=None):
    xf = x.astype(jnp.float32)
    y = xf * lax.rsqrt(jnp.mean(xf * xf, axis=-1, keepdims=True) + EPS)
    if g is not None:
        y = y * g.astype(jnp.float32)
    return y.astype(x.dtype)


def _rotary(x, pos):
    half = x.shape[-1] // 2
    inv_freq = ROPE_BASE ** (-jnp.arange(half, dtype=jnp.float32) / half)
    ang = pos[:, None] * inv_freq[None, :]
    cos = jnp.cos(ang)[None, :, None, :].astype(x.dtype)
    sin = jnp.sin(ang)[None, :, None, :].astype(x.dtype)
    x1, x2 = x[..., :half], x[..., half:]
    return jnp.concatenate([x1 * cos - x2 * sin, x1 * sin + x2 * cos], axis=-1)


def _rel_bias(rel_bias, q_pos, k_pos):
    dist = q_pos[:, None] - k_pos[None, :]
    idx = jnp.clip(dist, -REL_CLIP, REL_CLIP) + REL_CLIP
    return rel_bias[:, idx]


def _attend(q, k, v, bias, mask):
    s = jnp.einsum('...qhd,...khd->...hqk', q, k).astype(jnp.float32) * (HEAD_DIM_A ** -0.5)
    s = s + bias.astype(jnp.float32)
    if mask is not None:
        s = jnp.where(mask, s, -jnp.inf)
    p = jax.nn.softmax(s, axis=-1).astype(v.dtype)
    return jnp.einsum('...hqk,...khd->...qhd', p, v)


def _chunk_band(t):
    B, S = t.shape[:2]
    nc = S // CHUNK
    tc = t.reshape(B, nc, CHUNK, *t.shape[2:])
    tp = jnp.pad(tc, ((0, 0), (N_LEFT_CHUNKS, 0), (0, 0), (0, 0), (0, 0)))
    band = jnp.stack([tp[:, i:i + nc] for i in range(N_LEFT_CHUNKS + 1)], axis=2)
    return band.reshape(B, nc, (N_LEFT_CHUNKS + 1) * CHUNK, *t.shape[2:])


def _attn_prompt(q, k, v, rel_bias):
    B, S, H, d = q.shape
    nc = S // CHUNK
    band = (N_LEFT_CHUNKS + 1) * CHUNK
    qc = q.reshape(B, nc, CHUNK, H, d)
    kb, vb = _chunk_band(k), _chunk_band(v)
    bias = _rel_bias(rel_bias, jnp.arange(CHUNK) + ATT_WINDOW, jnp.arange(band))
    key_chunk = jnp.arange(nc)[:, None] - N_LEFT_CHUNKS + (jnp.arange(band) // CHUNK)[None, :]
    mask = (key_chunk >= 0)[:, None, None, :]
    o = _attend(qc, kb, vb, bias, mask).reshape(B, S, H, d)
    keep = min(ATT_WINDOW, S)
    return o, k[:, S - keep:], v[:, S - keep:]


def _attn_sample(q, k, v, rel_bias, cache_k, cache_v):
    T = q.shape[1]
    W = cache_k.shape[1]
    kk = jnp.concatenate([cache_k, k], axis=1)
    vv = jnp.concatenate([cache_v, v], axis=1)
    bias = _rel_bias(rel_bias, jnp.arange(T) + W, jnp.arange(W + T))
    o = _attend(q, kk, vv, bias, None)
    return o, k, v


def _log_decay():
    return jnp.log(1.0 - jnp.exp2(-5.0 - jnp.arange(N_HEADS_R, dtype=jnp.float32)))


def _retention_block(q, k, v, state, log_g):
    C = q.shape[2]
    i = jnp.arange(C, dtype=jnp.float32)
    diff = i[:, None] - i[None, :]
    dmask = jnp.where(diff >= 0, jnp.exp(log_g[:, None, None] * jnp.maximum(diff, 0.0)), 0.0).astype(q.dtype)
    qdec = jnp.exp(log_g[:, None] * (i + 1.0))[..., None].astype(q.dtype)
    kdec = jnp.exp(log_g[:, None] * (C - 1.0 - i))[..., None].astype(q.dtype)
    sdec = jnp.exp(log_g * C)[:, None, None].astype(q.dtype)
    inner = jnp.einsum('bhij,bhjv->bhiv', jnp.einsum('bhik,bhjk->bhij', q, k) * dmask, v)
    cross = jnp.einsum('bhik,bhkv->bhiv', q * qdec, state)
    new_state = sdec * state + jnp.einsum('bhjk,bhjv->bhkv', k * kdec, v)
    return inner + cross, new_state


def _ret_prompt(q, k, v):
    B, S, H, _ = q.shape
    nc = S // CHUNK
    log_g = _log_decay()

    def to_chunks(t):
        return t.reshape(B, nc, CHUNK, H, t.shape[-1]).transpose(1, 0, 3, 2, 4)

    def step(st, qkv):
        o, st = _retention_block(qkv[0], qkv[1], qkv[2], st, log_g)
        return st, o

    s0 = jnp.zeros((B, H, KEY_DIM_R, VAL_DIM_R), q.dtype)
    s_final, o = lax.scan(step, s0, (to_chunks(q), to_chunks(k), to_chunks(v)))
    o = o.transpose(1, 0, 3, 2, 4).reshape(B, S, H, VAL_DIM_R)
    return o, s_final


def _ret_sample(q, k, v, state):
    o, s = _retention_block(q.transpose(0, 2, 1, 3), k.transpose(0, 2, 1, 3),
                            v.transpose(0, 2, 1, 3), state.astype(q.dtype), _log_decay())
    return o.transpose(0, 2, 1, 3), s


def _swiglu(x, wg, wu, wd):
    return (jax.nn.silu(x @ wg) * (x @ wu)) @ wd


def _route(xf, w_router, b_router):
    N = xf.shape[0]
    s = jax.nn.sigmoid(jnp.dot(xf, w_router).astype(jnp.float32))
    sel = s + b_router.astype(jnp.float32)
    grp = sel.reshape(N, N_GROUPS, N_EXPERTS // N_GROUPS)
    gscore = lax.top_k(grp, 2)[0].sum(-1)
    _, gidx = lax.top_k(gscore, TOPK_GROUPS)
    gmask = jax.nn.one_hot(gidx, N_GROUPS, dtype=jnp.float32).sum(1) > 0
    emask = jnp.repeat(gmask, N_EXPERTS // N_GROUPS, axis=1)
    _, idx = lax.top_k(jnp.where(emask, sel, -jnp.inf), TOP_K)
    w = jnp.take_along_axis(s, idx, axis=-1)
    w = w / jnp.sum(w, axis=-1, keepdims=True) * ROUTED_SCALE
    return idx, w.astype(xf.dtype)


def _routed_experts(xf, idx, w, w_gate, w_up, w_down):
    N = xf.shape[0]
    A = N * TOP_K
    flat_e = idx.reshape(-1).astype(jnp.int32)
    flat_tok = jnp.arange(A, dtype=jnp.int32) // TOP_K
    flat_w = w.reshape(-1)
    order = jnp.argsort(flat_e)
    se = flat_e[order]
    counts = jnp.bincount(flat_e, length=N_EXPERTS).astype(jnp.int32)
    starts = jnp.cumsum(counts) - counts
    padded = ((counts + EXPERT_BLOCK - 1) // EXPERT_BLOCK) * EXPERT_BLOCK
    pends = jnp.cumsum(padded)
    pstarts = pends - padded
    dest = pstarts[se] + (jnp.arange(A, dtype=jnp.int32) - starts[se])
    nb = -(-A // EXPERT_BLOCK) + N_EXPERTS
    P = nb * EXPERT_BLOCK
    slot_tok = jnp.zeros((P,), jnp.int32).at[dest].set(flat_tok[order])
    slot_w = jnp.zeros((P,), xf.dtype).at[dest].set(flat_w[order])
    block_start = jnp.arange(nb, dtype=jnp.int32) * EXPERT_BLOCK
    block_e = jnp.minimum(jnp.searchsorted(pends, block_start, side='right'), N_EXPERTS - 1)

    def block(args):
        tok, e = args
        return _swiglu(xf[tok], w_gate[e], w_up[e], w_down[e])

    out = lax.map(block, (slot_tok.reshape(nb, EXPERT_BLOCK), block_e))
    out = out.reshape(P, -1) * slot_w[:, None]
    return jax.ops.segment_sum(out, slot_tok, num_segments=N)


def _moe(x, lw):
    B, S, D = x.shape
    xf = x.reshape(B * S, D)
    idx, w = _route(xf, lw['w_router'], lw['b_router'])
    y = _routed_experts(xf, idx, w, lw['w_exp_gate'], lw['w_exp_up'], lw['w_exp_down'])
    y = y + _swiglu(xf, lw['w_sh_gate'], lw['w_sh_up'], lw['w_sh_down'])
    return y.reshape(B, S, D)


def _layer(x, c, pos, attn_fn, ret_fn, lw):
    B, S, _ = x.shape
    mod = (jnp.dot(jax.nn.silu(c), lw['w_ada']) + lw['b_ada'])[:, None, :]
    sh1, sc1, g1, sh2, sc2, g2 = jnp.split(mod, 6, axis=-1)
    n = _rmsnorm(x, lw['norm1_g']) * (1 + sc1) + sh1
    proj = n @ lw['w_in']
    qa, ka, va, qr, kr, vr, gr, ga, gb = jnp.split(proj, SPLIT_POINTS, axis=-1)
    qa = qa.reshape(B, S, N_HEADS_A, HEAD_DIM_A)
    ka = ka.reshape(B, S, N_HEADS_A, HEAD_DIM_A)
    va = va.reshape(B, S, N_HEADS_A, HEAD_DIM_A)
    qr = _rotary(qr.reshape(B, S, N_HEADS_R, KEY_DIM_R), pos)
    kr = _rotary(kr.reshape(B, S, N_HEADS_R, KEY_DIM_R), pos) * (KEY_DIM_R ** -0.5)
    vr = vr.reshape(B, S, N_HEADS_R, VAL_DIM_R)
    oa, new_k, new_v = attn_fn(qa, ka, va, lw['rel_bias'])
    orr, new_s = ret_fn(qr, kr, vr)
    ya = oa.reshape(B, S, D_ATT) @ lw['w_o_attn']
    yr = (_rmsnorm(orr).reshape(B, S, D_RET_V) * jax.nn.silu(gr)) @ lw['w_o_ret']
    mix = (jax.nn.sigmoid(ga) * ya + jax.nn.sigmoid(gb) * yr) @ lw['w_out']
    h = x + g1 * mix
    n2 = _rmsnorm(h, lw['norm2_g']) * (1 + sc2) + sh2
    h = h + g2 * _moe(n2, lw)
    return h, new_k, new_v, new_s


def setup_inputs(seed: int = 0) -> dict:
    key = jax.random.key(seed)
    ks = jax.random.split(key, 26)
    f32 = jnp.float32

    def nrm(k, shape, scale):
        return jax.random.normal(k, shape, f32) * scale

    att_cache = min(ATT_WINDOW, PAST_LEN)
    L, D, E, F = DEPTH, D_MODEL, N_EXPERTS, D_EXPERT
    return {
        'x_prompt': nrm(ks[0], (BATCH, SEQ, D), 1.0),
        'x_sample': nrm(ks[1], (DEC_BATCH, DEC_SEQ, D), 1.0),
        'cache_attn_k': nrm(ks[2], (L, DEC_BATCH, att_cache, N_HEADS_A, HEAD_DIM_A), 1.0),
        'cache_attn_v': nrm(ks[3], (L, DEC_BATCH, att_cache, N_HEADS_A, HEAD_DIM_A), 1.0),
        'state_ret': nrm(ks[4], (L, DEC_BATCH, N_HEADS_R, KEY_DIM_R, VAL_DIM_R), 1.0),
        'c_prompt': nrm(ks[5], (BATCH, D), 1.0),
        'c_sample': nrm(ks[6], (DEC_BATCH, D), 1.0),
        'norm1_g': 1.0 + nrm(ks[7], (L, D), 0.02),
        'norm2_g': 1.0 + nrm(ks[8], (L, D), 0.02),
        'w_ada': nrm(ks[9], (L, D, 6 * D), 0.5 * D ** -0.5),
        'b_ada': nrm(ks[10], (L, 6 * D), 0.02),
        'w_in': nrm(ks[11], (L, D, D_IN), D ** -0.5),
        'rel_bias': nrm(ks[12], (L, N_HEADS_A, 2 * REL_CLIP + 1), 0.5),
        'w_o_attn': nrm(ks[13], (L, D_ATT, D), D_ATT ** -0.5),
        'w_o_ret': nrm(ks[14], (L, D_RET_V, D), D_RET_V ** -0.5),
        'w_out': nrm(ks[15], (L, D, D), D ** -0.5),
        'w_router': nrm(ks[16], (L, D, E), D ** -0.5),
        'b_router': nrm(ks[17], (L, E), 0.01),
        'w_exp_gate': nrm(ks[18], (L, E, D, F), D ** -0.5),
        'w_exp_up': nrm(ks[19], (L, E, D, F), D ** -0.5),
        'w_exp_down': nrm(ks[20], (L, E, F, D), F ** -0.5),
        'w_sh_gate': nrm(ks[21], (L, D, D_SHARED), D ** -0.5),
        'w_sh_up': nrm(ks[22], (L, D, D_SHARED), D ** -0.5),
        'w_sh_down': nrm(ks[23], (L, D_SHARED, D), D_SHARED ** -0.5),
        'normf_g': 1.0 + nrm(ks[24], (D,), 0.02),
    }


def reference(x_prompt, x_sample, cache_attn_k, cache_attn_v, state_ret, c_prompt, c_sample,
              norm1_g, norm2_g, w_ada, b_ada, w_in, rel_bias, w_o_attn, w_o_ret, w_out,
              w_router, b_router, w_exp_gate, w_exp_up, w_exp_down, w_sh_gate, w_sh_up, w_sh_down,
              normf_g):
    pos_p = jnp.arange(x_prompt.shape[1], dtype=jnp.float32)
    pos_s = PAST_LEN + jnp.arange(x_sample.shape[1], dtype=jnp.float32)
    hp, hs = x_prompt, x_sample
    kp_l, vp_l, sp_l, ks_l, vs_l, ss_l = [], [], [], [], [], []
    for l in range(DEPTH):
        lw = {
            'norm1_g': norm1_g[l], 'norm2_g': norm2_g[l], 'w_ada': w_ada[l], 'b_ada': b_ada[l],
            'w_in': w_in[l], 'rel_bias': rel_bias[l], 'w_o_attn': w_o_attn[l], 'w_o_ret': w_o_ret[l],
            'w_out': w_out[l], 'w_router': w_router[l], 'b_router': b_router[l],
            'w_exp_gate': w_exp_gate[l], 'w_exp_up': w_exp_up[l], 'w_exp_down': w_exp_down[l],
            'w_sh_gate': w_sh_gate[l], 'w_sh_up': w_sh_up[l], 'w_sh_down': w_sh_down[l],
        }
        hp, kp, vp, sp = _layer(hp, c_prompt, pos_p, _attn_prompt, _ret_prompt, lw)
        hs, kss, vss, sss = _layer(
            hs, c_sample, pos_s,
            functools.partial(_attn_sample, cache_k=cache_attn_k[l], cache_v=cache_attn_v[l]),
            functools.partial(_ret_sample, state=state_ret[l]), lw)
        kp_l.append(kp); vp_l.append(vp); sp_l.append(sp)
        ks_l.append(kss); vs_l.append(vss); ss_l.append(sss)
    y_prompt = _rmsnorm(hp, normf_g)
    y_sample = _rmsnorm(hs, normf_g)
    new_attn_k_prompt = jnp.stack(kp_l)
    new_attn_v_prompt = jnp.stack(vp_l)
    new_state_ret_prompt = jnp.stack(sp_l)
    new_attn_k_sample = jnp.stack(ks_l)
    new_attn_v_sample = jnp.stack(vs_l)
    new_state_ret_sample = jnp.stack(ss_l)
    return (y_prompt, y_sample, new_attn_k_prompt, new_attn_v_prompt, new_state_ret_prompt,
            new_attn_k_sample, new_attn_v_sample, new_state_ret_sample)
```

```python
import functools

import numpy as np
import jax
import jax.numpy as jnp
from jax import lax
from jax.experimental import pallas as pl
from jax.experimental.pallas import tpu as pltpu

F32 = jnp.float32
BF16 = jnp.bfloat16
I32 = jnp.int32
U32 = jnp.uint32

D_MODEL = 1024
PAST_LEN = 4096
CHUNK = 64
N_LEFT_CHUNKS = 8
ATT_WINDOW = N_LEFT_CHUNKS * CHUNK
N_HEADS = 8
HEAD_DIM_A = 64
D_ATT = N_HEADS * HEAD_DIM_A
REL_CLIP = 128
KEY_DIM_R = 64
VAL_DIM_R = 128
D_RET_K = N_HEADS * KEY_DIM_R
D_RET_V = N_HEADS * VAL_DIM_R
ROPE_BASE = 10000.0
N_EXPERTS = 256
TOP_K = 8
N_GROUPS = 8
GROUP_SIZE = N_EXPERTS // N_GROUPS
TOPK_GROUPS = 4
D_EXPERT = 256
ROUTED_SCALE = 2.5
EPS = 1e-6
IN_WIDTHS = (D_ATT, D_ATT, D_ATT, D_RET_K, D_RET_K, D_RET_V, D_RET_V, D_MODEL, D_MODEL)
IN_OFFS = tuple(int(v) for v in np.cumsum((0,) + IN_WIDTHS))
D_IN = IN_OFFS[-1]

NEG_BIG = -1e30
LANES = 128
V7X_VMEM_BYTES = 64 * 1024 * 1024

ROW_TILE = 512
ATT_QB = 256
RET_CHUNK = 256
ROUTE_TILE = 512
MOVE_TILE = 256
FFN_TILE = 128


def _cparams(semantics, vmem_mb):
    return pltpu.CompilerParams(dimension_semantics=semantics,
                                vmem_limit_bytes=min(vmem_mb * 1024 * 1024, V7X_VMEM_BYTES - (6 << 20)))


def _silu(x):
    return x * jax.nn.sigmoid(x)


def _pack_bf16_pair(lo, hi):
    lo_b = pltpu.bitcast(lo.astype(BF16).astype(F32), U32) >> 16
    hi_b = pltpu.bitcast(hi.astype(BF16).astype(F32), U32) & jnp.uint32(0xFFFF0000)
    return lo_b | hi_b


def _unpack_bf16_pair(u):
    lo = pltpu.bitcast(u << 16, F32)
    hi = pltpu.bitcast(u & jnp.uint32(0xFFFF0000), F32)
    return lo, hi


def _ada_kernel(c_ref, w_ref, b_ref, o_ref):
    sc = _silu(c_ref[...]).astype(BF16)
    o_ref[...] = jnp.dot(sc, w_ref[...], preferred_element_type=F32) + b_ref[...]


def _ada(c_all, w_ada_bf, b_ada):
    rows = c_all.shape[0]
    n_out = w_ada_bf.shape[1]
    blk = D_MODEL
    return pl.pallas_call(
        _ada_kernel,
        grid=(n_out // blk,),
        in_specs=[pl.BlockSpec((rows, D_MODEL), lambda j: (0, 0)),
                  pl.BlockSpec((D_MODEL, blk), lambda j: (0, j)),
                  pl.BlockSpec((1, blk), lambda j: (0, j))],
        out_specs=pl.BlockSpec((rows, blk), lambda j: (0, j)),
        out_shape=jax.ShapeDtypeStruct((rows, n_out), F32),
        compiler_params=_cparams(("arbitrary",), 24),
        name="ada",
    )(c_all, w_ada_bf, b_ada)


def _inproj_kernel(x_ref, sc_ref, sh_ref, g_ref, cos_ref, sin_ref, w_ref,
                   qa_ref, ka_ref, va_ref, qr_ref, kr_ref, vr_ref, gr_ref, ga_ref, gb_ref,
                   kv_ref, *, tiles_per_batch):
    x = x_ref[...]
    xn = x * lax.rsqrt(jnp.mean(x * x, axis=-1, keepdims=True) + EPS) * g_ref[...]
    nb = (xn * (1.0 + sc_ref[0]) + sh_ref[0]).astype(BF16)

    def proj(seg):
        return jnp.dot(nb, w_ref[:, IN_OFFS[seg]:IN_OFFS[seg + 1]], preferred_element_type=F32)

    qa_ref[...] = proj(0).astype(BF16)
    ka = proj(1)
    va = proj(2)
    ka_ref[...] = ka.astype(BF16)
    va_ref[...] = va.astype(BF16)

    @pl.when(pl.program_id(0) % tiles_per_batch == tiles_per_batch - 1)
    def _():
        kv_ref[:, :D_ATT] = ka
        kv_ref[:, D_ATT:] = va

    cos = cos_ref[...]
    sin = sin_ref[...]
    first_half = (lax.broadcasted_iota(I32, (1, D_RET_K), 1) % KEY_DIM_R) < (KEY_DIM_R // 2)

    def rotary(t):
        partner = jnp.where(first_half, pltpu.roll(t, D_RET_K - KEY_DIM_R // 2, 1),
                            pltpu.roll(t, KEY_DIM_R // 2, 1))
        return t * cos + partner * sin

    qr_ref[...] = rotary(proj(3)).astype(BF16)
    kr_ref[...] = (rotary(proj(4)) * (KEY_DIM_R ** -0.5)).astype(BF16)
    vr_ref[...] = proj(5).astype(BF16)
    gr_ref[...] = proj(6).astype(BF16)
    ga_ref[...] = proj(7).astype(BF16)
    gb_ref[...] = proj(8).astype(BF16)


def _inproj(x2d, sc, sh, g, cos_t, sin_t, w_in_bf, tiles_per_batch):
    n = x2d.shape[0]
    tm = ROW_TILE
    n_tiles = n // tm
    n_batches = n_tiles // tiles_per_batch
    mod_rows = sc.shape[1]
    pos_tiles = cos_t.shape[0] // tm

    def row_spec(width):
        return pl.BlockSpec((tm, width), lambda i: (i, 0))

    mod_spec = pl.BlockSpec((1, mod_rows, D_MODEL), lambda i: (i // tiles_per_batch, 0, 0))
    pos_spec = pl.BlockSpec((tm, D_RET_K), lambda i: (i % pos_tiles, 0))
    out_widths = (D_ATT, D_ATT, D_ATT, D_RET_K, D_RET_K, D_RET_V, D_RET_V, D_MODEL, D_MODEL)
    out_shape = [jax.ShapeDtypeStruct((n, w), BF16) for w in out_widths]
    out_shape.append(jax.ShapeDtypeStruct((n_batches * tm, 2 * D_ATT), F32))
    out_specs = [row_spec(w) for w in out_widths]
    out_specs.append(pl.BlockSpec((tm, 2 * D_ATT), lambda i: (i // tiles_per_batch, 0)))
    return pl.pallas_call(
        functools.partial(_inproj_kernel, tiles_per_batch=tiles_per_batch),
        grid=(n_tiles,),
        in_specs=[row_spec(D_MODEL), mod_spec, mod_spec,
                  pl.BlockSpec((1, D_MODEL), lambda i: (0, 0)),
                  pos_spec, pos_spec,
                  pl.BlockSpec((D_MODEL, D_IN), lambda i: (0, 0))],
        out_specs=out_specs,
        out_shape=out_shape,
        compiler_params=_cparams(("arbitrary",), 56),
        name="inproj",
    )(x2d, sc, sh, g, cos_t, sin_t, w_in_bf)


def _softmax_pv(s, v_parts):
    m = functools.reduce(jnp.maximum, [jnp.max(t, axis=-1, keepdims=True) for t in s])
    ps = [jnp.exp(t - m) for t in s]
    l = functools.reduce(jnp.add, [jnp.sum(p, axis=-1, keepdims=True) for p in ps])
    o = functools.reduce(jnp.add, [jnp.dot(p.astype(BF16), v, preferred_element_type=F32)
                                   for p, v in zip(ps, v_parts)])
    return o / l


def _attn_prompt_kernel(q_ref, k0_ref, k1_ref, k2_ref, v0_ref, v1_ref, v2_ref, bias_ref, o_ref):
    j = pl.program_id(1)
    q = q_ref[...]
    k = jnp.concatenate([k0_ref[...], k1_ref[...], k2_ref[...]], axis=0)
    v = jnp.concatenate([v0_ref[...], v1_ref[...], v2_ref[...]], axis=0)
    n_keys = k.shape[0]
    key_block = lax.broadcasted_iota(I32, (1, n_keys), 1) // ATT_QB
    before_start = jnp.where(key_block < 2 - j, NEG_BIG, 0.0)
    outs = []
    for h in range(N_HEADS):
        sl = slice(h * HEAD_DIM_A, (h + 1) * HEAD_DIM_A)
        qh = (q[:, sl].astype(F32) * (HEAD_DIM_A ** -0.5)).astype(BF16)
        s = lax.dot_general(qh, k[:, sl], (((1,), (1,)), ((), ())), preferred_element_type=F32)
        s = s + bias_ref[h] + before_start
        outs.append(_softmax_pv([s], [v[:, sl]]))
    o_ref[...] = jnp.concatenate(outs, axis=1).astype(BF16)


def _attn_prompt(q, k, v, bias_full, batch, seq):
    qb = ATT_QB
    nq = seq // qb

    def q_map(b, j):
        return (b * nq + j, 0)

    def kv_map(back):
        return lambda b, j: (b * nq + jnp.maximum(j - back, 0), 0)

    blk = lambda m: pl.BlockSpec((qb, D_ATT), m)
    return pl.pallas_call(
        _attn_prompt_kernel,
        grid=(batch, nq),
        in_specs=[blk(q_map), blk(kv_map(2)), blk(kv_map(1)), blk(kv_map(0)),
                  blk(kv_map(2)), blk(kv_map(1)), blk(kv_map(0)),
                  pl.BlockSpec(bias_full.shape, lambda b, j: (0, 0, 0))],
        out_specs=blk(q_map),
        out_shape=jax.ShapeDtypeStruct((batch * seq, D_ATT), BF16),
        compiler_params=_cparams(("parallel", "arbitrary"), 40),
        name="attn_prompt",
    )(q, k, k, k, v, v, v, bias_full)


def _attn_sample_kernel(q_ref, kn_ref, vn_ref, ck_ref, cv_ref, bc_ref, bn_ref, o_ref, *, cache_len):
    q = q_ref[...]
    kn = kn_ref[...]
    vn = vn_ref[...]
    outs = []
    for h in range(N_HEADS):
        sl = slice(h * HEAD_DIM_A, (h + 1) * HEAD_DIM_A)
        qh = (q[:, sl].astype(F32) * (HEAD_DIM_A ** -0.5)).astype(BF16)
        kc = ck_ref[pl.ds(h, cache_len, stride=N_HEADS), :].astype(BF16)
        vc = cv_ref[pl.ds(h, cache_len, stride=N_HEADS), :].astype(BF16)
        dn = (((1,), (1,)), ((), ()))
        s_c = lax.dot_general(qh, kc, dn, preferred_element_type=F32) + bc_ref[h]
        s_n = lax.dot_general(qh, kn[:, sl], dn, preferred_element_type=F32) + bn_ref[h]
        outs.append(_softmax_pv([s_c, s_n], [vc, vn[:, sl]]))
    o_ref[...] = jnp.concatenate(outs, axis=1).astype(BF16)


def _attn_sample(q, k, v, cache_k2d, cache_v2d, bias_cache, bias_new, batch, t_new, cache_len):
    blk = pl.BlockSpec((t_new, D_ATT), lambda b: (b, 0))
    cblk = pl.BlockSpec((cache_len * N_HEADS, HEAD_DIM_A), lambda b: (b, 0))
    return pl.pallas_call(
        functools.partial(_attn_sample_kernel, cache_len=cache_len),
        grid=(batch,),
        in_specs=[blk, blk, blk, cblk, cblk,
                  pl.BlockSpec(bias_cache.shape, lambda b: (0, 0, 0)),
                  pl.BlockSpec(bias_new.shape, lambda b: (0, 0, 0))],
        out_specs=blk,
        out_shape=jax.ShapeDtypeStruct((batch * t_new, D_ATT), BF16),
        compiler_params=_cparams(("arbitrary",), 32),
        name="attn_sample",
    )(q, k, v, cache_k2d, cache_v2d, bias_cache, bias_new)


def _ret_kernel(q_ref, k_ref, v_ref, g_ref, s0_ref, dmask_ref, qdec_ref, kdec_ref, sdec_ref,
                y_ref, sout_ref, state_ref):
    c = pl.program_id(1)

    @pl.when(c == 0)
    def _():
        state_ref[...] = s0_ref[0]

    q = q_ref[...]
    k = k_ref[...]
    v = v_ref[...]
    g = g_ref[...]
    outs = []
    for h in range(N_HEADS):
        ks = slice(h * KEY_DIM_R, (h + 1) * KEY_DIM_R)
        vs = slice(h * VAL_DIM_R, (h + 1) * VAL_DIM_R)
        qh, kh, vh = q[:, ks], k[:, ks], v[:, vs]
        scores = lax.dot_general(qh, kh, (((1,), (1,)), ((), ())), preferred_element_type=F32)
        inner = jnp.dot((scores * dmask_ref[h]).astype(BF16), vh, preferred_element_type=F32)
        state = state_ref[h]
        cross = jnp.dot(qh, state.astype(BF16), preferred_element_type=F32) * qdec_ref[h]
        o = inner + cross
        v_dec = (vh.astype(F32) * kdec_ref[h]).astype(BF16)
        state_ref[h] = sdec_ref[h] * state + lax.dot_general(
            kh, v_dec, (((0,), (0,)), ((), ())), preferred_element_type=F32)
        on = o * lax.rsqrt(jnp.mean(o * o, axis=-1, keepdims=True) + EPS)
        outs.append(on * _silu(g[:, vs].astype(F32)))
    y_ref[...] = jnp.concatenate(outs, axis=1).astype(BF16)

    @pl.when(c == pl.num_programs(1) - 1)
    def _():
        sout_ref[0] = state_ref[...]


def _ret_tables(chunk):
    log_g = jnp.log(1.0 - jnp.exp2(-5.0 - jnp.arange(N_HEADS, dtype=F32)))
    i = jnp.arange(chunk, dtype=F32)
    diff = i[:, None] - i[None, :]
    dmask = jnp.where(diff >= 0, jnp.exp(log_g[:, None, None] * jnp.maximum(diff, 0.0)), 0.0)
    qdec = jnp.exp(log_g[:, None] * (i + 1.0))
    kdec = jnp.exp(log_g[:, None] * (chunk - 1.0 - i))
    sdec = jnp.exp(log_g * chunk)
    bc = lambda t: jnp.broadcast_to(t[:, :, None], (N_HEADS, t.shape[1], VAL_DIM_R)).astype(F32)
    sdec_t = jnp.broadcast_to(sdec[:, None, None], (N_HEADS, 1, VAL_DIM_R)).astype(F32)
    return dmask.astype(F32), bc(qdec), bc(kdec), sdec_t


def _retention(q, k, v, gate, state0, batch, seq, chunk):
    nc = seq // chunk
    dmask, qdec, kdec, sdec = _ret_tables(chunk)
    row = lambda w: pl.BlockSpec((chunk, w), lambda b, c: (b * nc + c, 0))
    const = lambda a: pl.BlockSpec(a.shape, lambda b, c: (0,) * a.ndim)
    st_spec = pl.BlockSpec((1, N_HEADS, KEY_DIM_R, VAL_DIM_R), lambda b, c: (b, 0, 0, 0))
    return pl.pallas_call(
        _ret_kernel,
        grid=(batch, nc),
        in_specs=[row(D_RET_K), row(D_RET_K), row(D_RET_V), row(D_RET_V), st_spec,
                  const(dmask), const(qdec), const(kdec), const(sdec)],
        out_specs=[row(D_RET_V), st_spec],
        out_shape=[jax.ShapeDtypeStruct((batch * seq, D_RET_V), BF16),
                   jax.ShapeDtypeStruct((batch, N_HEADS, KEY_DIM_R, VAL_DIM_R), F32)],
        scratch_shapes=[pltpu.VMEM((N_HEADS, KEY_DIM_R, VAL_DIM_R), F32)],
        compiler_params=_cparams(("parallel", "arbitrary"), 32),
        name="retention",
    )(q, k, v, gate, state0, dmask, qdec, kdec, sdec)


def _outproj_kernel(x_ref, oa_ref, yr_ref, ga_ref, gb_ref, g1_ref, sc2_ref, sh2_ref, g2_ref, n2g_ref,
                    woa_ref, wor_ref, wout_ref, wrt_ref, wsg_ref, wsu_ref, wsd_ref,
                    h_ref, n2p_ref, s_ref):
    ya = jnp.dot(oa_ref[...], woa_ref[...], preferred_element_type=F32)
    yr = jnp.dot(yr_ref[...], wor_ref[...], preferred_element_type=F32)
    merged = (jax.nn.sigmoid(ga_ref[...].astype(F32)) * ya
              + jax.nn.sigmoid(gb_ref[...].astype(F32)) * yr)
    mix = jnp.dot(merged.astype(BF16), wout_ref[...], preferred_element_type=F32)
    h = x_ref[...] + g1_ref[0] * mix
    hn = h * lax.rsqrt(jnp.mean(h * h, axis=-1, keepdims=True) + EPS) * n2g_ref[...]
    n2 = hn * (1.0 + sc2_ref[0]) + sh2_ref[0]
    n2b = n2.astype(BF16)
    s_ref[...] = jax.nn.sigmoid(jnp.dot(n2b, wrt_ref[...], preferred_element_type=F32))
    hid = _silu(jnp.dot(n2b, wsg_ref[...], preferred_element_type=F32)) * jnp.dot(
        n2b, wsu_ref[...], preferred_element_type=F32)
    shared = jnp.dot(hid.astype(BF16), wsd_ref[...], preferred_element_type=F32)
    h_ref[...] = h + g2_ref[0] * shared
    half = D_MODEL // 2
    n2p_ref[...] = _pack_bf16_pair(n2[:, :half], n2[:, half:])


def _outproj(x2d, oa, yr_in, ga, gb, g1, sc2, sh2, g2, n2g, weights, tiles_per_batch):
    n = x2d.shape[0]
    tm = ROW_TILE
    mod_rows = g1.shape[1]
    row = lambda w: pl.BlockSpec((tm, w), lambda i: (i, 0))
    mod_spec = pl.BlockSpec((1, mod_rows, D_MODEL), lambda i: (i // tiles_per_batch, 0, 0))
    const = lambda a: pl.BlockSpec(a.shape, lambda i: (0,) * a.ndim)
    return pl.pallas_call(
        _outproj_kernel,
        grid=(n // tm,),
        in_specs=[row(D_MODEL), row(D_ATT), row(D_RET_V), row(D_MODEL), row(D_MODEL),
                  mod_spec, mod_spec, mod_spec, mod_spec, const(n2g)] + [const(w) for w in weights],
        out_specs=[row(D_MODEL), row(D_MODEL // 2), row(N_EXPERTS)],
        out_shape=[jax.ShapeDtypeStruct((n, D_MODEL), F32),
                   jax.ShapeDtypeStruct((n, D_MODEL // 2), U32),
                   jax.ShapeDtypeStruct((n, N_EXPERTS), F32)],
        compiler_params=_cparams(("arbitrary",), 48),
        name="outproj",
    )(x2d, oa, yr_in, ga, gb, g1, sc2, sh2, g2, n2g, *weights)


def _route_kernel(s_ref, b_ref, idx_ref, w_ref, rank_ref, cnt_ref, run_ref, tri_ref):
    step = pl.program_id(0)
    t = s_ref.shape[0]

    @pl.when(step == 0)
    def _():
        run_ref[...] = jnp.zeros_like(run_ref)
        r = lax.broadcasted_iota(I32, (t, t), 0)
        c = lax.broadcasted_iota(I32, (t, t), 1)
        tri_ref[...] = jnp.where(c < r, 1.0, 0.0).astype(BF16)

    s = s_ref[...]
    sel = s + b_ref[...]
    lane = lax.broadcasted_iota(I32, (t, N_EXPERTS), 1)
    lane_f = lane.astype(F32)
    grp = lane // GROUP_SIZE

    def first_argmax(vals):
        m = jnp.max(vals, axis=-1, keepdims=True)
        pos = jnp.min(jnp.where(vals == m, lane_f, float(N_EXPERTS)), axis=-1, keepdims=True)
        return m, pos

    gscore = []
    for g in range(N_GROUPS):
        vals = jnp.where(grp == g, sel, -jnp.inf)
        m1, p1 = first_argmax(vals)
        m2 = jnp.max(jnp.where(lane_f == p1, -jnp.inf, vals), axis=-1, keepdims=True)
        gscore.append(m1 + m2)
    allowed = jnp.zeros((t, N_EXPERTS), F32)
    for g in range(N_GROUPS):
        beaten_by = jnp.zeros((t, 1), F32)
        for o in range(N_GROUPS):
            if o == g:
                continue
            wins = (gscore[o] > gscore[g]) if o > g else (gscore[o] >= gscore[g])
            beaten_by = beaten_by + jnp.where(wins, 1.0, 0.0)
        kept = jnp.where(beaten_by < TOPK_GROUPS, 1.0, 0.0)
        allowed = jnp.where(grp == g, kept, allowed)
    cand = jnp.where(allowed > 0.5, sel, -jnp.inf)

    out_lane = lax.broadcasted_iota(I32, (t, LANES), 1)
    picked = jnp.zeros((t, N_EXPERTS), F32)
    idx_cols, w_cols = [], []
    for _ in range(TOP_K):
        _, pos = first_argmax(cand)
        hit = lane_f == pos
        w_cols.append(jnp.sum(jnp.where(hit, s, 0.0), axis=-1, keepdims=True))
        idx_cols.append(pos)
        picked = jnp.where(hit, 1.0, picked)
        cand = jnp.where(hit, -jnp.inf, cand)
    w_sum = functools.reduce(jnp.add, w_cols)

    before = jnp.dot(tri_ref[...], picked.astype(BF16), preferred_element_type=F32) + run_ref[...]
    run_ref[...] = run_ref[...] + jnp.sum(picked, axis=0, keepdims=True)

    idx_out = jnp.zeros((t, LANES), F32)
    w_out = jnp.zeros((t, LANES), F32)
    rank_out = jnp.zeros((t, LANES), F32)
    for kk in range(TOP_K):
        hit = lane_f == idx_cols[kk]
        rk = jnp.sum(jnp.where(hit, before, 0.0), axis=-1, keepdims=True)
        idx_out = jnp.where(out_lane == kk, idx_cols[kk], idx_out)
        w_out = jnp.where(out_lane == kk, w_cols[kk] / w_sum * ROUTED_SCALE, w_out)
        rank_out = jnp.where(out_lane == kk, rk, rank_out)
    idx_ref[...] = idx_out.astype(I32)
    w_ref[...] = w_out
    rank_ref[...] = rank_out.astype(I32)

    @pl.when(step == pl.num_programs(0) - 1)
    def _():
        cnt_ref[...] = run_ref[...].astype(I32)


def _route(scores, b_router):
    n = scores.shape[0]
    t = ROUTE_TILE
    row = pl.BlockSpec((t, LANES), lambda i: (i, 0))
    return pl.pallas_call(
        _route_kernel,
        grid=(n // t,),
        in_specs=[pl.BlockSpec((t, N_EXPERTS), lambda i: (i, 0)),
                  pl.BlockSpec((1, N_EXPERTS), lambda i: (0, 0))],
        out_specs=[row, row, row, pl.BlockSpec((1, N_EXPERTS), lambda i: (0, 0))],
        out_shape=[jax.ShapeDtypeStruct((n, LANES), I32),
                   jax.ShapeDtypeStruct((n, LANES), F32),
                   jax.ShapeDtypeStruct((n, LANES), I32),
                   jax.ShapeDtypeStruct((1, N_EXPERTS), I32)],
        scratch_shapes=[pltpu.VMEM((1, N_EXPERTS), F32), pltpu.VMEM((t, t), BF16)],
        compiler_params=_cparams(("arbitrary",), 32),
        name="route",
    )(scores, b_router)


def _dest_kernel(idx_ref, rank_ref, start_ref, dest_ref):
    t = idx_ref.shape[0]
    idx = idx_ref[...]
    lane = lax.broadcasted_iota(I32, (t, N_EXPERTS), 1)
    out_lane = lax.broadcasted_iota(I32, (t, LANES), 1)
    starts = start_ref[...].astype(F32)
    base = jnp.zeros((t, LANES), F32)
    for kk in range(TOP_K):
        hit = lane == idx[:, kk:kk + 1]
        st = jnp.sum(jnp.where(hit, starts, 0.0), axis=-1, keepdims=True)
        base = jnp.where(out_lane == kk, st, base)
    dest_ref[...] = base.astype(I32) + rank_ref[...]


def _dest(idx, rank, starts):
    n = idx.shape[0]
    t = ROUTE_TILE
    row = pl.BlockSpec((t, LANES), lambda i: (i, 0))
    return pl.pallas_call(
        _dest_kernel,
        grid=(n // t,),
        in_specs=[row, row, pl.BlockSpec((1, N_EXPERTS), lambda i: (0, 0))],
        out_specs=row,
        out_shape=jax.ShapeDtypeStruct((n, LANES), I32),
        compiler_params=_cparams(("arbitrary",), 32),
        name="dest",
    )(idx, rank, starts)


def _row_copy(src, src_row, dst, dst_row, sem):
    return pltpu.make_async_copy(src.at[pl.ds(src_row, 1)], dst.at[pl.ds(dst_row, 1)], sem)


def _dispatch_kernel(dest_ref, x_ref, xs_ref, sem):
    t = x_ref.shape[0]

    def issue(i, carry):
        for kk in range(TOP_K):
            _row_copy(x_ref, i, xs_ref, dest_ref[i * TOP_K + kk], sem).start()
        return carry

    lax.fori_loop(0, t, issue, 0)

    def drain(i, carry):
        for kk in range(TOP_K):
            _row_copy(x_ref, 0, xs_ref, 0, sem).wait()
        return carry

    lax.fori_loop(0, t, drain, 0)


def _dispatch(dest_flat, n2p):
    n, width = n2p.shape
    t = MOVE_TILE
    return pl.pallas_call(
        _dispatch_kernel,
        grid=(n // t,),
        in_specs=[pl.BlockSpec((t * TOP_K,), lambda i: (i,), memory_space=pltpu.SMEM),
                  pl.BlockSpec((t, width), lambda i: (i, 0))],
        out_specs=pl.BlockSpec(memory_space=pl.ANY),
        out_shape=jax.ShapeDtypeStruct((n * TOP_K, width), U32),
        scratch_shapes=[pltpu.SemaphoreType.DMA(())],
        compiler_params=_cparams(("arbitrary",), 32),
        name="dispatch",
    )(dest_flat, n2p)


def _ffn_kernel(tile_ref, exp_ref, lo_ref, hi_ref, nitems_ref, xs_ref, wg_ref, wu_ref, wd_ref,
                ys_ref, acc_ref):
    g = pl.program_id(0)
    m = xs_ref.shape[0]

    @pl.when(g < nitems_ref[0])
    def _():
        tile = tile_ref[g]
        first = jnp.logical_or(g == 0, tile_ref[jnp.maximum(g - 1, 0)] != tile)
        rows = tile * m + lax.broadcasted_iota(I32, (m, 1), 0)
        mine = (rows >= lo_ref[g]) & (rows < hi_ref[g])
        lo, hi = _unpack_bf16_pair(xs_ref[...])
        x = jnp.concatenate([lo, hi], axis=1).astype(BF16)
        hg = jnp.dot(x, wg_ref[0].astype(BF16), preferred_element_type=F32)
        hu = jnp.dot(x, wu_ref[0].astype(BF16), preferred_element_type=F32)
        y = jnp.dot((_silu(hg) * hu).astype(BF16), wd_ref[0].astype(BF16), preferred_element_type=F32)
        y = jnp.where(mine, y, 0.0)

        @pl.when(first)
        def _():
            acc_ref[...] = y

        @pl.when(jnp.logical_not(first))
        def _():
            acc_ref[...] = acc_ref[...] + y

        acc = acc_ref[...]
        half = D_MODEL // 2
        ys_ref[...] = _pack_bf16_pair(acc[:, :half], acc[:, half:])


def _ffn(item_tile, item_expert, item_lo, item_hi, n_items, xs, w_gate, w_up, w_down):
    rows, width = xs.shape
    m = FFN_TILE
    max_items = item_tile.shape[0]
    grid_spec = pltpu.PrefetchScalarGridSpec(
        num_scalar_prefetch=5,
        grid=(max_items,),
        in_specs=[pl.BlockSpec((m, width), lambda g, tl, ex, lo, hi, ni: (tl[g], 0)),
                  pl.BlockSpec((1, D_MODEL, D_EXPERT), lambda g, tl, ex, lo, hi, ni: (ex[g], 0, 0)),
                  pl.BlockSpec((1, D_MODEL, D_EXPERT), lambda g, tl, ex, lo, hi, ni: (ex[g], 0, 0)),
                  pl.BlockSpec((1, D_EXPERT, D_MODEL), lambda g, tl, ex, lo, hi, ni: (ex[g], 0, 0))],
        out_specs=pl.BlockSpec((m, width), lambda g, tl, ex, lo, hi, ni: (tl[g], 0)),
        scratch_shapes=[pltpu.VMEM((m, D_MODEL), F32)],
    )
    return pl.pallas_call(
        _ffn_kernel,
        grid_spec=grid_spec,
        out_shape=jax.ShapeDtypeStruct((rows, width), U32),
        compiler_params=_cparams(("arbitrary",), 32),
        name="ffn",
    )(item_tile, item_expert, item_lo, item_hi, n_items, xs, w_gate, w_up, w_down)


def _ffn_items(counts, n_rows):
    m = FFN_TILE
    n_tiles = n_rows // m
    max_items = n_tiles + N_EXPERTS - 1
    ends = jnp.cumsum(counts)
    starts = ends - counts
    first_tile = starts // m
    last_tile = jnp.maximum(ends - 1, 0) // m
    per_expert = jnp.where(counts > 0, last_tile - first_tile + 1, 0)
    item_end = jnp.cumsum(per_expert)
    item_start = item_end - per_expert
    n_items = item_end[-1]
    g = jnp.arange(max_items, dtype=I32)
    g_eff = jnp.minimum(g, n_items - 1)
    expert = jnp.minimum(jnp.searchsorted(item_end, g_eff, side='right'), N_EXPERTS - 1).astype(I32)
    tile = (first_tile[expert] + (g_eff - item_start[expert])).astype(I32)
    return (tile, expert, starts[expert].astype(I32), ends[expert].astype(I32),
            n_items.reshape(1).astype(I32), starts.astype(I32))


def _combine_kernel(dest_ref, w_ref, h_ref, g2_ref, nf_ref, ys_ref, y_ref, buf_ref, sem):
    t = h_ref.shape[0]

    def issue(i, carry):
        for kk in range(TOP_K):
            _row_copy(ys_ref, dest_ref[i * TOP_K + kk], buf_ref.at[kk], i, sem).start()
        return carry

    lax.fori_loop(0, t, issue, 0)

    def drain(i, carry):
        for kk in range(TOP_K):
            _row_copy(ys_ref, 0, buf_ref.at[kk], 0, sem).wait()
        return carry

    lax.fori_loop(0, t, drain, 0)

    w = w_ref[...]
    acc_lo = jnp.zeros((t, D_MODEL // 2), F32)
    acc_hi = jnp.zeros((t, D_MODEL // 2), F32)
    for kk in range(TOP_K):
        lo, hi = _unpack_bf16_pair(buf_ref[kk])
        wk = w[:, kk:kk + 1]
        acc_lo = acc_lo + wk * lo
        acc_hi = acc_hi + wk * hi
    out = h_ref[...] + g2_ref[0] * jnp.concatenate([acc_lo, acc_hi], axis=1)
    y_ref[...] = out * lax.rsqrt(jnp.mean(out * out, axis=-1, keepdims=True) + EPS) * nf_ref[...]


def _combine(dest_flat, w, h2, g2, normf, ys, row_offset, tiles_per_batch):
    n = h2.shape[0]
    t = MOVE_TILE
    off = row_offset // t
    mod_rows = g2.shape[1]
    mod_tiles = max(ROW_TILE // t, 1) * tiles_per_batch if mod_rows == 1 else n // t
    return pl.pallas_call(
        _combine_kernel,
        grid=(n // t,),
        in_specs=[pl.BlockSpec((t * TOP_K,), lambda i: (i + off,), memory_space=pltpu.SMEM),
                  pl.BlockSpec((t, LANES), lambda i: (i + off, 0)),
                  pl.BlockSpec((t, D_MODEL), lambda i: (i, 0)),
                  pl.BlockSpec((1, mod_rows if mod_rows == 1 else t, D_MODEL),
                               (lambda i: (i // mod_tiles, 0, 0)) if mod_rows == 1
                               else (lambda i: (0, i, 0))),
                  pl.BlockSpec((1, D_MODEL), lambda i: (0, 0)),
                  pl.BlockSpec(memory_space=pl.ANY)],
        out_specs=pl.BlockSpec((t, D_MODEL), lambda i: (i, 0)),
        out_shape=jax.ShapeDtypeStruct((n, D_MODEL), F32),
        scratch_shapes=[pltpu.VMEM((TOP_K, t, D_MODEL // 2), U32), pltpu.SemaphoreType.DMA(())],
        compiler_params=_cparams(("arbitrary",), 32),
        name="combine",
    )(dest_flat, w, h2, g2, normf, ys)


def _rotary_tables(pos):
    half = KEY_DIM_R // 2
    inv_freq = ROPE_BASE ** (-jnp.arange(half, dtype=F32) / half)
    ang = pos[:, None] * inv_freq[None, :]
    cos = jnp.cos(ang)
    sin = jnp.sin(ang)
    cos_t = jnp.tile(jnp.concatenate([cos, cos], axis=1), (1, N_HEADS))
    sin_t = jnp.tile(jnp.concatenate([-sin, sin], axis=1), (1, N_HEADS))
    return cos_t.astype(F32), sin_t.astype(F32)


def _prompt_bias(rel_bias):
    r = np.arange(ATT_QB)[:, None]
    c = np.arange(ATT_QB + ATT_WINDOW)[None, :]
    idx = np.clip(ATT_WINDOW + r - c, -REL_CLIP, REL_CLIP) + REL_CLIP
    band = c - (r // CHUNK) * CHUNK
    valid = (band >= 0) & (band < ATT_WINDOW + CHUNK)
    return jnp.where(jnp.asarray(valid)[None], rel_bias[:, idx], NEG_BIG).astype(F32)


def _sample_bias(rel_bias, t_new, cache_len):
    q_pos = np.arange(t_new)[:, None] + cache_len
    k_pos = np.arange(cache_len + t_new)[None, :]
    idx = np.clip(q_pos - k_pos, -REL_CLIP, REL_CLIP) + REL_CLIP
    b = rel_bias[:, idx].astype(F32)
    return b[:, :, :cache_len], b[:, :, cache_len:]


def _mod_parts(mod, rows_each):
    parts = jnp.split(mod, 6, axis=-1)
    if rows_each == 1:
        return [p[:, None, :] for p in parts]
    return [jnp.repeat(p, rows_each, axis=0)[None] for p in parts]


def kernel(x_prompt, x_sample, cache_attn_k, cache_attn_v, state_ret, c_prompt, c_sample,
           norm1_g, norm2_g, w_ada, b_ada, w_in, rel_bias, w_o_attn, w_o_ret, w_out,
           w_router, b_router, w_exp_gate, w_exp_up, w_exp_down, w_sh_gate, w_sh_up, w_sh_down,
           normf_g):
    batch, seq, d = x_prompt.shape
    dec_batch, dec_seq, _ = x_sample.shape
    depth = w_in.shape[0]
    assert depth == 1 and d == D_MODEL
    assert seq % ROW_TILE == 0 and dec_batch * dec_seq == ROW_TILE and ROW_TILE == ATT_WINDOW
    cache_len = cache_attn_k.shape[2]
    n_p = batch * seq
    n_s = dec_batch * dec_seq
    tpb = seq // ROW_TILE
    l = 0

    bf = lambda a: a.astype(BF16)
    c_all = jnp.concatenate([c_prompt, c_sample], axis=0)
    pad = (-c_all.shape[0]) % 8
    c_all = jnp.pad(c_all, ((0, pad), (0, 0)))
    mod = _ada(c_all, bf(w_ada[l]), b_ada[l][None, :])
    mod_p = _mod_parts(mod[:batch], 1)
    mod_s = _mod_parts(mod[batch:batch + dec_batch], dec_seq)

    w_in_bf = bf(w_in[l])
    n1g = norm1_g[l][None, :]
    n2g = norm2_g[l][None, :]
    dense_w = [bf(w_o_attn[l]), bf(w_o_ret[l]), bf(w_out[l]), bf(w_router[l]),
               bf(w_sh_gate[l]), bf(w_sh_up[l]), bf(w_sh_down[l])]

    xp = x_prompt.reshape(n_p, d)
    xs_ = x_sample.reshape(n_s, d)
    cos_p, sin_p = _rotary_tables(jnp.arange(seq, dtype=F32))
    pos_s = PAST_LEN + jnp.arange(dec_seq, dtype=F32)
    cos_s, sin_s = _rotary_tables(jnp.tile(pos_s, dec_batch))

    (qa, ka, va, qr, kr, vr, gr, ga, gb, kv_p) = _inproj(
        xp, mod_p[1], mod_p[0], n1g, cos_p, sin_p, w_in_bf, tpb)
    oa = _attn_prompt(qa, ka, va, _prompt_bias(rel_bias[l]), batch, seq)
    zero_state = jnp.zeros((batch, N_HEADS, KEY_DIM_R, VAL_DIM_R), F32)
    yr_in, state_p = _retention(qr, kr, vr, gr, zero_state, batch, seq, RET_CHUNK)
    h_p, n2p_p, s_p = _outproj(xp, oa, yr_in, ga, gb, mod_p[2], mod_p[4], mod_p[3], mod_p[5], n2g,
                               dense_w, tpb)

    (qa_s, ka_s, va_s, qr_s, kr_s, vr_s, gr_s, ga_s, gb_s, kv_s) = _inproj(
        xs_, mod_s[1], mod_s[0], n1g, cos_s, sin_s, w_in_bf, 1)
    bias_c, bias_n = _sample_bias(rel_bias[l], dec_seq, cache_len)
    ck2d = cache_attn_k[l].reshape(dec_batch * cache_len * N_HEADS, HEAD_DIM_A)
    cv2d = cache_attn_v[l].reshape(dec_batch * cache_len * N_HEADS, HEAD_DIM_A)
    oa_s = _attn_sample(qa_s, ka_s, va_s, ck2d, cv2d, bias_c, bias_n, dec_batch, dec_seq, cache_len)
    yr_in_s, state_s = _retention(qr_s, kr_s, vr_s, gr_s, state_ret[l], dec_batch, dec_seq, dec_seq)
    h_s, n2p_s, s_s = _outproj(xs_, oa_s, yr_in_s, ga_s, gb_s, mod_s[2], mod_s[4], mod_s[3], mod_s[5],
                               n2g, dense_w, 1)

    n2p = jnp.concatenate([n2p_p, n2p_s], axis=0)
    scores = jnp.concatenate([s_p, s_s], axis=0)
    idx, w_route, rank, counts = _route(scores, b_router[l][None, :])
    n_rows = (n_p + n_s) * TOP_K
    item_tile, item_expert, item_lo, item_hi, n_items, starts = _ffn_items(counts[0], n_rows)
    dest = _dest(idx, rank, starts[None, :])
    dest_flat = dest[:, :TOP_K].reshape(-1)
    xs_sorted = _dispatch(dest_flat, n2p)
    ys_sorted = _ffn(item_tile, item_expert, item_lo, item_hi, n_items, xs_sorted,
                     w_exp_gate[l], w_exp_up[l], w_exp_down[l])
    nf = normf_g[None, :]
    y_p = _combine(dest_flat, w_route, h_p, mod_p[5], nf, ys_sorted, 0, tpb)
    y_s = _combine(dest_flat, w_route, h_s, mod_s[5], nf, ys_sorted, n_p, 1)

    keep = min(ATT_WINDOW, seq)
    kv_p = kv_p.reshape(batch, ROW_TILE, 2, N_HEADS, HEAD_DIM_A)[:, ROW_TILE - keep:]
    kv_s = kv_s.reshape(dec_batch, dec_seq, 2, N_HEADS, HEAD_DIM_A)
    return (y_p.reshape(batch, seq, d), y_s.reshape(dec_batch, dec_seq, d),
            kv_p[:, :, 0][None], kv_p[:, :, 1][None], state_p[None],
            kv_s[:, :, 0][None], kv_s[:, :, 1][None], state_s[None])
```

```python
import functools

import numpy as np
import jax
import jax.numpy as jnp
from jax import lax
from jax.experimental import pallas as pl
from jax.experimental.pallas import tpu as pltpu

F32 = jnp.float32
BF16 = jnp.bfloat16
I32 = jnp.int32
U32 = jnp.uint32

D_MODEL = 1024
PAST_LEN = 4096
CHUNK = 64
N_LEFT_CHUNKS = 8
ATT_WINDOW = N_LEFT_CHUNKS * CHUNK
N_HEADS = 8
HEAD_DIM_A = 64
D_ATT = N_HEADS * HEAD_DIM_A
REL_CLIP = 128
KEY_DIM_R = 64
VAL_DIM_R = 128
D_RET_K = N_HEADS * KEY_DIM_R
D_RET_V = N_HEADS * VAL_DIM_R
ROPE_BASE = 10000.0
N_EXPERTS = 256
TOP_K = 8
N_GROUPS = 8
GROUP_SIZE = N_EXPERTS // N_GROUPS
TOPK_GROUPS = 4
D_EXPERT = 256
ROUTED_SCALE = 2.5
EPS = 1e-6
IN_WIDTHS = (D_ATT, D_ATT, D_ATT, D_RET_K, D_RET_K, D_RET_V, D_RET_V, D_MODEL, D_MODEL)
IN_OFFS = tuple(int(v) for v in np.cumsum((0,) + IN_WIDTHS))
D_IN = IN_OFFS[-1]

NEG_BIG = -1e30
LANES = 128
V7X_VMEM_BYTES = 64 * 1024 * 1024

ROW_TILE = 512
ATT_QB = 256
RET_CHUNK = 256
ROUTE_TILE = 512
MOVE_TILE = 256
FFN_TILE = 256


def _cparams(semantics, vmem_mb):
    return pltpu.CompilerParams(dimension_semantics=semantics,
                                vmem_limit_bytes=min(vmem_mb * 1024 * 1024, V7X_VMEM_BYTES - (6 << 20)))


def _silu(x):
    return x * jax.nn.sigmoid(x)


def _pack_bf16_pair(lo, hi):
    lo_b = pltpu.bitcast(lo.astype(BF16).astype(F32), U32) >> 16
    hi_b = pltpu.bitcast(hi.astype(BF16).astype(F32), U32) & jnp.uint32(0xFFFF0000)
    return lo_b | hi_b


def _unpack_bf16_pair(u):
    lo = pltpu.bitcast(u << 16, F32)
    hi = pltpu.bitcast(u & jnp.uint32(0xFFFF0000), F32)
    return lo, hi


def _ada_kernel(c_ref, w_ref, b_ref, o_ref):
    sc = _silu(c_ref[...]).astype(BF16)
    o_ref[...] = jnp.dot(sc, w_ref[...], preferred_element_type=F32) + b_ref[...]


def _ada(c_all, w_ada_bf, b_ada):
    rows = c_all.shape[0]
    n_out = w_ada_bf.shape[1]
    blk = D_MODEL
    return pl.pallas_call(
        _ada_kernel,
        grid=(n_out // blk,),
        in_specs=[pl.BlockSpec((rows, D_MODEL), lambda j: (0, 0)),
                  pl.BlockSpec((D_MODEL, blk), lambda j: (0, j)),
                  pl.BlockSpec((1, blk), lambda j: (0, j))],
        out_specs=pl.BlockSpec((rows, blk), lambda j: (0, j)),
        out_shape=jax.ShapeDtypeStruct((rows, n_out), F32),
        compiler_params=_cparams(("arbitrary",), 24),
        name="ada",
    )(c_all, w_ada_bf, b_ada)


def _inproj_kernel(x_ref, sc_ref, sh_ref, g_ref, cos_ref, sin_ref, w_ref,
                   qa_ref, ka_ref, va_ref, qr_ref, kr_ref, vr_ref, gr_ref, ga_ref, gb_ref,
                   kv_ref, *, tiles_per_batch):
    x = x_ref[...]
    xn = x * lax.rsqrt(jnp.mean(x * x, axis=-1, keepdims=True) + EPS) * g_ref[...]
    nb = (xn * (1.0 + sc_ref[0]) + sh_ref[0]).astype(BF16)

    def proj(seg):
        return jnp.dot(nb, w_ref[:, IN_OFFS[seg]:IN_OFFS[seg + 1]], preferred_element_type=F32)

    qa_ref[...] = proj(0).astype(BF16)
    ka = proj(1)
    va = proj(2)
    ka_ref[...] = ka.astype(BF16)
    va_ref[...] = va.astype(BF16)

    @pl.when(pl.program_id(0) % tiles_per_batch == tiles_per_batch - 1)
    def _():
        kv_ref[:, :D_ATT] = ka
        kv_ref[:, D_ATT:] = va

    cos = cos_ref[...]
    sin = sin_ref[...]
    first_half = (lax.broadcasted_iota(I32, (1, D_RET_K), 1) % KEY_DIM_R) < (KEY_DIM_R // 2)

    def rotary(t):
        partner = jnp.where(first_half, pltpu.roll(t, D_RET_K - KEY_DIM_R // 2, 1),
                            pltpu.roll(t, KEY_DIM_R // 2, 1))
        return t * cos + partner * sin

    qr_ref[...] = rotary(proj(3)).astype(BF16)
    kr_ref[...] = (rotary(proj(4)) * (KEY_DIM_R ** -0.5)).astype(BF16)
    vr_ref[...] = proj(5).astype(BF16)
    gr_ref[...] = proj(6).astype(BF16)
    ga_ref[...] = proj(7).astype(BF16)
    gb_ref[...] = proj(8).astype(BF16)


def _inproj(x2d, sc, sh, g, cos_t, sin_t, w_in_bf, tiles_per_batch):
    n = x2d.shape[0]
    tm = ROW_TILE
    n_tiles = n // tm
    n_batches = n_tiles // tiles_per_batch
    mod_rows = sc.shape[1]
    pos_tiles = cos_t.shape[0] // tm

    def row_spec(width):
        return pl.BlockSpec((tm, width), lambda i: (i, 0))

    mod_spec = pl.BlockSpec((1, mod_rows, D_MODEL), lambda i: (i // tiles_per_batch, 0, 0))
    pos_spec = pl.BlockSpec((tm, D_RET_K), lambda i: (i % pos_tiles, 0))
    out_widths = (D_ATT, D_ATT, D_ATT, D_RET_K, D_RET_K, D_RET_V, D_RET_V, D_MODEL, D_MODEL)
    out_shape = [jax.ShapeDtypeStruct((n, w), BF16) for w in out_widths]
    out_shape.append(jax.ShapeDtypeStruct((n_batches * tm, 2 * D_ATT), F32))
    out_specs = [row_spec(w) for w in out_widths]
    out_specs.append(pl.BlockSpec((tm, 2 * D_ATT), lambda i: (i // tiles_per_batch, 0)))
    return pl.pallas_call(
        functools.partial(_inproj_kernel, tiles_per_batch=tiles_per_batch),
        grid=(n_tiles,),
        in_specs=[row_spec(D_MODEL), mod_spec, mod_spec,
                  pl.BlockSpec((1, D_MODEL), lambda i: (0, 0)),
                  pos_spec, pos_spec,
                  pl.BlockSpec((D_MODEL, D_IN), lambda i: (0, 0))],
        out_specs=out_specs,
        out_shape=out_shape,
        compiler_params=_cparams(("arbitrary",), 56),
        name="inproj",
    )(x2d, sc, sh, g, cos_t, sin_t, w_in_bf)


def _softmax_pv(s, v_parts):
    m = functools.reduce(jnp.maximum, [jnp.max(t, axis=-1, keepdims=True) for t in s])
    ps = [jnp.exp(t - m) for t in s]
    l = functools.reduce(jnp.add, [jnp.sum(p, axis=-1, keepdims=True) for p in ps])
    o = functools.reduce(jnp.add, [jnp.dot(p.astype(BF16), v, preferred_element_type=F32)
                                   for p, v in zip(ps, v_parts)])
    return o / l


def _attn_prompt_kernel(q_ref, k0_ref, k1_ref, k2_ref, v0_ref, v1_ref, v2_ref, bias_ref, o_ref):
    j = pl.program_id(1)
    q = q_ref[...]
    k = jnp.concatenate([k0_ref[...], k1_ref[...], k2_ref[...]], axis=0)
    v = jnp.concatenate([v0_ref[...], v1_ref[...], v2_ref[...]], axis=0)
    n_keys = k.shape[0]
    key_block = lax.broadcasted_iota(I32, (1, n_keys), 1) // ATT_QB
    before_start = jnp.where(key_block < 2 - j, NEG_BIG, 0.0)
    outs = []
    for h in range(N_HEADS):
        sl = slice(h * HEAD_DIM_A, (h + 1) * HEAD_DIM_A)
        qh = (q[:, sl].astype(F32) * (HEAD_DIM_A ** -0.5)).astype(BF16)
        s = lax.dot_general(qh, k[:, sl], (((1,), (1,)), ((), ())), preferred_element_type=F32)
        s = s + bias_ref[h] + before_start
        outs.append(_softmax_pv([s], [v[:, sl]]))
    o_ref[...] = jnp.concatenate(outs, axis=1).astype(BF16)


def _attn_prompt(q, k, v, bias_full, batch, seq):
    qb = ATT_QB
    nq = seq // qb

    def q_map(b, j):
        return (b * nq + j, 0)

    def kv_map(back):
        return lambda b, j: (b * nq + jnp.maximum(j - back, 0), 0)

    blk = lambda m: pl.BlockSpec((qb, D_ATT), m)
    return pl.pallas_call(
        _attn_prompt_kernel,
        grid=(batch, nq),
        in_specs=[blk(q_map), blk(kv_map(2)), blk(kv_map(1)), blk(kv_map(0)),
                  blk(kv_map(2)), blk(kv_map(1)), blk(kv_map(0)),
                  pl.BlockSpec(bias_full.shape, lambda b, j: (0, 0, 0))],
        out_specs=blk(q_map),
        out_shape=jax.ShapeDtypeStruct((batch * seq, D_ATT), BF16),
        compiler_params=_cparams(("parallel", "arbitrary"), 40),
        name="attn_prompt",
    )(q, k, k, k, v, v, v, bias_full)


def _attn_sample_kernel(q_ref, kn_ref, vn_ref, ck_ref, cv_ref, bc_ref, bn_ref, o_ref, *, cache_len):
    q = q_ref[...]
    kn = kn_ref[...]
    vn = vn_ref[...]
    outs = []
    for h in range(N_HEADS):
        sl = slice(h * HEAD_DIM_A, (h + 1) * HEAD_DIM_A)
        qh = (q[:, sl].astype(F32) * (HEAD_DIM_A ** -0.5)).astype(BF16)
        kc = ck_ref[pl.ds(h, cache_len, stride=N_HEADS), :].astype(BF16)
        vc = cv_ref[pl.ds(h, cache_len, stride=N_HEADS), :].astype(BF16)
        dn = (((1,), (1,)), ((), ()))
        s_c = lax.dot_general(qh, kc, dn, preferred_element_type=F32) + bc_ref[h]
        s_n = lax.dot_general(qh, kn[:, sl], dn, preferred_element_type=F32) + bn_ref[h]
        outs.append(_softmax_pv([s_c, s_n], [vc, vn[:, sl]]))
    o_ref[...] = jnp.concatenate(outs, axis=1).astype(BF16)


def _attn_sample(q, k, v, cache_k2d, cache_v2d, bias_cache, bias_new, batch, t_new, cache_len):
    blk = pl.BlockSpec((t_new, D_ATT), lambda b: (b, 0))
    cblk = pl.BlockSpec((cache_len * N_HEADS, HEAD_DIM_A), lambda b: (b, 0))
    return pl.pallas_call(
        functools.partial(_attn_sample_kernel, cache_len=cache_len),
        grid=(batch,),
        in_specs=[blk, blk, blk, cblk, cblk,
                  pl.BlockSpec(bias_cache.shape, lambda b: (0, 0, 0)),
                  pl.BlockSpec(bias_new.shape, lambda b: (0, 0, 0))],
        out_specs=blk,
        out_shape=jax.ShapeDtypeStruct((batch * t_new, D_ATT), BF16),
        compiler_params=_cparams(("arbitrary",), 32),
        name="attn_sample",
    )(q, k, v, cache_k2d, cache_v2d, bias_cache, bias_new)


def _ret_kernel(q_ref, k_ref, v_ref, g_ref, s0_ref, dmask_ref, qdec_ref, kdec_ref, sdec_ref,
                y_ref, sout_ref, state_ref):
    c = pl.program_id(1)

    @pl.when(c == 0)
    def _():
        state_ref[...] = s0_ref[0]

    q = q_ref[...]
    k = k_ref[...]
    v = v_ref[...]
    g = g_ref[...]
    outs = []
    for h in range(N_HEADS):
        ks = slice(h * KEY_DIM_R, (h + 1) * KEY_DIM_R)
        vs = slice(h * VAL_DIM_R, (h + 1) * VAL_DIM_R)
        qh, kh, vh = q[:, ks], k[:, ks], v[:, vs]
        scores = lax.dot_general(qh, kh, (((1,), (1,)), ((), ())), preferred_element_type=F32)
        inner = jnp.dot((scores * dmask_ref[h]).astype(BF16), vh, preferred_element_type=F32)
        state = state_ref[h]
        cross = jnp.dot(qh, state.astype(BF16), preferred_element_type=F32) * qdec_ref[h]
        o = inner + cross
        v_dec = (vh.astype(F32) * kdec_ref[h]).astype(BF16)
        state_ref[h] = sdec_ref[h] * state + lax.dot_general(
            kh, v_dec, (((0,), (0,)), ((), ())), preferred_element_type=F32)
        on = o * lax.rsqrt(jnp.mean(o * o, axis=-1, keepdims=True) + EPS)
        outs.append(on * _silu(g[:, vs].astype(F32)))
    y_ref[...] = jnp.concatenate(outs, axis=1).astype(BF16)

    @pl.when(c == pl.num_programs(1) - 1)
    def _():
        sout_ref[0] = state_ref[...]


def _ret_tables(chunk):
    log_g = jnp.log(1.0 - jnp.exp2(-5.0 - jnp.arange(N_HEADS, dtype=F32)))
    i = jnp.arange(chunk, dtype=F32)
    diff = i[:, None] - i[None, :]
    dmask = jnp.where(diff >= 0, jnp.exp(log_g[:, None, None] * jnp.maximum(diff, 0.0)), 0.0)
    qdec = jnp.exp(log_g[:, None] * (i + 1.0))
    kdec = jnp.exp(log_g[:, None] * (chunk - 1.0 - i))
    sdec = jnp.exp(log_g * chunk)
    bc = lambda t: jnp.broadcast_to(t[:, :, None], (N_HEADS, t.shape[1], VAL_DIM_R)).astype(F32)
    sdec_t = jnp.broadcast_to(sdec[:, None, None], (N_HEADS, 1, VAL_DIM_R)).astype(F32)
    return dmask.astype(F32), bc(qdec), bc(kdec), sdec_t


def _retention(q, k, v, gate, state0, batch, seq, chunk):
    nc = seq // chunk
    dmask, qdec, kdec, sdec = _ret_tables(chunk)
    row = lambda w: pl.BlockSpec((chunk, w), lambda b, c: (b * nc + c, 0))
    const = lambda a: pl.BlockSpec(a.shape, lambda b, c: (0,) * a.ndim)
    st_spec = pl.BlockSpec((1, N_HEADS, KEY_DIM_R, VAL_DIM_R), lambda b, c: (b, 0, 0, 0))
    return pl.pallas_call(
        _ret_kernel,
        grid=(batch, nc),
        in_specs=[row(D_RET_K), row(D_RET_K), row(D_RET_V), row(D_RET_V), st_spec,
                  const(dmask), const(qdec), const(kdec), const(sdec)],
        out_specs=[row(D_RET_V), st_spec],
        out_shape=[jax.ShapeDtypeStruct((batch * seq, D_RET_V), BF16),
                   jax.ShapeDtypeStruct((batch, N_HEADS, KEY_DIM_R, VAL_DIM_R), F32)],
        scratch_shapes=[pltpu.VMEM((N_HEADS, KEY_DIM_R, VAL_DIM_R), F32)],
        compiler_params=_cparams(("parallel", "arbitrary"), 32),
        name="retention",
    )(q, k, v, gate, state0, dmask, qdec, kdec, sdec)


def _outproj_kernel(x_ref, oa_ref, yr_ref, ga_ref, gb_ref, g1_ref, sc2_ref, sh2_ref, g2_ref, n2g_ref,
                    woa_ref, wor_ref, wout_ref, wrt_ref, wsg_ref, wsu_ref, wsd_ref,
                    h_ref, n2p_ref, s_ref):
    ya = jnp.dot(oa_ref[...], woa_ref[...], preferred_element_type=F32)
    yr = jnp.dot(yr_ref[...], wor_ref[...], preferred_element_type=F32)
    merged = (jax.nn.sigmoid(ga_ref[...].astype(F32)) * ya
              + jax.nn.sigmoid(gb_ref[...].astype(F32)) * yr)
    mix = jnp.dot(merged.astype(BF16), wout_ref[...], preferred_element_type=F32)
    h = x_ref[...] + g1_ref[0] * mix
    hn = h * lax.rsqrt(jnp.mean(h * h, axis=-1, keepdims=True) + EPS) * n2g_ref[...]
    n2 = hn * (1.0 + sc2_ref[0]) + sh2_ref[0]
    n2b = n2.astype(BF16)
    s_ref[...] = jax.nn.sigmoid(jnp.dot(n2b, wrt_ref[...], preferred_element_type=F32))
    hid = _silu(jnp.dot(n2b, wsg_ref[...], preferred_element_type=F32)) * jnp.dot(
        n2b, wsu_ref[...], preferred_element_type=F32)
    shared = jnp.dot(hid.astype(BF16), wsd_ref[...], preferred_element_type=F32)
    h_ref[...] = h + g2_ref[0] * shared
    half = D_MODEL // 2
    n2p_ref[...] = _pack_bf16_pair(n2[:, :half], n2[:, half:])


def _outproj(x2d, oa, yr_in, ga, gb, g1, sc2, sh2, g2, n2g, weights, tiles_per_batch):
    n = x2d.shape[0]
    tm = ROW_TILE
    mod_rows = g1.shape[1]
    row = lambda w: pl.BlockSpec((tm, w), lambda i: (i, 0))
    mod_spec = pl.BlockSpec((1, mod_rows, D_MODEL), lambda i: (i // tiles_per_batch, 0, 0))
    const = lambda a: pl.BlockSpec(a.shape, lambda i: (0,) * a.ndim)
    return pl.pallas_call(
        _outproj_kernel,
        grid=(n // tm,),
        in_specs=[row(D_MODEL), row(D_ATT), row(D_RET_V), row(D_MODEL), row(D_MODEL),
                  mod_spec, mod_spec, mod_spec, mod_spec, const(n2g)] + [const(w) for w in weights],
        out_specs=[row(D_MODEL), row(D_MODEL // 2), row(N_EXPERTS)],
        out_shape=[jax.ShapeDtypeStruct((n, D_MODEL), F32),
                   jax.ShapeDtypeStruct((n, D_MODEL // 2), U32),
                   jax.ShapeDtypeStruct((n, N_EXPERTS), F32)],
        compiler_params=_cparams(("arbitrary",), 48),
        name="outproj",
    )(x2d, oa, yr_in, ga, gb, g1, sc2, sh2, g2, n2g, *weights)


def _route_kernel(s_ref, b_ref, idx_ref, w_ref, rank_ref, cnt_ref, run_ref, tri_ref):
    step = pl.program_id(0)
    t = s_ref.shape[0]

    @pl.when(step == 0)
    def _():
        run_ref[...] = jnp.zeros_like(run_ref)
        r = lax.broadcasted_iota(I32, (t, t), 0)
        c = lax.broadcasted_iota(I32, (t, t), 1)
        tri_ref[...] = jnp.where(c < r, 1.0, 0.0).astype(BF16)

    s = s_ref[...]
    sel = s + b_ref[...]
    lane = lax.broadcasted_iota(I32, (t, N_EXPERTS), 1)
    lane_f = lane.astype(F32)
    grp = lane // GROUP_SIZE

    def first_argmax(vals):
        m = jnp.max(vals, axis=-1, keepdims=True)
        pos = jnp.min(jnp.where(vals == m, lane_f, float(N_EXPERTS)), axis=-1, keepdims=True)
        return m, pos

    gscore = []
    for g in range(N_GROUPS):
        vals = jnp.where(grp == g, sel, -jnp.inf)
        m1, p1 = first_argmax(vals)
        m2 = jnp.max(jnp.where(lane_f == p1, -jnp.inf, vals), axis=-1, keepdims=True)
        gscore.append(m1 + m2)
    allowed = jnp.zeros((t, N_EXPERTS), F32)
    for g in range(N_GROUPS):
        beaten_by = jnp.zeros((t, 1), F32)
        for o in range(N_GROUPS):
            if o == g:
                continue
            wins = (gscore[o] > gscore[g]) if o > g else (gscore[o] >= gscore[g])
            beaten_by = beaten_by + jnp.where(wins, 1.0, 0.0)
        kept = jnp.where(beaten_by < TOPK_GROUPS, 1.0, 0.0)
        allowed = jnp.where(grp == g, kept, allowed)
    cand = jnp.where(allowed > 0.5, sel, -jnp.inf)

    out_lane = lax.broadcasted_iota(I32, (t, LANES), 1)
    picked = jnp.zeros((t, N_EXPERTS), F32)
    idx_cols, w_cols = [], []
    for _ in range(TOP_K):
        _, pos = first_argmax(cand)
        hit = lane_f == pos
        w_cols.append(jnp.sum(jnp.where(hit, s, 0.0), axis=-1, keepdims=True))
        idx_cols.append(pos)
        picked = jnp.where(hit, 1.0, picked)
        cand = jnp.where(hit, -jnp.inf, cand)
    w_sum = functools.reduce(jnp.add, w_cols)

    before = jnp.dot(tri_ref[...], picked.astype(BF16), preferred_element_type=F32) + run_ref[...]
    run_ref[...] = run_ref[...] + jnp.sum(picked, axis=0, keepdims=True)

    idx_out = jnp.zeros((t, LANES), F32)
    w_out = jnp.zeros((t, LANES), F32)
    rank_out = jnp.zeros((t, LANES), F32)
    for kk in range(TOP_K):
        hit = lane_f == idx_cols[kk]
        rk = jnp.sum(jnp.where(hit, before, 0.0), axis=-1, keepdims=True)
        idx_out = jnp.where(out_lane == kk, idx_cols[kk], idx_out)
        w_out = jnp.where(out_lane == kk, w_cols[kk] / w_sum * ROUTED_SCALE, w_out)
        rank_out = jnp.where(out_lane == kk, rk, rank_out)
    idx_ref[...] = idx_out.astype(I32)
    w_ref[...] = w_out
    rank_ref[...] = rank_out.astype(I32)

    @pl.when(step == pl.num_programs(0) - 1)
    def _():
        cnt_ref[...] = run_ref[...].astype(I32)


def _route(scores, b_router):
    n = scores.shape[0]
    t = ROUTE_TILE
    row = pl.BlockSpec((t, LANES), lambda i: (i, 0))
    return pl.pallas_call(
        _route_kernel,
        grid=(n // t,),
        in_specs=[pl.BlockSpec((t, N_EXPERTS), lambda i: (i, 0)),
                  pl.BlockSpec((1, N_EXPERTS), lambda i: (0, 0))],
        out_specs=[row, row, row, pl.BlockSpec((1, N_EXPERTS), lambda i: (0, 0))],
        out_shape=[jax.ShapeDtypeStruct((n, LANES), I32),
                   jax.ShapeDtypeStruct((n, LANES), F32),
                   jax.ShapeDtypeStruct((n, LANES), I32),
                   jax.ShapeDtypeStruct((1, N_EXPERTS), I32)],
        scratch_shapes=[pltpu.VMEM((1, N_EXPERTS), F32), pltpu.VMEM((t, t), BF16)],
        compiler_params=_cparams(("arbitrary",), 32),
        name="route",
    )(scores, b_router)


def _dest_kernel(idx_ref, rank_ref, start_ref, dest_ref):
    t = idx_ref.shape[0]
    idx = idx_ref[...]
    lane = lax.broadcasted_iota(I32, (t, N_EXPERTS), 1)
    out_lane = lax.broadcasted_iota(I32, (t, LANES), 1)
    starts = start_ref[...].astype(F32)
    base = jnp.zeros((t, LANES), F32)
    for kk in range(TOP_K):
        hit = lane == idx[:, kk:kk + 1]
        st = jnp.sum(jnp.where(hit, starts, 0.0), axis=-1, keepdims=True)
        base = jnp.where(out_lane == kk, st, base)
    dest_ref[...] = base.astype(I32) + rank_ref[...]


def _dest(idx, rank, starts):
    n = idx.shape[0]
    t = ROUTE_TILE
    row = pl.BlockSpec((t, LANES), lambda i: (i, 0))
    return pl.pallas_call(
        _dest_kernel,
        grid=(n // t,),
        in_specs=[row, row, pl.BlockSpec((1, N_EXPERTS), lambda i: (0, 0))],
        out_specs=row,
        out_shape=jax.ShapeDtypeStruct((n, LANES), I32),
        compiler_params=_cparams(("arbitrary",), 32),
        name="dest",
    )(idx, rank, starts)


def _row_copy(src, src_row, dst, dst_row, sem):
    return pltpu.make_async_copy(src.at[pl.ds(src_row, 1)], dst.at[pl.ds(dst_row, 1)], sem)


def _dispatch_kernel(dest_ref, pend_ref, x_ref, xs_ref, zero_ref, sem):
    t = x_ref.shape[0]

    @pl.when(pl.program_id(0) == 0)
    def _():
        zero_ref[...] = jnp.zeros_like(zero_ref)

        def fill(e, carry):
            row = pl.multiple_of(jnp.maximum(pend_ref[e] - FFN_TILE, 0), FFN_TILE)
            pltpu.make_async_copy(zero_ref, xs_ref.at[pl.ds(row, FFN_TILE)], sem).start()
            return carry

        lax.fori_loop(0, N_EXPERTS, fill, 0)

        def fill_wait(e, carry):
            pltpu.make_async_copy(zero_ref, xs_ref.at[pl.ds(0, FFN_TILE)], sem).wait()
            return carry

        lax.fori_loop(0, N_EXPERTS, fill_wait, 0)

    def issue(i, carry):
        for kk in range(TOP_K):
            _row_copy(x_ref, i, xs_ref, dest_ref[i * TOP_K + kk], sem).start(priority=kk % 2)
        return carry

    lax.fori_loop(0, t, issue, 0)

    def drain(i, carry):
        for kk in range(TOP_K):
            _row_copy(x_ref, 0, xs_ref, 0, sem).wait()
        return carry

    lax.fori_loop(0, t, drain, 0)


def _dispatch(dest_flat, pend, n2p, n_sorted_rows):
    n, width = n2p.shape
    t = MOVE_TILE
    return pl.pallas_call(
        _dispatch_kernel,
        grid=(n // t,),
        in_specs=[pl.BlockSpec((t * TOP_K,), lambda i: (i,), memory_space=pltpu.SMEM),
                  pl.BlockSpec(memory_space=pltpu.SMEM),
                  pl.BlockSpec((t, width), lambda i: (i, 0))],
        out_specs=pl.BlockSpec(memory_space=pl.ANY),
        out_shape=jax.ShapeDtypeStruct((n_sorted_rows, width), U32),
        scratch_shapes=[pltpu.VMEM((FFN_TILE, width), U32), pltpu.SemaphoreType.DMA(())],
        compiler_params=_cparams(("arbitrary",), 32),
        name="dispatch",
    )(dest_flat, pend, n2p)


def _ffn_kernel(texp_ref, ntiles_ref, xs_ref, wg_ref, wu_ref, wd_ref, ys_ref, wgu_bf, wd_bf):
    g = pl.program_id(0)

    @pl.when(g < ntiles_ref[0])
    def _():
        changed = jnp.logical_or(g == 0, texp_ref[jnp.maximum(g - 1, 0)] != texp_ref[g])

        @pl.when(changed)
        def _():
            wgu_bf[:, :D_EXPERT] = wg_ref[0].astype(BF16)
            wgu_bf[:, D_EXPERT:] = wu_ref[0].astype(BF16)
            wd_bf[...] = wd_ref[0].astype(BF16)

        lo, hi = _unpack_bf16_pair(xs_ref[...])
        x = jnp.concatenate([lo, hi], axis=1).astype(BF16)
        gu = jnp.dot(x, wgu_bf[...], preferred_element_type=F32)
        hid = (_silu(gu[:, :D_EXPERT]) * gu[:, D_EXPERT:]).astype(BF16)
        y = jnp.dot(hid, wd_bf[...], preferred_element_type=F32)
        half = D_MODEL // 2
        ys_ref[...] = _pack_bf16_pair(y[:, :half], y[:, half:])


def _ffn(tile_expert, n_tiles, xs, w_gate, w_up, w_down):
    rows, width = xs.shape
    m = FFN_TILE
    max_tiles = tile_expert.shape[0]
    row_map = lambda g, te, nt: (jnp.minimum(g, nt[0] - 1), 0)
    w_map = lambda g, te, nt: (te[g], 0, 0)
    grid_spec = pltpu.PrefetchScalarGridSpec(
        num_scalar_prefetch=2,
        grid=(max_tiles,),
        in_specs=[pl.BlockSpec((m, width), row_map),
                  pl.BlockSpec((1, D_MODEL, D_EXPERT), w_map),
                  pl.BlockSpec((1, D_MODEL, D_EXPERT), w_map),
                  pl.BlockSpec((1, D_EXPERT, D_MODEL), w_map)],
        out_specs=pl.BlockSpec((m, width), row_map),
        scratch_shapes=[pltpu.VMEM((D_MODEL, 2 * D_EXPERT), BF16), pltpu.VMEM((D_EXPERT, D_MODEL), BF16)],
    )
    return pl.pallas_call(
        _ffn_kernel,
        grid_spec=grid_spec,
        out_shape=jax.ShapeDtypeStruct((rows, width), U32),
        compiler_params=_cparams(("arbitrary",), 32),
        name="ffn",
    )(tile_expert, n_tiles, xs, w_gate, w_up, w_down)


def _ffn_plan(counts, n_assign):
    m = FFN_TILE
    max_tiles = n_assign // m + N_EXPERTS
    padded = ((counts + m - 1) // m) * m
    pend = jnp.cumsum(padded).astype(I32)
    pstart = pend - padded
    n_tiles = pend[-1:] // m
    g = jnp.minimum(jnp.arange(max_tiles, dtype=I32), n_tiles - 1)
    tile_expert = jnp.sum((pend[None, :] <= (g * m)[:, None]).astype(I32), axis=1)
    return jnp.minimum(tile_expert, N_EXPERTS - 1), n_tiles, pstart, pend, max_tiles * m


def _combine_kernel(dest_ref, w_ref, h_ref, g2_ref, nf_ref, ys_ref, y_ref, buf_ref, sem):
    t = h_ref.shape[0]

    def issue(i, carry):
        for kk in range(TOP_K):
            _row_copy(ys_ref, dest_ref[i * TOP_K + kk], buf_ref.at[kk], i, sem).start(priority=kk % 2)
        return carry

    lax.fori_loop(0, t, issue, 0)

    def drain(i, carry):
        for kk in range(TOP_K):
            _row_copy(ys_ref, 0, buf_ref.at[kk], 0, sem).wait()
        return carry

    lax.fori_loop(0, t, drain, 0)

    w = w_ref[...]
    acc_lo = jnp.zeros((t, D_MODEL // 2), F32)
    acc_hi = jnp.zeros((t, D_MODEL // 2), F32)
    for kk in range(TOP_K):
        lo, hi = _unpack_bf16_pair(buf_ref[kk])
        wk = w[:, kk:kk + 1]
        acc_lo = acc_lo + wk * lo
        acc_hi = acc_hi + wk * hi
    out = h_ref[...] + g2_ref[0] * jnp.concatenate([acc_lo, acc_hi], axis=1)
    y_ref[...] = out * lax.rsqrt(jnp.mean(out * out, axis=-1, keepdims=True) + EPS) * nf_ref[...]


def _combine(dest_flat, w, h2, g2, normf, ys, row_offset, tiles_per_batch):
    n = h2.shape[0]
    t = MOVE_TILE
    off = row_offset // t
    mod_rows = g2.shape[1]
    mod_tiles = max(ROW_TILE // t, 1) * tiles_per_batch if mod_rows == 1 else n // t
    return pl.pallas_call(
        _combine_kernel,
        grid=(n // t,),
        in_specs=[pl.BlockSpec((t * TOP_K,), lambda i: (i + off,), memory_space=pltpu.SMEM),
                  pl.BlockSpec((t, LANES), lambda i: (i + off, 0)),
                  pl.BlockSpec((t, D_MODEL), lambda i: (i, 0)),
                  pl.BlockSpec((1, mod_rows if mod_rows == 1 else t, D_MODEL),
                               (lambda i: (i // mod_tiles, 0, 0)) if mod_rows == 1
                               else (lambda i: (0, i, 0))),
                  pl.BlockSpec((1, D_MODEL), lambda i: (0, 0)),
                  pl.BlockSpec(memory_space=pl.ANY)],
        out_specs=pl.BlockSpec((t, D_MODEL), lambda i: (i, 0)),
        out_shape=jax.ShapeDtypeStruct((n, D_MODEL), F32),
        scratch_shapes=[pltpu.VMEM((TOP_K, t, D_MODEL // 2), U32), pltpu.SemaphoreType.DMA(())],
        compiler_params=_cparams(("arbitrary",), 32),
        name="combine",
    )(dest_flat, w, h2, g2, normf, ys)


def _rotary_tables(pos):
    half = KEY_DIM_R // 2
    inv_freq = ROPE_BASE ** (-jnp.arange(half, dtype=F32) / half)
    ang = pos[:, None] * inv_freq[None, :]
    cos = jnp.cos(ang)
    sin = jnp.sin(ang)
    cos_t = jnp.tile(jnp.concatenate([cos, cos], axis=1), (1, N_HEADS))
    sin_t = jnp.tile(jnp.concatenate([-sin, sin], axis=1), (1, N_HEADS))
    return cos_t.astype(F32), sin_t.astype(F32)


def _rel_bias_table(rel_bias, n_rows, n_cols, q_offset):
    heads = rel_bias.shape[0]
    n_diag = n_rows + n_cols - 1
    dist = q_offset + (n_rows - 1) - np.arange(n_diag)
    idx = np.clip(dist, -REL_CLIP, REL_CLIP) + REL_CLIP
    n_hi = int(np.sum(dist > REL_CLIP))
    n_lo = int(np.sum(dist < -REL_CLIP))
    mid = rel_bias[:, int(idx[n_diag - n_lo - 1]):int(idx[n_hi]) + 1][:, ::-1]
    diag = jnp.concatenate([jnp.broadcast_to(rel_bias[:, 2 * REL_CLIP:], (heads, n_hi)), mid,
                            jnp.broadcast_to(rel_bias[:, :1], (heads, n_lo))], axis=1)
    period = n_diag + 1
    v = jnp.roll(jnp.pad(diag, ((0, 0), (0, 1))), -(n_rows - 1), axis=1)
    skew = jnp.tile(v, (1, n_rows))[:, :n_rows * (period - 1)].reshape(heads, n_rows, period - 1)
    return skew[:, :, :n_cols].astype(F32)


def _prompt_bias(rel_bias):
    n_cols = ATT_QB + ATT_WINDOW
    r = np.arange(ATT_QB)[:, None]
    c = np.arange(n_cols)[None, :]
    band = c - (r // CHUNK) * CHUNK
    valid = (band >= 0) & (band < ATT_WINDOW + CHUNK)
    table = _rel_bias_table(rel_bias, ATT_QB, n_cols, ATT_WINDOW)
    return jnp.where(jnp.asarray(valid)[None], table, NEG_BIG)


def _sample_bias(rel_bias, t_new, cache_len):
    b = _rel_bias_table(rel_bias, t_new, cache_len + t_new, cache_len)
    return b[:, :, :cache_len], b[:, :, cache_len:]


def _mod_parts(mod, rows_each):
    parts = jnp.split(mod, 6, axis=-1)
    if rows_each == 1:
        return [p[:, None, :] for p in parts]
    return [jnp.repeat(p, rows_each, axis=0)[None] for p in parts]


def kernel(x_prompt, x_sample, cache_attn_k, cache_attn_v, state_ret, c_prompt, c_sample,
           norm1_g, norm2_g, w_ada, b_ada, w_in, rel_bias, w_o_attn, w_o_ret, w_out,
           w_router, b_router, w_exp_gate, w_exp_up, w_exp_down, w_sh_gate, w_sh_up, w_sh_down,
           normf_g):
    batch, seq, d = x_prompt.shape
    dec_batch, dec_seq, _ = x_sample.shape
    depth = w_in.shape[0]
    assert depth == 1 and d == D_MODEL
    assert seq % ROW_TILE == 0 and dec_batch * dec_seq == ROW_TILE and ROW_TILE == ATT_WINDOW
    cache_len = cache_attn_k.shape[2]
    n_p = batch * seq
    n_s = dec_batch * dec_seq
    tpb = seq // ROW_TILE
    l = 0

    bf = lambda a: a.astype(BF16)
    c_all = jnp.concatenate([c_prompt, c_sample], axis=0)
    pad = (-c_all.shape[0]) % 8
    c_all = jnp.pad(c_all, ((0, pad), (0, 0)))
    mod = _ada(c_all, bf(w_ada[l]), b_ada[l][None, :])
    mod_p = _mod_parts(mod[:batch], 1)
    mod_s = _mod_parts(mod[batch:batch + dec_batch], dec_seq)

    w_in_bf = bf(w_in[l])
    n1g = norm1_g[l][None, :]
    n2g = norm2_g[l][None, :]
    dense_w = [bf(w_o_attn[l]), bf(w_o_ret[l]), bf(w_out[l]), bf(w_router[l]),
               bf(w_sh_gate[l]), bf(w_sh_up[l]), bf(w_sh_down[l])]

    xp = x_prompt.reshape(n_p, d)
    xs_ = x_sample.reshape(n_s, d)
    cos_p, sin_p = _rotary_tables(jnp.arange(seq, dtype=F32))
    pos_s = PAST_LEN + jnp.arange(dec_seq, dtype=F32)
    cos_s, sin_s = _rotary_tables(jnp.tile(pos_s, dec_batch))

    (qa, ka, va, qr, kr, vr, gr, ga, gb, kv_p) = _inproj(
        xp, mod_p[1], mod_p[0], n1g, cos_p, sin_p, w_in_bf, tpb)
    oa = _attn_prompt(qa, ka, va, _prompt_bias(rel_bias[l]), batch, seq)
    zero_state = jnp.zeros((batch, N_HEADS, KEY_DIM_R, VAL_DIM_R), F32)
    yr_in, state_p = _retention(qr, kr, vr, gr, zero_state, batch, seq, RET_CHUNK)
    h_p, n2p_p, s_p = _outproj(xp, oa, yr_in, ga, gb, mod_p[2], mod_p[4], mod_p[3], mod_p[5], n2g,
                               dense_w, tpb)

    (qa_s, ka_s, va_s, qr_s, kr_s, vr_s, gr_s, ga_s, gb_s, kv_s) = _inproj(
        xs_, mod_s[1], mod_s[0], n1g, cos_s, sin_s, w_in_bf, 1)
    bias_c, bias_n = _sample_bias(rel_bias[l], dec_seq, cache_len)
    ck2d = cache_attn_k[l].reshape(dec_batch * cache_len * N_HEADS, HEAD_DIM_A)
    cv2d = cache_attn_v[l].reshape(dec_batch * cache_len * N_HEADS, HEAD_DIM_A)
    oa_s = _attn_sample(qa_s, ka_s, va_s, ck2d, cv2d, bias_c, bias_n, dec_batch, dec_seq, cache_len)
    yr_in_s, state_s = _retention(qr_s, kr_s, vr_s, gr_s, state_ret[l], dec_batch, dec_seq, dec_seq)
    h_s, n2p_s, s_s = _outproj(xs_, oa_s, yr_in_s, ga_s, gb_s, mod_s[2], mod_s[4], mod_s[3], mod_s[5],
                               n2g, dense_w, 1)

    n2p = jnp.concatenate([n2p_p, n2p_s], axis=0)
    scores = jnp.concatenate([s_p, s_s], axis=0)
    idx, w_route, rank, counts = _route(scores, b_router[l][None, :])
    tile_expert, n_tiles, pstart, pend, n_sorted_rows = _ffn_plan(counts[0], (n_p + n_s) * TOP_K)
    dest = _dest(idx, rank, pstart[None, :])
    dest_flat = dest[:, :TOP_K].reshape(-1)
    xs_sorted = _dispatch(dest_flat, pend, n2p, n_sorted_rows)
    ys_sorted = _ffn(tile_expert, n_tiles, xs_sorted, w_exp_gate[l], w_exp_up[l], w_exp_down[l])
    nf = normf_g[None, :]
    y_p = _combine(dest_flat, w_route, h_p, mod_p[5], nf, ys_sorted, 0, tpb)
    y_s = _combine(dest_flat, w_route, h_s, mod_s[5], nf, ys_sorted, n_p, 1)

    keep = min(ATT_WINDOW, seq)
    kv_p = kv_p.reshape(batch, ROW_TILE, 2, N_HEADS, HEAD_DIM_A)[:, ROW_TILE - keep:]
    kv_s = kv_s.reshape(dec_batch, dec_seq, 2, N_HEADS, HEAD_DIM_A)
    return (y_p.reshape(batch, seq, d), y_s.reshape(dec_batch, dec_seq, d),
            kv_p[:, :, 0][None], kv_p[:, :, 1][None], state_p[None],
            kv_s[:, :, 0][None], kv_s[:, :, 1][None], state_s[None])
```

```python
import functools

import numpy as np
import jax
import jax.numpy as jnp
from jax import lax
from jax.experimental import pallas as pl
from jax.experimental.pallas import tpu as pltpu

F32 = jnp.float32
BF16 = jnp.bfloat16
I32 = jnp.int32
U32 = jnp.uint32

D_MODEL = 1024
PAST_LEN = 4096
CHUNK = 64
N_LEFT_CHUNKS = 8
ATT_WINDOW = N_LEFT_CHUNKS * CHUNK
N_HEADS = 8
HEAD_DIM_A = 64
D_ATT = N_HEADS * HEAD_DIM_A
REL_CLIP = 128
KEY_DIM_R = 64
VAL_DIM_R = 128
D_RET_K = N_HEADS * KEY_DIM_R
D_RET_V = N_HEADS * VAL_DIM_R
ROPE_BASE = 10000.0
N_EXPERTS = 256
TOP_K = 8
N_GROUPS = 8
GROUP_SIZE = N_EXPERTS // N_GROUPS
TOPK_GROUPS = 4
D_EXPERT = 256
ROUTED_SCALE = 2.5
EPS = 1e-6
IN_WIDTHS = (D_ATT, D_ATT, D_ATT, D_RET_K, D_RET_K, D_RET_V, D_RET_V, D_MODEL, D_MODEL)
IN_OFFS = tuple(int(v) for v in np.cumsum((0,) + IN_WIDTHS))
D_IN = IN_OFFS[-1]

NEG_BIG = -1e30
LANES = 128
V7X_VMEM_BYTES = 64 * 1024 * 1024

ROW_TILE = 512
ATT_QB = 256
RET_CHUNK = 256
ROUTE_TILE = 512
MOVE_TILE = 256
FFN_TILE = 256


def _cparams(semantics, vmem_mb):
    return pltpu.CompilerParams(dimension_semantics=semantics,
                                vmem_limit_bytes=min(vmem_mb * 1024 * 1024, V7X_VMEM_BYTES - (6 << 20)))


def _silu(x):
    return x * jax.nn.sigmoid(x)


def _pack_bf16_pair(lo, hi):
    lo_b = pltpu.bitcast(lo.astype(BF16).astype(F32), U32) >> 16
    hi_b = pltpu.bitcast(hi.astype(BF16).astype(F32), U32) & jnp.uint32(0xFFFF0000)
    return lo_b | hi_b


def _unpack_bf16_pair(u):
    lo = pltpu.bitcast(u << 16, F32)
    hi = pltpu.bitcast(u & jnp.uint32(0xFFFF0000), F32)
    return lo, hi


def _ada_kernel(c_ref, w_ref, b_ref, o_ref):
    sc = _silu(c_ref[...]).astype(BF16)
    o_ref[...] = jnp.dot(sc, w_ref[...], preferred_element_type=F32) + b_ref[...]


def _ada(c_all, w_ada_bf, b_ada):
    rows = c_all.shape[0]
    n_out = w_ada_bf.shape[1]
    blk = D_MODEL
    return pl.pallas_call(
        _ada_kernel,
        grid=(n_out // blk,),
        in_specs=[pl.BlockSpec((rows, D_MODEL), lambda j: (0, 0)),
                  pl.BlockSpec((D_MODEL, blk), lambda j: (0, j)),
                  pl.BlockSpec((1, blk), lambda j: (0, j))],
        out_specs=pl.BlockSpec((rows, blk), lambda j: (0, j)),
        out_shape=jax.ShapeDtypeStruct((rows, n_out), F32),
        compiler_params=_cparams(("arbitrary",), 24),
        name="ada",
    )(c_all, w_ada_bf, b_ada)


def _inproj_kernel(x_ref, sc_ref, sh_ref, g_ref, cos_ref, sin_ref, w_ref,
                   qa_ref, ka_ref, va_ref, qr_ref, kr_ref, vr_ref, gr_ref, ga_ref, gb_ref,
                   kv_ref, *, tiles_per_batch):
    x = x_ref[...]
    xn = x * lax.rsqrt(jnp.mean(x * x, axis=-1, keepdims=True) + EPS) * g_ref[...]
    nb = (xn * (1.0 + sc_ref[0]) + sh_ref[0]).astype(BF16)

    def proj(seg):
        return jnp.dot(nb, w_ref[:, IN_OFFS[seg]:IN_OFFS[seg + 1]], preferred_element_type=F32)

    qa_ref[...] = proj(0).astype(BF16)
    ka = proj(1)
    va = proj(2)
    ka_ref[...] = ka.astype(BF16)
    va_ref[...] = va.astype(BF16)

    @pl.when(pl.program_id(0) % tiles_per_batch == tiles_per_batch - 1)
    def _():
        kv_ref[:, :D_ATT] = ka
        kv_ref[:, D_ATT:] = va

    cos = cos_ref[...]
    sin = sin_ref[...]
    first_half = (lax.broadcasted_iota(I32, (1, D_RET_K), 1) % KEY_DIM_R) < (KEY_DIM_R // 2)

    def rotary(t):
        partner = jnp.where(first_half, pltpu.roll(t, D_RET_K - KEY_DIM_R // 2, 1),
                            pltpu.roll(t, KEY_DIM_R // 2, 1))
        return t * cos + partner * sin

    qr_ref[...] = rotary(proj(3)).astype(BF16)
    kr_ref[...] = (rotary(proj(4)) * (KEY_DIM_R ** -0.5)).astype(BF16)
    vr_ref[...] = proj(5).astype(BF16)
    gr_ref[...] = proj(6).astype(BF16)
    ga_ref[...] = proj(7).astype(BF16)
    gb_ref[...] = proj(8).astype(BF16)


def _inproj(x2d, sc, sh, g, cos_t, sin_t, w_in_bf, tiles_per_batch):
    n = x2d.shape[0]
    tm = ROW_TILE
    n_tiles = n // tm
    n_batches = n_tiles // tiles_per_batch
    mod_rows = sc.shape[1]
    pos_tiles = cos_t.shape[0] // tm

    def row_spec(width):
        return pl.BlockSpec((tm, width), lambda i: (i, 0))

    mod_spec = pl.BlockSpec((1, mod_rows, D_MODEL), lambda i: (i // tiles_per_batch, 0, 0))
    pos_spec = pl.BlockSpec((tm, D_RET_K), lambda i: (i % pos_tiles, 0))
    out_widths = (D_ATT, D_ATT, D_ATT, D_RET_K, D_RET_K, D_RET_V, D_RET_V, D_MODEL, D_MODEL)
    out_shape = [jax.ShapeDtypeStruct((n, w), BF16) for w in out_widths]
    out_shape.append(jax.ShapeDtypeStruct((n_batches * tm, 2 * D_ATT), F32))
    out_specs = [row_spec(w) for w in out_widths]
    out_specs.append(pl.BlockSpec((tm, 2 * D_ATT), lambda i: (i // tiles_per_batch, 0)))
    return pl.pallas_call(
        functools.partial(_inproj_kernel, tiles_per_batch=tiles_per_batch),
        grid=(n_tiles,),
        in_specs=[row_spec(D_MODEL), mod_spec, mod_spec,
                  pl.BlockSpec((1, D_MODEL), lambda i: (0, 0)),
                  pos_spec, pos_spec,
                  pl.BlockSpec((D_MODEL, D_IN), lambda i: (0, 0))],
        out_specs=out_specs,
        out_shape=out_shape,
        compiler_params=_cparams(("arbitrary",), 56),
        name="inproj",
    )(x2d, sc, sh, g, cos_t, sin_t, w_in_bf)


def _softmax_pv(s, v_parts):
    m = functools.reduce(jnp.maximum, [jnp.max(t, axis=-1, keepdims=True) for t in s])
    ps = [jnp.exp(t - m) for t in s]
    l = functools.reduce(jnp.add, [jnp.sum(p, axis=-1, keepdims=True) for p in ps])
    o = functools.reduce(jnp.add, [jnp.dot(p.astype(BF16), v, preferred_element_type=F32)
                                   for p, v in zip(ps, v_parts)])
    return o / l


def _attn_prompt_kernel(q_ref, k0_ref, k1_ref, k2_ref, v0_ref, v1_ref, v2_ref, bias_ref, o_ref):
    j = pl.program_id(1)
    q = q_ref[...]
    k = jnp.concatenate([k0_ref[...], k1_ref[...], k2_ref[...]], axis=0)
    v = jnp.concatenate([v0_ref[...], v1_ref[...], v2_ref[...]], axis=0)
    n_keys = k.shape[0]
    key_block = lax.broadcasted_iota(I32, (1, n_keys), 1) // ATT_QB
    before_start = jnp.where(key_block < 2 - j, NEG_BIG, 0.0)
    outs = []
    for h in range(N_HEADS):
        sl = slice(h * HEAD_DIM_A, (h + 1) * HEAD_DIM_A)
        qh = (q[:, sl].astype(F32) * (HEAD_DIM_A ** -0.5)).astype(BF16)
        s = lax.dot_general(qh, k[:, sl], (((1,), (1,)), ((), ())), preferred_element_type=F32)
        s = s + bias_ref[h] + before_start
        outs.append(_softmax_pv([s], [v[:, sl]]))
    o_ref[...] = jnp.concatenate(outs, axis=1).astype(BF16)


def _attn_prompt(q, k, v, bias_full, batch, seq):
    qb = ATT_QB
    nq = seq // qb

    def q_map(b, j):
        return (b * nq + j, 0)

    def kv_map(back):
        return lambda b, j: (b * nq + jnp.maximum(j - back, 0), 0)

    blk = lambda m: pl.BlockSpec((qb, D_ATT), m)
    return pl.pallas_call(
        _attn_prompt_kernel,
        grid=(batch, nq),
        in_specs=[blk(q_map), blk(kv_map(2)), blk(kv_map(1)), blk(kv_map(0)),
                  blk(kv_map(2)), blk(kv_map(1)), blk(kv_map(0)),
                  pl.BlockSpec(bias_full.shape, lambda b, j: (0, 0, 0))],
        out_specs=blk(q_map),
        out_shape=jax.ShapeDtypeStruct((batch * seq, D_ATT), BF16),
        compiler_params=_cparams(("parallel", "arbitrary"), 40),
        name="attn_prompt",
    )(q, k, k, k, v, v, v, bias_full)


SAMPLE_ATT_BATCHES = 2


def _attn_sample_kernel(q_ref, kn_ref, vn_ref, ck_ref, cv_ref, bc_ref, bn_ref, o_ref, *, t_new):
    dn = (((1,), (1,)), ((), ()))
    for b in range(SAMPLE_ATT_BATCHES):
        rows = slice(b * t_new, (b + 1) * t_new)
        q = q_ref[rows, :]
        kn = kn_ref[rows, :]
        vn = vn_ref[rows, :]
        outs = []
        for h in range(N_HEADS):
            sl = slice(h * HEAD_DIM_A, (h + 1) * HEAD_DIM_A)
            qh = (q[:, sl].astype(F32) * (HEAD_DIM_A ** -0.5)).astype(BF16)
            kc = ck_ref[b, :, h, :].astype(BF16)
            vc = cv_ref[b, :, h, :].astype(BF16)
            s_c = lax.dot_general(qh, kc, dn, preferred_element_type=F32) + bc_ref[h]
            s_n = lax.dot_general(qh, kn[:, sl], dn, preferred_element_type=F32) + bn_ref[h]
            outs.append(_softmax_pv([s_c, s_n], [vc, vn[:, sl]]))
        o_ref[rows, :] = jnp.concatenate(outs, axis=1).astype(BF16)


def _attn_sample(q, k, v, cache_k, cache_v, bias_cache, bias_new, batch, t_new, cache_len):
    nb = SAMPLE_ATT_BATCHES
    blk = pl.BlockSpec((nb * t_new, D_ATT), lambda b: (b, 0))
    cblk = pl.BlockSpec((None, nb, cache_len, N_HEADS, HEAD_DIM_A), lambda b: (0, b, 0, 0, 0))
    return pl.pallas_call(
        functools.partial(_attn_sample_kernel, t_new=t_new),
        grid=(batch // nb,),
        in_specs=[blk, blk, blk, cblk, cblk,
                  pl.BlockSpec(bias_cache.shape, lambda b: (0, 0, 0)),
                  pl.BlockSpec(bias_new.shape, lambda b: (0, 0, 0))],
        out_specs=blk,
        out_shape=jax.ShapeDtypeStruct((batch * t_new, D_ATT), BF16),
        compiler_params=_cparams(("arbitrary",), 40),
        name="attn_sample",
    )(q, k, v, cache_k, cache_v, bias_cache, bias_new)


def _ret_kernel(q_ref, k_ref, v_ref, g_ref, s0_ref, dmask_ref, qdec_ref, kdec_ref, sdec_ref,
                y_ref, sout_ref, state_ref):
    c = pl.program_id(1)

    @pl.when(c == 0)
    def _():
        state_ref[...] = s0_ref[0]

    q = q_ref[...]
    k = k_ref[...]
    v = v_ref[...]
    g = g_ref[...]
    outs = []
    for h in range(N_HEADS):
        ks = slice(h * KEY_DIM_R, (h + 1) * KEY_DIM_R)
        vs = slice(h * VAL_DIM_R, (h + 1) * VAL_DIM_R)
        qh, kh, vh = q[:, ks], k[:, ks], v[:, vs]
        scores = lax.dot_general(qh, kh, (((1,), (1,)), ((), ())), preferred_element_type=F32)
        inner = jnp.dot((scores * dmask_ref[h]).astype(BF16), vh, preferred_element_type=F32)
        state = state_ref[h]
        cross = jnp.dot(qh, state.astype(BF16), preferred_element_type=F32) * qdec_ref[h]
        o = inner + cross
        v_dec = (vh.astype(F32) * kdec_ref[h]).astype(BF16)
        state_ref[h] = sdec_ref[h] * state + lax.dot_general(
            kh, v_dec, (((0,), (0,)), ((), ())), preferred_element_type=F32)
        on = o * lax.rsqrt(jnp.mean(o * o, axis=-1, keepdims=True) + EPS)
        outs.append(on * _silu(g[:, vs].astype(F32)))
    y_ref[...] = jnp.concatenate(outs, axis=1).astype(BF16)

    @pl.when(c == pl.num_programs(1) - 1)
    def _():
        sout_ref[0] = state_ref[...]


def _ret_tables(chunk):
    log_g = jnp.log(1.0 - jnp.exp2(-5.0 - jnp.arange(N_HEADS, dtype=F32)))
    i = jnp.arange(chunk, dtype=F32)
    diff = i[:, None] - i[None, :]
    dmask = jnp.where(diff >= 0, jnp.exp(log_g[:, None, None] * jnp.maximum(diff, 0.0)), 0.0)
    qdec = jnp.exp(log_g[:, None] * (i + 1.0))
    kdec = jnp.exp(log_g[:, None] * (chunk - 1.0 - i))
    sdec = jnp.exp(log_g * chunk)
    bc = lambda t: jnp.broadcast_to(t[:, :, None], (N_HEADS, t.shape[1], VAL_DIM_R)).astype(F32)
    sdec_t = jnp.broadcast_to(sdec[:, None, None], (N_HEADS, 1, VAL_DIM_R)).astype(F32)
    return dmask.astype(F32), bc(qdec), bc(kdec), sdec_t


def _retention(q, k, v, gate, state0, batch, seq, chunk):
    nc = seq // chunk
    dmask, qdec, kdec, sdec = _ret_tables(chunk)
    row = lambda w: pl.BlockSpec((chunk, w), lambda b, c: (b * nc + c, 0))
    const = lambda a: pl.BlockSpec(a.shape, lambda b, c: (0,) * a.ndim)
    st_spec = pl.BlockSpec((1, N_HEADS, KEY_DIM_R, VAL_DIM_R), lambda b, c: (b, 0, 0, 0))
    return pl.pallas_call(
        _ret_kernel,
        grid=(batch, nc),
        in_specs=[row(D_RET_K), row(D_RET_K), row(D_RET_V), row(D_RET_V), st_spec,
                  const(dmask), const(qdec), const(kdec), const(sdec)],
        out_specs=[row(D_RET_V), st_spec],
        out_shape=[jax.ShapeDtypeStruct((batch * seq, D_RET_V), BF16),
                   jax.ShapeDtypeStruct((batch, N_HEADS, KEY_DIM_R, VAL_DIM_R), F32)],
        scratch_shapes=[pltpu.VMEM((N_HEADS, KEY_DIM_R, VAL_DIM_R), F32)],
        compiler_params=_cparams(("parallel", "arbitrary"), 32),
        name="retention",
    )(q, k, v, gate, state0, dmask, qdec, kdec, sdec)


def _outproj_kernel(x_ref, oa_ref, yr_ref, ga_ref, gb_ref, g1_ref, sc2_ref, sh2_ref, g2_ref, n2g_ref,
                    woa_ref, wor_ref, wout_ref, wrt_ref, wsg_ref, wsu_ref, wsd_ref,
                    h_ref, n2p_ref, s_ref):
    ya = jnp.dot(oa_ref[...], woa_ref[...], preferred_element_type=F32)
    yr = jnp.dot(yr_ref[...], wor_ref[...], preferred_element_type=F32)
    merged = (jax.nn.sigmoid(ga_ref[...].astype(F32)) * ya
              + jax.nn.sigmoid(gb_ref[...].astype(F32)) * yr)
    mix = jnp.dot(merged.astype(BF16), wout_ref[...], preferred_element_type=F32)
    h = x_ref[...] + g1_ref[0] * mix
    hn = h * lax.rsqrt(jnp.mean(h * h, axis=-1, keepdims=True) + EPS) * n2g_ref[...]
    n2 = hn * (1.0 + sc2_ref[0]) + sh2_ref[0]
    n2b = n2.astype(BF16)
    s_ref[...] = jax.nn.sigmoid(jnp.dot(n2b, wrt_ref[...], preferred_element_type=F32))
    hid = _silu(jnp.dot(n2b, wsg_ref[...], preferred_element_type=F32)) * jnp.dot(
        n2b, wsu_ref[...], preferred_element_type=F32)
    shared = jnp.dot(hid.astype(BF16), wsd_ref[...], preferred_element_type=F32)
    h_ref[...] = h + g2_ref[0] * shared
    half = D_MODEL // 2
    n2p_ref[...] = _pack_bf16_pair(n2[:, :half], n2[:, half:])


def _outproj(x2d, oa, yr_in, ga, gb, g1, sc2, sh2, g2, n2g, weights, tiles_per_batch):
    n = x2d.shape[0]
    tm = ROW_TILE
    mod_rows = g1.shape[1]
    row = lambda w: pl.BlockSpec((tm, w), lambda i: (i, 0))
    mod_spec = pl.BlockSpec((1, mod_rows, D_MODEL), lambda i: (i // tiles_per_batch, 0, 0))
    const = lambda a: pl.BlockSpec(a.shape, lambda i: (0,) * a.ndim)
    return pl.pallas_call(
        _outproj_kernel,
        grid=(n // tm,),
        in_specs=[row(D_MODEL), row(D_ATT), row(D_RET_V), row(D_MODEL), row(D_MODEL),
                  mod_spec, mod_spec, mod_spec, mod_spec, const(n2g)] + [const(w) for w in weights],
        out_specs=[row(D_MODEL), row(D_MODEL // 2), row(N_EXPERTS)],
        out_shape=[jax.ShapeDtypeStruct((n, D_MODEL), F32),
                   jax.ShapeDtypeStruct((n, D_MODEL // 2), U32),
                   jax.ShapeDtypeStruct((n, N_EXPERTS), F32)],
        compiler_params=_cparams(("arbitrary",), 48),
        name="outproj",
    )(x2d, oa, yr_in, ga, gb, g1, sc2, sh2, g2, n2g, *weights)


def _route_kernel(s_ref, b_ref, idx_ref, w_ref, rank_ref, cnt_ref, run_ref, tri_ref):
    step = pl.program_id(0)
    t = s_ref.shape[0]

    @pl.when(step == 0)
    def _():
        run_ref[...] = jnp.zeros_like(run_ref)
        r = lax.broadcasted_iota(I32, (t, t), 0)
        c = lax.broadcasted_iota(I32, (t, t), 1)
        tri_ref[...] = jnp.where(c < r, 1.0, 0.0).astype(BF16)

    s = s_ref[...]
    sel = s + b_ref[...]
    lane = lax.broadcasted_iota(I32, (t, N_EXPERTS), 1)
    lane_f = lane.astype(F32)
    grp = lane // GROUP_SIZE

    def first_argmax(vals):
        m = jnp.max(vals, axis=-1, keepdims=True)
        pos = jnp.min(jnp.where(vals == m, lane_f, float(N_EXPERTS)), axis=-1, keepdims=True)
        return m, pos

    gscore = []
    for g in range(N_GROUPS):
        vals = jnp.where(grp == g, sel, -jnp.inf)
        m1, p1 = first_argmax(vals)
        m2 = jnp.max(jnp.where(lane_f == p1, -jnp.inf, vals), axis=-1, keepdims=True)
        gscore.append(m1 + m2)
    allowed = jnp.zeros((t, N_EXPERTS), F32)
    for g in range(N_GROUPS):
        beaten_by = jnp.zeros((t, 1), F32)
        for o in range(N_GROUPS):
            if o == g:
                continue
            wins = (gscore[o] > gscore[g]) if o > g else (gscore[o] >= gscore[g])
            beaten_by = beaten_by + jnp.where(wins, 1.0, 0.0)
        kept = jnp.where(beaten_by < TOPK_GROUPS, 1.0, 0.0)
        allowed = jnp.where(grp == g, kept, allowed)
    cand = jnp.where(allowed > 0.5, sel, -jnp.inf)

    out_lane = lax.broadcasted_iota(I32, (t, LANES), 1)
    picked = jnp.zeros((t, N_EXPERTS), F32)
    idx_cols, w_cols = [], []
    for _ in range(TOP_K):
        _, pos = first_argmax(cand)
        hit = lane_f == pos
        w_cols.append(jnp.sum(jnp.where(hit, s, 0.0), axis=-1, keepdims=True))
        idx_cols.append(pos)
        picked = jnp.where(hit, 1.0, picked)
        cand = jnp.where(hit, -jnp.inf, cand)
    w_sum = functools.reduce(jnp.add, w_cols)

    before = jnp.dot(tri_ref[...], picked.astype(BF16), preferred_element_type=F32) + run_ref[...]
    run_ref[...] = run_ref[...] + jnp.sum(picked, axis=0, keepdims=True)

    idx_out = jnp.zeros((t, LANES), F32)
    w_out = jnp.zeros((t, LANES), F32)
    rank_out = jnp.zeros((t, LANES), F32)
    for kk in range(TOP_K):
        hit = lane_f == idx_cols[kk]
        rk = jnp.sum(jnp.where(hit, before, 0.0), axis=-1, keepdims=True)
        idx_out = jnp.where(out_lane == kk, idx_cols[kk], idx_out)
        w_out = jnp.where(out_lane == kk, w_cols[kk] / w_sum * ROUTED_SCALE, w_out)
        rank_out = jnp.where(out_lane == kk, rk, rank_out)
    idx_ref[...] = idx_out.astype(I32)
    w_ref[...] = w_out
    rank_ref[...] = rank_out.astype(I32)

    @pl.when(step == pl.num_programs(0) - 1)
    def _():
        cnt_ref[...] = run_ref[...].astype(I32)


def _route(scores, b_router):
    n = scores.shape[0]
    t = ROUTE_TILE
    row = pl.BlockSpec((t, LANES), lambda i: (i, 0))
    return pl.pallas_call(
        _route_kernel,
        grid=(n // t,),
        in_specs=[pl.BlockSpec((t, N_EXPERTS), lambda i: (i, 0)),
                  pl.BlockSpec((1, N_EXPERTS), lambda i: (0, 0))],
        out_specs=[row, row, row, pl.BlockSpec((1, N_EXPERTS), lambda i: (0, 0))],
        out_shape=[jax.ShapeDtypeStruct((n, LANES), I32),
                   jax.ShapeDtypeStruct((n, LANES), F32),
                   jax.ShapeDtypeStruct((n, LANES), I32),
                   jax.ShapeDtypeStruct((1, N_EXPERTS), I32)],
        scratch_shapes=[pltpu.VMEM((1, N_EXPERTS), F32), pltpu.VMEM((t, t), BF16)],
        compiler_params=_cparams(("arbitrary",), 32),
        name="route",
    )(scores, b_router)


def _dest_kernel(idx_ref, rank_ref, start_ref, dest_ref):
    t = idx_ref.shape[0]
    idx = idx_ref[...]
    lane = lax.broadcasted_iota(I32, (t, N_EXPERTS), 1)
    out_lane = lax.broadcasted_iota(I32, (t, LANES), 1)
    starts = start_ref[...].astype(F32)
    base = jnp.zeros((t, LANES), F32)
    for kk in range(TOP_K):
        hit = lane == idx[:, kk:kk + 1]
        st = jnp.sum(jnp.where(hit, starts, 0.0), axis=-1, keepdims=True)
        base = jnp.where(out_lane == kk, st, base)
    dest_ref[...] = base.astype(I32) + rank_ref[...]


def _dest(idx, rank, starts):
    n = idx.shape[0]
    t = ROUTE_TILE
    row = pl.BlockSpec((t, LANES), lambda i: (i, 0))
    return pl.pallas_call(
        _dest_kernel,
        grid=(n // t,),
        in_specs=[row, row, pl.BlockSpec((1, N_EXPERTS), lambda i: (0, 0))],
        out_specs=row,
        out_shape=jax.ShapeDtypeStruct((n, LANES), I32),
        compiler_params=_cparams(("arbitrary",), 32),
        name="dest",
    )(idx, rank, starts)


def _row_copy(src, src_row, dst, dst_row, sem):
    return pltpu.make_async_copy(src.at[pl.ds(src_row, 1)], dst.at[pl.ds(dst_row, 1)], sem)


def _dispatch_kernel(dest_ref, pend_ref, x_ref, xs_ref, zero_ref, sem):
    t = x_ref.shape[0]

    @pl.when(pl.program_id(0) == 0)
    def _():
        zero_ref[...] = jnp.zeros_like(zero_ref)

        def fill(e, carry):
            row = pl.multiple_of(jnp.maximum(pend_ref[e] - FFN_TILE, 0), FFN_TILE)
            pltpu.make_async_copy(zero_ref, xs_ref.at[pl.ds(row, FFN_TILE)], sem).start()
            return carry

        lax.fori_loop(0, N_EXPERTS, fill, 0)

        def fill_wait(e, carry):
            pltpu.make_async_copy(zero_ref, xs_ref.at[pl.ds(0, FFN_TILE)], sem).wait()
            return carry

        lax.fori_loop(0, N_EXPERTS, fill_wait, 0)

    def issue(i, carry):
        for kk in range(TOP_K):
            _row_copy(x_ref, i, xs_ref, dest_ref[i * TOP_K + kk], sem).start(priority=kk % 2)
        return carry

    lax.fori_loop(0, t, issue, 0)

    def drain(i, carry):
        for kk in range(TOP_K):
            _row_copy(x_ref, 0, xs_ref, 0, sem).wait()
        return carry

    lax.fori_loop(0, t, drain, 0)


def _dispatch(dest_flat, pend, n2p, n_sorted_rows):
    n, width = n2p.shape
    t = MOVE_TILE
    return pl.pallas_call(
        _dispatch_kernel,
        grid=(n // t,),
        in_specs=[pl.BlockSpec((t * TOP_K,), lambda i: (i,), memory_space=pltpu.SMEM),
                  pl.BlockSpec(memory_space=pltpu.SMEM),
                  pl.BlockSpec((t, width), lambda i: (i, 0))],
        out_specs=pl.BlockSpec(memory_space=pl.ANY),
        out_shape=jax.ShapeDtypeStruct((n_sorted_rows, width), U32),
        scratch_shapes=[pltpu.VMEM((FFN_TILE, width), U32), pltpu.SemaphoreType.DMA(())],
        compiler_params=_cparams(("arbitrary",), 32),
        name="dispatch",
    )(dest_flat, pend, n2p)


def _ffn_kernel(texp_ref, ntiles_ref, eslot_ref, enext_ref, xs_ref, wg_hbm, wu_hbm, wd_hbm, ys_ref,
                wg_buf, wu_buf, wd_buf, wgu_bf, wd_bf, sems):
    g = pl.program_id(0)

    def weight_copies(e, slot):
        return (pltpu.make_async_copy(wg_hbm.at[e], wg_buf.at[slot], sems.at[slot, 0]),
                pltpu.make_async_copy(wu_hbm.at[e], wu_buf.at[slot], sems.at[slot, 1]),
                pltpu.make_async_copy(wd_hbm.at[e], wd_buf.at[slot], sems.at[slot, 2]))

    @pl.when(g < ntiles_ref[0])
    def _():
        e = texp_ref[g]
        changed = jnp.logical_or(g == 0, texp_ref[jnp.maximum(g - 1, 0)] != e)

        @pl.when(changed)
        def _():
            slot = eslot_ref[e]

            @pl.when(g == 0)
            def _():
                for c in weight_copies(e, slot):
                    c.start()

            for c in weight_copies(e, slot):
                c.wait()
            nxt = enext_ref[e]

            @pl.when(nxt >= 0)
            def _():
                for c in weight_copies(nxt, 1 - slot):
                    c.start()

            wgu_bf[:, :D_EXPERT] = wg_buf[slot].astype(BF16)
            wgu_bf[:, D_EXPERT:] = wu_buf[slot].astype(BF16)
            wd_bf[...] = wd_buf[slot].astype(BF16)

        lo, hi = _unpack_bf16_pair(xs_ref[...])
        x = jnp.concatenate([lo, hi], axis=1).astype(BF16)
        gu = jnp.dot(x, wgu_bf[...], preferred_element_type=F32)
        hid = (_silu(gu[:, :D_EXPERT]) * gu[:, D_EXPERT:]).astype(BF16)
        y = jnp.dot(hid, wd_bf[...], preferred_element_type=F32)
        half = D_MODEL // 2
        ys_ref[...] = _pack_bf16_pair(y[:, :half], y[:, half:])


def _ffn(tile_expert, n_tiles, expert_slot, expert_next, xs, w_gate, w_up, w_down):
    rows, width = xs.shape
    m = FFN_TILE
    max_tiles = tile_expert.shape[0]
    row_map = lambda g, te, nt, es, en: (jnp.minimum(g, nt[0] - 1), 0)
    hbm = pl.BlockSpec(memory_space=pl.ANY)
    grid_spec = pltpu.PrefetchScalarGridSpec(
        num_scalar_prefetch=4,
        grid=(max_tiles,),
        in_specs=[pl.BlockSpec((m, width), row_map), hbm, hbm, hbm],
        out_specs=pl.BlockSpec((m, width), row_map),
        scratch_shapes=[pltpu.VMEM((2, D_MODEL, D_EXPERT), F32),
                        pltpu.VMEM((2, D_MODEL, D_EXPERT), F32),
                        pltpu.VMEM((2, D_EXPERT, D_MODEL), F32),
                        pltpu.VMEM((D_MODEL, 2 * D_EXPERT), BF16),
                        pltpu.VMEM((D_EXPERT, D_MODEL), BF16),
                        pltpu.SemaphoreType.DMA((2, 3))],
    )
    return pl.pallas_call(
        _ffn_kernel,
        grid_spec=grid_spec,
        out_shape=jax.ShapeDtypeStruct((rows, width), U32),
        compiler_params=_cparams(("arbitrary",), 32),
        name="ffn",
    )(tile_expert, n_tiles, expert_slot, expert_next, xs, w_gate, w_up, w_down)


def _ffn_plan(counts, n_assign):
    m = FFN_TILE
    max_tiles = n_assign // m + N_EXPERTS
    padded = ((counts + m - 1) // m) * m
    pend = jnp.cumsum(padded).astype(I32)
    pstart = pend - padded
    n_tiles = pend[-1:] // m
    g = jnp.minimum(jnp.arange(max_tiles, dtype=I32), n_tiles - 1)
    tile_expert = jnp.sum((pend[None, :] <= (g * m)[:, None]).astype(I32), axis=1)
    tile_expert = jnp.minimum(tile_expert, N_EXPERTS - 1)
    used = counts > 0
    expert_slot = ((jnp.cumsum(used.astype(I32)) - 1) % 2).astype(I32)
    ids = jnp.where(used, jnp.arange(N_EXPERTS, dtype=I32), N_EXPERTS)
    first_used_from = lax.cummin(ids, axis=0, reverse=True)
    nxt = jnp.concatenate([first_used_from[1:], jnp.full((1,), N_EXPERTS, I32)])
    expert_next = jnp.where(nxt < N_EXPERTS, nxt, -1).astype(I32)
    return tile_expert, n_tiles, expert_slot, expert_next, pstart, pend, max_tiles * m


def _combine_kernel(dest_ref, w_ref, h_ref, g2_ref, nf_ref, ys_ref, y_ref, buf_ref, sem):
    t = h_ref.shape[0]

    def issue(i, carry):
        for kk in range(TOP_K):
            _row_copy(ys_ref, dest_ref[i * TOP_K + kk], buf_ref.at[kk], i, sem).start(priority=kk % 2)
        return carry

    lax.fori_loop(0, t, issue, 0)

    def drain(i, carry):
        for kk in range(TOP_K):
            _row_copy(ys_ref, 0, buf_ref.at[kk], 0, sem).wait()
        return carry

    lax.fori_loop(0, t, drain, 0)

    w = w_ref[...]
    acc_lo = jnp.zeros((t, D_MODEL // 2), F32)
    acc_hi = jnp.zeros((t, D_MODEL // 2), F32)
    for kk in range(TOP_K):
        lo, hi = _unpack_bf16_pair(buf_ref[kk])
        wk = w[:, kk:kk + 1]
        acc_lo = acc_lo + wk * lo
        acc_hi = acc_hi + wk * hi
    out = h_ref[...] + g2_ref[0] * jnp.concatenate([acc_lo, acc_hi], axis=1)
    y_ref[...] = out * lax.rsqrt(jnp.mean(out * out, axis=-1, keepdims=True) + EPS) * nf_ref[...]


def _combine(dest_flat, w, h2, g2, normf, ys, row_offset, tiles_per_batch):
    n = h2.shape[0]
    t = MOVE_TILE
    off = row_offset // t
    mod_rows = g2.shape[1]
    mod_tiles = max(ROW_TILE // t, 1) * tiles_per_batch if mod_rows == 1 else n // t
    return pl.pallas_call(
        _combine_kernel,
        grid=(n // t,),
        in_specs=[pl.BlockSpec((t * TOP_K,), lambda i: (i + off,), memory_space=pltpu.SMEM),
                  pl.BlockSpec((t, LANES), lambda i: (i + off, 0)),
                  pl.BlockSpec((t, D_MODEL), lambda i: (i, 0)),
                  pl.BlockSpec((1, mod_rows if mod_rows == 1 else t, D_MODEL),
                               (lambda i: (i // mod_tiles, 0, 0)) if mod_rows == 1
                               else (lambda i: (0, i, 0))),
                  pl.BlockSpec((1, D_MODEL), lambda i: (0, 0)),
                  pl.BlockSpec(memory_space=pl.ANY)],
        out_specs=pl.BlockSpec((t, D_MODEL), lambda i: (i, 0)),
        out_shape=jax.ShapeDtypeStruct((n, D_MODEL), F32),
        scratch_shapes=[pltpu.VMEM((TOP_K, t, D_MODEL // 2), U32), pltpu.SemaphoreType.DMA(())],
        compiler_params=_cparams(("arbitrary",), 32),
        name="combine",
    )(dest_flat, w, h2, g2, normf, ys)


def _rotary_tables(pos):
    half = KEY_DIM_R // 2
    inv_freq = ROPE_BASE ** (-jnp.arange(half, dtype=F32) / half)
    ang = pos[:, None] * inv_freq[None, :]
    cos = jnp.cos(ang)
    sin = jnp.sin(ang)
    cos_t = jnp.tile(jnp.concatenate([cos, cos], axis=1), (1, N_HEADS))
    sin_t = jnp.tile(jnp.concatenate([-sin, sin], axis=1), (1, N_HEADS))
    return cos_t.astype(F32), sin_t.astype(F32)


def _rel_bias_table(rel_bias, n_rows, n_cols, q_offset):
    heads = rel_bias.shape[0]
    n_diag = n_rows + n_cols - 1
    dist = q_offset + (n_rows - 1) - np.arange(n_diag)
    idx = np.clip(dist, -REL_CLIP, REL_CLIP) + REL_CLIP
    n_hi = int(np.sum(dist > REL_CLIP))
    n_lo = int(np.sum(dist < -REL_CLIP))
    mid = rel_bias[:, int(idx[n_diag - n_lo - 1]):int(idx[n_hi]) + 1][:, ::-1]
    diag = jnp.concatenate([jnp.broadcast_to(rel_bias[:, 2 * REL_CLIP:], (heads, n_hi)), mid,
                            jnp.broadcast_to(rel_bias[:, :1], (heads, n_lo))], axis=1)
    period = n_diag + 1
    v = jnp.roll(jnp.pad(diag, ((0, 0), (0, 1))), -(n_rows - 1), axis=1)
    skew = jnp.tile(v, (1, n_rows))[:, :n_rows * (period - 1)].reshape(heads, n_rows, period - 1)
    return skew[:, :, :n_cols].astype(F32)


def _prompt_bias(rel_bias):
    n_cols = ATT_QB + ATT_WINDOW
    r = np.arange(ATT_QB)[:, None]
    c = np.arange(n_cols)[None, :]
    band = c - (r // CHUNK) * CHUNK
    valid = (band >= 0) & (band < ATT_WINDOW + CHUNK)
    table = _rel_bias_table(rel_bias, ATT_QB, n_cols, ATT_WINDOW)
    return jnp.where(jnp.asarray(valid)[None], table, NEG_BIG)


def _sample_bias(rel_bias, t_new, cache_len):
    b = _rel_bias_table(rel_bias, t_new, cache_len + t_new, cache_len)
    return b[:, :, :cache_len], b[:, :, cache_len:]


def _mod_parts(mod, rows_each):
    parts = jnp.split(mod, 6, axis=-1)
    if rows_each == 1:
        return [p[:, None, :] for p in parts]
    return [jnp.repeat(p, rows_each, axis=0)[None] for p in parts]


def kernel(x_prompt, x_sample, cache_attn_k, cache_attn_v, state_ret, c_prompt, c_sample,
           norm1_g, norm2_g, w_ada, b_ada, w_in, rel_bias, w_o_attn, w_o_ret, w_out,
           w_router, b_router, w_exp_gate, w_exp_up, w_exp_down, w_sh_gate, w_sh_up, w_sh_down,
           normf_g):
    batch, seq, d = x_prompt.shape
    dec_batch, dec_seq, _ = x_sample.shape
    depth = w_in.shape[0]
    assert depth == 1 and d == D_MODEL
    assert seq % ROW_TILE == 0 and dec_batch * dec_seq == ROW_TILE and ROW_TILE == ATT_WINDOW
    cache_len = cache_attn_k.shape[2]
    n_p = batch * seq
    n_s = dec_batch * dec_seq
    tpb = seq // ROW_TILE
    l = 0

    bf = lambda a: a.astype(BF16)
    c_all = jnp.concatenate([c_prompt, c_sample], axis=0)
    pad = (-c_all.shape[0]) % 8
    c_all = jnp.pad(c_all, ((0, pad), (0, 0)))
    mod = _ada(c_all, bf(w_ada[l]), b_ada[l][None, :])
    mod_p = _mod_parts(mod[:batch], 1)
    mod_s = _mod_parts(mod[batch:batch + dec_batch], dec_seq)

    w_in_bf = bf(w_in[l])
    n1g = norm1_g[l][None, :]
    n2g = norm2_g[l][None, :]
    dense_w = [bf(w_o_attn[l]), bf(w_o_ret[l]), bf(w_out[l]), bf(w_router[l]),
               bf(w_sh_gate[l]), bf(w_sh_up[l]), bf(w_sh_down[l])]

    xp = x_prompt.reshape(n_p, d)
    xs_ = x_sample.reshape(n_s, d)
    cos_p, sin_p = _rotary_tables(jnp.arange(seq, dtype=F32))
    pos_s = PAST_LEN + jnp.arange(dec_seq, dtype=F32)
    cos_s, sin_s = _rotary_tables(jnp.tile(pos_s, dec_batch))

    (qa, ka, va, qr, kr, vr, gr, ga, gb, kv_p) = _inproj(
        xp, mod_p[1], mod_p[0], n1g, cos_p, sin_p, w_in_bf, tpb)
    oa = _attn_prompt(qa, ka, va, _prompt_bias(rel_bias[l]), batch, seq)
    zero_state = jnp.zeros((batch, N_HEADS, KEY_DIM_R, VAL_DIM_R), F32)
    yr_in, state_p = _retention(qr, kr, vr, gr, zero_state, batch, seq, RET_CHUNK)
    h_p, n2p_p, s_p = _outproj(xp, oa, yr_in, ga, gb, mod_p[2], mod_p[4], mod_p[3], mod_p[5], n2g,
                               dense_w, tpb)

    (qa_s, ka_s, va_s, qr_s, kr_s, vr_s, gr_s, ga_s, gb_s, kv_s) = _inproj(
        xs_, mod_s[1], mod_s[0], n1g, cos_s, sin_s, w_in_bf, 1)
    bias_c, bias_n = _sample_bias(rel_bias[l], dec_seq, cache_len)
    oa_s = _attn_sample(qa_s, ka_s, va_s, cache_attn_k, cache_attn_v, bias_c, bias_n,
                        dec_batch, dec_seq, cache_len)
    yr_in_s, state_s = _retention(qr_s, kr_s, vr_s, gr_s, state_ret[l], dec_batch, dec_seq, dec_seq)
    h_s, n2p_s, s_s = _outproj(xs_, oa_s, yr_in_s, ga_s, gb_s, mod_s[2], mod_s[4], mod_s[3], mod_s[5],
                               n2g, dense_w, 1)

    n2p = jnp.concatenate([n2p_p, n2p_s], axis=0)
    scores = jnp.concatenate([s_p, s_s], axis=0)
    idx, w_route, rank, counts = _route(scores, b_router[l][None, :])
    (tile_expert, n_tiles, expert_slot, expert_next, pstart, pend,
     n_sorted_rows) = _ffn_plan(counts[0], (n_p + n_s) * TOP_K)
    dest = _dest(idx, rank, pstart[None, :])
    dest_flat = dest[:, :TOP_K].reshape(-1)
    xs_sorted = _dispatch(dest_flat, pend, n2p, n_sorted_rows)
    ys_sorted = _ffn(tile_expert, n_tiles, expert_slot, expert_next, xs_sorted,
                     w_exp_gate[l], w_exp_up[l], w_exp_down[l])
    nf = normf_g[None, :]
    y_p = _combine(dest_flat, w_route, h_p, mod_p[5], nf, ys_sorted, 0, tpb)
    y_s = _combine(dest_flat, w_route, h_s, mod_s[5], nf, ys_sorted, n_p, 1)

    keep = min(ATT_WINDOW, seq)
    kv_p = kv_p.reshape(batch, ROW_TILE, 2, N_HEADS, HEAD_DIM_A)[:, ROW_TILE - keep:]
    kv_s = kv_s.reshape(dec_batch, dec_seq, 2, N_HEADS, HEAD_DIM_A)
    return (y_p.reshape(batch, seq, d), y_s.reshape(dec_batch, dec_seq, d),
            kv_p[:, :, 0][None], kv_p[:, :, 1][None], state_p[None],
            kv_s[:, :, 0][None], kv_s[:, :, 1][None], state_s[None])
```

```python
import functools

import numpy as np
import jax
import jax.numpy as jnp
from jax import lax
from jax.experimental import pallas as pl
from jax.experimental.pallas import tpu as pltpu

F32 = jnp.float32
BF16 = jnp.bfloat16
I32 = jnp.int32
U32 = jnp.uint32

D_MODEL = 1024
PAST_LEN = 4096
CHUNK = 64
N_LEFT_CHUNKS = 8
ATT_WINDOW = N_LEFT_CHUNKS * CHUNK
N_HEADS = 8
HEAD_DIM_A = 64
D_ATT = N_HEADS * HEAD_DIM_A
REL_CLIP = 128
KEY_DIM_R = 64
VAL_DIM_R = 128
D_RET_K = N_HEADS * KEY_DIM_R
D_RET_V = N_HEADS * VAL_DIM_R
ROPE_BASE = 10000.0
N_EXPERTS = 256
TOP_K = 8
N_GROUPS = 8
GROUP_SIZE = N_EXPERTS // N_GROUPS
TOPK_GROUPS = 4
D_EXPERT = 256
ROUTED_SCALE = 2.5
EPS = 1e-6
IN_WIDTHS = (D_ATT, D_ATT, D_ATT, D_RET_K, D_RET_K, D_RET_V, D_RET_V, D_MODEL, D_MODEL)
IN_OFFS = tuple(int(v) for v in np.cumsum((0,) + IN_WIDTHS))
D_IN = IN_OFFS[-1]

NEG_BIG = -1e30
LANES = 128
V7X_VMEM_BYTES = 64 * 1024 * 1024

ROW_TILE = 512
ATT_QB = 256
RET_CHUNK = 256
ROUTE_TILE = 512
MOVE_TILE = 256
FFN_QUANTUM = 128
FFN_HALF = 2 * FFN_QUANTUM
FFN_TILE = 2 * FFN_HALF


def _cparams(semantics, vmem_mb):
    return pltpu.CompilerParams(dimension_semantics=semantics,
                                vmem_limit_bytes=min(vmem_mb * 1024 * 1024, V7X_VMEM_BYTES - (6 << 20)))


def _silu(x):
    return x * jax.nn.sigmoid(x)


def _pack_bf16_pair(lo, hi):
    lo_b = pltpu.bitcast(lo.astype(BF16).astype(F32), U32) >> 16
    hi_b = pltpu.bitcast(hi.astype(BF16).astype(F32), U32) & jnp.uint32(0xFFFF0000)
    return lo_b | hi_b


def _unpack_bf16_pair(u):
    lo = pltpu.bitcast(u << 16, F32)
    hi = pltpu.bitcast(u & jnp.uint32(0xFFFF0000), F32)
    return lo, hi


def _ada_kernel(c_ref, w_ref, b_ref, o_ref):
    sc = _silu(c_ref[...]).astype(BF16)
    o_ref[...] = jnp.dot(sc, w_ref[...], preferred_element_type=F32) + b_ref[...]


def _ada(c_all, w_ada_bf, b_ada):
    rows = c_all.shape[0]
    n_out = w_ada_bf.shape[1]
    blk = D_MODEL
    return pl.pallas_call(
        _ada_kernel,
        grid=(n_out // blk,),
        in_specs=[pl.BlockSpec((rows, D_MODEL), lambda j: (0, 0)),
                  pl.BlockSpec((D_MODEL, blk), lambda j: (0, j)),
                  pl.BlockSpec((1, blk), lambda j: (0, j))],
        out_specs=pl.BlockSpec((rows, blk), lambda j: (0, j)),
        out_shape=jax.ShapeDtypeStruct((rows, n_out), F32),
        compiler_params=_cparams(("arbitrary",), 24),
        name="ada",
    )(c_all, w_ada_bf, b_ada)


def _inproj_kernel(x_ref, sc_ref, sh_ref, g_ref, cos_ref, sin_ref, w_ref,
                   qa_ref, ka_ref, va_ref, qr_ref, kr_ref, vr_ref, gr_ref, ga_ref, gb_ref,
                   kv_ref, *, tiles_per_batch):
    x = x_ref[...]
    xn = x * lax.rsqrt(jnp.mean(x * x, axis=-1, keepdims=True) + EPS) * g_ref[...]
    nb = (xn * (1.0 + sc_ref[0]) + sh_ref[0]).astype(BF16)

    def proj(seg):
        return jnp.dot(nb, w_ref[:, IN_OFFS[seg]:IN_OFFS[seg + 1]], preferred_element_type=F32)

    qa_ref[...] = proj(0).astype(BF16)
    ka = proj(1)
    va = proj(2)
    ka_ref[...] = ka.astype(BF16)
    va_ref[...] = va.astype(BF16)

    @pl.when(pl.program_id(0) % tiles_per_batch == tiles_per_batch - 1)
    def _():
        kv_ref[:, :D_ATT] = ka
        kv_ref[:, D_ATT:] = va

    cos = cos_ref[...]
    sin = sin_ref[...]
    first_half = (lax.broadcasted_iota(I32, (1, D_RET_K), 1) % KEY_DIM_R) < (KEY_DIM_R // 2)

    def rotary(t):
        partner = jnp.where(first_half, pltpu.roll(t, D_RET_K - KEY_DIM_R // 2, 1),
                            pltpu.roll(t, KEY_DIM_R // 2, 1))
        return t * cos + partner * sin

    qr_ref[...] = rotary(proj(3)).astype(BF16)
    kr_ref[...] = (rotary(proj(4)) * (KEY_DIM_R ** -0.5)).astype(BF16)
    vr_ref[...] = proj(5).astype(BF16)
    gr_ref[...] = proj(6).astype(BF16)
    ga_ref[...] = proj(7).astype(BF16)
    gb_ref[...] = proj(8).astype(BF16)


def _inproj(x2d, sc, sh, g, cos_t, sin_t, w_in_bf, tiles_per_batch):
    n = x2d.shape[0]
    tm = ROW_TILE
    n_tiles = n // tm
    n_batches = n_tiles // tiles_per_batch
    mod_rows = sc.shape[1]
    pos_tiles = cos_t.shape[0] // tm

    def row_spec(width):
        return pl.BlockSpec((tm, width), lambda i: (i, 0))

    mod_spec = pl.BlockSpec((1, mod_rows, D_MODEL), lambda i: (i // tiles_per_batch, 0, 0))
    pos_spec = pl.BlockSpec((tm, D_RET_K), lambda i: (i % pos_tiles, 0))
    out_widths = (D_ATT, D_ATT, D_ATT, D_RET_K, D_RET_K, D_RET_V, D_RET_V, D_MODEL, D_MODEL)
    out_shape = [jax.ShapeDtypeStruct((n, w), BF16) for w in out_widths]
    out_shape.append(jax.ShapeDtypeStruct((n_batches * tm, 2 * D_ATT), F32))
    out_specs = [row_spec(w) for w in out_widths]
    out_specs.append(pl.BlockSpec((tm, 2 * D_ATT), lambda i: (i // tiles_per_batch, 0)))
    return pl.pallas_call(
        functools.partial(_inproj_kernel, tiles_per_batch=tiles_per_batch),
        grid=(n_tiles,),
        in_specs=[row_spec(D_MODEL), mod_spec, mod_spec,
                  pl.BlockSpec((1, D_MODEL), lambda i: (0, 0)),
                  pos_spec, pos_spec,
                  pl.BlockSpec((D_MODEL, D_IN), lambda i: (0, 0))],
        out_specs=out_specs,
        out_shape=out_shape,
        compiler_params=_cparams(("arbitrary",), 56),
        name="inproj",
    )(x2d, sc, sh, g, cos_t, sin_t, w_in_bf)


def _softmax_pv(s, v_parts):
    m = functools.reduce(jnp.maximum, [jnp.max(t, axis=-1, keepdims=True) for t in s])
    ps = [jnp.exp(t - m) for t in s]
    l = functools.reduce(jnp.add, [jnp.sum(p, axis=-1, keepdims=True) for p in ps])
    o = functools.reduce(jnp.add, [jnp.dot(p.astype(BF16), v, preferred_element_type=F32)
                                   for p, v in zip(ps, v_parts)])
    return o / l


def _attn_prompt_kernel(q_ref, k0_ref, k1_ref, k2_ref, v0_ref, v1_ref, v2_ref, bias_ref, o_ref):
    j = pl.program_id(1)
    q = q_ref[...]
    k = jnp.concatenate([k0_ref[...], k1_ref[...], k2_ref[...]], axis=0)
    v = jnp.concatenate([v0_ref[...], v1_ref[...], v2_ref[...]], axis=0)
    n_keys = k.shape[0]
    key_block = lax.broadcasted_iota(I32, (1, n_keys), 1) // ATT_QB
    before_start = jnp.where(key_block < 2 - j, NEG_BIG, 0.0)
    outs = []
    for h in range(N_HEADS):
        sl = slice(h * HEAD_DIM_A, (h + 1) * HEAD_DIM_A)
        qh = (q[:, sl].astype(F32) * (HEAD_DIM_A ** -0.5)).astype(BF16)
        s = lax.dot_general(qh, k[:, sl], (((1,), (1,)), ((), ())), preferred_element_type=F32)
        s = s + bias_ref[h] + before_start
        outs.append(_softmax_pv([s], [v[:, sl]]))
    o_ref[...] = jnp.concatenate(outs, axis=1).astype(BF16)


def _attn_prompt(q, k, v, bias_full, batch, seq):
    qb = ATT_QB
    nq = seq // qb

    def q_map(b, j):
        return (b * nq + j, 0)

    def kv_map(back):
        return lambda b, j: (b * nq + jnp.maximum(j - back, 0), 0)

    blk = lambda m: pl.BlockSpec((qb, D_ATT), m)
    return pl.pallas_call(
        _attn_prompt_kernel,
        grid=(batch, nq),
        in_specs=[blk(q_map), blk(kv_map(2)), blk(kv_map(1)), blk(kv_map(0)),
                  blk(kv_map(2)), blk(kv_map(1)), blk(kv_map(0)),
                  pl.BlockSpec(bias_full.shape, lambda b, j: (0, 0, 0))],
        out_specs=blk(q_map),
        out_shape=jax.ShapeDtypeStruct((batch * seq, D_ATT), BF16),
        compiler_params=_cparams(("parallel", "arbitrary"), 40),
        name="attn_prompt",
    )(q, k, k, k, v, v, v, bias_full)


SAMPLE_ATT_BATCHES = 2


def _attn_sample_kernel(q_ref, kn_ref, vn_ref, ck_ref, cv_ref, bc_ref, bn_ref, o_ref, *, t_new):
    nt = (((1,), (1,)), ((), ()))
    for b in range(SAMPLE_ATT_BATCHES):
        rows = slice(b * t_new, (b + 1) * t_new)
        q = q_ref[rows, :]
        kn = kn_ref[rows, :]
        vn = vn_ref[rows, :]
        outs = []
        for h in range(N_HEADS):
            sl = slice(h * HEAD_DIM_A, (h + 1) * HEAD_DIM_A)
            qh = (q[:, sl].astype(F32) * (HEAD_DIM_A ** -0.5)).astype(BF16)
            kc_t = ck_ref[b, h].astype(BF16)
            vc_t = cv_ref[b, h].astype(BF16)
            s_c = jnp.dot(qh, kc_t, preferred_element_type=F32) + bc_ref[h]
            s_n = lax.dot_general(qh, kn[:, sl], nt, preferred_element_type=F32) + bn_ref[h]
            m = jnp.maximum(jnp.max(s_c, axis=-1, keepdims=True), jnp.max(s_n, axis=-1, keepdims=True))
            p_c = jnp.exp(s_c - m)
            p_n = jnp.exp(s_n - m)
            l = jnp.sum(p_c, axis=-1, keepdims=True) + jnp.sum(p_n, axis=-1, keepdims=True)
            o = (lax.dot_general(p_c.astype(BF16), vc_t, nt, preferred_element_type=F32)
                 + jnp.dot(p_n.astype(BF16), vn[:, sl], preferred_element_type=F32))
            outs.append(o / l)
        o_ref[rows, :] = jnp.concatenate(outs, axis=1).astype(BF16)


def _attn_sample(q, k, v, cache_k_t, cache_v_t, bias_cache, bias_new, batch, t_new, cache_len):
    nb = SAMPLE_ATT_BATCHES
    blk = pl.BlockSpec((nb * t_new, D_ATT), lambda b: (b, 0))
    cblk = pl.BlockSpec((None, nb, N_HEADS, HEAD_DIM_A, cache_len), lambda b: (0, b, 0, 0, 0))
    return pl.pallas_call(
        functools.partial(_attn_sample_kernel, t_new=t_new),
        grid=(batch // nb,),
        in_specs=[blk, blk, blk, cblk, cblk,
                  pl.BlockSpec(bias_cache.shape, lambda b: (0, 0, 0)),
                  pl.BlockSpec(bias_new.shape, lambda b: (0, 0, 0))],
        out_specs=blk,
        out_shape=jax.ShapeDtypeStruct((batch * t_new, D_ATT), BF16),
        compiler_params=_cparams(("arbitrary",), 40),
        name="attn_sample",
    )(q, k, v, cache_k_t, cache_v_t, bias_cache, bias_new)


def _ret_kernel(q_ref, k_ref, v_ref, g_ref, s0_ref, dmask_ref, qdec_ref, kdec_ref, sdec_ref,
                y_ref, sout_ref, state_ref):
    c = pl.program_id(1)

    @pl.when(c == 0)
    def _():
        state_ref[...] = s0_ref[0]

    q = q_ref[...]
    k = k_ref[...]
    v = v_ref[...]
    g = g_ref[...]
    outs = []
    for h in range(N_HEADS):
        ks = slice(h * KEY_DIM_R, (h + 1) * KEY_DIM_R)
        vs = slice(h * VAL_DIM_R, (h + 1) * VAL_DIM_R)
        qh, kh, vh = q[:, ks], k[:, ks], v[:, vs]
        scores = lax.dot_general(qh, kh, (((1,), (1,)), ((), ())), preferred_element_type=F32)
        inner = jnp.dot((scores * dmask_ref[h]).astype(BF16), vh, preferred_element_type=F32)
        state = state_ref[h]
        cross = jnp.dot(qh, state.astype(BF16), preferred_element_type=F32) * qdec_ref[h]
        o = inner + cross
        v_dec = (vh.astype(F32) * kdec_ref[h]).astype(BF16)
        state_ref[h] = sdec_ref[h] * state + lax.dot_general(
            kh, v_dec, (((0,), (0,)), ((), ())), preferred_element_type=F32)
        on = o * lax.rsqrt(jnp.mean(o * o, axis=-1, keepdims=True) + EPS)
        outs.append(on * _silu(g[:, vs].astype(F32)))
    y_ref[...] = jnp.concatenate(outs, axis=1).astype(BF16)

    @pl.when(c == pl.num_programs(1) - 1)
    def _():
        sout_ref[0] = state_ref[...]


def _ret_tables(chunk):
    log_g = jnp.log(1.0 - jnp.exp2(-5.0 - jnp.arange(N_HEADS, dtype=F32)))
    i = jnp.arange(chunk, dtype=F32)
    diff = i[:, None] - i[None, :]
    dmask = jnp.where(diff >= 0, jnp.exp(log_g[:, None, None] * jnp.maximum(diff, 0.0)), 0.0)
    qdec = jnp.exp(log_g[:, None] * (i + 1.0))
    kdec = jnp.exp(log_g[:, None] * (chunk - 1.0 - i))
    sdec = jnp.exp(log_g * chunk)
    bc = lambda t: jnp.broadcast_to(t[:, :, None], (N_HEADS, t.shape[1], VAL_DIM_R)).astype(F32)
    sdec_t = jnp.broadcast_to(sdec[:, None, None], (N_HEADS, 1, VAL_DIM_R)).astype(F32)
    return dmask.astype(F32), bc(qdec), bc(kdec), sdec_t


def _retention(q, k, v, gate, state0, batch, seq, chunk):
    nc = seq // chunk
    dmask, qdec, kdec, sdec = _ret_tables(chunk)
    row = lambda w: pl.BlockSpec((chunk, w), lambda b, c: (b * nc + c, 0))
    const = lambda a: pl.BlockSpec(a.shape, lambda b, c: (0,) * a.ndim)
    st_spec = pl.BlockSpec((1, N_HEADS, KEY_DIM_R, VAL_DIM_R), lambda b, c: (b, 0, 0, 0))
    return pl.pallas_call(
        _ret_kernel,
        grid=(batch, nc),
        in_specs=[row(D_RET_K), row(D_RET_K), row(D_RET_V), row(D_RET_V), st_spec,
                  const(dmask), const(qdec), const(kdec), const(sdec)],
        out_specs=[row(D_RET_V), st_spec],
        out_shape=[jax.ShapeDtypeStruct((batch * seq, D_RET_V), BF16),
                   jax.ShapeDtypeStruct((batch, N_HEADS, KEY_DIM_R, VAL_DIM_R), F32)],
        scratch_shapes=[pltpu.VMEM((N_HEADS, KEY_DIM_R, VAL_DIM_R), F32)],
        compiler_params=_cparams(("parallel", "arbitrary"), 32),
        name="retention",
    )(q, k, v, gate, state0, dmask, qdec, kdec, sdec)


def _outproj_kernel(x_ref, oa_ref, yr_ref, ga_ref, gb_ref, g1_ref, sc2_ref, sh2_ref, g2_ref, n2g_ref,
                    woa_ref, wor_ref, wout_ref, wrt_ref, wsg_ref, wsu_ref, wsd_ref,
                    h_ref, n2p_ref, s_ref):
    ya = jnp.dot(oa_ref[...], woa_ref[...], preferred_element_type=F32)
    yr = jnp.dot(yr_ref[...], wor_ref[...], preferred_element_type=F32)
    merged = (jax.nn.sigmoid(ga_ref[...].astype(F32)) * ya
              + jax.nn.sigmoid(gb_ref[...].astype(F32)) * yr)
    mix = jnp.dot(merged.astype(BF16), wout_ref[...], preferred_element_type=F32)
    h = x_ref[...] + g1_ref[0] * mix
    hn = h * lax.rsqrt(jnp.mean(h * h, axis=-1, keepdims=True) + EPS) * n2g_ref[...]
    n2 = hn * (1.0 + sc2_ref[0]) + sh2_ref[0]
    n2b = n2.astype(BF16)
    s_ref[...] = jax.nn.sigmoid(jnp.dot(n2b, wrt_ref[...], preferred_element_type=F32))
    hid = _silu(jnp.dot(n2b, wsg_ref[...], preferred_element_type=F32)) * jnp.dot(
        n2b, wsu_ref[...], preferred_element_type=F32)
    shared = jnp.dot(hid.astype(BF16), wsd_ref[...], preferred_element_type=F32)
    h_ref[...] = h + g2_ref[0] * shared
    half = D_MODEL // 2
    n2p_ref[...] = _pack_bf16_pair(n2[:, :half], n2[:, half:])


def _outproj(x2d, oa, yr_in, ga, gb, g1, sc2, sh2, g2, n2g, weights, tiles_per_batch):
    n = x2d.shape[0]
    tm = ROW_TILE
    mod_rows = g1.shape[1]
    row = lambda w: pl.BlockSpec((tm, w), lambda i: (i, 0))
    mod_spec = pl.BlockSpec((1, mod_rows, D_MODEL), lambda i: (i // tiles_per_batch, 0, 0))
    const = lambda a: pl.BlockSpec(a.shape, lambda i: (0,) * a.ndim)
    return pl.pallas_call(
        _outproj_kernel,
        grid=(n // tm,),
        in_specs=[row(D_MODEL), row(D_ATT), row(D_RET_V), row(D_MODEL), row(D_MODEL),
                  mod_spec, mod_spec, mod_spec, mod_spec, const(n2g)] + [const(w) for w in weights],
        out_specs=[row(D_MODEL), row(D_MODEL // 2), row(N_EXPERTS)],
        out_shape=[jax.ShapeDtypeStruct((n, D_MODEL), F32),
                   jax.ShapeDtypeStruct((n, D_MODEL // 2), U32),
                   jax.ShapeDtypeStruct((n, N_EXPERTS), F32)],
        compiler_params=_cparams(("arbitrary",), 48),
        name="outproj",
    )(x2d, oa, yr_in, ga, gb, g1, sc2, sh2, g2, n2g, *weights)


def _route_kernel(s_ref, b_ref, idx_ref, w_ref, rank_ref, cnt_ref, run_ref, tri_ref):
    step = pl.program_id(0)
    t = s_ref.shape[0]

    @pl.when(step == 0)
    def _():
        run_ref[...] = jnp.zeros_like(run_ref)
        r = lax.broadcasted_iota(I32, (t, t), 0)
        c = lax.broadcasted_iota(I32, (t, t), 1)
        tri_ref[...] = jnp.where(c < r, 1.0, 0.0).astype(BF16)

    s = s_ref[...]
    sel = s + b_ref[...]
    lane = lax.broadcasted_iota(I32, (t, N_EXPERTS), 1)
    lane_f = lane.astype(F32)
    grp = lane // GROUP_SIZE

    def first_argmax(vals):
        m = jnp.max(vals, axis=-1, keepdims=True)
        pos = jnp.min(jnp.where(vals == m, lane_f, float(N_EXPERTS)), axis=-1, keepdims=True)
        return m, pos

    gscore = []
    for g in range(N_GROUPS):
        vals = jnp.where(grp == g, sel, -jnp.inf)
        m1, p1 = first_argmax(vals)
        m2 = jnp.max(jnp.where(lane_f == p1, -jnp.inf, vals), axis=-1, keepdims=True)
        gscore.append(m1 + m2)
    allowed = jnp.zeros((t, N_EXPERTS), F32)
    for g in range(N_GROUPS):
        beaten_by = jnp.zeros((t, 1), F32)
        for o in range(N_GROUPS):
            if o == g:
                continue
            wins = (gscore[o] > gscore[g]) if o > g else (gscore[o] >= gscore[g])
            beaten_by = beaten_by + jnp.where(wins, 1.0, 0.0)
        kept = jnp.where(beaten_by < TOPK_GROUPS, 1.0, 0.0)
        allowed = jnp.where(grp == g, kept, allowed)
    cand = jnp.where(allowed > 0.5, sel, -jnp.inf)

    out_lane = lax.broadcasted_iota(I32, (t, LANES), 1)
    picked = jnp.zeros((t, N_EXPERTS), F32)
    idx_cols, w_cols = [], []
    for _ in range(TOP_K):
        _, pos = first_argmax(cand)
        hit = lane_f == pos
        w_cols.append(jnp.sum(jnp.where(hit, s, 0.0), axis=-1, keepdims=True))
        idx_cols.append(pos)
        picked = jnp.where(hit, 1.0, picked)
        cand = jnp.where(hit, -jnp.inf, cand)
    w_sum = functools.reduce(jnp.add, w_cols)

    before = jnp.dot(tri_ref[...], picked.astype(BF16), preferred_element_type=F32) + run_ref[...]
    run_ref[...] = run_ref[...] + jnp.sum(picked, axis=0, keepdims=True)

    idx_out = jnp.zeros((t, LANES), F32)
    w_out = jnp.zeros((t, LANES), F32)
    rank_out = jnp.zeros((t, LANES), F32)
    for kk in range(TOP_K):
        hit = lane_f == idx_cols[kk]
        rk = jnp.sum(jnp.where(hit, before, 0.0), axis=-1, keepdims=True)
        idx_out = jnp.where(out_lane == kk, idx_cols[kk], idx_out)
        w_out = jnp.where(out_lane == kk, w_cols[kk] / w_sum * ROUTED_SCALE, w_out)
        rank_out = jnp.where(out_lane == kk, rk, rank_out)
    idx_ref[...] = idx_out.astype(I32)
    w_ref[...] = w_out
    rank_ref[...] = rank_out.astype(I32)

    @pl.when(step == pl.num_programs(0) - 1)
    def _():
        cnt_ref[...] = run_ref[...].astype(I32)


def _route(scores, b_router):
    n = scores.shape[0]
    t = ROUTE_TILE
    row = pl.BlockSpec((t, LANES), lambda i: (i, 0))
    return pl.pallas_call(
        _route_kernel,
        grid=(n // t,),
        in_specs=[pl.BlockSpec((t, N_EXPERTS), lambda i: (i, 0)),
                  pl.BlockSpec((1, N_EXPERTS), lambda i: (0, 0))],
        out_specs=[row, row, row, pl.BlockSpec((1, N_EXPERTS), lambda i: (0, 0))],
        out_shape=[jax.ShapeDtypeStruct((n, LANES), I32),
                   jax.ShapeDtypeStruct((n, LANES), F32),
                   jax.ShapeDtypeStruct((n, LANES), I32),
                   jax.ShapeDtypeStruct((1, N_EXPERTS), I32)],
        scratch_shapes=[pltpu.VMEM((1, N_EXPERTS), F32), pltpu.VMEM((t, t), BF16)],
        compiler_params=_cparams(("arbitrary",), 32),
        name="route",
    )(scores, b_router)


def _dest_kernel(idx_ref, rank_ref, start_ref, dest_ref):
    t = idx_ref.shape[0]
    idx = idx_ref[...]
    lane = lax.broadcasted_iota(I32, (t, N_EXPERTS), 1)
    out_lane = lax.broadcasted_iota(I32, (t, LANES), 1)
    starts = start_ref[...].astype(F32)
    base = jnp.zeros((t, LANES), F32)
    for kk in range(TOP_K):
        hit = lane == idx[:, kk:kk + 1]
        st = jnp.sum(jnp.where(hit, starts, 0.0), axis=-1, keepdims=True)
        base = jnp.where(out_lane == kk, st, base)
    dest_ref[...] = base.astype(I32) + rank_ref[...]


def _dest(idx, rank, starts):
    n = idx.shape[0]
    t = ROUTE_TILE
    row = pl.BlockSpec((t, LANES), lambda i: (i, 0))
    return pl.pallas_call(
        _dest_kernel,
        grid=(n // t,),
        in_specs=[row, row, pl.BlockSpec((1, N_EXPERTS), lambda i: (0, 0))],
        out_specs=row,
        out_shape=jax.ShapeDtypeStruct((n, LANES), I32),
        compiler_params=_cparams(("arbitrary",), 32),
        name="dest",
    )(idx, rank, starts)


def _row_copy(src, src_row, dst, dst_row, sem):
    return pltpu.make_async_copy(src.at[pl.ds(src_row, 1)], dst.at[pl.ds(dst_row, 1)], sem)


def _dispatch_kernel(dest_ref, vend_ref, x_ref, xs_ref, zero_ref, sem):
    t = x_ref.shape[0]

    @pl.when(pl.program_id(0) == 0)
    def _():
        zero_ref[...] = jnp.zeros_like(zero_ref)

        def fill(e, carry):
            row = pl.multiple_of(jnp.maximum(vend_ref[e] - FFN_QUANTUM, 0), FFN_QUANTUM)
            pltpu.make_async_copy(zero_ref, xs_ref.at[pl.ds(row, FFN_QUANTUM)], sem).start()
            return carry

        lax.fori_loop(0, N_EXPERTS, fill, 0)

        def fill_wait(e, carry):
            pltpu.make_async_copy(zero_ref, xs_ref.at[pl.ds(0, FFN_QUANTUM)], sem).wait()
            return carry

        lax.fori_loop(0, N_EXPERTS, fill_wait, 0)

    def issue(i, carry):
        for kk in range(TOP_K):
            _row_copy(x_ref, i, xs_ref, dest_ref[i * TOP_K + kk], sem).start(priority=kk % 2)
        return carry

    lax.fori_loop(0, t, issue, 0)

    def drain(i, carry):
        for kk in range(TOP_K):
            _row_copy(x_ref, 0, xs_ref, 0, sem).wait()
        return carry

    lax.fori_loop(0, t, drain, 0)


def _dispatch(dest_flat, pend, n2p, n_sorted_rows):
    n, width = n2p.shape
    t = MOVE_TILE
    return pl.pallas_call(
        _dispatch_kernel,
        grid=(n // t,),
        in_specs=[pl.BlockSpec((t * TOP_K,), lambda i: (i,), memory_space=pltpu.SMEM),
                  pl.BlockSpec(memory_space=pltpu.SMEM),
                  pl.BlockSpec((t, width), lambda i: (i, 0))],
        out_specs=pl.BlockSpec(memory_space=pl.ANY),
        out_shape=jax.ShapeDtypeStruct((n_sorted_rows, width), U32),
        scratch_shapes=[pltpu.VMEM((FFN_QUANTUM, width), U32), pltpu.SemaphoreType.DMA(())],
        compiler_params=_cparams(("arbitrary",), 32),
        name="dispatch",
    )(dest_flat, pend, n2p)


def _ffn_kernel(texp_ref, ntiles_ref, eslot_ref, enext_ref, nhalf_ref, xs_ref, wg_hbm, wu_hbm, wd_hbm,
                ys_ref, wg_buf, wu_buf, wd_buf, wgu_bf, wd_bf, sems):
    g = pl.program_id(0)

    def weight_copies(e, slot):
        return (pltpu.make_async_copy(wg_hbm.at[e], wg_buf.at[slot], sems.at[slot, 0]),
                pltpu.make_async_copy(wu_hbm.at[e], wu_buf.at[slot], sems.at[slot, 1]),
                pltpu.make_async_copy(wd_hbm.at[e], wd_buf.at[slot], sems.at[slot, 2]))

    @pl.when(g < ntiles_ref[0])
    def _():
        e = texp_ref[g]
        changed = jnp.logical_or(g == 0, texp_ref[jnp.maximum(g - 1, 0)] != e)

        @pl.when(changed)
        def _():
            slot = eslot_ref[e]

            @pl.when(g == 0)
            def _():
                for c in weight_copies(e, slot):
                    c.start()

            for c in weight_copies(e, slot):
                c.wait()
            nxt = enext_ref[e]

            @pl.when(nxt >= 0)
            def _():
                for c in weight_copies(nxt, 1 - slot):
                    c.start()

            wgu_bf[:, :D_EXPERT] = wg_buf[slot].astype(BF16)
            wgu_bf[:, D_EXPERT:] = wu_buf[slot].astype(BF16)
            wd_bf[...] = wd_buf[slot].astype(BF16)

        def expert_rows(rows):
            lo, hi = _unpack_bf16_pair(xs_ref[rows, :])
            x = jnp.concatenate([lo, hi], axis=1).astype(BF16)
            gu = jnp.dot(x, wgu_bf[...], preferred_element_type=F32)
            hid = (_silu(gu[:, :D_EXPERT]) * gu[:, D_EXPERT:]).astype(BF16)
            y = jnp.dot(hid, wd_bf[...], preferred_element_type=F32)
            half = D_MODEL // 2
            ys_ref[rows, :] = _pack_bf16_pair(y[:, :half], y[:, half:])

        for n_groups in range(1, FFN_TILE // FFN_QUANTUM + 1):

            @pl.when(nhalf_ref[g] == n_groups)
            def _(n_groups=n_groups):
                used = n_groups * FFN_QUANTUM
                for start in range(0, used, FFN_HALF):
                    expert_rows(slice(start, min(start + FFN_HALF, used)))
                if used < FFN_TILE:
                    ys_ref[used:, :] = jnp.zeros((FFN_TILE - used, ys_ref.shape[1]), U32)


def _ffn(tile_expert, n_tiles, expert_slot, expert_next, tile_halves, xs, w_gate, w_up, w_down):
    rows, width = xs.shape
    m = FFN_TILE
    max_tiles = tile_expert.shape[0]
    row_map = lambda g, te, nt, es, en, nh: (jnp.minimum(g, nt[0] - 1), 0)
    hbm = pl.BlockSpec(memory_space=pl.ANY)
    grid_spec = pltpu.PrefetchScalarGridSpec(
        num_scalar_prefetch=5,
        grid=(max_tiles,),
        in_specs=[pl.BlockSpec((m, width), row_map), hbm, hbm, hbm],
        out_specs=pl.BlockSpec((m, width), row_map),
        scratch_shapes=[pltpu.VMEM((2, D_MODEL, D_EXPERT), F32),
                        pltpu.VMEM((2, D_MODEL, D_EXPERT), F32),
                        pltpu.VMEM((2, D_EXPERT, D_MODEL), F32),
                        pltpu.VMEM((D_MODEL, 2 * D_EXPERT), BF16),
                        pltpu.VMEM((D_EXPERT, D_MODEL), BF16),
                        pltpu.SemaphoreType.DMA((2, 3))],
    )
    return pl.pallas_call(
        _ffn_kernel,
        grid_spec=grid_spec,
        out_shape=jax.ShapeDtypeStruct((rows, width), U32),
        compiler_params=_cparams(("arbitrary",), 32),
        name="ffn",
    )(tile_expert, n_tiles, expert_slot, expert_next, tile_halves, xs, w_gate, w_up, w_down)


def _ffn_plan(counts, n_assign):
    m = FFN_TILE
    max_tiles = n_assign // m + N_EXPERTS
    padded = ((counts + m - 1) // m) * m
    pend = jnp.cumsum(padded).astype(I32)
    pstart = pend - padded
    n_tiles = pend[-1:] // m
    g = jnp.minimum(jnp.arange(max_tiles, dtype=I32), n_tiles - 1)
    tile_expert = jnp.sum((pend[None, :] <= (g * m)[:, None]).astype(I32), axis=1)
    tile_expert = jnp.minimum(tile_expert, N_EXPERTS - 1)
    vend = pstart + ((counts + FFN_QUANTUM - 1) // FFN_QUANTUM) * FFN_QUANTUM
    own = tile_expert[:, None] == jnp.arange(N_EXPERTS, dtype=I32)[None, :]
    tile_vend = jnp.sum(jnp.where(own, vend[None, :], 0), axis=1)
    tile_halves = (jnp.clip(tile_vend - g * m, 0, m) // FFN_QUANTUM).astype(I32)
    used = counts > 0
    expert_slot = ((jnp.cumsum(used.astype(I32)) - 1) % 2).astype(I32)
    ids = jnp.where(used, jnp.arange(N_EXPERTS, dtype=I32), N_EXPERTS)
    first_used_from = lax.cummin(ids, axis=0, reverse=True)
    nxt = jnp.concatenate([first_used_from[1:], jnp.full((1,), N_EXPERTS, I32)])
    expert_next = jnp.where(nxt < N_EXPERTS, nxt, -1).astype(I32)
    return (tile_expert, n_tiles, expert_slot, expert_next, tile_halves, pstart, vend.astype(I32),
            max_tiles * m)


def _combine_kernel(dest_ref, w_ref, h_ref, g2_ref, nf_ref, ys_ref, y_ref, buf_ref, sem):
    t = h_ref.shape[0]

    def issue(i, carry):
        for kk in range(TOP_K):
            _row_copy(ys_ref, dest_ref[i * TOP_K + kk], buf_ref.at[kk], i, sem).start(priority=kk % 2)
        return carry

    lax.fori_loop(0, t, issue, 0)

    def drain(i, carry):
        for kk in range(TOP_K):
            _row_copy(ys_ref, 0, buf_ref.at[kk], 0, sem).wait()
        return carry

    lax.fori_loop(0, t, drain, 0)

    w = w_ref[...]
    acc_lo = jnp.zeros((t, D_MODEL // 2), F32)
    acc_hi = jnp.zeros((t, D_MODEL // 2), F32)
    for kk in range(TOP_K):
        lo, hi = _unpack_bf16_pair(buf_ref[kk])
        wk = w[:, kk:kk + 1]
        acc_lo = acc_lo + wk * lo
        acc_hi = acc_hi + wk * hi
    out = h_ref[...] + g2_ref[0] * jnp.concatenate([acc_lo, acc_hi], axis=1)
    y_ref[...] = out * lax.rsqrt(jnp.mean(out * out, axis=-1, keepdims=True) + EPS) * nf_ref[...]


def _combine(dest_flat, w, h2, g2, normf, ys, row_offset, tiles_per_batch):
    n = h2.shape[0]
    t = MOVE_TILE
    off = row_offset // t
    mod_rows = g2.shape[1]
    mod_tiles = max(ROW_TILE // t, 1) * tiles_per_batch if mod_rows == 1 else n // t
    return pl.pallas_call(
        _combine_kernel,
        grid=(n // t,),
        in_specs=[pl.BlockSpec((t * TOP_K,), lambda i: (i + off,), memory_space=pltpu.SMEM),
                  pl.BlockSpec((t, LANES), lambda i: (i + off, 0)),
                  pl.BlockSpec((t, D_MODEL), lambda i: (i, 0)),
                  pl.BlockSpec((1, mod_rows if mod_rows == 1 else t, D_MODEL),
                               (lambda i: (i // mod_tiles, 0, 0)) if mod_rows == 1
                               else (lambda i: (0, i, 0))),
                  pl.BlockSpec((1, D_MODEL), lambda i: (0, 0)),
                  pl.BlockSpec(memory_space=pl.ANY)],
        out_specs=pl.BlockSpec((t, D_MODEL), lambda i: (i, 0)),
        out_shape=jax.ShapeDtypeStruct((n, D_MODEL), F32),
        scratch_shapes=[pltpu.VMEM((TOP_K, t, D_MODEL // 2), U32), pltpu.SemaphoreType.DMA(())],
        compiler_params=_cparams(("arbitrary",), 32),
        name="combine",
    )(dest_flat, w, h2, g2, normf, ys)


def _rotary_tables(pos):
    half = KEY_DIM_R // 2
    inv_freq = ROPE_BASE ** (-jnp.arange(half, dtype=F32) / half)
    ang = pos[:, None] * inv_freq[None, :]
    cos = jnp.cos(ang)
    sin = jnp.sin(ang)
    cos_t = jnp.tile(jnp.concatenate([cos, cos], axis=1), (1, N_HEADS))
    sin_t = jnp.tile(jnp.concatenate([-sin, sin], axis=1), (1, N_HEADS))
    return cos_t.astype(F32), sin_t.astype(F32)


def _rel_bias_table(rel_bias, n_rows, n_cols, q_offset):
    heads = rel_bias.shape[0]
    n_diag = n_rows + n_cols - 1
    dist = q_offset + (n_rows - 1) - np.arange(n_diag)
    idx = np.clip(dist, -REL_CLIP, REL_CLIP) + REL_CLIP
    n_hi = int(np.sum(dist > REL_CLIP))
    n_lo = int(np.sum(dist < -REL_CLIP))
    mid = rel_bias[:, int(idx[n_diag - n_lo - 1]):int(idx[n_hi]) + 1][:, ::-1]
    diag = jnp.concatenate([jnp.broadcast_to(rel_bias[:, 2 * REL_CLIP:], (heads, n_hi)), mid,
                            jnp.broadcast_to(rel_bias[:, :1], (heads, n_lo))], axis=1)
    period = n_diag + 1
    v = jnp.roll(jnp.pad(diag, ((0, 0), (0, 1))), -(n_rows - 1), axis=1)
    skew = jnp.tile(v, (1, n_rows))[:, :n_rows * (period - 1)].reshape(heads, n_rows, period - 1)
    return skew[:, :, :n_cols].astype(F32)


def _prompt_bias(rel_bias):
    n_cols = ATT_QB + ATT_WINDOW
    r = np.arange(ATT_QB)[:, None]
    c = np.arange(n_cols)[None, :]
    band = c - (r // CHUNK) * CHUNK
    valid = (band >= 0) & (band < ATT_WINDOW + CHUNK)
    table = _rel_bias_table(rel_bias, ATT_QB, n_cols, ATT_WINDOW)
    return jnp.where(jnp.asarray(valid)[None], table, NEG_BIG)


def _sample_bias(rel_bias, t_new, cache_len):
    b = _rel_bias_table(rel_bias, t_new, cache_len + t_new, cache_len)
    return b[:, :, :cache_len], b[:, :, cache_len:]


def _mod_parts(mod, rows_each):
    parts = jnp.split(mod, 6, axis=-1)
    if rows_each == 1:
        return [p[:, None, :] for p in parts]
    return [jnp.repeat(p, rows_each, axis=0)[None] for p in parts]


def kernel(x_prompt, x_sample, cache_attn_k, cache_attn_v, state_ret, c_prompt, c_sample,
           norm1_g, norm2_g, w_ada, b_ada, w_in, rel_bias, w_o_attn, w_o_ret, w_out,
           w_router, b_router, w_exp_gate, w_exp_up, w_exp_down, w_sh_gate, w_sh_up, w_sh_down,
           normf_g):
    batch, seq, d = x_prompt.shape
    dec_batch, dec_seq, _ = x_sample.shape
    depth = w_in.shape[0]
    assert depth == 1 and d == D_MODEL
    assert seq % ROW_TILE == 0 and dec_batch * dec_seq == ROW_TILE and ROW_TILE == ATT_WINDOW
    cache_len = cache_attn_k.shape[2]
    n_p = batch * seq
    n_s = dec_batch * dec_seq
    tpb = seq // ROW_TILE
    l = 0

    bf = lambda a: a.astype(BF16)
    c_all = jnp.concatenate([c_prompt, c_sample], axis=0)
    pad = (-c_all.shape[0]) % 8
    c_all = jnp.pad(c_all, ((0, pad), (0, 0)))
    mod = _ada(c_all, bf(w_ada[l]), b_ada[l][None, :])
    mod_p = _mod_parts(mod[:batch], 1)
    mod_s = _mod_parts(mod[batch:batch + dec_batch], dec_seq)

    w_in_bf = bf(w_in[l])
    n1g = norm1_g[l][None, :]
    n2g = norm2_g[l][None, :]
    dense_w = [bf(w_o_attn[l]), bf(w_o_ret[l]), bf(w_out[l]), bf(w_router[l]),
               bf(w_sh_gate[l]), bf(w_sh_up[l]), bf(w_sh_down[l])]

    xp = x_prompt.reshape(n_p, d)
    xs_ = x_sample.reshape(n_s, d)
    cos_p, sin_p = _rotary_tables(jnp.arange(seq, dtype=F32))
    pos_s = PAST_LEN + jnp.arange(dec_seq, dtype=F32)
    cos_s, sin_s = _rotary_tables(jnp.tile(pos_s, dec_batch))

    (qa, ka, va, qr, kr, vr, gr, ga, gb, kv_p) = _inproj(
        xp, mod_p[1], mod_p[0], n1g, cos_p, sin_p, w_in_bf, tpb)
    oa = _attn_prompt(qa, ka, va, _prompt_bias(rel_bias[l]), batch, seq)
    zero_state = jnp.zeros((batch, N_HEADS, KEY_DIM_R, VAL_DIM_R), F32)
    yr_in, state_p = _retention(qr, kr, vr, gr, zero_state, batch, seq, RET_CHUNK)
    h_p, n2p_p, s_p = _outproj(xp, oa, yr_in, ga, gb, mod_p[2], mod_p[4], mod_p[3], mod_p[5], n2g,
                               dense_w, tpb)

    (qa_s, ka_s, va_s, qr_s, kr_s, vr_s, gr_s, ga_s, gb_s, kv_s) = _inproj(
        xs_, mod_s[1], mod_s[0], n1g, cos_s, sin_s, w_in_bf, 1)
    bias_c, bias_n = _sample_bias(rel_bias[l], dec_seq, cache_len)
    to_keys_minor = lambda c: jnp.transpose(c, (0, 1, 3, 4, 2))
    oa_s = _attn_sample(qa_s, ka_s, va_s, to_keys_minor(cache_attn_k), to_keys_minor(cache_attn_v),
                        bias_c, bias_n, dec_batch, dec_seq, cache_len)
    yr_in_s, state_s = _retention(qr_s, kr_s, vr_s, gr_s, state_ret[l], dec_batch, dec_seq, dec_seq)
    h_s, n2p_s, s_s = _outproj(xs_, oa_s, yr_in_s, ga_s, gb_s, mod_s[2], mod_s[4], mod_s[3], mod_s[5],
                               n2g, dense_w, 1)

    n2p = jnp.concatenate([n2p_p, n2p_s], axis=0)
    scores = jnp.concatenate([s_p, s_s], axis=0)
    idx, w_route, rank, counts = _route(scores, b_router[l][None, :])
    (tile_expert, n_tiles, expert_slot, expert_next, tile_halves, pstart, vend,
     n_sorted_rows) = _ffn_plan(counts[0], (n_p + n_s) * TOP_K)
    dest = _dest(idx, rank, pstart[None, :])
    dest_flat = dest[:, :TOP_K].reshape(-1)
    xs_sorted = _dispatch(dest_flat, vend, n2p, n_sorted_rows)
    ys_sorted = _ffn(tile_expert, n_tiles, expert_slot, expert_next, tile_halves, xs_sorted,
                     w_exp_gate[l], w_exp_up[l], w_exp_down[l])
    nf = normf_g[None, :]
    y_p = _combine(dest_flat, w_route, h_p, mod_p[5], nf, ys_sorted, 0, tpb)
    y_s = _combine(dest_flat, w_route, h_s, mod_s[5], nf, ys_sorted, n_p, 1)

    keep = min(ATT_WINDOW, seq)
    kv_p = kv_p.reshape(batch, ROW_TILE, 2, N_HEADS, HEAD_DIM_A)[:, ROW_TILE - keep:]
    kv_s = kv_s.reshape(dec_batch, dec_seq, 2, N_HEADS, HEAD_DIM_A)
    return (y_p.reshape(batch, seq, d), y_s.reshape(dec_batch, dec_seq, d),
            kv_p[:, :, 0][None], kv_p[:, :, 1][None], state_p[None],
            kv_s[:, :, 0][None], kv_s[:, :, 1][None], state_s[None])
```

```python
import functools

import numpy as np
import jax
import jax.numpy as jnp
from jax import lax
from jax.experimental import pallas as pl
from jax.experimental.pallas import tpu as pltpu

F32 = jnp.float32
BF16 = jnp.bfloat16
I32 = jnp.int32
U32 = jnp.uint32

D_MODEL = 1024
PAST_LEN = 4096
CHUNK = 64
N_LEFT_CHUNKS = 8
ATT_WINDOW = N_LEFT_CHUNKS * CHUNK
N_HEADS = 8
HEAD_DIM_A = 64
D_ATT = N_HEADS * HEAD_DIM_A
REL_CLIP = 128
KEY_DIM_R = 64
VAL_DIM_R = 128
D_RET_K = N_HEADS * KEY_DIM_R
D_RET_V = N_HEADS * VAL_DIM_R
ROPE_BASE = 10000.0
N_EXPERTS = 256
TOP_K = 8
N_GROUPS = 8
GROUP_SIZE = N_EXPERTS // N_GROUPS
TOPK_GROUPS = 4
D_EXPERT = 256
ROUTED_SCALE = 2.5
EPS = 1e-6
IN_WIDTHS = (D_ATT, D_ATT, D_ATT, D_RET_K, D_RET_K, D_RET_V, D_RET_V, D_MODEL, D_MODEL)
IN_OFFS = tuple(int(v) for v in np.cumsum((0,) + IN_WIDTHS))
D_IN = IN_OFFS[-1]

NEG_BIG = -1e30
LANES = 128
V7X_VMEM_BYTES = 64 * 1024 * 1024

ROW_TILE = 512
ATT_QB = 256
RET_CHUNK = 256
ROUTE_TILE = 512
MOVE_TILE = 256
FFN_QUANTUM = 128
FFN_HALF = 2 * FFN_QUANTUM
FFN_TILE = 2 * FFN_HALF


def _cparams(semantics, vmem_mb):
    return pltpu.CompilerParams(dimension_semantics=semantics,
                                vmem_limit_bytes=min(vmem_mb * 1024 * 1024, V7X_VMEM_BYTES - (6 << 20)))


def _silu(x):
    return x * jax.nn.sigmoid(x)


def _pack_bf16_pair(lo, hi):
    lo_b = pltpu.bitcast(lo.astype(BF16).astype(F32), U32) >> 16
    hi_b = pltpu.bitcast(hi.astype(BF16).astype(F32), U32) & jnp.uint32(0xFFFF0000)
    return lo_b | hi_b


def _unpack_bf16_pair(u):
    lo = pltpu.bitcast(u << 16, F32)
    hi = pltpu.bitcast(u & jnp.uint32(0xFFFF0000), F32)
    return lo, hi


def _ada_kernel(c_ref, w_ref, b_ref, o_ref):
    sc = _silu(c_ref[...]).astype(BF16)
    o_ref[...] = jnp.dot(sc, w_ref[...], preferred_element_type=F32) + b_ref[...]


def _ada(c_all, w_ada_bf, b_ada):
    rows = c_all.shape[0]
    n_out = w_ada_bf.shape[1]
    blk = D_MODEL
    return pl.pallas_call(
        _ada_kernel,
        grid=(n_out // blk,),
        in_specs=[pl.BlockSpec((rows, D_MODEL), lambda j: (0, 0)),
                  pl.BlockSpec((D_MODEL, blk), lambda j: (0, j)),
                  pl.BlockSpec((1, blk), lambda j: (0, j))],
        out_specs=pl.BlockSpec((rows, blk), lambda j: (0, j)),
        out_shape=jax.ShapeDtypeStruct((rows, n_out), F32),
        compiler_params=_cparams(("arbitrary",), 24),
        name="ada",
    )(c_all, w_ada_bf, b_ada)


def _inproj_kernel(x_ref, sc_ref, sh_ref, g_ref, cos_ref, sin_ref, w_ref,
                   qa_ref, ka_ref, va_ref, qr_ref, kr_ref, vr_ref, gr_ref, ga_ref, gb_ref,
                   kv_ref, *, tiles_per_batch):
    x = x_ref[...]
    xn = x * lax.rsqrt(jnp.mean(x * x, axis=-1, keepdims=True) + EPS) * g_ref[...]
    nb = (xn * (1.0 + sc_ref[0]) + sh_ref[0]).astype(BF16)

    def proj(seg):
        return jnp.dot(nb, w_ref[:, IN_OFFS[seg]:IN_OFFS[seg + 1]], preferred_element_type=F32)

    qa_ref[...] = proj(0).astype(BF16)
    ka = proj(1)
    va = proj(2)
    ka_ref[...] = ka.astype(BF16)
    va_ref[...] = va.astype(BF16)

    @pl.when(pl.program_id(0) % tiles_per_batch == tiles_per_batch - 1)
    def _():
        kv_ref[:, :D_ATT] = ka
        kv_ref[:, D_ATT:] = va

    cos = cos_ref[...]
    sin = sin_ref[...]
    first_half = (lax.broadcasted_iota(I32, (1, D_RET_K), 1) % KEY_DIM_R) < (KEY_DIM_R // 2)

    def rotary(t):
        partner = jnp.where(first_half, pltpu.roll(t, D_RET_K - KEY_DIM_R // 2, 1),
                            pltpu.roll(t, KEY_DIM_R // 2, 1))
        return t * cos + partner * sin

    qr_ref[...] = rotary(proj(3)).astype(BF16)
    kr_ref[...] = (rotary(proj(4)) * (KEY_DIM_R ** -0.5)).astype(BF16)
    vr_ref[...] = proj(5).astype(BF16)
    gr_ref[...] = proj(6).astype(BF16)
    ga_ref[...] = proj(7).astype(BF16)
    gb_ref[...] = proj(8).astype(BF16)


def _inproj(x2d, sc, sh, g, cos_t, sin_t, w_in_bf, tiles_per_batch):
    n = x2d.shape[0]
    tm = ROW_TILE
    n_tiles = n // tm
    n_batches = n_tiles // tiles_per_batch
    mod_rows = sc.shape[1]
    pos_tiles = cos_t.shape[0] // tm

    def row_spec(width):
        return pl.BlockSpec((tm, width), lambda i: (i, 0))

    mod_spec = pl.BlockSpec((1, mod_rows, D_MODEL), lambda i: (i // tiles_per_batch, 0, 0))
    pos_spec = pl.BlockSpec((tm, D_RET_K), lambda i: (i % pos_tiles, 0))
    out_widths = (D_ATT, D_ATT, D_ATT, D_RET_K, D_RET_K, D_RET_V, D_RET_V, D_MODEL, D_MODEL)
    out_shape = [jax.ShapeDtypeStruct((n, w), BF16) for w in out_widths]
    out_shape.append(jax.ShapeDtypeStruct((n_batches * tm, 2 * D_ATT), F32))
    out_specs = [row_spec(w) for w in out_widths]
    out_specs.append(pl.BlockSpec((tm, 2 * D_ATT), lambda i: (i // tiles_per_batch, 0)))
    return pl.pallas_call(
        functools.partial(_inproj_kernel, tiles_per_batch=tiles_per_batch),
        grid=(n_tiles,),
        in_specs=[row_spec(D_MODEL), mod_spec, mod_spec,
                  pl.BlockSpec((1, D_MODEL), lambda i: (0, 0)),
                  pos_spec, pos_spec,
                  pl.BlockSpec((D_MODEL, D_IN), lambda i: (0, 0))],
        out_specs=out_specs,
        out_shape=out_shape,
        compiler_params=_cparams(("arbitrary",), 56),
        name="inproj",
    )(x2d, sc, sh, g, cos_t, sin_t, w_in_bf)


def _softmax_pv(s, v_parts):
    m = functools.reduce(jnp.maximum, [jnp.max(t, axis=-1, keepdims=True) for t in s])
    ps = [jnp.exp(t - m) for t in s]
    l = functools.reduce(jnp.add, [jnp.sum(p, axis=-1, keepdims=True) for p in ps])
    o = functools.reduce(jnp.add, [jnp.dot(p.astype(BF16), v, preferred_element_type=F32)
                                   for p, v in zip(ps, v_parts)])
    return o / l


def _attn_prompt_kernel(q_ref, k0_ref, k1_ref, k2_ref, v0_ref, v1_ref, v2_ref, bias_ref, o_ref):
    j = pl.program_id(1)
    q = q_ref[...]
    k = jnp.concatenate([k0_ref[...], k1_ref[...], k2_ref[...]], axis=0)
    v = jnp.concatenate([v0_ref[...], v1_ref[...], v2_ref[...]], axis=0)
    n_keys = k.shape[0]
    key_block = lax.broadcasted_iota(I32, (1, n_keys), 1) // ATT_QB
    before_start = jnp.where(key_block < 2 - j, NEG_BIG, 0.0)
    outs = []
    for h in range(N_HEADS):
        sl = slice(h * HEAD_DIM_A, (h + 1) * HEAD_DIM_A)
        qh = (q[:, sl].astype(F32) * (HEAD_DIM_A ** -0.5)).astype(BF16)
        s = lax.dot_general(qh, k[:, sl], (((1,), (1,)), ((), ())), preferred_element_type=F32)
        s = s + bias_ref[h] + before_start
        outs.append(_softmax_pv([s], [v[:, sl]]))
    o_ref[...] = jnp.concatenate(outs, axis=1).astype(BF16)


def _attn_prompt(q, k, v, bias_full, batch, seq):
    qb = ATT_QB
    nq = seq // qb

    def q_map(b, j):
        return (b * nq + j, 0)

    def kv_map(back):
        return lambda b, j: (b * nq + jnp.maximum(j - back, 0), 0)

    blk = lambda m: pl.BlockSpec((qb, D_ATT), m)
    return pl.pallas_call(
        _attn_prompt_kernel,
        grid=(batch, nq),
        in_specs=[blk(q_map), blk(kv_map(2)), blk(kv_map(1)), blk(kv_map(0)),
                  blk(kv_map(2)), blk(kv_map(1)), blk(kv_map(0)),
                  pl.BlockSpec(bias_full.shape, lambda b, j: (0, 0, 0))],
        out_specs=blk(q_map),
        out_shape=jax.ShapeDtypeStruct((batch * seq, D_ATT), BF16),
        compiler_params=_cparams(("parallel", "arbitrary"), 40),
        name="attn_prompt",
    )(q, k, k, k, v, v, v, bias_full)


SAMPLE_ATT_BATCHES = 2


def _attn_sample_kernel(q_ref, kn_ref, vn_ref, ck_ref, cv_ref, bc_ref, bn_ref, o_ref, *, t_new):
    nt = (((1,), (1,)), ((), ()))
    for b in range(SAMPLE_ATT_BATCHES):
        rows = slice(b * t_new, (b + 1) * t_new)
        q = q_ref[rows, :]
        kn = kn_ref[rows, :]
        vn = vn_ref[rows, :]
        outs = []
        for h in range(N_HEADS):
            sl = slice(h * HEAD_DIM_A, (h + 1) * HEAD_DIM_A)
            qh = (q[:, sl].astype(F32) * (HEAD_DIM_A ** -0.5)).astype(BF16)
            kc_t = ck_ref[b, h].astype(BF16)
            vc_t = cv_ref[b, h].astype(BF16)
            s_c = jnp.dot(qh, kc_t, preferred_element_type=F32) + bc_ref[h]
            s_n = lax.dot_general(qh, kn[:, sl], nt, preferred_element_type=F32) + bn_ref[h]
            m = jnp.maximum(jnp.max(s_c, axis=-1, keepdims=True), jnp.max(s_n, axis=-1, keepdims=True))
            p_c = jnp.exp(s_c - m)
            p_n = jnp.exp(s_n - m)
            l = jnp.sum(p_c, axis=-1, keepdims=True) + jnp.sum(p_n, axis=-1, keepdims=True)
            o = (lax.dot_general(p_c.astype(BF16), vc_t, nt, preferred_element_type=F32)
                 + jnp.dot(p_n.astype(BF16), vn[:, sl], preferred_element_type=F32))
            outs.append(o / l)
        o_ref[rows, :] = jnp.concatenate(outs, axis=1).astype(BF16)


def _attn_sample(q, k, v, cache_k_t, cache_v_t, bias_cache, bias_new, batch, t_new, cache_len):
    nb = SAMPLE_ATT_BATCHES
    blk = pl.BlockSpec((nb * t_new, D_ATT), lambda b: (b, 0))
    cblk = pl.BlockSpec((None, nb, N_HEADS, HEAD_DIM_A, cache_len), lambda b: (0, b, 0, 0, 0))
    return pl.pallas_call(
        functools.partial(_attn_sample_kernel, t_new=t_new),
        grid=(batch // nb,),
        in_specs=[blk, blk, blk, cblk, cblk,
                  pl.BlockSpec(bias_cache.shape, lambda b: (0, 0, 0)),
                  pl.BlockSpec(bias_new.shape, lambda b: (0, 0, 0))],
        out_specs=blk,
        out_shape=jax.ShapeDtypeStruct((batch * t_new, D_ATT), BF16),
        compiler_params=_cparams(("arbitrary",), 40),
        name="attn_sample",
    )(q, k, v, cache_k_t, cache_v_t, bias_cache, bias_new)


def _ret_kernel(q_ref, k_ref, v_ref, g_ref, s0_ref, dmask_ref, qdec_ref, kdec_ref, sdec_ref,
                y_ref, sout_ref, state_ref):
    c = pl.program_id(1)

    @pl.when(c == 0)
    def _():
        state_ref[...] = s0_ref[0]

    q = q_ref[...]
    k = k_ref[...]
    v = v_ref[...]
    g = g_ref[...]
    outs = []
    for h in range(N_HEADS):
        ks = slice(h * KEY_DIM_R, (h + 1) * KEY_DIM_R)
        vs = slice(h * VAL_DIM_R, (h + 1) * VAL_DIM_R)
        qh, kh, vh = q[:, ks], k[:, ks], v[:, vs]
        scores = lax.dot_general(qh, kh, (((1,), (1,)), ((), ())), preferred_element_type=F32)
        inner = jnp.dot((scores * dmask_ref[h]).astype(BF16), vh, preferred_element_type=F32)
        state = state_ref[h]
        cross = jnp.dot(qh, state.astype(BF16), preferred_element_type=F32) * qdec_ref[h]
        o = inner + cross
        v_dec = (vh.astype(F32) * kdec_ref[h]).astype(BF16)
        state_ref[h] = sdec_ref[h] * state + lax.dot_general(
            kh, v_dec, (((0,), (0,)), ((), ())), preferred_element_type=F32)
        on = o * lax.rsqrt(jnp.mean(o * o, axis=-1, keepdims=True) + EPS)
        outs.append(on * _silu(g[:, vs].astype(F32)))
    y_ref[...] = jnp.concatenate(outs, axis=1).astype(BF16)

    @pl.when(c == pl.num_programs(1) - 1)
    def _():
        sout_ref[0] = state_ref[...]


def _ret_tables(chunk):
    log_g = jnp.log(1.0 - jnp.exp2(-5.0 - jnp.arange(N_HEADS, dtype=F32)))
    i = jnp.arange(chunk, dtype=F32)
    diff = i[:, None] - i[None, :]
    dmask = jnp.where(diff >= 0, jnp.exp(log_g[:, None, None] * jnp.maximum(diff, 0.0)), 0.0)
    qdec = jnp.exp(log_g[:, None] * (i + 1.0))
    kdec = jnp.exp(log_g[:, None] * (chunk - 1.0 - i))
    sdec = jnp.exp(log_g * chunk)
    bc = lambda t: jnp.broadcast_to(t[:, :, None], (N_HEADS, t.shape[1], VAL_DIM_R)).astype(F32)
    sdec_t = jnp.broadcast_to(sdec[:, None, None], (N_HEADS, 1, VAL_DIM_R)).astype(F32)
    return dmask.astype(F32), bc(qdec), bc(kdec), sdec_t


def _retention(q, k, v, gate, state0, batch, seq, chunk):
    nc = seq // chunk
    dmask, qdec, kdec, sdec = _ret_tables(chunk)
    row = lambda w: pl.BlockSpec((chunk, w), lambda b, c: (b * nc + c, 0))
    const = lambda a: pl.BlockSpec(a.shape, lambda b, c: (0,) * a.ndim)
    st_spec = pl.BlockSpec((1, N_HEADS, KEY_DIM_R, VAL_DIM_R), lambda b, c: (b, 0, 0, 0))
    return pl.pallas_call(
        _ret_kernel,
        grid=(batch, nc),
        in_specs=[row(D_RET_K), row(D_RET_K), row(D_RET_V), row(D_RET_V), st_spec,
                  const(dmask), const(qdec), const(kdec), const(sdec)],
        out_specs=[row(D_RET_V), st_spec],
        out_shape=[jax.ShapeDtypeStruct((batch * seq, D_RET_V), BF16),
                   jax.ShapeDtypeStruct((batch, N_HEADS, KEY_DIM_R, VAL_DIM_R), F32)],
        scratch_shapes=[pltpu.VMEM((N_HEADS, KEY_DIM_R, VAL_DIM_R), F32)],
        compiler_params=_cparams(("parallel", "arbitrary"), 32),
        name="retention",
    )(q, k, v, gate, state0, dmask, qdec, kdec, sdec)


def _outproj_kernel(x_ref, oa_ref, yr_ref, ga_ref, gb_ref, g1_ref, sc2_ref, sh2_ref, g2_ref, n2g_ref,
                    woa_ref, wor_ref, wout_ref, wrt_ref, wsg_ref, wsu_ref, wsd_ref,
                    h_ref, n2p_ref, s_ref):
    ya = jnp.dot(oa_ref[...], woa_ref[...], preferred_element_type=F32)
    yr = jnp.dot(yr_ref[...], wor_ref[...], preferred_element_type=F32)
    merged = (jax.nn.sigmoid(ga_ref[...].astype(F32)) * ya
              + jax.nn.sigmoid(gb_ref[...].astype(F32)) * yr)
    mix = jnp.dot(merged.astype(BF16), wout_ref[...], preferred_element_type=F32)
    h = x_ref[...] + g1_ref[0] * mix
    hn = h * lax.rsqrt(jnp.mean(h * h, axis=-1, keepdims=True) + EPS) * n2g_ref[...]
    n2 = hn * (1.0 + sc2_ref[0]) + sh2_ref[0]
    n2b = n2.astype(BF16)
    s_ref[...] = jax.nn.sigmoid(lax.dot_general(wrt_ref[...], n2b, (((1,), (1,)), ((), ())),
                                                preferred_element_type=F32))
    hid = _silu(jnp.dot(n2b, wsg_ref[...], preferred_element_type=F32)) * jnp.dot(
        n2b, wsu_ref[...], preferred_element_type=F32)
    shared = jnp.dot(hid.astype(BF16), wsd_ref[...], preferred_element_type=F32)
    h_ref[...] = h + g2_ref[0] * shared
    half = D_MODEL // 2
    n2p_ref[...] = _pack_bf16_pair(n2[:, :half], n2[:, half:])


def _outproj(x2d, oa, yr_in, ga, gb, g1, sc2, sh2, g2, n2g, weights, tiles_per_batch):
    n = x2d.shape[0]
    tm = ROW_TILE
    mod_rows = g1.shape[1]
    row = lambda w: pl.BlockSpec((tm, w), lambda i: (i, 0))
    mod_spec = pl.BlockSpec((1, mod_rows, D_MODEL), lambda i: (i // tiles_per_batch, 0, 0))
    const = lambda a: pl.BlockSpec(a.shape, lambda i: (0,) * a.ndim)
    return pl.pallas_call(
        _outproj_kernel,
        grid=(n // tm,),
        in_specs=[row(D_MODEL), row(D_ATT), row(D_RET_V), row(D_MODEL), row(D_MODEL),
                  mod_spec, mod_spec, mod_spec, mod_spec, const(n2g)] + [const(w) for w in weights],
        out_specs=[row(D_MODEL), row(D_MODEL // 2), pl.BlockSpec((N_EXPERTS, tm), lambda i: (0, i))],
        out_shape=[jax.ShapeDtypeStruct((n, D_MODEL), F32),
                   jax.ShapeDtypeStruct((n, D_MODEL // 2), U32),
                   jax.ShapeDtypeStruct((N_EXPERTS, n), F32)],
        compiler_params=_cparams(("arbitrary",), 48),
        name="outproj",
    )(x2d, oa, yr_in, ga, gb, g1, sc2, sh2, g2, n2g, *weights)


def _route_kernel(s_ref, b_ref, idx_ref, w_ref, rank_ref, cnt_ref, run_ref, tri_ref):
    step = pl.program_id(0)
    t = s_ref.shape[1]

    @pl.when(step == 0)
    def _():
        run_ref[...] = jnp.zeros_like(run_ref)
        r = lax.broadcasted_iota(I32, (t, t), 0)
        c = lax.broadcasted_iota(I32, (t, t), 1)
        tri_ref[...] = jnp.where(r < c, 1.0, 0.0).astype(BF16)

    s = s_ref[...]
    sel = s + b_ref[...]
    row_f = lax.broadcasted_iota(I32, (N_EXPERTS, t), 0).astype(F32)

    def first_argmax(vals, rows):
        m = jnp.max(vals, axis=0, keepdims=True)
        pos = jnp.min(jnp.where(vals == m, rows, float(N_EXPERTS)), axis=0, keepdims=True)
        return m, pos

    gscore = []
    group_row = lax.broadcasted_iota(I32, (GROUP_SIZE, t), 0).astype(F32)
    for g in range(N_GROUPS):
        rows = slice(g * GROUP_SIZE, (g + 1) * GROUP_SIZE)
        m1, p1 = first_argmax(sel[rows], group_row)
        m2 = jnp.max(jnp.where(group_row == p1, -jnp.inf, sel[rows]), axis=0, keepdims=True)
        gscore.append(m1 + m2)
    cand_parts = []
    for g in range(N_GROUPS):
        rows = slice(g * GROUP_SIZE, (g + 1) * GROUP_SIZE)
        beaten_by = jnp.zeros((1, t), F32)
        for o in range(N_GROUPS):
            if o == g:
                continue
            wins = (gscore[o] > gscore[g]) if o > g else (gscore[o] >= gscore[g])
            beaten_by = beaten_by + jnp.where(wins, 1.0, 0.0)
        cand_parts.append(jnp.where(beaten_by < TOPK_GROUPS, sel[rows], -jnp.inf))
    cand = jnp.concatenate(cand_parts, axis=0)

    picked = jnp.zeros((N_EXPERTS, t), F32)
    idx_rows, w_rows = [], []
    for _ in range(TOP_K):
        _, pos = first_argmax(cand, row_f)
        hit = row_f == pos
        w_rows.append(jnp.sum(jnp.where(hit, s, 0.0), axis=0, keepdims=True))
        idx_rows.append(pos)
        picked = jnp.where(hit, 1.0, picked)
        cand = jnp.where(hit, -jnp.inf, cand)
    w_sum = functools.reduce(jnp.add, w_rows)

    before = jnp.dot(picked.astype(BF16), tri_ref[...], preferred_element_type=F32) + run_ref[...]
    run_ref[...] = run_ref[...] + jnp.sum(picked, axis=1, keepdims=True)
    rank_rows = [jnp.sum(jnp.where(row_f == idx_rows[kk], before, 0.0), axis=0, keepdims=True)
                 for kk in range(TOP_K)]

    idx_ref[...] = jnp.concatenate(idx_rows, axis=0).astype(I32)
    w_ref[...] = jnp.concatenate([w / w_sum * ROUTED_SCALE for w in w_rows], axis=0)
    rank_ref[...] = jnp.concatenate(rank_rows, axis=0).astype(I32)

    @pl.when(step == pl.num_programs(0) - 1)
    def _():
        cnt_ref[...] = run_ref[...].astype(I32)


def _route(scores_t, b_col):
    n = scores_t.shape[1]
    t = ROUTE_TILE
    col = pl.BlockSpec((TOP_K, t), lambda i: (0, i))
    const = pl.BlockSpec((N_EXPERTS, t), lambda i: (0, 0))
    return pl.pallas_call(
        _route_kernel,
        grid=(n // t,),
        in_specs=[pl.BlockSpec((N_EXPERTS, t), lambda i: (0, i)), const],
        out_specs=[col, col, col, const],
        out_shape=[jax.ShapeDtypeStruct((TOP_K, n), I32),
                   jax.ShapeDtypeStruct((TOP_K, n), F32),
                   jax.ShapeDtypeStruct((TOP_K, n), I32),
                   jax.ShapeDtypeStruct((N_EXPERTS, t), I32)],
        scratch_shapes=[pltpu.VMEM((N_EXPERTS, t), F32), pltpu.VMEM((t, t), BF16)],
        compiler_params=_cparams(("arbitrary",), 32),
        name="route",
    )(scores_t, b_col)


def _dest_kernel(idx_ref, rank_ref, start_ref, dest_ref):
    t = idx_ref.shape[1]
    row = lax.broadcasted_iota(I32, (N_EXPERTS, t), 0)
    starts = start_ref[...]
    base = [jnp.sum(jnp.where(row == idx_ref[kk:kk + 1, :], starts, 0.0), axis=0, keepdims=True)
            for kk in range(TOP_K)]
    dest_ref[...] = jnp.concatenate(base, axis=0).astype(I32) + rank_ref[...]


def _dest(idx_t, rank_t, starts_col):
    n = idx_t.shape[1]
    t = ROUTE_TILE
    col = pl.BlockSpec((TOP_K, t), lambda i: (0, i))
    return pl.pallas_call(
        _dest_kernel,
        grid=(n // t,),
        in_specs=[col, col, pl.BlockSpec((N_EXPERTS, t), lambda i: (0, 0))],
        out_specs=col,
        out_shape=jax.ShapeDtypeStruct((TOP_K, n), I32),
        compiler_params=_cparams(("arbitrary",), 32),
        name="dest",
    )(idx_t, rank_t, starts_col)


def _row_copy(src, src_row, dst, dst_row, sem):
    return pltpu.make_async_copy(src.at[pl.ds(src_row, 1)], dst.at[pl.ds(dst_row, 1)], sem)


def _dispatch_kernel(dest_ref, vend_ref, x_ref, xs_ref, zero_ref, sem):
    t = x_ref.shape[0]

    @pl.when(pl.program_id(0) == 0)
    def _():
        zero_ref[...] = jnp.zeros_like(zero_ref)

        def fill(e, carry):
            row = pl.multiple_of(jnp.maximum(vend_ref[e] - FFN_QUANTUM, 0), FFN_QUANTUM)
            pltpu.make_async_copy(zero_ref, xs_ref.at[pl.ds(row, FFN_QUANTUM)], sem).start()
            return carry

        lax.fori_loop(0, N_EXPERTS, fill, 0)

        def fill_wait(e, carry):
            pltpu.make_async_copy(zero_ref, xs_ref.at[pl.ds(0, FFN_QUANTUM)], sem).wait()
            return carry

        lax.fori_loop(0, N_EXPERTS, fill_wait, 0)

    def issue(i, carry):
        for kk in range(TOP_K):
            _row_copy(x_ref, i, xs_ref, dest_ref[i * TOP_K + kk], sem).start(priority=kk % 2)
        return carry

    lax.fori_loop(0, t, issue, 0)

    def drain(i, carry):
        for kk in range(TOP_K):
            _row_copy(x_ref, 0, xs_ref, 0, sem).wait()
        return carry

    lax.fori_loop(0, t, drain, 0)


def _dispatch(dest_flat, pend, n2p, n_sorted_rows):
    n, width = n2p.shape
    t = MOVE_TILE
    return pl.pallas_call(
        _dispatch_kernel,
        grid=(n // t,),
        in_specs=[pl.BlockSpec((t * TOP_K,), lambda i: (i,), memory_space=pltpu.SMEM),
                  pl.BlockSpec(memory_space=pltpu.SMEM),
                  pl.BlockSpec((t, width), lambda i: (i, 0))],
        out_specs=pl.BlockSpec(memory_space=pl.ANY),
        out_shape=jax.ShapeDtypeStruct((n_sorted_rows, width), U32),
        scratch_shapes=[pltpu.VMEM((FFN_QUANTUM, width), U32), pltpu.SemaphoreType.DMA(())],
        compiler_params=_cparams(("arbitrary",), 32),
        name="dispatch",
    )(dest_flat, pend, n2p)


def _ffn_kernel(texp_ref, ntiles_ref, eslot_ref, enext_ref, nhalf_ref, xs_ref, wg_hbm, wu_hbm, wd_hbm,
                ys_ref, wg_buf, wu_buf, wd_buf, wgu_bf, wd_bf, sems):
    g = pl.program_id(0)

    def weight_copies(e, slot):
        return (pltpu.make_async_copy(wg_hbm.at[e], wg_buf.at[slot], sems.at[slot, 0]),
                pltpu.make_async_copy(wu_hbm.at[e], wu_buf.at[slot], sems.at[slot, 1]),
                pltpu.make_async_copy(wd_hbm.at[e], wd_buf.at[slot], sems.at[slot, 2]))

    @pl.when(g < ntiles_ref[0])
    def _():
        e = texp_ref[g]
        changed = jnp.logical_or(g == 0, texp_ref[jnp.maximum(g - 1, 0)] != e)

        @pl.when(changed)
        def _():
            slot = eslot_ref[e]

            @pl.when(g == 0)
            def _():
                for c in weight_copies(e, slot):
                    c.start()

            for c in weight_copies(e, slot):
                c.wait()
            nxt = enext_ref[e]

            @pl.when(nxt >= 0)
            def _():
                for c in weight_copies(nxt, 1 - slot):
                    c.start()

            wgu_bf[:, :D_EXPERT] = wg_buf[slot].astype(BF16)
            wgu_bf[:, D_EXPERT:] = wu_buf[slot].astype(BF16)
            wd_bf[...] = wd_buf[slot].astype(BF16)

        def expert_rows(rows):
            lo, hi = _unpack_bf16_pair(xs_ref[rows, :])
            x = jnp.concatenate([lo, hi], axis=1).astype(BF16)
            gu = jnp.dot(x, wgu_bf[...], preferred_element_type=F32)
            hid = (_silu(gu[:, :D_EXPERT]) * gu[:, D_EXPERT:]).astype(BF16)
            y = jnp.dot(hid, wd_bf[...], preferred_element_type=F32)
            half = D_MODEL // 2
            ys_ref[rows, :] = _pack_bf16_pair(y[:, :half], y[:, half:])

        for n_groups in range(1, FFN_TILE // FFN_QUANTUM + 1):

            @pl.when(nhalf_ref[g] == n_groups)
            def _(n_groups=n_groups):
                used = n_groups * FFN_QUANTUM
                for start in range(0, used, FFN_HALF):
                    expert_rows(slice(start, min(start + FFN_HALF, used)))
                if used < FFN_TILE:
                    ys_ref[used:, :] = jnp.zeros((FFN_TILE - used, ys_ref.shape[1]), U32)


def _ffn(tile_expert, n_tiles, expert_slot, expert_next, tile_halves, xs, w_gate, w_up, w_down):
    rows, width = xs.shape
    m = FFN_TILE
    max_tiles = tile_expert.shape[0]
    row_map = lambda g, te, nt, es, en, nh: (jnp.minimum(g, nt[0] - 1), 0)
    hbm = pl.BlockSpec(memory_space=pl.ANY)
    grid_spec = pltpu.PrefetchScalarGridSpec(
        num_scalar_prefetch=5,
        grid=(max_tiles,),
        in_specs=[pl.BlockSpec((m, width), row_map), hbm, hbm, hbm],
        out_specs=pl.BlockSpec((m, width), row_map),
        scratch_shapes=[pltpu.VMEM((2, D_MODEL, D_EXPERT), F32),
                        pltpu.VMEM((2, D_MODEL, D_EXPERT), F32),
                        pltpu.VMEM((2, D_EXPERT, D_MODEL), F32),
                        pltpu.VMEM((D_MODEL, 2 * D_EXPERT), BF16),
                        pltpu.VMEM((D_EXPERT, D_MODEL), BF16),
                        pltpu.SemaphoreType.DMA((2, 3))],
    )
    return pl.pallas_call(
        _ffn_kernel,
        grid_spec=grid_spec,
        out_shape=jax.ShapeDtypeStruct((rows, width), U32),
        compiler_params=_cparams(("arbitrary",), 32),
        name="ffn",
    )(tile_expert, n_tiles, expert_slot, expert_next, tile_halves, xs, w_gate, w_up, w_down)


def _ffn_plan(counts, n_assign):
    m = FFN_TILE
    max_tiles = n_assign // m + N_EXPERTS
    padded = ((counts + m - 1) // m) * m
    pend = jnp.cumsum(padded).astype(I32)
    pstart = pend - padded
    n_tiles = pend[-1:] // m
    g = jnp.minimum(jnp.arange(max_tiles, dtype=I32), n_tiles - 1)
    tile_expert = jnp.sum((pend[None, :] <= (g * m)[:, None]).astype(I32), axis=1)
    tile_expert = jnp.minimum(tile_expert, N_EXPERTS - 1)
    vend = pstart + ((counts + FFN_QUANTUM - 1) // FFN_QUANTUM) * FFN_QUANTUM
    own = tile_expert[:, None] == jnp.arange(N_EXPERTS, dtype=I32)[None, :]
    tile_vend = jnp.sum(jnp.where(own, vend[None, :], 0), axis=1)
    tile_halves = (jnp.clip(tile_vend - g * m, 0, m) // FFN_QUANTUM).astype(I32)
    used = counts > 0
    expert_slot = ((jnp.cumsum(used.astype(I32)) - 1) % 2).astype(I32)
    ids = jnp.where(used, jnp.arange(N_EXPERTS, dtype=I32), N_EXPERTS)
    first_used_from = lax.cummin(ids, axis=0, reverse=True)
    nxt = jnp.concatenate([first_used_from[1:], jnp.full((1,), N_EXPERTS, I32)])
    expert_next = jnp.where(nxt < N_EXPERTS, nxt, -1).astype(I32)
    return (tile_expert, n_tiles, expert_slot, expert_next, tile_halves, pstart, vend.astype(I32),
            max_tiles * m)


def _combine_kernel(dest_ref, w_ref, h_ref, g2_ref, nf_ref, ys_ref, y_ref, buf_ref, sem):
    t = h_ref.shape[0]

    def issue(i, carry):
        for kk in range(TOP_K):
            _row_copy(ys_ref, dest_ref[i * TOP_K + kk], buf_ref.at[kk], i, sem).start(priority=kk % 2)
        return carry

    lax.fori_loop(0, t, issue, 0)

    def drain(i, carry):
        for kk in range(TOP_K):
            _row_copy(ys_ref, 0, buf_ref.at[kk], 0, sem).wait()
        return carry

    lax.fori_loop(0, t, drain, 0)

    w = w_ref[...]
    acc_lo = jnp.zeros((t, D_MODEL // 2), F32)
    acc_hi = jnp.zeros((t, D_MODEL // 2), F32)
    for kk in range(TOP_K):
        lo, hi = _unpack_bf16_pair(buf_ref[kk])
        wk = w[:, kk:kk + 1]
        acc_lo = acc_lo + wk * lo
        acc_hi = acc_hi + wk * hi
    out = h_ref[...] + g2_ref[0] * jnp.concatenate([acc_lo, acc_hi], axis=1)
    y_ref[...] = out * lax.rsqrt(jnp.mean(out * out, axis=-1, keepdims=True) + EPS) * nf_ref[...]


def _combine(dest_flat, w, h2, g2, normf, ys, row_offset, tiles_per_batch):
    n = h2.shape[0]
    t = MOVE_TILE
    off = row_offset // t
    mod_rows = g2.shape[1]
    mod_tiles = max(ROW_TILE // t, 1) * tiles_per_batch if mod_rows == 1 else n // t
    return pl.pallas_call(
        _combine_kernel,
        grid=(n // t,),
        in_specs=[pl.BlockSpec((t * TOP_K,), lambda i: (i + off,), memory_space=pltpu.SMEM),
                  pl.BlockSpec((t, LANES), lambda i: (i + off, 0)),
                  pl.BlockSpec((t, D_MODEL), lambda i: (i, 0)),
                  pl.BlockSpec((1, mod_rows if mod_rows == 1 else t, D_MODEL),
                               (lambda i: (i // mod_tiles, 0, 0)) if mod_rows == 1
                               else (lambda i: (0, i, 0))),
                  pl.BlockSpec((1, D_MODEL), lambda i: (0, 0)),
                  pl.BlockSpec(memory_space=pl.ANY)],
        out_specs=pl.BlockSpec((t, D_MODEL), lambda i: (i, 0)),
        out_shape=jax.ShapeDtypeStruct((n, D_MODEL), F32),
        scratch_shapes=[pltpu.VMEM((TOP_K, t, D_MODEL // 2), U32), pltpu.SemaphoreType.DMA(())],
        compiler_params=_cparams(("arbitrary",), 32),
        name="combine",
    )(dest_flat, w, h2, g2, normf, ys)


def _rotary_tables(pos):
    half = KEY_DIM_R // 2
    inv_freq = ROPE_BASE ** (-jnp.arange(half, dtype=F32) / half)
    ang = pos[:, None] * inv_freq[None, :]
    cos = jnp.cos(ang)
    sin = jnp.sin(ang)
    cos_t = jnp.tile(jnp.concatenate([cos, cos], axis=1), (1, N_HEADS))
    sin_t = jnp.tile(jnp.concatenate([-sin, sin], axis=1), (1, N_HEADS))
    return cos_t.astype(F32), sin_t.astype(F32)


def _rel_bias_table(rel_bias, n_rows, n_cols, q_offset):
    heads = rel_bias.shape[0]
    n_diag = n_rows + n_cols - 1
    dist = q_offset + (n_rows - 1) - np.arange(n_diag)
    idx = np.clip(dist, -REL_CLIP, REL_CLIP) + REL_CLIP
    n_hi = int(np.sum(dist > REL_CLIP))
    n_lo = int(np.sum(dist < -REL_CLIP))
    mid = rel_bias[:, int(idx[n_diag - n_lo - 1]):int(idx[n_hi]) + 1][:, ::-1]
    diag = jnp.concatenate([jnp.broadcast_to(rel_bias[:, 2 * REL_CLIP:], (heads, n_hi)), mid,
                            jnp.broadcast_to(rel_bias[:, :1], (heads, n_lo))], axis=1)
    period = n_diag + 1
    v = jnp.roll(jnp.pad(diag, ((0, 0), (0, 1))), -(n_rows - 1), axis=1)
    skew = jnp.tile(v, (1, n_rows))[:, :n_rows * (period - 1)].reshape(heads, n_rows, period - 1)
    return skew[:, :, :n_cols].astype(F32)


def _prompt_bias(rel_bias):
    n_cols = ATT_QB + ATT_WINDOW
    r = np.arange(ATT_QB)[:, None]
    c = np.arange(n_cols)[None, :]
    band = c - (r // CHUNK) * CHUNK
    valid = (band >= 0) & (band < ATT_WINDOW + CHUNK)
    table = _rel_bias_table(rel_bias, ATT_QB, n_cols, ATT_WINDOW)
    return jnp.where(jnp.asarray(valid)[None], table, NEG_BIG)


def _sample_bias(rel_bias, t_new, cache_len):
    b = _rel_bias_table(rel_bias, t_new, cache_len + t_new, cache_len)
    return b[:, :, :cache_len], b[:, :, cache_len:]


def _mod_parts(mod, rows_each):
    parts = jnp.split(mod, 6, axis=-1)
    if rows_each == 1:
        return [p[:, None, :] for p in parts]
    return [jnp.repeat(p, rows_each, axis=0)[None] for p in parts]


def kernel(x_prompt, x_sample, cache_attn_k, cache_attn_v, state_ret, c_prompt, c_sample,
           norm1_g, norm2_g, w_ada, b_ada, w_in, rel_bias, w_o_attn, w_o_ret, w_out,
           w_router, b_router, w_exp_gate, w_exp_up, w_exp_down, w_sh_gate, w_sh_up, w_sh_down,
           normf_g):
    batch, seq, d = x_prompt.shape
    dec_batch, dec_seq, _ = x_sample.shape
    depth = w_in.shape[0]
    assert depth == 1 and d == D_MODEL
    assert seq % ROW_TILE == 0 and dec_batch * dec_seq == ROW_TILE and ROW_TILE == ATT_WINDOW
    cache_len = cache_attn_k.shape[2]
    n_p = batch * seq
    n_s = dec_batch * dec_seq
    tpb = seq // ROW_TILE
    l = 0

    bf = lambda a: a.astype(BF16)
    c_all = jnp.concatenate([c_prompt, c_sample], axis=0)
    pad = (-c_all.shape[0]) % 8
    c_all = jnp.pad(c_all, ((0, pad), (0, 0)))
    mod = _ada(c_all, bf(w_ada[l]), b_ada[l][None, :])
    mod_p = _mod_parts(mod[:batch], 1)
    mod_s = _mod_parts(mod[batch:batch + dec_batch], dec_seq)

    w_in_bf = bf(w_in[l])
    n1g = norm1_g[l][None, :]
    n2g = norm2_g[l][None, :]
    dense_w = [bf(w_o_attn[l]), bf(w_o_ret[l]), bf(w_out[l]), bf(w_router[l]).T,
               bf(w_sh_gate[l]), bf(w_sh_up[l]), bf(w_sh_down[l])]

    xp = x_prompt.reshape(n_p, d)
    xs_ = x_sample.reshape(n_s, d)
    cos_p, sin_p = _rotary_tables(jnp.arange(seq, dtype=F32))
    pos_s = PAST_LEN + jnp.arange(dec_seq, dtype=F32)
    cos_s, sin_s = _rotary_tables(jnp.tile(pos_s, dec_batch))

    (qa, ka, va, qr, kr, vr, gr, ga, gb, kv_p) = _inproj(
        xp, mod_p[1], mod_p[0], n1g, cos_p, sin_p, w_in_bf, tpb)
    oa = _attn_prompt(qa, ka, va, _prompt_bias(rel_bias[l]), batch, seq)
    zero_state = jnp.zeros((batch, N_HEADS, KEY_DIM_R, VAL_DIM_R), F32)
    yr_in, state_p = _retention(qr, kr, vr, gr, zero_state, batch, seq, RET_CHUNK)
    h_p, n2p_p, s_p = _outproj(xp, oa, yr_in, ga, gb, mod_p[2], mod_p[4], mod_p[3], mod_p[5], n2g,
                               dense_w, tpb)

    (qa_s, ka_s, va_s, qr_s, kr_s, vr_s, gr_s, ga_s, gb_s, kv_s) = _inproj(
        xs_, mod_s[1], mod_s[0], n1g, cos_s, sin_s, w_in_bf, 1)
    bias_c, bias_n = _sample_bias(rel_bias[l], dec_seq, cache_len)
    to_keys_minor = lambda c: jnp.transpose(c, (0, 1, 3, 4, 2))
    oa_s = _attn_sample(qa_s, ka_s, va_s, to_keys_minor(cache_attn_k), to_keys_minor(cache_attn_v),
                        bias_c, bias_n, dec_batch, dec_seq, cache_len)
    yr_in_s, state_s = _retention(qr_s, kr_s, vr_s, gr_s, state_ret[l], dec_batch, dec_seq, dec_seq)
    h_s, n2p_s, s_s = _outproj(xs_, oa_s, yr_in_s, ga_s, gb_s, mod_s[2], mod_s[4], mod_s[3], mod_s[5],
                               n2g, dense_w, 1)

    n2p = jnp.concatenate([n2p_p, n2p_s], axis=0)
    scores_t = jnp.concatenate([s_p, s_s], axis=1)
    lanes_of = lambda v: jnp.broadcast_to(v[:, None], (N_EXPERTS, ROUTE_TILE))
    idx_t, w_t, rank_t, counts = _route(scores_t, lanes_of(b_router[l]))
    (tile_expert, n_tiles, expert_slot, expert_next, tile_halves, pstart, vend,
     n_sorted_rows) = _ffn_plan(counts[:, 0], (n_p + n_s) * TOP_K)
    dest_t = _dest(idx_t, rank_t, lanes_of(pstart.astype(F32)))
    dest_flat = dest_t.T.reshape(-1)
    w_route = jnp.pad(w_t.T, ((0, 0), (0, LANES - TOP_K)))
    xs_sorted = _dispatch(dest_flat, vend, n2p, n_sorted_rows)
    ys_sorted = _ffn(tile_expert, n_tiles, expert_slot, expert_next, tile_halves, xs_sorted,
                     w_exp_gate[l], w_exp_up[l], w_exp_down[l])
    nf = normf_g[None, :]
    y_p = _combine(dest_flat, w_route, h_p, mod_p[5], nf, ys_sorted, 0, tpb)
    y_s = _combine(dest_flat, w_route, h_s, mod_s[5], nf, ys_sorted, n_p, 1)

    keep = min(ATT_WINDOW, seq)
    kv_p = kv_p.reshape(batch, ROW_TILE, 2, N_HEADS, HEAD_DIM_A)[:, ROW_TILE - keep:]
    kv_s = kv_s.reshape(dec_batch, dec_seq, 2, N_HEADS, HEAD_DIM_A)
    return (y_p.reshape(batch, seq, d), y_s.reshape(dec_batch, dec_seq, d),
            kv_p[:, :, 0][None], kv_p[:, :, 1][None], state_p[None],
            kv_s[:, :, 0][None], kv_s[:, :, 1][None], state_s[None])
```

```python
import functools

import numpy as np
import jax
import jax.numpy as jnp
from jax import lax
from jax.experimental import pallas as pl
from jax.experimental.pallas import tpu as pltpu
from jax.experimental.pallas import tpu_sc as plsc

F32 = jnp.float32
BF16 = jnp.bfloat16
I32 = jnp.int32
U32 = jnp.uint32

D_MODEL = 1024
PAST_LEN = 4096
CHUNK = 64
N_LEFT_CHUNKS = 8
ATT_WINDOW = N_LEFT_CHUNKS * CHUNK
N_HEADS = 8
HEAD_DIM_A = 64
D_ATT = N_HEADS * HEAD_DIM_A
REL_CLIP = 128
KEY_DIM_R = 64
VAL_DIM_R = 128
D_RET_K = N_HEADS * KEY_DIM_R
D_RET_V = N_HEADS * VAL_DIM_R
ROPE_BASE = 10000.0
N_EXPERTS = 256
TOP_K = 8
N_GROUPS = 8
GROUP_SIZE = N_EXPERTS // N_GROUPS
TOPK_GROUPS = 4
D_EXPERT = 256
ROUTED_SCALE = 2.5
EPS = 1e-6
IN_WIDTHS = (D_ATT, D_ATT, D_ATT, D_RET_K, D_RET_K, D_RET_V, D_RET_V, D_MODEL, D_MODEL)
IN_OFFS = tuple(int(v) for v in np.cumsum((0,) + IN_WIDTHS))
D_IN = IN_OFFS[-1]

NEG_BIG = -1e30
LANES = 128
V7X_VMEM_BYTES = 64 * 1024 * 1024
V7X_SC_CORES = 2
V7X_SC_SUBCORES = 16
SC_GATHER_ROWS = 64

ROW_TILE = 512
ATT_QB = 256
RET_CHUNK = 256
ROUTE_TILE = 512
MOVE_TILE = 256
FFN_QUANTUM = 128
FFN_HALF = 2 * FFN_QUANTUM
FFN_TILE = 2 * FFN_HALF


def _cparams(semantics, vmem_mb):
    return pltpu.CompilerParams(dimension_semantics=semantics,
                                vmem_limit_bytes=min(vmem_mb * 1024 * 1024, V7X_VMEM_BYTES - (6 << 20)))


def _silu(x):
    return x * jax.nn.sigmoid(x)


def _pack_bf16_pair(lo, hi):
    lo_b = pltpu.bitcast(lo.astype(BF16).astype(F32), U32) >> 16
    hi_b = pltpu.bitcast(hi.astype(BF16).astype(F32), U32) & jnp.uint32(0xFFFF0000)
    return lo_b | hi_b


def _unpack_bf16_pair(u):
    lo = pltpu.bitcast(u << 16, F32)
    hi = pltpu.bitcast(u & jnp.uint32(0xFFFF0000), F32)
    return lo, hi


def _ada_kernel(c_ref, w_ref, b_ref, o_ref):
    sc = _silu(c_ref[...]).astype(BF16)
    o_ref[...] = jnp.dot(sc, w_ref[...], preferred_element_type=F32) + b_ref[...]


def _ada(c_all, w_ada_bf, b_ada):
    rows = c_all.shape[0]
    n_out = w_ada_bf.shape[1]
    blk = D_MODEL
    return pl.pallas_call(
        _ada_kernel,
        grid=(n_out // blk,),
        in_specs=[pl.BlockSpec((rows, D_MODEL), lambda j: (0, 0)),
                  pl.BlockSpec((D_MODEL, blk), lambda j: (0, j)),
                  pl.BlockSpec((1, blk), lambda j: (0, j))],
        out_specs=pl.BlockSpec((rows, blk), lambda j: (0, j)),
        out_shape=jax.ShapeDtypeStruct((rows, n_out), F32),
        compiler_params=_cparams(("arbitrary",), 24),
        name="ada",
    )(c_all, w_ada_bf, b_ada)


def _inproj_kernel(x_ref, sc_ref, sh_ref, g_ref, cos_ref, sin_ref, w_ref,
                   qa_ref, ka_ref, va_ref, qr_ref, kr_ref, vr_ref, gr_ref, ga_ref, gb_ref,
                   kv_ref, *, tiles_per_batch):
    x = x_ref[...]
    xn = x * lax.rsqrt(jnp.mean(x * x, axis=-1, keepdims=True) + EPS) * g_ref[...]
    nb = (xn * (1.0 + sc_ref[0]) + sh_ref[0]).astype(BF16)

    def proj(seg):
        return jnp.dot(nb, w_ref[:, IN_OFFS[seg]:IN_OFFS[seg + 1]], preferred_element_type=F32)

    qa_ref[...] = proj(0).astype(BF16)
    ka = proj(1)
    va = proj(2)
    ka_ref[...] = ka.astype(BF16)
    va_ref[...] = va.astype(BF16)

    @pl.when(pl.program_id(0) % tiles_per_batch == tiles_per_batch - 1)
    def _():
        kv_ref[:, :D_ATT] = ka
        kv_ref[:, D_ATT:] = va

    cos = cos_ref[...]
    sin = sin_ref[...]
    first_half = (lax.broadcasted_iota(I32, (1, D_RET_K), 1) % KEY_DIM_R) < (KEY_DIM_R // 2)

    def rotary(t):
        partner = jnp.where(first_half, pltpu.roll(t, D_RET_K - KEY_DIM_R // 2, 1),
                            pltpu.roll(t, KEY_DIM_R // 2, 1))
        return t * cos + partner * sin

    qr_ref[...] = rotary(proj(3)).astype(BF16)
    kr_ref[...] = (rotary(proj(4)) * (KEY_DIM_R ** -0.5)).astype(BF16)
    vr_ref[...] = proj(5).astype(BF16)
    gr_ref[...] = proj(6).astype(BF16)
    ga_ref[...] = proj(7).astype(BF16)
    gb_ref[...] = proj(8).astype(BF16)


def _inproj(x2d, sc, sh, g, cos_t, sin_t, w_in_bf, tiles_per_batch):
    n = x2d.shape[0]
    tm = ROW_TILE
    n_tiles = n // tm
    n_batches = n_tiles // tiles_per_batch
    mod_rows = sc.shape[1]
    pos_tiles = cos_t.shape[0] // tm

    def row_spec(width):
        return pl.BlockSpec((tm, width), lambda i: (i, 0))

    mod_spec = pl.BlockSpec((1, mod_rows, D_MODEL), lambda i: (i // tiles_per_batch, 0, 0))
    pos_spec = pl.BlockSpec((tm, D_RET_K), lambda i: (i % pos_tiles, 0))
    out_widths = (D_ATT, D_ATT, D_ATT, D_RET_K, D_RET_K, D_RET_V, D_RET_V, D_MODEL, D_MODEL)
    out_shape = [jax.ShapeDtypeStruct((n, w), BF16) for w in out_widths]
    out_shape.append(jax.ShapeDtypeStruct((n_batches * tm, 2 * D_ATT), F32))
    out_specs = [row_spec(w) for w in out_widths]
    out_specs.append(pl.BlockSpec((tm, 2 * D_ATT), lambda i: (i // tiles_per_batch, 0)))
    return pl.pallas_call(
        functools.partial(_inproj_kernel, tiles_per_batch=tiles_per_batch),
        grid=(n_tiles,),
        in_specs=[row_spec(D_MODEL), mod_spec, mod_spec,
                  pl.BlockSpec((1, D_MODEL), lambda i: (0, 0)),
                  pos_spec, pos_spec,
                  pl.BlockSpec((D_MODEL, D_IN), lambda i: (0, 0))],
        out_specs=out_specs,
        out_shape=out_shape,
        compiler_params=_cparams(("arbitrary",), 56),
        name="inproj",
    )(x2d, sc, sh, g, cos_t, sin_t, w_in_bf)


def _softmax_pv(s, v_parts):
    m = functools.reduce(jnp.maximum, [jnp.max(t, axis=-1, keepdims=True) for t in s])
    ps = [jnp.exp(t - m) for t in s]
    l = functools.reduce(jnp.add, [jnp.sum(p, axis=-1, keepdims=True) for p in ps])
    o = functools.reduce(jnp.add, [jnp.dot(p.astype(BF16), v, preferred_element_type=F32)
                                   for p, v in zip(ps, v_parts)])
    return o / l


def _attn_prompt_kernel(q_ref, k0_ref, k1_ref, k2_ref, v0_ref, v1_ref, v2_ref, bias_ref, o_ref):
    j = pl.program_id(1)
    q = q_ref[...]
    k = jnp.concatenate([k0_ref[...], k1_ref[...], k2_ref[...]], axis=0)
    v = jnp.concatenate([v0_ref[...], v1_ref[...], v2_ref[...]], axis=0)
    n_keys = k.shape[0]
    key_block = lax.broadcasted_iota(I32, (1, n_keys), 1) // ATT_QB
    before_start = jnp.where(key_block < 2 - j, NEG_BIG, 0.0)
    outs = []
    for h in range(N_HEADS):
        sl = slice(h * HEAD_DIM_A, (h + 1) * HEAD_DIM_A)
        qh = (q[:, sl].astype(F32) * (HEAD_DIM_A ** -0.5)).astype(BF16)
        s = lax.dot_general(qh, k[:, sl], (((1,), (1,)), ((), ())), preferred_element_type=F32)
        s = s + bias_ref[h] + before_start
        outs.append(_softmax_pv([s], [v[:, sl]]))
    o_ref[...] = jnp.concatenate(outs, axis=1).astype(BF16)


def _attn_prompt(q, k, v, bias_full, batch, seq):
    qb = ATT_QB
    nq = seq // qb

    def q_map(b, j):
        return (b * nq + j, 0)

    def kv_map(back):
        return lambda b, j: (b * nq + jnp.maximum(j - back, 0), 0)

    blk = lambda m: pl.BlockSpec((qb, D_ATT), m)
    return pl.pallas_call(
        _attn_prompt_kernel,
        grid=(batch, nq),
        in_specs=[blk(q_map), blk(kv_map(2)), blk(kv_map(1)), blk(kv_map(0)),
                  blk(kv_map(2)), blk(kv_map(1)), blk(kv_map(0)),
                  pl.BlockSpec(bias_full.shape, lambda b, j: (0, 0, 0))],
        out_specs=blk(q_map),
        out_shape=jax.ShapeDtypeStruct((batch * seq, D_ATT), BF16),
        compiler_params=_cparams(("parallel", "arbitrary"), 40),
        name="attn_prompt",
    )(q, k, k, k, v, v, v, bias_full)


SAMPLE_ATT_BATCHES = 2


def _attn_sample_kernel(q_ref, kn_ref, vn_ref, ck_ref, cv_ref, bc_ref, bn_ref, o_ref, *, t_new):
    nt = (((1,), (1,)), ((), ()))
    for b in range(SAMPLE_ATT_BATCHES):
        rows = slice(b * t_new, (b + 1) * t_new)
        q = q_ref[rows, :]
        kn = kn_ref[rows, :]
        vn = vn_ref[rows, :]
        outs = []
        for h in range(N_HEADS):
            sl = slice(h * HEAD_DIM_A, (h + 1) * HEAD_DIM_A)
            qh = (q[:, sl].astype(F32) * (HEAD_DIM_A ** -0.5)).astype(BF16)
            kc_t = ck_ref[b, h].astype(BF16)
            vc_t = cv_ref[b, h].astype(BF16)
            s_c = jnp.dot(qh, kc_t, preferred_element_type=F32) + bc_ref[h]
            s_n = lax.dot_general(qh, kn[:, sl], nt, preferred_element_type=F32) + bn_ref[h]
            m = jnp.maximum(jnp.max(s_c, axis=-1, keepdims=True), jnp.max(s_n, axis=-1, keepdims=True))
            p_c = jnp.exp(s_c - m)
            p_n = jnp.exp(s_n - m)
            l = jnp.sum(p_c, axis=-1, keepdims=True) + jnp.sum(p_n, axis=-1, keepdims=True)
            o = (lax.dot_general(p_c.astype(BF16), vc_t, nt, preferred_element_type=F32)
                 + jnp.dot(p_n.astype(BF16), vn[:, sl], preferred_element_type=F32))
            outs.append(o / l)
        o_ref[rows, :] = jnp.concatenate(outs, axis=1).astype(BF16)


def _attn_sample(q, k, v, cache_k_t, cache_v_t, bias_cache, bias_new, batch, t_new, cache_len):
    nb = SAMPLE_ATT_BATCHES
    blk = pl.BlockSpec((nb * t_new, D_ATT), lambda b: (b, 0))
    cblk = pl.BlockSpec((None, nb, N_HEADS, HEAD_DIM_A, cache_len), lambda b: (0, b, 0, 0, 0))
    return pl.pallas_call(
        functools.partial(_attn_sample_kernel, t_new=t_new),
        grid=(batch // nb,),
        in_specs=[blk, blk, blk, cblk, cblk,
                  pl.BlockSpec(bias_cache.shape, lambda b: (0, 0, 0)),
                  pl.BlockSpec(bias_new.shape, lambda b: (0, 0, 0))],
        out_specs=blk,
        out_shape=jax.ShapeDtypeStruct((batch * t_new, D_ATT), BF16),
        compiler_params=_cparams(("arbitrary",), 40),
        name="attn_sample",
    )(q, k, v, cache_k_t, cache_v_t, bias_cache, bias_new)


def _ret_kernel(q_ref, k_ref, v_ref, g_ref, s0_ref, dmask_ref, qdec_ref, kdec_ref, sdec_ref,
                y_ref, sout_ref, state_ref):
    c = pl.program_id(1)

    @pl.when(c == 0)
    def _():
        state_ref[...] = s0_ref[0]

    q = q_ref[...]
    k = k_ref[...]
    v = v_ref[...]
    g = g_ref[...]
    outs = []
    for h in range(N_HEADS):
        ks = slice(h * KEY_DIM_R, (h + 1) * KEY_DIM_R)
        vs = slice(h * VAL_DIM_R, (h + 1) * VAL_DIM_R)
        qh, kh, vh = q[:, ks], k[:, ks], v[:, vs]
        scores = lax.dot_general(qh, kh, (((1,), (1,)), ((), ())), preferred_element_type=F32)
        inner = jnp.dot((scores * dmask_ref[h]).astype(BF16), vh, preferred_element_type=F32)
        state = state_ref[h]
        cross = jnp.dot(qh, state.astype(BF16), preferred_element_type=F32) * qdec_ref[h]
        o = inner + cross
        v_dec = (vh.astype(F32) * kdec_ref[h]).astype(BF16)
        state_ref[h] = sdec_ref[h] * state + lax.dot_general(
            kh, v_dec, (((0,), (0,)), ((), ())), preferred_element_type=F32)
        on = o * lax.rsqrt(jnp.mean(o * o, axis=-1, keepdims=True) + EPS)
        outs.append(on * _silu(g[:, vs].astype(F32)))
    y_ref[...] = jnp.concatenate(outs, axis=1).astype(BF16)

    @pl.when(c == pl.num_programs(1) - 1)
    def _():
        sout_ref[0] = state_ref[...]


def _ret_tables(chunk):
    log_g = jnp.log(1.0 - jnp.exp2(-5.0 - jnp.arange(N_HEADS, dtype=F32)))
    i = jnp.arange(chunk, dtype=F32)
    diff = i[:, None] - i[None, :]
    dmask = jnp.where(diff >= 0, jnp.exp(log_g[:, None, None] * jnp.maximum(diff, 0.0)), 0.0)
    qdec = jnp.exp(log_g[:, None] * (i + 1.0))
    kdec = jnp.exp(log_g[:, None] * (chunk - 1.0 - i))
    sdec = jnp.exp(log_g * chunk)
    bc = lambda t: jnp.broadcast_to(t[:, :, None], (N_HEADS, t.shape[1], VAL_DIM_R)).astype(F32)
    sdec_t = jnp.broadcast_to(sdec[:, None, None], (N_HEADS, 1, VAL_DIM_R)).astype(F32)
    return dmask.astype(F32), bc(qdec), bc(kdec), sdec_t


def _retention(q, k, v, gate, state0, batch, seq, chunk):
    nc = seq // chunk
    dmask, qdec, kdec, sdec = _ret_tables(chunk)
    row = lambda w: pl.BlockSpec((chunk, w), lambda b, c: (b * nc + c, 0))
    const = lambda a: pl.BlockSpec(a.shape, lambda b, c: (0,) * a.ndim)
    st_spec = pl.BlockSpec((1, N_HEADS, KEY_DIM_R, VAL_DIM_R), lambda b, c: (b, 0, 0, 0))
    return pl.pallas_call(
        _ret_kernel,
        grid=(batch, nc),
        in_specs=[row(D_RET_K), row(D_RET_K), row(D_RET_V), row(D_RET_V), st_spec,
                  const(dmask), const(qdec), const(kdec), const(sdec)],
        out_specs=[row(D_RET_V), st_spec],
        out_shape=[jax.ShapeDtypeStruct((batch * seq, D_RET_V), BF16),
                   jax.ShapeDtypeStruct((batch, N_HEADS, KEY_DIM_R, VAL_DIM_R), F32)],
        scratch_shapes=[pltpu.VMEM((N_HEADS, KEY_DIM_R, VAL_DIM_R), F32)],
        compiler_params=_cparams(("parallel", "arbitrary"), 32),
        name="retention",
    )(q, k, v, gate, state0, dmask, qdec, kdec, sdec)


def _outproj_kernel(x_ref, oa_ref, yr_ref, ga_ref, gb_ref, g1_ref, sc2_ref, sh2_ref, g2_ref, n2g_ref,
                    woa_ref, wor_ref, wout_ref, wrt_ref, wsg_ref, wsu_ref, wsd_ref,
                    h_ref, n2p_ref, s_ref):
    ya = jnp.dot(oa_ref[...], woa_ref[...], preferred_element_type=F32)
    yr = jnp.dot(yr_ref[...], wor_ref[...], preferred_element_type=F32)
    merged = (jax.nn.sigmoid(ga_ref[...].astype(F32)) * ya
              + jax.nn.sigmoid(gb_ref[...].astype(F32)) * yr)
    mix = jnp.dot(merged.astype(BF16), wout_ref[...], preferred_element_type=F32)
    h = x_ref[...] + g1_ref[0] * mix
    hn = h * lax.rsqrt(jnp.mean(h * h, axis=-1, keepdims=True) + EPS) * n2g_ref[...]
    n2 = hn * (1.0 + sc2_ref[0]) + sh2_ref[0]
    n2b = n2.astype(BF16)
    s_ref[...] = jax.nn.sigmoid(lax.dot_general(wrt_ref[...], n2b, (((1,), (1,)), ((), ())),
                                                preferred_element_type=F32))
    hid = _silu(jnp.dot(n2b, wsg_ref[...], preferred_element_type=F32)) * jnp.dot(
        n2b, wsu_ref[...], preferred_element_type=F32)
    shared = jnp.dot(hid.astype(BF16), wsd_ref[...], preferred_element_type=F32)
    h_ref[...] = h + g2_ref[0] * shared
    half = D_MODEL // 2
    n2p_ref[...] = _pack_bf16_pair(n2[:, :half], n2[:, half:])


def _outproj(x2d, oa, yr_in, ga, gb, g1, sc2, sh2, g2, n2g, weights, tiles_per_batch):
    n = x2d.shape[0]
    tm = ROW_TILE
    mod_rows = g1.shape[1]
    row = lambda w: pl.BlockSpec((tm, w), lambda i: (i, 0))
    mod_spec = pl.BlockSpec((1, mod_rows, D_MODEL), lambda i: (i // tiles_per_batch, 0, 0))
    const = lambda a: pl.BlockSpec(a.shape, lambda i: (0,) * a.ndim)
    return pl.pallas_call(
        _outproj_kernel,
        grid=(n // tm,),
        in_specs=[row(D_MODEL), row(D_ATT), row(D_RET_V), row(D_MODEL), row(D_MODEL),
                  mod_spec, mod_spec, mod_spec, mod_spec, const(n2g)] + [const(w) for w in weights],
        out_specs=[row(D_MODEL), row(D_MODEL // 2), pl.BlockSpec((N_EXPERTS, tm), lambda i: (0, i))],
        out_shape=[jax.ShapeDtypeStruct((n, D_MODEL), F32),
                   jax.ShapeDtypeStruct((n, D_MODEL // 2), U32),
                   jax.ShapeDtypeStruct((N_EXPERTS, n), F32)],
        compiler_params=_cparams(("arbitrary",), 48),
        name="outproj",
    )(x2d, oa, yr_in, ga, gb, g1, sc2, sh2, g2, n2g, *weights)


def _route_kernel(s_ref, b_ref, idx_ref, w_ref, rank_ref, cnt_ref, run_ref, tri_ref):
    step = pl.program_id(0)
    t = s_ref.shape[1]

    @pl.when(step == 0)
    def _():
        run_ref[...] = jnp.zeros_like(run_ref)
        r = lax.broadcasted_iota(I32, (t, t), 0)
        c = lax.broadcasted_iota(I32, (t, t), 1)
        tri_ref[...] = jnp.where(r < c, 1.0, 0.0).astype(BF16)

    s = s_ref[...]
    sel = s + b_ref[...]
    row_f = lax.broadcasted_iota(I32, (N_EXPERTS, t), 0).astype(F32)

    def first_argmax(vals, rows):
        m = jnp.max(vals, axis=0, keepdims=True)
        pos = jnp.min(jnp.where(vals == m, rows, float(N_EXPERTS)), axis=0, keepdims=True)
        return m, pos

    gscore = []
    group_row = lax.broadcasted_iota(I32, (GROUP_SIZE, t), 0).astype(F32)
    for g in range(N_GROUPS):
        rows = slice(g * GROUP_SIZE, (g + 1) * GROUP_SIZE)
        m1, p1 = first_argmax(sel[rows], group_row)
        m2 = jnp.max(jnp.where(group_row == p1, -jnp.inf, sel[rows]), axis=0, keepdims=True)
        gscore.append(m1 + m2)
    cand_parts = []
    for g in range(N_GROUPS):
        rows = slice(g * GROUP_SIZE, (g + 1) * GROUP_SIZE)
        beaten_by = jnp.zeros((1, t), F32)
        for o in range(N_GROUPS):
            if o == g:
                continue
            wins = (gscore[o] > gscore[g]) if o > g else (gscore[o] >= gscore[g])
            beaten_by = beaten_by + jnp.where(wins, 1.0, 0.0)
        cand_parts.append(jnp.where(beaten_by < TOPK_GROUPS, sel[rows], -jnp.inf))
    cand = jnp.concatenate(cand_parts, axis=0)

    picked = jnp.zeros((N_EXPERTS, t), F32)
    idx_rows, w_rows = [], []
    for _ in range(TOP_K):
        _, pos = first_argmax(cand, row_f)
        hit = row_f == pos
        w_rows.append(jnp.sum(jnp.where(hit, s, 0.0), axis=0, keepdims=True))
        idx_rows.append(pos)
        picked = jnp.where(hit, 1.0, picked)
        cand = jnp.where(hit, -jnp.inf, cand)
    w_sum = functools.reduce(jnp.add, w_rows)

    before = jnp.dot(picked.astype(BF16), tri_ref[...], preferred_element_type=F32) + run_ref[...]
    run_ref[...] = run_ref[...] + jnp.sum(picked, axis=1, keepdims=True)
    rank_rows = [jnp.sum(jnp.where(row_f == idx_rows[kk], before, 0.0), axis=0, keepdims=True)
                 for kk in range(TOP_K)]

    idx_ref[...] = jnp.concatenate(idx_rows, axis=0).astype(I32)
    w_ref[...] = jnp.concatenate([w / w_sum * ROUTED_SCALE for w in w_rows], axis=0)
    rank_ref[...] = jnp.concatenate(rank_rows, axis=0).astype(I32)

    @pl.when(step == pl.num_programs(0) - 1)
    def _():
        cnt_ref[...] = run_ref[...].astype(I32)


def _route(scores_t, b_col):
    n = scores_t.shape[1]
    t = ROUTE_TILE
    col = pl.BlockSpec((TOP_K, t), lambda i: (0, i))
    const = pl.BlockSpec((N_EXPERTS, t), lambda i: (0, 0))
    return pl.pallas_call(
        _route_kernel,
        grid=(n // t,),
        in_specs=[pl.BlockSpec((N_EXPERTS, t), lambda i: (0, i)), const],
        out_specs=[col, col, col, const],
        out_shape=[jax.ShapeDtypeStruct((TOP_K, n), I32),
                   jax.ShapeDtypeStruct((TOP_K, n), F32),
                   jax.ShapeDtypeStruct((TOP_K, n), I32),
                   jax.ShapeDtypeStruct((N_EXPERTS, t), I32)],
        scratch_shapes=[pltpu.VMEM((N_EXPERTS, t), F32), pltpu.VMEM((t, t), BF16)],
        compiler_params=_cparams(("arbitrary",), 32),
        name="route",
    )(scores_t, b_col)


def _dest_kernel(idx_ref, rank_ref, start_ref, dest_ref):
    t = idx_ref.shape[1]
    row = lax.broadcasted_iota(I32, (N_EXPERTS, t), 0)
    starts = start_ref[...]
    base = [jnp.sum(jnp.where(row == idx_ref[kk:kk + 1, :], starts, 0.0), axis=0, keepdims=True)
            for kk in range(TOP_K)]
    dest_ref[...] = jnp.concatenate(base, axis=0).astype(I32) + rank_ref[...]


def _dest(idx_t, rank_t, starts_col):
    n = idx_t.shape[1]
    t = ROUTE_TILE
    col = pl.BlockSpec((TOP_K, t), lambda i: (0, i))
    return pl.pallas_call(
        _dest_kernel,
        grid=(n // t,),
        in_specs=[col, col, pl.BlockSpec((N_EXPERTS, t), lambda i: (0, 0))],
        out_specs=col,
        out_shape=jax.ShapeDtypeStruct((TOP_K, n), I32),
        compiler_params=_cparams(("arbitrary",), 32),
        name="dest",
    )(idx_t, rank_t, starts_col)


def _row_copy(src, src_row, dst, dst_row, sem):
    return pltpu.make_async_copy(src.at[pl.ds(src_row, 1)], dst.at[pl.ds(dst_row, 1)], sem)


def _dispatch_kernel(dest_ref, vend_ref, x_ref, xs_ref, zero_ref, sem):
    t = x_ref.shape[0]

    @pl.when(pl.program_id(0) == 0)
    def _():
        zero_ref[...] = jnp.zeros_like(zero_ref)

        def fill(e, carry):
            row = pl.multiple_of(jnp.maximum(vend_ref[e] - FFN_QUANTUM, 0), FFN_QUANTUM)
            pltpu.make_async_copy(zero_ref, xs_ref.at[pl.ds(row, FFN_QUANTUM)], sem).start()
            return carry

        lax.fori_loop(0, N_EXPERTS, fill, 0)

        def fill_wait(e, carry):
            pltpu.make_async_copy(zero_ref, xs_ref.at[pl.ds(0, FFN_QUANTUM)], sem).wait()
            return carry

        lax.fori_loop(0, N_EXPERTS, fill_wait, 0)

    def issue(i, carry):
        for kk in range(TOP_K):
            _row_copy(x_ref, i, xs_ref, dest_ref[i * TOP_K + kk], sem).start(priority=kk % 2)
        return carry

    lax.fori_loop(0, t, issue, 0)

    def drain(i, carry):
        for kk in range(TOP_K):
            _row_copy(x_ref, 0, xs_ref, 0, sem).wait()
        return carry

    lax.fori_loop(0, t, drain, 0)


def _dispatch(dest_flat, pend, n2p, n_sorted_rows):
    n, width = n2p.shape
    t = MOVE_TILE
    return pl.pallas_call(
        _dispatch_kernel,
        grid=(n // t,),
        in_specs=[pl.BlockSpec((t * TOP_K,), lambda i: (i,), memory_space=pltpu.SMEM),
                  pl.BlockSpec(memory_space=pltpu.SMEM),
                  pl.BlockSpec((t, width), lambda i: (i, 0))],
        out_specs=pl.BlockSpec(memory_space=pl.ANY),
        out_shape=jax.ShapeDtypeStruct((n_sorted_rows, width), U32),
        scratch_shapes=[pltpu.VMEM((FFN_QUANTUM, width), U32), pltpu.SemaphoreType.DMA(())],
        compiler_params=_cparams(("arbitrary",), 32),
        name="dispatch",
    )(dest_flat, pend, n2p)


def _ffn_kernel(texp_ref, ntiles_ref, eslot_ref, enext_ref, nhalf_ref, xs_ref, wg_hbm, wu_hbm, wd_hbm,
                ys_ref, wg_buf, wu_buf, wd_buf, wgu_bf, wd_bf, sems):
    g = pl.program_id(0)

    def weight_copies(e, slot):
        return (pltpu.make_async_copy(wg_hbm.at[e], wg_buf.at[slot], sems.at[slot, 0]),
                pltpu.make_async_copy(wu_hbm.at[e], wu_buf.at[slot], sems.at[slot, 1]),
                pltpu.make_async_copy(wd_hbm.at[e], wd_buf.at[slot], sems.at[slot, 2]))

    @pl.when(g < ntiles_ref[0])
    def _():
        e = texp_ref[g]
        changed = jnp.logical_or(g == 0, texp_ref[jnp.maximum(g - 1, 0)] != e)

        @pl.when(changed)
        def _():
            slot = eslot_ref[e]

            @pl.when(g == 0)
            def _():
                for c in weight_copies(e, slot):
                    c.start()

            for c in weight_copies(e, slot):
                c.wait()
            nxt = enext_ref[e]

            @pl.when(nxt >= 0)
            def _():
                for c in weight_copies(nxt, 1 - slot):
                    c.start()

            wgu_bf[:, :D_EXPERT] = wg_buf[slot].astype(BF16)
            wgu_bf[:, D_EXPERT:] = wu_buf[slot].astype(BF16)
            wd_bf[...] = wd_buf[slot].astype(BF16)

        def expert_rows(rows):
            lo, hi = _unpack_bf16_pair(xs_ref[rows, :])
            x = jnp.concatenate([lo, hi], axis=1).astype(BF16)
            gu = jnp.dot(x, wgu_bf[...], preferred_element_type=F32)
            hid = (_silu(gu[:, :D_EXPERT]) * gu[:, D_EXPERT:]).astype(BF16)
            y = jnp.dot(hid, wd_bf[...], preferred_element_type=F32)
            half = D_MODEL // 2
            ys_ref[rows, :] = _pack_bf16_pair(y[:, :half], y[:, half:])

        for n_groups in range(1, FFN_TILE // FFN_QUANTUM + 1):

            @pl.when(nhalf_ref[g] == n_groups)
            def _(n_groups=n_groups):
                used = n_groups * FFN_QUANTUM
                for start in range(0, used, FFN_HALF):
                    expert_rows(slice(start, min(start + FFN_HALF, used)))
                if used < FFN_TILE:
                    ys_ref[used:, :] = jnp.zeros((FFN_TILE - used, ys_ref.shape[1]), U32)


def _ffn(tile_expert, n_tiles, expert_slot, expert_next, tile_halves, xs, w_gate, w_up, w_down):
    rows, width = xs.shape
    m = FFN_TILE
    max_tiles = tile_expert.shape[0]
    row_map = lambda g, te, nt, es, en, nh: (jnp.minimum(g, nt[0] - 1), 0)
    hbm = pl.BlockSpec(memory_space=pl.ANY)
    grid_spec = pltpu.PrefetchScalarGridSpec(
        num_scalar_prefetch=5,
        grid=(max_tiles,),
        in_specs=[pl.BlockSpec((m, width), row_map), hbm, hbm, hbm],
        out_specs=pl.BlockSpec((m, width), row_map),
        scratch_shapes=[pltpu.VMEM((2, D_MODEL, D_EXPERT), F32),
                        pltpu.VMEM((2, D_MODEL, D_EXPERT), F32),
                        pltpu.VMEM((2, D_EXPERT, D_MODEL), F32),
                        pltpu.VMEM((D_MODEL, 2 * D_EXPERT), BF16),
                        pltpu.VMEM((D_EXPERT, D_MODEL), BF16),
                        pltpu.SemaphoreType.DMA((2, 3))],
    )
    return pl.pallas_call(
        _ffn_kernel,
        grid_spec=grid_spec,
        out_shape=jax.ShapeDtypeStruct((rows, width), U32),
        compiler_params=_cparams(("arbitrary",), 32),
        name="ffn",
    )(tile_expert, n_tiles, expert_slot, expert_next, tile_halves, xs, w_gate, w_up, w_down)


def _ffn_plan(counts, n_assign):
    m = FFN_TILE
    max_tiles = n_assign // m + N_EXPERTS
    padded = ((counts + m - 1) // m) * m
    pend = jnp.cumsum(padded).astype(I32)
    pstart = pend - padded
    n_tiles = pend[-1:] // m
    g = jnp.minimum(jnp.arange(max_tiles, dtype=I32), n_tiles - 1)
    tile_expert = jnp.sum((pend[None, :] <= (g * m)[:, None]).astype(I32), axis=1)
    tile_expert = jnp.minimum(tile_expert, N_EXPERTS - 1)
    vend = pstart + ((counts + FFN_QUANTUM - 1) // FFN_QUANTUM) * FFN_QUANTUM
    own = tile_expert[:, None] == jnp.arange(N_EXPERTS, dtype=I32)[None, :]
    tile_vend = jnp.sum(jnp.where(own, vend[None, :], 0), axis=1)
    tile_halves = (jnp.clip(tile_vend - g * m, 0, m) // FFN_QUANTUM).astype(I32)
    used = counts > 0
    expert_slot = ((jnp.cumsum(used.astype(I32)) - 1) % 2).astype(I32)
    ids = jnp.where(used, jnp.arange(N_EXPERTS, dtype=I32), N_EXPERTS)
    first_used_from = lax.cummin(ids, axis=0, reverse=True)
    nxt = jnp.concatenate([first_used_from[1:], jnp.full((1,), N_EXPERTS, I32)])
    expert_next = jnp.where(nxt < N_EXPERTS, nxt, -1).astype(I32)
    return (tile_expert, n_tiles, expert_slot, expert_next, tile_halves, pstart, vend.astype(I32),
            max_tiles * m)


def _sc_gather_rows(table, idx):
    n_idx = idx.shape[0]
    width = table.shape[1]
    n_workers = V7X_SC_CORES * V7X_SC_SUBCORES
    per_worker = n_idx // n_workers
    n_chunks = per_worker // SC_GATHER_ROWS
    assert per_worker * n_workers == n_idx and n_chunks * SC_GATHER_ROWS == per_worker and n_chunks % 2 == 0
    mesh = plsc.VectorSubcoreMesh(core_axis_name="core", subcore_axis_name="subcore",
                                  num_cores=V7X_SC_CORES, num_subcores=V7X_SC_SUBCORES)

    def body(table_hbm, idx_hbm, out_hbm, idx_v, rows_v, gather_sem, write_sem):
        worker = lax.axis_index("subcore") * V7X_SC_CORES + lax.axis_index("core")
        base = worker * per_worker
        pltpu.sync_copy(idx_hbm.at[pl.ds(base, per_worker)], idx_v)

        def gather(c, b):
            off = pl.multiple_of(c * SC_GATHER_ROWS, SC_GATHER_ROWS)
            return pltpu.make_async_copy(table_hbm.at[idx_v.at[pl.ds(off, SC_GATHER_ROWS)]],
                                         rows_v.at[b], gather_sem.at[b])

        def write(c, b):
            off = pl.multiple_of(c * SC_GATHER_ROWS, SC_GATHER_ROWS)
            return pltpu.make_async_copy(rows_v.at[b], out_hbm.at[pl.ds(base + off, SC_GATHER_ROWS)],
                                         write_sem.at[b])

        gather(0, 0).start()

        @pl.loop(0, n_chunks, step=2)
        def _(c0):
            for b in range(2):
                c = c0 + b
                gather(c, b).wait()

                @pl.when(c >= 1)
                def _():
                    write(c - 1, 1 - b).wait()

                @pl.when(c + 1 < n_chunks)
                def _():
                    gather(c + 1, 1 - b).start()

                write(c, b).start()

        write(n_chunks - 1, (n_chunks - 1) % 2).wait()

    return pl.kernel(
        body, mesh=mesh,
        out_type=jax.ShapeDtypeStruct((n_idx, width), table.dtype),
        scratch_types=[pltpu.VMEM((per_worker,), I32),
                       pltpu.VMEM((2, SC_GATHER_ROWS, width), table.dtype),
                       pltpu.SemaphoreType.DMA((2,)),
                       pltpu.SemaphoreType.DMA((2,))],
        name="sc_gather_rows",
    )(table, idx)


def _combine_kernel(w_ref, h_ref, g2_ref, nf_ref, yk_ref, y_ref):
    t = h_ref.shape[0]
    w = w_ref[...]
    acc_lo = jnp.zeros((t, D_MODEL // 2), F32)
    acc_hi = jnp.zeros((t, D_MODEL // 2), F32)
    for kk in range(TOP_K):
        lo, hi = _unpack_bf16_pair(yk_ref[kk])
        wk = w[:, kk:kk + 1]
        acc_lo = acc_lo + wk * lo
        acc_hi = acc_hi + wk * hi
    out = h_ref[...] + g2_ref[0] * jnp.concatenate([acc_lo, acc_hi], axis=1)
    y_ref[...] = out * lax.rsqrt(jnp.mean(out * out, axis=-1, keepdims=True) + EPS) * nf_ref[...]


def _combine(w, h2, g2, normf, y_by_k, row_offset, tiles_per_batch):
    n = h2.shape[0]
    t = MOVE_TILE
    off = row_offset // t
    mod_rows = g2.shape[1]
    mod_tiles = max(ROW_TILE // t, 1) * tiles_per_batch if mod_rows == 1 else n // t
    return pl.pallas_call(
        _combine_kernel,
        grid=(n // t,),
        in_specs=[pl.BlockSpec((t, LANES), lambda i: (i + off, 0)),
                  pl.BlockSpec((t, D_MODEL), lambda i: (i, 0)),
                  pl.BlockSpec((1, mod_rows if mod_rows == 1 else t, D_MODEL),
                               (lambda i: (i // mod_tiles, 0, 0)) if mod_rows == 1
                               else (lambda i: (0, i, 0))),
                  pl.BlockSpec((1, D_MODEL), lambda i: (0, 0)),
                  pl.BlockSpec((TOP_K, t, D_MODEL // 2), lambda i: (0, i + off, 0))],
        out_specs=pl.BlockSpec((t, D_MODEL), lambda i: (i, 0)),
        out_shape=jax.ShapeDtypeStruct((n, D_MODEL), F32),
        compiler_params=_cparams(("arbitrary",), 40),
        name="combine",
    )(w, h2, g2, normf, y_by_k)


def _rotary_tables(pos):
    half = KEY_DIM_R // 2
    inv_freq = ROPE_BASE ** (-jnp.arange(half, dtype=F32) / half)
    ang = pos[:, None] * inv_freq[None, :]
    cos = jnp.cos(ang)
    sin = jnp.sin(ang)
    cos_t = jnp.tile(jnp.concatenate([cos, cos], axis=1), (1, N_HEADS))
    sin_t = jnp.tile(jnp.concatenate([-sin, sin], axis=1), (1, N_HEADS))
    return cos_t.astype(F32), sin_t.astype(F32)


def _rel_bias_table(rel_bias, n_rows, n_cols, q_offset):
    heads = rel_bias.shape[0]
    n_diag = n_rows + n_cols - 1
    dist = q_offset + (n_rows - 1) - np.arange(n_diag)
    idx = np.clip(dist, -REL_CLIP, REL_CLIP) + REL_CLIP
    n_hi = int(np.sum(dist > REL_CLIP))
    n_lo = int(np.sum(dist < -REL_CLIP))
    mid = rel_bias[:, int(idx[n_diag - n_lo - 1]):int(idx[n_hi]) + 1][:, ::-1]
    diag = jnp.concatenate([jnp.broadcast_to(rel_bias[:, 2 * REL_CLIP:], (heads, n_hi)), mid,
                            jnp.broadcast_to(rel_bias[:, :1], (heads, n_lo))], axis=1)
    period = n_diag + 1
    v = jnp.roll(jnp.pad(diag, ((0, 0), (0, 1))), -(n_rows - 1), axis=1)
    skew = jnp.tile(v, (1, n_rows))[:, :n_rows * (period - 1)].reshape(heads, n_rows, period - 1)
    return skew[:, :, :n_cols].astype(F32)


def _prompt_bias(rel_bias):
    n_cols = ATT_QB + ATT_WINDOW
    r = np.arange(ATT_QB)[:, None]
    c = np.arange(n_cols)[None, :]
    band = c - (r // CHUNK) * CHUNK
    valid = (band >= 0) & (band < ATT_WINDOW + CHUNK)
    table = _rel_bias_table(rel_bias, ATT_QB, n_cols, ATT_WINDOW)
    return jnp.where(jnp.asarray(valid)[None], table, NEG_BIG)


def _sample_bias(rel_bias, t_new, cache_len):
    b = _rel_bias_table(rel_bias, t_new, cache_len + t_new, cache_len)
    return b[:, :, :cache_len], b[:, :, cache_len:]


def _mod_parts(mod, rows_each):
    parts = jnp.split(mod, 6, axis=-1)
    if rows_each == 1:
        return [p[:, None, :] for p in parts]
    return [jnp.repeat(p, rows_each, axis=0)[None] for p in parts]


def kernel(x_prompt, x_sample, cache_attn_k, cache_attn_v, state_ret, c_prompt, c_sample,
           norm1_g, norm2_g, w_ada, b_ada, w_in, rel_bias, w_o_attn, w_o_ret, w_out,
           w_router, b_router, w_exp_gate, w_exp_up, w_exp_down, w_sh_gate, w_sh_up, w_sh_down,
           normf_g):
    batch, seq, d = x_prompt.shape
    dec_batch, dec_seq, _ = x_sample.shape
    depth = w_in.shape[0]
    assert depth == 1 and d == D_MODEL
    assert seq % ROW_TILE == 0 and dec_batch * dec_seq == ROW_TILE and ROW_TILE == ATT_WINDOW
    cache_len = cache_attn_k.shape[2]
    n_p = batch * seq
    n_s = dec_batch * dec_seq
    tpb = seq // ROW_TILE
    l = 0

    bf = lambda a: a.astype(BF16)
    c_all = jnp.concatenate([c_prompt, c_sample], axis=0)
    pad = (-c_all.shape[0]) % 8
    c_all = jnp.pad(c_all, ((0, pad), (0, 0)))
    mod = _ada(c_all, bf(w_ada[l]), b_ada[l][None, :])
    mod_p = _mod_parts(mod[:batch], 1)
    mod_s = _mod_parts(mod[batch:batch + dec_batch], dec_seq)

    w_in_bf = bf(w_in[l])
    n1g = norm1_g[l][None, :]
    n2g = norm2_g[l][None, :]
    dense_w = [bf(w_o_attn[l]), bf(w_o_ret[l]), bf(w_out[l]), bf(w_router[l]).T,
               bf(w_sh_gate[l]), bf(w_sh_up[l]), bf(w_sh_down[l])]

    xp = x_prompt.reshape(n_p, d)
    xs_ = x_sample.reshape(n_s, d)
    cos_p, sin_p = _rotary_tables(jnp.arange(seq, dtype=F32))
    pos_s = PAST_LEN + jnp.arange(dec_seq, dtype=F32)
    cos_s, sin_s = _rotary_tables(jnp.tile(pos_s, dec_batch))

    (qa, ka, va, qr, kr, vr, gr, ga, gb, kv_p) = _inproj(
        xp, mod_p[1], mod_p[0], n1g, cos_p, sin_p, w_in_bf, tpb)
    oa = _attn_prompt(qa, ka, va, _prompt_bias(rel_bias[l]), batch, seq)
    zero_state = jnp.zeros((batch, N_HEADS, KEY_DIM_R, VAL_DIM_R), F32)
    yr_in, state_p = _retention(qr, kr, vr, gr, zero_state, batch, seq, RET_CHUNK)
    h_p, n2p_p, s_p = _outproj(xp, oa, yr_in, ga, gb, mod_p[2], mod_p[4], mod_p[3], mod_p[5], n2g,
                               dense_w, tpb)

    (qa_s, ka_s, va_s, qr_s, kr_s, vr_s, gr_s, ga_s, gb_s, kv_s) = _inproj(
        xs_, mod_s[1], mod_s[0], n1g, cos_s, sin_s, w_in_bf, 1)
    bias_c, bias_n = _sample_bias(rel_bias[l], dec_seq, cache_len)
    to_keys_minor = lambda c: jnp.transpose(c, (0, 1, 3, 4, 2))
    oa_s = _attn_sample(qa_s, ka_s, va_s, to_keys_minor(cache_attn_k), to_keys_minor(cache_attn_v),
                        bias_c, bias_n, dec_batch, dec_seq, cache_len)
    yr_in_s, state_s = _retention(qr_s, kr_s, vr_s, gr_s, state_ret[l], dec_batch, dec_seq, dec_seq)
    h_s, n2p_s, s_s = _outproj(xs_, oa_s, yr_in_s, ga_s, gb_s, mod_s[2], mod_s[4], mod_s[3], mod_s[5],
                               n2g, dense_w, 1)

    n2p = jnp.concatenate([n2p_p, n2p_s], axis=0)
    scores_t = jnp.concatenate([s_p, s_s], axis=1)
    lanes_of = lambda v: jnp.broadcast_to(v[:, None], (N_EXPERTS, ROUTE_TILE))
    idx_t, w_t, rank_t, counts = _route(scores_t, lanes_of(b_router[l]))
    (tile_expert, n_tiles, expert_slot, expert_next, tile_halves, pstart, vend,
     n_sorted_rows) = _ffn_plan(counts[:, 0], (n_p + n_s) * TOP_K)
    dest_t = _dest(idx_t, rank_t, lanes_of(pstart.astype(F32)))
    dest_flat = dest_t.T.reshape(-1)
    w_route = jnp.pad(w_t.T, ((0, 0), (0, LANES - TOP_K)))
    xs_sorted = _dispatch(dest_flat, vend, n2p, n_sorted_rows)
    ys_sorted = _ffn(tile_expert, n_tiles, expert_slot, expert_next, tile_halves, xs_sorted,
                     w_exp_gate[l], w_exp_up[l], w_exp_down[l])
    nf = normf_g[None, :]
    y_by_k = _sc_gather_rows(ys_sorted, dest_t.reshape(-1)).reshape(TOP_K, n_p + n_s, d // 2)
    y_p = _combine(w_route, h_p, mod_p[5], nf, y_by_k, 0, tpb)
    y_s = _combine(w_route, h_s, mod_s[5], nf, y_by_k, n_p, 1)

    keep = min(ATT_WINDOW, seq)
    kv_p = kv_p.reshape(batch, ROW_TILE, 2, N_HEADS, HEAD_DIM_A)[:, ROW_TILE - keep:]
    kv_s = kv_s.reshape(dec_batch, dec_seq, 2, N_HEADS, HEAD_DIM_A)
    return (y_p.reshape(batch, seq, d), y_s.reshape(dec_batch, dec_seq, d),
            kv_p[:, :, 0][None], kv_p[:, :, 1][None], state_p[None],
            kv_s[:, :, 0][None], kv_s[:, :, 1][None], state_s[None])
```

```python
import functools

import numpy as np
import jax
import jax.numpy as jnp
from jax import lax
from jax.experimental import pallas as pl
from jax.experimental.pallas import tpu as pltpu
from jax.experimental.pallas import tpu_sc as plsc

F32 = jnp.float32
BF16 = jnp.bfloat16
I32 = jnp.int32
U32 = jnp.uint32

D_MODEL = 1024
PAST_LEN = 4096
CHUNK = 64
N_LEFT_CHUNKS = 8
ATT_WINDOW = N_LEFT_CHUNKS * CHUNK
N_HEADS = 8
HEAD_DIM_A = 64
D_ATT = N_HEADS * HEAD_DIM_A
REL_CLIP = 128
KEY_DIM_R = 64
VAL_DIM_R = 128
D_RET_K = N_HEADS * KEY_DIM_R
D_RET_V = N_HEADS * VAL_DIM_R
ROPE_BASE = 10000.0
N_EXPERTS = 256
TOP_K = 8
N_GROUPS = 8
GROUP_SIZE = N_EXPERTS // N_GROUPS
TOPK_GROUPS = 4
D_EXPERT = 256
ROUTED_SCALE = 2.5
EPS = 1e-6
IN_WIDTHS = (D_ATT, D_ATT, D_ATT, D_RET_K, D_RET_K, D_RET_V, D_RET_V, D_MODEL, D_MODEL)
IN_OFFS = tuple(int(v) for v in np.cumsum((0,) + IN_WIDTHS))
D_IN = IN_OFFS[-1]

NEG_BIG = -1e30
LANES = 128
V7X_VMEM_BYTES = 64 * 1024 * 1024
V7X_SC_CORES = 2
V7X_SC_SUBCORES = 16
SC_GATHER_ROWS = 64

ROW_TILE = 512
ATT_QB = 256
RET_CHUNK = 256
ROUTE_TILE = 512
MOVE_TILE = 256
FFN_QUANTUM = 128
FFN_HALF = 2 * FFN_QUANTUM
FFN_TILE = 2 * FFN_HALF


def _cparams(semantics, vmem_mb):
    return pltpu.CompilerParams(dimension_semantics=semantics,
                                vmem_limit_bytes=min(vmem_mb * 1024 * 1024, V7X_VMEM_BYTES - (6 << 20)))


def _silu(x):
    return x * jax.nn.sigmoid(x)


def _pack_bf16_pair(lo, hi):
    lo_b = pltpu.bitcast(lo.astype(BF16).astype(F32), U32) >> 16
    hi_b = pltpu.bitcast(hi.astype(BF16).astype(F32), U32) & jnp.uint32(0xFFFF0000)
    return lo_b | hi_b


def _unpack_bf16_pair(u):
    lo = pltpu.bitcast(u << 16, F32)
    hi = pltpu.bitcast(u & jnp.uint32(0xFFFF0000), F32)
    return lo, hi


def _ada_kernel(c_ref, w_ref, b_ref, o_ref):
    sc = _silu(c_ref[...]).astype(BF16)
    o_ref[...] = jnp.dot(sc, w_ref[...], preferred_element_type=F32) + b_ref[...]


def _ada(c_all, w_ada_bf, b_ada):
    rows = c_all.shape[0]
    n_out = w_ada_bf.shape[1]
    blk = D_MODEL
    return pl.pallas_call(
        _ada_kernel,
        grid=(n_out // blk,),
        in_specs=[pl.BlockSpec((rows, D_MODEL), lambda j: (0, 0)),
                  pl.BlockSpec((D_MODEL, blk), lambda j: (0, j)),
                  pl.BlockSpec((1, blk), lambda j: (0, j))],
        out_specs=pl.BlockSpec((rows, blk), lambda j: (0, j)),
        out_shape=jax.ShapeDtypeStruct((rows, n_out), F32),
        compiler_params=_cparams(("arbitrary",), 24),
        name="ada",
    )(c_all, w_ada_bf, b_ada)


def _inproj_kernel(x_ref, sc_ref, sh_ref, g_ref, cos_ref, sin_ref, w_ref,
                   qa_ref, ka_ref, va_ref, qr_ref, kr_ref, vr_ref, gr_ref, ga_ref, gb_ref,
                   kv_ref, *, tiles_per_batch):
    x = x_ref[...]
    xn = x * lax.rsqrt(jnp.mean(x * x, axis=-1, keepdims=True) + EPS) * g_ref[...]
    nb = (xn * (1.0 + sc_ref[0]) + sh_ref[0]).astype(BF16)

    def proj(seg):
        return jnp.dot(nb, w_ref[:, IN_OFFS[seg]:IN_OFFS[seg + 1]], preferred_element_type=F32)

    qa_ref[...] = proj(0).astype(BF16)
    ka = proj(1)
    va = proj(2)
    ka_ref[...] = ka.astype(BF16)
    va_ref[...] = va.astype(BF16)

    @pl.when(pl.program_id(0) % tiles_per_batch == tiles_per_batch - 1)
    def _():
        kv_ref[:, :D_ATT] = ka
        kv_ref[:, D_ATT:] = va

    cos = cos_ref[...]
    sin = sin_ref[...]
    first_half = (lax.broadcasted_iota(I32, (1, D_RET_K), 1) % KEY_DIM_R) < (KEY_DIM_R // 2)

    def rotary(t):
        partner = jnp.where(first_half, pltpu.roll(t, D_RET_K - KEY_DIM_R // 2, 1),
                            pltpu.roll(t, KEY_DIM_R // 2, 1))
        return t * cos + partner * sin

    qr_ref[...] = rotary(proj(3)).astype(BF16)
    kr_ref[...] = (rotary(proj(4)) * (KEY_DIM_R ** -0.5)).astype(BF16)
    vr_ref[...] = proj(5).astype(BF16)
    gr_ref[...] = proj(6).astype(BF16)
    ga_ref[...] = proj(7).astype(BF16)
    gb_ref[...] = proj(8).astype(BF16)


def _inproj(x2d, sc, sh, g, cos_t, sin_t, w_in_bf, tiles_per_batch):
    n = x2d.shape[0]
    tm = ROW_TILE
    n_tiles = n // tm
    n_batches = n_tiles // tiles_per_batch
    mod_rows = sc.shape[1]
    pos_tiles = cos_t.shape[0] // tm

    def row_spec(width):
        return pl.BlockSpec((tm, width), lambda i: (i, 0))

    mod_spec = pl.BlockSpec((1, mod_rows, D_MODEL), lambda i: (i // tiles_per_batch, 0, 0))
    pos_spec = pl.BlockSpec((tm, D_RET_K), lambda i: (i % pos_tiles, 0))
    out_widths = (D_ATT, D_ATT, D_ATT, D_RET_K, D_RET_K, D_RET_V, D_RET_V, D_MODEL, D_MODEL)
    out_shape = [jax.ShapeDtypeStruct((n, w), BF16) for w in out_widths]
    out_shape.append(jax.ShapeDtypeStruct((n_batches * tm, 2 * D_ATT), F32))
    out_specs = [row_spec(w) for w in out_widths]
    out_specs.append(pl.BlockSpec((tm, 2 * D_ATT), lambda i: (i // tiles_per_batch, 0)))
    return pl.pallas_call(
        functools.partial(_inproj_kernel, tiles_per_batch=tiles_per_batch),
        grid=(n_tiles,),
        in_specs=[row_spec(D_MODEL), mod_spec, mod_spec,
                  pl.BlockSpec((1, D_MODEL), lambda i: (0, 0)),
                  pos_spec, pos_spec,
                  pl.BlockSpec((D_MODEL, D_IN), lambda i: (0, 0))],
        out_specs=out_specs,
        out_shape=out_shape,
        compiler_params=_cparams(("arbitrary",), 56),
        name="inproj",
    )(x2d, sc, sh, g, cos_t, sin_t, w_in_bf)


def _softmax_pv(s, v_parts):
    m = functools.reduce(jnp.maximum, [jnp.max(t, axis=-1, keepdims=True) for t in s])
    ps = [jnp.exp(t - m) for t in s]
    l = functools.reduce(jnp.add, [jnp.sum(p, axis=-1, keepdims=True) for p in ps])
    o = functools.reduce(jnp.add, [jnp.dot(p.astype(BF16), v, preferred_element_type=F32)
                                   for p, v in zip(ps, v_parts)])
    return o / l


def _attn_prompt_kernel(q_ref, k0_ref, k1_ref, k2_ref, v0_ref, v1_ref, v2_ref, bias_ref, o_ref):
    j = pl.program_id(1)
    q = q_ref[...]
    k = jnp.concatenate([k0_ref[...], k1_ref[...], k2_ref[...]], axis=0)
    v = jnp.concatenate([v0_ref[...], v1_ref[...], v2_ref[...]], axis=0)
    n_keys = k.shape[0]
    key_block = lax.broadcasted_iota(I32, (1, n_keys), 1) // ATT_QB
    before_start = jnp.where(key_block < 2 - j, NEG_BIG, 0.0)
    outs = []
    for h in range(N_HEADS):
        sl = slice(h * HEAD_DIM_A, (h + 1) * HEAD_DIM_A)
        qh = (q[:, sl].astype(F32) * (HEAD_DIM_A ** -0.5)).astype(BF16)
        s = lax.dot_general(qh, k[:, sl], (((1,), (1,)), ((), ())), preferred_element_type=F32)
        s = s + bias_ref[h] + before_start
        outs.append(_softmax_pv([s], [v[:, sl]]))
    o_ref[...] = jnp.concatenate(outs, axis=1).astype(BF16)


def _attn_prompt(q, k, v, bias_full, batch, seq):
    qb = ATT_QB
    nq = seq // qb

    def q_map(b, j):
        return (b * nq + j, 0)

    def kv_map(back):
        return lambda b, j: (b * nq + jnp.maximum(j - back, 0), 0)

    blk = lambda m: pl.BlockSpec((qb, D_ATT), m)
    return pl.pallas_call(
        _attn_prompt_kernel,
        grid=(batch, nq),
        in_specs=[blk(q_map), blk(kv_map(2)), blk(kv_map(1)), blk(kv_map(0)),
                  blk(kv_map(2)), blk(kv_map(1)), blk(kv_map(0)),
                  pl.BlockSpec(bias_full.shape, lambda b, j: (0, 0, 0))],
        out_specs=blk(q_map),
        out_shape=jax.ShapeDtypeStruct((batch * seq, D_ATT), BF16),
        compiler_params=_cparams(("parallel", "arbitrary"), 40),
        name="attn_prompt",
    )(q, k, k, k, v, v, v, bias_full)


SAMPLE_ATT_BATCHES = 2


def _attn_sample_kernel(q_ref, kn_ref, vn_ref, ck_ref, cv_ref, bc_ref, bn_ref, o_ref, *, t_new):
    nt = (((1,), (1,)), ((), ()))
    for b in range(SAMPLE_ATT_BATCHES):
        rows = slice(b * t_new, (b + 1) * t_new)
        q = q_ref[rows, :]
        kn = kn_ref[rows, :]
        vn = vn_ref[rows, :]
        outs = []
        for h in range(N_HEADS):
            sl = slice(h * HEAD_DIM_A, (h + 1) * HEAD_DIM_A)
            qh = (q[:, sl].astype(F32) * (HEAD_DIM_A ** -0.5)).astype(BF16)
            kc_t = ck_ref[b, h].astype(BF16)
            vc_t = cv_ref[b, h].astype(BF16)
            s_c = jnp.dot(qh, kc_t, preferred_element_type=F32) + bc_ref[h]
            s_n = lax.dot_general(qh, kn[:, sl], nt, preferred_element_type=F32) + bn_ref[h]
            m = jnp.maximum(jnp.max(s_c, axis=-1, keepdims=True), jnp.max(s_n, axis=-1, keepdims=True))
            p_c = jnp.exp(s_c - m)
            p_n = jnp.exp(s_n - m)
            l = jnp.sum(p_c, axis=-1, keepdims=True) + jnp.sum(p_n, axis=-1, keepdims=True)
            o = (lax.dot_general(p_c.astype(BF16), vc_t, nt, preferred_element_type=F32)
                 + jnp.dot(p_n.astype(BF16), vn[:, sl], preferred_element_type=F32))
            outs.append(o / l)
        o_ref[rows, :] = jnp.concatenate(outs, axis=1).astype(BF16)


def _attn_sample(q, k, v, cache_k_t, cache_v_t, bias_cache, bias_new, batch, t_new, cache_len):
    nb = SAMPLE_ATT_BATCHES
    blk = pl.BlockSpec((nb * t_new, D_ATT), lambda b: (b, 0))
    cblk = pl.BlockSpec((None, nb, N_HEADS, HEAD_DIM_A, cache_len), lambda b: (0, b, 0, 0, 0))
    return pl.pallas_call(
        functools.partial(_attn_sample_kernel, t_new=t_new),
        grid=(batch // nb,),
        in_specs=[blk, blk, blk, cblk, cblk,
                  pl.BlockSpec(bias_cache.shape, lambda b: (0, 0, 0)),
                  pl.BlockSpec(bias_new.shape, lambda b: (0, 0, 0))],
        out_specs=blk,
        out_shape=jax.ShapeDtypeStruct((batch * t_new, D_ATT), BF16),
        compiler_params=_cparams(("arbitrary",), 40),
        name="attn_sample",
    )(q, k, v, cache_k_t, cache_v_t, bias_cache, bias_new)


def _ret_kernel(q_ref, k_ref, v_ref, g_ref, s0_ref, dmask_ref, qdec_ref, kdec_ref, sdec_ref,
                y_ref, sout_ref, state_ref):
    c = pl.program_id(1)

    @pl.when(c == 0)
    def _():
        state_ref[...] = s0_ref[0]

    q = q_ref[...]
    k = k_ref[...]
    v = v_ref[...]
    g = g_ref[...]
    outs = []
    for h in range(N_HEADS):
        ks = slice(h * KEY_DIM_R, (h + 1) * KEY_DIM_R)
        vs = slice(h * VAL_DIM_R, (h + 1) * VAL_DIM_R)
        qh, kh, vh = q[:, ks], k[:, ks], v[:, vs]
        scores = lax.dot_general(qh, kh, (((1,), (1,)), ((), ())), preferred_element_type=F32)
        inner = jnp.dot((scores * dmask_ref[h]).astype(BF16), vh, preferred_element_type=F32)
        state = state_ref[h]
        cross = jnp.dot(qh, state.astype(BF16), preferred_element_type=F32) * qdec_ref[h]
        o = inner + cross
        v_dec = (vh.astype(F32) * kdec_ref[h]).astype(BF16)
        state_ref[h] = sdec_ref[h] * state + lax.dot_general(
            kh, v_dec, (((0,), (0,)), ((), ())), preferred_element_type=F32)
        on = o * lax.rsqrt(jnp.mean(o * o, axis=-1, keepdims=True) + EPS)
        outs.append(on * _silu(g[:, vs].astype(F32)))
    y_ref[...] = jnp.concatenate(outs, axis=1).astype(BF16)

    @pl.when(c == pl.num_programs(1) - 1)
    def _():
        sout_ref[0] = state_ref[...]


def _ret_tables(chunk):
    log_g = jnp.log(1.0 - jnp.exp2(-5.0 - jnp.arange(N_HEADS, dtype=F32)))
    i = jnp.arange(chunk, dtype=F32)
    diff = i[:, None] - i[None, :]
    dmask = jnp.where(diff >= 0, jnp.exp(log_g[:, None, None] * jnp.maximum(diff, 0.0)), 0.0)
    qdec = jnp.exp(log_g[:, None] * (i + 1.0))
    kdec = jnp.exp(log_g[:, None] * (chunk - 1.0 - i))
    sdec = jnp.exp(log_g * chunk)
    bc = lambda t: jnp.broadcast_to(t[:, :, None], (N_HEADS, t.shape[1], VAL_DIM_R)).astype(F32)
    sdec_t = jnp.broadcast_to(sdec[:, None, None], (N_HEADS, 1, VAL_DIM_R)).astype(F32)
    return dmask.astype(F32), bc(qdec), bc(kdec), sdec_t


def _retention(q, k, v, gate, state0, batch, seq, chunk):
    nc = seq // chunk
    dmask, qdec, kdec, sdec = _ret_tables(chunk)
    row = lambda w: pl.BlockSpec((chunk, w), lambda b, c: (b * nc + c, 0))
    const = lambda a: pl.BlockSpec(a.shape, lambda b, c: (0,) * a.ndim)
    st_spec = pl.BlockSpec((1, N_HEADS, KEY_DIM_R, VAL_DIM_R), lambda b, c: (b, 0, 0, 0))
    return pl.pallas_call(
        _ret_kernel,
        grid=(batch, nc),
        in_specs=[row(D_RET_K), row(D_RET_K), row(D_RET_V), row(D_RET_V), st_spec,
                  const(dmask), const(qdec), const(kdec), const(sdec)],
        out_specs=[row(D_RET_V), st_spec],
        out_shape=[jax.ShapeDtypeStruct((batch * seq, D_RET_V), BF16),
                   jax.ShapeDtypeStruct((batch, N_HEADS, KEY_DIM_R, VAL_DIM_R), F32)],
        scratch_shapes=[pltpu.VMEM((N_HEADS, KEY_DIM_R, VAL_DIM_R), F32)],
        compiler_params=_cparams(("parallel", "arbitrary"), 32),
        name="retention",
    )(q, k, v, gate, state0, dmask, qdec, kdec, sdec)


def _outproj_kernel(x_ref, oa_ref, yr_ref, ga_ref, gb_ref, g1_ref, sc2_ref, sh2_ref, g2_ref, n2g_ref,
                    woa_ref, wor_ref, wout_ref, wrt_ref, wsg_ref, wsu_ref, wsd_ref,
                    h_ref, n2p_ref, s_ref):
    ya = jnp.dot(oa_ref[...], woa_ref[...], preferred_element_type=F32)
    yr = jnp.dot(yr_ref[...], wor_ref[...], preferred_element_type=F32)
    merged = (jax.nn.sigmoid(ga_ref[...].astype(F32)) * ya
              + jax.nn.sigmoid(gb_ref[...].astype(F32)) * yr)
    mix = jnp.dot(merged.astype(BF16), wout_ref[...], preferred_element_type=F32)
    h = x_ref[...] + g1_ref[0] * mix
    hn = h * lax.rsqrt(jnp.mean(h * h, axis=-1, keepdims=True) + EPS) * n2g_ref[...]
    n2 = hn * (1.0 + sc2_ref[0]) + sh2_ref[0]
    n2b = n2.astype(BF16)
    s_ref[...] = jax.nn.sigmoid(lax.dot_general(wrt_ref[...], n2b, (((1,), (1,)), ((), ())),
                                                preferred_element_type=F32))
    hid = _silu(jnp.dot(n2b, wsg_ref[...], preferred_element_type=F32)) * jnp.dot(
        n2b, wsu_ref[...], preferred_element_type=F32)
    shared = jnp.dot(hid.astype(BF16), wsd_ref[...], preferred_element_type=F32)
    h_ref[...] = h + g2_ref[0] * shared
    half = D_MODEL // 2
    n2p_ref[...] = _pack_bf16_pair(n2[:, :half], n2[:, half:])


def _outproj(x2d, oa, yr_in, ga, gb, g1, sc2, sh2, g2, n2g, weights, tiles_per_batch):
    n = x2d.shape[0]
    tm = ROW_TILE
    mod_rows = g1.shape[1]
    row = lambda w: pl.BlockSpec((tm, w), lambda i: (i, 0))
    mod_spec = pl.BlockSpec((1, mod_rows, D_MODEL), lambda i: (i // tiles_per_batch, 0, 0))
    const = lambda a: pl.BlockSpec(a.shape, lambda i: (0,) * a.ndim)
    return pl.pallas_call(
        _outproj_kernel,
        grid=(n // tm,),
        in_specs=[row(D_MODEL), row(D_ATT), row(D_RET_V), row(D_MODEL), row(D_MODEL),
                  mod_spec, mod_spec, mod_spec, mod_spec, const(n2g)] + [const(w) for w in weights],
        out_specs=[row(D_MODEL), row(D_MODEL // 2), pl.BlockSpec((N_EXPERTS, tm), lambda i: (0, i))],
        out_shape=[jax.ShapeDtypeStruct((n, D_MODEL), F32),
                   jax.ShapeDtypeStruct((n, D_MODEL // 2), U32),
                   jax.ShapeDtypeStruct((N_EXPERTS, n), F32)],
        compiler_params=_cparams(("arbitrary",), 48),
        name="outproj",
    )(x2d, oa, yr_in, ga, gb, g1, sc2, sh2, g2, n2g, *weights)


def _route_kernel(s_ref, b_ref, idx_ref, w_ref, rank_ref, cnt_ref, run_ref, tri_ref):
    step = pl.program_id(0)
    t = s_ref.shape[1]

    @pl.when(step == 0)
    def _():
        run_ref[...] = jnp.zeros_like(run_ref)
        r = lax.broadcasted_iota(I32, (t, t), 0)
        c = lax.broadcasted_iota(I32, (t, t), 1)
        tri_ref[...] = jnp.where(r < c, 1.0, 0.0).astype(BF16)

    s = s_ref[...]
    sel = s + b_ref[...]
    row_f = lax.broadcasted_iota(I32, (N_EXPERTS, t), 0).astype(F32)

    def first_argmax(vals, rows):
        m = jnp.max(vals, axis=0, keepdims=True)
        pos = jnp.min(jnp.where(vals == m, rows, float(N_EXPERTS)), axis=0, keepdims=True)
        return m, pos

    gscore = []
    group_row = lax.broadcasted_iota(I32, (GROUP_SIZE, t), 0).astype(F32)
    for g in range(N_GROUPS):
        rows = slice(g * GROUP_SIZE, (g + 1) * GROUP_SIZE)
        m1, p1 = first_argmax(sel[rows], group_row)
        m2 = jnp.max(jnp.where(group_row == p1, -jnp.inf, sel[rows]), axis=0, keepdims=True)
        gscore.append(m1 + m2)
    cand_parts = []
    for g in range(N_GROUPS):
        rows = slice(g * GROUP_SIZE, (g + 1) * GROUP_SIZE)
        beaten_by = jnp.zeros((1, t), F32)
        for o in range(N_GROUPS):
            if o == g:
                continue
            wins = (gscore[o] > gscore[g]) if o > g else (gscore[o] >= gscore[g])
            beaten_by = beaten_by + jnp.where(wins, 1.0, 0.0)
        cand_parts.append(jnp.where(beaten_by < TOPK_GROUPS, sel[rows], -jnp.inf))
    cand = jnp.concatenate(cand_parts, axis=0)

    picked = jnp.zeros((N_EXPERTS, t), F32)
    idx_rows, w_rows = [], []
    for _ in range(TOP_K):
        _, pos = first_argmax(cand, row_f)
        hit = row_f == pos
        w_rows.append(jnp.sum(jnp.where(hit, s, 0.0), axis=0, keepdims=True))
        idx_rows.append(pos)
        picked = jnp.where(hit, 1.0, picked)
        cand = jnp.where(hit, -jnp.inf, cand)
    w_sum = functools.reduce(jnp.add, w_rows)

    before = jnp.dot(picked.astype(BF16), tri_ref[...], preferred_element_type=F32) + run_ref[...]
    run_ref[...] = run_ref[...] + jnp.sum(picked, axis=1, keepdims=True)
    rank_rows = [jnp.sum(jnp.where(row_f == idx_rows[kk], before, 0.0), axis=0, keepdims=True)
                 for kk in range(TOP_K)]

    idx_ref[...] = jnp.concatenate(idx_rows, axis=0).astype(I32)
    w_ref[...] = jnp.concatenate([w / w_sum * ROUTED_SCALE for w in w_rows], axis=0)
    rank_ref[...] = jnp.concatenate(rank_rows, axis=0).astype(I32)

    @pl.when(step == pl.num_programs(0) - 1)
    def _():
        cnt_ref[...] = run_ref[...].astype(I32)


def _route(scores_t, b_col):
    n = scores_t.shape[1]
    t = ROUTE_TILE
    col = pl.BlockSpec((TOP_K, t), lambda i: (0, i))
    const = pl.BlockSpec((N_EXPERTS, t), lambda i: (0, 0))
    return pl.pallas_call(
        _route_kernel,
        grid=(n // t,),
        in_specs=[pl.BlockSpec((N_EXPERTS, t), lambda i: (0, i)), const],
        out_specs=[col, col, col, const],
        out_shape=[jax.ShapeDtypeStruct((TOP_K, n), I32),
                   jax.ShapeDtypeStruct((TOP_K, n), F32),
                   jax.ShapeDtypeStruct((TOP_K, n), I32),
                   jax.ShapeDtypeStruct((N_EXPERTS, t), I32)],
        scratch_shapes=[pltpu.VMEM((N_EXPERTS, t), F32), pltpu.VMEM((t, t), BF16)],
        compiler_params=_cparams(("arbitrary",), 32),
        name="route",
    )(scores_t, b_col)


def _dest_kernel(idx_ref, rank_ref, start_ref, dest_ref):
    t = idx_ref.shape[1]
    row = lax.broadcasted_iota(I32, (N_EXPERTS, t), 0)
    starts = start_ref[...]
    base = [jnp.sum(jnp.where(row == idx_ref[kk:kk + 1, :], starts, 0.0), axis=0, keepdims=True)
            for kk in range(TOP_K)]
    dest_ref[...] = jnp.concatenate(base, axis=0).astype(I32) + rank_ref[...]


def _dest(idx_t, rank_t, starts_col):
    n = idx_t.shape[1]
    t = ROUTE_TILE
    col = pl.BlockSpec((TOP_K, t), lambda i: (0, i))
    return pl.pallas_call(
        _dest_kernel,
        grid=(n // t,),
        in_specs=[col, col, pl.BlockSpec((N_EXPERTS, t), lambda i: (0, 0))],
        out_specs=col,
        out_shape=jax.ShapeDtypeStruct((TOP_K, n), I32),
        compiler_params=_cparams(("arbitrary",), 32),
        name="dest",
    )(idx_t, rank_t, starts_col)


def _sc_mesh():
    return plsc.VectorSubcoreMesh(core_axis_name="core", subcore_axis_name="subcore",
                                  num_cores=V7X_SC_CORES, num_subcores=V7X_SC_SUBCORES)


def _sc_dispatch(n2p, dest_kmajor, pad_rows, n_out_rows):
    n, width = n2p.shape
    rows = SC_GATHER_ROWS
    n_workers = V7X_SC_CORES * V7X_SC_SUBCORES
    src_chunks = n // rows
    items = dest_kmajor.shape[0] // rows
    per_worker = items // n_workers
    pad_per_worker = pad_rows.shape[0] // rows // n_workers
    assert src_chunks * rows == n and per_worker * n_workers == items and per_worker % 2 == 0
    assert pad_per_worker * n_workers * rows == pad_rows.shape[0]
    idx3 = dest_kmajor.reshape(n_workers, per_worker, rows)
    pad3 = pad_rows.reshape(n_workers, pad_per_worker, rows)
    zeros = jnp.zeros((rows, width), n2p.dtype)

    def body(src_hbm, idx_hbm, pad_hbm, zero_hbm, out_hbm, idx_v, pad_v, rows_v, load_sem, scat_sem):
        worker = lax.axis_index("subcore") * V7X_SC_CORES + lax.axis_index("core")
        pltpu.sync_copy(idx_hbm.at[worker], idx_v)
        pltpu.sync_copy(pad_hbm.at[worker], pad_v)
        pltpu.sync_copy(zero_hbm, rows_v.at[0])

        @pl.loop(0, pad_per_worker)
        def _(c):
            pltpu.sync_copy(rows_v.at[0], out_hbm.at[pad_v.at[c]])

        def load(c, b):
            chunk = lax.rem(worker * per_worker + c, src_chunks)
            off = pl.multiple_of(chunk * rows, rows)
            return pltpu.make_async_copy(src_hbm.at[pl.ds(off, rows)], rows_v.at[b], load_sem.at[b])

        def scatter(c, b):
            return pltpu.make_async_copy(rows_v.at[b], out_hbm.at[idx_v.at[c]], scat_sem.at[b])

        load(0, 0).start()

        @pl.loop(0, per_worker, step=2)
        def _(c0):
            for b in range(2):
                c = c0 + b
                load(c, b).wait()

                @pl.when(c >= 1)
                def _():
                    scatter(c - 1, 1 - b).wait()

                @pl.when(c + 1 < per_worker)
                def _():
                    load(c + 1, 1 - b).start()

                scatter(c, b).start()

        scatter(per_worker - 1, (per_worker - 1) % 2).wait()

    return pl.kernel(
        body, mesh=_sc_mesh(),
        out_type=jax.ShapeDtypeStruct((n_out_rows, width), n2p.dtype),
        scratch_types=[pltpu.VMEM((per_worker, rows), I32),
                       pltpu.VMEM((pad_per_worker, rows), I32),
                       pltpu.VMEM((2, rows, width), n2p.dtype),
                       pltpu.SemaphoreType.DMA((2,)),
                       pltpu.SemaphoreType.DMA((2,))],
        name="sc_dispatch",
    )(n2p, idx3, pad3, zeros)


def _ffn_kernel(texp_ref, ntiles_ref, eslot_ref, enext_ref, nhalf_ref, xs_ref, wg_hbm, wu_hbm, wd_hbm,
                ys_ref, wg_buf, wu_buf, wd_buf, wgu_bf, wd_bf, sems):
    g = pl.program_id(0)

    def weight_copies(e, slot):
        return (pltpu.make_async_copy(wg_hbm.at[e], wg_buf.at[slot], sems.at[slot, 0]),
                pltpu.make_async_copy(wu_hbm.at[e], wu_buf.at[slot], sems.at[slot, 1]),
                pltpu.make_async_copy(wd_hbm.at[e], wd_buf.at[slot], sems.at[slot, 2]))

    @pl.when(g < ntiles_ref[0])
    def _():
        e = texp_ref[g]
        changed = jnp.logical_or(g == 0, texp_ref[jnp.maximum(g - 1, 0)] != e)

        @pl.when(changed)
        def _():
            slot = eslot_ref[e]

            @pl.when(g == 0)
            def _():
                for c in weight_copies(e, slot):
                    c.start()

            for c in weight_copies(e, slot):
                c.wait()
            nxt = enext_ref[e]

            @pl.when(nxt >= 0)
            def _():
                for c in weight_copies(nxt, 1 - slot):
                    c.start()

            wgu_bf[:, :D_EXPERT] = wg_buf[slot].astype(BF16)
            wgu_bf[:, D_EXPERT:] = wu_buf[slot].astype(BF16)
            wd_bf[...] = wd_buf[slot].astype(BF16)

        def expert_rows(rows):
            lo, hi = _unpack_bf16_pair(xs_ref[rows, :])
            x = jnp.concatenate([lo, hi], axis=1).astype(BF16)
            gu = jnp.dot(x, wgu_bf[...], preferred_element_type=F32)
            hid = (_silu(gu[:, :D_EXPERT]) * gu[:, D_EXPERT:]).astype(BF16)
            y = jnp.dot(hid, wd_bf[...], preferred_element_type=F32)
            half = D_MODEL // 2
            ys_ref[rows, :] = _pack_bf16_pair(y[:, :half], y[:, half:])

        for n_groups in range(1, FFN_TILE // FFN_QUANTUM + 1):

            @pl.when(nhalf_ref[g] == n_groups)
            def _(n_groups=n_groups):
                used = n_groups * FFN_QUANTUM
                for start in range(0, used, FFN_HALF):
                    expert_rows(slice(start, min(start + FFN_HALF, used)))
                if used < FFN_TILE:
                    ys_ref[used:, :] = jnp.zeros((FFN_TILE - used, ys_ref.shape[1]), U32)


def _ffn(tile_expert, n_tiles, expert_slot, expert_next, tile_halves, xs, w_gate, w_up, w_down):
    rows, width = xs.shape
    m = FFN_TILE
    max_tiles = tile_expert.shape[0]
    row_map = lambda g, te, nt, es, en, nh: (jnp.minimum(g, nt[0] - 1), 0)
    hbm = pl.BlockSpec(memory_space=pl.ANY)
    grid_spec = pltpu.PrefetchScalarGridSpec(
        num_scalar_prefetch=5,
        grid=(max_tiles,),
        in_specs=[pl.BlockSpec((m, width), row_map), hbm, hbm, hbm],
        out_specs=pl.BlockSpec((m, width), row_map),
        scratch_shapes=[pltpu.VMEM((2, D_MODEL, D_EXPERT), F32),
                        pltpu.VMEM((2, D_MODEL, D_EXPERT), F32),
                        pltpu.VMEM((2, D_EXPERT, D_MODEL), F32),
                        pltpu.VMEM((D_MODEL, 2 * D_EXPERT), BF16),
                        pltpu.VMEM((D_EXPERT, D_MODEL), BF16),
                        pltpu.SemaphoreType.DMA((2, 3))],
    )
    return pl.pallas_call(
        _ffn_kernel,
        grid_spec=grid_spec,
        out_shape=jax.ShapeDtypeStruct((rows, width), U32),
        compiler_params=_cparams(("arbitrary",), 32),
        name="ffn",
    )(tile_expert, n_tiles, expert_slot, expert_next, tile_halves, xs, w_gate, w_up, w_down)


def _ffn_plan(counts, n_assign):
    m = FFN_TILE
    max_tiles = n_assign // m + N_EXPERTS
    padded = ((counts + m - 1) // m) * m
    pend = jnp.cumsum(padded).astype(I32)
    pstart = pend - padded
    n_tiles = pend[-1:] // m
    g = jnp.minimum(jnp.arange(max_tiles, dtype=I32), n_tiles - 1)
    tile_expert = jnp.sum((pend[None, :] <= (g * m)[:, None]).astype(I32), axis=1)
    tile_expert = jnp.minimum(tile_expert, N_EXPERTS - 1)
    vend = pstart + ((counts + FFN_QUANTUM - 1) // FFN_QUANTUM) * FFN_QUANTUM
    own = tile_expert[:, None] == jnp.arange(N_EXPERTS, dtype=I32)[None, :]
    tile_vend = jnp.sum(jnp.where(own, vend[None, :], 0), axis=1)
    tile_halves = (jnp.clip(tile_vend - g * m, 0, m) // FFN_QUANTUM).astype(I32)
    used = counts > 0
    expert_slot = ((jnp.cumsum(used.astype(I32)) - 1) % 2).astype(I32)
    ids = jnp.where(used, jnp.arange(N_EXPERTS, dtype=I32), N_EXPERTS)
    first_used_from = lax.cummin(ids, axis=0, reverse=True)
    nxt = jnp.concatenate([first_used_from[1:], jnp.full((1,), N_EXPERTS, I32)])
    expert_next = jnp.where(nxt < N_EXPERTS, nxt, -1).astype(I32)
    spare_row = max_tiles * m
    j = jnp.arange(FFN_QUANTUM, dtype=I32)[None, :]
    seg_end = (pstart + counts)[:, None]
    pad_rows = jnp.where(j < (vend[:, None] - seg_end), seg_end + j, spare_row).astype(I32).reshape(-1)
    return (tile_expert, n_tiles, expert_slot, expert_next, tile_halves, pstart, pad_rows,
            spare_row + 8)


def _sc_gather_rows(table, idx):
    n_idx = idx.shape[0]
    width = table.shape[1]
    n_workers = V7X_SC_CORES * V7X_SC_SUBCORES
    per_worker = n_idx // n_workers
    n_chunks = per_worker // SC_GATHER_ROWS
    assert per_worker * n_workers == n_idx and n_chunks * SC_GATHER_ROWS == per_worker and n_chunks % 2 == 0

    def body(table_hbm, idx_hbm, out_hbm, idx_v, rows_v, gather_sem, write_sem):
        worker = lax.axis_index("subcore") * V7X_SC_CORES + lax.axis_index("core")
        base = worker * per_worker
        pltpu.sync_copy(idx_hbm.at[pl.ds(base, per_worker)], idx_v)

        def gather(c, b):
            off = pl.multiple_of(c * SC_GATHER_ROWS, SC_GATHER_ROWS)
            return pltpu.make_async_copy(table_hbm.at[idx_v.at[pl.ds(off, SC_GATHER_ROWS)]],
                                         rows_v.at[b], gather_sem.at[b])

        def write(c, b):
            off = pl.multiple_of(c * SC_GATHER_ROWS, SC_GATHER_ROWS)
            return pltpu.make_async_copy(rows_v.at[b], out_hbm.at[pl.ds(base + off, SC_GATHER_ROWS)],
                                         write_sem.at[b])

        gather(0, 0).start()

        @pl.loop(0, n_chunks, step=2)
        def _(c0):
            for b in range(2):
                c = c0 + b
                gather(c, b).wait()

                @pl.when(c >= 1)
                def _():
                    write(c - 1, 1 - b).wait()

                @pl.when(c + 1 < n_chunks)
                def _():
                    gather(c + 1, 1 - b).start()

                write(c, b).start()

        write(n_chunks - 1, (n_chunks - 1) % 2).wait()

    return pl.kernel(
        body, mesh=_sc_mesh(),
        out_type=jax.ShapeDtypeStruct((n_idx, width), table.dtype),
        scratch_types=[pltpu.VMEM((per_worker,), I32),
                       pltpu.VMEM((2, SC_GATHER_ROWS, width), table.dtype),
                       pltpu.SemaphoreType.DMA((2,)),
                       pltpu.SemaphoreType.DMA((2,))],
        name="sc_gather_rows",
    )(table, idx)


def _combine_kernel(w_ref, h_ref, g2_ref, nf_ref, yk_ref, y_ref):
    t = h_ref.shape[0]
    w = w_ref[...]
    acc_lo = jnp.zeros((t, D_MODEL // 2), F32)
    acc_hi = jnp.zeros((t, D_MODEL // 2), F32)
    for kk in range(TOP_K):
        lo, hi = _unpack_bf16_pair(yk_ref[kk])
        wk = w[:, kk:kk + 1]
        acc_lo = acc_lo + wk * lo
        acc_hi = acc_hi + wk * hi
    out = h_ref[...] + g2_ref[0] * jnp.concatenate([acc_lo, acc_hi], axis=1)
    y_ref[...] = out * lax.rsqrt(jnp.mean(out * out, axis=-1, keepdims=True) + EPS) * nf_ref[...]


def _combine(w, h2, g2, normf, y_by_k, row_offset, tiles_per_batch):
    n = h2.shape[0]
    t = MOVE_TILE
    off = row_offset // t
    mod_rows = g2.shape[1]
    mod_tiles = max(ROW_TILE // t, 1) * tiles_per_batch if mod_rows == 1 else n // t
    return pl.pallas_call(
        _combine_kernel,
        grid=(n // t,),
        in_specs=[pl.BlockSpec((t, LANES), lambda i: (i + off, 0)),
                  pl.BlockSpec((t, D_MODEL), lambda i: (i, 0)),
                  pl.BlockSpec((1, mod_rows if mod_rows == 1 else t, D_MODEL),
                               (lambda i: (i // mod_tiles, 0, 0)) if mod_rows == 1
                               else (lambda i: (0, i, 0))),
                  pl.BlockSpec((1, D_MODEL), lambda i: (0, 0)),
                  pl.BlockSpec((TOP_K, t, D_MODEL // 2), lambda i: (0, i + off, 0))],
        out_specs=pl.BlockSpec((t, D_MODEL), lambda i: (i, 0)),
        out_shape=jax.ShapeDtypeStruct((n, D_MODEL), F32),
        compiler_params=_cparams(("arbitrary",), 40),
        name="combine",
    )(w, h2, g2, normf, y_by_k)


def _rotary_tables(pos):
    half = KEY_DIM_R // 2
    inv_freq = ROPE_BASE ** (-jnp.arange(half, dtype=F32) / half)
    ang = pos[:, None] * inv_freq[None, :]
    cos = jnp.cos(ang)
    sin = jnp.sin(ang)
    cos_t = jnp.tile(jnp.concatenate([cos, cos], axis=1), (1, N_HEADS))
    sin_t = jnp.tile(jnp.concatenate([-sin, sin], axis=1), (1, N_HEADS))
    return cos_t.astype(F32), sin_t.astype(F32)


def _rel_bias_table(rel_bias, n_rows, n_cols, q_offset):
    heads = rel_bias.shape[0]
    n_diag = n_rows + n_cols - 1
    dist = q_offset + (n_rows - 1) - np.arange(n_diag)
    idx = np.clip(dist, -REL_CLIP, REL_CLIP) + REL_CLIP
    n_hi = int(np.sum(dist > REL_CLIP))
    n_lo = int(np.sum(dist < -REL_CLIP))
    mid = rel_bias[:, int(idx[n_diag - n_lo - 1]):int(idx[n_hi]) + 1][:, ::-1]
    diag = jnp.concatenate([jnp.broadcast_to(rel_bias[:, 2 * REL_CLIP:], (heads, n_hi)), mid,
                            jnp.broadcast_to(rel_bias[:, :1], (heads, n_lo))], axis=1)
    period = n_diag + 1
    v = jnp.roll(jnp.pad(diag, ((0, 0), (0, 1))), -(n_rows - 1), axis=1)
    skew = jnp.tile(v, (1, n_rows))[:, :n_rows * (period - 1)].reshape(heads, n_rows, period - 1)
    return skew[:, :, :n_cols].astype(F32)


def _prompt_bias(rel_bias):
    n_cols = ATT_QB + ATT_WINDOW
    r = np.arange(ATT_QB)[:, None]
    c = np.arange(n_cols)[None, :]
    band = c - (r // CHUNK) * CHUNK
    valid = (band >= 0) & (band < ATT_WINDOW + CHUNK)
    table = _rel_bias_table(rel_bias, ATT_QB, n_cols, ATT_WINDOW)
    return jnp.where(jnp.asarray(valid)[None], table, NEG_BIG)


def _sample_bias(rel_bias, t_new, cache_len):
    b = _rel_bias_table(rel_bias, t_new, cache_len + t_new, cache_len)
    return b[:, :, :cache_len], b[:, :, cache_len:]


def _mod_parts(mod, rows_each):
    parts = jnp.split(mod, 6, axis=-1)
    if rows_each == 1:
        return [p[:, None, :] for p in parts]
    return [jnp.repeat(p, rows_each, axis=0)[None] for p in parts]


def kernel(x_prompt, x_sample, cache_attn_k, cache_attn_v, state_ret, c_prompt, c_sample,
           norm1_g, norm2_g, w_ada, b_ada, w_in, rel_bias, w_o_attn, w_o_ret, w_out,
           w_router, b_router, w_exp_gate, w_exp_up, w_exp_down, w_sh_gate, w_sh_up, w_sh_down,
           normf_g):
    batch, seq, d = x_prompt.shape
    dec_batch, dec_seq, _ = x_sample.shape
    depth = w_in.shape[0]
    assert depth == 1 and d == D_MODEL
    assert seq % ROW_TILE == 0 and dec_batch * dec_seq == ROW_TILE and ROW_TILE == ATT_WINDOW
    cache_len = cache_attn_k.shape[2]
    n_p = batch * seq
    n_s = dec_batch * dec_seq
    tpb = seq // ROW_TILE
    l = 0

    bf = lambda a: a.astype(BF16)
    c_all = jnp.concatenate([c_prompt, c_sample], axis=0)
    pad = (-c_all.shape[0]) % 8
    c_all = jnp.pad(c_all, ((0, pad), (0, 0)))
    mod = _ada(c_all, bf(w_ada[l]), b_ada[l][None, :])
    mod_p = _mod_parts(mod[:batch], 1)
    mod_s = _mod_parts(mod[batch:batch + dec_batch], dec_seq)

    w_in_bf = bf(w_in[l])
    n1g = norm1_g[l][None, :]
    n2g = norm2_g[l][None, :]
    dense_w = [bf(w_o_attn[l]), bf(w_o_ret[l]), bf(w_out[l]), bf(w_router[l]).T,
               bf(w_sh_gate[l]), bf(w_sh_up[l]), bf(w_sh_down[l])]

    xp = x_prompt.reshape(n_p, d)
    xs_ = x_sample.reshape(n_s, d)
    cos_p, sin_p = _rotary_tables(jnp.arange(seq, dtype=F32))
    pos_s = PAST_LEN + jnp.arange(dec_seq, dtype=F32)
    cos_s, sin_s = _rotary_tables(jnp.tile(pos_s, dec_batch))

    (qa, ka, va, qr, kr, vr, gr, ga, gb, kv_p) = _inproj(
        xp, mod_p[1], mod_p[0], n1g, cos_p, sin_p, w_in_bf, tpb)
    oa = _attn_prompt(qa, ka, va, _prompt_bias(rel_bias[l]), batch, seq)
    zero_state = jnp.zeros((batch, N_HEADS, KEY_DIM_R, VAL_DIM_R), F32)
    yr_in, state_p = _retention(qr, kr, vr, gr, zero_state, batch, seq, RET_CHUNK)
    h_p, n2p_p, s_p = _outproj(xp, oa, yr_in, ga, gb, mod_p[2], mod_p[4], mod_p[3], mod_p[5], n2g,
                               dense_w, tpb)

    (qa_s, ka_s, va_s, qr_s, kr_s, vr_s, gr_s, ga_s, gb_s, kv_s) = _inproj(
        xs_, mod_s[1], mod_s[0], n1g, cos_s, sin_s, w_in_bf, 1)
    bias_c, bias_n = _sample_bias(rel_bias[l], dec_seq, cache_len)
    to_keys_minor = lambda c: jnp.transpose(c, (0, 1, 3, 4, 2))
    oa_s = _attn_sample(qa_s, ka_s, va_s, to_keys_minor(cache_attn_k), to_keys_minor(cache_attn_v),
                        bias_c, bias_n, dec_batch, dec_seq, cache_len)
    yr_in_s, state_s = _retention(qr_s, kr_s, vr_s, gr_s, state_ret[l], dec_batch, dec_seq, dec_seq)
    h_s, n2p_s, s_s = _outproj(xs_, oa_s, yr_in_s, ga_s, gb_s, mod_s[2], mod_s[4], mod_s[3], mod_s[5],
                               n2g, dense_w, 1)

    n2p = jnp.concatenate([n2p_p, n2p_s], axis=0)
    scores_t = jnp.concatenate([s_p, s_s], axis=1)
    lanes_of = lambda v: jnp.broadcast_to(v[:, None], (N_EXPERTS, ROUTE_TILE))
    idx_t, w_t, rank_t, counts = _route(scores_t, lanes_of(b_router[l]))
    (tile_expert, n_tiles, expert_slot, expert_next, tile_halves, pstart, pad_rows,
     n_sorted_rows) = _ffn_plan(counts[:, 0], (n_p + n_s) * TOP_K)
    dest_t = _dest(idx_t, rank_t, lanes_of(pstart.astype(F32)))
    dest_kmajor = dest_t.reshape(-1)
    w_route = jnp.pad(w_t.T, ((0, 0), (0, LANES - TOP_K)))
    xs_sorted = _sc_dispatch(n2p, dest_kmajor, pad_rows, n_sorted_rows)
    ys_sorted = _ffn(tile_expert, n_tiles, expert_slot, expert_next, tile_halves, xs_sorted,
                     w_exp_gate[l], w_exp_up[l], w_exp_down[l])
    nf = normf_g[None, :]
    y_by_k = _sc_gather_rows(ys_sorted, dest_kmajor).reshape(TOP_K, n_p + n_s, d // 2)
    y_p = _combine(w_route, h_p, mod_p[5], nf, y_by_k, 0, tpb)
    y_s = _combine(w_route, h_s, mod_s[5], nf, y_by_k, n_p, 1)

    keep = min(ATT_WINDOW, seq)
    kv_p = kv_p.reshape(batch, ROW_TILE, 2, N_HEADS, HEAD_DIM_A)[:, ROW_TILE - keep:]
    kv_s = kv_s.reshape(dec_batch, dec_seq, 2, N_HEADS, HEAD_DIM_A)
    return (y_p.reshape(batch, seq, d), y_s.reshape(dec_batch, dec_seq, d),
            kv_p[:, :, 0][None], kv_p[:, :, 1][None], state_p[None],
            kv_s[:, :, 0][None], kv_s[:, :, 1][None], state_s[None])
```

```python
import functools

import numpy as np
import jax
import jax.numpy as jnp
from jax import lax
from jax.experimental import pallas as pl
from jax.experimental.pallas import tpu as pltpu
from jax.experimental.pallas import tpu_sc as plsc

F32 = jnp.float32
BF16 = jnp.bfloat16
I32 = jnp.int32
U32 = jnp.uint32

D_MODEL = 1024
PAST_LEN = 4096
CHUNK = 64
N_LEFT_CHUNKS = 8
ATT_WINDOW = N_LEFT_CHUNKS * CHUNK
N_HEADS = 8
HEAD_DIM_A = 64
D_ATT = N_HEADS * HEAD_DIM_A
REL_CLIP = 128
KEY_DIM_R = 64
VAL_DIM_R = 128
D_RET_K = N_HEADS * KEY_DIM_R
D_RET_V = N_HEADS * VAL_DIM_R
ROPE_BASE = 10000.0
N_EXPERTS = 256
TOP_K = 8
N_GROUPS = 8
GROUP_SIZE = N_EXPERTS // N_GROUPS
TOPK_GROUPS = 4
D_EXPERT = 256
ROUTED_SCALE = 2.5
EPS = 1e-6
IN_WIDTHS = (D_ATT, D_ATT, D_ATT, D_RET_K, D_RET_K, D_RET_V, D_RET_V, D_MODEL, D_MODEL)
IN_OFFS = tuple(int(v) for v in np.cumsum((0,) + IN_WIDTHS))
D_IN = IN_OFFS[-1]

NEG_BIG = -1e30
LANES = 128
V7X_VMEM_BYTES = 64 * 1024 * 1024
V7X_SC_CORES = 2
V7X_SC_SUBCORES = 16
SC_GATHER_ROWS = 64
SPARE_ROWS = 8192

ROW_TILE = 512
ATT_QB = 256
RET_CHUNK = 256
ROUTE_TILE = 512
MOVE_TILE = 256
FFN_QUANTUM = 128
FFN_HALF = 2 * FFN_QUANTUM
FFN_TILE = 2 * FFN_HALF


def _cparams(semantics, vmem_mb):
    return pltpu.CompilerParams(dimension_semantics=semantics,
                                vmem_limit_bytes=min(vmem_mb * 1024 * 1024, V7X_VMEM_BYTES - (6 << 20)))


def _silu(x):
    return x * jax.nn.sigmoid(x)


def _pack_bf16_pair(lo, hi):
    lo_b = pltpu.bitcast(lo.astype(BF16).astype(F32), U32) >> 16
    hi_b = pltpu.bitcast(hi.astype(BF16).astype(F32), U32) & jnp.uint32(0xFFFF0000)
    return lo_b | hi_b


def _unpack_bf16_pair(u):
    lo = pltpu.bitcast(u << 16, F32)
    hi = pltpu.bitcast(u & jnp.uint32(0xFFFF0000), F32)
    return lo, hi


def _ada_kernel(c_ref, w_ref, b_ref, o_ref):
    sc = _silu(c_ref[...]).astype(BF16)
    o_ref[...] = jnp.dot(sc, w_ref[...], preferred_element_type=F32) + b_ref[...]


def _ada(c_all, w_ada_bf, b_ada):
    rows = c_all.shape[0]
    n_out = w_ada_bf.shape[1]
    blk = D_MODEL
    return pl.pallas_call(
        _ada_kernel,
        grid=(n_out // blk,),
        in_specs=[pl.BlockSpec((rows, D_MODEL), lambda j: (0, 0)),
                  pl.BlockSpec((D_MODEL, blk), lambda j: (0, j)),
                  pl.BlockSpec((1, blk), lambda j: (0, j))],
        out_specs=pl.BlockSpec((rows, blk), lambda j: (0, j)),
        out_shape=jax.ShapeDtypeStruct((rows, n_out), F32),
        compiler_params=_cparams(("arbitrary",), 24),
        name="ada",
    )(c_all, w_ada_bf, b_ada)


def _inproj_kernel(x_ref, sc_ref, sh_ref, g_ref, cos_ref, sin_ref, w_ref,
                   qa_ref, ka_ref, va_ref, qr_ref, kr_ref, vr_ref, gr_ref, ga_ref, gb_ref,
                   kv_ref, *, tiles_per_batch):
    x = x_ref[...]
    xn = x * lax.rsqrt(jnp.mean(x * x, axis=-1, keepdims=True) + EPS) * g_ref[...]
    nb = (xn * (1.0 + sc_ref[0]) + sh_ref[0]).astype(BF16)

    def proj(seg):
        return jnp.dot(nb, w_ref[:, IN_OFFS[seg]:IN_OFFS[seg + 1]], preferred_element_type=F32)

    qa_ref[...] = proj(0).astype(BF16)
    ka = proj(1)
    va = proj(2)
    ka_ref[...] = ka.astype(BF16)
    va_ref[...] = va.astype(BF16)

    @pl.when(pl.program_id(0) % tiles_per_batch == tiles_per_batch - 1)
    def _():
        kv_ref[:, :D_ATT] = ka
        kv_ref[:, D_ATT:] = va

    cos = cos_ref[...]
    sin = sin_ref[...]
    first_half = (lax.broadcasted_iota(I32, (1, D_RET_K), 1) % KEY_DIM_R) < (KEY_DIM_R // 2)

    def rotary(t):
        partner = jnp.where(first_half, pltpu.roll(t, D_RET_K - KEY_DIM_R // 2, 1),
                            pltpu.roll(t, KEY_DIM_R // 2, 1))
        return t * cos + partner * sin

    qr_ref[...] = rotary(proj(3)).astype(BF16)
    kr_ref[...] = (rotary(proj(4)) * (KEY_DIM_R ** -0.5)).astype(BF16)
    vr_ref[...] = proj(5).astype(BF16)
    gr_ref[...] = proj(6).astype(BF16)
    ga_ref[...] = proj(7).astype(BF16)
    gb_ref[...] = proj(8).astype(BF16)


def _inproj(x2d, sc, sh, g, cos_t, sin_t, w_in_bf, tiles_per_batch):
    n = x2d.shape[0]
    tm = ROW_TILE
    n_tiles = n // tm
    n_batches = n_tiles // tiles_per_batch
    mod_rows = sc.shape[1]
    pos_tiles = cos_t.shape[0] // tm

    def row_spec(width):
        return pl.BlockSpec((tm, width), lambda i: (i, 0))

    mod_spec = pl.BlockSpec((1, mod_rows, D_MODEL), lambda i: (i // tiles_per_batch, 0, 0))
    pos_spec = pl.BlockSpec((tm, D_RET_K), lambda i: (i % pos_tiles, 0))
    out_widths = (D_ATT, D_ATT, D_ATT, D_RET_K, D_RET_K, D_RET_V, D_RET_V, D_MODEL, D_MODEL)
    out_shape = [jax.ShapeDtypeStruct((n, w), BF16) for w in out_widths]
    out_shape.append(jax.ShapeDtypeStruct((n_batches * tm, 2 * D_ATT), F32))
    out_specs = [row_spec(w) for w in out_widths]
    out_specs.append(pl.BlockSpec((tm, 2 * D_ATT), lambda i: (i // tiles_per_batch, 0)))
    return pl.pallas_call(
        functools.partial(_inproj_kernel, tiles_per_batch=tiles_per_batch),
        grid=(n_tiles,),
        in_specs=[row_spec(D_MODEL), mod_spec, mod_spec,
                  pl.BlockSpec((1, D_MODEL), lambda i: (0, 0)),
                  pos_spec, pos_spec,
                  pl.BlockSpec((D_MODEL, D_IN), lambda i: (0, 0))],
        out_specs=out_specs,
        out_shape=out_shape,
        compiler_params=_cparams(("arbitrary",), 56),
        name="inproj",
    )(x2d, sc, sh, g, cos_t, sin_t, w_in_bf)


def _softmax_pv(s, v_parts):
    m = functools.reduce(jnp.maximum, [jnp.max(t, axis=-1, keepdims=True) for t in s])
    ps = [jnp.exp(t - m) for t in s]
    l = functools.reduce(jnp.add, [jnp.sum(p, axis=-1, keepdims=True) for p in ps])
    o = functools.reduce(jnp.add, [jnp.dot(p.astype(BF16), v, preferred_element_type=F32)
                                   for p, v in zip(ps, v_parts)])
    return o / l


def _attn_prompt_kernel(q_ref, k0_ref, k1_ref, k2_ref, v0_ref, v1_ref, v2_ref, bias_ref, o_ref):
    j = pl.program_id(1)
    q = q_ref[...]
    k = jnp.concatenate([k0_ref[...], k1_ref[...], k2_ref[...]], axis=0)
    v = jnp.concatenate([v0_ref[...], v1_ref[...], v2_ref[...]], axis=0)
    n_keys = k.shape[0]
    key_block = lax.broadcasted_iota(I32, (1, n_keys), 1) // ATT_QB
    before_start = jnp.where(key_block < 2 - j, NEG_BIG, 0.0)
    outs = []
    for h in range(N_HEADS):
        sl = slice(h * HEAD_DIM_A, (h + 1) * HEAD_DIM_A)
        qh = (q[:, sl].astype(F32) * (HEAD_DIM_A ** -0.5)).astype(BF16)
        s = lax.dot_general(qh, k[:, sl], (((1,), (1,)), ((), ())), preferred_element_type=F32)
        s = s + bias_ref[h] + before_start
        outs.append(_softmax_pv([s], [v[:, sl]]))
    o_ref[...] = jnp.concatenate(outs, axis=1).astype(BF16)


def _attn_prompt(q, k, v, bias_full, batch, seq):
    qb = ATT_QB
    nq = seq // qb

    def q_map(b, j):
        return (b * nq + j, 0)

    def kv_map(back):
        return lambda b, j: (b * nq + jnp.maximum(j - back, 0), 0)

    blk = lambda m: pl.BlockSpec((qb, D_ATT), m)
    return pl.pallas_call(
        _attn_prompt_kernel,
        grid=(batch, nq),
        in_specs=[blk(q_map), blk(kv_map(2)), blk(kv_map(1)), blk(kv_map(0)),
                  blk(kv_map(2)), blk(kv_map(1)), blk(kv_map(0)),
                  pl.BlockSpec(bias_full.shape, lambda b, j: (0, 0, 0))],
        out_specs=blk(q_map),
        out_shape=jax.ShapeDtypeStruct((batch * seq, D_ATT), BF16),
        compiler_params=_cparams(("parallel", "arbitrary"), 40),
        name="attn_prompt",
    )(q, k, k, k, v, v, v, bias_full)


SAMPLE_ATT_BATCHES = 2


def _attn_sample_kernel(q_ref, kn_ref, vn_ref, ck_ref, cv_ref, bc_ref, bn_ref, o_ref, *, t_new):
    nt = (((1,), (1,)), ((), ()))
    for b in range(SAMPLE_ATT_BATCHES):
        rows = slice(b * t_new, (b + 1) * t_new)
        q = q_ref[rows, :]
        kn = kn_ref[rows, :]
        vn = vn_ref[rows, :]
        outs = []
        for h in range(N_HEADS):
            sl = slice(h * HEAD_DIM_A, (h + 1) * HEAD_DIM_A)
            qh = (q[:, sl].astype(F32) * (HEAD_DIM_A ** -0.5)).astype(BF16)
            kc_t = ck_ref[b, h].astype(BF16)
            vc_t = cv_ref[b, h].astype(BF16)
            s_c = jnp.dot(qh, kc_t, preferred_element_type=F32) + bc_ref[h]
            s_n = lax.dot_general(qh, kn[:, sl], nt, preferred_element_type=F32) + bn_ref[h]
            m = jnp.maximum(jnp.max(s_c, axis=-1, keepdims=True), jnp.max(s_n, axis=-1, keepdims=True))
            p_c = jnp.exp(s_c - m)
            p_n = jnp.exp(s_n - m)
            l = jnp.sum(p_c, axis=-1, keepdims=True) + jnp.sum(p_n, axis=-1, keepdims=True)
            o = (lax.dot_general(p_c.astype(BF16), vc_t, nt, preferred_element_type=F32)
                 + jnp.dot(p_n.astype(BF16), vn[:, sl], preferred_element_type=F32))
            outs.append(o / l)
        o_ref[rows, :] = jnp.concatenate(outs, axis=1).astype(BF16)


def _attn_sample(q, k, v, cache_k_t, cache_v_t, bias_cache, bias_new, batch, t_new, cache_len):
    nb = SAMPLE_ATT_BATCHES
    blk = pl.BlockSpec((nb * t_new, D_ATT), lambda b: (b, 0))
    cblk = pl.BlockSpec((None, nb, N_HEADS, HEAD_DIM_A, cache_len), lambda b: (0, b, 0, 0, 0))
    return pl.pallas_call(
        functools.partial(_attn_sample_kernel, t_new=t_new),
        grid=(batch // nb,),
        in_specs=[blk, blk, blk, cblk, cblk,
                  pl.BlockSpec(bias_cache.shape, lambda b: (0, 0, 0)),
                  pl.BlockSpec(bias_new.shape, lambda b: (0, 0, 0))],
        out_specs=blk,
        out_shape=jax.ShapeDtypeStruct((batch * t_new, D_ATT), BF16),
        compiler_params=_cparams(("arbitrary",), 40),
        name="attn_sample",
    )(q, k, v, cache_k_t, cache_v_t, bias_cache, bias_new)


def _ret_kernel(q_ref, k_ref, v_ref, g_ref, s0_ref, dmask_ref, qdec_ref, kdec_ref, sdec_ref,
                y_ref, sout_ref, state_ref):
    c = pl.program_id(1)

    @pl.when(c == 0)
    def _():
        state_ref[...] = s0_ref[0]

    q = q_ref[...]
    k = k_ref[...]
    v = v_ref[...]
    g = g_ref[...]
    outs = []
    for h in range(N_HEADS):
        ks = slice(h * KEY_DIM_R, (h + 1) * KEY_DIM_R)
        vs = slice(h * VAL_DIM_R, (h + 1) * VAL_DIM_R)
        qh, kh, vh = q[:, ks], k[:, ks], v[:, vs]
        scores = lax.dot_general(qh, kh, (((1,), (1,)), ((), ())), preferred_element_type=F32)
        inner = jnp.dot((scores * dmask_ref[h]).astype(BF16), vh, preferred_element_type=F32)
        state = state_ref[h]
        cross = jnp.dot(qh, state.astype(BF16), preferred_element_type=F32) * qdec_ref[h]
        o = inner + cross
        v_dec = (vh.astype(F32) * kdec_ref[h]).astype(BF16)
        state_ref[h] = sdec_ref[h] * state + lax.dot_general(
            kh, v_dec, (((0,), (0,)), ((), ())), preferred_element_type=F32)
        on = o * lax.rsqrt(jnp.mean(o * o, axis=-1, keepdims=True) + EPS)
        outs.append(on * _silu(g[:, vs].astype(F32)))
    y_ref[...] = jnp.concatenate(outs, axis=1).astype(BF16)

    @pl.when(c == pl.num_programs(1) - 1)
    def _():
        sout_ref[0] = state_ref[...]


def _ret_tables(chunk):
    log_g = jnp.log(1.0 - jnp.exp2(-5.0 - jnp.arange(N_HEADS, dtype=F32)))
    i = jnp.arange(chunk, dtype=F32)
    diff = i[:, None] - i[None, :]
    dmask = jnp.where(diff >= 0, jnp.exp(log_g[:, None, None] * jnp.maximum(diff, 0.0)), 0.0)
    qdec = jnp.exp(log_g[:, None] * (i + 1.0))
    kdec = jnp.exp(log_g[:, None] * (chunk - 1.0 - i))
    sdec = jnp.exp(log_g * chunk)
    bc = lambda t: jnp.broadcast_to(t[:, :, None], (N_HEADS, t.shape[1], VAL_DIM_R)).astype(F32)
    sdec_t = jnp.broadcast_to(sdec[:, None, None], (N_HEADS, 1, VAL_DIM_R)).astype(F32)
    return dmask.astype(F32), bc(qdec), bc(kdec), sdec_t


def _retention(q, k, v, gate, state0, batch, seq, chunk):
    nc = seq // chunk
    dmask, qdec, kdec, sdec = _ret_tables(chunk)
    row = lambda w: pl.BlockSpec((chunk, w), lambda b, c: (b * nc + c, 0))
    const = lambda a: pl.BlockSpec(a.shape, lambda b, c: (0,) * a.ndim)
    st_spec = pl.BlockSpec((1, N_HEADS, KEY_DIM_R, VAL_DIM_R), lambda b, c: (b, 0, 0, 0))
    return pl.pallas_call(
        _ret_kernel,
        grid=(batch, nc),
        in_specs=[row(D_RET_K), row(D_RET_K), row(D_RET_V), row(D_RET_V), st_spec,
                  const(dmask), const(qdec), const(kdec), const(sdec)],
        out_specs=[row(D_RET_V), st_spec],
        out_shape=[jax.ShapeDtypeStruct((batch * seq, D_RET_V), BF16),
                   jax.ShapeDtypeStruct((batch, N_HEADS, KEY_DIM_R, VAL_DIM_R), F32)],
        scratch_shapes=[pltpu.VMEM((N_HEADS, KEY_DIM_R, VAL_DIM_R), F32)],
        compiler_params=_cparams(("parallel", "arbitrary"), 32),
        name="retention",
    )(q, k, v, gate, state0, dmask, qdec, kdec, sdec)


def _outproj_kernel(x_ref, oa_ref, yr_ref, ga_ref, gb_ref, g1_ref, sc2_ref, sh2_ref, g2_ref, n2g_ref,
                    woa_ref, wor_ref, wout_ref, wrt_ref, wsg_ref, wsu_ref, wsd_ref,
                    h_ref, n2p_ref, s_ref):
    ya = jnp.dot(oa_ref[...], woa_ref[...], preferred_element_type=F32)
    yr = jnp.dot(yr_ref[...], wor_ref[...], preferred_element_type=F32)
    merged = (jax.nn.sigmoid(ga_ref[...].astype(F32)) * ya
              + jax.nn.sigmoid(gb_ref[...].astype(F32)) * yr)
    mix = jnp.dot(merged.astype(BF16), wout_ref[...], preferred_element_type=F32)
    h = x_ref[...] + g1_ref[0] * mix
    hn = h * lax.rsqrt(jnp.mean(h * h, axis=-1, keepdims=True) + EPS) * n2g_ref[...]
    n2 = hn * (1.0 + sc2_ref[0]) + sh2_ref[0]
    n2b = n2.astype(BF16)
    s_ref[...] = jax.nn.sigmoid(lax.dot_general(wrt_ref[...], n2b, (((1,), (1,)), ((), ())),
                                                preferred_element_type=F32))
    hid = _silu(jnp.dot(n2b, wsg_ref[...], preferred_element_type=F32)) * jnp.dot(
        n2b, wsu_ref[...], preferred_element_type=F32)
    shared = jnp.dot(hid.astype(BF16), wsd_ref[...], preferred_element_type=F32)
    h_ref[...] = h + g2_ref[0] * shared
    half = D_MODEL // 2
    n2p_ref[...] = _pack_bf16_pair(n2[:, :half], n2[:, half:])


def _outproj(x2d, oa, yr_in, ga, gb, g1, sc2, sh2, g2, n2g, weights, tiles_per_batch):
    n = x2d.shape[0]
    tm = ROW_TILE
    mod_rows = g1.shape[1]
    row = lambda w: pl.BlockSpec((tm, w), lambda i: (i, 0))
    mod_spec = pl.BlockSpec((1, mod_rows, D_MODEL), lambda i: (i // tiles_per_batch, 0, 0))
    const = lambda a: pl.BlockSpec(a.shape, lambda i: (0,) * a.ndim)
    return pl.pallas_call(
        _outproj_kernel,
        grid=(n // tm,),
        in_specs=[row(D_MODEL), row(D_ATT), row(D_RET_V), row(D_MODEL), row(D_MODEL),
                  mod_spec, mod_spec, mod_spec, mod_spec, const(n2g)] + [const(w) for w in weights],
        out_specs=[row(D_MODEL), row(D_MODEL // 2), pl.BlockSpec((N_EXPERTS, tm), lambda i: (0, i))],
        out_shape=[jax.ShapeDtypeStruct((n, D_MODEL), F32),
                   jax.ShapeDtypeStruct((n, D_MODEL // 2), U32),
                   jax.ShapeDtypeStruct((N_EXPERTS, n), F32)],
        compiler_params=_cparams(("arbitrary",), 48),
        name="outproj",
    )(x2d, oa, yr_in, ga, gb, g1, sc2, sh2, g2, n2g, *weights)


def _route_kernel(s_ref, b_ref, idx_ref, w_ref, rank_ref, cnt_ref, run_ref, tri_ref):
    step = pl.program_id(0)
    t = s_ref.shape[1]

    @pl.when(step == 0)
    def _():
        run_ref[...] = jnp.zeros_like(run_ref)
        r = lax.broadcasted_iota(I32, (t, t), 0)
        c = lax.broadcasted_iota(I32, (t, t), 1)
        tri_ref[...] = jnp.where(r < c, 1.0, 0.0).astype(BF16)

    s = s_ref[...]
    sel = s + b_ref[...]
    row_f = lax.broadcasted_iota(I32, (N_EXPERTS, t), 0).astype(F32)

    def first_argmax(vals, rows):
        m = jnp.max(vals, axis=0, keepdims=True)
        pos = jnp.min(jnp.where(vals == m, rows, float(N_EXPERTS)), axis=0, keepdims=True)
        return m, pos

    gscore = []
    group_row = lax.broadcasted_iota(I32, (GROUP_SIZE, t), 0).astype(F32)
    for g in range(N_GROUPS):
        rows = slice(g * GROUP_SIZE, (g + 1) * GROUP_SIZE)
        m1, p1 = first_argmax(sel[rows], group_row)
        m2 = jnp.max(jnp.where(group_row == p1, -jnp.inf, sel[rows]), axis=0, keepdims=True)
        gscore.append(m1 + m2)
    cand_parts = []
    for g in range(N_GROUPS):
        rows = slice(g * GROUP_SIZE, (g + 1) * GROUP_SIZE)
        beaten_by = jnp.zeros((1, t), F32)
        for o in range(N_GROUPS):
            if o == g:
                continue
            wins = (gscore[o] > gscore[g]) if o > g else (gscore[o] >= gscore[g])
            beaten_by = beaten_by + jnp.where(wins, 1.0, 0.0)
        cand_parts.append(jnp.where(beaten_by < TOPK_GROUPS, sel[rows], -jnp.inf))
    cand = jnp.concatenate(cand_parts, axis=0)

    picked = jnp.zeros((N_EXPERTS, t), F32)
    idx_rows, w_rows = [], []
    for _ in range(TOP_K):
        _, pos = first_argmax(cand, row_f)
        hit = row_f == pos
        w_rows.append(jnp.sum(jnp.where(hit, s, 0.0), axis=0, keepdims=True))
        idx_rows.append(pos)
        picked = jnp.where(hit, 1.0, picked)
        cand = jnp.where(hit, -jnp.inf, cand)
    w_sum = functools.reduce(jnp.add, w_rows)

    before = jnp.dot(picked.astype(BF16), tri_ref[...], preferred_element_type=F32) + run_ref[...]
    run_ref[...] = run_ref[...] + jnp.sum(picked, axis=1, keepdims=True)
    rank_rows = [jnp.sum(jnp.where(row_f == idx_rows[kk], before, 0.0), axis=0, keepdims=True)
                 for kk in range(TOP_K)]

    idx_ref[...] = jnp.concatenate(idx_rows, axis=0).astype(I32)
    w_ref[...] = jnp.concatenate([w / w_sum * ROUTED_SCALE for w in w_rows], axis=0)
    rank_ref[...] = jnp.concatenate(rank_rows, axis=0).astype(I32)

    @pl.when(step == pl.num_programs(0) - 1)
    def _():
        cnt_ref[...] = run_ref[...].astype(I32)


def _route(scores_t, b_col):
    n = scores_t.shape[1]
    t = ROUTE_TILE
    col = pl.BlockSpec((TOP_K, t), lambda i: (0, i))
    const = pl.BlockSpec((N_EXPERTS, t), lambda i: (0, 0))
    return pl.pallas_call(
        _route_kernel,
        grid=(n // t,),
        in_specs=[pl.BlockSpec((N_EXPERTS, t), lambda i: (0, i)), const],
        out_specs=[col, col, col, const],
        out_shape=[jax.ShapeDtypeStruct((TOP_K, n), I32),
                   jax.ShapeDtypeStruct((TOP_K, n), F32),
                   jax.ShapeDtypeStruct((TOP_K, n), I32),
                   jax.ShapeDtypeStruct((N_EXPERTS, t), I32)],
        scratch_shapes=[pltpu.VMEM((N_EXPERTS, t), F32), pltpu.VMEM((t, t), BF16)],
        compiler_params=_cparams(("arbitrary",), 32),
        name="route",
    )(scores_t, b_col)


def _dest_kernel(idx_ref, rank_ref, start_ref, dest_ref):
    t = idx_ref.shape[1]
    row = lax.broadcasted_iota(I32, (N_EXPERTS, t), 0)
    starts = start_ref[...]
    base = [jnp.sum(jnp.where(row == idx_ref[kk:kk + 1, :], starts, 0.0), axis=0, keepdims=True)
            for kk in range(TOP_K)]
    dest_ref[...] = jnp.concatenate(base, axis=0).astype(I32) + rank_ref[...]


def _dest(idx_t, rank_t, starts_col):
    n = idx_t.shape[1]
    t = ROUTE_TILE
    col = pl.BlockSpec((TOP_K, t), lambda i: (0, i))
    return pl.pallas_call(
        _dest_kernel,
        grid=(n // t,),
        in_specs=[col, col, pl.BlockSpec((N_EXPERTS, t), lambda i: (0, 0))],
        out_specs=col,
        out_shape=jax.ShapeDtypeStruct((TOP_K, n), I32),
        compiler_params=_cparams(("arbitrary",), 32),
        name="dest",
    )(idx_t, rank_t, starts_col)


def _sc_mesh():
    return plsc.VectorSubcoreMesh(core_axis_name="core", subcore_axis_name="subcore",
                                  num_cores=V7X_SC_CORES, num_subcores=V7X_SC_SUBCORES)


def _sc_dispatch(n2p, dest_kmajor, pad_rows, n_out_rows):
    n, width = n2p.shape
    rows = SC_GATHER_ROWS
    n_workers = V7X_SC_CORES * V7X_SC_SUBCORES
    src_chunks = n // rows
    items = dest_kmajor.shape[0] // rows
    per_worker = items // n_workers
    pad_per_worker = pad_rows.shape[0] // rows // n_workers
    assert src_chunks * rows == n and per_worker * n_workers == items and per_worker % 2 == 0
    assert pad_per_worker * n_workers * rows == pad_rows.shape[0]
    idx3 = dest_kmajor.reshape(n_workers, per_worker, rows)
    pad3 = pad_rows.reshape(n_workers, pad_per_worker, rows)
    zeros = jnp.zeros((rows, width), n2p.dtype)

    def body(src_hbm, idx_hbm, pad_hbm, zero_hbm, out_hbm, idx_v, pad_v, rows_v, load_sem, scat_sem):
        worker = lax.axis_index("subcore") * V7X_SC_CORES + lax.axis_index("core")
        pltpu.sync_copy(idx_hbm.at[worker], idx_v)
        pltpu.sync_copy(pad_hbm.at[worker], pad_v)
        pltpu.sync_copy(zero_hbm, rows_v.at[0])

        def zero_fill(c):
            return pltpu.make_async_copy(rows_v.at[0], out_hbm.at[pad_v.at[c]], scat_sem.at[0])

        @pl.loop(0, pad_per_worker)
        def _(c):
            zero_fill(c).start()

        @pl.loop(0, pad_per_worker)
        def _(c):
            zero_fill(c).wait()

        def load(c, b):
            chunk = lax.rem(worker * per_worker + c, src_chunks)
            off = pl.multiple_of(chunk * rows, rows)
            return pltpu.make_async_copy(src_hbm.at[pl.ds(off, rows)], rows_v.at[b], load_sem.at[b])

        def scatter(c, b):
            return pltpu.make_async_copy(rows_v.at[b], out_hbm.at[idx_v.at[c]], scat_sem.at[b])

        load(0, 0).start()

        @pl.loop(0, per_worker, step=2)
        def _(c0):
            for b in range(2):
                c = c0 + b
                load(c, b).wait()

                @pl.when(c >= 1)
                def _():
                    scatter(c - 1, 1 - b).wait()

                @pl.when(c + 1 < per_worker)
                def _():
                    load(c + 1, 1 - b).start()

                scatter(c, b).start()

        scatter(per_worker - 1, (per_worker - 1) % 2).wait()

    return pl.kernel(
        body, mesh=_sc_mesh(),
        out_type=jax.ShapeDtypeStruct((n_out_rows, width), n2p.dtype),
        scratch_types=[pltpu.VMEM((per_worker, rows), I32),
                       pltpu.VMEM((pad_per_worker, rows), I32),
                       pltpu.VMEM((2, rows, width), n2p.dtype),
                       pltpu.SemaphoreType.DMA((2,)),
                       pltpu.SemaphoreType.DMA((2,))],
        name="sc_dispatch",
    )(n2p, idx3, pad3, zeros)


def _ffn_kernel(texp_ref, ntiles_ref, eslot_ref, enext_ref, nhalf_ref, xs_ref, wg_hbm, wu_hbm, wd_hbm,
                ys_ref, wg_buf, wu_buf, wd_buf, wgu_bf, wd_bf, sems):
    g = pl.program_id(0)

    def weight_copies(e, slot):
        return (pltpu.make_async_copy(wg_hbm.at[e], wg_buf.at[slot], sems.at[slot, 0]),
                pltpu.make_async_copy(wu_hbm.at[e], wu_buf.at[slot], sems.at[slot, 1]),
                pltpu.make_async_copy(wd_hbm.at[e], wd_buf.at[slot], sems.at[slot, 2]))

    @pl.when(g < ntiles_ref[0])
    def _():
        e = texp_ref[g]
        changed = jnp.logical_or(g == 0, texp_ref[jnp.maximum(g - 1, 0)] != e)

        @pl.when(changed)
        def _():
            slot = eslot_ref[e]

            @pl.when(g == 0)
            def _():
                for c in weight_copies(e, slot):
                    c.start()

            for c in weight_copies(e, slot):
                c.wait()
            nxt = enext_ref[e]

            @pl.when(nxt >= 0)
            def _():
                for c in weight_copies(nxt, 1 - slot):
                    c.start()

            wgu_bf[:, :D_EXPERT] = wg_buf[slot].astype(BF16)
            wgu_bf[:, D_EXPERT:] = wu_buf[slot].astype(BF16)
            wd_bf[...] = wd_buf[slot].astype(BF16)

        def expert_rows(rows):
            lo, hi = _unpack_bf16_pair(xs_ref[rows, :])
            x = jnp.concatenate([lo, hi], axis=1).astype(BF16)
            gu = jnp.dot(x, wgu_bf[...], preferred_element_type=F32)
            hid = (_silu(gu[:, :D_EXPERT]) * gu[:, D_EXPERT:]).astype(BF16)
            y = jnp.dot(hid, wd_bf[...], preferred_element_type=F32)
            half = D_MODEL // 2
            ys_ref[rows, :] = _pack_bf16_pair(y[:, :half], y[:, half:])

        for n_groups in range(1, FFN_TILE // FFN_QUANTUM + 1):

            @pl.when(nhalf_ref[g] == n_groups)
            def _(n_groups=n_groups):
                used = n_groups * FFN_QUANTUM
                for start in range(0, used, FFN_HALF):
                    expert_rows(slice(start, min(start + FFN_HALF, used)))
                if used < FFN_TILE:
                    ys_ref[used:, :] = jnp.zeros((FFN_TILE - used, ys_ref.shape[1]), U32)


def _ffn(tile_expert, n_tiles, expert_slot, expert_next, tile_halves, xs, w_gate, w_up, w_down):
    rows, width = xs.shape
    m = FFN_TILE
    max_tiles = tile_expert.shape[0]
    row_map = lambda g, te, nt, es, en, nh: (jnp.minimum(g, nt[0] - 1), 0)
    hbm = pl.BlockSpec(memory_space=pl.ANY)
    grid_spec = pltpu.PrefetchScalarGridSpec(
        num_scalar_prefetch=5,
        grid=(max_tiles,),
        in_specs=[pl.BlockSpec((m, width), row_map), hbm, hbm, hbm],
        out_specs=pl.BlockSpec((m, width), row_map),
        scratch_shapes=[pltpu.VMEM((2, D_MODEL, D_EXPERT), F32),
                        pltpu.VMEM((2, D_MODEL, D_EXPERT), F32),
                        pltpu.VMEM((2, D_EXPERT, D_MODEL), F32),
                        pltpu.VMEM((D_MODEL, 2 * D_EXPERT), BF16),
                        pltpu.VMEM((D_EXPERT, D_MODEL), BF16),
                        pltpu.SemaphoreType.DMA((2, 3))],
    )
    return pl.pallas_call(
        _ffn_kernel,
        grid_spec=grid_spec,
        out_shape=jax.ShapeDtypeStruct((rows, width), U32),
        compiler_params=_cparams(("arbitrary",), 32),
        name="ffn",
    )(tile_expert, n_tiles, expert_slot, expert_next, tile_halves, xs, w_gate, w_up, w_down)


def _ffn_plan(counts, n_assign):
    m = FFN_TILE
    max_tiles = n_assign // m + N_EXPERTS
    padded = ((counts + m - 1) // m) * m
    pend = jnp.cumsum(padded).astype(I32)
    pstart = pend - padded
    n_tiles = pend[-1:] // m
    g = jnp.minimum(jnp.arange(max_tiles, dtype=I32), n_tiles - 1)
    tile_expert = jnp.sum((pend[None, :] <= (g * m)[:, None]).astype(I32), axis=1)
    tile_expert = jnp.minimum(tile_expert, N_EXPERTS - 1)
    vend = pstart + ((counts + FFN_QUANTUM - 1) // FFN_QUANTUM) * FFN_QUANTUM
    own = tile_expert[:, None] == jnp.arange(N_EXPERTS, dtype=I32)[None, :]
    tile_vend = jnp.sum(jnp.where(own, vend[None, :], 0), axis=1)
    tile_halves = (jnp.clip(tile_vend - g * m, 0, m) // FFN_QUANTUM).astype(I32)
    used = counts > 0
    expert_slot = ((jnp.cumsum(used.astype(I32)) - 1) % 2).astype(I32)
    ids = jnp.where(used, jnp.arange(N_EXPERTS, dtype=I32), N_EXPERTS)
    first_used_from = lax.cummin(ids, axis=0, reverse=True)
    nxt = jnp.concatenate([first_used_from[1:], jnp.full((1,), N_EXPERTS, I32)])
    expert_next = jnp.where(nxt < N_EXPERTS, nxt, -1).astype(I32)
    spare_row = max_tiles * m
    j = jnp.arange(FFN_QUANTUM, dtype=I32)[None, :]
    seg_end = (pstart + counts)[:, None]
    spare = spare_row + (jnp.arange(N_EXPERTS, dtype=I32)[:, None] * FFN_QUANTUM + j) % SPARE_ROWS
    pad_rows = jnp.where(j < (vend[:, None] - seg_end), seg_end + j, spare).astype(I32).reshape(-1)
    return (tile_expert, n_tiles, expert_slot, expert_next, tile_halves, pstart, pad_rows,
            spare_row + SPARE_ROWS)


def _sc_gather_rows(table, idx):
    n_idx = idx.shape[0]
    width = table.shape[1]
    n_workers = V7X_SC_CORES * V7X_SC_SUBCORES
    per_worker = n_idx // n_workers
    n_chunks = per_worker // SC_GATHER_ROWS
    assert per_worker * n_workers == n_idx and n_chunks * SC_GATHER_ROWS == per_worker and n_chunks % 2 == 0

    def body(table_hbm, idx_hbm, out_hbm, idx_v, rows_v, gather_sem, write_sem):
        worker = lax.axis_index("subcore") * V7X_SC_CORES + lax.axis_index("core")
        base = worker * per_worker
        pltpu.sync_copy(idx_hbm.at[pl.ds(base, per_worker)], idx_v)

        def gather(c, b):
            off = pl.multiple_of(c * SC_GATHER_ROWS, SC_GATHER_ROWS)
            return pltpu.make_async_copy(table_hbm.at[idx_v.at[pl.ds(off, SC_GATHER_ROWS)]],
                                         rows_v.at[b], gather_sem.at[b])

        def write(c, b):
            off = pl.multiple_of(c * SC_GATHER_ROWS, SC_GATHER_ROWS)
            return pltpu.make_async_copy(rows_v.at[b], out_hbm.at[pl.ds(base + off, SC_GATHER_ROWS)],
                                         write_sem.at[b])

        gather(0, 0).start()

        @pl.loop(0, n_chunks, step=2)
        def _(c0):
            for b in range(2):
                c = c0 + b
                gather(c, b).wait()

                @pl.when(c >= 1)
                def _():
                    write(c - 1, 1 - b).wait()

                @pl.when(c + 1 < n_chunks)
                def _():
                    gather(c + 1, 1 - b).start()

                write(c, b).start()

        write(n_chunks - 1, (n_chunks - 1) % 2).wait()

    return pl.kernel(
        body, mesh=_sc_mesh(),
        out_type=jax.ShapeDtypeStruct((n_idx, width), table.dtype),
        scratch_types=[pltpu.VMEM((per_worker,), I32),
                       pltpu.VMEM((2, SC_GATHER_ROWS, width), table.dtype),
                       pltpu.SemaphoreType.DMA((2,)),
                       pltpu.SemaphoreType.DMA((2,))],
        name="sc_gather_rows",
    )(table, idx)


def _combine_kernel(w_ref, h_ref, g2_ref, nf_ref, yk_ref, y_ref):
    t = h_ref.shape[0]
    w = w_ref[...]
    acc_lo = jnp.zeros((t, D_MODEL // 2), F32)
    acc_hi = jnp.zeros((t, D_MODEL // 2), F32)
    for kk in range(TOP_K):
        lo, hi = _unpack_bf16_pair(yk_ref[kk])
        wk = w[:, kk:kk + 1]
        acc_lo = acc_lo + wk * lo
        acc_hi = acc_hi + wk * hi
    out = h_ref[...] + g2_ref[0] * jnp.concatenate([acc_lo, acc_hi], axis=1)
    y_ref[...] = out * lax.rsqrt(jnp.mean(out * out, axis=-1, keepdims=True) + EPS) * nf_ref[...]


def _combine(w, h2, g2, normf, y_by_k, row_offset, tiles_per_batch):
    n = h2.shape[0]
    t = MOVE_TILE
    off = row_offset // t
    mod_rows = g2.shape[1]
    mod_tiles = max(ROW_TILE // t, 1) * tiles_per_batch if mod_rows == 1 else n // t
    return pl.pallas_call(
        _combine_kernel,
        grid=(n // t,),
        in_specs=[pl.BlockSpec((t, LANES), lambda i: (i + off, 0)),
                  pl.BlockSpec((t, D_MODEL), lambda i: (i, 0)),
                  pl.BlockSpec((1, mod_rows if mod_rows == 1 else t, D_MODEL),
                               (lambda i: (i // mod_tiles, 0, 0)) if mod_rows == 1
                               else (lambda i: (0, i, 0))),
                  pl.BlockSpec((1, D_MODEL), lambda i: (0, 0)),
                  pl.BlockSpec((TOP_K, t, D_MODEL // 2), lambda i: (0, i + off, 0))],
        out_specs=pl.BlockSpec((t, D_MODEL), lambda i: (i, 0)),
        out_shape=jax.ShapeDtypeStruct((n, D_MODEL), F32),
        compiler_params=_cparams(("arbitrary",), 40),
        name="combine",
    )(w, h2, g2, normf, y_by_k)


def _rotary_tables(pos):
    half = KEY_DIM_R // 2
    inv_freq = ROPE_BASE ** (-jnp.arange(half, dtype=F32) / half)
    ang = pos[:, None] * inv_freq[None, :]
    cos = jnp.cos(ang)
    sin = jnp.sin(ang)
    cos_t = jnp.tile(jnp.concatenate([cos, cos], axis=1), (1, N_HEADS))
    sin_t = jnp.tile(jnp.concatenate([-sin, sin], axis=1), (1, N_HEADS))
    return cos_t.astype(F32), sin_t.astype(F32)


def _rel_bias_table(rel_bias, n_rows, n_cols, q_offset):
    heads = rel_bias.shape[0]
    n_diag = n_rows + n_cols - 1
    dist = q_offset + (n_rows - 1) - np.arange(n_diag)
    idx = np.clip(dist, -REL_CLIP, REL_CLIP) + REL_CLIP
    n_hi = int(np.sum(dist > REL_CLIP))
    n_lo = int(np.sum(dist < -REL_CLIP))
    mid = rel_bias[:, int(idx[n_diag - n_lo - 1]):int(idx[n_hi]) + 1][:, ::-1]
    diag = jnp.concatenate([jnp.broadcast_to(rel_bias[:, 2 * REL_CLIP:], (heads, n_hi)), mid,
                            jnp.broadcast_to(rel_bias[:, :1], (heads, n_lo))], axis=1)
    period = n_diag + 1
    v = jnp.roll(jnp.pad(diag, ((0, 0), (0, 1))), -(n_rows - 1), axis=1)
    skew = jnp.tile(v, (1, n_rows))[:, :n_rows * (period - 1)].reshape(heads, n_rows, period - 1)
    return skew[:, :, :n_cols].astype(F32)


def _prompt_bias(rel_bias):
    n_cols = ATT_QB + ATT_WINDOW
    r = np.arange(ATT_QB)[:, None]
    c = np.arange(n_cols)[None, :]
    band = c - (r // CHUNK) * CHUNK
    valid = (band >= 0) & (band < ATT_WINDOW + CHUNK)
    table = _rel_bias_table(rel_bias, ATT_QB, n_cols, ATT_WINDOW)
    return jnp.where(jnp.asarray(valid)[None], table, NEG_BIG)


def _sample_bias(rel_bias, t_new, cache_len):
    b = _rel_bias_table(rel_bias, t_new, cache_len + t_new, cache_len)
    return b[:, :, :cache_len], b[:, :, cache_len:]


def _mod_parts(mod, rows_each):
    parts = jnp.split(mod, 6, axis=-1)
    if rows_each == 1:
        return [p[:, None, :] for p in parts]
    return [jnp.repeat(p, rows_each, axis=0)[None] for p in parts]


def kernel(x_prompt, x_sample, cache_attn_k, cache_attn_v, state_ret, c_prompt, c_sample,
           norm1_g, norm2_g, w_ada, b_ada, w_in, rel_bias, w_o_attn, w_o_ret, w_out,
           w_router, b_router, w_exp_gate, w_exp_up, w_exp_down, w_sh_gate, w_sh_up, w_sh_down,
           normf_g):
    batch, seq, d = x_prompt.shape
    dec_batch, dec_seq, _ = x_sample.shape
    depth = w_in.shape[0]
    assert depth == 1 and d == D_MODEL
    assert seq % ROW_TILE == 0 and dec_batch * dec_seq == ROW_TILE and ROW_TILE == ATT_WINDOW
    cache_len = cache_attn_k.shape[2]
    n_p = batch * seq
    n_s = dec_batch * dec_seq
    tpb = seq // ROW_TILE
    l = 0

    bf = lambda a: a.astype(BF16)
    c_all = jnp.concatenate([c_prompt, c_sample], axis=0)
    pad = (-c_all.shape[0]) % 8
    c_all = jnp.pad(c_all, ((0, pad), (0, 0)))
    mod = _ada(c_all, bf(w_ada[l]), b_ada[l][None, :])
    mod_p = _mod_parts(mod[:batch], 1)
    mod_s = _mod_parts(mod[batch:batch + dec_batch], dec_seq)

    w_in_bf = bf(w_in[l])
    n1g = norm1_g[l][None, :]
    n2g = norm2_g[l][None, :]
    dense_w = [bf(w_o_attn[l]), bf(w_o_ret[l]), bf(w_out[l]), bf(w_router[l]).T,
               bf(w_sh_gate[l]), bf(w_sh_up[l]), bf(w_sh_down[l])]

    xp = x_prompt.reshape(n_p, d)
    xs_ = x_sample.reshape(n_s, d)
    cos_p, sin_p = _rotary_tables(jnp.arange(seq, dtype=F32))
    pos_s = PAST_LEN + jnp.arange(dec_seq, dtype=F32)
    cos_s, sin_s = _rotary_tables(jnp.tile(pos_s, dec_batch))

    (qa, ka, va, qr, kr, vr, gr, ga, gb, kv_p) = _inproj(
        xp, mod_p[1], mod_p[0], n1g, cos_p, sin_p, w_in_bf, tpb)
    oa = _attn_prompt(qa, ka, va, _prompt_bias(rel_bias[l]), batch, seq)
    zero_state = jnp.zeros((batch, N_HEADS, KEY_DIM_R, VAL_DIM_R), F32)
    yr_in, state_p = _retention(qr, kr, vr, gr, zero_state, batch, seq, RET_CHUNK)
    h_p, n2p_p, s_p = _outproj(xp, oa, yr_in, ga, gb, mod_p[2], mod_p[4], mod_p[3], mod_p[5], n2g,
                               dense_w, tpb)

    (qa_s, ka_s, va_s, qr_s, kr_s, vr_s, gr_s, ga_s, gb_s, kv_s) = _inproj(
        xs_, mod_s[1], mod_s[0], n1g, cos_s, sin_s, w_in_bf, 1)
    bias_c, bias_n = _sample_bias(rel_bias[l], dec_seq, cache_len)
    to_keys_minor = lambda c: jnp.transpose(c, (0, 1, 3, 4, 2))
    oa_s = _attn_sample(qa_s, ka_s, va_s, to_keys_minor(cache_attn_k), to_keys_minor(cache_attn_v),
                        bias_c, bias_n, dec_batch, dec_seq, cache_len)
    yr_in_s, state_s = _retention(qr_s, kr_s, vr_s, gr_s, state_ret[l], dec_batch, dec_seq, dec_seq)
    h_s, n2p_s, s_s = _outproj(xs_, oa_s, yr_in_s, ga_s, gb_s, mod_s[2], mod_s[4], mod_s[3], mod_s[5],
                               n2g, dense_w, 1)

    n2p = jnp.concatenate([n2p_p, n2p_s], axis=0)
    scores_t = jnp.concatenate([s_p, s_s], axis=1)
    lanes_of = lambda v: jnp.broadcast_to(v[:, None], (N_EXPERTS, ROUTE_TILE))
    idx_t, w_t, rank_t, counts = _route(scores_t, lanes_of(b_router[l]))
    (tile_expert, n_tiles, expert_slot, expert_next, tile_halves, pstart, pad_rows,
     n_sorted_rows) = _ffn_plan(counts[:, 0], (n_p + n_s) * TOP_K)
    dest_t = _dest(idx_t, rank_t, lanes_of(pstart.astype(F32)))
    dest_kmajor = dest_t.reshape(-1)
    w_route = jnp.pad(w_t.T, ((0, 0), (0, LANES - TOP_K)))
    xs_sorted = _sc_dispatch(n2p, dest_kmajor, pad_rows, n_sorted_rows)
    ys_sorted = _ffn(tile_expert, n_tiles, expert_slot, expert_next, tile_halves, xs_sorted,
                     w_exp_gate[l], w_exp_up[l], w_exp_down[l])
    nf = normf_g[None, :]
    y_by_k = _sc_gather_rows(ys_sorted, dest_kmajor).reshape(TOP_K, n_p + n_s, d // 2)
    y_p = _combine(w_route, h_p, mod_p[5], nf, y_by_k, 0, tpb)
    y_s = _combine(w_route, h_s, mod_s[5], nf, y_by_k, n_p, 1)

    keep = min(ATT_WINDOW, seq)
    kv_p = kv_p.reshape(batch, ROW_TILE, 2, N_HEADS, HEAD_DIM_A)[:, ROW_TILE - keep:]
    kv_s = kv_s.reshape(dec_batch, dec_seq, 2, N_HEADS, HEAD_DIM_A)
    return (y_p.reshape(batch, seq, d), y_s.reshape(dec_batch, dec_seq, d),
            kv_p[:, :, 0][None], kv_p[:, :, 1][None], state_p[None],
            kv_s[:, :, 0][None], kv_s[:, :, 1][None], state_s[None])
```

```python
import functools

import numpy as np
import jax
import jax.numpy as jnp
from jax import lax
from jax.experimental import pallas as pl
from jax.experimental.pallas import tpu as pltpu
from jax.experimental.pallas import tpu_sc as plsc

F32 = jnp.float32
BF16 = jnp.bfloat16
I32 = jnp.int32
U32 = jnp.uint32

D_MODEL = 1024
PAST_LEN = 4096
CHUNK = 64
N_LEFT_CHUNKS = 8
ATT_WINDOW = N_LEFT_CHUNKS * CHUNK
N_HEADS = 8
HEAD_DIM_A = 64
D_ATT = N_HEADS * HEAD_DIM_A
REL_CLIP = 128
KEY_DIM_R = 64
VAL_DIM_R = 128
D_RET_K = N_HEADS * KEY_DIM_R
D_RET_V = N_HEADS * VAL_DIM_R
ROPE_BASE = 10000.0
N_EXPERTS = 256
TOP_K = 8
N_GROUPS = 8
GROUP_SIZE = N_EXPERTS // N_GROUPS
TOPK_GROUPS = 4
D_EXPERT = 256
ROUTED_SCALE = 2.5
EPS = 1e-6
IN_WIDTHS = (D_ATT, D_ATT, D_ATT, D_RET_K, D_RET_K, D_RET_V, D_RET_V, D_MODEL, D_MODEL)
IN_OFFS = tuple(int(v) for v in np.cumsum((0,) + IN_WIDTHS))
D_IN = IN_OFFS[-1]

NEG_BIG = -1e30
LANES = 128
V7X_VMEM_BYTES = 64 * 1024 * 1024
V7X_SC_CORES = 2
V7X_SC_SUBCORES = 16
SC_GATHER_ROWS = 64
SPARE_ROWS = 8192

ROW_TILE = 512
ATT_QB = 256
RET_CHUNK = 256
ROUTE_TILE = 512
MOVE_TILE = 256
FFN_QUANTUM = 128
FFN_MATMUL_ROWS = 2 * FFN_QUANTUM
FFN_TILE = 5 * FFN_QUANTUM


def _cparams(semantics, vmem_mb):
    return pltpu.CompilerParams(dimension_semantics=semantics,
                                vmem_limit_bytes=min(vmem_mb * 1024 * 1024, V7X_VMEM_BYTES - (6 << 20)))


def _silu(x):
    return x * jax.nn.sigmoid(x)


def _pack_bf16_pair(lo, hi):
    lo_b = pltpu.bitcast(lo.astype(BF16).astype(F32), U32) >> 16
    hi_b = pltpu.bitcast(hi.astype(BF16).astype(F32), U32) & jnp.uint32(0xFFFF0000)
    return lo_b | hi_b


def _unpack_bf16_pair(u):
    lo = pltpu.bitcast(u << 16, F32)
    hi = pltpu.bitcast(u & jnp.uint32(0xFFFF0000), F32)
    return lo, hi


def _ada_kernel(c_ref, w_ref, b_ref, o_ref):
    sc = _silu(c_ref[...]).astype(BF16)
    o_ref[...] = jnp.dot(sc, w_ref[...], preferred_element_type=F32) + b_ref[...]


def _ada(c_all, w_ada_bf, b_ada):
    rows = c_all.shape[0]
    n_out = w_ada_bf.shape[1]
    blk = D_MODEL
    return pl.pallas_call(
        _ada_kernel,
        grid=(n_out // blk,),
        in_specs=[pl.BlockSpec((rows, D_MODEL), lambda j: (0, 0)),
                  pl.BlockSpec((D_MODEL, blk), lambda j: (0, j)),
                  pl.BlockSpec((1, blk), lambda j: (0, j))],
        out_specs=pl.BlockSpec((rows, blk), lambda j: (0, j)),
        out_shape=jax.ShapeDtypeStruct((rows, n_out), F32),
        compiler_params=_cparams(("arbitrary",), 24),
        name="ada",
    )(c_all, w_ada_bf, b_ada)


def _inproj_kernel(x_ref, sc_ref, sh_ref, g_ref, cos_ref, sin_ref, w_ref,
                   qa_ref, ka_ref, va_ref, qr_ref, kr_ref, vr_ref, gr_ref, ga_ref, gb_ref,
                   kv_ref, *, tiles_per_batch):
    x = x_ref[...]
    xn = x * lax.rsqrt(jnp.mean(x * x, axis=-1, keepdims=True) + EPS) * g_ref[...]
    nb = (xn * (1.0 + sc_ref[0]) + sh_ref[0]).astype(BF16)

    def proj(seg):
        return jnp.dot(nb, w_ref[:, IN_OFFS[seg]:IN_OFFS[seg + 1]], preferred_element_type=F32)

    qa_ref[...] = proj(0).astype(BF16)
    ka = proj(1)
    va = proj(2)
    ka_ref[...] = ka.astype(BF16)
    va_ref[...] = va.astype(BF16)

    @pl.when(pl.program_id(0) % tiles_per_batch == tiles_per_batch - 1)
    def _():
        kv_ref[:, :D_ATT] = ka
        kv_ref[:, D_ATT:] = va

    cos = cos_ref[...]
    sin = sin_ref[...]
    first_half = (lax.broadcasted_iota(I32, (1, D_RET_K), 1) % KEY_DIM_R) < (KEY_DIM_R // 2)

    def rotary(t):
        partner = jnp.where(first_half, pltpu.roll(t, D_RET_K - KEY_DIM_R // 2, 1),
                            pltpu.roll(t, KEY_DIM_R // 2, 1))
        return t * cos + partner * sin

    qr_ref[...] = rotary(proj(3)).astype(BF16)
    kr_ref[...] = (rotary(proj(4)) * (KEY_DIM_R ** -0.5)).astype(BF16)
    vr_ref[...] = proj(5).astype(BF16)
    gr_ref[...] = proj(6).astype(BF16)
    ga_ref[...] = proj(7).astype(BF16)
    gb_ref[...] = proj(8).astype(BF16)


def _inproj(x2d, sc, sh, g, cos_t, sin_t, w_in_bf, tiles_per_batch):
    n = x2d.shape[0]
    tm = ROW_TILE
    n_tiles = n // tm
    n_batches = n_tiles // tiles_per_batch
    mod_rows = sc.shape[1]
    pos_tiles = cos_t.shape[0] // tm

    def row_spec(width):
        return pl.BlockSpec((tm, width), lambda i: (i, 0))

    mod_spec = pl.BlockSpec((1, mod_rows, D_MODEL), lambda i: (i // tiles_per_batch, 0, 0))
    pos_spec = pl.BlockSpec((tm, D_RET_K), lambda i: (i % pos_tiles, 0))
    out_widths = (D_ATT, D_ATT, D_ATT, D_RET_K, D_RET_K, D_RET_V, D_RET_V, D_MODEL, D_MODEL)
    out_shape = [jax.ShapeDtypeStruct((n, w), BF16) for w in out_widths]
    out_shape.append(jax.ShapeDtypeStruct((n_batches * tm, 2 * D_ATT), F32))
    out_specs = [row_spec(w) for w in out_widths]
    out_specs.append(pl.BlockSpec((tm, 2 * D_ATT), lambda i: (i // tiles_per_batch, 0)))
    return pl.pallas_call(
        functools.partial(_inproj_kernel, tiles_per_batch=tiles_per_batch),
        grid=(n_tiles,),
        in_specs=[row_spec(D_MODEL), mod_spec, mod_spec,
                  pl.BlockSpec((1, D_MODEL), lambda i: (0, 0)),
                  pos_spec, pos_spec,
                  pl.BlockSpec((D_MODEL, D_IN), lambda i: (0, 0))],
        out_specs=out_specs,
        out_shape=out_shape,
        compiler_params=_cparams(("arbitrary",), 56),
        name="inproj",
    )(x2d, sc, sh, g, cos_t, sin_t, w_in_bf)


def _softmax_pv(s, v_parts):
    m = functools.reduce(jnp.maximum, [jnp.max(t, axis=-1, keepdims=True) for t in s])
    ps = [jnp.exp(t - m) for t in s]
    l = functools.reduce(jnp.add, [jnp.sum(p, axis=-1, keepdims=True) for p in ps])
    o = functools.reduce(jnp.add, [jnp.dot(p.astype(BF16), v, preferred_element_type=F32)
                                   for p, v in zip(ps, v_parts)])
    return o / l


def _attn_prompt_kernel(q_ref, k0_ref, k1_ref, k2_ref, v0_ref, v1_ref, v2_ref, bias_ref, o_ref):
    j = pl.program_id(1)
    q = q_ref[...]
    k = jnp.concatenate([k0_ref[...], k1_ref[...], k2_ref[...]], axis=0)
    v = jnp.concatenate([v0_ref[...], v1_ref[...], v2_ref[...]], axis=0)
    n_keys = k.shape[0]
    key_block = lax.broadcasted_iota(I32, (1, n_keys), 1) // ATT_QB
    before_start = jnp.where(key_block < 2 - j, NEG_BIG, 0.0)
    outs = []
    for h in range(N_HEADS):
        sl = slice(h * HEAD_DIM_A, (h + 1) * HEAD_DIM_A)
        qh = (q[:, sl].astype(F32) * (HEAD_DIM_A ** -0.5)).astype(BF16)
        s = lax.dot_general(qh, k[:, sl], (((1,), (1,)), ((), ())), preferred_element_type=F32)
        s = s + bias_ref[h] + before_start
        outs.append(_softmax_pv([s], [v[:, sl]]))
    o_ref[...] = jnp.concatenate(outs, axis=1).astype(BF16)


def _attn_prompt(q, k, v, bias_full, batch, seq):
    qb = ATT_QB
    nq = seq // qb

    def q_map(b, j):
        return (b * nq + j, 0)

    def kv_map(back):
        return lambda b, j: (b * nq + jnp.maximum(j - back, 0), 0)

    blk = lambda m: pl.BlockSpec((qb, D_ATT), m)
    return pl.pallas_call(
        _attn_prompt_kernel,
        grid=(batch, nq),
        in_specs=[blk(q_map), blk(kv_map(2)), blk(kv_map(1)), blk(kv_map(0)),
                  blk(kv_map(2)), blk(kv_map(1)), blk(kv_map(0)),
                  pl.BlockSpec(bias_full.shape, lambda b, j: (0, 0, 0))],
        out_specs=blk(q_map),
        out_shape=jax.ShapeDtypeStruct((batch * seq, D_ATT), BF16),
        compiler_params=_cparams(("parallel", "arbitrary"), 40),
        name="attn_prompt",
    )(q, k, k, k, v, v, v, bias_full)


SAMPLE_ATT_BATCHES = 2


def _attn_sample_kernel(q_ref, kn_ref, vn_ref, ck_ref, cv_ref, bc_ref, bn_ref, o_ref, *, t_new):
    nt = (((1,), (1,)), ((), ()))
    for b in range(SAMPLE_ATT_BATCHES):
        rows = slice(b * t_new, (b + 1) * t_new)
        q = q_ref[rows, :]
        kn = kn_ref[rows, :]
        vn = vn_ref[rows, :]
        outs = []
        for h in range(N_HEADS):
            sl = slice(h * HEAD_DIM_A, (h + 1) * HEAD_DIM_A)
            qh = (q[:, sl].astype(F32) * (HEAD_DIM_A ** -0.5)).astype(BF16)
            kc_t = ck_ref[b, h].astype(BF16)
            vc_t = cv_ref[b, h].astype(BF16)
            s_c = jnp.dot(qh, kc_t, preferred_element_type=F32) + bc_ref[h]
            s_n = lax.dot_general(qh, kn[:, sl], nt, preferred_element_type=F32) + bn_ref[h]
            m = jnp.maximum(jnp.max(s_c, axis=-1, keepdims=True), jnp.max(s_n, axis=-1, keepdims=True))
            p_c = jnp.exp(s_c - m)
            p_n = jnp.exp(s_n - m)
            l = jnp.sum(p_c, axis=-1, keepdims=True) + jnp.sum(p_n, axis=-1, keepdims=True)
            o = (lax.dot_general(p_c.astype(BF16), vc_t, nt, preferred_element_type=F32)
                 + jnp.dot(p_n.astype(BF16), vn[:, sl], preferred_element_type=F32))
            outs.append(o / l)
        o_ref[rows, :] = jnp.concatenate(outs, axis=1).astype(BF16)


def _attn_sample(q, k, v, cache_k_t, cache_v_t, bias_cache, bias_new, batch, t_new, cache_len):
    nb = SAMPLE_ATT_BATCHES
    blk = pl.BlockSpec((nb * t_new, D_ATT), lambda b: (b, 0))
    cblk = pl.BlockSpec((None, nb, N_HEADS, HEAD_DIM_A, cache_len), lambda b: (0, b, 0, 0, 0))
    return pl.pallas_call(
        functools.partial(_attn_sample_kernel, t_new=t_new),
        grid=(batch // nb,),
        in_specs=[blk, blk, blk, cblk, cblk,
                  pl.BlockSpec(bias_cache.shape, lambda b: (0, 0, 0)),
                  pl.BlockSpec(bias_new.shape, lambda b: (0, 0, 0))],
        out_specs=blk,
        out_shape=jax.ShapeDtypeStruct((batch * t_new, D_ATT), BF16),
        compiler_params=_cparams(("arbitrary",), 40),
        name="attn_sample",
    )(q, k, v, cache_k_t, cache_v_t, bias_cache, bias_new)


def _ret_kernel(q_ref, k_ref, v_ref, g_ref, s0_ref, dmask_ref, qdec_ref, kdec_ref, sdec_ref,
                y_ref, sout_ref, state_ref):
    c = pl.program_id(1)

    @pl.when(c == 0)
    def _():
        state_ref[...] = s0_ref[0]

    q = q_ref[...]
    k = k_ref[...]
    v = v_ref[...]
    g = g_ref[...]
    outs = []
    for h in range(N_HEADS):
        ks = slice(h * KEY_DIM_R, (h + 1) * KEY_DIM_R)
        vs = slice(h * VAL_DIM_R, (h + 1) * VAL_DIM_R)
        qh, kh, vh = q[:, ks], k[:, ks], v[:, vs]
        scores = lax.dot_general(qh, kh, (((1,), (1,)), ((), ())), preferred_element_type=F32)
        inner = jnp.dot((scores * dmask_ref[h]).astype(BF16), vh, preferred_element_type=F32)
        state = state_ref[h]
        cross = jnp.dot(qh, state.astype(BF16), preferred_element_type=F32) * qdec_ref[h]
        o = inner + cross
        v_dec = (vh.astype(F32) * kdec_ref[h]).astype(BF16)
        state_ref[h] = sdec_ref[h] * state + lax.dot_general(
            kh, v_dec, (((0,), (0,)), ((), ())), preferred_element_type=F32)
        on = o * lax.rsqrt(jnp.mean(o * o, axis=-1, keepdims=True) + EPS)
        outs.append(on * _silu(g[:, vs].astype(F32)))
    y_ref[...] = jnp.concatenate(outs, axis=1).astype(BF16)

    @pl.when(c == pl.num_programs(1) - 1)
    def _():
        sout_ref[0] = state_ref[...]


def _ret_tables(chunk):
    log_g = jnp.log(1.0 - jnp.exp2(-5.0 - jnp.arange(N_HEADS, dtype=F32)))
    i = jnp.arange(chunk, dtype=F32)
    diff = i[:, None] - i[None, :]
    dmask = jnp.where(diff >= 0, jnp.exp(log_g[:, None, None] * jnp.maximum(diff, 0.0)), 0.0)
    qdec = jnp.exp(log_g[:, None] * (i + 1.0))
    kdec = jnp.exp(log_g[:, None] * (chunk - 1.0 - i))
    sdec = jnp.exp(log_g * chunk)
    bc = lambda t: jnp.broadcast_to(t[:, :, None], (N_HEADS, t.shape[1], VAL_DIM_R)).astype(F32)
    sdec_t = jnp.broadcast_to(sdec[:, None, None], (N_HEADS, 1, VAL_DIM_R)).astype(F32)
    return dmask.astype(F32), bc(qdec), bc(kdec), sdec_t


def _retention(q, k, v, gate, state0, batch, seq, chunk):
    nc = seq // chunk
    dmask, qdec, kdec, sdec = _ret_tables(chunk)
    row = lambda w: pl.BlockSpec((chunk, w), lambda b, c: (b * nc + c, 0))
    const = lambda a: pl.BlockSpec(a.shape, lambda b, c: (0,) * a.ndim)
    st_spec = pl.BlockSpec((1, N_HEADS, KEY_DIM_R, VAL_DIM_R), lambda b, c: (b, 0, 0, 0))
    return pl.pallas_call(
        _ret_kernel,
        grid=(batch, nc),
        in_specs=[row(D_RET_K), row(D_RET_K), row(D_RET_V), row(D_RET_V), st_spec,
                  const(dmask), const(qdec), const(kdec), const(sdec)],
        out_specs=[row(D_RET_V), st_spec],
        out_shape=[jax.ShapeDtypeStruct((batch * seq, D_RET_V), BF16),
                   jax.ShapeDtypeStruct((batch, N_HEADS, KEY_DIM_R, VAL_DIM_R), F32)],
        scratch_shapes=[pltpu.VMEM((N_HEADS, KEY_DIM_R, VAL_DIM_R), F32)],
        compiler_params=_cparams(("parallel", "arbitrary"), 32),
        name="retention",
    )(q, k, v, gate, state0, dmask, qdec, kdec, sdec)


def _outproj_kernel(x_ref, oa_ref, yr_ref, ga_ref, gb_ref, g1_ref, sc2_ref, sh2_ref, g2_ref, n2g_ref,
                    woa_ref, wor_ref, wout_ref, wrt_ref, wsg_ref, wsu_ref, wsd_ref,
                    h_ref, n2p_ref, s_ref):
    ya = jnp.dot(oa_ref[...], woa_ref[...], preferred_element_type=F32)
    yr = jnp.dot(yr_ref[...], wor_ref[...], preferred_element_type=F32)
    merged = (jax.nn.sigmoid(ga_ref[...].astype(F32)) * ya
              + jax.nn.sigmoid(gb_ref[...].astype(F32)) * yr)
    mix = jnp.dot(merged.astype(BF16), wout_ref[...], preferred_element_type=F32)
    h = x_ref[...] + g1_ref[0] * mix
    hn = h * lax.rsqrt(jnp.mean(h * h, axis=-1, keepdims=True) + EPS) * n2g_ref[...]
    n2 = hn * (1.0 + sc2_ref[0]) + sh2_ref[0]
    n2b = n2.astype(BF16)
    s_ref[...] = jax.nn.sigmoid(lax.dot_general(wrt_ref[...], n2b, (((1,), (1,)), ((), ())),
                                                preferred_element_type=F32))
    hid = _silu(jnp.dot(n2b, wsg_ref[...], preferred_element_type=F32)) * jnp.dot(
        n2b, wsu_ref[...], preferred_element_type=F32)
    shared = jnp.dot(hid.astype(BF16), wsd_ref[...], preferred_element_type=F32)
    h_ref[...] = h + g2_ref[0] * shared
    half = D_MODEL // 2
    n2p_ref[...] = _pack_bf16_pair(n2[:, :half], n2[:, half:])


def _outproj(x2d, oa, yr_in, ga, gb, g1, sc2, sh2, g2, n2g, weights, tiles_per_batch):
    n = x2d.shape[0]
    tm = ROW_TILE
    mod_rows = g1.shape[1]
    row = lambda w: pl.BlockSpec((tm, w), lambda i: (i, 0))
    mod_spec = pl.BlockSpec((1, mod_rows, D_MODEL), lambda i: (i // tiles_per_batch, 0, 0))
    const = lambda a: pl.BlockSpec(a.shape, lambda i: (0,) * a.ndim)
    return pl.pallas_call(
        _outproj_kernel,
        grid=(n // tm,),
        in_specs=[row(D_MODEL), row(D_ATT), row(D_RET_V), row(D_MODEL), row(D_MODEL),
                  mod_spec, mod_spec, mod_spec, mod_spec, const(n2g)] + [const(w) for w in weights],
        out_specs=[row(D_MODEL), row(D_MODEL // 2), pl.BlockSpec((N_EXPERTS, tm), lambda i: (0, i))],
        out_shape=[jax.ShapeDtypeStruct((n, D_MODEL), F32),
                   jax.ShapeDtypeStruct((n, D_MODEL // 2), U32),
                   jax.ShapeDtypeStruct((N_EXPERTS, n), F32)],
        compiler_params=_cparams(("arbitrary",), 48),
        name="outproj",
    )(x2d, oa, yr_in, ga, gb, g1, sc2, sh2, g2, n2g, *weights)


def _route_kernel(s_ref, b_ref, idx_ref, w_ref, rank_ref, cnt_ref, run_ref, tri_ref):
    step = pl.program_id(0)
    t = s_ref.shape[1]

    @pl.when(step == 0)
    def _():
        run_ref[...] = jnp.zeros_like(run_ref)
        r = lax.broadcasted_iota(I32, (t, t), 0)
        c = lax.broadcasted_iota(I32, (t, t), 1)
        tri_ref[...] = jnp.where(r < c, 1.0, 0.0).astype(BF16)

    s = s_ref[...]
    sel = s + b_ref[...]
    row_f = lax.broadcasted_iota(I32, (N_EXPERTS, t), 0).astype(F32)

    def first_argmax(vals, rows):
        m = jnp.max(vals, axis=0, keepdims=True)
        pos = jnp.min(jnp.where(vals == m, rows, float(N_EXPERTS)), axis=0, keepdims=True)
        return m, pos

    gscore = []
    group_row = lax.broadcasted_iota(I32, (GROUP_SIZE, t), 0).astype(F32)
    for g in range(N_GROUPS):
        rows = slice(g * GROUP_SIZE, (g + 1) * GROUP_SIZE)
        m1, p1 = first_argmax(sel[rows], group_row)
        m2 = jnp.max(jnp.where(group_row == p1, -jnp.inf, sel[rows]), axis=0, keepdims=True)
        gscore.append(m1 + m2)
    cand_parts = []
    for g in range(N_GROUPS):
        rows = slice(g * GROUP_SIZE, (g + 1) * GROUP_SIZE)
        beaten_by = jnp.zeros((1, t), F32)
        for o in range(N_GROUPS):
            if o == g:
                continue
            wins = (gscore[o] > gscore[g]) if o > g else (gscore[o] >= gscore[g])
            beaten_by = beaten_by + jnp.where(wins, 1.0, 0.0)
        cand_parts.append(jnp.where(beaten_by < TOPK_GROUPS, sel[rows], -jnp.inf))
    cand = jnp.concatenate(cand_parts, axis=0)

    picked = jnp.zeros((N_EXPERTS, t), F32)
    idx_rows, w_rows = [], []
    for _ in range(TOP_K):
        _, pos = first_argmax(cand, row_f)
        hit = row_f == pos
        w_rows.append(jnp.sum(jnp.where(hit, s, 0.0), axis=0, keepdims=True))
        idx_rows.append(pos)
        picked = jnp.where(hit, 1.0, picked)
        cand = jnp.where(hit, -jnp.inf, cand)
    w_sum = functools.reduce(jnp.add, w_rows)

    before = jnp.dot(picked.astype(BF16), tri_ref[...], preferred_element_type=F32) + run_ref[...]
    run_ref[...] = run_ref[...] + jnp.sum(picked, axis=1, keepdims=True)
    rank_rows = [jnp.sum(jnp.where(row_f == idx_rows[kk], before, 0.0), axis=0, keepdims=True)
                 for kk in range(TOP_K)]

    idx_ref[...] = jnp.concatenate(idx_rows, axis=0).astype(I32)
    w_ref[...] = jnp.concatenate([w / w_sum * ROUTED_SCALE for w in w_rows], axis=0)
    rank_ref[...] = jnp.concatenate(rank_rows, axis=0).astype(I32)

    @pl.when(step == pl.num_programs(0) - 1)
    def _():
        cnt_ref[...] = run_ref[...].astype(I32)


def _route(scores_t, b_col):
    n = scores_t.shape[1]
    t = ROUTE_TILE
    col = pl.BlockSpec((TOP_K, t), lambda i: (0, i))
    const = pl.BlockSpec((N_EXPERTS, t), lambda i: (0, 0))
    return pl.pallas_call(
        _route_kernel,
        grid=(n // t,),
        in_specs=[pl.BlockSpec((N_EXPERTS, t), lambda i: (0, i)), const],
        out_specs=[col, col, col, const],
        out_shape=[jax.ShapeDtypeStruct((TOP_K, n), I32),
                   jax.ShapeDtypeStruct((TOP_K, n), F32),
                   jax.ShapeDtypeStruct((TOP_K, n), I32),
                   jax.ShapeDtypeStruct((N_EXPERTS, t), I32)],
        scratch_shapes=[pltpu.VMEM((N_EXPERTS, t), F32), pltpu.VMEM((t, t), BF16)],
        compiler_params=_cparams(("arbitrary",), 32),
        name="route",
    )(scores_t, b_col)


def _dest_kernel(idx_ref, rank_ref, start_ref, dest_ref):
    t = idx_ref.shape[1]
    row = lax.broadcasted_iota(I32, (N_EXPERTS, t), 0)
    starts = start_ref[...]
    base = [jnp.sum(jnp.where(row == idx_ref[kk:kk + 1, :], starts, 0.0), axis=0, keepdims=True)
            for kk in range(TOP_K)]
    dest_ref[...] = jnp.concatenate(base, axis=0).astype(I32) + rank_ref[...]


def _dest(idx_t, rank_t, starts_col):
    n = idx_t.shape[1]
    t = ROUTE_TILE
    col = pl.BlockSpec((TOP_K, t), lambda i: (0, i))
    return pl.pallas_call(
        _dest_kernel,
        grid=(n // t,),
        in_specs=[col, col, pl.BlockSpec((N_EXPERTS, t), lambda i: (0, 0))],
        out_specs=col,
        out_shape=jax.ShapeDtypeStruct((TOP_K, n), I32),
        compiler_params=_cparams(("arbitrary",), 32),
        name="dest",
    )(idx_t, rank_t, starts_col)


def _sc_mesh():
    return plsc.VectorSubcoreMesh(core_axis_name="core", subcore_axis_name="subcore",
                                  num_cores=V7X_SC_CORES, num_subcores=V7X_SC_SUBCORES)


def _sc_dispatch(n2p, dest_kmajor, pad_rows, n_out_rows):
    n, width = n2p.shape
    rows = SC_GATHER_ROWS
    n_workers = V7X_SC_CORES * V7X_SC_SUBCORES
    src_chunks = n // rows
    items = dest_kmajor.shape[0] // rows
    per_worker = items // n_workers
    pad_per_worker = pad_rows.shape[0] // rows // n_workers
    assert src_chunks * rows == n and per_worker * n_workers == items and per_worker % 2 == 0
    assert pad_per_worker * n_workers * rows == pad_rows.shape[0]
    idx3 = dest_kmajor.reshape(n_workers, per_worker, rows)
    pad3 = pad_rows.reshape(n_workers, pad_per_worker, rows)
    zeros = jnp.zeros((rows, width), n2p.dtype)

    def body(src_hbm, idx_hbm, pad_hbm, zero_hbm, out_hbm, idx_v, pad_v, rows_v, load_sem, scat_sem):
        worker = lax.axis_index("subcore") * V7X_SC_CORES + lax.axis_index("core")
        pltpu.sync_copy(idx_hbm.at[worker], idx_v)
        pltpu.sync_copy(pad_hbm.at[worker], pad_v)
        pltpu.sync_copy(zero_hbm, rows_v.at[0])

        def zero_fill(c):
            return pltpu.make_async_copy(rows_v.at[0], out_hbm.at[pad_v.at[c]], scat_sem.at[0])

        @pl.loop(0, pad_per_worker)
        def _(c):
            zero_fill(c).start()

        @pl.loop(0, pad_per_worker)
        def _(c):
            zero_fill(c).wait()

        def load(c, b):
            chunk = lax.rem(worker * per_worker + c, src_chunks)
            off = pl.multiple_of(chunk * rows, rows)
            return pltpu.make_async_copy(src_hbm.at[pl.ds(off, rows)], rows_v.at[b], load_sem.at[b])

        def scatter(c, b):
            return pltpu.make_async_copy(rows_v.at[b], out_hbm.at[idx_v.at[c]], scat_sem.at[b])

        load(0, 0).start()

        @pl.loop(0, per_worker, step=2)
        def _(c0):
            for b in range(2):
                c = c0 + b
                load(c, b).wait()

                @pl.when(c >= 1)
                def _():
                    scatter(c - 1, 1 - b).wait()

                @pl.when(c + 1 < per_worker)
                def _():
                    load(c + 1, 1 - b).start()

                scatter(c, b).start()

        scatter(per_worker - 1, (per_worker - 1) % 2).wait()

    return pl.kernel(
        body, mesh=_sc_mesh(),
        out_type=jax.ShapeDtypeStruct((n_out_rows, width), n2p.dtype),
        scratch_types=[pltpu.VMEM((per_worker, rows), I32),
                       pltpu.VMEM((pad_per_worker, rows), I32),
                       pltpu.VMEM((2, rows, width), n2p.dtype),
                       pltpu.SemaphoreType.DMA((2,)),
                       pltpu.SemaphoreType.DMA((2,))],
        name="sc_dispatch",
    )(n2p, idx3, pad3, zeros)


def _ffn_kernel(texp_ref, ntiles_ref, eslot_ref, enext_ref, nhalf_ref, xs_ref, wg_hbm, wu_hbm, wd_hbm,
                ys_ref, wg_buf, wu_buf, wd_buf, wgu_bf, wd_bf, sems):
    g = pl.program_id(0)

    def weight_copies(e, slot):
        return (pltpu.make_async_copy(wg_hbm.at[e], wg_buf.at[slot], sems.at[slot, 0]),
                pltpu.make_async_copy(wu_hbm.at[e], wu_buf.at[slot], sems.at[slot, 1]),
                pltpu.make_async_copy(wd_hbm.at[e], wd_buf.at[slot], sems.at[slot, 2]))

    @pl.when(g < ntiles_ref[0])
    def _():
        e = texp_ref[g]
        changed = jnp.logical_or(g == 0, texp_ref[jnp.maximum(g - 1, 0)] != e)

        @pl.when(changed)
        def _():
            slot = eslot_ref[e]

            @pl.when(g == 0)
            def _():
                for c in weight_copies(e, slot):
                    c.start()

            for c in weight_copies(e, slot):
                c.wait()
            nxt = enext_ref[e]

            @pl.when(nxt >= 0)
            def _():
                for c in weight_copies(nxt, 1 - slot):
                    c.start()

            wgu_bf[:, :D_EXPERT] = wg_buf[slot].astype(BF16)
            wgu_bf[:, D_EXPERT:] = wu_buf[slot].astype(BF16)
            wd_bf[...] = wd_buf[slot].astype(BF16)

        def expert_rows(rows):
            lo, hi = _unpack_bf16_pair(xs_ref[rows, :])
            x = jnp.concatenate([lo, hi], axis=1).astype(BF16)
            gu = jnp.dot(x, wgu_bf[...], preferred_element_type=F32)
            hid = (_silu(gu[:, :D_EXPERT]) * gu[:, D_EXPERT:]).astype(BF16)
            y = jnp.dot(hid, wd_bf[...], preferred_element_type=F32)
            half = D_MODEL // 2
            ys_ref[rows, :] = _pack_bf16_pair(y[:, :half], y[:, half:])

        for n_groups in range(1, FFN_TILE // FFN_QUANTUM + 1):

            @pl.when(nhalf_ref[g] == n_groups)
            def _(n_groups=n_groups):
                used = n_groups * FFN_QUANTUM
                for start in range(0, used, FFN_MATMUL_ROWS):
                    expert_rows(slice(start, min(start + FFN_MATMUL_ROWS, used)))
                if used < FFN_TILE:
                    ys_ref[used:, :] = jnp.zeros((FFN_TILE - used, ys_ref.shape[1]), U32)


def _ffn(tile_expert, n_tiles, expert_slot, expert_next, tile_halves, xs, w_gate, w_up, w_down):
    rows, width = xs.shape
    m = FFN_TILE
    max_tiles = tile_expert.shape[0]
    row_map = lambda g, te, nt, es, en, nh: (jnp.minimum(g, nt[0] - 1), 0)
    hbm = pl.BlockSpec(memory_space=pl.ANY)
    grid_spec = pltpu.PrefetchScalarGridSpec(
        num_scalar_prefetch=5,
        grid=(max_tiles,),
        in_specs=[pl.BlockSpec((m, width), row_map), hbm, hbm, hbm],
        out_specs=pl.BlockSpec((m, width), row_map),
        scratch_shapes=[pltpu.VMEM((2, D_MODEL, D_EXPERT), F32),
                        pltpu.VMEM((2, D_MODEL, D_EXPERT), F32),
                        pltpu.VMEM((2, D_EXPERT, D_MODEL), F32),
                        pltpu.VMEM((D_MODEL, 2 * D_EXPERT), BF16),
                        pltpu.VMEM((D_EXPERT, D_MODEL), BF16),
                        pltpu.SemaphoreType.DMA((2, 3))],
    )
    return pl.pallas_call(
        _ffn_kernel,
        grid_spec=grid_spec,
        out_shape=jax.ShapeDtypeStruct((rows, width), U32),
        compiler_params=_cparams(("arbitrary",), 32),
        name="ffn",
    )(tile_expert, n_tiles, expert_slot, expert_next, tile_halves, xs, w_gate, w_up, w_down)


def _ffn_plan(counts, n_assign):
    m = FFN_TILE
    max_tiles = n_assign // m + N_EXPERTS
    padded = ((counts + m - 1) // m) * m
    pend = jnp.cumsum(padded).astype(I32)
    pstart = pend - padded
    n_tiles = pend[-1:] // m
    g = jnp.minimum(jnp.arange(max_tiles, dtype=I32), n_tiles - 1)
    tile_expert = jnp.sum((pend[None, :] <= (g * m)[:, None]).astype(I32), axis=1)
    tile_expert = jnp.minimum(tile_expert, N_EXPERTS - 1)
    vend = pstart + ((counts + FFN_QUANTUM - 1) // FFN_QUANTUM) * FFN_QUANTUM
    own = tile_expert[:, None] == jnp.arange(N_EXPERTS, dtype=I32)[None, :]
    tile_vend = jnp.sum(jnp.where(own, vend[None, :], 0), axis=1)
    tile_halves = (jnp.clip(tile_vend - g * m, 0, m) // FFN_QUANTUM).astype(I32)
    used = counts > 0
    expert_slot = ((jnp.cumsum(used.astype(I32)) - 1) % 2).astype(I32)
    ids = jnp.where(used, jnp.arange(N_EXPERTS, dtype=I32), N_EXPERTS)
    first_used_from = lax.cummin(ids, axis=0, reverse=True)
    nxt = jnp.concatenate([first_used_from[1:], jnp.full((1,), N_EXPERTS, I32)])
    expert_next = jnp.where(nxt < N_EXPERTS, nxt, -1).astype(I32)
    spare_row = max_tiles * m
    j = jnp.arange(FFN_QUANTUM, dtype=I32)[None, :]
    seg_end = (pstart + counts)[:, None]
    spare = spare_row + (jnp.arange(N_EXPERTS, dtype=I32)[:, None] * FFN_QUANTUM + j) % SPARE_ROWS
    pad_rows = jnp.where(j < (vend[:, None] - seg_end), seg_end + j, spare).astype(I32).reshape(-1)
    return (tile_expert, n_tiles, expert_slot, expert_next, tile_halves, pstart, pad_rows,
            spare_row + SPARE_ROWS)


def _sc_gather_rows(table, idx):
    n_idx = idx.shape[0]
    width = table.shape[1]
    n_workers = V7X_SC_CORES * V7X_SC_SUBCORES
    per_worker = n_idx // n_workers
    n_chunks = per_worker // SC_GATHER_ROWS
    assert per_worker * n_workers == n_idx and n_chunks * SC_GATHER_ROWS == per_worker and n_chunks % 2 == 0

    def body(table_hbm, idx_hbm, out_hbm, idx_v, rows_v, gather_sem, write_sem):
        worker = lax.axis_index("subcore") * V7X_SC_CORES + lax.axis_index("core")
        base = worker * per_worker
        pltpu.sync_copy(idx_hbm.at[pl.ds(base, per_worker)], idx_v)

        def gather(c, b):
            off = pl.multiple_of(c * SC_GATHER_ROWS, SC_GATHER_ROWS)
            return pltpu.make_async_copy(table_hbm.at[idx_v.at[pl.ds(off, SC_GATHER_ROWS)]],
                                         rows_v.at[b], gather_sem.at[b])

        def write(c, b):
            off = pl.multiple_of(c * SC_GATHER_ROWS, SC_GATHER_ROWS)
            return pltpu.make_async_copy(rows_v.at[b], out_hbm.at[pl.ds(base + off, SC_GATHER_ROWS)],
                                         write_sem.at[b])

        gather(0, 0).start()

        @pl.loop(0, n_chunks, step=2)
        def _(c0):
            for b in range(2):
                c = c0 + b
                gather(c, b).wait()

                @pl.when(c >= 1)
                def _():
                    write(c - 1, 1 - b).wait()

                @pl.when(c + 1 < n_chunks)
                def _():
                    gather(c + 1, 1 - b).start()

                write(c, b).start()

        write(n_chunks - 1, (n_chunks - 1) % 2).wait()

    return pl.kernel(
        body, mesh=_sc_mesh(),
        out_type=jax.ShapeDtypeStruct((n_idx, width), table.dtype),
        scratch_types=[pltpu.VMEM((per_worker,), I32),
                       pltpu.VMEM((2, SC_GATHER_ROWS, width), table.dtype),
                       pltpu.SemaphoreType.DMA((2,)),
                       pltpu.SemaphoreType.DMA((2,))],
        name="sc_gather_rows",
    )(table, idx)


def _combine_kernel(w_ref, h_ref, g2_ref, nf_ref, yk_ref, y_ref):
    t = h_ref.shape[0]
    w = w_ref[...]
    acc_lo = jnp.zeros((t, D_MODEL // 2), F32)
    acc_hi = jnp.zeros((t, D_MODEL // 2), F32)
    for kk in range(TOP_K):
        lo, hi = _unpack_bf16_pair(yk_ref[kk])
        wk = w[:, kk:kk + 1]
        acc_lo = acc_lo + wk * lo
        acc_hi = acc_hi + wk * hi
    out = h_ref[...] + g2_ref[0] * jnp.concatenate([acc_lo, acc_hi], axis=1)
    y_ref[...] = out * lax.rsqrt(jnp.mean(out * out, axis=-1, keepdims=True) + EPS) * nf_ref[...]


def _combine(w, h2, g2, normf, y_by_k, row_offset, tiles_per_batch):
    n = h2.shape[0]
    t = MOVE_TILE
    off = row_offset // t
    mod_rows = g2.shape[1]
    mod_tiles = max(ROW_TILE // t, 1) * tiles_per_batch if mod_rows == 1 else n // t
    return pl.pallas_call(
        _combine_kernel,
        grid=(n // t,),
        in_specs=[pl.BlockSpec((t, LANES), lambda i: (i + off, 0)),
                  pl.BlockSpec((t, D_MODEL), lambda i: (i, 0)),
                  pl.BlockSpec((1, mod_rows if mod_rows == 1 else t, D_MODEL),
                               (lambda i: (i // mod_tiles, 0, 0)) if mod_rows == 1
                               else (lambda i: (0, i, 0))),
                  pl.BlockSpec((1, D_MODEL), lambda i: (0, 0)),
                  pl.BlockSpec((TOP_K, t, D_MODEL // 2), lambda i: (0, i + off, 0))],
        out_specs=pl.BlockSpec((t, D_MODEL), lambda i: (i, 0)),
        out_shape=jax.ShapeDtypeStruct((n, D_MODEL), F32),
        compiler_params=_cparams(("arbitrary",), 40),
        name="combine",
    )(w, h2, g2, normf, y_by_k)


def _rotary_tables(pos):
    half = KEY_DIM_R // 2
    inv_freq = ROPE_BASE ** (-jnp.arange(half, dtype=F32) / half)
    ang = pos[:, None] * inv_freq[None, :]
    cos = jnp.cos(ang)
    sin = jnp.sin(ang)
    cos_t = jnp.tile(jnp.concatenate([cos, cos], axis=1), (1, N_HEADS))
    sin_t = jnp.tile(jnp.concatenate([-sin, sin], axis=1), (1, N_HEADS))
    return cos_t.astype(F32), sin_t.astype(F32)


def _rel_bias_table(rel_bias, n_rows, n_cols, q_offset):
    heads = rel_bias.shape[0]
    n_diag = n_rows + n_cols - 1
    dist = q_offset + (n_rows - 1) - np.arange(n_diag)
    idx = np.clip(dist, -REL_CLIP, REL_CLIP) + REL_CLIP
    n_hi = int(np.sum(dist > REL_CLIP))
    n_lo = int(np.sum(dist < -REL_CLIP))
    mid = rel_bias[:, int(idx[n_diag - n_lo - 1]):int(idx[n_hi]) + 1][:, ::-1]
    diag = jnp.concatenate([jnp.broadcast_to(rel_bias[:, 2 * REL_CLIP:], (heads, n_hi)), mid,
                            jnp.broadcast_to(rel_bias[:, :1], (heads, n_lo))], axis=1)
    period = n_diag + 1
    v = jnp.roll(jnp.pad(diag, ((0, 0), (0, 1))), -(n_rows - 1), axis=1)
    skew = jnp.tile(v, (1, n_rows))[:, :n_rows * (period - 1)].reshape(heads, n_rows, period - 1)
    return skew[:, :, :n_cols].astype(F32)


def _prompt_bias(rel_bias):
    n_cols = ATT_QB + ATT_WINDOW
    r = np.arange(ATT_QB)[:, None]
    c = np.arange(n_cols)[None, :]
    band = c - (r // CHUNK) * CHUNK
    valid = (band >= 0) & (band < ATT_WINDOW + CHUNK)
    table = _rel_bias_table(rel_bias, ATT_QB, n_cols, ATT_WINDOW)
    return jnp.where(jnp.asarray(valid)[None], table, NEG_BIG)


def _sample_bias(rel_bias, t_new, cache_len):
    b = _rel_bias_table(rel_bias, t_new, cache_len + t_new, cache_len)
    return b[:, :, :cache_len], b[:, :, cache_len:]


def _mod_parts(mod, rows_each):
    parts = jnp.split(mod, 6, axis=-1)
    if rows_each == 1:
        return [p[:, None, :] for p in parts]
    return [jnp.repeat(p, rows_each, axis=0)[None] for p in parts]


def kernel(x_prompt, x_sample, cache_attn_k, cache_attn_v, state_ret, c_prompt, c_sample,
           norm1_g, norm2_g, w_ada, b_ada, w_in, rel_bias, w_o_attn, w_o_ret, w_out,
           w_router, b_router, w_exp_gate, w_exp_up, w_exp_down, w_sh_gate, w_sh_up, w_sh_down,
           normf_g):
    batch, seq, d = x_prompt.shape
    dec_batch, dec_seq, _ = x_sample.shape
    depth = w_in.shape[0]
    assert depth == 1 and d == D_MODEL
    assert seq % ROW_TILE == 0 and dec_batch * dec_seq == ROW_TILE and ROW_TILE == ATT_WINDOW
    cache_len = cache_attn_k.shape[2]
    n_p = batch * seq
    n_s = dec_batch * dec_seq
    tpb = seq // ROW_TILE
    l = 0

    bf = lambda a: a.astype(BF16)
    c_all = jnp.concatenate([c_prompt, c_sample], axis=0)
    pad = (-c_all.shape[0]) % 8
    c_all = jnp.pad(c_all, ((0, pad), (0, 0)))
    mod = _ada(c_all, bf(w_ada[l]), b_ada[l][None, :])
    mod_p = _mod_parts(mod[:batch], 1)
    mod_s = _mod_parts(mod[batch:batch + dec_batch], dec_seq)

    w_in_bf = bf(w_in[l])
    n1g = norm1_g[l][None, :]
    n2g = norm2_g[l][None, :]
    dense_w = [bf(w_o_attn[l]), bf(w_o_ret[l]), bf(w_out[l]), bf(w_router[l]).T,
               bf(w_sh_gate[l]), bf(w_sh_up[l]), bf(w_sh_down[l])]

    xp = x_prompt.reshape(n_p, d)
    xs_ = x_sample.reshape(n_s, d)
    cos_p, sin_p = _rotary_tables(jnp.arange(seq, dtype=F32))
    pos_s = PAST_LEN + jnp.arange(dec_seq, dtype=F32)
    cos_s, sin_s = _rotary_tables(jnp.tile(pos_s, dec_batch))

    (qa, ka, va, qr, kr, vr, gr, ga, gb, kv_p) = _inproj(
        xp, mod_p[1], mod_p[0], n1g, cos_p, sin_p, w_in_bf, tpb)
    oa = _attn_prompt(qa, ka, va, _prompt_bias(rel_bias[l]), batch, seq)
    zero_state = jnp.zeros((batch, N_HEADS, KEY_DIM_R, VAL_DIM_R), F32)
    yr_in, state_p = _retention(qr, kr, vr, gr, zero_state, batch, seq, RET_CHUNK)
    h_p, n2p_p, s_p = _outproj(xp, oa, yr_in, ga, gb, mod_p[2], mod_p[4], mod_p[3], mod_p[5], n2g,
                               dense_w, tpb)

    (qa_s, ka_s, va_s, qr_s, kr_s, vr_s, gr_s, ga_s, gb_s, kv_s) = _inproj(
        xs_, mod_s[1], mod_s[0], n1g, cos_s, sin_s, w_in_bf, 1)
    bias_c, bias_n = _sample_bias(rel_bias[l], dec_seq, cache_len)
    to_keys_minor = lambda c: jnp.transpose(c, (0, 1, 3, 4, 2))
    oa_s = _attn_sample(qa_s, ka_s, va_s, to_keys_minor(cache_attn_k), to_keys_minor(cache_attn_v),
                        bias_c, bias_n, dec_batch, dec_seq, cache_len)
    yr_in_s, state_s = _retention(qr_s, kr_s, vr_s, gr_s, state_ret[l], dec_batch, dec_seq, dec_seq)
    h_s, n2p_s, s_s = _outproj(xs_, oa_s, yr_in_s, ga_s, gb_s, mod_s[2], mod_s[4], mod_s[3], mod_s[5],
                               n2g, dense_w, 1)

    n2p = jnp.concatenate([n2p_p, n2p_s], axis=0)
    scores_t = jnp.concatenate([s_p, s_s], axis=1)
    lanes_of = lambda v: jnp.broadcast_to(v[:, None], (N_EXPERTS, ROUTE_TILE))
    idx_t, w_t, rank_t, counts = _route(scores_t, lanes_of(b_router[l]))
    (tile_expert, n_tiles, expert_slot, expert_next, tile_halves, pstart, pad_rows,
     n_sorted_rows) = _ffn_plan(counts[:, 0], (n_p + n_s) * TOP_K)
    dest_t = _dest(idx_t, rank_t, lanes_of(pstart.astype(F32)))
    dest_kmajor = dest_t.reshape(-1)
    w_route = jnp.pad(w_t.T, ((0, 0), (0, LANES - TOP_K)))
    xs_sorted = _sc_dispatch(n2p, dest_kmajor, pad_rows, n_sorted_rows)
    ys_sorted = _ffn(tile_expert, n_tiles, expert_slot, expert_next, tile_halves, xs_sorted,
                     w_exp_gate[l], w_exp_up[l], w_exp_down[l])
    nf = normf_g[None, :]
    y_by_k = _sc_gather_rows(ys_sorted, dest_kmajor).reshape(TOP_K, n_p + n_s, d // 2)
    y_p = _combine(w_route, h_p, mod_p[5], nf, y_by_k, 0, tpb)
    y_s = _combine(w_route, h_s, mod_s[5], nf, y_by_k, n_p, 1)

    keep = min(ATT_WINDOW, seq)
    kv_p = kv_p.reshape(batch, ROW_TILE, 2, N_HEADS, HEAD_DIM_A)[:, ROW_TILE - keep:]
    kv_s = kv_s.reshape(dec_batch, dec_seq, 2, N_HEADS, HEAD_DIM_A)
    return (y_p.reshape(batch, seq, d), y_s.reshape(dec_batch, dec_seq, d),
            kv_p[:, :, 0][None], kv_p[:, :, 1][None], state_p[None],
            kv_s[:, :, 0][None], kv_s[:, :, 1][None], state_s[None])
```

```python
import functools

import numpy as np
import jax
import jax.numpy as jnp
from jax import lax
from jax.experimental import pallas as pl
from jax.experimental.pallas import tpu as pltpu
from jax.experimental.pallas import tpu_sc as plsc

F32 = jnp.float32
BF16 = jnp.bfloat16
I32 = jnp.int32
U32 = jnp.uint32

D_MODEL = 1024
PAST_LEN = 4096
CHUNK = 64
N_LEFT_CHUNKS = 8
ATT_WINDOW = N_LEFT_CHUNKS * CHUNK
N_HEADS = 8
HEAD_DIM_A = 64
D_ATT = N_HEADS * HEAD_DIM_A
REL_CLIP = 128
KEY_DIM_R = 64
VAL_DIM_R = 128
D_RET_K = N_HEADS * KEY_DIM_R
D_RET_V = N_HEADS * VAL_DIM_R
ROPE_BASE = 10000.0
N_EXPERTS = 256
TOP_K = 8
N_GROUPS = 8
GROUP_SIZE = N_EXPERTS // N_GROUPS
TOPK_GROUPS = 4
D_EXPERT = 256
ROUTED_SCALE = 2.5
EPS = 1e-6
IN_WIDTHS = (D_ATT, D_ATT, D_ATT, D_RET_K, D_RET_K, D_RET_V, D_RET_V, D_MODEL, D_MODEL)
IN_OFFS = tuple(int(v) for v in np.cumsum((0,) + IN_WIDTHS))
D_IN = IN_OFFS[-1]

NEG_BIG = -1e30
LANES = 128
V7X_VMEM_BYTES = 64 * 1024 * 1024
V7X_SC_CORES = 2
V7X_SC_SUBCORES = 16
SC_GATHER_ROWS = 64
SPARE_ROWS = 8192

ROW_TILE = 512
ATT_QB = 256
RET_CHUNK = 256
ROUTE_TILE = 512
MOVE_TILE = 256
FFN_QUANTUM = 128
FFN_MATMUL_ROWS = 2 * FFN_QUANTUM
FFN_TILE = 5 * FFN_QUANTUM
FFN_WEIGHT_SLOTS = 3


def _cparams(semantics, vmem_mb):
    return pltpu.CompilerParams(dimension_semantics=semantics,
                                vmem_limit_bytes=min(vmem_mb * 1024 * 1024, V7X_VMEM_BYTES - (6 << 20)))


def _silu(x):
    return x * jax.nn.sigmoid(x)


def _pack_bf16_pair(lo, hi):
    lo_b = pltpu.bitcast(lo.astype(BF16).astype(F32), U32) >> 16
    hi_b = pltpu.bitcast(hi.astype(BF16).astype(F32), U32) & jnp.uint32(0xFFFF0000)
    return lo_b | hi_b


def _unpack_bf16_pair(u):
    lo = pltpu.bitcast(u << 16, F32)
    hi = pltpu.bitcast(u & jnp.uint32(0xFFFF0000), F32)
    return lo, hi


def _ada_kernel(c_ref, w_ref, b_ref, o_ref):
    sc = _silu(c_ref[...]).astype(BF16)
    o_ref[...] = jnp.dot(sc, w_ref[...], preferred_element_type=F32) + b_ref[...]


def _ada(c_all, w_ada_bf, b_ada):
    rows = c_all.shape[0]
    n_out = w_ada_bf.shape[1]
    blk = D_MODEL
    return pl.pallas_call(
        _ada_kernel,
        grid=(n_out // blk,),
        in_specs=[pl.BlockSpec((rows, D_MODEL), lambda j: (0, 0)),
                  pl.BlockSpec((D_MODEL, blk), lambda j: (0, j)),
                  pl.BlockSpec((1, blk), lambda j: (0, j))],
        out_specs=pl.BlockSpec((rows, blk), lambda j: (0, j)),
        out_shape=jax.ShapeDtypeStruct((rows, n_out), F32),
        compiler_params=_cparams(("arbitrary",), 24),
        name="ada",
    )(c_all, w_ada_bf, b_ada)


def _inproj_kernel(x_ref, sc_ref, sh_ref, g_ref, cos_ref, sin_ref, w_ref,
                   qa_ref, ka_ref, va_ref, qr_ref, kr_ref, vr_ref, gr_ref, ga_ref, gb_ref,
                   kv_ref, *, tiles_per_batch):
    x = x_ref[...]
    xn = x * lax.rsqrt(jnp.mean(x * x, axis=-1, keepdims=True) + EPS) * g_ref[...]
    nb = (xn * (1.0 + sc_ref[0]) + sh_ref[0]).astype(BF16)

    def proj(seg):
        return jnp.dot(nb, w_ref[:, IN_OFFS[seg]:IN_OFFS[seg + 1]], preferred_element_type=F32)

    qa_ref[...] = proj(0).astype(BF16)
    ka = proj(1)
    va = proj(2)
    ka_ref[...] = ka.astype(BF16)
    va_ref[...] = va.astype(BF16)

    @pl.when(pl.program_id(0) % tiles_per_batch == tiles_per_batch - 1)
    def _():
        kv_ref[:, :D_ATT] = ka
        kv_ref[:, D_ATT:] = va

    cos = cos_ref[...]
    sin = sin_ref[...]
    first_half = (lax.broadcasted_iota(I32, (1, D_RET_K), 1) % KEY_DIM_R) < (KEY_DIM_R // 2)

    def rotary(t):
        partner = jnp.where(first_half, pltpu.roll(t, D_RET_K - KEY_DIM_R // 2, 1),
                            pltpu.roll(t, KEY_DIM_R // 2, 1))
        return t * cos + partner * sin

    qr_ref[...] = rotary(proj(3)).astype(BF16)
    kr_ref[...] = (rotary(proj(4)) * (KEY_DIM_R ** -0.5)).astype(BF16)
    vr_ref[...] = proj(5).astype(BF16)
    gr_ref[...] = proj(6).astype(BF16)
    ga_ref[...] = proj(7).astype(BF16)
    gb_ref[...] = proj(8).astype(BF16)


def _inproj(x2d, sc, sh, g, cos_t, sin_t, w_in_bf, tiles_per_batch):
    n = x2d.shape[0]
    tm = ROW_TILE
    n_tiles = n // tm
    n_batches = n_tiles // tiles_per_batch
    mod_rows = sc.shape[1]
    pos_tiles = cos_t.shape[0] // tm

    def row_spec(width):
        return pl.BlockSpec((tm, width), lambda i: (i, 0))

    mod_spec = pl.BlockSpec((1, mod_rows, D_MODEL), lambda i: (i // tiles_per_batch, 0, 0))
    pos_spec = pl.BlockSpec((tm, D_RET_K), lambda i: (i % pos_tiles, 0))
    out_widths = (D_ATT, D_ATT, D_ATT, D_RET_K, D_RET_K, D_RET_V, D_RET_V, D_MODEL, D_MODEL)
    out_shape = [jax.ShapeDtypeStruct((n, w), BF16) for w in out_widths]
    out_shape.append(jax.ShapeDtypeStruct((n_batches * tm, 2 * D_ATT), F32))
    out_specs = [row_spec(w) for w in out_widths]
    out_specs.append(pl.BlockSpec((tm, 2 * D_ATT), lambda i: (i // tiles_per_batch, 0)))
    return pl.pallas_call(
        functools.partial(_inproj_kernel, tiles_per_batch=tiles_per_batch),
        grid=(n_tiles,),
        in_specs=[row_spec(D_MODEL), mod_spec, mod_spec,
                  pl.BlockSpec((1, D_MODEL), lambda i: (0, 0)),
                  pos_spec, pos_spec,
                  pl.BlockSpec((D_MODEL, D_IN), lambda i: (0, 0))],
        out_specs=out_specs,
        out_shape=out_shape,
        compiler_params=_cparams(("arbitrary",), 56),
        name="inproj",
    )(x2d, sc, sh, g, cos_t, sin_t, w_in_bf)


def _softmax_pv(s, v_parts):
    m = functools.reduce(jnp.maximum, [jnp.max(t, axis=-1, keepdims=True) for t in s])
    ps = [jnp.exp(t - m) for t in s]
    l = functools.reduce(jnp.add, [jnp.sum(p, axis=-1, keepdims=True) for p in ps])
    o = functools.reduce(jnp.add, [jnp.dot(p.astype(BF16), v, preferred_element_type=F32)
                                   for p, v in zip(ps, v_parts)])
    return o / l


def _attn_prompt_kernel(q_ref, k0_ref, k1_ref, k2_ref, v0_ref, v1_ref, v2_ref, bias_ref, o_ref):
    j = pl.program_id(1)
    q = q_ref[...]
    k = jnp.concatenate([k0_ref[...], k1_ref[...], k2_ref[...]], axis=0)
    v = jnp.concatenate([v0_ref[...], v1_ref[...], v2_ref[...]], axis=0)
    n_keys = k.shape[0]
    key_block = lax.broadcasted_iota(I32, (1, n_keys), 1) // ATT_QB
    before_start = jnp.where(key_block < 2 - j, NEG_BIG, 0.0)
    outs = []
    for h in range(N_HEADS):
        sl = slice(h * HEAD_DIM_A, (h + 1) * HEAD_DIM_A)
        qh = (q[:, sl].astype(F32) * (HEAD_DIM_A ** -0.5)).astype(BF16)
        s = lax.dot_general(qh, k[:, sl], (((1,), (1,)), ((), ())), preferred_element_type=F32)
        s = s + bias_ref[h] + before_start
        outs.append(_softmax_pv([s], [v[:, sl]]))
    o_ref[...] = jnp.concatenate(outs, axis=1).astype(BF16)


def _attn_prompt(q, k, v, bias_full, batch, seq):
    qb = ATT_QB
    nq = seq // qb

    def q_map(b, j):
        return (b * nq + j, 0)

    def kv_map(back):
        return lambda b, j: (b * nq + jnp.maximum(j - back, 0), 0)

    blk = lambda m: pl.BlockSpec((qb, D_ATT), m)
    return pl.pallas_call(
        _attn_prompt_kernel,
        grid=(batch, nq),
        in_specs=[blk(q_map), blk(kv_map(2)), blk(kv_map(1)), blk(kv_map(0)),
                  blk(kv_map(2)), blk(kv_map(1)), blk(kv_map(0)),
                  pl.BlockSpec(bias_full.shape, lambda b, j: (0, 0, 0))],
        out_specs=blk(q_map),
        out_shape=jax.ShapeDtypeStruct((batch * seq, D_ATT), BF16),
        compiler_params=_cparams(("parallel", "arbitrary"), 40),
        name="attn_prompt",
    )(q, k, k, k, v, v, v, bias_full)


SAMPLE_ATT_BATCHES = 2


def _attn_sample_kernel(q_ref, kn_ref, vn_ref, ck_ref, cv_ref, bc_ref, bn_ref, o_ref, *, t_new):
    nt = (((1,), (1,)), ((), ()))
    for b in range(SAMPLE_ATT_BATCHES):
        rows = slice(b * t_new, (b + 1) * t_new)
        q = q_ref[rows, :]
        kn = kn_ref[rows, :]
        vn = vn_ref[rows, :]
        outs = []
        for h in range(N_HEADS):
            sl = slice(h * HEAD_DIM_A, (h + 1) * HEAD_DIM_A)
            qh = (q[:, sl].astype(F32) * (HEAD_DIM_A ** -0.5)).astype(BF16)
            kc_t = ck_ref[b, h].astype(BF16)
            vc_t = cv_ref[b, h].astype(BF16)
            s_c = jnp.dot(qh, kc_t, preferred_element_type=F32) + bc_ref[h]
            s_n = lax.dot_general(qh, kn[:, sl], nt, preferred_element_type=F32) + bn_ref[h]
            m = jnp.maximum(jnp.max(s_c, axis=-1, keepdims=True), jnp.max(s_n, axis=-1, keepdims=True))
            p_c = jnp.exp(s_c - m)
            p_n = jnp.exp(s_n - m)
            l = jnp.sum(p_c, axis=-1, keepdims=True) + jnp.sum(p_n, axis=-1, keepdims=True)
            o = (lax.dot_general(p_c.astype(BF16), vc_t, nt, preferred_element_type=F32)
                 + jnp.dot(p_n.astype(BF16), vn[:, sl], preferred_element_type=F32))
            outs.append(o / l)
        o_ref[rows, :] = jnp.concatenate(outs, axis=1).astype(BF16)


def _attn_sample(q, k, v, cache_k_t, cache_v_t, bias_cache, bias_new, batch, t_new, cache_len):
    nb = SAMPLE_ATT_BATCHES
    blk = pl.BlockSpec((nb * t_new, D_ATT), lambda b: (b, 0))
    cblk = pl.BlockSpec((None, nb, N_HEADS, HEAD_DIM_A, cache_len), lambda b: (0, b, 0, 0, 0))
    return pl.pallas_call(
        functools.partial(_attn_sample_kernel, t_new=t_new),
        grid=(batch // nb,),
        in_specs=[blk, blk, blk, cblk, cblk,
                  pl.BlockSpec(bias_cache.shape, lambda b: (0, 0, 0)),
                  pl.BlockSpec(bias_new.shape, lambda b: (0, 0, 0))],
        out_specs=blk,
        out_shape=jax.ShapeDtypeStruct((batch * t_new, D_ATT), BF16),
        compiler_params=_cparams(("arbitrary",), 40),
        name="attn_sample",
    )(q, k, v, cache_k_t, cache_v_t, bias_cache, bias_new)


def _ret_kernel(q_ref, k_ref, v_ref, g_ref, s0_ref, dmask_ref, qdec_ref, kdec_ref, sdec_ref,
                y_ref, sout_ref, state_ref):
    c = pl.program_id(1)

    @pl.when(c == 0)
    def _():
        state_ref[...] = s0_ref[0]

    q = q_ref[...]
    k = k_ref[...]
    v = v_ref[...]
    g = g_ref[...]
    outs = []
    for h in range(N_HEADS):
        ks = slice(h * KEY_DIM_R, (h + 1) * KEY_DIM_R)
        vs = slice(h * VAL_DIM_R, (h + 1) * VAL_DIM_R)
        qh, kh, vh = q[:, ks], k[:, ks], v[:, vs]
        scores = lax.dot_general(qh, kh, (((1,), (1,)), ((), ())), preferred_element_type=F32)
        inner = jnp.dot((scores * dmask_ref[h]).astype(BF16), vh, preferred_element_type=F32)
        state = state_ref[h]
        cross = jnp.dot(qh, state.astype(BF16), preferred_element_type=F32) * qdec_ref[h]
        o = inner + cross
        v_dec = (vh.astype(F32) * kdec_ref[h]).astype(BF16)
        state_ref[h] = sdec_ref[h] * state + lax.dot_general(
            kh, v_dec, (((0,), (0,)), ((), ())), preferred_element_type=F32)
        on = o * lax.rsqrt(jnp.mean(o * o, axis=-1, keepdims=True) + EPS)
        outs.append(on * _silu(g[:, vs].astype(F32)))
    y_ref[...] = jnp.concatenate(outs, axis=1).astype(BF16)

    @pl.when(c == pl.num_programs(1) - 1)
    def _():
        sout_ref[0] = state_ref[...]


def _ret_tables(chunk):
    log_g = jnp.log(1.0 - jnp.exp2(-5.0 - jnp.arange(N_HEADS, dtype=F32)))
    i = jnp.arange(chunk, dtype=F32)
    diff = i[:, None] - i[None, :]
    dmask = jnp.where(diff >= 0, jnp.exp(log_g[:, None, None] * jnp.maximum(diff, 0.0)), 0.0)
    qdec = jnp.exp(log_g[:, None] * (i + 1.0))
    kdec = jnp.exp(log_g[:, None] * (chunk - 1.0 - i))
    sdec = jnp.exp(log_g * chunk)
    bc = lambda t: jnp.broadcast_to(t[:, :, None], (N_HEADS, t.shape[1], VAL_DIM_R)).astype(F32)
    sdec_t = jnp.broadcast_to(sdec[:, None, None], (N_HEADS, 1, VAL_DIM_R)).astype(F32)
    return dmask.astype(F32), bc(qdec), bc(kdec), sdec_t


def _retention(q, k, v, gate, state0, batch, seq, chunk):
    nc = seq // chunk
    dmask, qdec, kdec, sdec = _ret_tables(chunk)
    row = lambda w: pl.BlockSpec((chunk, w), lambda b, c: (b * nc + c, 0))
    const = lambda a: pl.BlockSpec(a.shape, lambda b, c: (0,) * a.ndim)
    st_spec = pl.BlockSpec((1, N_HEADS, KEY_DIM_R, VAL_DIM_R), lambda b, c: (b, 0, 0, 0))
    return pl.pallas_call(
        _ret_kernel,
        grid=(batch, nc),
        in_specs=[row(D_RET_K), row(D_RET_K), row(D_RET_V), row(D_RET_V), st_spec,
                  const(dmask), const(qdec), const(kdec), const(sdec)],
        out_specs=[row(D_RET_V), st_spec],
        out_shape=[jax.ShapeDtypeStruct((batch * seq, D_RET_V), BF16),
                   jax.ShapeDtypeStruct((batch, N_HEADS, KEY_DIM_R, VAL_DIM_R), F32)],
        scratch_shapes=[pltpu.VMEM((N_HEADS, KEY_DIM_R, VAL_DIM_R), F32)],
        compiler_params=_cparams(("parallel", "arbitrary"), 32),
        name="retention",
    )(q, k, v, gate, state0, dmask, qdec, kdec, sdec)


def _outproj_kernel(x_ref, oa_ref, yr_ref, ga_ref, gb_ref, g1_ref, sc2_ref, sh2_ref, g2_ref, n2g_ref,
                    woa_ref, wor_ref, wout_ref, wrt_ref, wsg_ref, wsu_ref, wsd_ref,
                    h_ref, n2p_ref, s_ref):
    ya = jnp.dot(oa_ref[...], woa_ref[...], preferred_element_type=F32)
    yr = jnp.dot(yr_ref[...], wor_ref[...], preferred_element_type=F32)
    merged = (jax.nn.sigmoid(ga_ref[...].astype(F32)) * ya
              + jax.nn.sigmoid(gb_ref[...].astype(F32)) * yr)
    mix = jnp.dot(merged.astype(BF16), wout_ref[...], preferred_element_type=F32)
    h = x_ref[...] + g1_ref[0] * mix
    hn = h * lax.rsqrt(jnp.mean(h * h, axis=-1, keepdims=True) + EPS) * n2g_ref[...]
    n2 = hn * (1.0 + sc2_ref[0]) + sh2_ref[0]
    n2b = n2.astype(BF16)
    s_ref[...] = jax.nn.sigmoid(lax.dot_general(wrt_ref[...], n2b, (((1,), (1,)), ((), ())),
                                                preferred_element_type=F32))
    hid = _silu(jnp.dot(n2b, wsg_ref[...], preferred_element_type=F32)) * jnp.dot(
        n2b, wsu_ref[...], preferred_element_type=F32)
    shared = jnp.dot(hid.astype(BF16), wsd_ref[...], preferred_element_type=F32)
    h_ref[...] = h + g2_ref[0] * shared
    half = D_MODEL // 2
    n2p_ref[...] = _pack_bf16_pair(n2[:, :half], n2[:, half:])


def _outproj(x2d, oa, yr_in, ga, gb, g1, sc2, sh2, g2, n2g, weights, tiles_per_batch):
    n = x2d.shape[0]
    tm = ROW_TILE
    mod_rows = g1.shape[1]
    row = lambda w: pl.BlockSpec((tm, w), lambda i: (i, 0))
    mod_spec = pl.BlockSpec((1, mod_rows, D_MODEL), lambda i: (i // tiles_per_batch, 0, 0))
    const = lambda a: pl.BlockSpec(a.shape, lambda i: (0,) * a.ndim)
    return pl.pallas_call(
        _outproj_kernel,
        grid=(n // tm,),
        in_specs=[row(D_MODEL), row(D_ATT), row(D_RET_V), row(D_MODEL), row(D_MODEL),
                  mod_spec, mod_spec, mod_spec, mod_spec, const(n2g)] + [const(w) for w in weights],
        out_specs=[row(D_MODEL), row(D_MODEL // 2), pl.BlockSpec((N_EXPERTS, tm), lambda i: (0, i))],
        out_shape=[jax.ShapeDtypeStruct((n, D_MODEL), F32),
                   jax.ShapeDtypeStruct((n, D_MODEL // 2), U32),
                   jax.ShapeDtypeStruct((N_EXPERTS, n), F32)],
        compiler_params=_cparams(("arbitrary",), 48),
        name="outproj",
    )(x2d, oa, yr_in, ga, gb, g1, sc2, sh2, g2, n2g, *weights)


def _route_kernel(s_ref, b_ref, idx_ref, w_ref, rank_ref, cnt_ref, run_ref, tri_ref):
    step = pl.program_id(0)
    t = s_ref.shape[1]

    @pl.when(step == 0)
    def _():
        run_ref[...] = jnp.zeros_like(run_ref)
        r = lax.broadcasted_iota(I32, (t, t), 0)
        c = lax.broadcasted_iota(I32, (t, t), 1)
        tri_ref[...] = jnp.where(r < c, 1.0, 0.0).astype(BF16)

    s = s_ref[...]
    sel = s + b_ref[...]
    row_f = lax.broadcasted_iota(I32, (N_EXPERTS, t), 0).astype(F32)

    def first_argmax(vals, rows):
        m = jnp.max(vals, axis=0, keepdims=True)
        pos = jnp.min(jnp.where(vals == m, rows, float(N_EXPERTS)), axis=0, keepdims=True)
        return m, pos

    gscore = []
    group_row = lax.broadcasted_iota(I32, (GROUP_SIZE, t), 0).astype(F32)
    for g in range(N_GROUPS):
        rows = slice(g * GROUP_SIZE, (g + 1) * GROUP_SIZE)
        m1, p1 = first_argmax(sel[rows], group_row)
        m2 = jnp.max(jnp.where(group_row == p1, -jnp.inf, sel[rows]), axis=0, keepdims=True)
        gscore.append(m1 + m2)
    cand_parts = []
    for g in range(N_GROUPS):
        rows = slice(g * GROUP_SIZE, (g + 1) * GROUP_SIZE)
        beaten_by = jnp.zeros((1, t), F32)
        for o in range(N_GROUPS):
            if o == g:
                continue
            wins = (gscore[o] > gscore[g]) if o > g else (gscore[o] >= gscore[g])
            beaten_by = beaten_by + jnp.where(wins, 1.0, 0.0)
        cand_parts.append(jnp.where(beaten_by < TOPK_GROUPS, sel[rows], -jnp.inf))
    cand = jnp.concatenate(cand_parts, axis=0)

    picked = jnp.zeros((N_EXPERTS, t), F32)
    idx_rows, w_rows = [], []
    for _ in range(TOP_K):
        _, pos = first_argmax(cand, row_f)
        hit = row_f == pos
        w_rows.append(jnp.sum(jnp.where(hit, s, 0.0), axis=0, keepdims=True))
        idx_rows.append(pos)
        picked = jnp.where(hit, 1.0, picked)
        cand = jnp.where(hit, -jnp.inf, cand)
    w_sum = functools.reduce(jnp.add, w_rows)

    before = jnp.dot(picked.astype(BF16), tri_ref[...], preferred_element_type=F32) + run_ref[...]
    run_ref[...] = run_ref[...] + jnp.sum(picked, axis=1, keepdims=True)
    rank_rows = [jnp.sum(jnp.where(row_f == idx_rows[kk], before, 0.0), axis=0, keepdims=True)
                 for kk in range(TOP_K)]

    idx_ref[...] = jnp.concatenate(idx_rows, axis=0).astype(I32)
    w_ref[...] = jnp.concatenate([w / w_sum * ROUTED_SCALE for w in w_rows], axis=0)
    rank_ref[...] = jnp.concatenate(rank_rows, axis=0).astype(I32)

    @pl.when(step == pl.num_programs(0) - 1)
    def _():
        cnt_ref[...] = run_ref[...].astype(I32)


def _route(scores_t, b_col):
    n = scores_t.shape[1]
    t = ROUTE_TILE
    col = pl.BlockSpec((TOP_K, t), lambda i: (0, i))
    const = pl.BlockSpec((N_EXPERTS, t), lambda i: (0, 0))
    return pl.pallas_call(
        _route_kernel,
        grid=(n // t,),
        in_specs=[pl.BlockSpec((N_EXPERTS, t), lambda i: (0, i)), const],
        out_specs=[col, col, col, const],
        out_shape=[jax.ShapeDtypeStruct((TOP_K, n), I32),
                   jax.ShapeDtypeStruct((TOP_K, n), F32),
                   jax.ShapeDtypeStruct((TOP_K, n), I32),
                   jax.ShapeDtypeStruct((N_EXPERTS, t), I32)],
        scratch_shapes=[pltpu.VMEM((N_EXPERTS, t), F32), pltpu.VMEM((t, t), BF16)],
        compiler_params=_cparams(("arbitrary",), 32),
        name="route",
    )(scores_t, b_col)


def _dest_kernel(idx_ref, rank_ref, start_ref, dest_ref):
    t = idx_ref.shape[1]
    row = lax.broadcasted_iota(I32, (N_EXPERTS, t), 0)
    starts = start_ref[...]
    base = [jnp.sum(jnp.where(row == idx_ref[kk:kk + 1, :], starts, 0.0), axis=0, keepdims=True)
            for kk in range(TOP_K)]
    dest_ref[...] = jnp.concatenate(base, axis=0).astype(I32) + rank_ref[...]


def _dest(idx_t, rank_t, starts_col):
    n = idx_t.shape[1]
    t = ROUTE_TILE
    col = pl.BlockSpec((TOP_K, t), lambda i: (0, i))
    return pl.pallas_call(
        _dest_kernel,
        grid=(n // t,),
        in_specs=[col, col, pl.BlockSpec((N_EXPERTS, t), lambda i: (0, 0))],
        out_specs=col,
        out_shape=jax.ShapeDtypeStruct((TOP_K, n), I32),
        compiler_params=_cparams(("arbitrary",), 32),
        name="dest",
    )(idx_t, rank_t, starts_col)


def _sc_mesh():
    return plsc.VectorSubcoreMesh(core_axis_name="core", subcore_axis_name="subcore",
                                  num_cores=V7X_SC_CORES, num_subcores=V7X_SC_SUBCORES)


def _sc_dispatch(n2p, dest_kmajor, pad_rows, n_out_rows):
    n, width = n2p.shape
    rows = SC_GATHER_ROWS
    n_workers = V7X_SC_CORES * V7X_SC_SUBCORES
    src_chunks = n // rows
    items = dest_kmajor.shape[0] // rows
    per_worker = items // n_workers
    pad_per_worker = pad_rows.shape[0] // rows // n_workers
    assert src_chunks * rows == n and per_worker * n_workers == items and per_worker % 2 == 0
    assert pad_per_worker * n_workers * rows == pad_rows.shape[0]
    idx3 = dest_kmajor.reshape(n_workers, per_worker, rows)
    pad3 = pad_rows.reshape(n_workers, pad_per_worker, rows)
    zeros = jnp.zeros((rows, width), n2p.dtype)

    def body(src_hbm, idx_hbm, pad_hbm, zero_hbm, out_hbm, idx_v, pad_v, rows_v, load_sem, scat_sem):
        worker = lax.axis_index("subcore") * V7X_SC_CORES + lax.axis_index("core")
        pltpu.sync_copy(idx_hbm.at[worker], idx_v)
        pltpu.sync_copy(pad_hbm.at[worker], pad_v)
        pltpu.sync_copy(zero_hbm, rows_v.at[0])

        def zero_fill(c):
            return pltpu.make_async_copy(rows_v.at[0], out_hbm.at[pad_v.at[c]], scat_sem.at[0])

        @pl.loop(0, pad_per_worker)
        def _(c):
            zero_fill(c).start()

        @pl.loop(0, pad_per_worker)
        def _(c):
            zero_fill(c).wait()

        def load(c, b):
            chunk = lax.rem(worker * per_worker + c, src_chunks)
            off = pl.multiple_of(chunk * rows, rows)
            return pltpu.make_async_copy(src_hbm.at[pl.ds(off, rows)], rows_v.at[b], load_sem.at[b])

        def scatter(c, b):
            return pltpu.make_async_copy(rows_v.at[b], out_hbm.at[idx_v.at[c]], scat_sem.at[b])

        load(0, 0).start()

        @pl.loop(0, per_worker, step=2)
        def _(c0):
            for b in range(2):
                c = c0 + b
                load(c, b).wait()

                @pl.when(c >= 1)
                def _():
                    scatter(c - 1, 1 - b).wait()

                @pl.when(c + 1 < per_worker)
                def _():
                    load(c + 1, 1 - b).start()

                scatter(c, b).start()

        scatter(per_worker - 1, (per_worker - 1) % 2).wait()

    return pl.kernel(
        body, mesh=_sc_mesh(),
        out_type=jax.ShapeDtypeStruct((n_out_rows, width), n2p.dtype),
        scratch_types=[pltpu.VMEM((per_worker, rows), I32),
                       pltpu.VMEM((pad_per_worker, rows), I32),
                       pltpu.VMEM((2, rows, width), n2p.dtype),
                       pltpu.SemaphoreType.DMA((2,)),
                       pltpu.SemaphoreType.DMA((2,))],
        name="sc_dispatch",
    )(n2p, idx3, pad3, zeros)


def _ffn_kernel(texp_ref, ntiles_ref, eslot_ref, enext_ref, nhalf_ref, xs_ref, wg_hbm, wu_hbm, wd_hbm,
                ys_ref, wg_buf, wu_buf, wd_buf, wgu_bf, wd_bf, sems):
    g = pl.program_id(0)

    def weight_copies(e, slot):
        return (pltpu.make_async_copy(wg_hbm.at[e], wg_buf.at[slot], sems.at[slot, 0]),
                pltpu.make_async_copy(wu_hbm.at[e], wu_buf.at[slot], sems.at[slot, 1]),
                pltpu.make_async_copy(wd_hbm.at[e], wd_buf.at[slot], sems.at[slot, 2]))

    @pl.when(g < ntiles_ref[0])
    def _():
        e = texp_ref[g]
        changed = jnp.logical_or(g == 0, texp_ref[jnp.maximum(g - 1, 0)] != e)

        @pl.when(changed)
        def _():
            slot = eslot_ref[e]

            def fetch_ahead(first_hop, hops, target_slot):
                ahead = first_hop
                for _ in range(hops - 1):
                    ahead = jnp.where(ahead >= 0, enext_ref[jnp.maximum(ahead, 0)], -1)

                @pl.when(ahead >= 0)
                def _():
                    for c in weight_copies(ahead, target_slot):
                        c.start()

            @pl.when(g == 0)
            def _():
                for c in weight_copies(e, slot):
                    c.start()
                for hops in range(1, FFN_WEIGHT_SLOTS - 1):
                    fetch_ahead(enext_ref[e], hops, lax.rem(slot + hops, FFN_WEIGHT_SLOTS))

            for c in weight_copies(e, slot):
                c.wait()
            fetch_ahead(enext_ref[e], FFN_WEIGHT_SLOTS - 1,
                        lax.rem(slot + FFN_WEIGHT_SLOTS - 1, FFN_WEIGHT_SLOTS))

            wgu_bf[:, :D_EXPERT] = wg_buf[slot].astype(BF16)
            wgu_bf[:, D_EXPERT:] = wu_buf[slot].astype(BF16)
            wd_bf[...] = wd_buf[slot].astype(BF16)

        def expert_rows(rows):
            lo, hi = _unpack_bf16_pair(xs_ref[rows, :])
            x = jnp.concatenate([lo, hi], axis=1).astype(BF16)
            gu = jnp.dot(x, wgu_bf[...], preferred_element_type=F32)
            hid = (_silu(gu[:, :D_EXPERT]) * gu[:, D_EXPERT:]).astype(BF16)
            y = jnp.dot(hid, wd_bf[...], preferred_element_type=F32)
            half = D_MODEL // 2
            ys_ref[rows, :] = _pack_bf16_pair(y[:, :half], y[:, half:])

        for n_groups in range(1, FFN_TILE // FFN_QUANTUM + 1):

            @pl.when(nhalf_ref[g] == n_groups)
            def _(n_groups=n_groups):
                used = n_groups * FFN_QUANTUM
                for start in range(0, used, FFN_MATMUL_ROWS):
                    expert_rows(slice(start, min(start + FFN_MATMUL_ROWS, used)))
                if used < FFN_TILE:
                    ys_ref[used:, :] = jnp.zeros((FFN_TILE - used, ys_ref.shape[1]), U32)


def _ffn(tile_expert, n_tiles, expert_slot, expert_next, tile_halves, xs, w_gate, w_up, w_down):
    rows, width = xs.shape
    m = FFN_TILE
    max_tiles = tile_expert.shape[0]
    row_map = lambda g, te, nt, es, en, nh: (jnp.minimum(g, nt[0] - 1), 0)
    hbm = pl.BlockSpec(memory_space=pl.ANY)
    grid_spec = pltpu.PrefetchScalarGridSpec(
        num_scalar_prefetch=5,
        grid=(max_tiles,),
        in_specs=[pl.BlockSpec((m, width), row_map), hbm, hbm, hbm],
        out_specs=pl.BlockSpec((m, width), row_map),
        scratch_shapes=[pltpu.VMEM((FFN_WEIGHT_SLOTS, D_MODEL, D_EXPERT), F32),
                        pltpu.VMEM((FFN_WEIGHT_SLOTS, D_MODEL, D_EXPERT), F32),
                        pltpu.VMEM((FFN_WEIGHT_SLOTS, D_EXPERT, D_MODEL), F32),
                        pltpu.VMEM((D_MODEL, 2 * D_EXPERT), BF16),
                        pltpu.VMEM((D_EXPERT, D_MODEL), BF16),
                        pltpu.SemaphoreType.DMA((FFN_WEIGHT_SLOTS, 3))],
    )
    return pl.pallas_call(
        _ffn_kernel,
        grid_spec=grid_spec,
        out_shape=jax.ShapeDtypeStruct((rows, width), U32),
        compiler_params=_cparams(("arbitrary",), 32),
        name="ffn",
    )(tile_expert, n_tiles, expert_slot, expert_next, tile_halves, xs, w_gate, w_up, w_down)


def _ffn_plan(counts, n_assign):
    m = FFN_TILE
    max_tiles = n_assign // m + N_EXPERTS
    padded = ((counts + m - 1) // m) * m
    pend = jnp.cumsum(padded).astype(I32)
    pstart = pend - padded
    n_tiles = pend[-1:] // m
    g = jnp.minimum(jnp.arange(max_tiles, dtype=I32), n_tiles - 1)
    tile_expert = jnp.sum((pend[None, :] <= (g * m)[:, None]).astype(I32), axis=1)
    tile_expert = jnp.minimum(tile_expert, N_EXPERTS - 1)
    vend = pstart + ((counts + FFN_QUANTUM - 1) // FFN_QUANTUM) * FFN_QUANTUM
    own = tile_expert[:, None] == jnp.arange(N_EXPERTS, dtype=I32)[None, :]
    tile_vend = jnp.sum(jnp.where(own, vend[None, :], 0), axis=1)
    tile_halves = (jnp.clip(tile_vend - g * m, 0, m) // FFN_QUANTUM).astype(I32)
    used = counts > 0
    expert_slot = ((jnp.cumsum(used.astype(I32)) - 1) % FFN_WEIGHT_SLOTS).astype(I32)
    ids = jnp.where(used, jnp.arange(N_EXPERTS, dtype=I32), N_EXPERTS)
    first_used_from = lax.cummin(ids, axis=0, reverse=True)
    nxt = jnp.concatenate([first_used_from[1:], jnp.full((1,), N_EXPERTS, I32)])
    expert_next = jnp.where(nxt < N_EXPERTS, nxt, -1).astype(I32)
    spare_row = max_tiles * m
    j = jnp.arange(FFN_QUANTUM, dtype=I32)[None, :]
    seg_end = (pstart + counts)[:, None]
    spare = spare_row + (jnp.arange(N_EXPERTS, dtype=I32)[:, None] * FFN_QUANTUM + j) % SPARE_ROWS
    pad_rows = jnp.where(j < (vend[:, None] - seg_end), seg_end + j, spare).astype(I32).reshape(-1)
    return (tile_expert, n_tiles, expert_slot, expert_next, tile_halves, pstart, pad_rows,
            spare_row + SPARE_ROWS)


def _sc_gather_rows(table, idx):
    n_idx = idx.shape[0]
    width = table.shape[1]
    n_workers = V7X_SC_CORES * V7X_SC_SUBCORES
    per_worker = n_idx // n_workers
    n_chunks = per_worker // SC_GATHER_ROWS
    assert per_worker * n_workers == n_idx and n_chunks * SC_GATHER_ROWS == per_worker and n_chunks % 2 == 0

    def body(table_hbm, idx_hbm, out_hbm, idx_v, rows_v, gather_sem, write_sem):
        worker = lax.axis_index("subcore") * V7X_SC_CORES + lax.axis_index("core")
        base = worker * per_worker
        pltpu.sync_copy(idx_hbm.at[pl.ds(base, per_worker)], idx_v)

        def gather(c, b):
            off = pl.multiple_of(c * SC_GATHER_ROWS, SC_GATHER_ROWS)
            return pltpu.make_async_copy(table_hbm.at[idx_v.at[pl.ds(off, SC_GATHER_ROWS)]],
                                         rows_v.at[b], gather_sem.at[b])

        def write(c, b):
            off = pl.multiple_of(c * SC_GATHER_ROWS, SC_GATHER_ROWS)
            return pltpu.make_async_copy(rows_v.at[b], out_hbm.at[pl.ds(base + off, SC_GATHER_ROWS)],
                                         write_sem.at[b])

        gather(0, 0).start()

        @pl.loop(0, n_chunks, step=2)
        def _(c0):
            for b in range(2):
                c = c0 + b
                gather(c, b).wait()

                @pl.when(c >= 1)
                def _():
                    write(c - 1, 1 - b).wait()

                @pl.when(c + 1 < n_chunks)
                def _():
                    gather(c + 1, 1 - b).start()

                write(c, b).start()

        write(n_chunks - 1, (n_chunks - 1) % 2).wait()

    return pl.kernel(
        body, mesh=_sc_mesh(),
        out_type=jax.ShapeDtypeStruct((n_idx, width), table.dtype),
        scratch_types=[pltpu.VMEM((per_worker,), I32),
                       pltpu.VMEM((2, SC_GATHER_ROWS, width), table.dtype),
                       pltpu.SemaphoreType.DMA((2,)),
                       pltpu.SemaphoreType.DMA((2,))],
        name="sc_gather_rows",
    )(table, idx)


def _combine_kernel(w_ref, h_ref, g2_ref, nf_ref, yk_ref, y_ref):
    t = h_ref.shape[0]
    w = w_ref[...]
    acc_lo = jnp.zeros((t, D_MODEL // 2), F32)
    acc_hi = jnp.zeros((t, D_MODEL // 2), F32)
    for kk in range(TOP_K):
        lo, hi = _unpack_bf16_pair(yk_ref[kk])
        wk = w[:, kk:kk + 1]
        acc_lo = acc_lo + wk * lo
        acc_hi = acc_hi + wk * hi
    out = h_ref[...] + g2_ref[0] * jnp.concatenate([acc_lo, acc_hi], axis=1)
    y_ref[...] = out * lax.rsqrt(jnp.mean(out * out, axis=-1, keepdims=True) + EPS) * nf_ref[...]


def _combine(w, h2, g2, normf, y_by_k, row_offset, tiles_per_batch):
    n = h2.shape[0]
    t = MOVE_TILE
    off = row_offset // t
    mod_rows = g2.shape[1]
    mod_tiles = max(ROW_TILE // t, 1) * tiles_per_batch if mod_rows == 1 else n // t
    return pl.pallas_call(
        _combine_kernel,
        grid=(n // t,),
        in_specs=[pl.BlockSpec((t, LANES), lambda i: (i + off, 0)),
                  pl.BlockSpec((t, D_MODEL), lambda i: (i, 0)),
                  pl.BlockSpec((1, mod_rows if mod_rows == 1 else t, D_MODEL),
                               (lambda i: (i // mod_tiles, 0, 0)) if mod_rows == 1
                               else (lambda i: (0, i, 0))),
                  pl.BlockSpec((1, D_MODEL), lambda i: (0, 0)),
                  pl.BlockSpec((TOP_K, t, D_MODEL // 2), lambda i: (0, i + off, 0))],
        out_specs=pl.BlockSpec((t, D_MODEL), lambda i: (i, 0)),
        out_shape=jax.ShapeDtypeStruct((n, D_MODEL), F32),
        compiler_params=_cparams(("arbitrary",), 40),
        name="combine",
    )(w, h2, g2, normf, y_by_k)


def _rotary_tables(pos):
    half = KEY_DIM_R // 2
    inv_freq = ROPE_BASE ** (-jnp.arange(half, dtype=F32) / half)
    ang = pos[:, None] * inv_freq[None, :]
    cos = jnp.cos(ang)
    sin = jnp.sin(ang)
    cos_t = jnp.tile(jnp.concatenate([cos, cos], axis=1), (1, N_HEADS))
    sin_t = jnp.tile(jnp.concatenate([-sin, sin], axis=1), (1, N_HEADS))
    return cos_t.astype(F32), sin_t.astype(F32)


def _rel_bias_table(rel_bias, n_rows, n_cols, q_offset):
    heads = rel_bias.shape[0]
    n_diag = n_rows + n_cols - 1
    dist = q_offset + (n_rows - 1) - np.arange(n_diag)
    idx = np.clip(dist, -REL_CLIP, REL_CLIP) + REL_CLIP
    n_hi = int(np.sum(dist > REL_CLIP))
    n_lo = int(np.sum(dist < -REL_CLIP))
    mid = rel_bias[:, int(idx[n_diag - n_lo - 1]):int(idx[n_hi]) + 1][:, ::-1]
    diag = jnp.concatenate([jnp.broadcast_to(rel_bias[:, 2 * REL_CLIP:], (heads, n_hi)), mid,
                            jnp.broadcast_to(rel_bias[:, :1], (heads, n_lo))], axis=1)
    period = n_diag + 1
    v = jnp.roll(jnp.pad(diag, ((0, 0), (0, 1))), -(n_rows - 1), axis=1)
    skew = jnp.tile(v, (1, n_rows))[:, :n_rows * (period - 1)].reshape(heads, n_rows, period - 1)
    return skew[:, :, :n_cols].astype(F32)


def _prompt_bias(rel_bias):
    n_cols = ATT_QB + ATT_WINDOW
    r = np.arange(ATT_QB)[:, None]
    c = np.arange(n_cols)[None, :]
    band = c - (r // CHUNK) * CHUNK
    valid = (band >= 0) & (band < ATT_WINDOW + CHUNK)
    table = _rel_bias_table(rel_bias, ATT_QB, n_cols, ATT_WINDOW)
    return jnp.where(jnp.asarray(valid)[None], table, NEG_BIG)


def _sample_bias(rel_bias, t_new, cache_len):
    b = _rel_bias_table(rel_bias, t_new, cache_len + t_new, cache_len)
    return b[:, :, :cache_len], b[:, :, cache_len:]


def _mod_parts(mod, rows_each):
    parts = jnp.split(mod, 6, axis=-1)
    if rows_each == 1:
        return [p[:, None, :] for p in parts]
    return [jnp.repeat(p, rows_each, axis=0)[None] for p in parts]


def kernel(x_prompt, x_sample, cache_attn_k, cache_attn_v, state_ret, c_prompt, c_sample,
           norm1_g, norm2_g, w_ada, b_ada, w_in, rel_bias, w_o_attn, w_o_ret, w_out,
           w_router, b_router, w_exp_gate, w_exp_up, w_exp_down, w_sh_gate, w_sh_up, w_sh_down,
           normf_g):
    batch, seq, d = x_prompt.shape
    dec_batch, dec_seq, _ = x_sample.shape
    depth = w_in.shape[0]
    assert depth == 1 and d == D_MODEL
    assert seq % ROW_TILE == 0 and dec_batch * dec_seq == ROW_TILE and ROW_TILE == ATT_WINDOW
    cache_len = cache_attn_k.shape[2]
    n_p = batch * seq
    n_s = dec_batch * dec_seq
    tpb = seq // ROW_TILE
    l = 0

    bf = lambda a: a.astype(BF16)
    c_all = jnp.concatenate([c_prompt, c_sample], axis=0)
    pad = (-c_all.shape[0]) % 8
    c_all = jnp.pad(c_all, ((0, pad), (0, 0)))
    mod = _ada(c_all, bf(w_ada[l]), b_ada[l][None, :])
    mod_p = _mod_parts(mod[:batch], 1)
    mod_s = _mod_parts(mod[batch:batch + dec_batch], dec_seq)

    w_in_bf = bf(w_in[l])
    n1g = norm1_g[l][None, :]
    n2g = norm2_g[l][None, :]
    dense_w = [bf(w_o_attn[l]), bf(w_o_ret[l]), bf(w_out[l]), bf(w_router[l]).T,
               bf(w_sh_gate[l]), bf(w_sh_up[l]), bf(w_sh_down[l])]

    xp = x_prompt.reshape(n_p, d)
    xs_ = x_sample.reshape(n_s, d)
    cos_p, sin_p = _rotary_tables(jnp.arange(seq, dtype=F32))
    pos_s = PAST_LEN + jnp.arange(dec_seq, dtype=F32)
    cos_s, sin_s = _rotary_tables(jnp.tile(pos_s, dec_batch))

    (qa, ka, va, qr, kr, vr, gr, ga, gb, kv_p) = _inproj(
        xp, mod_p[1], mod_p[0], n1g, cos_p, sin_p, w_in_bf, tpb)
    oa = _attn_prompt(qa, ka, va, _prompt_bias(rel_bias[l]), batch, seq)
    zero_state = jnp.zeros((batch, N_HEADS, KEY_DIM_R, VAL_DIM_R), F32)
    yr_in, state_p = _retention(qr, kr, vr, gr, zero_state, batch, seq, RET_CHUNK)
    h_p, n2p_p, s_p = _outproj(xp, oa, yr_in, ga, gb, mod_p[2], mod_p[4], mod_p[3], mod_p[5], n2g,
                               dense_w, tpb)

    (qa_s, ka_s, va_s, qr_s, kr_s, vr_s, gr_s, ga_s, gb_s, kv_s) = _inproj(
        xs_, mod_s[1], mod_s[0], n1g, cos_s, sin_s, w_in_bf, 1)
    bias_c, bias_n = _sample_bias(rel_bias[l], dec_seq, cache_len)
    to_keys_minor = lambda c: jnp.transpose(c, (0, 1, 3, 4, 2))
    oa_s = _attn_sample(qa_s, ka_s, va_s, to_keys_minor(cache_attn_k), to_keys_minor(cache_attn_v),
                        bias_c, bias_n, dec_batch, dec_seq, cache_len)
    yr_in_s, state_s = _retention(qr_s, kr_s, vr_s, gr_s, state_ret[l], dec_batch, dec_seq, dec_seq)
    h_s, n2p_s, s_s = _outproj(xs_, oa_s, yr_in_s, ga_s, gb_s, mod_s[2], mod_s[4], mod_s[3], mod_s[5],
                               n2g, dense_w, 1)

    n2p = jnp.concatenate([n2p_p, n2p_s], axis=0)
    scores_t = jnp.concatenate([s_p, s_s], axis=1)
    lanes_of = lambda v: jnp.broadcast_to(v[:, None], (N_EXPERTS, ROUTE_TILE))
    idx_t, w_t, rank_t, counts = _route(scores_t, lanes_of(b_router[l]))
    (tile_expert, n_tiles, expert_slot, expert_next, tile_halves, pstart, pad_rows,
     n_sorted_rows) = _ffn_plan(counts[:, 0], (n_p + n_s) * TOP_K)
    dest_t = _dest(idx_t, rank_t, lanes_of(pstart.astype(F32)))
    dest_kmajor = dest_t.reshape(-1)
    w_route = jnp.pad(w_t.T, ((0, 0), (0, LANES - TOP_K)))
    xs_sorted = _sc_dispatch(n2p, dest_kmajor, pad_rows, n_sorted_rows)
    ys_sorted = _ffn(tile_expert, n_tiles, expert_slot, expert_next, tile_halves, xs_sorted,
                     w_exp_gate[l], w_exp_up[l], w_exp_down[l])
    nf = normf_g[None, :]
    y_by_k = _sc_gather_rows(ys_sorted, dest_kmajor).reshape(TOP_K, n_p + n_s, d // 2)
    y_p = _combine(w_route, h_p, mod_p[5], nf, y_by_k, 0, tpb)
    y_s = _combine(w_route, h_s, mod_s[5], nf, y_by_k, n_p, 1)

    keep = min(ATT_WINDOW, seq)
    kv_p = kv_p.reshape(batch, ROW_TILE, 2, N_HEADS, HEAD_DIM_A)[:, ROW_TILE - keep:]
    kv_s = kv_s.reshape(dec_batch, dec_seq, 2, N_HEADS, HEAD_DIM_A)
    return (y_p.reshape(batch, seq, d), y_s.reshape(dec_batch, dec_seq, d),
            kv_p[:, :, 0][None], kv_p[:, :, 1][None], state_p[None],
            kv_s[:, :, 0][None], kv_s[:, :, 1][None], state_s[None])
```

```python
import functools

import numpy as np
import jax
import jax.numpy as jnp
from jax import lax
from jax.experimental import pallas as pl
from jax.experimental.pallas import tpu as pltpu
from jax.experimental.pallas import tpu_sc as plsc

F32 = jnp.float32
BF16 = jnp.bfloat16
I32 = jnp.int32
U32 = jnp.uint32

D_MODEL = 1024
PAST_LEN = 4096
CHUNK = 64
N_LEFT_CHUNKS = 8
ATT_WINDOW = N_LEFT_CHUNKS * CHUNK
N_HEADS = 8
HEAD_DIM_A = 64
D_ATT = N_HEADS * HEAD_DIM_A
REL_CLIP = 128
KEY_DIM_R = 64
VAL_DIM_R = 128
D_RET_K = N_HEADS * KEY_DIM_R
D_RET_V = N_HEADS * VAL_DIM_R
ROPE_BASE = 10000.0
N_EXPERTS = 256
TOP_K = 8
N_GROUPS = 8
GROUP_SIZE = N_EXPERTS // N_GROUPS
TOPK_GROUPS = 4
D_EXPERT = 256
ROUTED_SCALE = 2.5
EPS = 1e-6
IN_WIDTHS = (D_ATT, D_ATT, D_ATT, D_RET_K, D_RET_K, D_RET_V, D_RET_V, D_MODEL, D_MODEL)
IN_OFFS = tuple(int(v) for v in np.cumsum((0,) + IN_WIDTHS))
D_IN = IN_OFFS[-1]

NEG_BIG = -1e30
LANES = 128
V7X_VMEM_BYTES = 64 * 1024 * 1024
V7X_SC_CORES = 2
V7X_SC_SUBCORES = 16
SC_GATHER_ROWS = 64
SPARE_ROWS = 8192

ROW_TILE = 512
ATT_QB = 256
RET_CHUNK = 256
ROUTE_TILE = 512
MOVE_TILE = 256
FFN_QUANTUM = 128
FFN_MATMUL_ROWS = 2 * FFN_QUANTUM
FFN_TILE = 5 * FFN_QUANTUM
FFN_WEIGHT_SLOTS = 4


def _cparams(semantics, vmem_mb):
    return pltpu.CompilerParams(dimension_semantics=semantics,
                                vmem_limit_bytes=min(vmem_mb * 1024 * 1024, V7X_VMEM_BYTES - (6 << 20)))


def _silu(x):
    return x * jax.nn.sigmoid(x)


def _pack_bf16_pair(lo, hi):
    lo_b = pltpu.bitcast(lo.astype(BF16).astype(F32), U32) >> 16
    hi_b = pltpu.bitcast(hi.astype(BF16).astype(F32), U32) & jnp.uint32(0xFFFF0000)
    return lo_b | hi_b


def _unpack_bf16_pair(u):
    lo = pltpu.bitcast(u << 16, F32)
    hi = pltpu.bitcast(u & jnp.uint32(0xFFFF0000), F32)
    return lo, hi


def _ada_kernel(c_ref, w_ref, b_ref, o_ref):
    sc = _silu(c_ref[...]).astype(BF16)
    o_ref[...] = jnp.dot(sc, w_ref[...], preferred_element_type=F32) + b_ref[...]


def _ada(c_all, w_ada_bf, b_ada):
    rows = c_all.shape[0]
    n_out = w_ada_bf.shape[1]
    blk = D_MODEL
    return pl.pallas_call(
        _ada_kernel,
        grid=(n_out // blk,),
        in_specs=[pl.BlockSpec((rows, D_MODEL), lambda j: (0, 0)),
                  pl.BlockSpec((D_MODEL, blk), lambda j: (0, j)),
                  pl.BlockSpec((1, blk), lambda j: (0, j))],
        out_specs=pl.BlockSpec((rows, blk), lambda j: (0, j)),
        out_shape=jax.ShapeDtypeStruct((rows, n_out), F32),
        compiler_params=_cparams(("arbitrary",), 24),
        name="ada",
    )(c_all, w_ada_bf, b_ada)


def _inproj_kernel(x_ref, sc_ref, sh_ref, g_ref, cos_ref, sin_ref, w_ref,
                   qa_ref, ka_ref, va_ref, qr_ref, kr_ref, vr_ref, gr_ref, ga_ref, gb_ref,
                   kv_ref, *, tiles_per_batch):
    x = x_ref[...]
    xn = x * lax.rsqrt(jnp.mean(x * x, axis=-1, keepdims=True) + EPS) * g_ref[...]
    nb = (xn * (1.0 + sc_ref[0]) + sh_ref[0]).astype(BF16)

    def proj(seg):
        return jnp.dot(nb, w_ref[:, IN_OFFS[seg]:IN_OFFS[seg + 1]], preferred_element_type=F32)

    qa_ref[...] = proj(0).astype(BF16)
    ka = proj(1)
    va = proj(2)
    ka_ref[...] = ka.astype(BF16)
    va_ref[...] = va.astype(BF16)

    @pl.when(pl.program_id(0) % tiles_per_batch == tiles_per_batch - 1)
    def _():
        kv_ref[:, :D_ATT] = ka
        kv_ref[:, D_ATT:] = va

    cos = cos_ref[...]
    sin = sin_ref[...]
    first_half = (lax.broadcasted_iota(I32, (1, D_RET_K), 1) % KEY_DIM_R) < (KEY_DIM_R // 2)

    def rotary(t):
        partner = jnp.where(first_half, pltpu.roll(t, D_RET_K - KEY_DIM_R // 2, 1),
                            pltpu.roll(t, KEY_DIM_R // 2, 1))
        return t * cos + partner * sin

    qr_ref[...] = rotary(proj(3)).astype(BF16)
    kr_ref[...] = (rotary(proj(4)) * (KEY_DIM_R ** -0.5)).astype(BF16)
    vr_ref[...] = proj(5).astype(BF16)
    gr_ref[...] = proj(6).astype(BF16)
    ga_ref[...] = proj(7).astype(BF16)
    gb_ref[...] = proj(8).astype(BF16)


def _inproj(x2d, sc, sh, g, cos_t, sin_t, w_in_bf, tiles_per_batch):
    n = x2d.shape[0]
    tm = ROW_TILE
    n_tiles = n // tm
    n_batches = n_tiles // tiles_per_batch
    mod_rows = sc.shape[1]
    pos_tiles = cos_t.shape[0] // tm

    def row_spec(width):
        return pl.BlockSpec((tm, width), lambda i: (i, 0))

    mod_spec = pl.BlockSpec((1, mod_rows, D_MODEL), lambda i: (i // tiles_per_batch, 0, 0))
    pos_spec = pl.BlockSpec((tm, D_RET_K), lambda i: (i % pos_tiles, 0))
    out_widths = (D_ATT, D_ATT, D_ATT, D_RET_K, D_RET_K, D_RET_V, D_RET_V, D_MODEL, D_MODEL)
    out_shape = [jax.ShapeDtypeStruct((n, w), BF16) for w in out_widths]
    out_shape.append(jax.ShapeDtypeStruct((n_batches * tm, 2 * D_ATT), F32))
    out_specs = [row_spec(w) for w in out_widths]
    out_specs.append(pl.BlockSpec((tm, 2 * D_ATT), lambda i: (i // tiles_per_batch, 0)))
    return pl.pallas_call(
        functools.partial(_inproj_kernel, tiles_per_batch=tiles_per_batch),
        grid=(n_tiles,),
        in_specs=[row_spec(D_MODEL), mod_spec, mod_spec,
                  pl.BlockSpec((1, D_MODEL), lambda i: (0, 0)),
                  pos_spec, pos_spec,
                  pl.BlockSpec((D_MODEL, D_IN), lambda i: (0, 0))],
        out_specs=out_specs,
        out_shape=out_shape,
        compiler_params=_cparams(("arbitrary",), 56),
        name="inproj",
    )(x2d, sc, sh, g, cos_t, sin_t, w_in_bf)


def _softmax_pv(s, v_parts):
    m = functools.reduce(jnp.maximum, [jnp.max(t, axis=-1, keepdims=True) for t in s])
    ps = [jnp.exp(t - m) for t in s]
    l = functools.reduce(jnp.add, [jnp.sum(p, axis=-1, keepdims=True) for p in ps])
    o = functools.reduce(jnp.add, [jnp.dot(p.astype(BF16), v, preferred_element_type=F32)
                                   for p, v in zip(ps, v_parts)])
    return o / l


def _attn_prompt_kernel(q_ref, k0_ref, k1_ref, k2_ref, v0_ref, v1_ref, v2_ref, bias_ref, o_ref):
    j = pl.program_id(1)
    q = q_ref[...]
    k = jnp.concatenate([k0_ref[...], k1_ref[...], k2_ref[...]], axis=0)
    v = jnp.concatenate([v0_ref[...], v1_ref[...], v2_ref[...]], axis=0)
    n_keys = k.shape[0]
    key_block = lax.broadcasted_iota(I32, (1, n_keys), 1) // ATT_QB
    before_start = jnp.where(key_block < 2 - j, NEG_BIG, 0.0)
    outs = []
    for h in range(N_HEADS):
        sl = slice(h * HEAD_DIM_A, (h + 1) * HEAD_DIM_A)
        qh = (q[:, sl].astype(F32) * (HEAD_DIM_A ** -0.5)).astype(BF16)
        s = lax.dot_general(qh, k[:, sl], (((1,), (1,)), ((), ())), preferred_element_type=F32)
        s = s + bias_ref[h] + before_start
        outs.append(_softmax_pv([s], [v[:, sl]]))
    o_ref[...] = jnp.concatenate(outs, axis=1).astype(BF16)


def _attn_prompt(q, k, v, bias_full, batch, seq):
    qb = ATT_QB
    nq = seq // qb

    def q_map(b, j):
        return (b * nq + j, 0)

    def kv_map(back):
        return lambda b, j: (b * nq + jnp.maximum(j - back, 0), 0)

    blk = lambda m: pl.BlockSpec((qb, D_ATT), m)
    return pl.pallas_call(
        _attn_prompt_kernel,
        grid=(batch, nq),
        in_specs=[blk(q_map), blk(kv_map(2)), blk(kv_map(1)), blk(kv_map(0)),
                  blk(kv_map(2)), blk(kv_map(1)), blk(kv_map(0)),
                  pl.BlockSpec(bias_full.shape, lambda b, j: (0, 0, 0))],
        out_specs=blk(q_map),
        out_shape=jax.ShapeDtypeStruct((batch * seq, D_ATT), BF16),
        compiler_params=_cparams(("parallel", "arbitrary"), 40),
        name="attn_prompt",
    )(q, k, k, k, v, v, v, bias_full)


SAMPLE_ATT_BATCHES = 2


def _attn_sample_kernel(q_ref, kn_ref, vn_ref, ck_ref, cv_ref, bc_ref, bn_ref, o_ref, *, t_new):
    nt = (((1,), (1,)), ((), ()))
    for b in range(SAMPLE_ATT_BATCHES):
        rows = slice(b * t_new, (b + 1) * t_new)
        q = q_ref[rows, :]
        kn = kn_ref[rows, :]
        vn = vn_ref[rows, :]
        outs = []
        for h in range(N_HEADS):
            sl = slice(h * HEAD_DIM_A, (h + 1) * HEAD_DIM_A)
            qh = (q[:, sl].astype(F32) * (HEAD_DIM_A ** -0.5)).astype(BF16)
            kc_t = ck_ref[b, h].astype(BF16)
            vc_t = cv_ref[b, h].astype(BF16)
            s_c = jnp.dot(qh, kc_t, preferred_element_type=F32) + bc_ref[h]
            s_n = lax.dot_general(qh, kn[:, sl], nt, preferred_element_type=F32) + bn_ref[h]
            m = jnp.maximum(jnp.max(s_c, axis=-1, keepdims=True), jnp.max(s_n, axis=-1, keepdims=True))
            p_c = jnp.exp(s_c - m)
            p_n = jnp.exp(s_n - m)
            l = jnp.sum(p_c, axis=-1, keepdims=True) + jnp.sum(p_n, axis=-1, keepdims=True)
            o = (lax.dot_general(p_c.astype(BF16), vc_t, nt, preferred_element_type=F32)
                 + jnp.dot(p_n.astype(BF16), vn[:, sl], preferred_element_type=F32))
            outs.append(o / l)
        o_ref[rows, :] = jnp.concatenate(outs, axis=1).astype(BF16)


def _attn_sample(q, k, v, cache_k_t, cache_v_t, bias_cache, bias_new, batch, t_new, cache_len):
    nb = SAMPLE_ATT_BATCHES
    blk = pl.BlockSpec((nb * t_new, D_ATT), lambda b: (b, 0))
    cblk = pl.BlockSpec((None, nb, N_HEADS, HEAD_DIM_A, cache_len), lambda b: (0, b, 0, 0, 0))
    return pl.pallas_call(
        functools.partial(_attn_sample_kernel, t_new=t_new),
        grid=(batch // nb,),
        in_specs=[blk, blk, blk, cblk, cblk,
                  pl.BlockSpec(bias_cache.shape, lambda b: (0, 0, 0)),
                  pl.BlockSpec(bias_new.shape, lambda b: (0, 0, 0))],
        out_specs=blk,
        out_shape=jax.ShapeDtypeStruct((batch * t_new, D_ATT), BF16),
        compiler_params=_cparams(("arbitrary",), 40),
        name="attn_sample",
    )(q, k, v, cache_k_t, cache_v_t, bias_cache, bias_new)


def _ret_kernel(q_ref, k_ref, v_ref, g_ref, s0_ref, dmask_ref, qdec_ref, kdec_ref, sdec_ref,
                y_ref, sout_ref, state_ref):
    c = pl.program_id(1)

    @pl.when(c == 0)
    def _():
        state_ref[...] = s0_ref[0]

    q = q_ref[...]
    k = k_ref[...]
    v = v_ref[...]
    g = g_ref[...]
    outs = []
    for h in range(N_HEADS):
        ks = slice(h * KEY_DIM_R, (h + 1) * KEY_DIM_R)
        vs = slice(h * VAL_DIM_R, (h + 1) * VAL_DIM_R)
        qh, kh, vh = q[:, ks], k[:, ks], v[:, vs]
        scores = lax.dot_general(qh, kh, (((1,), (1,)), ((), ())), preferred_element_type=F32)
        inner = jnp.dot((scores * dmask_ref[h]).astype(BF16), vh, preferred_element_type=F32)
        state = state_ref[h]
        cross = jnp.dot(qh, state.astype(BF16), preferred_element_type=F32) * qdec_ref[h]
        o = inner + cross
        v_dec = (vh.astype(F32) * kdec_ref[h]).astype(BF16)
        state_ref[h] = sdec_ref[h] * state + lax.dot_general(
            kh, v_dec, (((0,), (0,)), ((), ())), preferred_element_type=F32)
        on = o * lax.rsqrt(jnp.mean(o * o, axis=-1, keepdims=True) + EPS)
        outs.append(on * _silu(g[:, vs].astype(F32)))
    y_ref[...] = jnp.concatenate(outs, axis=1).astype(BF16)

    @pl.when(c == pl.num_programs(1) - 1)
    def _():
        sout_ref[0] = state_ref[...]


def _ret_tables(chunk):
    log_g = jnp.log(1.0 - jnp.exp2(-5.0 - jnp.arange(N_HEADS, dtype=F32)))
    i = jnp.arange(chunk, dtype=F32)
    diff = i[:, None] - i[None, :]
    dmask = jnp.where(diff >= 0, jnp.exp(log_g[:, None, None] * jnp.maximum(diff, 0.0)), 0.0)
    qdec = jnp.exp(log_g[:, None] * (i + 1.0))
    kdec = jnp.exp(log_g[:, None] * (chunk - 1.0 - i))
    sdec = jnp.exp(log_g * chunk)
    bc = lambda t: jnp.broadcast_to(t[:, :, None], (N_HEADS, t.shape[1], VAL_DIM_R)).astype(F32)
    sdec_t = jnp.broadcast_to(sdec[:, None, None], (N_HEADS, 1, VAL_DIM_R)).astype(F32)
    return dmask.astype(F32), bc(qdec), bc(kdec), sdec_t


def _retention(q, k, v, gate, state0, batch, seq, chunk):
    nc = seq // chunk
    dmask, qdec, kdec, sdec = _ret_tables(chunk)
    row = lambda w: pl.BlockSpec((chunk, w), lambda b, c: (b * nc + c, 0))
    const = lambda a: pl.BlockSpec(a.shape, lambda b, c: (0,) * a.ndim)
    st_spec = pl.BlockSpec((1, N_HEADS, KEY_DIM_R, VAL_DIM_R), lambda b, c: (b, 0, 0, 0))
    return pl.pallas_call(
        _ret_kernel,
        grid=(batch, nc),
        in_specs=[row(D_RET_K), row(D_RET_K), row(D_RET_V), row(D_RET_V), st_spec,
                  const(dmask), const(qdec), const(kdec), const(sdec)],
        out_specs=[row(D_RET_V), st_spec],
        out_shape=[jax.ShapeDtypeStruct((batch * seq, D_RET_V), BF16),
                   jax.ShapeDtypeStruct((batch, N_HEADS, KEY_DIM_R, VAL_DIM_R), F32)],
        scratch_shapes=[pltpu.VMEM((N_HEADS, KEY_DIM_R, VAL_DIM_R), F32)],
        compiler_params=_cparams(("parallel", "arbitrary"), 32),
        name="retention",
    )(q, k, v, gate, state0, dmask, qdec, kdec, sdec)


def _outproj_kernel(x_ref, oa_ref, yr_ref, ga_ref, gb_ref, g1_ref, sc2_ref, sh2_ref, g2_ref, n2g_ref,
                    woa_ref, wor_ref, wout_ref, wrt_ref, wsg_ref, wsu_ref, wsd_ref,
                    h_ref, n2p_ref, s_ref):
    ya = jnp.dot(oa_ref[...], woa_ref[...], preferred_element_type=F32)
    yr = jnp.dot(yr_ref[...], wor_ref[...], preferred_element_type=F32)
    merged = (jax.nn.sigmoid(ga_ref[...].astype(F32)) * ya
              + jax.nn.sigmoid(gb_ref[...].astype(F32)) * yr)
    mix = jnp.dot(merged.astype(BF16), wout_ref[...], preferred_element_type=F32)
    h = x_ref[...] + g1_ref[0] * mix
    hn = h * lax.rsqrt(jnp.mean(h * h, axis=-1, keepdims=True) + EPS) * n2g_ref[...]
    n2 = hn * (1.0 + sc2_ref[0]) + sh2_ref[0]
    n2b = n2.astype(BF16)
    s_ref[...] = jax.nn.sigmoid(lax.dot_general(wrt_ref[...], n2b, (((1,), (1,)), ((), ())),
                                                preferred_element_type=F32))
    hid = _silu(jnp.dot(n2b, wsg_ref[...], preferred_element_type=F32)) * jnp.dot(
        n2b, wsu_ref[...], preferred_element_type=F32)
    shared = jnp.dot(hid.astype(BF16), wsd_ref[...], preferred_element_type=F32)
    h_ref[...] = h + g2_ref[0] * shared
    half = D_MODEL // 2
    n2p_ref[...] = _pack_bf16_pair(n2[:, :half], n2[:, half:])


def _outproj(x2d, oa, yr_in, ga, gb, g1, sc2, sh2, g2, n2g, weights, tiles_per_batch):
    n = x2d.shape[0]
    tm = ROW_TILE
    mod_rows = g1.shape[1]
    row = lambda w: pl.BlockSpec((tm, w), lambda i: (i, 0))
    mod_spec = pl.BlockSpec((1, mod_rows, D_MODEL), lambda i: (i // tiles_per_batch, 0, 0))
    const = lambda a: pl.BlockSpec(a.shape, lambda i: (0,) * a.ndim)
    return pl.pallas_call(
        _outproj_kernel,
        grid=(n // tm,),
        in_specs=[row(D_MODEL), row(D_ATT), row(D_RET_V), row(D_MODEL), row(D_MODEL),
                  mod_spec, mod_spec, mod_spec, mod_spec, const(n2g)] + [const(w) for w in weights],
        out_specs=[row(D_MODEL), row(D_MODEL // 2), pl.BlockSpec((N_EXPERTS, tm), lambda i: (0, i))],
        out_shape=[jax.ShapeDtypeStruct((n, D_MODEL), F32),
                   jax.ShapeDtypeStruct((n, D_MODEL // 2), U32),
                   jax.ShapeDtypeStruct((N_EXPERTS, n), F32)],
        compiler_params=_cparams(("arbitrary",), 48),
        name="outproj",
    )(x2d, oa, yr_in, ga, gb, g1, sc2, sh2, g2, n2g, *weights)


def _route_kernel(s_ref, b_ref, idx_ref, w_ref, rank_ref, cnt_ref, run_ref, tri_ref):
    step = pl.program_id(0)
    t = s_ref.shape[1]

    @pl.when(step == 0)
    def _():
        run_ref[...] = jnp.zeros_like(run_ref)
        r = lax.broadcasted_iota(I32, (t, t), 0)
        c = lax.broadcasted_iota(I32, (t, t), 1)
        tri_ref[...] = jnp.where(r < c, 1.0, 0.0).astype(BF16)

    s = s_ref[...]
    sel = s + b_ref[...]
    row_f = lax.broadcasted_iota(I32, (N_EXPERTS, t), 0).astype(F32)

    def first_argmax(vals, rows):
        m = jnp.max(vals, axis=0, keepdims=True)
        pos = jnp.min(jnp.where(vals == m, rows, float(N_EXPERTS)), axis=0, keepdims=True)
        return m, pos

    gscore = []
    group_row = lax.broadcasted_iota(I32, (GROUP_SIZE, t), 0).astype(F32)
    for g in range(N_GROUPS):
        rows = slice(g * GROUP_SIZE, (g + 1) * GROUP_SIZE)
        m1, p1 = first_argmax(sel[rows], group_row)
        m2 = jnp.max(jnp.where(group_row == p1, -jnp.inf, sel[rows]), axis=0, keepdims=True)
        gscore.append(m1 + m2)
    cand_parts = []
    for g in range(N_GROUPS):
        rows = slice(g * GROUP_SIZE, (g + 1) * GROUP_SIZE)
        beaten_by = jnp.zeros((1, t), F32)
        for o in range(N_GROUPS):
            if o == g:
                continue
            wins = (gscore[o] > gscore[g]) if o > g else (gscore[o] >= gscore[g])
            beaten_by = beaten_by + jnp.where(wins, 1.0, 0.0)
        cand_parts.append(jnp.where(beaten_by < TOPK_GROUPS, sel[rows], -jnp.inf))
    cand = jnp.concatenate(cand_parts, axis=0)

    picked = jnp.zeros((N_EXPERTS, t), F32)
    idx_rows, w_rows = [], []
    for _ in range(TOP_K):
        _, pos = first_argmax(cand, row_f)
        hit = row_f == pos
        w_rows.append(jnp.sum(jnp.where(hit, s, 0.0), axis=0, keepdims=True))
        idx_rows.append(pos)
        picked = jnp.where(hit, 1.0, picked)
        cand = jnp.where(hit, -jnp.inf, cand)
    w_sum = functools.reduce(jnp.add, w_rows)

    before = jnp.dot(picked.astype(BF16), tri_ref[...], preferred_element_type=F32) + run_ref[...]
    run_ref[...] = run_ref[...] + jnp.sum(picked, axis=1, keepdims=True)
    rank_rows = [jnp.sum(jnp.where(row_f == idx_rows[kk], before, 0.0), axis=0, keepdims=True)
                 for kk in range(TOP_K)]

    idx_ref[...] = jnp.concatenate(idx_rows, axis=0).astype(I32)
    w_ref[...] = jnp.concatenate([w / w_sum * ROUTED_SCALE for w in w_rows], axis=0)
    rank_ref[...] = jnp.concatenate(rank_rows, axis=0).astype(I32)

    @pl.when(step == pl.num_programs(0) - 1)
    def _():
        cnt_ref[...] = run_ref[...].astype(I32)


def _route(scores_t, b_col):
    n = scores_t.shape[1]
    t = ROUTE_TILE
    col = pl.BlockSpec((TOP_K, t), lambda i: (0, i))
    const = pl.BlockSpec((N_EXPERTS, t), lambda i: (0, 0))
    return pl.pallas_call(
        _route_kernel,
        grid=(n // t,),
        in_specs=[pl.BlockSpec((N_EXPERTS, t), lambda i: (0, i)), const],
        out_specs=[col, col, col, const],
        out_shape=[jax.ShapeDtypeStruct((TOP_K, n), I32),
                   jax.ShapeDtypeStruct((TOP_K, n), F32),
                   jax.ShapeDtypeStruct((TOP_K, n), I32),
                   jax.ShapeDtypeStruct((N_EXPERTS, t), I32)],
        scratch_shapes=[pltpu.VMEM((N_EXPERTS, t), F32), pltpu.VMEM((t, t), BF16)],
        compiler_params=_cparams(("arbitrary",), 32),
        name="route",
    )(scores_t, b_col)


def _dest_kernel(idx_ref, rank_ref, start_ref, dest_ref):
    t = idx_ref.shape[1]
    row = lax.broadcasted_iota(I32, (N_EXPERTS, t), 0)
    starts = start_ref[...]
    base = [jnp.sum(jnp.where(row == idx_ref[kk:kk + 1, :], starts, 0.0), axis=0, keepdims=True)
            for kk in range(TOP_K)]
    dest_ref[...] = jnp.concatenate(base, axis=0).astype(I32) + rank_ref[...]


def _dest(idx_t, rank_t, starts_col):
    n = idx_t.shape[1]
    t = ROUTE_TILE
    col = pl.BlockSpec((TOP_K, t), lambda i: (0, i))
    return pl.pallas_call(
        _dest_kernel,
        grid=(n // t,),
        in_specs=[col, col, pl.BlockSpec((N_EXPERTS, t), lambda i: (0, 0))],
        out_specs=col,
        out_shape=jax.ShapeDtypeStruct((TOP_K, n), I32),
        compiler_params=_cparams(("arbitrary",), 32),
        name="dest",
    )(idx_t, rank_t, starts_col)


def _sc_mesh():
    return plsc.VectorSubcoreMesh(core_axis_name="core", subcore_axis_name="subcore",
                                  num_cores=V7X_SC_CORES, num_subcores=V7X_SC_SUBCORES)


def _sc_dispatch(n2p, dest_kmajor, pad_rows, n_out_rows):
    n, width = n2p.shape
    rows = SC_GATHER_ROWS
    n_workers = V7X_SC_CORES * V7X_SC_SUBCORES
    src_chunks = n // rows
    items = dest_kmajor.shape[0] // rows
    per_worker = items // n_workers
    pad_per_worker = pad_rows.shape[0] // rows // n_workers
    assert src_chunks * rows == n and per_worker * n_workers == items and per_worker % 2 == 0
    assert pad_per_worker * n_workers * rows == pad_rows.shape[0]
    idx3 = dest_kmajor.reshape(n_workers, per_worker, rows)
    pad3 = pad_rows.reshape(n_workers, pad_per_worker, rows)
    zeros = jnp.zeros((rows, width), n2p.dtype)

    def body(src_hbm, idx_hbm, pad_hbm, zero_hbm, out_hbm, idx_v, pad_v, rows_v, load_sem, scat_sem):
        worker = lax.axis_index("subcore") * V7X_SC_CORES + lax.axis_index("core")
        pltpu.sync_copy(idx_hbm.at[worker], idx_v)
        pltpu.sync_copy(pad_hbm.at[worker], pad_v)
        pltpu.sync_copy(zero_hbm, rows_v.at[0])

        def zero_fill(c):
            return pltpu.make_async_copy(rows_v.at[0], out_hbm.at[pad_v.at[c]], scat_sem.at[0])

        @pl.loop(0, pad_per_worker)
        def _(c):
            zero_fill(c).start()

        @pl.loop(0, pad_per_worker)
        def _(c):
            zero_fill(c).wait()

        def load(c, b):
            chunk = lax.rem(worker * per_worker + c, src_chunks)
            off = pl.multiple_of(chunk * rows, rows)
            return pltpu.make_async_copy(src_hbm.at[pl.ds(off, rows)], rows_v.at[b], load_sem.at[b])

        def scatter(c, b):
            return pltpu.make_async_copy(rows_v.at[b], out_hbm.at[idx_v.at[c]], scat_sem.at[b])

        load(0, 0).start()

        @pl.loop(0, per_worker, step=2)
        def _(c0):
            for b in range(2):
                c = c0 + b
                load(c, b).wait()

                @pl.when(c >= 1)
                def _():
                    scatter(c - 1, 1 - b).wait()

                @pl.when(c + 1 < per_worker)
                def _():
                    load(c + 1, 1 - b).start()

                scatter(c, b).start()

        scatter(per_worker - 1, (per_worker - 1) % 2).wait()

    return pl.kernel(
        body, mesh=_sc_mesh(),
        out_type=jax.ShapeDtypeStruct((n_out_rows, width), n2p.dtype),
        scratch_types=[pltpu.VMEM((per_worker, rows), I32),
                       pltpu.VMEM((pad_per_worker, rows), I32),
                       pltpu.VMEM((2, rows, width), n2p.dtype),
                       pltpu.SemaphoreType.DMA((2,)),
                       pltpu.SemaphoreType.DMA((2,))],
        name="sc_dispatch",
    )(n2p, idx3, pad3, zeros)


def _ffn_kernel(texp_ref, ntiles_ref, eslot_ref, enext_ref, nhalf_ref, xs_ref, wg_hbm, wu_hbm, wd_hbm,
                ys_ref, wg_buf, wu_buf, wd_buf, wgu_bf, wd_bf, sems):
    g = pl.program_id(0)

    def weight_copies(e, slot):
        return (pltpu.make_async_copy(wg_hbm.at[e], wg_buf.at[slot], sems.at[slot, 0]),
                pltpu.make_async_copy(wu_hbm.at[e], wu_buf.at[slot], sems.at[slot, 1]),
                pltpu.make_async_copy(wd_hbm.at[e], wd_buf.at[slot], sems.at[slot, 2]))

    @pl.when(g < ntiles_ref[0])
    def _():
        e = texp_ref[g]
        changed = jnp.logical_or(g == 0, texp_ref[jnp.maximum(g - 1, 0)] != e)

        @pl.when(changed)
        def _():
            slot = eslot_ref[e]

            def fetch_ahead(first_hop, hops, target_slot):
                ahead = first_hop
                for _ in range(hops - 1):
                    ahead = jnp.where(ahead >= 0, enext_ref[jnp.maximum(ahead, 0)], -1)

                @pl.when(ahead >= 0)
                def _():
                    for c in weight_copies(ahead, target_slot):
                        c.start()

            @pl.when(g == 0)
            def _():
                for c in weight_copies(e, slot):
                    c.start()
                for hops in range(1, FFN_WEIGHT_SLOTS - 1):
                    fetch_ahead(enext_ref[e], hops, lax.rem(slot + hops, FFN_WEIGHT_SLOTS))

            fetch_ahead(enext_ref[e], FFN_WEIGHT_SLOTS - 1,
                        lax.rem(slot + FFN_WEIGHT_SLOTS - 1, FFN_WEIGHT_SLOTS))
            for c in weight_copies(e, slot):
                c.wait()

            wgu_bf[:, :D_EXPERT] = wg_buf[slot].astype(BF16)
            wgu_bf[:, D_EXPERT:] = wu_buf[slot].astype(BF16)
            wd_bf[...] = wd_buf[slot].astype(BF16)

        def expert_rows(rows):
            lo, hi = _unpack_bf16_pair(xs_ref[rows, :])
            x = jnp.concatenate([lo, hi], axis=1).astype(BF16)
            gu = jnp.dot(x, wgu_bf[...], preferred_element_type=F32)
            hid = (_silu(gu[:, :D_EXPERT]) * gu[:, D_EXPERT:]).astype(BF16)
            y = jnp.dot(hid, wd_bf[...], preferred_element_type=F32)
            half = D_MODEL // 2
            ys_ref[rows, :] = _pack_bf16_pair(y[:, :half], y[:, half:])

        for n_groups in range(1, FFN_TILE // FFN_QUANTUM + 1):

            @pl.when(nhalf_ref[g] == n_groups)
            def _(n_groups=n_groups):
                used = n_groups * FFN_QUANTUM
                for start in range(0, used, FFN_MATMUL_ROWS):
                    expert_rows(slice(start, min(start + FFN_MATMUL_ROWS, used)))
                if used < FFN_TILE:
                    ys_ref[used:, :] = jnp.zeros((FFN_TILE - used, ys_ref.shape[1]), U32)


def _ffn(tile_expert, n_tiles, expert_slot, expert_next, tile_halves, xs, w_gate, w_up, w_down):
    rows, width = xs.shape
    m = FFN_TILE
    max_tiles = tile_expert.shape[0]
    row_map = lambda g, te, nt, es, en, nh: (jnp.minimum(g, nt[0] - 1), 0)
    hbm = pl.BlockSpec(memory_space=pl.ANY)
    grid_spec = pltpu.PrefetchScalarGridSpec(
        num_scalar_prefetch=5,
        grid=(max_tiles,),
        in_specs=[pl.BlockSpec((m, width), row_map), hbm, hbm, hbm],
        out_specs=pl.BlockSpec((m, width), row_map),
        scratch_shapes=[pltpu.VMEM((FFN_WEIGHT_SLOTS, D_MODEL, D_EXPERT), F32),
                        pltpu.VMEM((FFN_WEIGHT_SLOTS, D_MODEL, D_EXPERT), F32),
                        pltpu.VMEM((FFN_WEIGHT_SLOTS, D_EXPERT, D_MODEL), F32),
                        pltpu.VMEM((D_MODEL, 2 * D_EXPERT), BF16),
                        pltpu.VMEM((D_EXPERT, D_MODEL), BF16),
                        pltpu.SemaphoreType.DMA((FFN_WEIGHT_SLOTS, 3))],
    )
    return pl.pallas_call(
        _ffn_kernel,
        grid_spec=grid_spec,
        out_shape=jax.ShapeDtypeStruct((rows, width), U32),
        compiler_params=_cparams(("arbitrary",), 32),
        name="ffn",
    )(tile_expert, n_tiles, expert_slot, expert_next, tile_halves, xs, w_gate, w_up, w_down)


def _ffn_plan(counts, n_assign):
    m = FFN_TILE
    max_tiles = n_assign // m + N_EXPERTS
    padded = ((counts + m - 1) // m) * m
    pend = jnp.cumsum(padded).astype(I32)
    pstart = pend - padded
    n_tiles = pend[-1:] // m
    g = jnp.minimum(jnp.arange(max_tiles, dtype=I32), n_tiles - 1)
    tile_expert = jnp.sum((pend[None, :] <= (g * m)[:, None]).astype(I32), axis=1)
    tile_expert = jnp.minimum(tile_expert, N_EXPERTS - 1)
    vend = pstart + ((counts + FFN_QUANTUM - 1) // FFN_QUANTUM) * FFN_QUANTUM
    own = tile_expert[:, None] == jnp.arange(N_EXPERTS, dtype=I32)[None, :]
    tile_vend = jnp.sum(jnp.where(own, vend[None, :], 0), axis=1)
    tile_halves = (jnp.clip(tile_vend - g * m, 0, m) // FFN_QUANTUM).astype(I32)
    used = counts > 0
    expert_slot = ((jnp.cumsum(used.astype(I32)) - 1) % FFN_WEIGHT_SLOTS).astype(I32)
    ids = jnp.where(used, jnp.arange(N_EXPERTS, dtype=I32), N_EXPERTS)
    first_used_from = lax.cummin(ids, axis=0, reverse=True)
    nxt = jnp.concatenate([first_used_from[1:], jnp.full((1,), N_EXPERTS, I32)])
    expert_next = jnp.where(nxt < N_EXPERTS, nxt, -1).astype(I32)
    spare_row = max_tiles * m
    j = jnp.arange(FFN_QUANTUM, dtype=I32)[None, :]
    seg_end = (pstart + counts)[:, None]
    spare = spare_row + (jnp.arange(N_EXPERTS, dtype=I32)[:, None] * FFN_QUANTUM + j) % SPARE_ROWS
    pad_rows = jnp.where(j < (vend[:, None] - seg_end), seg_end + j, spare).astype(I32).reshape(-1)
    return (tile_expert, n_tiles, expert_slot, expert_next, tile_halves, pstart, pad_rows,
            spare_row + SPARE_ROWS)


def _sc_gather_rows(table, idx):
    n_idx = idx.shape[0]
    width = table.shape[1]
    n_workers = V7X_SC_CORES * V7X_SC_SUBCORES
    per_worker = n_idx // n_workers
    n_chunks = per_worker // SC_GATHER_ROWS
    assert per_worker * n_workers == n_idx and n_chunks * SC_GATHER_ROWS == per_worker and n_chunks % 2 == 0

    def body(table_hbm, idx_hbm, out_hbm, idx_v, rows_v, gather_sem, write_sem):
        worker = lax.axis_index("subcore") * V7X_SC_CORES + lax.axis_index("core")
        base = worker * per_worker
        pltpu.sync_copy(idx_hbm.at[pl.ds(base, per_worker)], idx_v)

        def gather(c, b):
            off = pl.multiple_of(c * SC_GATHER_ROWS, SC_GATHER_ROWS)
            return pltpu.make_async_copy(table_hbm.at[idx_v.at[pl.ds(off, SC_GATHER_ROWS)]],
                                         rows_v.at[b], gather_sem.at[b])

        def write(c, b):
            off = pl.multiple_of(c * SC_GATHER_ROWS, SC_GATHER_ROWS)
            return pltpu.make_async_copy(rows_v.at[b], out_hbm.at[pl.ds(base + off, SC_GATHER_ROWS)],
                                         write_sem.at[b])

        gather(0, 0).start()

        @pl.loop(0, n_chunks, step=2)
        def _(c0):
            for b in range(2):
                c = c0 + b
                gather(c, b).wait()

                @pl.when(c >= 1)
                def _():
                    write(c - 1, 1 - b).wait()

                @pl.when(c + 1 < n_chunks)
                def _():
                    gather(c + 1, 1 - b).start()

                write(c, b).start()

        write(n_chunks - 1, (n_chunks - 1) % 2).wait()

    return pl.kernel(
        body, mesh=_sc_mesh(),
        out_type=jax.ShapeDtypeStruct((n_idx, width), table.dtype),
        scratch_types=[pltpu.VMEM((per_worker,), I32),
                       pltpu.VMEM((2, SC_GATHER_ROWS, width), table.dtype),
                       pltpu.SemaphoreType.DMA((2,)),
                       pltpu.SemaphoreType.DMA((2,))],
        name="sc_gather_rows",
    )(table, idx)


def _combine_kernel(w_ref, h_ref, g2_ref, nf_ref, yk_ref, y_ref):
    t = h_ref.shape[0]
    w = w_ref[...]
    acc_lo = jnp.zeros((t, D_MODEL // 2), F32)
    acc_hi = jnp.zeros((t, D_MODEL // 2), F32)
    for kk in range(TOP_K):
        lo, hi = _unpack_bf16_pair(yk_ref[kk])
        wk = w[:, kk:kk + 1]
        acc_lo = acc_lo + wk * lo
        acc_hi = acc_hi + wk * hi
    out = h_ref[...] + g2_ref[0] * jnp.concatenate([acc_lo, acc_hi], axis=1)
    y_ref[...] = out * lax.rsqrt(jnp.mean(out * out, axis=-1, keepdims=True) + EPS) * nf_ref[...]


def _combine(w, h2, g2, normf, y_by_k, row_offset, tiles_per_batch):
    n = h2.shape[0]
    t = MOVE_TILE
    off = row_offset // t
    mod_rows = g2.shape[1]
    mod_tiles = max(ROW_TILE // t, 1) * tiles_per_batch if mod_rows == 1 else n // t
    return pl.pallas_call(
        _combine_kernel,
        grid=(n // t,),
        in_specs=[pl.BlockSpec((t, LANES), lambda i: (i + off, 0)),
                  pl.BlockSpec((t, D_MODEL), lambda i: (i, 0)),
                  pl.BlockSpec((1, mod_rows if mod_rows == 1 else t, D_MODEL),
                               (lambda i: (i // mod_tiles, 0, 0)) if mod_rows == 1
                               else (lambda i: (0, i, 0))),
                  pl.BlockSpec((1, D_MODEL), lambda i: (0, 0)),
                  pl.BlockSpec((TOP_K, t, D_MODEL // 2), lambda i: (0, i + off, 0))],
        out_specs=pl.BlockSpec((t, D_MODEL), lambda i: (i, 0)),
        out_shape=jax.ShapeDtypeStruct((n, D_MODEL), F32),
        compiler_params=_cparams(("arbitrary",), 40),
        name="combine",
    )(w, h2, g2, normf, y_by_k)


def _rotary_tables(pos):
    half = KEY_DIM_R // 2
    inv_freq = ROPE_BASE ** (-jnp.arange(half, dtype=F32) / half)
    ang = pos[:, None] * inv_freq[None, :]
    cos = jnp.cos(ang)
    sin = jnp.sin(ang)
    cos_t = jnp.tile(jnp.concatenate([cos, cos], axis=1), (1, N_HEADS))
    sin_t = jnp.tile(jnp.concatenate([-sin, sin], axis=1), (1, N_HEADS))
    return cos_t.astype(F32), sin_t.astype(F32)


def _rel_bias_table(rel_bias, n_rows, n_cols, q_offset):
    heads = rel_bias.shape[0]
    n_diag = n_rows + n_cols - 1
    dist = q_offset + (n_rows - 1) - np.arange(n_diag)
    idx = np.clip(dist, -REL_CLIP, REL_CLIP) + REL_CLIP
    n_hi = int(np.sum(dist > REL_CLIP))
    n_lo = int(np.sum(dist < -REL_CLIP))
    mid = rel_bias[:, int(idx[n_diag - n_lo - 1]):int(idx[n_hi]) + 1][:, ::-1]
    diag = jnp.concatenate([jnp.broadcast_to(rel_bias[:, 2 * REL_CLIP:], (heads, n_hi)), mid,
                            jnp.broadcast_to(rel_bias[:, :1], (heads, n_lo))], axis=1)
    period = n_diag + 1
    v = jnp.roll(jnp.pad(diag, ((0, 0), (0, 1))), -(n_rows - 1), axis=1)
    skew = jnp.tile(v, (1, n_rows))[:, :n_rows * (period - 1)].reshape(heads, n_rows, period - 1)
    return skew[:, :, :n_cols].astype(F32)


def _prompt_bias(rel_bias):
    n_cols = ATT_QB + ATT_WINDOW
    r = np.arange(ATT_QB)[:, None]
    c = np.arange(n_cols)[None, :]
    band = c - (r // CHUNK) * CHUNK
    valid = (band >= 0) & (band < ATT_WINDOW + CHUNK)
    table = _rel_bias_table(rel_bias, ATT_QB, n_cols, ATT_WINDOW)
    return jnp.where(jnp.asarray(valid)[None], table, NEG_BIG)


def _sample_bias(rel_bias, t_new, cache_len):
    b = _rel_bias_table(rel_bias, t_new, cache_len + t_new, cache_len)
    return b[:, :, :cache_len], b[:, :, cache_len:]


def _mod_parts(mod, rows_each):
    parts = jnp.split(mod, 6, axis=-1)
    if rows_each == 1:
        return [p[:, None, :] for p in parts]
    return [jnp.repeat(p, rows_each, axis=0)[None] for p in parts]


def kernel(x_prompt, x_sample, cache_attn_k, cache_attn_v, state_ret, c_prompt, c_sample,
           norm1_g, norm2_g, w_ada, b_ada, w_in, rel_bias, w_o_attn, w_o_ret, w_out,
           w_router, b_router, w_exp_gate, w_exp_up, w_exp_down, w_sh_gate, w_sh_up, w_sh_down,
           normf_g):
    batch, seq, d = x_prompt.shape
    dec_batch, dec_seq, _ = x_sample.shape
    depth = w_in.shape[0]
    assert depth == 1 and d == D_MODEL
    assert seq % ROW_TILE == 0 and dec_batch * dec_seq == ROW_TILE and ROW_TILE == ATT_WINDOW
    cache_len = cache_attn_k.shape[2]
    n_p = batch * seq
    n_s = dec_batch * dec_seq
    tpb = seq // ROW_TILE
    l = 0

    bf = lambda a: a.astype(BF16)
    c_all = jnp.concatenate([c_prompt, c_sample], axis=0)
    pad = (-c_all.shape[0]) % 8
    c_all = jnp.pad(c_all, ((0, pad), (0, 0)))
    mod = _ada(c_all, bf(w_ada[l]), b_ada[l][None, :])
    mod_p = _mod_parts(mod[:batch], 1)
    mod_s = _mod_parts(mod[batch:batch + dec_batch], dec_seq)

    w_in_bf = bf(w_in[l])
    n1g = norm1_g[l][None, :]
    n2g = norm2_g[l][None, :]
    dense_w = [bf(w_o_attn[l]), bf(w_o_ret[l]), bf(w_out[l]), bf(w_router[l]).T,
               bf(w_sh_gate[l]), bf(w_sh_up[l]), bf(w_sh_down[l])]

    xp = x_prompt.reshape(n_p, d)
    xs_ = x_sample.reshape(n_s, d)
    cos_p, sin_p = _rotary_tables(jnp.arange(seq, dtype=F32))
    pos_s = PAST_LEN + jnp.arange(dec_seq, dtype=F32)
    cos_s, sin_s = _rotary_tables(jnp.tile(pos_s, dec_batch))

    (qa, ka, va, qr, kr, vr, gr, ga, gb, kv_p) = _inproj(
        xp, mod_p[1], mod_p[0], n1g, cos_p, sin_p, w_in_bf, tpb)
    oa = _attn_prompt(qa, ka, va, _prompt_bias(rel_bias[l]), batch, seq)
    zero_state = jnp.zeros((batch, N_HEADS, KEY_DIM_R, VAL_DIM_R), F32)
    yr_in, state_p = _retention(qr, kr, vr, gr, zero_state, batch, seq, RET_CHUNK)
    h_p, n2p_p, s_p = _outproj(xp, oa, yr_in, ga, gb, mod_p[2], mod_p[4], mod_p[3], mod_p[5], n2g,
                               dense_w, tpb)

    (qa_s, ka_s, va_s, qr_s, kr_s, vr_s, gr_s, ga_s, gb_s, kv_s) = _inproj(
        xs_, mod_s[1], mod_s[0], n1g, cos_s, sin_s, w_in_bf, 1)
    bias_c, bias_n = _sample_bias(rel_bias[l], dec_seq, cache_len)
    to_keys_minor = lambda c: jnp.transpose(c, (0, 1, 3, 4, 2))
    oa_s = _attn_sample(qa_s, ka_s, va_s, to_keys_minor(cache_attn_k), to_keys_minor(cache_attn_v),
                        bias_c, bias_n, dec_batch, dec_seq, cache_len)
    yr_in_s, state_s = _retention(qr_s, kr_s, vr_s, gr_s, state_ret[l], dec_batch, dec_seq, dec_seq)
    h_s, n2p_s, s_s = _outproj(xs_, oa_s, yr_in_s, ga_s, gb_s, mod_s[2], mod_s[4], mod_s[3], mod_s[5],
                               n2g, dense_w, 1)

    n2p = jnp.concatenate([n2p_p, n2p_s], axis=0)
    scores_t = jnp.concatenate([s_p, s_s], axis=1)
    lanes_of = lambda v: jnp.broadcast_to(v[:, None], (N_EXPERTS, ROUTE_TILE))
    idx_t, w_t, rank_t, counts = _route(scores_t, lanes_of(b_router[l]))
    (tile_expert, n_tiles, expert_slot, expert_next, tile_halves, pstart, pad_rows,
     n_sorted_rows) = _ffn_plan(counts[:, 0], (n_p + n_s) * TOP_K)
    dest_t = _dest(idx_t, rank_t, lanes_of(pstart.astype(F32)))
    dest_kmajor = dest_t.reshape(-1)
    w_route = jnp.pad(w_t.T, ((0, 0), (0, LANES - TOP_K)))
    xs_sorted = _sc_dispatch(n2p, dest_kmajor, pad_rows, n_sorted_rows)
    ys_sorted = _ffn(tile_expert, n_tiles, expert_slot, expert_next, tile_halves, xs_sorted,
                     w_exp_gate[l], w_exp_up[l], w_exp_down[l])
    nf = normf_g[None, :]
    y_by_k = _sc_gather_rows(ys_sorted, dest_kmajor).reshape(TOP_K, n_p + n_s, d // 2)
    y_p = _combine(w_route, h_p, mod_p[5], nf, y_by_k, 0, tpb)
    y_s = _combine(w_route, h_s, mod_s[5], nf, y_by_k, n_p, 1)

    keep = min(ATT_WINDOW, seq)
    kv_p = kv_p.reshape(batch, ROW_TILE, 2, N_HEADS, HEAD_DIM_A)[:, ROW_TILE - keep:]
    kv_s = kv_s.reshape(dec_batch, dec_seq, 2, N_HEADS, HEAD_DIM_A)
    return (y_p.reshape(batch, seq, d), y_s.reshape(dec_batch, dec_seq, d),
            kv_p[:, :, 0][None], kv_p[:, :, 1][None], state_p[None],
            kv_s[:, :, 0][None], kv_s[:, :, 1][None], state_s[None])
```

```python
import functools

import numpy as np
import jax
import jax.numpy as jnp
from jax import lax
from jax.experimental import pallas as pl
from jax.experimental.pallas import tpu as pltpu
from jax.experimental.pallas import tpu_sc as plsc

F32 = jnp.float32
BF16 = jnp.bfloat16
I32 = jnp.int32
U32 = jnp.uint32

D_MODEL = 1024
PAST_LEN = 4096
CHUNK = 64
N_LEFT_CHUNKS = 8
ATT_WINDOW = N_LEFT_CHUNKS * CHUNK
N_HEADS = 8
HEAD_DIM_A = 64
D_ATT = N_HEADS * HEAD_DIM_A
REL_CLIP = 128
KEY_DIM_R = 64
VAL_DIM_R = 128
D_RET_K = N_HEADS * KEY_DIM_R
D_RET_V = N_HEADS * VAL_DIM_R
ROPE_BASE = 10000.0
N_EXPERTS = 256
TOP_K = 8
N_GROUPS = 8
GROUP_SIZE = N_EXPERTS // N_GROUPS
TOPK_GROUPS = 4
D_EXPERT = 256
ROUTED_SCALE = 2.5
EPS = 1e-6
IN_WIDTHS = (D_ATT, D_ATT, D_ATT, D_RET_K, D_RET_K, D_RET_V, D_RET_V, D_MODEL, D_MODEL)
IN_OFFS = tuple(int(v) for v in np.cumsum((0,) + IN_WIDTHS))
D_IN = IN_OFFS[-1]

NEG_BIG = -1e30
LANES = 128
V7X_VMEM_BYTES = 64 * 1024 * 1024
V7X_SC_CORES = 2
V7X_SC_SUBCORES = 16
SC_GATHER_ROWS = 64
SPARE_ROWS = 8192

ROW_TILE = 512
ATT_QB = 256
RET_CHUNK = 256
ROUTE_TILE = 512
MOVE_TILE = 256
FFN_QUANTUM = 128
FFN_MATMUL_ROWS = 2 * FFN_QUANTUM
FFN_TILE = 5 * FFN_QUANTUM
FFN_WEIGHT_SLOTS = 4


def _cparams(semantics, vmem_mb):
    return pltpu.CompilerParams(dimension_semantics=semantics,
                                vmem_limit_bytes=min(vmem_mb * 1024 * 1024, V7X_VMEM_BYTES - (6 << 20)))


def _silu(x):
    return x * jax.nn.sigmoid(x)


def _pack_bf16_pair(lo, hi):
    lo_b = pltpu.bitcast(lo.astype(BF16).astype(F32), U32) >> 16
    hi_b = pltpu.bitcast(hi.astype(BF16).astype(F32), U32) & jnp.uint32(0xFFFF0000)
    return lo_b | hi_b


def _unpack_bf16_pair(u):
    lo = pltpu.bitcast(u << 16, F32)
    hi = pltpu.bitcast(u & jnp.uint32(0xFFFF0000), F32)
    return lo, hi


def _ada_kernel(c_ref, w_ref, b_ref, o_ref):
    sc = _silu(c_ref[...]).astype(BF16)
    o_ref[...] = jnp.dot(sc, w_ref[...], preferred_element_type=F32) + b_ref[...]


def _ada(c_all, w_ada_bf, b_ada):
    rows = c_all.shape[0]
    n_out = w_ada_bf.shape[1]
    blk = D_MODEL
    return pl.pallas_call(
        _ada_kernel,
        grid=(n_out // blk,),
        in_specs=[pl.BlockSpec((rows, D_MODEL), lambda j: (0, 0)),
                  pl.BlockSpec((D_MODEL, blk), lambda j: (0, j)),
                  pl.BlockSpec((1, blk), lambda j: (0, j))],
        out_specs=pl.BlockSpec((rows, blk), lambda j: (0, j)),
        out_shape=jax.ShapeDtypeStruct((rows, n_out), F32),
        compiler_params=_cparams(("arbitrary",), 24),
        name="ada",
    )(c_all, w_ada_bf, b_ada)


def _inproj_kernel(x_ref, sc_ref, sh_ref, g_ref, cos_ref, sin_ref, w_ref,
                   qa_ref, ka_ref, va_ref, qr_ref, kr_ref, vr_ref, gr_ref, ga_ref, gb_ref,
                   kv_ref, *, tiles_per_batch):
    x = x_ref[...]
    xn = x * lax.rsqrt(jnp.mean(x * x, axis=-1, keepdims=True) + EPS) * g_ref[...]
    nb = (xn * (1.0 + sc_ref[0]) + sh_ref[0]).astype(BF16)

    def proj(seg):
        return jnp.dot(nb, w_ref[:, IN_OFFS[seg]:IN_OFFS[seg + 1]], preferred_element_type=F32)

    qa_ref[...] = proj(0).astype(BF16)
    ka = proj(1)
    va = proj(2)
    ka_ref[...] = ka.astype(BF16)
    va_ref[...] = va.astype(BF16)

    @pl.when(pl.program_id(0) % tiles_per_batch == tiles_per_batch - 1)
    def _():
        kv_ref[:, :D_ATT] = ka
        kv_ref[:, D_ATT:] = va

    cos = jnp.tile(cos_ref[...], (1, N_HEADS))
    sin = jnp.tile(sin_ref[...], (1, N_HEADS))
    first_half = (lax.broadcasted_iota(I32, (1, D_RET_K), 1) % KEY_DIM_R) < (KEY_DIM_R // 2)

    def rotary(t):
        partner = jnp.where(first_half, pltpu.roll(t, D_RET_K - KEY_DIM_R // 2, 1),
                            pltpu.roll(t, KEY_DIM_R // 2, 1))
        return t * cos + partner * sin

    qr_ref[...] = rotary(proj(3)).astype(BF16)
    kr_ref[...] = (rotary(proj(4)) * (KEY_DIM_R ** -0.5)).astype(BF16)
    vr_ref[...] = proj(5).astype(BF16)
    gr_ref[...] = proj(6).astype(BF16)
    ga_ref[...] = proj(7).astype(BF16)
    gb_ref[...] = proj(8).astype(BF16)


def _inproj(x2d, sc, sh, g, cos_t, sin_t, w_in_bf, tiles_per_batch):
    n = x2d.shape[0]
    tm = ROW_TILE
    n_tiles = n // tm
    n_batches = n_tiles // tiles_per_batch
    mod_rows = sc.shape[1]
    pos_tiles = cos_t.shape[0] // tm

    def row_spec(width):
        return pl.BlockSpec((tm, width), lambda i: (i, 0))

    mod_spec = pl.BlockSpec((1, mod_rows, D_MODEL), lambda i: (i // tiles_per_batch, 0, 0))
    pos_spec = pl.BlockSpec((tm, KEY_DIM_R), lambda i: (i % pos_tiles, 0))
    out_widths = (D_ATT, D_ATT, D_ATT, D_RET_K, D_RET_K, D_RET_V, D_RET_V, D_MODEL, D_MODEL)
    out_shape = [jax.ShapeDtypeStruct((n, w), BF16) for w in out_widths]
    out_shape.append(jax.ShapeDtypeStruct((n_batches * tm, 2 * D_ATT), F32))
    out_specs = [row_spec(w) for w in out_widths]
    out_specs.append(pl.BlockSpec((tm, 2 * D_ATT), lambda i: (i // tiles_per_batch, 0)))
    return pl.pallas_call(
        functools.partial(_inproj_kernel, tiles_per_batch=tiles_per_batch),
        grid=(n_tiles,),
        in_specs=[row_spec(D_MODEL), mod_spec, mod_spec,
                  pl.BlockSpec((1, D_MODEL), lambda i: (0, 0)),
                  pos_spec, pos_spec,
                  pl.BlockSpec((D_MODEL, D_IN), lambda i: (0, 0))],
        out_specs=out_specs,
        out_shape=out_shape,
        compiler_params=_cparams(("arbitrary",), 56),
        name="inproj",
    )(x2d, sc, sh, g, cos_t, sin_t, w_in_bf)


def _softmax_pv(s, v_parts):
    m = functools.reduce(jnp.maximum, [jnp.max(t, axis=-1, keepdims=True) for t in s])
    ps = [jnp.exp(t - m) for t in s]
    l = functools.reduce(jnp.add, [jnp.sum(p, axis=-1, keepdims=True) for p in ps])
    o = functools.reduce(jnp.add, [jnp.dot(p.astype(BF16), v, preferred_element_type=F32)
                                   for p, v in zip(ps, v_parts)])
    return o / l


def _attn_prompt_kernel(q_ref, k0_ref, k1_ref, k2_ref, v0_ref, v1_ref, v2_ref, bias_ref, o_ref):
    j = pl.program_id(1)
    q = q_ref[...]
    k = jnp.concatenate([k0_ref[...], k1_ref[...], k2_ref[...]], axis=0)
    v = jnp.concatenate([v0_ref[...], v1_ref[...], v2_ref[...]], axis=0)
    n_keys = k.shape[0]
    key_block = lax.broadcasted_iota(I32, (1, n_keys), 1) // ATT_QB
    before_start = jnp.where(key_block < 2 - j, NEG_BIG, 0.0)
    outs = []
    for h in range(N_HEADS):
        sl = slice(h * HEAD_DIM_A, (h + 1) * HEAD_DIM_A)
        qh = (q[:, sl].astype(F32) * (HEAD_DIM_A ** -0.5)).astype(BF16)
        s = lax.dot_general(qh, k[:, sl], (((1,), (1,)), ((), ())), preferred_element_type=F32)
        s = s + bias_ref[h] + before_start
        outs.append(_softmax_pv([s], [v[:, sl]]))
    o_ref[...] = jnp.concatenate(outs, axis=1).astype(BF16)


def _attn_prompt(q, k, v, bias_full, batch, seq):
    qb = ATT_QB
    nq = seq // qb

    def q_map(b, j):
        return (b * nq + j, 0)

    def kv_map(back):
        return lambda b, j: (b * nq + jnp.maximum(j - back, 0), 0)

    blk = lambda m: pl.BlockSpec((qb, D_ATT), m)
    return pl.pallas_call(
        _attn_prompt_kernel,
        grid=(batch, nq),
        in_specs=[blk(q_map), blk(kv_map(2)), blk(kv_map(1)), blk(kv_map(0)),
                  blk(kv_map(2)), blk(kv_map(1)), blk(kv_map(0)),
                  pl.BlockSpec(bias_full.shape, lambda b, j: (0, 0, 0))],
        out_specs=blk(q_map),
        out_shape=jax.ShapeDtypeStruct((batch * seq, D_ATT), BF16),
        compiler_params=_cparams(("parallel", "arbitrary"), 40),
        name="attn_prompt",
    )(q, k, k, k, v, v, v, bias_full)


SAMPLE_ATT_BATCHES = 2


def _attn_sample_kernel(q_ref, kn_ref, vn_ref, ck_ref, cv_ref, bc_ref, bn_ref, o_ref, *, t_new):
    nt = (((1,), (1,)), ((), ()))
    for b in range(SAMPLE_ATT_BATCHES):
        rows = slice(b * t_new, (b + 1) * t_new)
        q = q_ref[rows, :]
        kn = kn_ref[rows, :]
        vn = vn_ref[rows, :]
        outs = []
        for h in range(N_HEADS):
            sl = slice(h * HEAD_DIM_A, (h + 1) * HEAD_DIM_A)
            qh = (q[:, sl].astype(F32) * (HEAD_DIM_A ** -0.5)).astype(BF16)
            kc_t = ck_ref[b, h].astype(BF16)
            vc_t = cv_ref[b, h].astype(BF16)
            s_c = jnp.dot(qh, kc_t, preferred_element_type=F32) + bc_ref[h]
            s_n = lax.dot_general(qh, kn[:, sl], nt, preferred_element_type=F32) + bn_ref[h]
            m = jnp.maximum(jnp.max(s_c, axis=-1, keepdims=True), jnp.max(s_n, axis=-1, keepdims=True))
            p_c = jnp.exp(s_c - m)
            p_n = jnp.exp(s_n - m)
            l = jnp.sum(p_c, axis=-1, keepdims=True) + jnp.sum(p_n, axis=-1, keepdims=True)
            o = (lax.dot_general(p_c.astype(BF16), vc_t, nt, preferred_element_type=F32)
                 + jnp.dot(p_n.astype(BF16), vn[:, sl], preferred_element_type=F32))
            outs.append(o / l)
        o_ref[rows, :] = jnp.concatenate(outs, axis=1).astype(BF16)


def _attn_sample(q, k, v, cache_k_t, cache_v_t, bias_cache, bias_new, batch, t_new, cache_len):
    nb = SAMPLE_ATT_BATCHES
    blk = pl.BlockSpec((nb * t_new, D_ATT), lambda b: (b, 0))
    cblk = pl.BlockSpec((None, nb, N_HEADS, HEAD_DIM_A, cache_len), lambda b: (0, b, 0, 0, 0))
    return pl.pallas_call(
        functools.partial(_attn_sample_kernel, t_new=t_new),
        grid=(batch // nb,),
        in_specs=[blk, blk, blk, cblk, cblk,
                  pl.BlockSpec(bias_cache.shape, lambda b: (0, 0, 0)),
                  pl.BlockSpec(bias_new.shape, lambda b: (0, 0, 0))],
        out_specs=blk,
        out_shape=jax.ShapeDtypeStruct((batch * t_new, D_ATT), BF16),
        compiler_params=_cparams(("arbitrary",), 40),
        name="attn_sample",
    )(q, k, v, cache_k_t, cache_v_t, bias_cache, bias_new)


def _ret_kernel(q_ref, k_ref, v_ref, g_ref, s0_ref, dmask_ref, qdec_ref, kdec_ref, sdec_ref,
                y_ref, sout_ref, state_ref):
    c = pl.program_id(1)

    @pl.when(c == 0)
    def _():
        state_ref[...] = s0_ref[0]

    q = q_ref[...]
    k = k_ref[...]
    v = v_ref[...]
    g = g_ref[...]
    outs = []
    for h in range(N_HEADS):
        ks = slice(h * KEY_DIM_R, (h + 1) * KEY_DIM_R)
        vs = slice(h * VAL_DIM_R, (h + 1) * VAL_DIM_R)
        qh, kh, vh = q[:, ks], k[:, ks], v[:, vs]
        scores = lax.dot_general(qh, kh, (((1,), (1,)), ((), ())), preferred_element_type=F32)
        inner = jnp.dot((scores * dmask_ref[h]).astype(BF16), vh, preferred_element_type=F32)
        state = state_ref[h]
        cross = jnp.dot(qh, state.astype(BF16), preferred_element_type=F32) * qdec_ref[h]
        o = inner + cross
        v_dec = (vh.astype(F32) * kdec_ref[h]).astype(BF16)
        state_ref[h] = sdec_ref[h] * state + lax.dot_general(
            kh, v_dec, (((0,), (0,)), ((), ())), preferred_element_type=F32)
        on = o * lax.rsqrt(jnp.mean(o * o, axis=-1, keepdims=True) + EPS)
        outs.append(on * _silu(g[:, vs].astype(F32)))
    y_ref[...] = jnp.concatenate(outs, axis=1).astype(BF16)

    @pl.when(c == pl.num_programs(1) - 1)
    def _():
        sout_ref[0] = state_ref[...]


def _ret_tables(chunk):
    log_g = jnp.log(1.0 - jnp.exp2(-5.0 - jnp.arange(N_HEADS, dtype=F32)))
    i = jnp.arange(chunk, dtype=F32)
    diff = i[:, None] - i[None, :]
    dmask = jnp.where(diff >= 0, jnp.exp(log_g[:, None, None] * jnp.maximum(diff, 0.0)), 0.0)
    qdec = jnp.exp(log_g[:, None] * (i + 1.0))
    kdec = jnp.exp(log_g[:, None] * (chunk - 1.0 - i))
    sdec = jnp.exp(log_g * chunk)
    bc = lambda t: jnp.broadcast_to(t[:, :, None], (N_HEADS, t.shape[1], VAL_DIM_R)).astype(F32)
    sdec_t = jnp.broadcast_to(sdec[:, None, None], (N_HEADS, 1, VAL_DIM_R)).astype(F32)
    return dmask.astype(F32), bc(qdec), bc(kdec), sdec_t


def _retention(q, k, v, gate, state0, batch, seq, chunk):
    nc = seq // chunk
    dmask, qdec, kdec, sdec = _ret_tables(chunk)
    row = lambda w: pl.BlockSpec((chunk, w), lambda b, c: (b * nc + c, 0))
    const = lambda a: pl.BlockSpec(a.shape, lambda b, c: (0,) * a.ndim)
    st_spec = pl.BlockSpec((1, N_HEADS, KEY_DIM_R, VAL_DIM_R), lambda b, c: (b, 0, 0, 0))
    return pl.pallas_call(
        _ret_kernel,
        grid=(batch, nc),
        in_specs=[row(D_RET_K), row(D_RET_K), row(D_RET_V), row(D_RET_V), st_spec,
                  const(dmask), const(qdec), const(kdec), const(sdec)],
        out_specs=[row(D_RET_V), st_spec],
        out_shape=[jax.ShapeDtypeStruct((batch * seq, D_RET_V), BF16),
                   jax.ShapeDtypeStruct((batch, N_HEADS, KEY_DIM_R, VAL_DIM_R), F32)],
        scratch_shapes=[pltpu.VMEM((N_HEADS, KEY_DIM_R, VAL_DIM_R), F32)],
        compiler_params=_cparams(("parallel", "arbitrary"), 32),
        name="retention",
    )(q, k, v, gate, state0, dmask, qdec, kdec, sdec)


def _outproj_kernel(x_ref, oa_ref, yr_ref, ga_ref, gb_ref, g1_ref, sc2_ref, sh2_ref, g2_ref, n2g_ref,
                    woa_ref, wor_ref, wout_ref, wrt_ref, wsg_ref, wsu_ref, wsd_ref, *rest):
    h_ref, n2p_ref, s_ref = rest[-3:]
    ya = jnp.dot(oa_ref[...], woa_ref[...], preferred_element_type=F32)
    yr = jnp.dot(yr_ref[...], wor_ref[...], preferred_element_type=F32)
    merged = (jax.nn.sigmoid(ga_ref[...].astype(F32)) * ya
              + jax.nn.sigmoid(gb_ref[...].astype(F32)) * yr)
    mix = jnp.dot(merged.astype(BF16), wout_ref[...], preferred_element_type=F32)
    h = x_ref[...] + g1_ref[0] * mix
    hn = h * lax.rsqrt(jnp.mean(h * h, axis=-1, keepdims=True) + EPS) * n2g_ref[...]
    n2 = hn * (1.0 + sc2_ref[0]) + sh2_ref[0]
    n2b = n2.astype(BF16)
    s_ref[...] = jax.nn.sigmoid(lax.dot_general(wrt_ref[...], n2b, (((1,), (1,)), ((), ())),
                                                preferred_element_type=F32))
    hid = _silu(jnp.dot(n2b, wsg_ref[...], preferred_element_type=F32)) * jnp.dot(
        n2b, wsu_ref[...], preferred_element_type=F32)
    shared = jnp.dot(hid.astype(BF16), wsd_ref[...], preferred_element_type=F32)
    h_ref[...] = h + g2_ref[0] * shared
    half = D_MODEL // 2
    n2p_ref[...] = _pack_bf16_pair(n2[:, :half], n2[:, half:])


def _outproj(x2d, oa, yr_in, ga, gb, g1, sc2, sh2, g2, n2g, weights, tiles_per_batch,
             all_tokens, token_offset, carried=None):
    n = x2d.shape[0]
    tm = ROW_TILE
    off = token_offset // tm
    mod_rows = g1.shape[1]
    row = lambda w: pl.BlockSpec((tm, w), lambda i: (i, 0))
    mod_spec = pl.BlockSpec((1, mod_rows, D_MODEL), lambda i: (i // tiles_per_batch, 0, 0))
    const = lambda a: pl.BlockSpec(a.shape, lambda i: (0,) * a.ndim)
    in_specs = [row(D_MODEL), row(D_ATT), row(D_RET_V), row(D_MODEL), row(D_MODEL),
                mod_spec, mod_spec, mod_spec, mod_spec, const(n2g)] + [const(w) for w in weights]
    args = [x2d, oa, yr_in, ga, gb, g1, sc2, sh2, g2, n2g, *weights]
    aliases = {}
    if carried is not None:
        aliases = {len(args): 1, len(args) + 1: 2}
        in_specs += [pl.BlockSpec(memory_space=pl.ANY)] * 2
        args += list(carried)
    return pl.pallas_call(
        _outproj_kernel,
        grid=(n // tm,),
        in_specs=in_specs,
        out_specs=[row(D_MODEL),
                   pl.BlockSpec((tm, D_MODEL // 2), lambda i: (i + off, 0)),
                   pl.BlockSpec((N_EXPERTS, tm), lambda i: (0, i + off))],
        out_shape=[jax.ShapeDtypeStruct((n, D_MODEL), F32),
                   jax.ShapeDtypeStruct((all_tokens, D_MODEL // 2), U32),
                   jax.ShapeDtypeStruct((N_EXPERTS, all_tokens), F32)],
        input_output_aliases=aliases,
        compiler_params=_cparams(("arbitrary",), 48),
        name="outproj",
    )(*args)


def _route_kernel(s_ref, b_ref, idx_ref, w_ref, rank_ref, cnt_ref, run_ref, tri_ref):
    step = pl.program_id(0)
    t = s_ref.shape[1]

    @pl.when(step == 0)
    def _():
        run_ref[...] = jnp.zeros_like(run_ref)
        r = lax.broadcasted_iota(I32, (t, t), 0)
        c = lax.broadcasted_iota(I32, (t, t), 1)
        tri_ref[...] = jnp.where(r < c, 1.0, 0.0).astype(BF16)

    s = s_ref[...]
    sel = s + b_ref[...]
    row_f = lax.broadcasted_iota(I32, (N_EXPERTS, t), 0).astype(F32)

    def first_argmax(vals, rows):
        m = jnp.max(vals, axis=0, keepdims=True)
        pos = jnp.min(jnp.where(vals == m, rows, float(N_EXPERTS)), axis=0, keepdims=True)
        return m, pos

    gscore = []
    group_row = lax.broadcasted_iota(I32, (GROUP_SIZE, t), 0).astype(F32)
    for g in range(N_GROUPS):
        rows = slice(g * GROUP_SIZE, (g + 1) * GROUP_SIZE)
        m1, p1 = first_argmax(sel[rows], group_row)
        m2 = jnp.max(jnp.where(group_row == p1, -jnp.inf, sel[rows]), axis=0, keepdims=True)
        gscore.append(m1 + m2)
    cand_parts = []
    for g in range(N_GROUPS):
        rows = slice(g * GROUP_SIZE, (g + 1) * GROUP_SIZE)
        beaten_by = jnp.zeros((1, t), F32)
        for o in range(N_GROUPS):
            if o == g:
                continue
            wins = (gscore[o] > gscore[g]) if o > g else (gscore[o] >= gscore[g])
            beaten_by = beaten_by + jnp.where(wins, 1.0, 0.0)
        cand_parts.append(jnp.where(beaten_by < TOPK_GROUPS, sel[rows], -jnp.inf))
    cand = jnp.concatenate(cand_parts, axis=0)

    picked = jnp.zeros((N_EXPERTS, t), F32)
    idx_rows, w_rows = [], []
    for _ in range(TOP_K):
        _, pos = first_argmax(cand, row_f)
        hit = row_f == pos
        w_rows.append(jnp.sum(jnp.where(hit, s, 0.0), axis=0, keepdims=True))
        idx_rows.append(pos)
        picked = jnp.where(hit, 1.0, picked)
        cand = jnp.where(hit, -jnp.inf, cand)
    w_sum = functools.reduce(jnp.add, w_rows)

    before = jnp.dot(picked.astype(BF16), tri_ref[...], preferred_element_type=F32) + run_ref[...]
    run_ref[...] = run_ref[...] + jnp.sum(picked, axis=1, keepdims=True)
    rank_rows = [jnp.sum(jnp.where(row_f == idx_rows[kk], before, 0.0), axis=0, keepdims=True)
                 for kk in range(TOP_K)]

    idx_ref[...] = jnp.concatenate(idx_rows, axis=0).astype(I32)
    w_ref[...] = jnp.concatenate([w / w_sum * ROUTED_SCALE for w in w_rows], axis=0)
    rank_ref[...] = jnp.concatenate(rank_rows, axis=0).astype(I32)

    @pl.when(step == pl.num_programs(0) - 1)
    def _():
        cnt_ref[...] = run_ref[...].astype(I32)


def _route(scores_t, b_col):
    n = scores_t.shape[1]
    t = ROUTE_TILE
    col = pl.BlockSpec((TOP_K, t), lambda i: (0, i))
    const = pl.BlockSpec((N_EXPERTS, t), lambda i: (0, 0))
    return pl.pallas_call(
        _route_kernel,
        grid=(n // t,),
        in_specs=[pl.BlockSpec((N_EXPERTS, t), lambda i: (0, i)), const],
        out_specs=[col, col, col, const],
        out_shape=[jax.ShapeDtypeStruct((TOP_K, n), I32),
                   jax.ShapeDtypeStruct((TOP_K, n), F32),
                   jax.ShapeDtypeStruct((TOP_K, n), I32),
                   jax.ShapeDtypeStruct((N_EXPERTS, t), I32)],
        scratch_shapes=[pltpu.VMEM((N_EXPERTS, t), F32), pltpu.VMEM((t, t), BF16)],
        compiler_params=_cparams(("arbitrary",), 32),
        name="route",
    )(scores_t, b_col)


def _dest_kernel(idx_ref, rank_ref, start_ref, dest_ref):
    t = idx_ref.shape[1]
    row = lax.broadcasted_iota(I32, (N_EXPERTS, t), 0)
    starts = start_ref[...]
    base = [jnp.sum(jnp.where(row == idx_ref[kk:kk + 1, :], starts, 0.0), axis=0, keepdims=True)
            for kk in range(TOP_K)]
    dest_ref[...] = jnp.concatenate(base, axis=0).astype(I32) + rank_ref[...]


def _dest(idx_t, rank_t, starts_col):
    n = idx_t.shape[1]
    t = ROUTE_TILE
    col = pl.BlockSpec((TOP_K, t), lambda i: (0, i))
    return pl.pallas_call(
        _dest_kernel,
        grid=(n // t,),
        in_specs=[col, col, pl.BlockSpec((N_EXPERTS, t), lambda i: (0, 0))],
        out_specs=col,
        out_shape=jax.ShapeDtypeStruct((TOP_K, n), I32),
        compiler_params=_cparams(("arbitrary",), 32),
        name="dest",
    )(idx_t, rank_t, starts_col)


def _sc_mesh():
    return plsc.VectorSubcoreMesh(core_axis_name="core", subcore_axis_name="subcore",
                                  num_cores=V7X_SC_CORES, num_subcores=V7X_SC_SUBCORES)


def _sc_dispatch(n2p, dest_kmajor, pad_rows, n_out_rows):
    n, width = n2p.shape
    rows = SC_GATHER_ROWS
    n_workers = V7X_SC_CORES * V7X_SC_SUBCORES
    src_chunks = n // rows
    items = dest_kmajor.shape[0] // rows
    per_worker = items // n_workers
    pad_per_worker = pad_rows.shape[0] // rows // n_workers
    assert src_chunks * rows == n and per_worker * n_workers == items and per_worker % 2 == 0
    assert pad_per_worker * n_workers * rows == pad_rows.shape[0]
    idx3 = dest_kmajor.reshape(n_workers, per_worker, rows)
    pad3 = pad_rows.reshape(n_workers, pad_per_worker, rows)
    zeros = jnp.zeros((rows, width), n2p.dtype)

    def body(src_hbm, idx_hbm, pad_hbm, zero_hbm, out_hbm, idx_v, pad_v, rows_v, load_sem, scat_sem):
        worker = lax.axis_index("subcore") * V7X_SC_CORES + lax.axis_index("core")
        pltpu.sync_copy(idx_hbm.at[worker], idx_v)
        pltpu.sync_copy(pad_hbm.at[worker], pad_v)
        pltpu.sync_copy(zero_hbm, rows_v.at[0])

        def zero_fill(c):
            return pltpu.make_async_copy(rows_v.at[0], out_hbm.at[pad_v.at[c]], scat_sem.at[0])

        @pl.loop(0, pad_per_worker)
        def _(c):
            zero_fill(c).start()

        @pl.loop(0, pad_per_worker)
        def _(c):
            zero_fill(c).wait()

        def load(c, b):
            chunk = lax.rem(worker * per_worker + c, src_chunks)
            off = pl.multiple_of(chunk * rows, rows)
            return pltpu.make_async_copy(src_hbm.at[pl.ds(off, rows)], rows_v.at[b], load_sem.at[b])

        def scatter(c, b):
            return pltpu.make_async_copy(rows_v.at[b], out_hbm.at[idx_v.at[c]], scat_sem.at[b])

        load(0, 0).start()

        @pl.loop(0, per_worker, step=2)
        def _(c0):
            for b in range(2):
                c = c0 + b
                load(c, b).wait()

                @pl.when(c >= 1)
                def _():
                    scatter(c - 1, 1 - b).wait()

                @pl.when(c + 1 < per_worker)
                def _():
                    load(c + 1, 1 - b).start()

                scatter(c, b).start()

        scatter(per_worker - 1, (per_worker - 1) % 2).wait()

    return pl.kernel(
        body, mesh=_sc_mesh(),
        out_type=jax.ShapeDtypeStruct((n_out_rows, width), n2p.dtype),
        scratch_types=[pltpu.VMEM((per_worker, rows), I32),
                       pltpu.VMEM((pad_per_worker, rows), I32),
                       pltpu.VMEM((2, rows, width), n2p.dtype),
                       pltpu.SemaphoreType.DMA((2,)),
                       pltpu.SemaphoreType.DMA((2,))],
        name="sc_dispatch",
    )(n2p, idx3, pad3, zeros)


def _ffn_kernel(texp_ref, ntiles_ref, eslot_ref, enext_ref, nhalf_ref, xs_ref, wg_hbm, wu_hbm, wd_hbm,
                ys_ref, wg_buf, wu_buf, wd_buf, wgu_bf, wd_bf, sems):
    g = pl.program_id(0)

    def weight_copies(e, slot):
        return (pltpu.make_async_copy(wg_hbm.at[e], wg_buf.at[slot], sems.at[slot, 0]),
                pltpu.make_async_copy(wu_hbm.at[e], wu_buf.at[slot], sems.at[slot, 1]),
                pltpu.make_async_copy(wd_hbm.at[e], wd_buf.at[slot], sems.at[slot, 2]))

    @pl.when(g < ntiles_ref[0])
    def _():
        e = texp_ref[g]
        changed = jnp.logical_or(g == 0, texp_ref[jnp.maximum(g - 1, 0)] != e)

        @pl.when(changed)
        def _():
            slot = eslot_ref[e]

            def fetch_ahead(first_hop, hops, target_slot):
                ahead = first_hop
                for _ in range(hops - 1):
                    ahead = jnp.where(ahead >= 0, enext_ref[jnp.maximum(ahead, 0)], -1)

                @pl.when(ahead >= 0)
                def _():
                    for c in weight_copies(ahead, target_slot):
                        c.start()

            @pl.when(g == 0)
            def _():
                for c in weight_copies(e, slot):
                    c.start()
                for hops in range(1, FFN_WEIGHT_SLOTS - 1):
                    fetch_ahead(enext_ref[e], hops, lax.rem(slot + hops, FFN_WEIGHT_SLOTS))

            fetch_ahead(enext_ref[e], FFN_WEIGHT_SLOTS - 1,
                        lax.rem(slot + FFN_WEIGHT_SLOTS - 1, FFN_WEIGHT_SLOTS))
            for c in weight_copies(e, slot):
                c.wait()

            wgu_bf[:, :D_EXPERT] = wg_buf[slot].astype(BF16)
            wgu_bf[:, D_EXPERT:] = wu_buf[slot].astype(BF16)
            wd_bf[...] = wd_buf[slot].astype(BF16)

        def expert_rows(rows):
            lo, hi = _unpack_bf16_pair(xs_ref[rows, :])
            x = jnp.concatenate([lo, hi], axis=1).astype(BF16)
            gu = jnp.dot(x, wgu_bf[...], preferred_element_type=F32)
            hid = (_silu(gu[:, :D_EXPERT]) * gu[:, D_EXPERT:]).astype(BF16)
            y = jnp.dot(hid, wd_bf[...], preferred_element_type=F32)
            half = D_MODEL // 2
            ys_ref[rows, :] = _pack_bf16_pair(y[:, :half], y[:, half:])

        for n_groups in range(1, FFN_TILE // FFN_QUANTUM + 1):

            @pl.when(nhalf_ref[g] == n_groups)
            def _(n_groups=n_groups):
                used = n_groups * FFN_QUANTUM
                for start in range(0, used, FFN_MATMUL_ROWS):
                    expert_rows(slice(start, min(start + FFN_MATMUL_ROWS, used)))
                if used < FFN_TILE:
                    ys_ref[used:, :] = jnp.zeros((FFN_TILE - used, ys_ref.shape[1]), U32)


def _ffn(tile_expert, n_tiles, expert_slot, expert_next, tile_halves, xs, w_gate, w_up, w_down):
    rows, width = xs.shape
    m = FFN_TILE
    max_tiles = tile_expert.shape[0]
    row_map = lambda g, te, nt, es, en, nh: (jnp.minimum(g, nt[0] - 1), 0)
    hbm = pl.BlockSpec(memory_space=pl.ANY)
    grid_spec = pltpu.PrefetchScalarGridSpec(
        num_scalar_prefetch=5,
        grid=(max_tiles,),
        in_specs=[pl.BlockSpec((m, width), row_map), hbm, hbm, hbm],
        out_specs=pl.BlockSpec((m, width), row_map),
        scratch_shapes=[pltpu.VMEM((FFN_WEIGHT_SLOTS, D_MODEL, D_EXPERT), F32),
                        pltpu.VMEM((FFN_WEIGHT_SLOTS, D_MODEL, D_EXPERT), F32),
                        pltpu.VMEM((FFN_WEIGHT_SLOTS, D_EXPERT, D_MODEL), F32),
                        pltpu.VMEM((D_MODEL, 2 * D_EXPERT), BF16),
                        pltpu.VMEM((D_EXPERT, D_MODEL), BF16),
                        pltpu.SemaphoreType.DMA((FFN_WEIGHT_SLOTS, 3))],
    )
    return pl.pallas_call(
        _ffn_kernel,
        grid_spec=grid_spec,
        out_shape=jax.ShapeDtypeStruct((rows, width), U32),
        compiler_params=_cparams(("arbitrary",), 32),
        name="ffn",
    )(tile_expert, n_tiles, expert_slot, expert_next, tile_halves, xs, w_gate, w_up, w_down)


def _ffn_plan(counts, n_assign):
    m = FFN_TILE
    max_tiles = n_assign // m + N_EXPERTS
    padded = ((counts + m - 1) // m) * m
    pend = jnp.cumsum(padded).astype(I32)
    pstart = pend - padded
    n_tiles = pend[-1:] // m
    g = jnp.minimum(jnp.arange(max_tiles, dtype=I32), n_tiles - 1)
    tile_expert = jnp.sum((pend[None, :] <= (g * m)[:, None]).astype(I32), axis=1)
    tile_expert = jnp.minimum(tile_expert, N_EXPERTS - 1)
    vend = pstart + ((counts + FFN_QUANTUM - 1) // FFN_QUANTUM) * FFN_QUANTUM
    own = tile_expert[:, None] == jnp.arange(N_EXPERTS, dtype=I32)[None, :]
    tile_vend = jnp.sum(jnp.where(own, vend[None, :], 0), axis=1)
    tile_halves = (jnp.clip(tile_vend - g * m, 0, m) // FFN_QUANTUM).astype(I32)
    used = counts > 0
    expert_slot = ((jnp.cumsum(used.astype(I32)) - 1) % FFN_WEIGHT_SLOTS).astype(I32)
    ids = jnp.where(used, jnp.arange(N_EXPERTS, dtype=I32), N_EXPERTS)
    first_used_from = lax.cummin(ids, axis=0, reverse=True)
    nxt = jnp.concatenate([first_used_from[1:], jnp.full((1,), N_EXPERTS, I32)])
    expert_next = jnp.where(nxt < N_EXPERTS, nxt, -1).astype(I32)
    spare_row = max_tiles * m
    j = jnp.arange(FFN_QUANTUM, dtype=I32)[None, :]
    seg_end = (pstart + counts)[:, None]
    spare = spare_row + (jnp.arange(N_EXPERTS, dtype=I32)[:, None] * FFN_QUANTUM + j) % SPARE_ROWS
    pad_rows = jnp.where(j < (vend[:, None] - seg_end), seg_end + j, spare).astype(I32).reshape(-1)
    return (tile_expert, n_tiles, expert_slot, expert_next, tile_halves, pstart, pad_rows,
            spare_row + SPARE_ROWS)


def _sc_gather_rows(table, idx):
    n_idx = idx.shape[0]
    width = table.shape[1]
    n_workers = V7X_SC_CORES * V7X_SC_SUBCORES
    per_worker = n_idx // n_workers
    n_chunks = per_worker // SC_GATHER_ROWS
    assert per_worker * n_workers == n_idx and n_chunks * SC_GATHER_ROWS == per_worker and n_chunks % 2 == 0

    def body(table_hbm, idx_hbm, out_hbm, idx_v, rows_v, gather_sem, write_sem):
        worker = lax.axis_index("subcore") * V7X_SC_CORES + lax.axis_index("core")
        base = worker * per_worker
        pltpu.sync_copy(idx_hbm.at[pl.ds(base, per_worker)], idx_v)

        def gather(c, b):
            off = pl.multiple_of(c * SC_GATHER_ROWS, SC_GATHER_ROWS)
            return pltpu.make_async_copy(table_hbm.at[idx_v.at[pl.ds(off, SC_GATHER_ROWS)]],
                                         rows_v.at[b], gather_sem.at[b])

        def write(c, b):
            off = pl.multiple_of(c * SC_GATHER_ROWS, SC_GATHER_ROWS)
            return pltpu.make_async_copy(rows_v.at[b], out_hbm.at[pl.ds(base + off, SC_GATHER_ROWS)],
                                         write_sem.at[b])

        gather(0, 0).start()

        @pl.loop(0, n_chunks, step=2)
        def _(c0):
            for b in range(2):
                c = c0 + b
                gather(c, b).wait()

                @pl.when(c >= 1)
                def _():
                    write(c - 1, 1 - b).wait()

                @pl.when(c + 1 < n_chunks)
                def _():
                    gather(c + 1, 1 - b).start()

                write(c, b).start()

        write(n_chunks - 1, (n_chunks - 1) % 2).wait()

    return pl.kernel(
        body, mesh=_sc_mesh(),
        out_type=jax.ShapeDtypeStruct((n_idx, width), table.dtype),
        scratch_types=[pltpu.VMEM((per_worker,), I32),
                       pltpu.VMEM((2, SC_GATHER_ROWS, width), table.dtype),
                       pltpu.SemaphoreType.DMA((2,)),
                       pltpu.SemaphoreType.DMA((2,))],
        name="sc_gather_rows",
    )(table, idx)


def _combine_kernel(w_ref, h_ref, g2_ref, nf_ref, yk_ref, y_ref):
    t = h_ref.shape[0]
    w = w_ref[...]
    acc_lo = jnp.zeros((t, D_MODEL // 2), F32)
    acc_hi = jnp.zeros((t, D_MODEL // 2), F32)
    for kk in range(TOP_K):
        lo, hi = _unpack_bf16_pair(yk_ref[kk])
        wk = w[:, kk:kk + 1]
        acc_lo = acc_lo + wk * lo
        acc_hi = acc_hi + wk * hi
    out = h_ref[...] + g2_ref[0] * jnp.concatenate([acc_lo, acc_hi], axis=1)
    y_ref[...] = out * lax.rsqrt(jnp.mean(out * out, axis=-1, keepdims=True) + EPS) * nf_ref[...]


def _combine(w, h2, g2, normf, y_by_k, row_offset, tiles_per_batch):
    n = h2.shape[0]
    t = MOVE_TILE
    off = row_offset // t
    mod_rows = g2.shape[1]
    mod_tiles = max(ROW_TILE // t, 1) * tiles_per_batch if mod_rows == 1 else n // t
    return pl.pallas_call(
        _combine_kernel,
        grid=(n // t,),
        in_specs=[pl.BlockSpec((t, LANES), lambda i: (i + off, 0)),
                  pl.BlockSpec((t, D_MODEL), lambda i: (i, 0)),
                  pl.BlockSpec((1, mod_rows if mod_rows == 1 else t, D_MODEL),
                               (lambda i: (i // mod_tiles, 0, 0)) if mod_rows == 1
                               else (lambda i: (0, i, 0))),
                  pl.BlockSpec((1, D_MODEL), lambda i: (0, 0)),
                  pl.BlockSpec((TOP_K, t, D_MODEL // 2), lambda i: (0, i + off, 0))],
        out_specs=pl.BlockSpec((t, D_MODEL), lambda i: (i, 0)),
        out_shape=jax.ShapeDtypeStruct((n, D_MODEL), F32),
        compiler_params=_cparams(("arbitrary",), 40),
        name="combine",
    )(w, h2, g2, normf, y_by_k)


def _rotary_tables(pos):
    half = KEY_DIM_R // 2
    inv_freq = ROPE_BASE ** (-jnp.arange(half, dtype=F32) / half)
    ang = pos[:, None] * inv_freq[None, :]
    cos = jnp.cos(ang)
    sin = jnp.sin(ang)
    cos_t = jnp.concatenate([cos, cos], axis=1)
    sin_t = jnp.concatenate([-sin, sin], axis=1)
    return cos_t.astype(F32), sin_t.astype(F32)


def _rel_bias_table(rel_bias, n_rows, n_cols, q_offset):
    heads = rel_bias.shape[0]
    n_diag = n_rows + n_cols - 1
    dist = q_offset + (n_rows - 1) - np.arange(n_diag)
    idx = np.clip(dist, -REL_CLIP, REL_CLIP) + REL_CLIP
    n_hi = int(np.sum(dist > REL_CLIP))
    n_lo = int(np.sum(dist < -REL_CLIP))
    mid = rel_bias[:, int(idx[n_diag - n_lo - 1]):int(idx[n_hi]) + 1][:, ::-1]
    diag = jnp.concatenate([jnp.broadcast_to(rel_bias[:, 2 * REL_CLIP:], (heads, n_hi)), mid,
                            jnp.broadcast_to(rel_bias[:, :1], (heads, n_lo))], axis=1)
    period = n_diag + 1
    v = jnp.roll(jnp.pad(diag, ((0, 0), (0, 1))), -(n_rows - 1), axis=1)
    skew = jnp.tile(v, (1, n_rows))[:, :n_rows * (period - 1)].reshape(heads, n_rows, period - 1)
    return skew[:, :, :n_cols].astype(F32)


def _prompt_bias(rel_bias):
    n_cols = ATT_QB + ATT_WINDOW
    r = np.arange(ATT_QB)[:, None]
    c = np.arange(n_cols)[None, :]
    band = c - (r // CHUNK) * CHUNK
    valid = (band >= 0) & (band < ATT_WINDOW + CHUNK)
    table = _rel_bias_table(rel_bias, ATT_QB, n_cols, ATT_WINDOW)
    return jnp.where(jnp.asarray(valid)[None], table, NEG_BIG)


def _sample_bias(rel_bias, t_new, cache_len):
    b = _rel_bias_table(rel_bias, t_new, cache_len + t_new, cache_len)
    return b[:, :, :cache_len], b[:, :, cache_len:]


def _mod_parts(mod, rows_each):
    parts = jnp.split(mod, 6, axis=-1)
    if rows_each == 1:
        return [p[:, None, :] for p in parts]
    return [jnp.repeat(p, rows_each, axis=0)[None] for p in parts]


def kernel(x_prompt, x_sample, cache_attn_k, cache_attn_v, state_ret, c_prompt, c_sample,
           norm1_g, norm2_g, w_ada, b_ada, w_in, rel_bias, w_o_attn, w_o_ret, w_out,
           w_router, b_router, w_exp_gate, w_exp_up, w_exp_down, w_sh_gate, w_sh_up, w_sh_down,
           normf_g):
    batch, seq, d = x_prompt.shape
    dec_batch, dec_seq, _ = x_sample.shape
    depth = w_in.shape[0]
    assert depth == 1 and d == D_MODEL
    assert seq % ROW_TILE == 0 and dec_batch * dec_seq == ROW_TILE and ROW_TILE == ATT_WINDOW
    cache_len = cache_attn_k.shape[2]
    n_p = batch * seq
    n_s = dec_batch * dec_seq
    tpb = seq // ROW_TILE
    l = 0

    bf = lambda a: a.astype(BF16)
    c_all = jnp.concatenate([c_prompt, c_sample], axis=0)
    pad = (-c_all.shape[0]) % 8
    c_all = jnp.pad(c_all, ((0, pad), (0, 0)))
    mod = _ada(c_all, bf(w_ada[l]), b_ada[l][None, :])
    mod_p = _mod_parts(mod[:batch], 1)
    mod_s = _mod_parts(mod[batch:batch + dec_batch], dec_seq)

    w_in_bf = bf(w_in[l])
    n1g = norm1_g[l][None, :]
    n2g = norm2_g[l][None, :]
    dense_w = [bf(w_o_attn[l]), bf(w_o_ret[l]), bf(w_out[l]), bf(w_router[l]).T,
               bf(w_sh_gate[l]), bf(w_sh_up[l]), bf(w_sh_down[l])]

    xp = x_prompt.reshape(n_p, d)
    xs_ = x_sample.reshape(n_s, d)
    cos_p, sin_p = _rotary_tables(jnp.arange(seq, dtype=F32))
    pos_s = PAST_LEN + jnp.arange(dec_seq, dtype=F32)
    cos_s, sin_s = _rotary_tables(jnp.tile(pos_s, dec_batch))

    (qa, ka, va, qr, kr, vr, gr, ga, gb, kv_p) = _inproj(
        xp, mod_p[1], mod_p[0], n1g, cos_p, sin_p, w_in_bf, tpb)
    oa = _attn_prompt(qa, ka, va, _prompt_bias(rel_bias[l]), batch, seq)
    zero_state = jnp.zeros((batch, N_HEADS, KEY_DIM_R, VAL_DIM_R), F32)
    yr_in, state_p = _retention(qr, kr, vr, gr, zero_state, batch, seq, RET_CHUNK)
    h_p, n2p_p, s_p = _outproj(xp, oa, yr_in, ga, gb, mod_p[2], mod_p[4], mod_p[3], mod_p[5], n2g,
                               dense_w, tpb, n_p + n_s, 0)

    (qa_s, ka_s, va_s, qr_s, kr_s, vr_s, gr_s, ga_s, gb_s, kv_s) = _inproj(
        xs_, mod_s[1], mod_s[0], n1g, cos_s, sin_s, w_in_bf, 1)
    bias_c, bias_n = _sample_bias(rel_bias[l], dec_seq, cache_len)
    to_keys_minor = lambda c: jnp.transpose(c, (0, 1, 3, 4, 2))
    oa_s = _attn_sample(qa_s, ka_s, va_s, to_keys_minor(cache_attn_k), to_keys_minor(cache_attn_v),
                        bias_c, bias_n, dec_batch, dec_seq, cache_len)
    yr_in_s, state_s = _retention(qr_s, kr_s, vr_s, gr_s, state_ret[l], dec_batch, dec_seq, dec_seq)
    h_s, n2p, scores_t = _outproj(xs_, oa_s, yr_in_s, ga_s, gb_s, mod_s[2], mod_s[4], mod_s[3],
                                  mod_s[5], n2g, dense_w, 1, n_p + n_s, n_p, carried=(n2p_p, s_p))

    lanes_of = lambda v: jnp.broadcast_to(v[:, None], (N_EXPERTS, ROUTE_TILE))
    idx_t, w_t, rank_t, counts = _route(scores_t, lanes_of(b_router[l]))
    (tile_expert, n_tiles, expert_slot, expert_next, tile_halves, pstart, pad_rows,
     n_sorted_rows) = _ffn_plan(counts[:, 0], (n_p + n_s) * TOP_K)
    dest_t = _dest(idx_t, rank_t, lanes_of(pstart.astype(F32)))
    dest_kmajor = dest_t.reshape(-1)
    w_route = jnp.pad(w_t.T, ((0, 0), (0, LANES - TOP_K)))
    xs_sorted = _sc_dispatch(n2p, dest_kmajor, pad_rows, n_sorted_rows)
    ys_sorted = _ffn(tile_expert, n_tiles, expert_slot, expert_next, tile_halves, xs_sorted,
                     w_exp_gate[l], w_exp_up[l], w_exp_down[l])
    nf = normf_g[None, :]
    y_by_k = _sc_gather_rows(ys_sorted, dest_kmajor).reshape(TOP_K, n_p + n_s, d // 2)
    y_p = _combine(w_route, h_p, mod_p[5], nf, y_by_k, 0, tpb)
    y_s = _combine(w_route, h_s, mod_s[5], nf, y_by_k, n_p, 1)

    keep = min(ATT_WINDOW, seq)
    kv_p = kv_p.reshape(batch, ROW_TILE, 2, N_HEADS, HEAD_DIM_A)[:, ROW_TILE - keep:]
    kv_s = kv_s.reshape(dec_batch, dec_seq, 2, N_HEADS, HEAD_DIM_A)
    return (y_p.reshape(batch, seq, d), y_s.reshape(dec_batch, dec_seq, d),
            kv_p[:, :, 0][None], kv_p[:, :, 1][None], state_p[None],
            kv_s[:, :, 0][None], kv_s[:, :, 1][None], state_s[None])
```

```python
import functools

import numpy as np
import jax
import jax.numpy as jnp
from jax import lax
from jax.experimental import pallas as pl
from jax.experimental.pallas import tpu as pltpu
from jax.experimental.pallas import tpu_sc as plsc

F32 = jnp.float32
BF16 = jnp.bfloat16
I32 = jnp.int32
U32 = jnp.uint32

D_MODEL = 1024
PAST_LEN = 4096
CHUNK = 64
N_LEFT_CHUNKS = 8
ATT_WINDOW = N_LEFT_CHUNKS * CHUNK
N_HEADS = 8
HEAD_DIM_A = 64
D_ATT = N_HEADS * HEAD_DIM_A
REL_CLIP = 128
KEY_DIM_R = 64
VAL_DIM_R = 128
D_RET_K = N_HEADS * KEY_DIM_R
D_RET_V = N_HEADS * VAL_DIM_R
ROPE_BASE = 10000.0
N_EXPERTS = 256
TOP_K = 8
N_GROUPS = 8
GROUP_SIZE = N_EXPERTS // N_GROUPS
TOPK_GROUPS = 4
D_EXPERT = 256
ROUTED_SCALE = 2.5
EPS = 1e-6
IN_WIDTHS = (D_ATT, D_ATT, D_ATT, D_RET_K, D_RET_K, D_RET_V, D_RET_V, D_MODEL, D_MODEL)
IN_OFFS = tuple(int(v) for v in np.cumsum((0,) + IN_WIDTHS))
D_IN = IN_OFFS[-1]

NEG_BIG = -1e30
LANES = 128
V7X_VMEM_BYTES = 64 * 1024 * 1024
V7X_SC_CORES = 2
V7X_SC_SUBCORES = 16
SC_GATHER_ROWS = 96
SPARE_ROWS = 8192

ROW_TILE = 512
ATT_QB = 256
RET_CHUNK = 256
ROUTE_TILE = 512
MOVE_TILE = 256
FFN_QUANTUM = 128
FFN_MATMUL_ROWS = 2 * FFN_QUANTUM
FFN_TILE = 5 * FFN_QUANTUM
FFN_WEIGHT_SLOTS = 4


def _cparams(semantics, vmem_mb):
    return pltpu.CompilerParams(dimension_semantics=semantics,
                                vmem_limit_bytes=min(vmem_mb * 1024 * 1024, V7X_VMEM_BYTES - (6 << 20)))


def _silu(x):
    return x * jax.nn.sigmoid(x)


def _pack_bf16_pair(lo, hi):
    lo_b = pltpu.bitcast(lo.astype(BF16).astype(F32), U32) >> 16
    hi_b = pltpu.bitcast(hi.astype(BF16).astype(F32), U32) & jnp.uint32(0xFFFF0000)
    return lo_b | hi_b


def _unpack_bf16_pair(u):
    lo = pltpu.bitcast(u << 16, F32)
    hi = pltpu.bitcast(u & jnp.uint32(0xFFFF0000), F32)
    return lo, hi


def _ada_kernel(c_ref, w_ref, b_ref, o_ref):
    sc = _silu(c_ref[...]).astype(BF16)
    o_ref[...] = jnp.dot(sc, w_ref[...], preferred_element_type=F32) + b_ref[...]


def _ada(c_all, w_ada_bf, b_ada):
    rows = c_all.shape[0]
    n_out = w_ada_bf.shape[1]
    blk = D_MODEL
    return pl.pallas_call(
        _ada_kernel,
        grid=(n_out // blk,),
        in_specs=[pl.BlockSpec((rows, D_MODEL), lambda j: (0, 0)),
                  pl.BlockSpec((D_MODEL, blk), lambda j: (0, j)),
                  pl.BlockSpec((1, blk), lambda j: (0, j))],
        out_specs=pl.BlockSpec((rows, blk), lambda j: (0, j)),
        out_shape=jax.ShapeDtypeStruct((rows, n_out), F32),
        compiler_params=_cparams(("arbitrary",), 24),
        name="ada",
    )(c_all, w_ada_bf, b_ada)


def _inproj_kernel(x_ref, sc_ref, sh_ref, g_ref, cos_ref, sin_ref, w_ref,
                   qa_ref, ka_ref, va_ref, qr_ref, kr_ref, vr_ref, gr_ref, ga_ref, gb_ref,
                   kv_ref, *, tiles_per_batch):
    x = x_ref[...]
    xn = x * lax.rsqrt(jnp.mean(x * x, axis=-1, keepdims=True) + EPS) * g_ref[...]
    nb = (xn * (1.0 + sc_ref[0]) + sh_ref[0]).astype(BF16)

    def proj(seg):
        return jnp.dot(nb, w_ref[:, IN_OFFS[seg]:IN_OFFS[seg + 1]], preferred_element_type=F32)

    qa_ref[...] = proj(0).astype(BF16)
    ka = proj(1)
    va = proj(2)
    ka_ref[...] = ka.astype(BF16)
    va_ref[...] = va.astype(BF16)

    @pl.when(pl.program_id(0) % tiles_per_batch == tiles_per_batch - 1)
    def _():
        kv_ref[:, :D_ATT] = ka
        kv_ref[:, D_ATT:] = va

    cos = jnp.tile(cos_ref[...], (1, N_HEADS))
    sin = jnp.tile(sin_ref[...], (1, N_HEADS))
    first_half = (lax.broadcasted_iota(I32, (1, D_RET_K), 1) % KEY_DIM_R) < (KEY_DIM_R // 2)

    def rotary(t):
        partner = jnp.where(first_half, pltpu.roll(t, D_RET_K - KEY_DIM_R // 2, 1),
                            pltpu.roll(t, KEY_DIM_R // 2, 1))
        return t * cos + partner * sin

    qr_ref[...] = rotary(proj(3)).astype(BF16)
    kr_ref[...] = (rotary(proj(4)) * (KEY_DIM_R ** -0.5)).astype(BF16)
    vr_ref[...] = proj(5).astype(BF16)
    gr_ref[...] = proj(6).astype(BF16)
    ga_ref[...] = proj(7).astype(BF16)
    gb_ref[...] = proj(8).astype(BF16)


def _inproj(x2d, sc, sh, g, cos_t, sin_t, w_in_bf, tiles_per_batch):
    n = x2d.shape[0]
    tm = ROW_TILE
    n_tiles = n // tm
    n_batches = n_tiles // tiles_per_batch
    mod_rows = sc.shape[1]
    pos_tiles = cos_t.shape[0] // tm

    def row_spec(width):
        return pl.BlockSpec((tm, width), lambda i: (i, 0))

    mod_spec = pl.BlockSpec((1, mod_rows, D_MODEL), lambda i: (i // tiles_per_batch, 0, 0))
    pos_spec = pl.BlockSpec((tm, KEY_DIM_R), lambda i: (i % pos_tiles, 0))
    out_widths = (D_ATT, D_ATT, D_ATT, D_RET_K, D_RET_K, D_RET_V, D_RET_V, D_MODEL, D_MODEL)
    out_shape = [jax.ShapeDtypeStruct((n, w), BF16) for w in out_widths]
    out_shape.append(jax.ShapeDtypeStruct((n_batches * tm, 2 * D_ATT), F32))
    out_specs = [row_spec(w) for w in out_widths]
    out_specs.append(pl.BlockSpec((tm, 2 * D_ATT), lambda i: (i // tiles_per_batch, 0)))
    return pl.pallas_call(
        functools.partial(_inproj_kernel, tiles_per_batch=tiles_per_batch),
        grid=(n_tiles,),
        in_specs=[row_spec(D_MODEL), mod_spec, mod_spec,
                  pl.BlockSpec((1, D_MODEL), lambda i: (0, 0)),
                  pos_spec, pos_spec,
                  pl.BlockSpec((D_MODEL, D_IN), lambda i: (0, 0))],
        out_specs=out_specs,
        out_shape=out_shape,
        compiler_params=_cparams(("arbitrary",), 56),
        name="inproj",
    )(x2d, sc, sh, g, cos_t, sin_t, w_in_bf)


def _softmax_pv(s, v_parts):
    m = functools.reduce(jnp.maximum, [jnp.max(t, axis=-1, keepdims=True) for t in s])
    ps = [jnp.exp(t - m) for t in s]
    l = functools.reduce(jnp.add, [jnp.sum(p, axis=-1, keepdims=True) for p in ps])
    o = functools.reduce(jnp.add, [jnp.dot(p.astype(BF16), v, preferred_element_type=F32)
                                   for p, v in zip(ps, v_parts)])
    return o / l


def _attn_prompt_kernel(q_ref, k0_ref, k1_ref, k2_ref, v0_ref, v1_ref, v2_ref, bias_ref, o_ref):
    j = pl.program_id(1)
    q = q_ref[...]
    k = jnp.concatenate([k0_ref[...], k1_ref[...], k2_ref[...]], axis=0)
    v = jnp.concatenate([v0_ref[...], v1_ref[...], v2_ref[...]], axis=0)
    n_keys = k.shape[0]
    key_block = lax.broadcasted_iota(I32, (1, n_keys), 1) // ATT_QB
    before_start = jnp.where(key_block < 2 - j, NEG_BIG, 0.0)
    outs = []
    for h in range(N_HEADS):
        sl = slice(h * HEAD_DIM_A, (h + 1) * HEAD_DIM_A)
        qh = (q[:, sl].astype(F32) * (HEAD_DIM_A ** -0.5)).astype(BF16)
        s = lax.dot_general(qh, k[:, sl], (((1,), (1,)), ((), ())), preferred_element_type=F32)
        s = s + bias_ref[h] + before_start
        outs.append(_softmax_pv([s], [v[:, sl]]))
    o_ref[...] = jnp.concatenate(outs, axis=1).astype(BF16)


def _attn_prompt(q, k, v, bias_full, batch, seq):
    qb = ATT_QB
    nq = seq // qb

    def q_map(b, j):
        return (b * nq + j, 0)

    def kv_map(back):
        return lambda b, j: (b * nq + jnp.maximum(j - back, 0), 0)

    blk = lambda m: pl.BlockSpec((qb, D_ATT), m)
    return pl.pallas_call(
        _attn_prompt_kernel,
        grid=(batch, nq),
        in_specs=[blk(q_map), blk(kv_map(2)), blk(kv_map(1)), blk(kv_map(0)),
                  blk(kv_map(2)), blk(kv_map(1)), blk(kv_map(0)),
                  pl.BlockSpec(bias_full.shape, lambda b, j: (0, 0, 0))],
        out_specs=blk(q_map),
        out_shape=jax.ShapeDtypeStruct((batch * seq, D_ATT), BF16),
        compiler_params=_cparams(("parallel", "arbitrary"), 40),
        name="attn_prompt",
    )(q, k, k, k, v, v, v, bias_full)


SAMPLE_ATT_BATCHES = 2


def _attn_sample_kernel(q_ref, kn_ref, vn_ref, ck_ref, cv_ref, bc_ref, bn_ref, o_ref, *, t_new):
    nt = (((1,), (1,)), ((), ()))
    for b in range(SAMPLE_ATT_BATCHES):
        rows = slice(b * t_new, (b + 1) * t_new)
        q = q_ref[rows, :]
        kn = kn_ref[rows, :]
        vn = vn_ref[rows, :]
        outs = []
        for h in range(N_HEADS):
            sl = slice(h * HEAD_DIM_A, (h + 1) * HEAD_DIM_A)
            qh = (q[:, sl].astype(F32) * (HEAD_DIM_A ** -0.5)).astype(BF16)
            kc_t = ck_ref[b, h].astype(BF16)
            vc_t = cv_ref[b, h].astype(BF16)
            s_c = jnp.dot(qh, kc_t, preferred_element_type=F32) + bc_ref[h]
            s_n = lax.dot_general(qh, kn[:, sl], nt, preferred_element_type=F32) + bn_ref[h]
            m = jnp.maximum(jnp.max(s_c, axis=-1, keepdims=True), jnp.max(s_n, axis=-1, keepdims=True))
            p_c = jnp.exp(s_c - m)
            p_n = jnp.exp(s_n - m)
            l = jnp.sum(p_c, axis=-1, keepdims=True) + jnp.sum(p_n, axis=-1, keepdims=True)
            o = (lax.dot_general(p_c.astype(BF16), vc_t, nt, preferred_element_type=F32)
                 + jnp.dot(p_n.astype(BF16), vn[:, sl], preferred_element_type=F32))
            outs.append(o / l)
        o_ref[rows, :] = jnp.concatenate(outs, axis=1).astype(BF16)


def _attn_sample(q, k, v, cache_k_t, cache_v_t, bias_cache, bias_new, batch, t_new, cache_len):
    nb = SAMPLE_ATT_BATCHES
    blk = pl.BlockSpec((nb * t_new, D_ATT), lambda b: (b, 0))
    cblk = pl.BlockSpec((None, nb, N_HEADS, HEAD_DIM_A, cache_len), lambda b: (0, b, 0, 0, 0))
    return pl.pallas_call(
        functools.partial(_attn_sample_kernel, t_new=t_new),
        grid=(batch // nb,),
        in_specs=[blk, blk, blk, cblk, cblk,
                  pl.BlockSpec(bias_cache.shape, lambda b: (0, 0, 0)),
                  pl.BlockSpec(bias_new.shape, lambda b: (0, 0, 0))],
        out_specs=blk,
        out_shape=jax.ShapeDtypeStruct((batch * t_new, D_ATT), BF16),
        compiler_params=_cparams(("arbitrary",), 40),
        name="attn_sample",
    )(q, k, v, cache_k_t, cache_v_t, bias_cache, bias_new)


def _ret_kernel(q_ref, k_ref, v_ref, g_ref, s0_ref, dmask_ref, qdec_ref, kdec_ref, sdec_ref,
                y_ref, sout_ref, state_ref):
    c = pl.program_id(1)

    @pl.when(c == 0)
    def _():
        state_ref[...] = s0_ref[0]

    q = q_ref[...]
    k = k_ref[...]
    v = v_ref[...]
    g = g_ref[...]
    outs = []
    for h in range(N_HEADS):
        ks = slice(h * KEY_DIM_R, (h + 1) * KEY_DIM_R)
        vs = slice(h * VAL_DIM_R, (h + 1) * VAL_DIM_R)
        qh, kh, vh = q[:, ks], k[:, ks], v[:, vs]
        scores = lax.dot_general(qh, kh, (((1,), (1,)), ((), ())), preferred_element_type=F32)
        inner = jnp.dot((scores * dmask_ref[h]).astype(BF16), vh, preferred_element_type=F32)
        state = state_ref[h]
        cross = jnp.dot(qh, state.astype(BF16), preferred_element_type=F32) * qdec_ref[h]
        o = inner + cross
        v_dec = (vh.astype(F32) * kdec_ref[h]).astype(BF16)
        state_ref[h] = sdec_ref[h] * state + lax.dot_general(
            kh, v_dec, (((0,), (0,)), ((), ())), preferred_element_type=F32)
        on = o * lax.rsqrt(jnp.mean(o * o, axis=-1, keepdims=True) + EPS)
        outs.append(on * _silu(g[:, vs].astype(F32)))
    y_ref[...] = jnp.concatenate(outs, axis=1).astype(BF16)

    @pl.when(c == pl.num_programs(1) - 1)
    def _():
        sout_ref[0] = state_ref[...]


def _ret_tables(chunk):
    log_g = jnp.log(1.0 - jnp.exp2(-5.0 - jnp.arange(N_HEADS, dtype=F32)))
    i = jnp.arange(chunk, dtype=F32)
    diff = i[:, None] - i[None, :]
    dmask = jnp.where(diff >= 0, jnp.exp(log_g[:, None, None] * jnp.maximum(diff, 0.0)), 0.0)
    qdec = jnp.exp(log_g[:, None] * (i + 1.0))
    kdec = jnp.exp(log_g[:, None] * (chunk - 1.0 - i))
    sdec = jnp.exp(log_g * chunk)
    bc = lambda t: jnp.broadcast_to(t[:, :, None], (N_HEADS, t.shape[1], VAL_DIM_R)).astype(F32)
    sdec_t = jnp.broadcast_to(sdec[:, None, None], (N_HEADS, 1, VAL_DIM_R)).astype(F32)
    return dmask.astype(F32), bc(qdec), bc(kdec), sdec_t


def _retention(q, k, v, gate, state0, batch, seq, chunk):
    nc = seq // chunk
    dmask, qdec, kdec, sdec = _ret_tables(chunk)
    row = lambda w: pl.BlockSpec((chunk, w), lambda b, c: (b * nc + c, 0))
    const = lambda a: pl.BlockSpec(a.shape, lambda b, c: (0,) * a.ndim)
    st_spec = pl.BlockSpec((1, N_HEADS, KEY_DIM_R, VAL_DIM_R), lambda b, c: (b, 0, 0, 0))
    return pl.pallas_call(
        _ret_kernel,
        grid=(batch, nc),
        in_specs=[row(D_RET_K), row(D_RET_K), row(D_RET_V), row(D_RET_V), st_spec,
                  const(dmask), const(qdec), const(kdec), const(sdec)],
        out_specs=[row(D_RET_V), st_spec],
        out_shape=[jax.ShapeDtypeStruct((batch * seq, D_RET_V), BF16),
                   jax.ShapeDtypeStruct((batch, N_HEADS, KEY_DIM_R, VAL_DIM_R), F32)],
        scratch_shapes=[pltpu.VMEM((N_HEADS, KEY_DIM_R, VAL_DIM_R), F32)],
        compiler_params=_cparams(("parallel", "arbitrary"), 32),
        name="retention",
    )(q, k, v, gate, state0, dmask, qdec, kdec, sdec)


def _outproj_kernel(x_ref, oa_ref, yr_ref, ga_ref, gb_ref, g1_ref, sc2_ref, sh2_ref, g2_ref, n2g_ref,
                    woa_ref, wor_ref, wout_ref, wrt_ref, wsg_ref, wsu_ref, wsd_ref, *rest):
    h_ref, n2p_ref, s_ref = rest[-3:]
    ya = jnp.dot(oa_ref[...], woa_ref[...], preferred_element_type=F32)
    yr = jnp.dot(yr_ref[...], wor_ref[...], preferred_element_type=F32)
    merged = (jax.nn.sigmoid(ga_ref[...].astype(F32)) * ya
              + jax.nn.sigmoid(gb_ref[...].astype(F32)) * yr)
    mix = jnp.dot(merged.astype(BF16), wout_ref[...], preferred_element_type=F32)
    h = x_ref[...] + g1_ref[0] * mix
    hn = h * lax.rsqrt(jnp.mean(h * h, axis=-1, keepdims=True) + EPS) * n2g_ref[...]
    n2 = hn * (1.0 + sc2_ref[0]) + sh2_ref[0]
    n2b = n2.astype(BF16)
    s_ref[...] = jax.nn.sigmoid(lax.dot_general(wrt_ref[...], n2b, (((1,), (1,)), ((), ())),
                                                preferred_element_type=F32))
    hid = _silu(jnp.dot(n2b, wsg_ref[...], preferred_element_type=F32)) * jnp.dot(
        n2b, wsu_ref[...], preferred_element_type=F32)
    shared = jnp.dot(hid.astype(BF16), wsd_ref[...], preferred_element_type=F32)
    h_ref[...] = h + g2_ref[0] * shared
    half = D_MODEL // 2
    n2p_ref[...] = _pack_bf16_pair(n2[:, :half], n2[:, half:])


def _outproj(x2d, oa, yr_in, ga, gb, g1, sc2, sh2, g2, n2g, weights, tiles_per_batch,
             all_tokens, token_offset, carried=None):
    n = x2d.shape[0]
    tm = ROW_TILE
    off = token_offset // tm
    mod_rows = g1.shape[1]
    row = lambda w: pl.BlockSpec((tm, w), lambda i: (i, 0))
    mod_spec = pl.BlockSpec((1, mod_rows, D_MODEL), lambda i: (i // tiles_per_batch, 0, 0))
    const = lambda a: pl.BlockSpec(a.shape, lambda i: (0,) * a.ndim)
    in_specs = [row(D_MODEL), row(D_ATT), row(D_RET_V), row(D_MODEL), row(D_MODEL),
                mod_spec, mod_spec, mod_spec, mod_spec, const(n2g)] + [const(w) for w in weights]
    args = [x2d, oa, yr_in, ga, gb, g1, sc2, sh2, g2, n2g, *weights]
    aliases = {}
    if carried is not None:
        aliases = {len(args): 1, len(args) + 1: 2}
        in_specs += [pl.BlockSpec(memory_space=pl.ANY)] * 2
        args += list(carried)
    return pl.pallas_call(
        _outproj_kernel,
        grid=(n // tm,),
        in_specs=in_specs,
        out_specs=[row(D_MODEL),
                   pl.BlockSpec((tm, D_MODEL // 2), lambda i: (i + off, 0)),
                   pl.BlockSpec((N_EXPERTS, tm), lambda i: (0, i + off))],
        out_shape=[jax.ShapeDtypeStruct((n, D_MODEL), F32),
                   jax.ShapeDtypeStruct((all_tokens, D_MODEL // 2), U32),
                   jax.ShapeDtypeStruct((N_EXPERTS, all_tokens), F32)],
        input_output_aliases=aliases,
        compiler_params=_cparams(("arbitrary",), 48),
        name="outproj",
    )(*args)


def _route_kernel(s_ref, b_ref, idx_ref, w_ref, rank_ref, cnt_ref, run_ref, tri_ref):
    step = pl.program_id(0)
    t = s_ref.shape[1]

    @pl.when(step == 0)
    def _():
        run_ref[...] = jnp.zeros_like(run_ref)
        r = lax.broadcasted_iota(I32, (t, t), 0)
        c = lax.broadcasted_iota(I32, (t, t), 1)
        tri_ref[...] = jnp.where(r < c, 1.0, 0.0).astype(BF16)

    s = s_ref[...]
    sel = s + b_ref[...]
    row_f = lax.broadcasted_iota(I32, (N_EXPERTS, t), 0).astype(F32)

    def first_argmax(vals, rows):
        m = jnp.max(vals, axis=0, keepdims=True)
        pos = jnp.min(jnp.where(vals == m, rows, float(N_EXPERTS)), axis=0, keepdims=True)
        return m, pos

    gscore = []
    group_row = lax.broadcasted_iota(I32, (GROUP_SIZE, t), 0).astype(F32)
    for g in range(N_GROUPS):
        rows = slice(g * GROUP_SIZE, (g + 1) * GROUP_SIZE)
        m1, p1 = first_argmax(sel[rows], group_row)
        m2 = jnp.max(jnp.where(group_row == p1, -jnp.inf, sel[rows]), axis=0, keepdims=True)
        gscore.append(m1 + m2)
    cand_parts = []
    for g in range(N_GROUPS):
        rows = slice(g * GROUP_SIZE, (g + 1) * GROUP_SIZE)
        beaten_by = jnp.zeros((1, t), F32)
        for o in range(N_GROUPS):
            if o == g:
                continue
            wins = (gscore[o] > gscore[g]) if o > g else (gscore[o] >= gscore[g])
            beaten_by = beaten_by + jnp.where(wins, 1.0, 0.0)
        cand_parts.append(jnp.where(beaten_by < TOPK_GROUPS, sel[rows], -jnp.inf))
    cand = jnp.concatenate(cand_parts, axis=0)

    picked = jnp.zeros((N_EXPERTS, t), F32)
    idx_rows, w_rows = [], []
    for _ in range(TOP_K):
        _, pos = first_argmax(cand, row_f)
        hit = row_f == pos
        w_rows.append(jnp.sum(jnp.where(hit, s, 0.0), axis=0, keepdims=True))
        idx_rows.append(pos)
        picked = jnp.where(hit, 1.0, picked)
        cand = jnp.where(hit, -jnp.inf, cand)
    w_sum = functools.reduce(jnp.add, w_rows)

    before = jnp.dot(picked.astype(BF16), tri_ref[...], preferred_element_type=F32) + run_ref[...]
    run_ref[...] = run_ref[...] + jnp.sum(picked, axis=1, keepdims=True)
    rank_rows = [jnp.sum(jnp.where(row_f == idx_rows[kk], before, 0.0), axis=0, keepdims=True)
                 for kk in range(TOP_K)]

    idx_ref[...] = jnp.concatenate(idx_rows, axis=0).astype(I32)
    w_ref[...] = jnp.concatenate([w / w_sum * ROUTED_SCALE for w in w_rows], axis=0)
    rank_ref[...] = jnp.concatenate(rank_rows, axis=0).astype(I32)

    @pl.when(step == pl.num_programs(0) - 1)
    def _():
        cnt_ref[...] = run_ref[...].astype(I32)


def _route(scores_t, b_col):
    n = scores_t.shape[1]
    t = ROUTE_TILE
    col = pl.BlockSpec((TOP_K, t), lambda i: (0, i))
    const = pl.BlockSpec((N_EXPERTS, t), lambda i: (0, 0))
    return pl.pallas_call(
        _route_kernel,
        grid=(n // t,),
        in_specs=[pl.BlockSpec((N_EXPERTS, t), lambda i: (0, i)), const],
        out_specs=[col, col, col, const],
        out_shape=[jax.ShapeDtypeStruct((TOP_K, n), I32),
                   jax.ShapeDtypeStruct((TOP_K, n), F32),
                   jax.ShapeDtypeStruct((TOP_K, n), I32),
                   jax.ShapeDtypeStruct((N_EXPERTS, t), I32)],
        scratch_shapes=[pltpu.VMEM((N_EXPERTS, t), F32), pltpu.VMEM((t, t), BF16)],
        compiler_params=_cparams(("arbitrary",), 32),
        name="route",
    )(scores_t, b_col)


def _dest_kernel(idx_ref, rank_ref, start_ref, dest_ref):
    t = idx_ref.shape[1]
    row = lax.broadcasted_iota(I32, (N_EXPERTS, t), 0)
    starts = start_ref[...]
    base = [jnp.sum(jnp.where(row == idx_ref[kk:kk + 1, :], starts, 0.0), axis=0, keepdims=True)
            for kk in range(TOP_K)]
    dest_ref[...] = jnp.concatenate(base, axis=0).astype(I32) + rank_ref[...]


def _dest(idx_t, rank_t, starts_col):
    n = idx_t.shape[1]
    t = ROUTE_TILE
    col = pl.BlockSpec((TOP_K, t), lambda i: (0, i))
    return pl.pallas_call(
        _dest_kernel,
        grid=(n // t,),
        in_specs=[col, col, pl.BlockSpec((N_EXPERTS, t), lambda i: (0, 0))],
        out_specs=col,
        out_shape=jax.ShapeDtypeStruct((TOP_K, n), I32),
        compiler_params=_cparams(("arbitrary",), 32),
        name="dest",
    )(idx_t, rank_t, starts_col)


def _sc_mesh():
    return plsc.VectorSubcoreMesh(core_axis_name="core", subcore_axis_name="subcore",
                                  num_cores=V7X_SC_CORES, num_subcores=V7X_SC_SUBCORES)


def _sc_dispatch(n2p, dest_kmajor, pad_rows, n_out_rows):
    n, width = n2p.shape
    rows = SC_GATHER_ROWS
    n_workers = V7X_SC_CORES * V7X_SC_SUBCORES
    src_chunks = n // rows
    items = dest_kmajor.shape[0] // rows
    per_worker = items // n_workers
    pad_per_worker = pad_rows.shape[0] // rows // n_workers
    assert src_chunks * rows == n and per_worker * n_workers == items and per_worker % 2 == 0
    assert pad_per_worker * n_workers * rows == pad_rows.shape[0]
    idx3 = dest_kmajor.reshape(n_workers, per_worker, rows)
    pad3 = pad_rows.reshape(n_workers, pad_per_worker, rows)
    zeros = jnp.zeros((rows, width), n2p.dtype)

    def body(src_hbm, idx_hbm, pad_hbm, zero_hbm, out_hbm, idx_v, pad_v, rows_v, load_sem, scat_sem):
        worker = lax.axis_index("subcore") * V7X_SC_CORES + lax.axis_index("core")
        pltpu.sync_copy(idx_hbm.at[worker], idx_v)
        pltpu.sync_copy(pad_hbm.at[worker], pad_v)
        pltpu.sync_copy(zero_hbm, rows_v.at[0])

        def zero_fill(c):
            return pltpu.make_async_copy(rows_v.at[0], out_hbm.at[pad_v.at[c]], scat_sem.at[0])

        @pl.loop(0, pad_per_worker)
        def _(c):
            zero_fill(c).start()

        @pl.loop(0, pad_per_worker)
        def _(c):
            zero_fill(c).wait()

        def load(c, b):
            chunk = lax.rem(worker * per_worker + c, src_chunks)
            off = pl.multiple_of(chunk * rows, rows)
            return pltpu.make_async_copy(src_hbm.at[pl.ds(off, rows)], rows_v.at[b], load_sem.at[b])

        def scatter(c, b):
            return pltpu.make_async_copy(rows_v.at[b], out_hbm.at[idx_v.at[c]], scat_sem.at[b])

        load(0, 0).start()

        @pl.loop(0, per_worker, step=2)
        def _(c0):
            for b in range(2):
                c = c0 + b
                load(c, b).wait()

                @pl.when(c >= 1)
                def _():
                    scatter(c - 1, 1 - b).wait()

                @pl.when(c + 1 < per_worker)
                def _():
                    load(c + 1, 1 - b).start()

                scatter(c, b).start()

        scatter(per_worker - 1, (per_worker - 1) % 2).wait()

    return pl.kernel(
        body, mesh=_sc_mesh(),
        out_type=jax.ShapeDtypeStruct((n_out_rows, width), n2p.dtype),
        scratch_types=[pltpu.VMEM((per_worker, rows), I32),
                       pltpu.VMEM((pad_per_worker, rows), I32),
                       pltpu.VMEM((2, rows, width), n2p.dtype),
                       pltpu.SemaphoreType.DMA((2,)),
                       pltpu.SemaphoreType.DMA((2,))],
        name="sc_dispatch",
    )(n2p, idx3, pad3, zeros)


def _ffn_kernel(texp_ref, ntiles_ref, eslot_ref, enext_ref, nhalf_ref, xs_ref, wg_hbm, wu_hbm, wd_hbm,
                ys_ref, wg_buf, wu_buf, wd_buf, wgu_bf, wd_bf, sems):
    g = pl.program_id(0)

    def weight_copies(e, slot):
        return (pltpu.make_async_copy(wg_hbm.at[e], wg_buf.at[slot], sems.at[slot, 0]),
                pltpu.make_async_copy(wu_hbm.at[e], wu_buf.at[slot], sems.at[slot, 1]),
                pltpu.make_async_copy(wd_hbm.at[e], wd_buf.at[slot], sems.at[slot, 2]))

    @pl.when(g < ntiles_ref[0])
    def _():
        e = texp_ref[g]
        changed = jnp.logical_or(g == 0, texp_ref[jnp.maximum(g - 1, 0)] != e)

        @pl.when(changed)
        def _():
            slot = eslot_ref[e]

            def fetch_ahead(first_hop, hops, target_slot):
                ahead = first_hop
                for _ in range(hops - 1):
                    ahead = jnp.where(ahead >= 0, enext_ref[jnp.maximum(ahead, 0)], -1)

                @pl.when(ahead >= 0)
                def _():
                    for c in weight_copies(ahead, target_slot):
                        c.start()

            @pl.when(g == 0)
            def _():
                for c in weight_copies(e, slot):
                    c.start()
                for hops in range(1, FFN_WEIGHT_SLOTS - 1):
                    fetch_ahead(enext_ref[e], hops, lax.rem(slot + hops, FFN_WEIGHT_SLOTS))

            fetch_ahead(enext_ref[e], FFN_WEIGHT_SLOTS - 1,
                        lax.rem(slot + FFN_WEIGHT_SLOTS - 1, FFN_WEIGHT_SLOTS))
            for c in weight_copies(e, slot):
                c.wait()

            wgu_bf[:, :D_EXPERT] = wg_buf[slot].astype(BF16)
            wgu_bf[:, D_EXPERT:] = wu_buf[slot].astype(BF16)
            wd_bf[...] = wd_buf[slot].astype(BF16)

        def expert_rows(rows):
            lo, hi = _unpack_bf16_pair(xs_ref[rows, :])
            x = jnp.concatenate([lo, hi], axis=1).astype(BF16)
            gu = jnp.dot(x, wgu_bf[...], preferred_element_type=F32)
            hid = (_silu(gu[:, :D_EXPERT]) * gu[:, D_EXPERT:]).astype(BF16)
            y = jnp.dot(hid, wd_bf[...], preferred_element_type=F32)
            half = D_MODEL // 2
            ys_ref[rows, :] = _pack_bf16_pair(y[:, :half], y[:, half:])

        for n_groups in range(1, FFN_TILE // FFN_QUANTUM + 1):

            @pl.when(nhalf_ref[g] == n_groups)
            def _(n_groups=n_groups):
                used = n_groups * FFN_QUANTUM
                for start in range(0, used, FFN_MATMUL_ROWS):
                    expert_rows(slice(start, min(start + FFN_MATMUL_ROWS, used)))
                if used < FFN_TILE:
                    ys_ref[used:, :] = jnp.zeros((FFN_TILE - used, ys_ref.shape[1]), U32)


def _ffn(tile_expert, n_tiles, expert_slot, expert_next, tile_halves, xs, w_gate, w_up, w_down):
    rows, width = xs.shape
    m = FFN_TILE
    max_tiles = tile_expert.shape[0]
    row_map = lambda g, te, nt, es, en, nh: (jnp.minimum(g, nt[0] - 1), 0)
    hbm = pl.BlockSpec(memory_space=pl.ANY)
    grid_spec = pltpu.PrefetchScalarGridSpec(
        num_scalar_prefetch=5,
        grid=(max_tiles,),
        in_specs=[pl.BlockSpec((m, width), row_map), hbm, hbm, hbm],
        out_specs=pl.BlockSpec((m, width), row_map),
        scratch_shapes=[pltpu.VMEM((FFN_WEIGHT_SLOTS, D_MODEL, D_EXPERT), F32),
                        pltpu.VMEM((FFN_WEIGHT_SLOTS, D_MODEL, D_EXPERT), F32),
                        pltpu.VMEM((FFN_WEIGHT_SLOTS, D_EXPERT, D_MODEL), F32),
                        pltpu.VMEM((D_MODEL, 2 * D_EXPERT), BF16),
                        pltpu.VMEM((D_EXPERT, D_MODEL), BF16),
                        pltpu.SemaphoreType.DMA((FFN_WEIGHT_SLOTS, 3))],
    )
    return pl.pallas_call(
        _ffn_kernel,
        grid_spec=grid_spec,
        out_shape=jax.ShapeDtypeStruct((rows, width), U32),
        compiler_params=_cparams(("arbitrary",), 32),
        name="ffn",
    )(tile_expert, n_tiles, expert_slot, expert_next, tile_halves, xs, w_gate, w_up, w_down)


def _ffn_plan(counts, n_assign):
    m = FFN_TILE
    max_tiles = n_assign // m + N_EXPERTS
    padded = ((counts + m - 1) // m) * m
    pend = jnp.cumsum(padded).astype(I32)
    pstart = pend - padded
    n_tiles = pend[-1:] // m
    g = jnp.minimum(jnp.arange(max_tiles, dtype=I32), n_tiles - 1)
    tile_expert = jnp.sum((pend[None, :] <= (g * m)[:, None]).astype(I32), axis=1)
    tile_expert = jnp.minimum(tile_expert, N_EXPERTS - 1)
    vend = pstart + ((counts + FFN_QUANTUM - 1) // FFN_QUANTUM) * FFN_QUANTUM
    own = tile_expert[:, None] == jnp.arange(N_EXPERTS, dtype=I32)[None, :]
    tile_vend = jnp.sum(jnp.where(own, vend[None, :], 0), axis=1)
    tile_halves = (jnp.clip(tile_vend - g * m, 0, m) // FFN_QUANTUM).astype(I32)
    used = counts > 0
    expert_slot = ((jnp.cumsum(used.astype(I32)) - 1) % FFN_WEIGHT_SLOTS).astype(I32)
    ids = jnp.where(used, jnp.arange(N_EXPERTS, dtype=I32), N_EXPERTS)
    first_used_from = lax.cummin(ids, axis=0, reverse=True)
    nxt = jnp.concatenate([first_used_from[1:], jnp.full((1,), N_EXPERTS, I32)])
    expert_next = jnp.where(nxt < N_EXPERTS, nxt, -1).astype(I32)
    spare_row = max_tiles * m
    j = jnp.arange(FFN_QUANTUM, dtype=I32)[None, :]
    seg_end = (pstart + counts)[:, None]
    spare = spare_row + (jnp.arange(N_EXPERTS, dtype=I32)[:, None] * FFN_QUANTUM + j) % SPARE_ROWS
    pad_rows = jnp.where(j < (vend[:, None] - seg_end), seg_end + j, spare).astype(I32).reshape(-1)
    unit = SC_GATHER_ROWS * V7X_SC_CORES * V7X_SC_SUBCORES
    extra = (-pad_rows.shape[0]) % unit
    filler = spare_row + (pad_rows.shape[0] + jnp.arange(extra, dtype=I32)) % SPARE_ROWS
    pad_rows = jnp.concatenate([pad_rows, filler])
    return (tile_expert, n_tiles, expert_slot, expert_next, tile_halves, pstart, pad_rows,
            spare_row + SPARE_ROWS)


def _sc_gather_rows(table, idx):
    n_idx = idx.shape[0]
    width = table.shape[1]
    n_workers = V7X_SC_CORES * V7X_SC_SUBCORES
    per_worker = n_idx // n_workers
    n_chunks = per_worker // SC_GATHER_ROWS
    assert per_worker * n_workers == n_idx and n_chunks * SC_GATHER_ROWS == per_worker and n_chunks % 2 == 0

    def body(table_hbm, idx_hbm, out_hbm, idx_v, rows_v, gather_sem, write_sem):
        worker = lax.axis_index("subcore") * V7X_SC_CORES + lax.axis_index("core")
        base = worker * per_worker
        pltpu.sync_copy(idx_hbm.at[pl.ds(base, per_worker)], idx_v)

        def gather(c, b):
            off = pl.multiple_of(c * SC_GATHER_ROWS, SC_GATHER_ROWS)
            return pltpu.make_async_copy(table_hbm.at[idx_v.at[pl.ds(off, SC_GATHER_ROWS)]],
                                         rows_v.at[b], gather_sem.at[b])

        def write(c, b):
            off = pl.multiple_of(c * SC_GATHER_ROWS, SC_GATHER_ROWS)
            return pltpu.make_async_copy(rows_v.at[b], out_hbm.at[pl.ds(base + off, SC_GATHER_ROWS)],
                                         write_sem.at[b])

        gather(0, 0).start()

        @pl.loop(0, n_chunks, step=2)
        def _(c0):
            for b in range(2):
                c = c0 + b
                gather(c, b).wait()

                @pl.when(c >= 1)
                def _():
                    write(c - 1, 1 - b).wait()

                @pl.when(c + 1 < n_chunks)
                def _():
                    gather(c + 1, 1 - b).start()

                write(c, b).start()

        write(n_chunks - 1, (n_chunks - 1) % 2).wait()

    return pl.kernel(
        body, mesh=_sc_mesh(),
        out_type=jax.ShapeDtypeStruct((n_idx, width), table.dtype),
        scratch_types=[pltpu.VMEM((per_worker,), I32),
                       pltpu.VMEM((2, SC_GATHER_ROWS, width), table.dtype),
                       pltpu.SemaphoreType.DMA((2,)),
                       pltpu.SemaphoreType.DMA((2,))],
        name="sc_gather_rows",
    )(table, idx)


def _combine_kernel(w_ref, h_ref, g2_ref, nf_ref, yk_ref, y_ref):
    t = h_ref.shape[0]
    w = w_ref[...]
    acc_lo = jnp.zeros((t, D_MODEL // 2), F32)
    acc_hi = jnp.zeros((t, D_MODEL // 2), F32)
    for kk in range(TOP_K):
        lo, hi = _unpack_bf16_pair(yk_ref[kk])
        wk = w[:, kk:kk + 1]
        acc_lo = acc_lo + wk * lo
        acc_hi = acc_hi + wk * hi
    out = h_ref[...] + g2_ref[0] * jnp.concatenate([acc_lo, acc_hi], axis=1)
    y_ref[...] = out * lax.rsqrt(jnp.mean(out * out, axis=-1, keepdims=True) + EPS) * nf_ref[...]


def _combine(w, h2, g2, normf, y_by_k, row_offset, tiles_per_batch):
    n = h2.shape[0]
    t = MOVE_TILE
    off = row_offset // t
    mod_rows = g2.shape[1]
    mod_tiles = max(ROW_TILE // t, 1) * tiles_per_batch if mod_rows == 1 else n // t
    return pl.pallas_call(
        _combine_kernel,
        grid=(n // t,),
        in_specs=[pl.BlockSpec((t, LANES), lambda i: (i + off, 0)),
                  pl.BlockSpec((t, D_MODEL), lambda i: (i, 0)),
                  pl.BlockSpec((1, mod_rows if mod_rows == 1 else t, D_MODEL),
                               (lambda i: (i // mod_tiles, 0, 0)) if mod_rows == 1
                               else (lambda i: (0, i, 0))),
                  pl.BlockSpec((1, D_MODEL), lambda i: (0, 0)),
                  pl.BlockSpec((TOP_K, t, D_MODEL // 2), lambda i: (0, i + off, 0))],
        out_specs=pl.BlockSpec((t, D_MODEL), lambda i: (i, 0)),
        out_shape=jax.ShapeDtypeStruct((n, D_MODEL), F32),
        compiler_params=_cparams(("arbitrary",), 40),
        name="combine",
    )(w, h2, g2, normf, y_by_k)


def _rotary_tables(pos):
    half = KEY_DIM_R // 2
    inv_freq = ROPE_BASE ** (-jnp.arange(half, dtype=F32) / half)
    ang = pos[:, None] * inv_freq[None, :]
    cos = jnp.cos(ang)
    sin = jnp.sin(ang)
    cos_t = jnp.concatenate([cos, cos], axis=1)
    sin_t = jnp.concatenate([-sin, sin], axis=1)
    return cos_t.astype(F32), sin_t.astype(F32)


def _rel_bias_table(rel_bias, n_rows, n_cols, q_offset):
    heads = rel_bias.shape[0]
    n_diag = n_rows + n_cols - 1
    dist = q_offset + (n_rows - 1) - np.arange(n_diag)
    idx = np.clip(dist, -REL_CLIP, REL_CLIP) + REL_CLIP
    n_hi = int(np.sum(dist > REL_CLIP))
    n_lo = int(np.sum(dist < -REL_CLIP))
    mid = rel_bias[:, int(idx[n_diag - n_lo - 1]):int(idx[n_hi]) + 1][:, ::-1]
    diag = jnp.concatenate([jnp.broadcast_to(rel_bias[:, 2 * REL_CLIP:], (heads, n_hi)), mid,
                            jnp.broadcast_to(rel_bias[:, :1], (heads, n_lo))], axis=1)
    period = n_diag + 1
    v = jnp.roll(jnp.pad(diag, ((0, 0), (0, 1))), -(n_rows - 1), axis=1)
    skew = jnp.tile(v, (1, n_rows))[:, :n_rows * (period - 1)].reshape(heads, n_rows, period - 1)
    return skew[:, :, :n_cols].astype(F32)


def _prompt_bias(rel_bias):
    n_cols = ATT_QB + ATT_WINDOW
    r = np.arange(ATT_QB)[:, None]
    c = np.arange(n_cols)[None, :]
    band = c - (r // CHUNK) * CHUNK
    valid = (band >= 0) & (band < ATT_WINDOW + CHUNK)
    table = _rel_bias_table(rel_bias, ATT_QB, n_cols, ATT_WINDOW)
    return jnp.where(jnp.asarray(valid)[None], table, NEG_BIG)


def _sample_bias(rel_bias, t_new, cache_len):
    b = _rel_bias_table(rel_bias, t_new, cache_len + t_new, cache_len)
    return b[:, :, :cache_len], b[:, :, cache_len:]


def _mod_parts(mod, rows_each):
    parts = jnp.split(mod, 6, axis=-1)
    if rows_each == 1:
        return [p[:, None, :] for p in parts]
    return [jnp.repeat(p, rows_each, axis=0)[None] for p in parts]


def kernel(x_prompt, x_sample, cache_attn_k, cache_attn_v, state_ret, c_prompt, c_sample,
           norm1_g, norm2_g, w_ada, b_ada, w_in, rel_bias, w_o_attn, w_o_ret, w_out,
           w_router, b_router, w_exp_gate, w_exp_up, w_exp_down, w_sh_gate, w_sh_up, w_sh_down,
           normf_g):
    batch, seq, d = x_prompt.shape
    dec_batch, dec_seq, _ = x_sample.shape
    depth = w_in.shape[0]
    assert depth == 1 and d == D_MODEL
    assert seq % ROW_TILE == 0 and dec_batch * dec_seq == ROW_TILE and ROW_TILE == ATT_WINDOW
    cache_len = cache_attn_k.shape[2]
    n_p = batch * seq
    n_s = dec_batch * dec_seq
    tpb = seq // ROW_TILE
    l = 0

    bf = lambda a: a.astype(BF16)
    c_all = jnp.concatenate([c_prompt, c_sample], axis=0)
    pad = (-c_all.shape[0]) % 8
    c_all = jnp.pad(c_all, ((0, pad), (0, 0)))
    mod = _ada(c_all, bf(w_ada[l]), b_ada[l][None, :])
    mod_p = _mod_parts(mod[:batch], 1)
    mod_s = _mod_parts(mod[batch:batch + dec_batch], dec_seq)

    w_in_bf = bf(w_in[l])
    n1g = norm1_g[l][None, :]
    n2g = norm2_g[l][None, :]
    dense_w = [bf(w_o_attn[l]), bf(w_o_ret[l]), bf(w_out[l]), bf(w_router[l]).T,
               bf(w_sh_gate[l]), bf(w_sh_up[l]), bf(w_sh_down[l])]

    xp = x_prompt.reshape(n_p, d)
    xs_ = x_sample.reshape(n_s, d)
    cos_p, sin_p = _rotary_tables(jnp.arange(seq, dtype=F32))
    pos_s = PAST_LEN + jnp.arange(dec_seq, dtype=F32)
    cos_s, sin_s = _rotary_tables(jnp.tile(pos_s, dec_batch))

    (qa, ka, va, qr, kr, vr, gr, ga, gb, kv_p) = _inproj(
        xp, mod_p[1], mod_p[0], n1g, cos_p, sin_p, w_in_bf, tpb)
    oa = _attn_prompt(qa, ka, va, _prompt_bias(rel_bias[l]), batch, seq)
    zero_state = jnp.zeros((batch, N_HEADS, KEY_DIM_R, VAL_DIM_R), F32)
    yr_in, state_p = _retention(qr, kr, vr, gr, zero_state, batch, seq, RET_CHUNK)
    h_p, n2p_p, s_p = _outproj(xp, oa, yr_in, ga, gb, mod_p[2], mod_p[4], mod_p[3], mod_p[5], n2g,
                               dense_w, tpb, n_p + n_s, 0)

    (qa_s, ka_s, va_s, qr_s, kr_s, vr_s, gr_s, ga_s, gb_s, kv_s) = _inproj(
        xs_, mod_s[1], mod_s[0], n1g, cos_s, sin_s, w_in_bf, 1)
    bias_c, bias_n = _sample_bias(rel_bias[l], dec_seq, cache_len)
    to_keys_minor = lambda c: jnp.transpose(c, (0, 1, 3, 4, 2))
    oa_s = _attn_sample(qa_s, ka_s, va_s, to_keys_minor(cache_attn_k), to_keys_minor(cache_attn_v),
                        bias_c, bias_n, dec_batch, dec_seq, cache_len)
    yr_in_s, state_s = _retention(qr_s, kr_s, vr_s, gr_s, state_ret[l], dec_batch, dec_seq, dec_seq)
    h_s, n2p, scores_t = _outproj(xs_, oa_s, yr_in_s, ga_s, gb_s, mod_s[2], mod_s[4], mod_s[3],
                                  mod_s[5], n2g, dense_w, 1, n_p + n_s, n_p, carried=(n2p_p, s_p))

    lanes_of = lambda v: jnp.broadcast_to(v[:, None], (N_EXPERTS, ROUTE_TILE))
    idx_t, w_t, rank_t, counts = _route(scores_t, lanes_of(b_router[l]))
    (tile_expert, n_tiles, expert_slot, expert_next, tile_halves, pstart, pad_rows,
     n_sorted_rows) = _ffn_plan(counts[:, 0], (n_p + n_s) * TOP_K)
    dest_t = _dest(idx_t, rank_t, lanes_of(pstart.astype(F32)))
    dest_kmajor = dest_t.reshape(-1)
    w_route = jnp.pad(w_t.T, ((0, 0), (0, LANES - TOP_K)))
    xs_sorted = _sc_dispatch(n2p, dest_kmajor, pad_rows, n_sorted_rows)
    ys_sorted = _ffn(tile_expert, n_tiles, expert_slot, expert_next, tile_halves, xs_sorted,
                     w_exp_gate[l], w_exp_up[l], w_exp_down[l])
    nf = normf_g[None, :]
    y_by_k = _sc_gather_rows(ys_sorted, dest_kmajor).reshape(TOP_K, n_p + n_s, d // 2)
    y_p = _combine(w_route, h_p, mod_p[5], nf, y_by_k, 0, tpb)
    y_s = _combine(w_route, h_s, mod_s[5], nf, y_by_k, n_p, 1)

    keep = min(ATT_WINDOW, seq)
    kv_p = kv_p.reshape(batch, ROW_TILE, 2, N_HEADS, HEAD_DIM_A)[:, ROW_TILE - keep:]
    kv_s = kv_s.reshape(dec_batch, dec_seq, 2, N_HEADS, HEAD_DIM_A)
    return (y_p.reshape(batch, seq, d), y_s.reshape(dec_batch, dec_seq, d),
            kv_p[:, :, 0][None], kv_p[:, :, 1][None], state_p[None],
            kv_s[:, :, 0][None], kv_s[:, :, 1][None], state_s[None])
```

```python
import functools

import numpy as np
import jax
import jax.numpy as jnp
from jax import lax
from jax.experimental import pallas as pl
from jax.experimental.pallas import tpu as pltpu
from jax.experimental.pallas import tpu_sc as plsc

F32 = jnp.float32
BF16 = jnp.bfloat16
I32 = jnp.int32
U32 = jnp.uint32

D_MODEL = 1024
PAST_LEN = 4096
CHUNK = 64
N_LEFT_CHUNKS = 8
ATT_WINDOW = N_LEFT_CHUNKS * CHUNK
N_HEADS = 8
HEAD_DIM_A = 64
D_ATT = N_HEADS * HEAD_DIM_A
REL_CLIP = 128
KEY_DIM_R = 64
VAL_DIM_R = 128
D_RET_K = N_HEADS * KEY_DIM_R
D_RET_V = N_HEADS * VAL_DIM_R
ROPE_BASE = 10000.0
N_EXPERTS = 256
TOP_K = 8
N_GROUPS = 8
GROUP_SIZE = N_EXPERTS // N_GROUPS
TOPK_GROUPS = 4
D_EXPERT = 256
ROUTED_SCALE = 2.5
EPS = 1e-6
IN_WIDTHS = (D_ATT, D_ATT, D_ATT, D_RET_K, D_RET_K, D_RET_V, D_RET_V, D_MODEL, D_MODEL)
IN_OFFS = tuple(int(v) for v in np.cumsum((0,) + IN_WIDTHS))
D_IN = IN_OFFS[-1]

NEG_BIG = -1e30
LANES = 128
V7X_VMEM_BYTES = 64 * 1024 * 1024
V7X_SC_CORES = 2
V7X_SC_SUBCORES = 16
SC_GATHER_ROWS = 96
SPARE_ROWS = 8192

ROW_TILE = 512
ATT_QB = 256
RET_CHUNK = 256
ROUTE_TILE = 512
MOVE_TILE = 256
FFN_QUANTUM = 128
FFN_MATMUL_ROWS = 5 * FFN_QUANTUM
FFN_TILE = 5 * FFN_QUANTUM
FFN_WEIGHT_SLOTS = 4


def _cparams(semantics, vmem_mb):
    return pltpu.CompilerParams(dimension_semantics=semantics,
                                vmem_limit_bytes=min(vmem_mb * 1024 * 1024, V7X_VMEM_BYTES - (6 << 20)))


def _silu(x):
    return x * jax.nn.sigmoid(x)


def _pack_bf16_pair(lo, hi):
    lo_b = pltpu.bitcast(lo.astype(BF16).astype(F32), U32) >> 16
    hi_b = pltpu.bitcast(hi.astype(BF16).astype(F32), U32) & jnp.uint32(0xFFFF0000)
    return lo_b | hi_b


def _unpack_bf16_pair(u):
    lo = pltpu.bitcast(u << 16, F32)
    hi = pltpu.bitcast(u & jnp.uint32(0xFFFF0000), F32)
    return lo, hi


def _ada_kernel(c_ref, w_ref, b_ref, o_ref):
    sc = _silu(c_ref[...]).astype(BF16)
    o_ref[...] = jnp.dot(sc, w_ref[...], preferred_element_type=F32) + b_ref[...]


def _ada(c_all, w_ada_bf, b_ada):
    rows = c_all.shape[0]
    n_out = w_ada_bf.shape[1]
    blk = D_MODEL
    return pl.pallas_call(
        _ada_kernel,
        grid=(n_out // blk,),
        in_specs=[pl.BlockSpec((rows, D_MODEL), lambda j: (0, 0)),
                  pl.BlockSpec((D_MODEL, blk), lambda j: (0, j)),
                  pl.BlockSpec((1, blk), lambda j: (0, j))],
        out_specs=pl.BlockSpec((rows, blk), lambda j: (0, j)),
        out_shape=jax.ShapeDtypeStruct((rows, n_out), F32),
        compiler_params=_cparams(("arbitrary",), 24),
        name="ada",
    )(c_all, w_ada_bf, b_ada)


def _inproj_kernel(x_ref, sc_ref, sh_ref, g_ref, cos_ref, sin_ref, w_ref,
                   qa_ref, ka_ref, va_ref, qr_ref, kr_ref, vr_ref, gr_ref, ga_ref, gb_ref,
                   kv_ref, *, tiles_per_batch):
    x = x_ref[...]
    xn = x * lax.rsqrt(jnp.mean(x * x, axis=-1, keepdims=True) + EPS) * g_ref[...]
    nb = (xn * (1.0 + sc_ref[0]) + sh_ref[0]).astype(BF16)

    def proj(seg):
        return jnp.dot(nb, w_ref[:, IN_OFFS[seg]:IN_OFFS[seg + 1]], preferred_element_type=F32)

    qa_ref[...] = proj(0).astype(BF16)
    ka = proj(1)
    va = proj(2)
    ka_ref[...] = ka.astype(BF16)
    va_ref[...] = va.astype(BF16)

    @pl.when(pl.program_id(0) % tiles_per_batch == tiles_per_batch - 1)
    def _():
        kv_ref[:, :D_ATT] = ka
        kv_ref[:, D_ATT:] = va

    cos = jnp.tile(cos_ref[...], (1, N_HEADS))
    sin = jnp.tile(sin_ref[...], (1, N_HEADS))
    first_half = (lax.broadcasted_iota(I32, (1, D_RET_K), 1) % KEY_DIM_R) < (KEY_DIM_R // 2)

    def rotary(t):
        partner = jnp.where(first_half, pltpu.roll(t, D_RET_K - KEY_DIM_R // 2, 1),
                            pltpu.roll(t, KEY_DIM_R // 2, 1))
        return t * cos + partner * sin

    qr_ref[...] = rotary(proj(3)).astype(BF16)
    kr_ref[...] = (rotary(proj(4)) * (KEY_DIM_R ** -0.5)).astype(BF16)
    vr_ref[...] = proj(5).astype(BF16)
    gr_ref[...] = proj(6).astype(BF16)
    ga_ref[...] = proj(7).astype(BF16)
    gb_ref[...] = proj(8).astype(BF16)


def _inproj(x2d, sc, sh, g, cos_t, sin_t, w_in_bf, tiles_per_batch):
    n = x2d.shape[0]
    tm = ROW_TILE
    n_tiles = n // tm
    n_batches = n_tiles // tiles_per_batch
    mod_rows = sc.shape[1]
    pos_tiles = cos_t.shape[0] // tm

    def row_spec(width):
        return pl.BlockSpec((tm, width), lambda i: (i, 0))

    mod_spec = pl.BlockSpec((1, mod_rows, D_MODEL), lambda i: (i // tiles_per_batch, 0, 0))
    pos_spec = pl.BlockSpec((tm, KEY_DIM_R), lambda i: (i % pos_tiles, 0))
    out_widths = (D_ATT, D_ATT, D_ATT, D_RET_K, D_RET_K, D_RET_V, D_RET_V, D_MODEL, D_MODEL)
    out_shape = [jax.ShapeDtypeStruct((n, w), BF16) for w in out_widths]
    out_shape.append(jax.ShapeDtypeStruct((n_batches * tm, 2 * D_ATT), F32))
    out_specs = [row_spec(w) for w in out_widths]
    out_specs.append(pl.BlockSpec((tm, 2 * D_ATT), lambda i: (i // tiles_per_batch, 0)))
    return pl.pallas_call(
        functools.partial(_inproj_kernel, tiles_per_batch=tiles_per_batch),
        grid=(n_tiles,),
        in_specs=[row_spec(D_MODEL), mod_spec, mod_spec,
                  pl.BlockSpec((1, D_MODEL), lambda i: (0, 0)),
                  pos_spec, pos_spec,
                  pl.BlockSpec((D_MODEL, D_IN), lambda i: (0, 0))],
        out_specs=out_specs,
        out_shape=out_shape,
        compiler_params=_cparams(("arbitrary",), 56),
        name="inproj",
    )(x2d, sc, sh, g, cos_t, sin_t, w_in_bf)


def _softmax_pv(s, v_parts):
    m = functools.reduce(jnp.maximum, [jnp.max(t, axis=-1, keepdims=True) for t in s])
    ps = [jnp.exp(t - m) for t in s]
    l = functools.reduce(jnp.add, [jnp.sum(p, axis=-1, keepdims=True) for p in ps])
    o = functools.reduce(jnp.add, [jnp.dot(p.astype(BF16), v, preferred_element_type=F32)
                                   for p, v in zip(ps, v_parts)])
    return o / l


def _attn_prompt_kernel(q_ref, k0_ref, k1_ref, k2_ref, v0_ref, v1_ref, v2_ref, bias_ref, o_ref):
    j = pl.program_id(1)
    q = q_ref[...]
    k = jnp.concatenate([k0_ref[...], k1_ref[...], k2_ref[...]], axis=0)
    v = jnp.concatenate([v0_ref[...], v1_ref[...], v2_ref[...]], axis=0)
    n_keys = k.shape[0]
    key_block = lax.broadcasted_iota(I32, (1, n_keys), 1) // ATT_QB
    before_start = jnp.where(key_block < 2 - j, NEG_BIG, 0.0)
    outs = []
    for h in range(N_HEADS):
        sl = slice(h * HEAD_DIM_A, (h + 1) * HEAD_DIM_A)
        qh = (q[:, sl].astype(F32) * (HEAD_DIM_A ** -0.5)).astype(BF16)
        s = lax.dot_general(qh, k[:, sl], (((1,), (1,)), ((), ())), preferred_element_type=F32)
        s = s + bias_ref[h] + before_start
        outs.append(_softmax_pv([s], [v[:, sl]]))
    o_ref[...] = jnp.concatenate(outs, axis=1).astype(BF16)


def _attn_prompt(q, k, v, bias_full, batch, seq):
    qb = ATT_QB
    nq = seq // qb

    def q_map(b, j):
        return (b * nq + j, 0)

    def kv_map(back):
        return lambda b, j: (b * nq + jnp.maximum(j - back, 0), 0)

    blk = lambda m: pl.BlockSpec((qb, D_ATT), m)
    return pl.pallas_call(
        _attn_prompt_kernel,
        grid=(batch, nq),
        in_specs=[blk(q_map), blk(kv_map(2)), blk(kv_map(1)), blk(kv_map(0)),
                  blk(kv_map(2)), blk(kv_map(1)), blk(kv_map(0)),
                  pl.BlockSpec(bias_full.shape, lambda b, j: (0, 0, 0))],
        out_specs=blk(q_map),
        out_shape=jax.ShapeDtypeStruct((batch * seq, D_ATT), BF16),
        compiler_params=_cparams(("parallel", "arbitrary"), 40),
        name="attn_prompt",
    )(q, k, k, k, v, v, v, bias_full)


SAMPLE_ATT_BATCHES = 2


def _attn_sample_kernel(q_ref, kn_ref, vn_ref, ck_ref, cv_ref, bc_ref, bn_ref, o_ref, *, t_new):
    nt = (((1,), (1,)), ((), ()))
    for b in range(SAMPLE_ATT_BATCHES):
        rows = slice(b * t_new, (b + 1) * t_new)
        q = q_ref[rows, :]
        kn = kn_ref[rows, :]
        vn = vn_ref[rows, :]
        outs = []
        for h in range(N_HEADS):
            sl = slice(h * HEAD_DIM_A, (h + 1) * HEAD_DIM_A)
            qh = (q[:, sl].astype(F32) * (HEAD_DIM_A ** -0.5)).astype(BF16)
            kc_t = ck_ref[b, h].astype(BF16)
            vc_t = cv_ref[b, h].astype(BF16)
            s_c = jnp.dot(qh, kc_t, preferred_element_type=F32) + bc_ref[h]
            s_n = lax.dot_general(qh, kn[:, sl], nt, preferred_element_type=F32) + bn_ref[h]
            m = jnp.maximum(jnp.max(s_c, axis=-1, keepdims=True), jnp.max(s_n, axis=-1, keepdims=True))
            p_c = jnp.exp(s_c - m)
            p_n = jnp.exp(s_n - m)
            l = jnp.sum(p_c, axis=-1, keepdims=True) + jnp.sum(p_n, axis=-1, keepdims=True)
            o = (lax.dot_general(p_c.astype(BF16), vc_t, nt, preferred_element_type=F32)
                 + jnp.dot(p_n.astype(BF16), vn[:, sl], preferred_element_type=F32))
            outs.append(o / l)
        o_ref[rows, :] = jnp.concatenate(outs, axis=1).astype(BF16)


def _attn_sample(q, k, v, cache_k_t, cache_v_t, bias_cache, bias_new, batch, t_new, cache_len):
    nb = SAMPLE_ATT_BATCHES
    blk = pl.BlockSpec((nb * t_new, D_ATT), lambda b: (b, 0))
    cblk = pl.BlockSpec((None, nb, N_HEADS, HEAD_DIM_A, cache_len), lambda b: (0, b, 0, 0, 0))
    return pl.pallas_call(
        functools.partial(_attn_sample_kernel, t_new=t_new),
        grid=(batch // nb,),
        in_specs=[blk, blk, blk, cblk, cblk,
                  pl.BlockSpec(bias_cache.shape, lambda b: (0, 0, 0)),
                  pl.BlockSpec(bias_new.shape, lambda b: (0, 0, 0))],
        out_specs=blk,
        out_shape=jax.ShapeDtypeStruct((batch * t_new, D_ATT), BF16),
        compiler_params=_cparams(("arbitrary",), 40),
        name="attn_sample",
    )(q, k, v, cache_k_t, cache_v_t, bias_cache, bias_new)


def _ret_kernel(q_ref, k_ref, v_ref, g_ref, s0_ref, dmask_ref, qdec_ref, kdec_ref, sdec_ref,
                y_ref, sout_ref, state_ref):
    c = pl.program_id(1)

    @pl.when(c == 0)
    def _():
        state_ref[...] = s0_ref[0]

    q = q_ref[...]
    k = k_ref[...]
    v = v_ref[...]
    g = g_ref[...]
    outs = []
    for h in range(N_HEADS):
        ks = slice(h * KEY_DIM_R, (h + 1) * KEY_DIM_R)
        vs = slice(h * VAL_DIM_R, (h + 1) * VAL_DIM_R)
        qh, kh, vh = q[:, ks], k[:, ks], v[:, vs]
        scores = lax.dot_general(qh, kh, (((1,), (1,)), ((), ())), preferred_element_type=F32)
        inner = jnp.dot((scores * dmask_ref[h]).astype(BF16), vh, preferred_element_type=F32)
        state = state_ref[h]
        cross = jnp.dot(qh, state.astype(BF16), preferred_element_type=F32) * qdec_ref[h]
        o = inner + cross
        v_dec = (vh.astype(F32) * kdec_ref[h]).astype(BF16)
        state_ref[h] = sdec_ref[h] * state + lax.dot_general(
            kh, v_dec, (((0,), (0,)), ((), ())), preferred_element_type=F32)
        on = o * lax.rsqrt(jnp.mean(o * o, axis=-1, keepdims=True) + EPS)
        outs.append(on * _silu(g[:, vs].astype(F32)))
    y_ref[...] = jnp.concatenate(outs, axis=1).astype(BF16)

    @pl.when(c == pl.num_programs(1) - 1)
    def _():
        sout_ref[0] = state_ref[...]


def _ret_tables(chunk):
    log_g = jnp.log(1.0 - jnp.exp2(-5.0 - jnp.arange(N_HEADS, dtype=F32)))
    i = jnp.arange(chunk, dtype=F32)
    diff = i[:, None] - i[None, :]
    dmask = jnp.where(diff >= 0, jnp.exp(log_g[:, None, None] * jnp.maximum(diff, 0.0)), 0.0)
    qdec = jnp.exp(log_g[:, None] * (i + 1.0))
    kdec = jnp.exp(log_g[:, None] * (chunk - 1.0 - i))
    sdec = jnp.exp(log_g * chunk)
    bc = lambda t: jnp.broadcast_to(t[:, :, None], (N_HEADS, t.shape[1], VAL_DIM_R)).astype(F32)
    sdec_t = jnp.broadcast_to(sdec[:, None, None], (N_HEADS, 1, VAL_DIM_R)).astype(F32)
    return dmask.astype(F32), bc(qdec), bc(kdec), sdec_t


def _retention(q, k, v, gate, state0, batch, seq, chunk):
    nc = seq // chunk
    dmask, qdec, kdec, sdec = _ret_tables(chunk)
    row = lambda w: pl.BlockSpec((chunk, w), lambda b, c: (b * nc + c, 0))
    const = lambda a: pl.BlockSpec(a.shape, lambda b, c: (0,) * a.ndim)
    st_spec = pl.BlockSpec((1, N_HEADS, KEY_DIM_R, VAL_DIM_R), lambda b, c: (b, 0, 0, 0))
    return pl.pallas_call(
        _ret_kernel,
        grid=(batch, nc),
        in_specs=[row(D_RET_K), row(D_RET_K), row(D_RET_V), row(D_RET_V), st_spec,
                  const(dmask), const(qdec), const(kdec), const(sdec)],
        out_specs=[row(D_RET_V), st_spec],
        out_shape=[jax.ShapeDtypeStruct((batch * seq, D_RET_V), BF16),
                   jax.ShapeDtypeStruct((batch, N_HEADS, KEY_DIM_R, VAL_DIM_R), F32)],
        scratch_shapes=[pltpu.VMEM((N_HEADS, KEY_DIM_R, VAL_DIM_R), F32)],
        compiler_params=_cparams(("parallel", "arbitrary"), 32),
        name="retention",
    )(q, k, v, gate, state0, dmask, qdec, kdec, sdec)


def _outproj_kernel(x_ref, oa_ref, yr_ref, ga_ref, gb_ref, g1_ref, sc2_ref, sh2_ref, g2_ref, n2g_ref,
                    woa_ref, wor_ref, wout_ref, wrt_ref, wsg_ref, wsu_ref, wsd_ref, *rest):
    h_ref, n2p_ref, s_ref = rest[-3:]
    ya = jnp.dot(oa_ref[...], woa_ref[...], preferred_element_type=F32)
    yr = jnp.dot(yr_ref[...], wor_ref[...], preferred_element_type=F32)
    merged = (jax.nn.sigmoid(ga_ref[...].astype(F32)) * ya
              + jax.nn.sigmoid(gb_ref[...].astype(F32)) * yr)
    mix = jnp.dot(merged.astype(BF16), wout_ref[...], preferred_element_type=F32)
    h = x_ref[...] + g1_ref[0] * mix
    hn = h * lax.rsqrt(jnp.mean(h * h, axis=-1, keepdims=True) + EPS) * n2g_ref[...]
    n2 = hn * (1.0 + sc2_ref[0]) + sh2_ref[0]
    n2b = n2.astype(BF16)
    s_ref[...] = jax.nn.sigmoid(lax.dot_general(wrt_ref[...], n2b, (((1,), (1,)), ((), ())),
                                                preferred_element_type=F32))
    hid = _silu(jnp.dot(n2b, wsg_ref[...], preferred_element_type=F32)) * jnp.dot(
        n2b, wsu_ref[...], preferred_element_type=F32)
    shared = jnp.dot(hid.astype(BF16), wsd_ref[...], preferred_element_type=F32)
    h_ref[...] = h + g2_ref[0] * shared
    half = D_MODEL // 2
    n2p_ref[...] = _pack_bf16_pair(n2[:, :half], n2[:, half:])


def _outproj(x2d, oa, yr_in, ga, gb, g1, sc2, sh2, g2, n2g, weights, tiles_per_batch,
             all_tokens, token_offset, carried=None):
    n = x2d.shape[0]
    tm = ROW_TILE
    off = token_offset // tm
    mod_rows = g1.shape[1]
    row = lambda w: pl.BlockSpec((tm, w), lambda i: (i, 0))
    mod_spec = pl.BlockSpec((1, mod_rows, D_MODEL), lambda i: (i // tiles_per_batch, 0, 0))
    const = lambda a: pl.BlockSpec(a.shape, lambda i: (0,) * a.ndim)
    in_specs = [row(D_MODEL), row(D_ATT), row(D_RET_V), row(D_MODEL), row(D_MODEL),
                mod_spec, mod_spec, mod_spec, mod_spec, const(n2g)] + [const(w) for w in weights]
    args = [x2d, oa, yr_in, ga, gb, g1, sc2, sh2, g2, n2g, *weights]
    aliases = {}
    if carried is not None:
        aliases = {len(args): 1, len(args) + 1: 2}
        in_specs += [pl.BlockSpec(memory_space=pl.ANY)] * 2
        args += list(carried)
    return pl.pallas_call(
        _outproj_kernel,
        grid=(n // tm,),
        in_specs=in_specs,
        out_specs=[row(D_MODEL),
                   pl.BlockSpec((tm, D_MODEL // 2), lambda i: (i + off, 0)),
                   pl.BlockSpec((N_EXPERTS, tm), lambda i: (0, i + off))],
        out_shape=[jax.ShapeDtypeStruct((n, D_MODEL), F32),
                   jax.ShapeDtypeStruct((all_tokens, D_MODEL // 2), U32),
                   jax.ShapeDtypeStruct((N_EXPERTS, all_tokens), F32)],
        input_output_aliases=aliases,
        compiler_params=_cparams(("arbitrary",), 48),
        name="outproj",
    )(*args)


def _route_kernel(s_ref, b_ref, idx_ref, w_ref, rank_ref, cnt_ref, run_ref, tri_ref):
    step = pl.program_id(0)
    t = s_ref.shape[1]

    @pl.when(step == 0)
    def _():
        run_ref[...] = jnp.zeros_like(run_ref)
        r = lax.broadcasted_iota(I32, (t, t), 0)
        c = lax.broadcasted_iota(I32, (t, t), 1)
        tri_ref[...] = jnp.where(r < c, 1.0, 0.0).astype(BF16)

    s = s_ref[...]
    sel = s + b_ref[...]
    row_f = lax.broadcasted_iota(I32, (N_EXPERTS, t), 0).astype(F32)

    def first_argmax(vals, rows):
        m = jnp.max(vals, axis=0, keepdims=True)
        pos = jnp.min(jnp.where(vals == m, rows, float(N_EXPERTS)), axis=0, keepdims=True)
        return m, pos

    gscore = []
    group_row = lax.broadcasted_iota(I32, (GROUP_SIZE, t), 0).astype(F32)
    for g in range(N_GROUPS):
        rows = slice(g * GROUP_SIZE, (g + 1) * GROUP_SIZE)
        m1, p1 = first_argmax(sel[rows], group_row)
        m2 = jnp.max(jnp.where(group_row == p1, -jnp.inf, sel[rows]), axis=0, keepdims=True)
        gscore.append(m1 + m2)
    cand_parts = []
    for g in range(N_GROUPS):
        rows = slice(g * GROUP_SIZE, (g + 1) * GROUP_SIZE)
        beaten_by = jnp.zeros((1, t), F32)
        for o in range(N_GROUPS):
            if o == g:
                continue
            wins = (gscore[o] > gscore[g]) if o > g else (gscore[o] >= gscore[g])
            beaten_by = beaten_by + jnp.where(wins, 1.0, 0.0)
        cand_parts.append(jnp.where(beaten_by < TOPK_GROUPS, sel[rows], -jnp.inf))
    cand = jnp.concatenate(cand_parts, axis=0)

    picked = jnp.zeros((N_EXPERTS, t), F32)
    idx_rows, w_rows = [], []
    for _ in range(TOP_K):
        _, pos = first_argmax(cand, row_f)
        hit = row_f == pos
        w_rows.append(jnp.sum(jnp.where(hit, s, 0.0), axis=0, keepdims=True))
        idx_rows.append(pos)
        picked = jnp.where(hit, 1.0, picked)
        cand = jnp.where(hit, -jnp.inf, cand)
    w_sum = functools.reduce(jnp.add, w_rows)

    before = jnp.dot(picked.astype(BF16), tri_ref[...], preferred_element_type=F32) + run_ref[...]
    run_ref[...] = run_ref[...] + jnp.sum(picked, axis=1, keepdims=True)
    rank_rows = [jnp.sum(jnp.where(row_f == idx_rows[kk], before, 0.0), axis=0, keepdims=True)
                 for kk in range(TOP_K)]

    idx_ref[...] = jnp.concatenate(idx_rows, axis=0).astype(I32)
    w_ref[...] = jnp.concatenate([w / w_sum * ROUTED_SCALE for w in w_rows], axis=0)
    rank_ref[...] = jnp.concatenate(rank_rows, axis=0).astype(I32)

    @pl.when(step == pl.num_programs(0) - 1)
    def _():
        cnt_ref[...] = run_ref[...].astype(I32)


def _route(scores_t, b_col):
    n = scores_t.shape[1]
    t = ROUTE_TILE
    col = pl.BlockSpec((TOP_K, t), lambda i: (0, i))
    const = pl.BlockSpec((N_EXPERTS, t), lambda i: (0, 0))
    return pl.pallas_call(
        _route_kernel,
        grid=(n // t,),
        in_specs=[pl.BlockSpec((N_EXPERTS, t), lambda i: (0, i)), const],
        out_specs=[col, col, col, const],
        out_shape=[jax.ShapeDtypeStruct((TOP_K, n), I32),
                   jax.ShapeDtypeStruct((TOP_K, n), F32),
                   jax.ShapeDtypeStruct((TOP_K, n), I32),
                   jax.ShapeDtypeStruct((N_EXPERTS, t), I32)],
        scratch_shapes=[pltpu.VMEM((N_EXPERTS, t), F32), pltpu.VMEM((t, t), BF16)],
        compiler_params=_cparams(("arbitrary",), 32),
        name="route",
    )(scores_t, b_col)


def _dest_kernel(idx_ref, rank_ref, start_ref, dest_ref):
    t = idx_ref.shape[1]
    row = lax.broadcasted_iota(I32, (N_EXPERTS, t), 0)
    starts = start_ref[...]
    base = [jnp.sum(jnp.where(row == idx_ref[kk:kk + 1, :], starts, 0.0), axis=0, keepdims=True)
            for kk in range(TOP_K)]
    dest_ref[...] = jnp.concatenate(base, axis=0).astype(I32) + rank_ref[...]


def _dest(idx_t, rank_t, starts_col):
    n = idx_t.shape[1]
    t = ROUTE_TILE
    col = pl.BlockSpec((TOP_K, t), lambda i: (0, i))
    return pl.pallas_call(
        _dest_kernel,
        grid=(n // t,),
        in_specs=[col, col, pl.BlockSpec((N_EXPERTS, t), lambda i: (0, 0))],
        out_specs=col,
        out_shape=jax.ShapeDtypeStruct((TOP_K, n), I32),
        compiler_params=_cparams(("arbitrary",), 32),
        name="dest",
    )(idx_t, rank_t, starts_col)


def _sc_mesh():
    return plsc.VectorSubcoreMesh(core_axis_name="core", subcore_axis_name="subcore",
                                  num_cores=V7X_SC_CORES, num_subcores=V7X_SC_SUBCORES)


def _sc_dispatch(n2p, dest_kmajor, pad_rows, n_out_rows):
    n, width = n2p.shape
    rows = SC_GATHER_ROWS
    n_workers = V7X_SC_CORES * V7X_SC_SUBCORES
    src_chunks = n // rows
    items = dest_kmajor.shape[0] // rows
    per_worker = items // n_workers
    pad_per_worker = pad_rows.shape[0] // rows // n_workers
    assert src_chunks * rows == n and per_worker * n_workers == items and per_worker % 2 == 0
    assert pad_per_worker * n_workers * rows == pad_rows.shape[0]
    idx3 = dest_kmajor.reshape(n_workers, per_worker, rows)
    pad3 = pad_rows.reshape(n_workers, pad_per_worker, rows)
    zeros = jnp.zeros((rows, width), n2p.dtype)

    def body(src_hbm, idx_hbm, pad_hbm, zero_hbm, out_hbm, idx_v, pad_v, rows_v, load_sem, scat_sem):
        worker = lax.axis_index("subcore") * V7X_SC_CORES + lax.axis_index("core")
        pltpu.sync_copy(idx_hbm.at[worker], idx_v)
        pltpu.sync_copy(pad_hbm.at[worker], pad_v)
        pltpu.sync_copy(zero_hbm, rows_v.at[0])

        def zero_fill(c):
            return pltpu.make_async_copy(rows_v.at[0], out_hbm.at[pad_v.at[c]], scat_sem.at[0])

        @pl.loop(0, pad_per_worker)
        def _(c):
            zero_fill(c).start()

        @pl.loop(0, pad_per_worker)
        def _(c):
            zero_fill(c).wait()

        def load(c, b):
            chunk = lax.rem(worker * per_worker + c, src_chunks)
            off = pl.multiple_of(chunk * rows, rows)
            return pltpu.make_async_copy(src_hbm.at[pl.ds(off, rows)], rows_v.at[b], load_sem.at[b])

        def scatter(c, b):
            return pltpu.make_async_copy(rows_v.at[b], out_hbm.at[idx_v.at[c]], scat_sem.at[b])

        load(0, 0).start()

        @pl.loop(0, per_worker, step=2)
        def _(c0):
            for b in range(2):
                c = c0 + b
                load(c, b).wait()

                @pl.when(c >= 1)
                def _():
                    scatter(c - 1, 1 - b).wait()

                @pl.when(c + 1 < per_worker)
                def _():
                    load(c + 1, 1 - b).start()

                scatter(c, b).start()

        scatter(per_worker - 1, (per_worker - 1) % 2).wait()

    return pl.kernel(
        body, mesh=_sc_mesh(),
        out_type=jax.ShapeDtypeStruct((n_out_rows, width), n2p.dtype),
        scratch_types=[pltpu.VMEM((per_worker, rows), I32),
                       pltpu.VMEM((pad_per_worker, rows), I32),
                       pltpu.VMEM((2, rows, width), n2p.dtype),
                       pltpu.SemaphoreType.DMA((2,)),
                       pltpu.SemaphoreType.DMA((2,))],
        name="sc_dispatch",
    )(n2p, idx3, pad3, zeros)


def _ffn_kernel(texp_ref, ntiles_ref, eslot_ref, enext_ref, nhalf_ref, xs_ref, wg_hbm, wu_hbm, wd_hbm,
                ys_ref, wg_buf, wu_buf, wd_buf, wgu_bf, wd_bf, sems):
    g = pl.program_id(0)

    def weight_copies(e, slot):
        return (pltpu.make_async_copy(wg_hbm.at[e], wg_buf.at[slot], sems.at[slot, 0]),
                pltpu.make_async_copy(wu_hbm.at[e], wu_buf.at[slot], sems.at[slot, 1]),
                pltpu.make_async_copy(wd_hbm.at[e], wd_buf.at[slot], sems.at[slot, 2]))

    @pl.when(g < ntiles_ref[0])
    def _():
        e = texp_ref[g]
        changed = jnp.logical_or(g == 0, texp_ref[jnp.maximum(g - 1, 0)] != e)

        @pl.when(changed)
        def _():
            slot = eslot_ref[e]

            def fetch_ahead(first_hop, hops, target_slot):
                ahead = first_hop
                for _ in range(hops - 1):
                    ahead = jnp.where(ahead >= 0, enext_ref[jnp.maximum(ahead, 0)], -1)

                @pl.when(ahead >= 0)
                def _():
                    for c in weight_copies(ahead, target_slot):
                        c.start()

            @pl.when(g == 0)
            def _():
                for c in weight_copies(e, slot):
                    c.start()
                for hops in range(1, FFN_WEIGHT_SLOTS - 1):
                    fetch_ahead(enext_ref[e], hops, lax.rem(slot + hops, FFN_WEIGHT_SLOTS))

            fetch_ahead(enext_ref[e], FFN_WEIGHT_SLOTS - 1,
                        lax.rem(slot + FFN_WEIGHT_SLOTS - 1, FFN_WEIGHT_SLOTS))
            for c in weight_copies(e, slot):
                c.wait()

            wgu_bf[:, :D_EXPERT] = wg_buf[slot].astype(BF16)
            wgu_bf[:, D_EXPERT:] = wu_buf[slot].astype(BF16)
            wd_bf[...] = wd_buf[slot].astype(BF16)

        def expert_rows(rows):
            lo, hi = _unpack_bf16_pair(xs_ref[rows, :])
            x = jnp.concatenate([lo, hi], axis=1).astype(BF16)
            gu = jnp.dot(x, wgu_bf[...], preferred_element_type=F32)
            hid = (_silu(gu[:, :D_EXPERT]) * gu[:, D_EXPERT:]).astype(BF16)
            y = jnp.dot(hid, wd_bf[...], preferred_element_type=F32)
            half = D_MODEL // 2
            ys_ref[rows, :] = _pack_bf16_pair(y[:, :half], y[:, half:])

        for n_groups in range(1, FFN_TILE // FFN_QUANTUM + 1):

            @pl.when(nhalf_ref[g] == n_groups)
            def _(n_groups=n_groups):
                used = n_groups * FFN_QUANTUM
                for start in range(0, used, FFN_MATMUL_ROWS):
                    expert_rows(slice(start, min(start + FFN_MATMUL_ROWS, used)))
                if used < FFN_TILE:
                    ys_ref[used:, :] = jnp.zeros((FFN_TILE - used, ys_ref.shape[1]), U32)


def _ffn(tile_expert, n_tiles, expert_slot, expert_next, tile_halves, xs, w_gate, w_up, w_down):
    rows, width = xs.shape
    m = FFN_TILE
    max_tiles = tile_expert.shape[0]
    row_map = lambda g, te, nt, es, en, nh: (jnp.minimum(g, nt[0] - 1), 0)
    hbm = pl.BlockSpec(memory_space=pl.ANY)
    grid_spec = pltpu.PrefetchScalarGridSpec(
        num_scalar_prefetch=5,
        grid=(max_tiles,),
        in_specs=[pl.BlockSpec((m, width), row_map), hbm, hbm, hbm],
        out_specs=pl.BlockSpec((m, width), row_map),
        scratch_shapes=[pltpu.VMEM((FFN_WEIGHT_SLOTS, D_MODEL, D_EXPERT), F32),
                        pltpu.VMEM((FFN_WEIGHT_SLOTS, D_MODEL, D_EXPERT), F32),
                        pltpu.VMEM((FFN_WEIGHT_SLOTS, D_EXPERT, D_MODEL), F32),
                        pltpu.VMEM((D_MODEL, 2 * D_EXPERT), BF16),
                        pltpu.VMEM((D_EXPERT, D_MODEL), BF16),
                        pltpu.SemaphoreType.DMA((FFN_WEIGHT_SLOTS, 3))],
    )
    return pl.pallas_call(
        _ffn_kernel,
        grid_spec=grid_spec,
        out_shape=jax.ShapeDtypeStruct((rows, width), U32),
        compiler_params=_cparams(("arbitrary",), 32),
        name="ffn",
    )(tile_expert, n_tiles, expert_slot, expert_next, tile_halves, xs, w_gate, w_up, w_down)


def _ffn_plan(counts, n_assign):
    m = FFN_TILE
    max_tiles = n_assign // m + N_EXPERTS
    padded = ((counts + m - 1) // m) * m
    pend = jnp.cumsum(padded).astype(I32)
    pstart = pend - padded
    n_tiles = pend[-1:] // m
    g = jnp.minimum(jnp.arange(max_tiles, dtype=I32), n_tiles - 1)
    tile_expert = jnp.sum((pend[None, :] <= (g * m)[:, None]).astype(I32), axis=1)
    tile_expert = jnp.minimum(tile_expert, N_EXPERTS - 1)
    vend = pstart + ((counts + FFN_QUANTUM - 1) // FFN_QUANTUM) * FFN_QUANTUM
    own = tile_expert[:, None] == jnp.arange(N_EXPERTS, dtype=I32)[None, :]
    tile_vend = jnp.sum(jnp.where(own, vend[None, :], 0), axis=1)
    tile_halves = (jnp.clip(tile_vend - g * m, 0, m) // FFN_QUANTUM).astype(I32)
    used = counts > 0
    expert_slot = ((jnp.cumsum(used.astype(I32)) - 1) % FFN_WEIGHT_SLOTS).astype(I32)
    ids = jnp.where(used, jnp.arange(N_EXPERTS, dtype=I32), N_EXPERTS)
    first_used_from = lax.cummin(ids, axis=0, reverse=True)
    nxt = jnp.concatenate([first_used_from[1:], jnp.full((1,), N_EXPERTS, I32)])
    expert_next = jnp.where(nxt < N_EXPERTS, nxt, -1).astype(I32)
    spare_row = max_tiles * m
    j = jnp.arange(FFN_QUANTUM, dtype=I32)[None, :]
    seg_end = (pstart + counts)[:, None]
    spare = spare_row + (jnp.arange(N_EXPERTS, dtype=I32)[:, None] * FFN_QUANTUM + j) % SPARE_ROWS
    pad_rows = jnp.where(j < (vend[:, None] - seg_end), seg_end + j, spare).astype(I32).reshape(-1)
    unit = SC_GATHER_ROWS * V7X_SC_CORES * V7X_SC_SUBCORES
    extra = (-pad_rows.shape[0]) % unit
    filler = spare_row + (pad_rows.shape[0] + jnp.arange(extra, dtype=I32)) % SPARE_ROWS
    pad_rows = jnp.concatenate([pad_rows, filler])
    return (tile_expert, n_tiles, expert_slot, expert_next, tile_halves, pstart, pad_rows,
            spare_row + SPARE_ROWS)


def _sc_gather_rows(table, idx):
    n_idx = idx.shape[0]
    width = table.shape[1]
    n_workers = V7X_SC_CORES * V7X_SC_SUBCORES
    per_worker = n_idx // n_workers
    n_chunks = per_worker // SC_GATHER_ROWS
    assert per_worker * n_workers == n_idx and n_chunks * SC_GATHER_ROWS == per_worker and n_chunks % 2 == 0

    def body(table_hbm, idx_hbm, out_hbm, idx_v, rows_v, gather_sem, write_sem):
        worker = lax.axis_index("subcore") * V7X_SC_CORES + lax.axis_index("core")
        base = worker * per_worker
        pltpu.sync_copy(idx_hbm.at[pl.ds(base, per_worker)], idx_v)

        def gather(c, b):
            off = pl.multiple_of(c * SC_GATHER_ROWS, SC_GATHER_ROWS)
            return pltpu.make_async_copy(table_hbm.at[idx_v.at[pl.ds(off, SC_GATHER_ROWS)]],
                                         rows_v.at[b], gather_sem.at[b])

        def write(c, b):
            off = pl.multiple_of(c * SC_GATHER_ROWS, SC_GATHER_ROWS)
            return pltpu.make_async_copy(rows_v.at[b], out_hbm.at[pl.ds(base + off, SC_GATHER_ROWS)],
                                         write_sem.at[b])

        gather(0, 0).start()

        @pl.loop(0, n_chunks, step=2)
        def _(c0):
            for b in range(2):
                c = c0 + b
                gather(c, b).wait()

                @pl.when(c >= 1)
                def _():
                    write(c - 1, 1 - b).wait()

                @pl.when(c + 1 < n_chunks)
                def _():
                    gather(c + 1, 1 - b).start()

                write(c, b).start()

        write(n_chunks - 1, (n_chunks - 1) % 2).wait()

    return pl.kernel(
        body, mesh=_sc_mesh(),
        out_type=jax.ShapeDtypeStruct((n_idx, width), table.dtype),
        scratch_types=[pltpu.VMEM((per_worker,), I32),
                       pltpu.VMEM((2, SC_GATHER_ROWS, width), table.dtype),
                       pltpu.SemaphoreType.DMA((2,)),
                       pltpu.SemaphoreType.DMA((2,))],
        name="sc_gather_rows",
    )(table, idx)


def _combine_kernel(w_ref, h_ref, g2_ref, nf_ref, yk_ref, y_ref):
    t = h_ref.shape[0]
    w = w_ref[...]
    acc_lo = jnp.zeros((t, D_MODEL // 2), F32)
    acc_hi = jnp.zeros((t, D_MODEL // 2), F32)
    for kk in range(TOP_K):
        lo, hi = _unpack_bf16_pair(yk_ref[kk])
        wk = w[:, kk:kk + 1]
        acc_lo = acc_lo + wk * lo
        acc_hi = acc_hi + wk * hi
    out = h_ref[...] + g2_ref[0] * jnp.concatenate([acc_lo, acc_hi], axis=1)
    y_ref[...] = out * lax.rsqrt(jnp.mean(out * out, axis=-1, keepdims=True) + EPS) * nf_ref[...]


def _combine(w, h2, g2, normf, y_by_k, row_offset, tiles_per_batch):
    n = h2.shape[0]
    t = MOVE_TILE
    off = row_offset // t
    mod_rows = g2.shape[1]
    mod_tiles = max(ROW_TILE // t, 1) * tiles_per_batch if mod_rows == 1 else n // t
    return pl.pallas_call(
        _combine_kernel,
        grid=(n // t,),
        in_specs=[pl.BlockSpec((t, LANES), lambda i: (i + off, 0)),
                  pl.BlockSpec((t, D_MODEL), lambda i: (i, 0)),
                  pl.BlockSpec((1, mod_rows if mod_rows == 1 else t, D_MODEL),
                               (lambda i: (i // mod_tiles, 0, 0)) if mod_rows == 1
                               else (lambda i: (0, i, 0))),
                  pl.BlockSpec((1, D_MODEL), lambda i: (0, 0)),
                  pl.BlockSpec((TOP_K, t, D_MODEL // 2), lambda i: (0, i + off, 0))],
        out_specs=pl.BlockSpec((t, D_MODEL), lambda i: (i, 0)),
        out_shape=jax.ShapeDtypeStruct((n, D_MODEL), F32),
        compiler_params=_cparams(("arbitrary",), 40),
        name="combine",
    )(w, h2, g2, normf, y_by_k)


def _rotary_tables(pos):
    half = KEY_DIM_R // 2
    inv_freq = ROPE_BASE ** (-jnp.arange(half, dtype=F32) / half)
    ang = pos[:, None] * inv_freq[None, :]
    cos = jnp.cos(ang)
    sin = jnp.sin(ang)
    cos_t = jnp.concatenate([cos, cos], axis=1)
    sin_t = jnp.concatenate([-sin, sin], axis=1)
    return cos_t.astype(F32), sin_t.astype(F32)


def _rel_bias_table(rel_bias, n_rows, n_cols, q_offset):
    heads = rel_bias.shape[0]
    n_diag = n_rows + n_cols - 1
    dist = q_offset + (n_rows - 1) - np.arange(n_diag)
    idx = np.clip(dist, -REL_CLIP, REL_CLIP) + REL_CLIP
    n_hi = int(np.sum(dist > REL_CLIP))
    n_lo = int(np.sum(dist < -REL_CLIP))
    mid = rel_bias[:, int(idx[n_diag - n_lo - 1]):int(idx[n_hi]) + 1][:, ::-1]
    diag = jnp.concatenate([jnp.broadcast_to(rel_bias[:, 2 * REL_CLIP:], (heads, n_hi)), mid,
                            jnp.broadcast_to(rel_bias[:, :1], (heads, n_lo))], axis=1)
    period = n_diag + 1
    v = jnp.roll(jnp.pad(diag, ((0, 0), (0, 1))), -(n_rows - 1), axis=1)
    skew = jnp.tile(v, (1, n_rows))[:, :n_rows * (period - 1)].reshape(heads, n_rows, period - 1)
    return skew[:, :, :n_cols].astype(F32)


def _prompt_bias(rel_bias):
    n_cols = ATT_QB + ATT_WINDOW
    r = np.arange(ATT_QB)[:, None]
    c = np.arange(n_cols)[None, :]
    band = c - (r // CHUNK) * CHUNK
    valid = (band >= 0) & (band < ATT_WINDOW + CHUNK)
    table = _rel_bias_table(rel_bias, ATT_QB, n_cols, ATT_WINDOW)
    return jnp.where(jnp.asarray(valid)[None], table, NEG_BIG)


def _sample_bias(rel_bias, t_new, cache_len):
    b = _rel_bias_table(rel_bias, t_new, cache_len + t_new, cache_len)
    return b[:, :, :cache_len], b[:, :, cache_len:]


def _mod_parts(mod, rows_each):
    parts = jnp.split(mod, 6, axis=-1)
    if rows_each == 1:
        return [p[:, None, :] for p in parts]
    return [jnp.repeat(p, rows_each, axis=0)[None] for p in parts]


def kernel(x_prompt, x_sample, cache_attn_k, cache_attn_v, state_ret, c_prompt, c_sample,
           norm1_g, norm2_g, w_ada, b_ada, w_in, rel_bias, w_o_attn, w_o_ret, w_out,
           w_router, b_router, w_exp_gate, w_exp_up, w_exp_down, w_sh_gate, w_sh_up, w_sh_down,
           normf_g):
    batch, seq, d = x_prompt.shape
    dec_batch, dec_seq, _ = x_sample.shape
    depth = w_in.shape[0]
    assert depth == 1 and d == D_MODEL
    assert seq % ROW_TILE == 0 and dec_batch * dec_seq == ROW_TILE and ROW_TILE == ATT_WINDOW
    cache_len = cache_attn_k.shape[2]
    n_p = batch * seq
    n_s = dec_batch * dec_seq
    tpb = seq // ROW_TILE
    l = 0

    bf = lambda a: a.astype(BF16)
    c_all = jnp.concatenate([c_prompt, c_sample], axis=0)
    pad = (-c_all.shape[0]) % 8
    c_all = jnp.pad(c_all, ((0, pad), (0, 0)))
    mod = _ada(c_all, bf(w_ada[l]), b_ada[l][None, :])
    mod_p = _mod_parts(mod[:batch], 1)
    mod_s = _mod_parts(mod[batch:batch + dec_batch], dec_seq)

    w_in_bf = bf(w_in[l])
    n1g = norm1_g[l][None, :]
    n2g = norm2_g[l][None, :]
    dense_w = [bf(w_o_attn[l]), bf(w_o_ret[l]), bf(w_out[l]), bf(w_router[l]).T,
               bf(w_sh_gate[l]), bf(w_sh_up[l]), bf(w_sh_down[l])]

    xp = x_prompt.reshape(n_p, d)
    xs_ = x_sample.reshape(n_s, d)
    cos_p, sin_p = _rotary_tables(jnp.arange(seq, dtype=F32))
    pos_s = PAST_LEN + jnp.arange(dec_seq, dtype=F32)
    cos_s, sin_s = _rotary_tables(jnp.tile(pos_s, dec_batch))

    (qa, ka, va, qr, kr, vr, gr, ga, gb, kv_p) = _inproj(
        xp, mod_p[1], mod_p[0], n1g, cos_p, sin_p, w_in_bf, tpb)
    oa = _attn_prompt(qa, ka, va, _prompt_bias(rel_bias[l]), batch, seq)
    zero_state = jnp.zeros((batch, N_HEADS, KEY_DIM_R, VAL_DIM_R), F32)
    yr_in, state_p = _retention(qr, kr, vr, gr, zero_state, batch, seq, RET_CHUNK)
    h_p, n2p_p, s_p = _outproj(xp, oa, yr_in, ga, gb, mod_p[2], mod_p[4], mod_p[3], mod_p[5], n2g,
                               dense_w, tpb, n_p + n_s, 0)

    (qa_s, ka_s, va_s, qr_s, kr_s, vr_s, gr_s, ga_s, gb_s, kv_s) = _inproj(
        xs_, mod_s[1], mod_s[0], n1g, cos_s, sin_s, w_in_bf, 1)
    bias_c, bias_n = _sample_bias(rel_bias[l], dec_seq, cache_len)
    to_keys_minor = lambda c: jnp.transpose(c, (0, 1, 3, 4, 2))
    oa_s = _attn_sample(qa_s, ka_s, va_s, to_keys_minor(cache_attn_k), to_keys_minor(cache_attn_v),
                        bias_c, bias_n, dec_batch, dec_seq, cache_len)
    yr_in_s, state_s = _retention(qr_s, kr_s, vr_s, gr_s, state_ret[l], dec_batch, dec_seq, dec_seq)
    h_s, n2p, scores_t = _outproj(xs_, oa_s, yr_in_s, ga_s, gb_s, mod_s[2], mod_s[4], mod_s[3],
                                  mod_s[5], n2g, dense_w, 1, n_p + n_s, n_p, carried=(n2p_p, s_p))

    lanes_of = lambda v: jnp.broadcast_to(v[:, None], (N_EXPERTS, ROUTE_TILE))
    idx_t, w_t, rank_t, counts = _route(scores_t, lanes_of(b_router[l]))
    (tile_expert, n_tiles, expert_slot, expert_next, tile_halves, pstart, pad_rows,
     n_sorted_rows) = _ffn_plan(counts[:, 0], (n_p + n_s) * TOP_K)
    dest_t = _dest(idx_t, rank_t, lanes_of(pstart.astype(F32)))
    dest_kmajor = dest_t.reshape(-1)
    w_route = jnp.pad(w_t.T, ((0, 0), (0, LANES - TOP_K)))
    xs_sorted = _sc_dispatch(n2p, dest_kmajor, pad_rows, n_sorted_rows)
    ys_sorted = _ffn(tile_expert, n_tiles, expert_slot, expert_next, tile_halves, xs_sorted,
                     w_exp_gate[l], w_exp_up[l], w_exp_down[l])
    nf = normf_g[None, :]
    y_by_k = _sc_gather_rows(ys_sorted, dest_kmajor).reshape(TOP_K, n_p + n_s, d // 2)
    y_p = _combine(w_route, h_p, mod_p[5], nf, y_by_k, 0, tpb)
    y_s = _combine(w_route, h_s, mod_s[5], nf, y_by_k, n_p, 1)

    keep = min(ATT_WINDOW, seq)
    kv_p = kv_p.reshape(batch, ROW_TILE, 2, N_HEADS, HEAD_DIM_A)[:, ROW_TILE - keep:]
    kv_s = kv_s.reshape(dec_batch, dec_seq, 2, N_HEADS, HEAD_DIM_A)
    return (y_p.reshape(batch, seq, d), y_s.reshape(dec_batch, dec_seq, d),
            kv_p[:, :, 0][None], kv_p[:, :, 1][None], state_p[None],
            kv_s[:, :, 0][None], kv_s[:, :, 1][None], state_s[None])
```

```python
import functools

import numpy as np
import jax
import jax.numpy as jnp
from jax import lax
from jax.experimental import pallas as pl
from jax.experimental.pallas import tpu as pltpu
from jax.experimental.pallas import tpu_sc as plsc

F32 = jnp.float32
BF16 = jnp.bfloat16
I32 = jnp.int32
U32 = jnp.uint32

D_MODEL = 1024
PAST_LEN = 4096
CHUNK = 64
N_LEFT_CHUNKS = 8
ATT_WINDOW = N_LEFT_CHUNKS * CHUNK
N_HEADS = 8
HEAD_DIM_A = 64
D_ATT = N_HEADS * HEAD_DIM_A
REL_CLIP = 128
KEY_DIM_R = 64
VAL_DIM_R = 128
D_RET_K = N_HEADS * KEY_DIM_R
D_RET_V = N_HEADS * VAL_DIM_R
ROPE_BASE = 10000.0
N_EXPERTS = 256
TOP_K = 8
N_GROUPS = 8
GROUP_SIZE = N_EXPERTS // N_GROUPS
TOPK_GROUPS = 4
D_EXPERT = 256
ROUTED_SCALE = 2.5
EPS = 1e-6
IN_WIDTHS = (D_ATT, D_ATT, D_ATT, D_RET_K, D_RET_K, D_RET_V, D_RET_V, D_MODEL, D_MODEL)
IN_OFFS = tuple(int(v) for v in np.cumsum((0,) + IN_WIDTHS))
D_IN = IN_OFFS[-1]

NEG_BIG = -1e30
LANES = 128
V7X_VMEM_BYTES = 64 * 1024 * 1024
V7X_SC_CORES = 2
V7X_SC_SUBCORES = 16
SC_GATHER_ROWS = 96
SPARE_ROWS = 8192

ROW_TILE = 512
ATT_QB = 256
RET_CHUNK = 256
ROUTE_TILE = 512
MOVE_TILE = 256
FFN_QUANTUM = 128
FFN_MATMUL_ROWS = 5 * FFN_QUANTUM
FFN_TILE = 5 * FFN_QUANTUM
FFN_WEIGHT_SLOTS = 4


def _cparams(semantics, vmem_mb):
    return pltpu.CompilerParams(dimension_semantics=semantics,
                                vmem_limit_bytes=min(vmem_mb * 1024 * 1024, V7X_VMEM_BYTES - (6 << 20)))


def _silu(x):
    return x * jax.nn.sigmoid(x)


def _pack_bf16_pair(lo, hi):
    return pltpu.bitcast(pltpu.pack_elementwise([lo, hi], packed_dtype=BF16), U32)


def _unpack_bf16_pair(u):
    words = pltpu.bitcast(u, I32)
    lo = pltpu.unpack_elementwise(words, index=0, packed_dtype=BF16, unpacked_dtype=F32)
    hi = pltpu.unpack_elementwise(words, index=1, packed_dtype=BF16, unpacked_dtype=F32)
    return lo, hi


def _ada_kernel(c_ref, w_ref, b_ref, o_ref):
    sc = _silu(c_ref[...]).astype(BF16)
    o_ref[...] = jnp.dot(sc, w_ref[...], preferred_element_type=F32) + b_ref[...]


def _ada(c_all, w_ada_bf, b_ada):
    rows = c_all.shape[0]
    n_out = w_ada_bf.shape[1]
    blk = D_MODEL
    return pl.pallas_call(
        _ada_kernel,
        grid=(n_out // blk,),
        in_specs=[pl.BlockSpec((rows, D_MODEL), lambda j: (0, 0)),
                  pl.BlockSpec((D_MODEL, blk), lambda j: (0, j)),
                  pl.BlockSpec((1, blk), lambda j: (0, j))],
        out_specs=pl.BlockSpec((rows, blk), lambda j: (0, j)),
        out_shape=jax.ShapeDtypeStruct((rows, n_out), F32),
        compiler_params=_cparams(("arbitrary",), 24),
        name="ada",
    )(c_all, w_ada_bf, b_ada)


def _inproj_kernel(x_ref, sc_ref, sh_ref, g_ref, cos_ref, sin_ref, w_ref,
                   qa_ref, ka_ref, va_ref, qr_ref, kr_ref, vr_ref, gr_ref, ga_ref, gb_ref,
                   kv_ref, *, tiles_per_batch):
    x = x_ref[...]
    xn = x * lax.rsqrt(jnp.mean(x * x, axis=-1, keepdims=True) + EPS) * g_ref[...]
    nb = (xn * (1.0 + sc_ref[0]) + sh_ref[0]).astype(BF16)

    def proj(seg):
        return jnp.dot(nb, w_ref[:, IN_OFFS[seg]:IN_OFFS[seg + 1]], preferred_element_type=F32)

    qa_ref[...] = proj(0).astype(BF16)
    ka = proj(1)
    va = proj(2)
    ka_ref[...] = ka.astype(BF16)
    va_ref[...] = va.astype(BF16)

    @pl.when(pl.program_id(0) % tiles_per_batch == tiles_per_batch - 1)
    def _():
        kv_ref[:, :D_ATT] = ka
        kv_ref[:, D_ATT:] = va

    cos = jnp.tile(cos_ref[...], (1, N_HEADS))
    sin = jnp.tile(sin_ref[...], (1, N_HEADS))
    first_half = (lax.broadcasted_iota(I32, (1, D_RET_K), 1) % KEY_DIM_R) < (KEY_DIM_R // 2)

    def rotary(t):
        partner = jnp.where(first_half, pltpu.roll(t, D_RET_K - KEY_DIM_R // 2, 1),
                            pltpu.roll(t, KEY_DIM_R // 2, 1))
        return t * cos + partner * sin

    qr_ref[...] = rotary(proj(3)).astype(BF16)
    kr_ref[...] = (rotary(proj(4)) * (KEY_DIM_R ** -0.5)).astype(BF16)
    vr_ref[...] = proj(5).astype(BF16)
    gr_ref[...] = proj(6).astype(BF16)
    ga_ref[...] = proj(7).astype(BF16)
    gb_ref[...] = proj(8).astype(BF16)


def _inproj(x2d, sc, sh, g, cos_t, sin_t, w_in_bf, tiles_per_batch):
    n = x2d.shape[0]
    tm = ROW_TILE
    n_tiles = n // tm
    n_batches = n_tiles // tiles_per_batch
    mod_rows = sc.shape[1]
    pos_tiles = cos_t.shape[0] // tm

    def row_spec(width):
        return pl.BlockSpec((tm, width), lambda i: (i, 0))

    mod_spec = pl.BlockSpec((1, mod_rows, D_MODEL), lambda i: (i // tiles_per_batch, 0, 0))
    pos_spec = pl.BlockSpec((tm, KEY_DIM_R), lambda i: (i % pos_tiles, 0))
    out_widths = (D_ATT, D_ATT, D_ATT, D_RET_K, D_RET_K, D_RET_V, D_RET_V, D_MODEL, D_MODEL)
    out_shape = [jax.ShapeDtypeStruct((n, w), BF16) for w in out_widths]
    out_shape.append(jax.ShapeDtypeStruct((n_batches * tm, 2 * D_ATT), F32))
    out_specs = [row_spec(w) for w in out_widths]
    out_specs.append(pl.BlockSpec((tm, 2 * D_ATT), lambda i: (i // tiles_per_batch, 0)))
    return pl.pallas_call(
        functools.partial(_inproj_kernel, tiles_per_batch=tiles_per_batch),
        grid=(n_tiles,),
        in_specs=[row_spec(D_MODEL), mod_spec, mod_spec,
                  pl.BlockSpec((1, D_MODEL), lambda i: (0, 0)),
                  pos_spec, pos_spec,
                  pl.BlockSpec((D_MODEL, D_IN), lambda i: (0, 0))],
        out_specs=out_specs,
        out_shape=out_shape,
        compiler_params=_cparams(("arbitrary",), 56),
        name="inproj",
    )(x2d, sc, sh, g, cos_t, sin_t, w_in_bf)


def _softmax_pv(s, v_parts):
    m = functools.reduce(jnp.maximum, [jnp.max(t, axis=-1, keepdims=True) for t in s])
    ps = [jnp.exp(t - m) for t in s]
    l = functools.reduce(jnp.add, [jnp.sum(p, axis=-1, keepdims=True) for p in ps])
    o = functools.reduce(jnp.add, [jnp.dot(p.astype(BF16), v, preferred_element_type=F32)
                                   for p, v in zip(ps, v_parts)])
    return o / l


def _attn_prompt_kernel(q_ref, k0_ref, k1_ref, k2_ref, v0_ref, v1_ref, v2_ref, bias_ref, o_ref):
    j = pl.program_id(1)
    q = q_ref[...]
    k = jnp.concatenate([k0_ref[...], k1_ref[...], k2_ref[...]], axis=0)
    v = jnp.concatenate([v0_ref[...], v1_ref[...], v2_ref[...]], axis=0)
    n_keys = k.shape[0]
    key_block = lax.broadcasted_iota(I32, (1, n_keys), 1) // ATT_QB
    before_start = jnp.where(key_block < 2 - j, NEG_BIG, 0.0)
    outs = []
    for h in range(N_HEADS):
        sl = slice(h * HEAD_DIM_A, (h + 1) * HEAD_DIM_A)
        qh = (q[:, sl].astype(F32) * (HEAD_DIM_A ** -0.5)).astype(BF16)
        s = lax.dot_general(qh, k[:, sl], (((1,), (1,)), ((), ())), preferred_element_type=F32)
        s = s + bias_ref[h] + before_start
        outs.append(_softmax_pv([s], [v[:, sl]]))
    o_ref[...] = jnp.concatenate(outs, axis=1).astype(BF16)


def _attn_prompt(q, k, v, bias_full, batch, seq):
    qb = ATT_QB
    nq = seq // qb

    def q_map(b, j):
        return (b * nq + j, 0)

    def kv_map(back):
        return lambda b, j: (b * nq + jnp.maximum(j - back, 0), 0)

    blk = lambda m: pl.BlockSpec((qb, D_ATT), m)
    return pl.pallas_call(
        _attn_prompt_kernel,
        grid=(batch, nq),
        in_specs=[blk(q_map), blk(kv_map(2)), blk(kv_map(1)), blk(kv_map(0)),
                  blk(kv_map(2)), blk(kv_map(1)), blk(kv_map(0)),
                  pl.BlockSpec(bias_full.shape, lambda b, j: (0, 0, 0))],
        out_specs=blk(q_map),
        out_shape=jax.ShapeDtypeStruct((batch * seq, D_ATT), BF16),
        compiler_params=_cparams(("parallel", "arbitrary"), 40),
        name="attn_prompt",
    )(q, k, k, k, v, v, v, bias_full)


SAMPLE_ATT_BATCHES = 2


def _attn_sample_kernel(q_ref, kn_ref, vn_ref, ck_ref, cv_ref, bc_ref, bn_ref, o_ref, *, t_new):
    nt = (((1,), (1,)), ((), ()))
    for b in range(SAMPLE_ATT_BATCHES):
        rows = slice(b * t_new, (b + 1) * t_new)
        q = q_ref[rows, :]
        kn = kn_ref[rows, :]
        vn = vn_ref[rows, :]
        outs = []
        for h in range(N_HEADS):
            sl = slice(h * HEAD_DIM_A, (h + 1) * HEAD_DIM_A)
            qh = (q[:, sl].astype(F32) * (HEAD_DIM_A ** -0.5)).astype(BF16)
            kc_t = ck_ref[b, h].astype(BF16)
            vc_t = cv_ref[b, h].astype(BF16)
            s_c = jnp.dot(qh, kc_t, preferred_element_type=F32) + bc_ref[h]
            s_n = lax.dot_general(qh, kn[:, sl], nt, preferred_element_type=F32) + bn_ref[h]
            m = jnp.maximum(jnp.max(s_c, axis=-1, keepdims=True), jnp.max(s_n, axis=-1, keepdims=True))
            p_c = jnp.exp(s_c - m)
            p_n = jnp.exp(s_n - m)
            l = jnp.sum(p_c, axis=-1, keepdims=True) + jnp.sum(p_n, axis=-1, keepdims=True)
            o = (lax.dot_general(p_c.astype(BF16), vc_t, nt, preferred_element_type=F32)
                 + jnp.dot(p_n.astype(BF16), vn[:, sl], preferred_element_type=F32))
            outs.append(o / l)
        o_ref[rows, :] = jnp.concatenate(outs, axis=1).astype(BF16)


def _attn_sample(q, k, v, cache_k_t, cache_v_t, bias_cache, bias_new, batch, t_new, cache_len):
    nb = SAMPLE_ATT_BATCHES
    blk = pl.BlockSpec((nb * t_new, D_ATT), lambda b: (b, 0))
    cblk = pl.BlockSpec((None, nb, N_HEADS, HEAD_DIM_A, cache_len), lambda b: (0, b, 0, 0, 0))
    return pl.pallas_call(
        functools.partial(_attn_sample_kernel, t_new=t_new),
        grid=(batch // nb,),
        in_specs=[blk, blk, blk, cblk, cblk,
                  pl.BlockSpec(bias_cache.shape, lambda b: (0, 0, 0)),
                  pl.BlockSpec(bias_new.shape, lambda b: (0, 0, 0))],
        out_specs=blk,
        out_shape=jax.ShapeDtypeStruct((batch * t_new, D_ATT), BF16),
        compiler_params=_cparams(("arbitrary",), 40),
        name="attn_sample",
    )(q, k, v, cache_k_t, cache_v_t, bias_cache, bias_new)


def _ret_kernel(q_ref, k_ref, v_ref, g_ref, s0_ref, dmask_ref, qdec_ref, kdec_ref, sdec_ref,
                y_ref, sout_ref, state_ref):
    c = pl.program_id(1)

    @pl.when(c == 0)
    def _():
        state_ref[...] = s0_ref[0]

    q = q_ref[...]
    k = k_ref[...]
    v = v_ref[...]
    g = g_ref[...]
    outs = []
    for h in range(N_HEADS):
        ks = slice(h * KEY_DIM_R, (h + 1) * KEY_DIM_R)
        vs = slice(h * VAL_DIM_R, (h + 1) * VAL_DIM_R)
        qh, kh, vh = q[:, ks], k[:, ks], v[:, vs]
        scores = lax.dot_general(qh, kh, (((1,), (1,)), ((), ())), preferred_element_type=F32)
        inner = jnp.dot((scores * dmask_ref[h]).astype(BF16), vh, preferred_element_type=F32)
        state = state_ref[h]
        cross = jnp.dot(qh, state.astype(BF16), preferred_element_type=F32) * qdec_ref[h]
        o = inner + cross
        v_dec = (vh.astype(F32) * kdec_ref[h]).astype(BF16)
        state_ref[h] = sdec_ref[h] * state + lax.dot_general(
            kh, v_dec, (((0,), (0,)), ((), ())), preferred_element_type=F32)
        on = o * lax.rsqrt(jnp.mean(o * o, axis=-1, keepdims=True) + EPS)
        outs.append(on * _silu(g[:, vs].astype(F32)))
    y_ref[...] = jnp.concatenate(outs, axis=1).astype(BF16)

    @pl.when(c == pl.num_programs(1) - 1)
    def _():
        sout_ref[0] = state_ref[...]


def _ret_tables(chunk):
    log_g = jnp.log(1.0 - jnp.exp2(-5.0 - jnp.arange(N_HEADS, dtype=F32)))
    i = jnp.arange(chunk, dtype=F32)
    diff = i[:, None] - i[None, :]
    dmask = jnp.where(diff >= 0, jnp.exp(log_g[:, None, None] * jnp.maximum(diff, 0.0)), 0.0)
    qdec = jnp.exp(log_g[:, None] * (i + 1.0))
    kdec = jnp.exp(log_g[:, None] * (chunk - 1.0 - i))
    sdec = jnp.exp(log_g * chunk)
    bc = lambda t: jnp.broadcast_to(t[:, :, None], (N_HEADS, t.shape[1], VAL_DIM_R)).astype(F32)
    sdec_t = jnp.broadcast_to(sdec[:, None, None], (N_HEADS, 1, VAL_DIM_R)).astype(F32)
    return dmask.astype(F32), bc(qdec), bc(kdec), sdec_t


def _retention(q, k, v, gate, state0, batch, seq, chunk):
    nc = seq // chunk
    dmask, qdec, kdec, sdec = _ret_tables(chunk)
    row = lambda w: pl.BlockSpec((chunk, w), lambda b, c: (b * nc + c, 0))
    const = lambda a: pl.BlockSpec(a.shape, lambda b, c: (0,) * a.ndim)
    st_spec = pl.BlockSpec((1, N_HEADS, KEY_DIM_R, VAL_DIM_R), lambda b, c: (b, 0, 0, 0))
    return pl.pallas_call(
        _ret_kernel,
        grid=(batch, nc),
        in_specs=[row(D_RET_K), row(D_RET_K), row(D_RET_V), row(D_RET_V), st_spec,
                  const(dmask), const(qdec), const(kdec), const(sdec)],
        out_specs=[row(D_RET_V), st_spec],
        out_shape=[jax.ShapeDtypeStruct((batch * seq, D_RET_V), BF16),
                   jax.ShapeDtypeStruct((batch, N_HEADS, KEY_DIM_R, VAL_DIM_R), F32)],
        scratch_shapes=[pltpu.VMEM((N_HEADS, KEY_DIM_R, VAL_DIM_R), F32)],
        compiler_params=_cparams(("parallel", "arbitrary"), 32),
        name="retention",
    )(q, k, v, gate, state0, dmask, qdec, kdec, sdec)


def _outproj_kernel(x_ref, oa_ref, yr_ref, ga_ref, gb_ref, g1_ref, sc2_ref, sh2_ref, g2_ref, n2g_ref,
                    woa_ref, wor_ref, wout_ref, wrt_ref, wsg_ref, wsu_ref, wsd_ref, *rest):
    h_ref, n2p_ref, s_ref = rest[-3:]
    ya = jnp.dot(oa_ref[...], woa_ref[...], preferred_element_type=F32)
    yr = jnp.dot(yr_ref[...], wor_ref[...], preferred_element_type=F32)
    merged = (jax.nn.sigmoid(ga_ref[...].astype(F32)) * ya
              + jax.nn.sigmoid(gb_ref[...].astype(F32)) * yr)
    mix = jnp.dot(merged.astype(BF16), wout_ref[...], preferred_element_type=F32)
    h = x_ref[...] + g1_ref[0] * mix
    hn = h * lax.rsqrt(jnp.mean(h * h, axis=-1, keepdims=True) + EPS) * n2g_ref[...]
    n2 = hn * (1.0 + sc2_ref[0]) + sh2_ref[0]
    n2b = n2.astype(BF16)
    s_ref[...] = jax.nn.sigmoid(lax.dot_general(wrt_ref[...], n2b, (((1,), (1,)), ((), ())),
                                                preferred_element_type=F32))
    hid = _silu(jnp.dot(n2b, wsg_ref[...], preferred_element_type=F32)) * jnp.dot(
        n2b, wsu_ref[...], preferred_element_type=F32)
    shared = jnp.dot(hid.astype(BF16), wsd_ref[...], preferred_element_type=F32)
    h_ref[...] = h + g2_ref[0] * shared
    half = D_MODEL // 2
    n2p_ref[...] = _pack_bf16_pair(n2[:, :half], n2[:, half:])


def _outproj(x2d, oa, yr_in, ga, gb, g1, sc2, sh2, g2, n2g, weights, tiles_per_batch,
             all_tokens, token_offset, carried=None):
    n = x2d.shape[0]
    tm = ROW_TILE
    off = token_offset // tm
    mod_rows = g1.shape[1]
    row = lambda w: pl.BlockSpec((tm, w), lambda i: (i, 0))
    mod_spec = pl.BlockSpec((1, mod_rows, D_MODEL), lambda i: (i // tiles_per_batch, 0, 0))
    const = lambda a: pl.BlockSpec(a.shape, lambda i: (0,) * a.ndim)
    in_specs = [row(D_MODEL), row(D_ATT), row(D_RET_V), row(D_MODEL), row(D_MODEL),
                mod_spec, mod_spec, mod_spec, mod_spec, const(n2g)] + [const(w) for w in weights]
    args = [x2d, oa, yr_in, ga, gb, g1, sc2, sh2, g2, n2g, *weights]
    aliases = {}
    if carried is not None:
        aliases = {len(args): 1, len(args) + 1: 2}
        in_specs += [pl.BlockSpec(memory_space=pl.ANY)] * 2
        args += list(carried)
    return pl.pallas_call(
        _outproj_kernel,
        grid=(n // tm,),
        in_specs=in_specs,
        out_specs=[row(D_MODEL),
                   pl.BlockSpec((tm, D_MODEL // 2), lambda i: (i + off, 0)),
                   pl.BlockSpec((N_EXPERTS, tm), lambda i: (0, i + off))],
        out_shape=[jax.ShapeDtypeStruct((n, D_MODEL), F32),
                   jax.ShapeDtypeStruct((all_tokens, D_MODEL // 2), U32),
                   jax.ShapeDtypeStruct((N_EXPERTS, all_tokens), F32)],
        input_output_aliases=aliases,
        compiler_params=_cparams(("arbitrary",), 48),
        name="outproj",
    )(*args)


def _route_kernel(s_ref, b_ref, idx_ref, w_ref, rank_ref, cnt_ref, run_ref, tri_ref):
    step = pl.program_id(0)
    t = s_ref.shape[1]

    @pl.when(step == 0)
    def _():
        run_ref[...] = jnp.zeros_like(run_ref)
        r = lax.broadcasted_iota(I32, (t, t), 0)
        c = lax.broadcasted_iota(I32, (t, t), 1)
        tri_ref[...] = jnp.where(r < c, 1.0, 0.0).astype(BF16)

    s = s_ref[...]
    sel = s + b_ref[...]
    row_f = lax.broadcasted_iota(I32, (N_EXPERTS, t), 0).astype(F32)

    def first_argmax(vals, rows):
        m = jnp.max(vals, axis=0, keepdims=True)
        pos = jnp.min(jnp.where(vals == m, rows, float(N_EXPERTS)), axis=0, keepdims=True)
        return m, pos

    gscore = []
    group_row = lax.broadcasted_iota(I32, (GROUP_SIZE, t), 0).astype(F32)
    for g in range(N_GROUPS):
        rows = slice(g * GROUP_SIZE, (g + 1) * GROUP_SIZE)
        m1, p1 = first_argmax(sel[rows], group_row)
        m2 = jnp.max(jnp.where(group_row == p1, -jnp.inf, sel[rows]), axis=0, keepdims=True)
        gscore.append(m1 + m2)
    cand_parts = []
    for g in range(N_GROUPS):
        rows = slice(g * GROUP_SIZE, (g + 1) * GROUP_SIZE)
        beaten_by = jnp.zeros((1, t), F32)
        for o in range(N_GROUPS):
            if o == g:
                continue
            wins = (gscore[o] > gscore[g]) if o > g else (gscore[o] >= gscore[g])
            beaten_by = beaten_by + jnp.where(wins, 1.0, 0.0)
        cand_parts.append(jnp.where(beaten_by < TOPK_GROUPS, sel[rows], -jnp.inf))
    cand = jnp.concatenate(cand_parts, axis=0)

    picked = jnp.zeros((N_EXPERTS, t), F32)
    idx_rows, w_rows = [], []
    for _ in range(TOP_K):
        _, pos = first_argmax(cand, row_f)
        hit = row_f == pos
        w_rows.append(jnp.sum(jnp.where(hit, s, 0.0), axis=0, keepdims=True))
        idx_rows.append(pos)
        picked = jnp.where(hit, 1.0, picked)
        cand = jnp.where(hit, -jnp.inf, cand)
    w_sum = functools.reduce(jnp.add, w_rows)

    before = jnp.dot(picked.astype(BF16), tri_ref[...], preferred_element_type=F32) + run_ref[...]
    run_ref[...] = run_ref[...] + jnp.sum(picked, axis=1, keepdims=True)
    rank_rows = [jnp.sum(jnp.where(row_f == idx_rows[kk], before, 0.0), axis=0, keepdims=True)
                 for kk in range(TOP_K)]

    idx_ref[...] = jnp.concatenate(idx_rows, axis=0).astype(I32)
    w_ref[...] = jnp.concatenate([w / w_sum * ROUTED_SCALE for w in w_rows], axis=0)
    rank_ref[...] = jnp.concatenate(rank_rows, axis=0).astype(I32)

    @pl.when(step == pl.num_programs(0) - 1)
    def _():
        cnt_ref[...] = run_ref[...].astype(I32)


def _route(scores_t, b_col):
    n = scores_t.shape[1]
    t = ROUTE_TILE
    col = pl.BlockSpec((TOP_K, t), lambda i: (0, i))
    const = pl.BlockSpec((N_EXPERTS, t), lambda i: (0, 0))
    return pl.pallas_call(
        _route_kernel,
        grid=(n // t,),
        in_specs=[pl.BlockSpec((N_EXPERTS, t), lambda i: (0, i)), const],
        out_specs=[col, col, col, const],
        out_shape=[jax.ShapeDtypeStruct((TOP_K, n), I32),
                   jax.ShapeDtypeStruct((TOP_K, n), F32),
                   jax.ShapeDtypeStruct((TOP_K, n), I32),
                   jax.ShapeDtypeStruct((N_EXPERTS, t), I32)],
        scratch_shapes=[pltpu.VMEM((N_EXPERTS, t), F32), pltpu.VMEM((t, t), BF16)],
        compiler_params=_cparams(("arbitrary",), 32),
        name="route",
    )(scores_t, b_col)


def _dest_kernel(idx_ref, rank_ref, start_ref, dest_ref):
    t = idx_ref.shape[1]
    row = lax.broadcasted_iota(I32, (N_EXPERTS, t), 0)
    starts = start_ref[...]
    base = [jnp.sum(jnp.where(row == idx_ref[kk:kk + 1, :], starts, 0.0), axis=0, keepdims=True)
            for kk in range(TOP_K)]
    dest_ref[...] = jnp.concatenate(base, axis=0).astype(I32) + rank_ref[...]


def _dest(idx_t, rank_t, starts_col):
    n = idx_t.shape[1]
    t = ROUTE_TILE
    col = pl.BlockSpec((TOP_K, t), lambda i: (0, i))
    return pl.pallas_call(
        _dest_kernel,
        grid=(n // t,),
        in_specs=[col, col, pl.BlockSpec((N_EXPERTS, t), lambda i: (0, 0))],
        out_specs=col,
        out_shape=jax.ShapeDtypeStruct((TOP_K, n), I32),
        compiler_params=_cparams(("arbitrary",), 32),
        name="dest",
    )(idx_t, rank_t, starts_col)


def _sc_mesh():
    return plsc.VectorSubcoreMesh(core_axis_name="core", subcore_axis_name="subcore",
                                  num_cores=V7X_SC_CORES, num_subcores=V7X_SC_SUBCORES)


def _sc_dispatch(n2p, dest_kmajor, pad_rows, n_out_rows):
    n, width = n2p.shape
    rows = SC_GATHER_ROWS
    n_workers = V7X_SC_CORES * V7X_SC_SUBCORES
    src_chunks = n // rows
    items = dest_kmajor.shape[0] // rows
    per_worker = items // n_workers
    pad_per_worker = pad_rows.shape[0] // rows // n_workers
    assert src_chunks * rows == n and per_worker * n_workers == items and per_worker % 2 == 0
    assert pad_per_worker * n_workers * rows == pad_rows.shape[0]
    idx3 = dest_kmajor.reshape(n_workers, per_worker, rows)
    pad3 = pad_rows.reshape(n_workers, pad_per_worker, rows)
    zeros = jnp.zeros((rows, width), n2p.dtype)

    def body(src_hbm, idx_hbm, pad_hbm, zero_hbm, out_hbm, idx_v, pad_v, rows_v, load_sem, scat_sem):
        worker = lax.axis_index("subcore") * V7X_SC_CORES + lax.axis_index("core")
        pltpu.sync_copy(idx_hbm.at[worker], idx_v)
        pltpu.sync_copy(pad_hbm.at[worker], pad_v)
        pltpu.sync_copy(zero_hbm, rows_v.at[0])

        def zero_fill(c):
            return pltpu.make_async_copy(rows_v.at[0], out_hbm.at[pad_v.at[c]], scat_sem.at[0])

        @pl.loop(0, pad_per_worker)
        def _(c):
            zero_fill(c).start()

        @pl.loop(0, pad_per_worker)
        def _(c):
            zero_fill(c).wait()

        def load(c, b):
            chunk = lax.rem(worker * per_worker + c, src_chunks)
            off = pl.multiple_of(chunk * rows, rows)
            return pltpu.make_async_copy(src_hbm.at[pl.ds(off, rows)], rows_v.at[b], load_sem.at[b])

        def scatter(c, b):
            return pltpu.make_async_copy(rows_v.at[b], out_hbm.at[idx_v.at[c]], scat_sem.at[b])

        load(0, 0).start()

        @pl.loop(0, per_worker, step=2)
        def _(c0):
            for b in range(2):
                c = c0 + b
                load(c, b).wait()

                @pl.when(c >= 1)
                def _():
                    scatter(c - 1, 1 - b).wait()

                @pl.when(c + 1 < per_worker)
                def _():
                    load(c + 1, 1 - b).start()

                scatter(c, b).start()

        scatter(per_worker - 1, (per_worker - 1) % 2).wait()

    return pl.kernel(
        body, mesh=_sc_mesh(),
        out_type=jax.ShapeDtypeStruct((n_out_rows, width), n2p.dtype),
        scratch_types=[pltpu.VMEM((per_worker, rows), I32),
                       pltpu.VMEM((pad_per_worker, rows), I32),
                       pltpu.VMEM((2, rows, width), n2p.dtype),
                       pltpu.SemaphoreType.DMA((2,)),
                       pltpu.SemaphoreType.DMA((2,))],
        name="sc_dispatch",
    )(n2p, idx3, pad3, zeros)


def _ffn_kernel(texp_ref, ntiles_ref, eslot_ref, enext_ref, nhalf_ref, xs_ref, wg_hbm, wu_hbm, wd_hbm,
                ys_ref, wg_buf, wu_buf, wd_buf, wgu_bf, wd_bf, sems):
    g = pl.program_id(0)

    def weight_copies(e, slot):
        return (pltpu.make_async_copy(wg_hbm.at[e], wg_buf.at[slot], sems.at[slot, 0]),
                pltpu.make_async_copy(wu_hbm.at[e], wu_buf.at[slot], sems.at[slot, 1]),
                pltpu.make_async_copy(wd_hbm.at[e], wd_buf.at[slot], sems.at[slot, 2]))

    @pl.when(g < ntiles_ref[0])
    def _():
        e = texp_ref[g]
        changed = jnp.logical_or(g == 0, texp_ref[jnp.maximum(g - 1, 0)] != e)

        @pl.when(changed)
        def _():
            slot = eslot_ref[e]

            def fetch_ahead(first_hop, hops, target_slot):
                ahead = first_hop
                for _ in range(hops - 1):
                    ahead = jnp.where(ahead >= 0, enext_ref[jnp.maximum(ahead, 0)], -1)

                @pl.when(ahead >= 0)
                def _():
                    for c in weight_copies(ahead, target_slot):
                        c.start()

            @pl.when(g == 0)
            def _():
                for c in weight_copies(e, slot):
                    c.start()
                for hops in range(1, FFN_WEIGHT_SLOTS - 1):
                    fetch_ahead(enext_ref[e], hops, lax.rem(slot + hops, FFN_WEIGHT_SLOTS))

            fetch_ahead(enext_ref[e], FFN_WEIGHT_SLOTS - 1,
                        lax.rem(slot + FFN_WEIGHT_SLOTS - 1, FFN_WEIGHT_SLOTS))
            for c in weight_copies(e, slot):
                c.wait()

            wgu_bf[:, :D_EXPERT] = wg_buf[slot].astype(BF16)
            wgu_bf[:, D_EXPERT:] = wu_buf[slot].astype(BF16)
            wd_bf[...] = wd_buf[slot].astype(BF16)

        def expert_rows(rows):
            lo, hi = _unpack_bf16_pair(xs_ref[rows, :])
            x = jnp.concatenate([lo, hi], axis=1).astype(BF16)
            gu = jnp.dot(x, wgu_bf[...], preferred_element_type=F32)
            hid = (_silu(gu[:, :D_EXPERT]) * gu[:, D_EXPERT:]).astype(BF16)
            y = jnp.dot(hid, wd_bf[...], preferred_element_type=F32)
            half = D_MODEL // 2
            ys_ref[rows, :] = _pack_bf16_pair(y[:, :half], y[:, half:])

        for n_groups in range(1, FFN_TILE // FFN_QUANTUM + 1):

            @pl.when(nhalf_ref[g] == n_groups)
            def _(n_groups=n_groups):
                used = n_groups * FFN_QUANTUM
                for start in range(0, used, FFN_MATMUL_ROWS):
                    expert_rows(slice(start, min(start + FFN_MATMUL_ROWS, used)))
                if used < FFN_TILE:
                    ys_ref[used:, :] = jnp.zeros((FFN_TILE - used, ys_ref.shape[1]), U32)


def _ffn(tile_expert, n_tiles, expert_slot, expert_next, tile_halves, xs, w_gate, w_up, w_down):
    rows, width = xs.shape
    m = FFN_TILE
    max_tiles = tile_expert.shape[0]
    row_map = lambda g, te, nt, es, en, nh: (jnp.minimum(g, nt[0] - 1), 0)
    hbm = pl.BlockSpec(memory_space=pl.ANY)
    grid_spec = pltpu.PrefetchScalarGridSpec(
        num_scalar_prefetch=5,
        grid=(max_tiles,),
        in_specs=[pl.BlockSpec((m, width), row_map), hbm, hbm, hbm],
        out_specs=pl.BlockSpec((m, width), row_map),
        scratch_shapes=[pltpu.VMEM((FFN_WEIGHT_SLOTS, D_MODEL, D_EXPERT), F32),
                        pltpu.VMEM((FFN_WEIGHT_SLOTS, D_MODEL, D_EXPERT), F32),
                        pltpu.VMEM((FFN_WEIGHT_SLOTS, D_EXPERT, D_MODEL), F32),
                        pltpu.VMEM((D_MODEL, 2 * D_EXPERT), BF16),
                        pltpu.VMEM((D_EXPERT, D_MODEL), BF16),
                        pltpu.SemaphoreType.DMA((FFN_WEIGHT_SLOTS, 3))],
    )
    return pl.pallas_call(
        _ffn_kernel,
        grid_spec=grid_spec,
        out_shape=jax.ShapeDtypeStruct((rows, width), U32),
        compiler_params=_cparams(("arbitrary",), 32),
        name="ffn",
    )(tile_expert, n_tiles, expert_slot, expert_next, tile_halves, xs, w_gate, w_up, w_down)


def _ffn_plan(counts, n_assign):
    m = FFN_TILE
    max_tiles = n_assign // m + N_EXPERTS
    padded = ((counts + m - 1) // m) * m
    pend = jnp.cumsum(padded).astype(I32)
    pstart = pend - padded
    n_tiles = pend[-1:] // m
    g = jnp.minimum(jnp.arange(max_tiles, dtype=I32), n_tiles - 1)
    tile_expert = jnp.sum((pend[None, :] <= (g * m)[:, None]).astype(I32), axis=1)
    tile_expert = jnp.minimum(tile_expert, N_EXPERTS - 1)
    vend = pstart + ((counts + FFN_QUANTUM - 1) // FFN_QUANTUM) * FFN_QUANTUM
    own = tile_expert[:, None] == jnp.arange(N_EXPERTS, dtype=I32)[None, :]
    tile_vend = jnp.sum(jnp.where(own, vend[None, :], 0), axis=1)
    tile_halves = (jnp.clip(tile_vend - g * m, 0, m) // FFN_QUANTUM).astype(I32)
    used = counts > 0
    expert_slot = ((jnp.cumsum(used.astype(I32)) - 1) % FFN_WEIGHT_SLOTS).astype(I32)
    ids = jnp.where(used, jnp.arange(N_EXPERTS, dtype=I32), N_EXPERTS)
    first_used_from = lax.cummin(ids, axis=0, reverse=True)
    nxt = jnp.concatenate([first_used_from[1:], jnp.full((1,), N_EXPERTS, I32)])
    expert_next = jnp.where(nxt < N_EXPERTS, nxt, -1).astype(I32)
    spare_row = max_tiles * m
    j = jnp.arange(FFN_QUANTUM, dtype=I32)[None, :]
    seg_end = (pstart + counts)[:, None]
    spare = spare_row + (jnp.arange(N_EXPERTS, dtype=I32)[:, None] * FFN_QUANTUM + j) % SPARE_ROWS
    pad_rows = jnp.where(j < (vend[:, None] - seg_end), seg_end + j, spare).astype(I32).reshape(-1)
    unit = SC_GATHER_ROWS * V7X_SC_CORES * V7X_SC_SUBCORES
    extra = (-pad_rows.shape[0]) % unit
    filler = spare_row + (pad_rows.shape[0] + jnp.arange(extra, dtype=I32)) % SPARE_ROWS
    pad_rows = jnp.concatenate([pad_rows, filler])
    return (tile_expert, n_tiles, expert_slot, expert_next, tile_halves, pstart, pad_rows,
            spare_row + SPARE_ROWS)


def _sc_gather_rows(table, idx):
    n_idx = idx.shape[0]
    width = table.shape[1]
    n_workers = V7X_SC_CORES * V7X_SC_SUBCORES
    per_worker = n_idx // n_workers
    n_chunks = per_worker // SC_GATHER_ROWS
    assert per_worker * n_workers == n_idx and n_chunks * SC_GATHER_ROWS == per_worker and n_chunks % 2 == 0

    def body(table_hbm, idx_hbm, out_hbm, idx_v, rows_v, gather_sem, write_sem):
        worker = lax.axis_index("subcore") * V7X_SC_CORES + lax.axis_index("core")
        base = worker * per_worker
        pltpu.sync_copy(idx_hbm.at[pl.ds(base, per_worker)], idx_v)

        def gather(c, b):
            off = pl.multiple_of(c * SC_GATHER_ROWS, SC_GATHER_ROWS)
            return pltpu.make_async_copy(table_hbm.at[idx_v.at[pl.ds(off, SC_GATHER_ROWS)]],
                                         rows_v.at[b], gather_sem.at[b])

        def write(c, b):
            off = pl.multiple_of(c * SC_GATHER_ROWS, SC_GATHER_ROWS)
            return pltpu.make_async_copy(rows_v.at[b], out_hbm.at[pl.ds(base + off, SC_GATHER_ROWS)],
                                         write_sem.at[b])

        gather(0, 0).start()

        @pl.loop(0, n_chunks, step=2)
        def _(c0):
            for b in range(2):
                c = c0 + b
                gather(c, b).wait()

                @pl.when(c >= 1)
                def _():
                    write(c - 1, 1 - b).wait()

                @pl.when(c + 1 < n_chunks)
                def _():
                    gather(c + 1, 1 - b).start()

                write(c, b).start()

        write(n_chunks - 1, (n_chunks - 1) % 2).wait()

    return pl.kernel(
        body, mesh=_sc_mesh(),
        out_type=jax.ShapeDtypeStruct((n_idx, width), table.dtype),
        scratch_types=[pltpu.VMEM((per_worker,), I32),
                       pltpu.VMEM((2, SC_GATHER_ROWS, width), table.dtype),
                       pltpu.SemaphoreType.DMA((2,)),
                       pltpu.SemaphoreType.DMA((2,))],
        name="sc_gather_rows",
    )(table, idx)


def _combine_kernel(w_ref, h_ref, g2_ref, nf_ref, yk_ref, y_ref):
    t = h_ref.shape[0]
    w = w_ref[...]
    acc_lo = jnp.zeros((t, D_MODEL // 2), F32)
    acc_hi = jnp.zeros((t, D_MODEL // 2), F32)
    for kk in range(TOP_K):
        lo, hi = _unpack_bf16_pair(yk_ref[kk])
        wk = w[:, kk:kk + 1]
        acc_lo = acc_lo + wk * lo
        acc_hi = acc_hi + wk * hi
    out = h_ref[...] + g2_ref[0] * jnp.concatenate([acc_lo, acc_hi], axis=1)
    y_ref[...] = out * lax.rsqrt(jnp.mean(out * out, axis=-1, keepdims=True) + EPS) * nf_ref[...]


def _combine(w, h2, g2, normf, y_by_k, row_offset, tiles_per_batch):
    n = h2.shape[0]
    t = MOVE_TILE
    off = row_offset // t
    mod_rows = g2.shape[1]
    mod_tiles = max(ROW_TILE // t, 1) * tiles_per_batch if mod_rows == 1 else n // t
    return pl.pallas_call(
        _combine_kernel,
        grid=(n // t,),
        in_specs=[pl.BlockSpec((t, LANES), lambda i: (i + off, 0)),
                  pl.BlockSpec((t, D_MODEL), lambda i: (i, 0)),
                  pl.BlockSpec((1, mod_rows if mod_rows == 1 else t, D_MODEL),
                               (lambda i: (i // mod_tiles, 0, 0)) if mod_rows == 1
                               else (lambda i: (0, i, 0))),
                  pl.BlockSpec((1, D_MODEL), lambda i: (0, 0)),
                  pl.BlockSpec((TOP_K, t, D_MODEL // 2), lambda i: (0, i + off, 0))],
        out_specs=pl.BlockSpec((t, D_MODEL), lambda i: (i, 0)),
        out_shape=jax.ShapeDtypeStruct((n, D_MODEL), F32),
        compiler_params=_cparams(("arbitrary",), 40),
        name="combine",
    )(w, h2, g2, normf, y_by_k)


def _rotary_tables(pos):
    half = KEY_DIM_R // 2
    inv_freq = ROPE_BASE ** (-jnp.arange(half, dtype=F32) / half)
    ang = pos[:, None] * inv_freq[None, :]
    cos = jnp.cos(ang)
    sin = jnp.sin(ang)
    cos_t = jnp.concatenate([cos, cos], axis=1)
    sin_t = jnp.concatenate([-sin, sin], axis=1)
    return cos_t.astype(F32), sin_t.astype(F32)


def _rel_bias_table(rel_bias, n_rows, n_cols, q_offset):
    heads = rel_bias.shape[0]
    n_diag = n_rows + n_cols - 1
    dist = q_offset + (n_rows - 1) - np.arange(n_diag)
    idx = np.clip(dist, -REL_CLIP, REL_CLIP) + REL_CLIP
    n_hi = int(np.sum(dist > REL_CLIP))
    n_lo = int(np.sum(dist < -REL_CLIP))
    mid = rel_bias[:, int(idx[n_diag - n_lo - 1]):int(idx[n_hi]) + 1][:, ::-1]
    diag = jnp.concatenate([jnp.broadcast_to(rel_bias[:, 2 * REL_CLIP:], (heads, n_hi)), mid,
                            jnp.broadcast_to(rel_bias[:, :1], (heads, n_lo))], axis=1)
    period = n_diag + 1
    v = jnp.roll(jnp.pad(diag, ((0, 0), (0, 1))), -(n_rows - 1), axis=1)
    skew = jnp.tile(v, (1, n_rows))[:, :n_rows * (period - 1)].reshape(heads, n_rows, period - 1)
    return skew[:, :, :n_cols].astype(F32)


def _prompt_bias(rel_bias):
    n_cols = ATT_QB + ATT_WINDOW
    r = np.arange(ATT_QB)[:, None]
    c = np.arange(n_cols)[None, :]
    band = c - (r // CHUNK) * CHUNK
    valid = (band >= 0) & (band < ATT_WINDOW + CHUNK)
    table = _rel_bias_table(rel_bias, ATT_QB, n_cols, ATT_WINDOW)
    return jnp.where(jnp.asarray(valid)[None], table, NEG_BIG)


def _sample_bias(rel_bias, t_new, cache_len):
    b = _rel_bias_table(rel_bias, t_new, cache_len + t_new, cache_len)
    return b[:, :, :cache_len], b[:, :, cache_len:]


def _mod_parts(mod, rows_each):
    parts = jnp.split(mod, 6, axis=-1)
    if rows_each == 1:
        return [p[:, None, :] for p in parts]
    return [jnp.repeat(p, rows_each, axis=0)[None] for p in parts]


def kernel(x_prompt, x_sample, cache_attn_k, cache_attn_v, state_ret, c_prompt, c_sample,
           norm1_g, norm2_g, w_ada, b_ada, w_in, rel_bias, w_o_attn, w_o_ret, w_out,
           w_router, b_router, w_exp_gate, w_exp_up, w_exp_down, w_sh_gate, w_sh_up, w_sh_down,
           normf_g):
    batch, seq, d = x_prompt.shape
    dec_batch, dec_seq, _ = x_sample.shape
    depth = w_in.shape[0]
    assert depth == 1 and d == D_MODEL
    assert seq % ROW_TILE == 0 and dec_batch * dec_seq == ROW_TILE and ROW_TILE == ATT_WINDOW
    cache_len = cache_attn_k.shape[2]
    n_p = batch * seq
    n_s = dec_batch * dec_seq
    tpb = seq // ROW_TILE
    l = 0

    bf = lambda a: a.astype(BF16)
    c_all = jnp.concatenate([c_prompt, c_sample], axis=0)
    pad = (-c_all.shape[0]) % 8
    c_all = jnp.pad(c_all, ((0, pad), (0, 0)))
    mod = _ada(c_all, bf(w_ada[l]), b_ada[l][None, :])
    mod_p = _mod_parts(mod[:batch], 1)
    mod_s = _mod_parts(mod[batch:batch + dec_batch], dec_seq)

    w_in_bf = bf(w_in[l])
    n1g = norm1_g[l][None, :]
    n2g = norm2_g[l][None, :]
    dense_w = [bf(w_o_attn[l]), bf(w_o_ret[l]), bf(w_out[l]), bf(w_router[l]).T,
               bf(w_sh_gate[l]), bf(w_sh_up[l]), bf(w_sh_down[l])]

    xp = x_prompt.reshape(n_p, d)
    xs_ = x_sample.reshape(n_s, d)
    cos_p, sin_p = _rotary_tables(jnp.arange(seq, dtype=F32))
    pos_s = PAST_LEN + jnp.arange(dec_seq, dtype=F32)
    cos_s, sin_s = _rotary_tables(jnp.tile(pos_s, dec_batch))

    (qa, ka, va, qr, kr, vr, gr, ga, gb, kv_p) = _inproj(
        xp, mod_p[1], mod_p[0], n1g, cos_p, sin_p, w_in_bf, tpb)
    oa = _attn_prompt(qa, ka, va, _prompt_bias(rel_bias[l]), batch, seq)
    zero_state = jnp.zeros((batch, N_HEADS, KEY_DIM_R, VAL_DIM_R), F32)
    yr_in, state_p = _retention(qr, kr, vr, gr, zero_state, batch, seq, RET_CHUNK)
    h_p, n2p_p, s_p = _outproj(xp, oa, yr_in, ga, gb, mod_p[2], mod_p[4], mod_p[3], mod_p[5], n2g,
                               dense_w, tpb, n_p + n_s, 0)

    (qa_s, ka_s, va_s, qr_s, kr_s, vr_s, gr_s, ga_s, gb_s, kv_s) = _inproj(
        xs_, mod_s[1], mod_s[0], n1g, cos_s, sin_s, w_in_bf, 1)
    bias_c, bias_n = _sample_bias(rel_bias[l], dec_seq, cache_len)
    to_keys_minor = lambda c: jnp.transpose(c, (0, 1, 3, 4, 2))
    oa_s = _attn_sample(qa_s, ka_s, va_s, to_keys_minor(cache_attn_k), to_keys_minor(cache_attn_v),
                        bias_c, bias_n, dec_batch, dec_seq, cache_len)
    yr_in_s, state_s = _retention(qr_s, kr_s, vr_s, gr_s, state_ret[l], dec_batch, dec_seq, dec_seq)
    h_s, n2p, scores_t = _outproj(xs_, oa_s, yr_in_s, ga_s, gb_s, mod_s[2], mod_s[4], mod_s[3],
                                  mod_s[5], n2g, dense_w, 1, n_p + n_s, n_p, carried=(n2p_p, s_p))

    lanes_of = lambda v: jnp.broadcast_to(v[:, None], (N_EXPERTS, ROUTE_TILE))
    idx_t, w_t, rank_t, counts = _route(scores_t, lanes_of(b_router[l]))
    (tile_expert, n_tiles, expert_slot, expert_next, tile_halves, pstart, pad_rows,
     n_sorted_rows) = _ffn_plan(counts[:, 0], (n_p + n_s) * TOP_K)
    dest_t = _dest(idx_t, rank_t, lanes_of(pstart.astype(F32)))
    dest_kmajor = dest_t.reshape(-1)
    w_route = jnp.pad(w_t.T, ((0, 0), (0, LANES - TOP_K)))
    xs_sorted = _sc_dispatch(n2p, dest_kmajor, pad_rows, n_sorted_rows)
    ys_sorted = _ffn(tile_expert, n_tiles, expert_slot, expert_next, tile_halves, xs_sorted,
                     w_exp_gate[l], w_exp_up[l], w_exp_down[l])
    nf = normf_g[None, :]
    y_by_k = _sc_gather_rows(ys_sorted, dest_kmajor).reshape(TOP_K, n_p + n_s, d // 2)
    y_p = _combine(w_route, h_p, mod_p[5], nf, y_by_k, 0, tpb)
    y_s = _combine(w_route, h_s, mod_s[5], nf, y_by_k, n_p, 1)

    keep = min(ATT_WINDOW, seq)
    kv_p = kv_p.reshape(batch, ROW_TILE, 2, N_HEADS, HEAD_DIM_A)[:, ROW_TILE - keep:]
    kv_s = kv_s.reshape(dec_batch, dec_seq, 2, N_HEADS, HEAD_DIM_A)
    return (y_p.reshape(batch, seq, d), y_s.reshape(dec_batch, dec_seq, d),
            kv_p[:, :, 0][None], kv_p[:, :, 1][None], state_p[None],
            kv_s[:, :, 0][None], kv_s[:, :, 1][None], state_s[None])
```

```python
import functools

import numpy as np
import jax
import jax.numpy as jnp
from jax import lax
from jax.experimental import pallas as pl
from jax.experimental.pallas import tpu as pltpu
from jax.experimental.pallas import tpu_sc as plsc

F32 = jnp.float32
BF16 = jnp.bfloat16
I32 = jnp.int32
U32 = jnp.uint32

D_MODEL = 1024
PAST_LEN = 4096
CHUNK = 64
N_LEFT_CHUNKS = 8
ATT_WINDOW = N_LEFT_CHUNKS * CHUNK
N_HEADS = 8
HEAD_DIM_A = 64
D_ATT = N_HEADS * HEAD_DIM_A
REL_CLIP = 128
KEY_DIM_R = 64
VAL_DIM_R = 128
D_RET_K = N_HEADS * KEY_DIM_R
D_RET_V = N_HEADS * VAL_DIM_R
ROPE_BASE = 10000.0
N_EXPERTS = 256
TOP_K = 8
N_GROUPS = 8
GROUP_SIZE = N_EXPERTS // N_GROUPS
TOPK_GROUPS = 4
D_EXPERT = 256
ROUTED_SCALE = 2.5
EPS = 1e-6
IN_WIDTHS = (D_ATT, D_ATT, D_ATT, D_RET_K, D_RET_K, D_RET_V, D_RET_V, D_MODEL, D_MODEL)
IN_OFFS = tuple(int(v) for v in np.cumsum((0,) + IN_WIDTHS))
D_IN = IN_OFFS[-1]

NEG_BIG = -1e30
LANES = 128
V7X_VMEM_BYTES = 64 * 1024 * 1024
V7X_SC_CORES = 2
V7X_SC_SUBCORES = 16
SC_GATHER_ROWS = 96
SPARE_ROWS = 8192

ROW_TILE = 512
ATT_QB = 256
RET_CHUNK = 256
ROUTE_TILE = 512
MOVE_TILE = 256
FFN_QUANTUM = 128
FFN_FINE = 64
FFN_FINE_FROM = 384
FFN_TILE = 5 * FFN_QUANTUM
FFN_TILE_SIZES = tuple(n for n in range(1, FFN_TILE // FFN_FINE + 1)
                       if n * FFN_FINE > FFN_FINE_FROM or (n * FFN_FINE) % FFN_QUANTUM == 0)
FFN_WEIGHT_SLOTS = 4


def _cparams(semantics, vmem_mb):
    return pltpu.CompilerParams(dimension_semantics=semantics,
                                vmem_limit_bytes=min(vmem_mb * 1024 * 1024, V7X_VMEM_BYTES - (6 << 20)))


def _silu(x):
    return x * jax.nn.sigmoid(x)


def _pack_bf16_pair(lo, hi):
    return pltpu.bitcast(pltpu.pack_elementwise([lo, hi], packed_dtype=BF16), U32)


def _unpack_bf16_pair(u):
    words = pltpu.bitcast(u, I32)
    lo = pltpu.unpack_elementwise(words, index=0, packed_dtype=BF16, unpacked_dtype=F32)
    hi = pltpu.unpack_elementwise(words, index=1, packed_dtype=BF16, unpacked_dtype=F32)
    return lo, hi


def _ada_kernel(c_ref, w_ref, b_ref, o_ref):
    sc = _silu(c_ref[...]).astype(BF16)
    o_ref[...] = jnp.dot(sc, w_ref[...], preferred_element_type=F32) + b_ref[...]


def _ada(c_all, w_ada_bf, b_ada):
    rows = c_all.shape[0]
    n_out = w_ada_bf.shape[1]
    blk = D_MODEL
    return pl.pallas_call(
        _ada_kernel,
        grid=(n_out // blk,),
        in_specs=[pl.BlockSpec((rows, D_MODEL), lambda j: (0, 0)),
                  pl.BlockSpec((D_MODEL, blk), lambda j: (0, j)),
                  pl.BlockSpec((1, blk), lambda j: (0, j))],
        out_specs=pl.BlockSpec((rows, blk), lambda j: (0, j)),
        out_shape=jax.ShapeDtypeStruct((rows, n_out), F32),
        compiler_params=_cparams(("arbitrary",), 24),
        name="ada",
    )(c_all, w_ada_bf, b_ada)


def _inproj_kernel(x_ref, sc_ref, sh_ref, g_ref, cos_ref, sin_ref, w_ref,
                   qa_ref, ka_ref, va_ref, qr_ref, kr_ref, vr_ref, gr_ref, ga_ref, gb_ref,
                   kv_ref, *, tiles_per_batch):
    x = x_ref[...]
    xn = x * lax.rsqrt(jnp.mean(x * x, axis=-1, keepdims=True) + EPS) * g_ref[...]
    nb = (xn * (1.0 + sc_ref[0]) + sh_ref[0]).astype(BF16)

    def proj(seg):
        return jnp.dot(nb, w_ref[:, IN_OFFS[seg]:IN_OFFS[seg + 1]], preferred_element_type=F32)

    qa_ref[...] = proj(0).astype(BF16)
    ka = proj(1)
    va = proj(2)
    ka_ref[...] = ka.astype(BF16)
    va_ref[...] = va.astype(BF16)

    @pl.when(pl.program_id(0) % tiles_per_batch == tiles_per_batch - 1)
    def _():
        kv_ref[:, :D_ATT] = ka
        kv_ref[:, D_ATT:] = va

    cos = jnp.tile(cos_ref[...], (1, N_HEADS))
    sin = jnp.tile(sin_ref[...], (1, N_HEADS))
    first_half = (lax.broadcasted_iota(I32, (1, D_RET_K), 1) % KEY_DIM_R) < (KEY_DIM_R // 2)

    def rotary(t):
        partner = jnp.where(first_half, pltpu.roll(t, D_RET_K - KEY_DIM_R // 2, 1),
                            pltpu.roll(t, KEY_DIM_R // 2, 1))
        return t * cos + partner * sin

    qr_ref[...] = rotary(proj(3)).astype(BF16)
    kr_ref[...] = (rotary(proj(4)) * (KEY_DIM_R ** -0.5)).astype(BF16)
    vr_ref[...] = proj(5).astype(BF16)
    gr_ref[...] = proj(6).astype(BF16)
    ga_ref[...] = proj(7).astype(BF16)
    gb_ref[...] = proj(8).astype(BF16)


def _inproj(x2d, sc, sh, g, cos_t, sin_t, w_in_bf, tiles_per_batch):
    n = x2d.shape[0]
    tm = ROW_TILE
    n_tiles = n // tm
    n_batches = n_tiles // tiles_per_batch
    mod_rows = sc.shape[1]
    pos_tiles = cos_t.shape[0] // tm

    def row_spec(width):
        return pl.BlockSpec((tm, width), lambda i: (i, 0))

    mod_spec = pl.BlockSpec((1, mod_rows, D_MODEL), lambda i: (i // tiles_per_batch, 0, 0))
    pos_spec = pl.BlockSpec((tm, KEY_DIM_R), lambda i: (i % pos_tiles, 0))
    out_widths = (D_ATT, D_ATT, D_ATT, D_RET_K, D_RET_K, D_RET_V, D_RET_V, D_MODEL, D_MODEL)
    out_shape = [jax.ShapeDtypeStruct((n, w), BF16) for w in out_widths]
    out_shape.append(jax.ShapeDtypeStruct((n_batches * tm, 2 * D_ATT), F32))
    out_specs = [row_spec(w) for w in out_widths]
    out_specs.append(pl.BlockSpec((tm, 2 * D_ATT), lambda i: (i // tiles_per_batch, 0)))
    return pl.pallas_call(
        functools.partial(_inproj_kernel, tiles_per_batch=tiles_per_batch),
        grid=(n_tiles,),
        in_specs=[row_spec(D_MODEL), mod_spec, mod_spec,
                  pl.BlockSpec((1, D_MODEL), lambda i: (0, 0)),
                  pos_spec, pos_spec,
                  pl.BlockSpec((D_MODEL, D_IN), lambda i: (0, 0))],
        out_specs=out_specs,
        out_shape=out_shape,
        compiler_params=_cparams(("arbitrary",), 56),
        name="inproj",
    )(x2d, sc, sh, g, cos_t, sin_t, w_in_bf)


def _softmax_pv(s, v_parts):
    m = functools.reduce(jnp.maximum, [jnp.max(t, axis=-1, keepdims=True) for t in s])
    ps = [jnp.exp(t - m) for t in s]
    l = functools.reduce(jnp.add, [jnp.sum(p, axis=-1, keepdims=True) for p in ps])
    o = functools.reduce(jnp.add, [jnp.dot(p.astype(BF16), v, preferred_element_type=F32)
                                   for p, v in zip(ps, v_parts)])
    return o / l


def _attn_prompt_kernel(q_ref, k0_ref, k1_ref, k2_ref, v0_ref, v1_ref, v2_ref, bias_ref, o_ref):
    j = pl.program_id(1)
    q = q_ref[...]
    k = jnp.concatenate([k0_ref[...], k1_ref[...], k2_ref[...]], axis=0)
    v = jnp.concatenate([v0_ref[...], v1_ref[...], v2_ref[...]], axis=0)
    n_keys = k.shape[0]
    key_block = lax.broadcasted_iota(I32, (1, n_keys), 1) // ATT_QB
    before_start = jnp.where(key_block < 2 - j, NEG_BIG, 0.0)
    outs = []
    for h in range(N_HEADS):
        sl = slice(h * HEAD_DIM_A, (h + 1) * HEAD_DIM_A)
        qh = (q[:, sl].astype(F32) * (HEAD_DIM_A ** -0.5)).astype(BF16)
        s = lax.dot_general(qh, k[:, sl], (((1,), (1,)), ((), ())), preferred_element_type=F32)
        s = s + bias_ref[h] + before_start
        outs.append(_softmax_pv([s], [v[:, sl]]))
    o_ref[...] = jnp.concatenate(outs, axis=1).astype(BF16)


def _attn_prompt(q, k, v, bias_full, batch, seq):
    qb = ATT_QB
    nq = seq // qb

    def q_map(b, j):
        return (b * nq + j, 0)

    def kv_map(back):
        return lambda b, j: (b * nq + jnp.maximum(j - back, 0), 0)

    blk = lambda m: pl.BlockSpec((qb, D_ATT), m)
    return pl.pallas_call(
        _attn_prompt_kernel,
        grid=(batch, nq),
        in_specs=[blk(q_map), blk(kv_map(2)), blk(kv_map(1)), blk(kv_map(0)),
                  blk(kv_map(2)), blk(kv_map(1)), blk(kv_map(0)),
                  pl.BlockSpec(bias_full.shape, lambda b, j: (0, 0, 0))],
        out_specs=blk(q_map),
        out_shape=jax.ShapeDtypeStruct((batch * seq, D_ATT), BF16),
        compiler_params=_cparams(("parallel", "arbitrary"), 40),
        name="attn_prompt",
    )(q, k, k, k, v, v, v, bias_full)


SAMPLE_ATT_BATCHES = 2


def _attn_sample_kernel(q_ref, kn_ref, vn_ref, ck_ref, cv_ref, bc_ref, bn_ref, o_ref, *, t_new):
    nt = (((1,), (1,)), ((), ()))
    for b in range(SAMPLE_ATT_BATCHES):
        rows = slice(b * t_new, (b + 1) * t_new)
        q = q_ref[rows, :]
        kn = kn_ref[rows, :]
        vn = vn_ref[rows, :]
        outs = []
        for h in range(N_HEADS):
            sl = slice(h * HEAD_DIM_A, (h + 1) * HEAD_DIM_A)
            qh = (q[:, sl].astype(F32) * (HEAD_DIM_A ** -0.5)).astype(BF16)
            kc_t = ck_ref[b, h].astype(BF16)
            vc_t = cv_ref[b, h].astype(BF16)
            s_c = jnp.dot(qh, kc_t, preferred_element_type=F32) + bc_ref[h]
            s_n = lax.dot_general(qh, kn[:, sl], nt, preferred_element_type=F32) + bn_ref[h]
            m = jnp.maximum(jnp.max(s_c, axis=-1, keepdims=True), jnp.max(s_n, axis=-1, keepdims=True))
            p_c = jnp.exp(s_c - m)
            p_n = jnp.exp(s_n - m)
            l = jnp.sum(p_c, axis=-1, keepdims=True) + jnp.sum(p_n, axis=-1, keepdims=True)
            o = (lax.dot_general(p_c.astype(BF16), vc_t, nt, preferred_element_type=F32)
                 + jnp.dot(p_n.astype(BF16), vn[:, sl], preferred_element_type=F32))
            outs.append(o / l)
        o_ref[rows, :] = jnp.concatenate(outs, axis=1).astype(BF16)


def _attn_sample(q, k, v, cache_k_t, cache_v_t, bias_cache, bias_new, batch, t_new, cache_len):
    nb = SAMPLE_ATT_BATCHES
    blk = pl.BlockSpec((nb * t_new, D_ATT), lambda b: (b, 0))
    cblk = pl.BlockSpec((None, nb, N_HEADS, HEAD_DIM_A, cache_len), lambda b: (0, b, 0, 0, 0))
    return pl.pallas_call(
        functools.partial(_attn_sample_kernel, t_new=t_new),
        grid=(batch // nb,),
        in_specs=[blk, blk, blk, cblk, cblk,
                  pl.BlockSpec(bias_cache.shape, lambda b: (0, 0, 0)),
                  pl.BlockSpec(bias_new.shape, lambda b: (0, 0, 0))],
        out_specs=blk,
        out_shape=jax.ShapeDtypeStruct((batch * t_new, D_ATT), BF16),
        compiler_params=_cparams(("arbitrary",), 40),
        name="attn_sample",
    )(q, k, v, cache_k_t, cache_v_t, bias_cache, bias_new)


def _ret_kernel(q_ref, k_ref, v_ref, g_ref, s0_ref, dmask_ref, qdec_ref, kdec_ref, sdec_ref,
                y_ref, sout_ref, state_ref):
    c = pl.program_id(1)

    @pl.when(c == 0)
    def _():
        state_ref[...] = s0_ref[0]

    q = q_ref[...]
    k = k_ref[...]
    v = v_ref[...]
    g = g_ref[...]
    outs = []
    for h in range(N_HEADS):
        ks = slice(h * KEY_DIM_R, (h + 1) * KEY_DIM_R)
        vs = slice(h * VAL_DIM_R, (h + 1) * VAL_DIM_R)
        qh, kh, vh = q[:, ks], k[:, ks], v[:, vs]
        scores = lax.dot_general(qh, kh, (((1,), (1,)), ((), ())), preferred_element_type=F32)
        inner = jnp.dot((scores * dmask_ref[h]).astype(BF16), vh, preferred_element_type=F32)
        state = state_ref[h]
        cross = jnp.dot(qh, state.astype(BF16), preferred_element_type=F32) * qdec_ref[h]
        o = inner + cross
        v_dec = (vh.astype(F32) * kdec_ref[h]).astype(BF16)
        state_ref[h] = sdec_ref[h] * state + lax.dot_general(
            kh, v_dec, (((0,), (0,)), ((), ())), preferred_element_type=F32)
        on = o * lax.rsqrt(jnp.mean(o * o, axis=-1, keepdims=True) + EPS)
        outs.append(on * _silu(g[:, vs].astype(F32)))
    y_ref[...] = jnp.concatenate(outs, axis=1).astype(BF16)

    @pl.when(c == pl.num_programs(1) - 1)
    def _():
        sout_ref[0] = state_ref[...]


def _ret_tables(chunk):
    log_g = jnp.log(1.0 - jnp.exp2(-5.0 - jnp.arange(N_HEADS, dtype=F32)))
    i = jnp.arange(chunk, dtype=F32)
    diff = i[:, None] - i[None, :]
    dmask = jnp.where(diff >= 0, jnp.exp(log_g[:, None, None] * jnp.maximum(diff, 0.0)), 0.0)
    qdec = jnp.exp(log_g[:, None] * (i + 1.0))
    kdec = jnp.exp(log_g[:, None] * (chunk - 1.0 - i))
    sdec = jnp.exp(log_g * chunk)
    bc = lambda t: jnp.broadcast_to(t[:, :, None], (N_HEADS, t.shape[1], VAL_DIM_R)).astype(F32)
    sdec_t = jnp.broadcast_to(sdec[:, None, None], (N_HEADS, 1, VAL_DIM_R)).astype(F32)
    return dmask.astype(F32), bc(qdec), bc(kdec), sdec_t


def _retention(q, k, v, gate, state0, batch, seq, chunk):
    nc = seq // chunk
    dmask, qdec, kdec, sdec = _ret_tables(chunk)
    row = lambda w: pl.BlockSpec((chunk, w), lambda b, c: (b * nc + c, 0))
    const = lambda a: pl.BlockSpec(a.shape, lambda b, c: (0,) * a.ndim)
    st_spec = pl.BlockSpec((1, N_HEADS, KEY_DIM_R, VAL_DIM_R), lambda b, c: (b, 0, 0, 0))
    return pl.pallas_call(
        _ret_kernel,
        grid=(batch, nc),
        in_specs=[row(D_RET_K), row(D_RET_K), row(D_RET_V), row(D_RET_V), st_spec,
                  const(dmask), const(qdec), const(kdec), const(sdec)],
        out_specs=[row(D_RET_V), st_spec],
        out_shape=[jax.ShapeDtypeStruct((batch * seq, D_RET_V), BF16),
                   jax.ShapeDtypeStruct((batch, N_HEADS, KEY_DIM_R, VAL_DIM_R), F32)],
        scratch_shapes=[pltpu.VMEM((N_HEADS, KEY_DIM_R, VAL_DIM_R), F32)],
        compiler_params=_cparams(("parallel", "arbitrary"), 32),
        name="retention",
    )(q, k, v, gate, state0, dmask, qdec, kdec, sdec)


def _outproj_kernel(x_ref, oa_ref, yr_ref, ga_ref, gb_ref, g1_ref, sc2_ref, sh2_ref, g2_ref, n2g_ref,
                    woa_ref, wor_ref, wout_ref, wrt_ref, wsg_ref, wsu_ref, wsd_ref, *rest):
    h_ref, n2p_ref, s_ref = rest[-3:]
    ya = jnp.dot(oa_ref[...], woa_ref[...], preferred_element_type=F32)
    yr = jnp.dot(yr_ref[...], wor_ref[...], preferred_element_type=F32)
    merged = (jax.nn.sigmoid(ga_ref[...].astype(F32)) * ya
              + jax.nn.sigmoid(gb_ref[...].astype(F32)) * yr)
    mix = jnp.dot(merged.astype(BF16), wout_ref[...], preferred_element_type=F32)
    h = x_ref[...] + g1_ref[0] * mix
    hn = h * lax.rsqrt(jnp.mean(h * h, axis=-1, keepdims=True) + EPS) * n2g_ref[...]
    n2 = hn * (1.0 + sc2_ref[0]) + sh2_ref[0]
    n2b = n2.astype(BF16)
    s_ref[...] = jax.nn.sigmoid(lax.dot_general(wrt_ref[...], n2b, (((1,), (1,)), ((), ())),
                                                preferred_element_type=F32))
    hid = _silu(jnp.dot(n2b, wsg_ref[...], preferred_element_type=F32)) * jnp.dot(
        n2b, wsu_ref[...], preferred_element_type=F32)
    shared = jnp.dot(hid.astype(BF16), wsd_ref[...], preferred_element_type=F32)
    h_ref[...] = h + g2_ref[0] * shared
    half = D_MODEL // 2
    n2p_ref[...] = _pack_bf16_pair(n2[:, :half], n2[:, half:])


def _outproj(x2d, oa, yr_in, ga, gb, g1, sc2, sh2, g2, n2g, weights, tiles_per_batch,
             all_tokens, token_offset, carried=None):
    n = x2d.shape[0]
    tm = ROW_TILE
    off = token_offset // tm
    mod_rows = g1.shape[1]
    row = lambda w: pl.BlockSpec((tm, w), lambda i: (i, 0))
    mod_spec = pl.BlockSpec((1, mod_rows, D_MODEL), lambda i: (i // tiles_per_batch, 0, 0))
    const = lambda a: pl.BlockSpec(a.shape, lambda i: (0,) * a.ndim)
    in_specs = [row(D_MODEL), row(D_ATT), row(D_RET_V), row(D_MODEL), row(D_MODEL),
                mod_spec, mod_spec, mod_spec, mod_spec, const(n2g)] + [const(w) for w in weights]
    args = [x2d, oa, yr_in, ga, gb, g1, sc2, sh2, g2, n2g, *weights]
    aliases = {}
    if carried is not None:
        aliases = {len(args): 1, len(args) + 1: 2}
        in_specs += [pl.BlockSpec(memory_space=pl.ANY)] * 2
        args += list(carried)
    return pl.pallas_call(
        _outproj_kernel,
        grid=(n // tm,),
        in_specs=in_specs,
        out_specs=[row(D_MODEL),
                   pl.BlockSpec((tm, D_MODEL // 2), lambda i: (i + off, 0)),
                   pl.BlockSpec((N_EXPERTS, tm), lambda i: (0, i + off))],
        out_shape=[jax.ShapeDtypeStruct((n, D_MODEL), F32),
                   jax.ShapeDtypeStruct((all_tokens, D_MODEL // 2), U32),
                   jax.ShapeDtypeStruct((N_EXPERTS, all_tokens), F32)],
        input_output_aliases=aliases,
        compiler_params=_cparams(("arbitrary",), 48),
        name="outproj",
    )(*args)


def _route_kernel(s_ref, b_ref, idx_ref, w_ref, rank_ref, cnt_ref, run_ref, tri_ref):
    step = pl.program_id(0)
    t = s_ref.shape[1]

    @pl.when(step == 0)
    def _():
        run_ref[...] = jnp.zeros_like(run_ref)
        r = lax.broadcasted_iota(I32, (t, t), 0)
        c = lax.broadcasted_iota(I32, (t, t), 1)
        tri_ref[...] = jnp.where(r < c, 1.0, 0.0).astype(BF16)

    s = s_ref[...]
    sel = s + b_ref[...]
    row_f = lax.broadcasted_iota(I32, (N_EXPERTS, t), 0).astype(F32)

    def first_argmax(vals, rows):
        m = jnp.max(vals, axis=0, keepdims=True)
        pos = jnp.min(jnp.where(vals == m, rows, float(N_EXPERTS)), axis=0, keepdims=True)
        return m, pos

    gscore = []
    group_row = lax.broadcasted_iota(I32, (GROUP_SIZE, t), 0).astype(F32)
    for g in range(N_GROUPS):
        rows = slice(g * GROUP_SIZE, (g + 1) * GROUP_SIZE)
        m1, p1 = first_argmax(sel[rows], group_row)
        m2 = jnp.max(jnp.where(group_row == p1, -jnp.inf, sel[rows]), axis=0, keepdims=True)
        gscore.append(m1 + m2)
    cand_parts = []
    for g in range(N_GROUPS):
        rows = slice(g * GROUP_SIZE, (g + 1) * GROUP_SIZE)
        beaten_by = jnp.zeros((1, t), F32)
        for o in range(N_GROUPS):
            if o == g:
                continue
            wins = (gscore[o] > gscore[g]) if o > g else (gscore[o] >= gscore[g])
            beaten_by = beaten_by + jnp.where(wins, 1.0, 0.0)
        cand_parts.append(jnp.where(beaten_by < TOPK_GROUPS, sel[rows], -jnp.inf))
    cand = jnp.concatenate(cand_parts, axis=0)

    picked = jnp.zeros((N_EXPERTS, t), F32)
    idx_rows, w_rows = [], []
    for _ in range(TOP_K):
        _, pos = first_argmax(cand, row_f)
        hit = row_f == pos
        w_rows.append(jnp.sum(jnp.where(hit, s, 0.0), axis=0, keepdims=True))
        idx_rows.append(pos)
        picked = jnp.where(hit, 1.0, picked)
        cand = jnp.where(hit, -jnp.inf, cand)
    w_sum = functools.reduce(jnp.add, w_rows)

    before = jnp.dot(picked.astype(BF16), tri_ref[...], preferred_element_type=F32) + run_ref[...]
    run_ref[...] = run_ref[...] + jnp.sum(picked, axis=1, keepdims=True)
    rank_rows = [jnp.sum(jnp.where(row_f == idx_rows[kk], before, 0.0), axis=0, keepdims=True)
                 for kk in range(TOP_K)]

    idx_ref[...] = jnp.concatenate(idx_rows, axis=0).astype(I32)
    w_ref[...] = jnp.concatenate([w / w_sum * ROUTED_SCALE for w in w_rows], axis=0)
    rank_ref[...] = jnp.concatenate(rank_rows, axis=0).astype(I32)

    @pl.when(step == pl.num_programs(0) - 1)
    def _():
        cnt_ref[...] = run_ref[...].astype(I32)


def _route(scores_t, b_col):
    n = scores_t.shape[1]
    t = ROUTE_TILE
    col = pl.BlockSpec((TOP_K, t), lambda i: (0, i))
    const = pl.BlockSpec((N_EXPERTS, t), lambda i: (0, 0))
    return pl.pallas_call(
        _route_kernel,
        grid=(n // t,),
        in_specs=[pl.BlockSpec((N_EXPERTS, t), lambda i: (0, i)), const],
        out_specs=[col, col, col, const],
        out_shape=[jax.ShapeDtypeStruct((TOP_K, n), I32),
                   jax.ShapeDtypeStruct((TOP_K, n), F32),
                   jax.ShapeDtypeStruct((TOP_K, n), I32),
                   jax.ShapeDtypeStruct((N_EXPERTS, t), I32)],
        scratch_shapes=[pltpu.VMEM((N_EXPERTS, t), F32), pltpu.VMEM((t, t), BF16)],
        compiler_params=_cparams(("arbitrary",), 32),
        name="route",
    )(scores_t, b_col)


def _dest_kernel(idx_ref, rank_ref, start_ref, dest_ref):
    t = idx_ref.shape[1]
    row = lax.broadcasted_iota(I32, (N_EXPERTS, t), 0)
    starts = start_ref[...]
    base = [jnp.sum(jnp.where(row == idx_ref[kk:kk + 1, :], starts, 0.0), axis=0, keepdims=True)
            for kk in range(TOP_K)]
    dest_ref[...] = jnp.concatenate(base, axis=0).astype(I32) + rank_ref[...]


def _dest(idx_t, rank_t, starts_col):
    n = idx_t.shape[1]
    t = ROUTE_TILE
    col = pl.BlockSpec((TOP_K, t), lambda i: (0, i))
    return pl.pallas_call(
        _dest_kernel,
        grid=(n // t,),
        in_specs=[col, col, pl.BlockSpec((N_EXPERTS, t), lambda i: (0, 0))],
        out_specs=col,
        out_shape=jax.ShapeDtypeStruct((TOP_K, n), I32),
        compiler_params=_cparams(("arbitrary",), 32),
        name="dest",
    )(idx_t, rank_t, starts_col)


def _sc_mesh():
    return plsc.VectorSubcoreMesh(core_axis_name="core", subcore_axis_name="subcore",
                                  num_cores=V7X_SC_CORES, num_subcores=V7X_SC_SUBCORES)


def _sc_dispatch(n2p, dest_kmajor, pad_rows, n_out_rows):
    n, width = n2p.shape
    rows = SC_GATHER_ROWS
    n_workers = V7X_SC_CORES * V7X_SC_SUBCORES
    src_chunks = n // rows
    items = dest_kmajor.shape[0] // rows
    per_worker = items // n_workers
    pad_per_worker = pad_rows.shape[0] // rows // n_workers
    assert src_chunks * rows == n and per_worker * n_workers == items and per_worker % 2 == 0
    assert pad_per_worker * n_workers * rows == pad_rows.shape[0]
    idx3 = dest_kmajor.reshape(n_workers, per_worker, rows)
    pad3 = pad_rows.reshape(n_workers, pad_per_worker, rows)
    zeros = jnp.zeros((rows, width), n2p.dtype)

    def body(src_hbm, idx_hbm, pad_hbm, zero_hbm, out_hbm, idx_v, pad_v, rows_v, load_sem, scat_sem):
        worker = lax.axis_index("subcore") * V7X_SC_CORES + lax.axis_index("core")
        pltpu.sync_copy(idx_hbm.at[worker], idx_v)
        pltpu.sync_copy(pad_hbm.at[worker], pad_v)
        pltpu.sync_copy(zero_hbm, rows_v.at[0])

        def zero_fill(c):
            return pltpu.make_async_copy(rows_v.at[0], out_hbm.at[pad_v.at[c]], scat_sem.at[0])

        @pl.loop(0, pad_per_worker)
        def _(c):
            zero_fill(c).start()

        @pl.loop(0, pad_per_worker)
        def _(c):
            zero_fill(c).wait()

        def load(c, b):
            chunk = lax.rem(worker * per_worker + c, src_chunks)
            off = pl.multiple_of(chunk * rows, rows)
            return pltpu.make_async_copy(src_hbm.at[pl.ds(off, rows)], rows_v.at[b], load_sem.at[b])

        def scatter(c, b):
            return pltpu.make_async_copy(rows_v.at[b], out_hbm.at[idx_v.at[c]], scat_sem.at[b])

        load(0, 0).start()

        @pl.loop(0, per_worker, step=2)
        def _(c0):
            for b in range(2):
                c = c0 + b
                load(c, b).wait()

                @pl.when(c >= 1)
                def _():
                    scatter(c - 1, 1 - b).wait()

                @pl.when(c + 1 < per_worker)
                def _():
                    load(c + 1, 1 - b).start()

                scatter(c, b).start()

        scatter(per_worker - 1, (per_worker - 1) % 2).wait()

    return pl.kernel(
        body, mesh=_sc_mesh(),
        out_type=jax.ShapeDtypeStruct((n_out_rows, width), n2p.dtype),
        scratch_types=[pltpu.VMEM((per_worker, rows), I32),
                       pltpu.VMEM((pad_per_worker, rows), I32),
                       pltpu.VMEM((2, rows, width), n2p.dtype),
                       pltpu.SemaphoreType.DMA((2,)),
                       pltpu.SemaphoreType.DMA((2,))],
        name="sc_dispatch",
    )(n2p, idx3, pad3, zeros)


def _ffn_kernel(texp_ref, ntiles_ref, eslot_ref, enext_ref, nhalf_ref, xs_ref, wg_hbm, wu_hbm, wd_hbm,
                ys_ref, wg_buf, wu_buf, wd_buf, wgu_bf, wd_bf, sems):
    g = pl.program_id(0)

    def weight_copies(e, slot):
        return (pltpu.make_async_copy(wg_hbm.at[e], wg_buf.at[slot], sems.at[slot, 0]),
                pltpu.make_async_copy(wu_hbm.at[e], wu_buf.at[slot], sems.at[slot, 1]),
                pltpu.make_async_copy(wd_hbm.at[e], wd_buf.at[slot], sems.at[slot, 2]))

    @pl.when(g < ntiles_ref[0])
    def _():
        e = texp_ref[g]
        changed = jnp.logical_or(g == 0, texp_ref[jnp.maximum(g - 1, 0)] != e)

        @pl.when(changed)
        def _():
            slot = eslot_ref[e]

            def fetch_ahead(first_hop, hops, target_slot):
                ahead = first_hop
                for _ in range(hops - 1):
                    ahead = jnp.where(ahead >= 0, enext_ref[jnp.maximum(ahead, 0)], -1)

                @pl.when(ahead >= 0)
                def _():
                    for c in weight_copies(ahead, target_slot):
                        c.start()

            @pl.when(g == 0)
            def _():
                for c in weight_copies(e, slot):
                    c.start()
                for hops in range(1, FFN_WEIGHT_SLOTS - 1):
                    fetch_ahead(enext_ref[e], hops, lax.rem(slot + hops, FFN_WEIGHT_SLOTS))

            fetch_ahead(enext_ref[e], FFN_WEIGHT_SLOTS - 1,
                        lax.rem(slot + FFN_WEIGHT_SLOTS - 1, FFN_WEIGHT_SLOTS))
            for c in weight_copies(e, slot):
                c.wait()

            wgu_bf[:, :D_EXPERT] = wg_buf[slot].astype(BF16)
            wgu_bf[:, D_EXPERT:] = wu_buf[slot].astype(BF16)
            wd_bf[...] = wd_buf[slot].astype(BF16)

        def expert_rows(rows):
            lo, hi = _unpack_bf16_pair(xs_ref[rows, :])
            x = jnp.concatenate([lo, hi], axis=1).astype(BF16)
            gu = jnp.dot(x, wgu_bf[...], preferred_element_type=F32)
            hid = (_silu(gu[:, :D_EXPERT]) * gu[:, D_EXPERT:]).astype(BF16)
            y = jnp.dot(hid, wd_bf[...], preferred_element_type=F32)
            half = D_MODEL // 2
            ys_ref[rows, :] = _pack_bf16_pair(y[:, :half], y[:, half:])

        for n_groups in FFN_TILE_SIZES:

            @pl.when(nhalf_ref[g] == n_groups)
            def _(n_groups=n_groups):
                used = n_groups * FFN_FINE
                expert_rows(slice(0, used))
                if used < FFN_TILE:
                    ys_ref[used:, :] = jnp.zeros((FFN_TILE - used, ys_ref.shape[1]), U32)


def _ffn(tile_expert, n_tiles, expert_slot, expert_next, tile_halves, xs, w_gate, w_up, w_down):
    rows, width = xs.shape
    m = FFN_TILE
    max_tiles = tile_expert.shape[0]
    row_map = lambda g, te, nt, es, en, nh: (jnp.minimum(g, nt[0] - 1), 0)
    hbm = pl.BlockSpec(memory_space=pl.ANY)
    grid_spec = pltpu.PrefetchScalarGridSpec(
        num_scalar_prefetch=5,
        grid=(max_tiles,),
        in_specs=[pl.BlockSpec((m, width), row_map), hbm, hbm, hbm],
        out_specs=pl.BlockSpec((m, width), row_map),
        scratch_shapes=[pltpu.VMEM((FFN_WEIGHT_SLOTS, D_MODEL, D_EXPERT), F32),
                        pltpu.VMEM((FFN_WEIGHT_SLOTS, D_MODEL, D_EXPERT), F32),
                        pltpu.VMEM((FFN_WEIGHT_SLOTS, D_EXPERT, D_MODEL), F32),
                        pltpu.VMEM((D_MODEL, 2 * D_EXPERT), BF16),
                        pltpu.VMEM((D_EXPERT, D_MODEL), BF16),
                        pltpu.SemaphoreType.DMA((FFN_WEIGHT_SLOTS, 3))],
    )
    return pl.pallas_call(
        _ffn_kernel,
        grid_spec=grid_spec,
        out_shape=jax.ShapeDtypeStruct((rows, width), U32),
        compiler_params=_cparams(("arbitrary",), 32),
        name="ffn",
    )(tile_expert, n_tiles, expert_slot, expert_next, tile_halves, xs, w_gate, w_up, w_down)


def _ffn_plan(counts, n_assign):
    m = FFN_TILE
    max_tiles = n_assign // m + N_EXPERTS
    padded = ((counts + m - 1) // m) * m
    pend = jnp.cumsum(padded).astype(I32)
    pstart = pend - padded
    n_tiles = pend[-1:] // m
    g = jnp.minimum(jnp.arange(max_tiles, dtype=I32), n_tiles - 1)
    tile_expert = jnp.sum((pend[None, :] <= (g * m)[:, None]).astype(I32), axis=1)
    tile_expert = jnp.minimum(tile_expert, N_EXPERTS - 1)
    full_rows = jnp.maximum(padded - m, 0)
    last_rows = counts - full_rows
    round_to = lambda v, q: ((v + q - 1) // q) * q
    last_rows = jnp.where(last_rows > FFN_FINE_FROM, round_to(last_rows, FFN_FINE),
                          round_to(last_rows, FFN_QUANTUM))
    vend = pstart + full_rows + last_rows
    own = tile_expert[:, None] == jnp.arange(N_EXPERTS, dtype=I32)[None, :]
    tile_vend = jnp.sum(jnp.where(own, vend[None, :], 0), axis=1)
    tile_halves = (jnp.clip(tile_vend - g * m, 0, m) // FFN_FINE).astype(I32)
    used = counts > 0
    expert_slot = ((jnp.cumsum(used.astype(I32)) - 1) % FFN_WEIGHT_SLOTS).astype(I32)
    ids = jnp.where(used, jnp.arange(N_EXPERTS, dtype=I32), N_EXPERTS)
    first_used_from = lax.cummin(ids, axis=0, reverse=True)
    nxt = jnp.concatenate([first_used_from[1:], jnp.full((1,), N_EXPERTS, I32)])
    expert_next = jnp.where(nxt < N_EXPERTS, nxt, -1).astype(I32)
    spare_row = max_tiles * m
    j = jnp.arange(FFN_QUANTUM, dtype=I32)[None, :]
    seg_end = (pstart + counts)[:, None]
    spare = spare_row + (jnp.arange(N_EXPERTS, dtype=I32)[:, None] * FFN_QUANTUM + j) % SPARE_ROWS
    pad_rows = jnp.where(j < (vend[:, None] - seg_end), seg_end + j, spare).astype(I32).reshape(-1)
    unit = SC_GATHER_ROWS * V7X_SC_CORES * V7X_SC_SUBCORES
    extra = (-pad_rows.shape[0]) % unit
    filler = spare_row + (pad_rows.shape[0] + jnp.arange(extra, dtype=I32)) % SPARE_ROWS
    pad_rows = jnp.concatenate([pad_rows, filler])
    return (tile_expert, n_tiles, expert_slot, expert_next, tile_halves, pstart, pad_rows,
            spare_row + SPARE_ROWS)


def _sc_gather_rows(table, idx):
    n_idx = idx.shape[0]
    width = table.shape[1]
    n_workers = V7X_SC_CORES * V7X_SC_SUBCORES
    per_worker = n_idx // n_workers
    n_chunks = per_worker // SC_GATHER_ROWS
    assert per_worker * n_workers == n_idx and n_chunks * SC_GATHER_ROWS == per_worker and n_chunks % 2 == 0

    def body(table_hbm, idx_hbm, out_hbm, idx_v, rows_v, gather_sem, write_sem):
        worker = lax.axis_index("subcore") * V7X_SC_CORES + lax.axis_index("core")
        base = worker * per_worker
        pltpu.sync_copy(idx_hbm.at[pl.ds(base, per_worker)], idx_v)

        def gather(c, b):
            off = pl.multiple_of(c * SC_GATHER_ROWS, SC_GATHER_ROWS)
            return pltpu.make_async_copy(table_hbm.at[idx_v.at[pl.ds(off, SC_GATHER_ROWS)]],
                                         rows_v.at[b], gather_sem.at[b])

        def write(c, b):
            off = pl.multiple_of(c * SC_GATHER_ROWS, SC_GATHER_ROWS)
            return pltpu.make_async_copy(rows_v.at[b], out_hbm.at[pl.ds(base + off, SC_GATHER_ROWS)],
                                         write_sem.at[b])

        gather(0, 0).start()

        @pl.loop(0, n_chunks, step=2)
        def _(c0):
            for b in range(2):
                c = c0 + b
                gather(c, b).wait()

                @pl.when(c >= 1)
                def _():
                    write(c - 1, 1 - b).wait()

                @pl.when(c + 1 < n_chunks)
                def _():
                    gather(c + 1, 1 - b).start()

                write(c, b).start()

        write(n_chunks - 1, (n_chunks - 1) % 2).wait()

    return pl.kernel(
        body, mesh=_sc_mesh(),
        out_type=jax.ShapeDtypeStruct((n_idx, width), table.dtype),
        scratch_types=[pltpu.VMEM((per_worker,), I32),
                       pltpu.VMEM((2, SC_GATHER_ROWS, width), table.dtype),
                       pltpu.SemaphoreType.DMA((2,)),
                       pltpu.SemaphoreType.DMA((2,))],
        name="sc_gather_rows",
    )(table, idx)


def _combine_kernel(w_ref, h_ref, g2_ref, nf_ref, yk_ref, y_ref):
    t = h_ref.shape[0]
    w = w_ref[...]
    acc_lo = jnp.zeros((t, D_MODEL // 2), F32)
    acc_hi = jnp.zeros((t, D_MODEL // 2), F32)
    for kk in range(TOP_K):
        lo, hi = _unpack_bf16_pair(yk_ref[kk])
        wk = w[:, kk:kk + 1]
        acc_lo = acc_lo + wk * lo
        acc_hi = acc_hi + wk * hi
    out = h_ref[...] + g2_ref[0] * jnp.concatenate([acc_lo, acc_hi], axis=1)
    y_ref[...] = out * lax.rsqrt(jnp.mean(out * out, axis=-1, keepdims=True) + EPS) * nf_ref[...]


def _combine(w, h2, g2, normf, y_by_k, row_offset, tiles_per_batch):
    n = h2.shape[0]
    t = MOVE_TILE
    off = row_offset // t
    mod_rows = g2.shape[1]
    mod_tiles = max(ROW_TILE // t, 1) * tiles_per_batch if mod_rows == 1 else n // t
    return pl.pallas_call(
        _combine_kernel,
        grid=(n // t,),
        in_specs=[pl.BlockSpec((t, LANES), lambda i: (i + off, 0)),
                  pl.BlockSpec((t, D_MODEL), lambda i: (i, 0)),
                  pl.BlockSpec((1, mod_rows if mod_rows == 1 else t, D_MODEL),
                               (lambda i: (i // mod_tiles, 0, 0)) if mod_rows == 1
                               else (lambda i: (0, i, 0))),
                  pl.BlockSpec((1, D_MODEL), lambda i: (0, 0)),
                  pl.BlockSpec((TOP_K, t, D_MODEL // 2), lambda i: (0, i + off, 0))],
        out_specs=pl.BlockSpec((t, D_MODEL), lambda i: (i, 0)),
        out_shape=jax.ShapeDtypeStruct((n, D_MODEL), F32),
        compiler_params=_cparams(("arbitrary",), 40),
        name="combine",
    )(w, h2, g2, normf, y_by_k)


def _rotary_tables(pos):
    half = KEY_DIM_R // 2
    inv_freq = ROPE_BASE ** (-jnp.arange(half, dtype=F32) / half)
    ang = pos[:, None] * inv_freq[None, :]
    cos = jnp.cos(ang)
    sin = jnp.sin(ang)
    cos_t = jnp.concatenate([cos, cos], axis=1)
    sin_t = jnp.concatenate([-sin, sin], axis=1)
    return cos_t.astype(F32), sin_t.astype(F32)


def _rel_bias_table(rel_bias, n_rows, n_cols, q_offset):
    heads = rel_bias.shape[0]
    n_diag = n_rows + n_cols - 1
    dist = q_offset + (n_rows - 1) - np.arange(n_diag)
    idx = np.clip(dist, -REL_CLIP, REL_CLIP) + REL_CLIP
    n_hi = int(np.sum(dist > REL_CLIP))
    n_lo = int(np.sum(dist < -REL_CLIP))
    mid = rel_bias[:, int(idx[n_diag - n_lo - 1]):int(idx[n_hi]) + 1][:, ::-1]
    diag = jnp.concatenate([jnp.broadcast_to(rel_bias[:, 2 * REL_CLIP:], (heads, n_hi)), mid,
                            jnp.broadcast_to(rel_bias[:, :1], (heads, n_lo))], axis=1)
    period = n_diag + 1
    v = jnp.roll(jnp.pad(diag, ((0, 0), (0, 1))), -(n_rows - 1), axis=1)
    skew = jnp.tile(v, (1, n_rows))[:, :n_rows * (period - 1)].reshape(heads, n_rows, period - 1)
    return skew[:, :, :n_cols].astype(F32)


def _prompt_bias(rel_bias):
    n_cols = ATT_QB + ATT_WINDOW
    r = np.arange(ATT_QB)[:, None]
    c = np.arange(n_cols)[None, :]
    band = c - (r // CHUNK) * CHUNK
    valid = (band >= 0) & (band < ATT_WINDOW + CHUNK)
    table = _rel_bias_table(rel_bias, ATT_QB, n_cols, ATT_WINDOW)
    return jnp.where(jnp.asarray(valid)[None], table, NEG_BIG)


def _sample_bias(rel_bias, t_new, cache_len):
    b = _rel_bias_table(rel_bias, t_new, cache_len + t_new, cache_len)
    return b[:, :, :cache_len], b[:, :, cache_len:]


def _mod_parts(mod, rows_each):
    parts = jnp.split(mod, 6, axis=-1)
    if rows_each == 1:
        return [p[:, None, :] for p in parts]
    return [jnp.repeat(p, rows_each, axis=0)[None] for p in parts]


def kernel(x_prompt, x_sample, cache_attn_k, cache_attn_v, state_ret, c_prompt, c_sample,
           norm1_g, norm2_g, w_ada, b_ada, w_in, rel_bias, w_o_attn, w_o_ret, w_out,
           w_router, b_router, w_exp_gate, w_exp_up, w_exp_down, w_sh_gate, w_sh_up, w_sh_down,
           normf_g):
    batch, seq, d = x_prompt.shape
    dec_batch, dec_seq, _ = x_sample.shape
    depth = w_in.shape[0]
    assert depth == 1 and d == D_MODEL
    assert seq % ROW_TILE == 0 and dec_batch * dec_seq == ROW_TILE and ROW_TILE == ATT_WINDOW
    cache_len = cache_attn_k.shape[2]
    n_p = batch * seq
    n_s = dec_batch * dec_seq
    tpb = seq // ROW_TILE
    l = 0

    bf = lambda a: a.astype(BF16)
    c_all = jnp.concatenate([c_prompt, c_sample], axis=0)
    pad = (-c_all.shape[0]) % 8
    c_all = jnp.pad(c_all, ((0, pad), (0, 0)))
    mod = _ada(c_all, bf(w_ada[l]), b_ada[l][None, :])
    mod_p = _mod_parts(mod[:batch], 1)
    mod_s = _mod_parts(mod[batch:batch + dec_batch], dec_seq)

    w_in_bf = bf(w_in[l])
    n1g = norm1_g[l][None, :]
    n2g = norm2_g[l][None, :]
    dense_w = [bf(w_o_attn[l]), bf(w_o_ret[l]), bf(w_out[l]), bf(w_router[l]).T,
               bf(w_sh_gate[l]), bf(w_sh_up[l]), bf(w_sh_down[l])]

    xp = x_prompt.reshape(n_p, d)
    xs_ = x_sample.reshape(n_s, d)
    cos_p, sin_p = _rotary_tables(jnp.arange(seq, dtype=F32))
    pos_s = PAST_LEN + jnp.arange(dec_seq, dtype=F32)
    cos_s, sin_s = _rotary_tables(jnp.tile(pos_s, dec_batch))

    (qa, ka, va, qr, kr, vr, gr, ga, gb, kv_p) = _inproj(
        xp, mod_p[1], mod_p[0], n1g, cos_p, sin_p, w_in_bf, tpb)
    oa = _attn_prompt(qa, ka, va, _prompt_bias(rel_bias[l]), batch, seq)
    zero_state = jnp.zeros((batch, N_HEADS, KEY_DIM_R, VAL_DIM_R), F32)
    yr_in, state_p = _retention(qr, kr, vr, gr, zero_state, batch, seq, RET_CHUNK)
    h_p, n2p_p, s_p = _outproj(xp, oa, yr_in, ga, gb, mod_p[2], mod_p[4], mod_p[3], mod_p[5], n2g,
                               dense_w, tpb, n_p + n_s, 0)

    (qa_s, ka_s, va_s, qr_s, kr_s, vr_s, gr_s, ga_s, gb_s, kv_s) = _inproj(
        xs_, mod_s[1], mod_s[0], n1g, cos_s, sin_s, w_in_bf, 1)
    bias_c, bias_n = _sample_bias(rel_bias[l], dec_seq, cache_len)
    to_keys_minor = lambda c: jnp.transpose(c, (0, 1, 3, 4, 2))
    oa_s = _attn_sample(qa_s, ka_s, va_s, to_keys_minor(cache_attn_k), to_keys_minor(cache_attn_v),
                        bias_c, bias_n, dec_batch, dec_seq, cache_len)
    yr_in_s, state_s = _retention(qr_s, kr_s, vr_s, gr_s, state_ret[l], dec_batch, dec_seq, dec_seq)
    h_s, n2p, scores_t = _outproj(xs_, oa_s, yr_in_s, ga_s, gb_s, mod_s[2], mod_s[4], mod_s[3],
                                  mod_s[5], n2g, dense_w, 1, n_p + n_s, n_p, carried=(n2p_p, s_p))

    lanes_of = lambda v: jnp.broadcast_to(v[:, None], (N_EXPERTS, ROUTE_TILE))
    idx_t, w_t, rank_t, counts = _route(scores_t, lanes_of(b_router[l]))
    (tile_expert, n_tiles, expert_slot, expert_next, tile_halves, pstart, pad_rows,
     n_sorted_rows) = _ffn_plan(counts[:, 0], (n_p + n_s) * TOP_K)
    dest_t = _dest(idx_t, rank_t, lanes_of(pstart.astype(F32)))
    dest_kmajor = dest_t.reshape(-1)
    w_route = jnp.pad(w_t.T, ((0, 0), (0, LANES - TOP_K)))
    xs_sorted = _sc_dispatch(n2p, dest_kmajor, pad_rows, n_sorted_rows)
    ys_sorted = _ffn(tile_expert, n_tiles, expert_slot, expert_next, tile_halves, xs_sorted,
                     w_exp_gate[l], w_exp_up[l], w_exp_down[l])
    nf = normf_g[None, :]
    y_by_k = _sc_gather_rows(ys_sorted, dest_kmajor).reshape(TOP_K, n_p + n_s, d // 2)
    y_p = _combine(w_route, h_p, mod_p[5], nf, y_by_k, 0, tpb)
    y_s = _combine(w_route, h_s, mod_s[5], nf, y_by_k, n_p, 1)

    keep = min(ATT_WINDOW, seq)
    kv_p = kv_p.reshape(batch, ROW_TILE, 2, N_HEADS, HEAD_DIM_A)[:, ROW_TILE - keep:]
    kv_s = kv_s.reshape(dec_batch, dec_seq, 2, N_HEADS, HEAD_DIM_A)
    return (y_p.reshape(batch, seq, d), y_s.reshape(dec_batch, dec_seq, d),
            kv_p[:, :, 0][None], kv_p[:, :, 1][None], state_p[None],
            kv_s[:, :, 0][None], kv_s[:, :, 1][None], state_s[None])
```

```python
import functools

import numpy as np
import jax
import jax.numpy as jnp
from jax import lax
from jax.experimental import pallas as pl
from jax.experimental.pallas import tpu as pltpu
from jax.experimental.pallas import tpu_sc as plsc

F32 = jnp.float32
BF16 = jnp.bfloat16
I32 = jnp.int32
U32 = jnp.uint32

D_MODEL = 1024
PAST_LEN = 4096
CHUNK = 64
N_LEFT_CHUNKS = 8
ATT_WINDOW = N_LEFT_CHUNKS * CHUNK
N_HEADS = 8
HEAD_DIM_A = 64
D_ATT = N_HEADS * HEAD_DIM_A
REL_CLIP = 128
KEY_DIM_R = 64
VAL_DIM_R = 128
D_RET_K = N_HEADS * KEY_DIM_R
D_RET_V = N_HEADS * VAL_DIM_R
ROPE_BASE = 10000.0
N_EXPERTS = 256
TOP_K = 8
N_GROUPS = 8
GROUP_SIZE = N_EXPERTS // N_GROUPS
TOPK_GROUPS = 4
D_EXPERT = 256
ROUTED_SCALE = 2.5
EPS = 1e-6
IN_WIDTHS = (D_ATT, D_ATT, D_ATT, D_RET_K, D_RET_K, D_RET_V, D_RET_V, D_MODEL, D_MODEL)
IN_OFFS = tuple(int(v) for v in np.cumsum((0,) + IN_WIDTHS))
D_IN = IN_OFFS[-1]

NEG_BIG = -1e30
LANES = 128
V7X_VMEM_BYTES = 64 * 1024 * 1024
V7X_SC_CORES = 2
V7X_SC_SUBCORES = 16
SC_GATHER_ROWS = 96
SPARE_ROWS = 8192

ROW_TILE = 512
ATT_QB = 256
RET_CHUNK = 256
ROUTE_TILE = 512
MOVE_TILE = 256
FFN_QUANTUM = 128
FFN_FINE = 64
FFN_FINE_FROM = 384
FFN_TILE = 5 * FFN_QUANTUM
FFN_TILE_SIZES = tuple(n for n in range(1, FFN_TILE // FFN_FINE + 1)
                       if n * FFN_FINE > FFN_FINE_FROM or (n * FFN_FINE) % FFN_QUANTUM == 0)
FFN_WEIGHT_SLOTS = 3
VMEM_RESERVE_BYTES = 6 << 20


def _cparams(semantics, vmem_mb):
    return pltpu.CompilerParams(dimension_semantics=semantics,
                                vmem_limit_bytes=min(vmem_mb << 20, V7X_VMEM_BYTES - VMEM_RESERVE_BYTES))


def _silu(x):
    return x * jax.nn.sigmoid(x)


def _pack_bf16_pair(lo, hi):
    return pltpu.bitcast(pltpu.pack_elementwise([lo, hi], packed_dtype=BF16), U32)


def _unpack_bf16_pair(u):
    words = pltpu.bitcast(u, I32)
    lo = pltpu.unpack_elementwise(words, index=0, packed_dtype=BF16, unpacked_dtype=F32)
    hi = pltpu.unpack_elementwise(words, index=1, packed_dtype=BF16, unpacked_dtype=F32)
    return lo, hi


def _ada_kernel(c_ref, w_ref, b_ref, o_ref):
    sc = _silu(c_ref[...]).astype(BF16)
    o_ref[...] = jnp.dot(sc, w_ref[...], preferred_element_type=F32) + b_ref[...]


def _ada(c_all, w_ada_bf, b_ada):
    rows = c_all.shape[0]
    n_out = w_ada_bf.shape[1]
    blk = D_MODEL
    return pl.pallas_call(
        _ada_kernel,
        grid=(n_out // blk,),
        in_specs=[pl.BlockSpec((rows, D_MODEL), lambda j: (0, 0)),
                  pl.BlockSpec((D_MODEL, blk), lambda j: (0, j)),
                  pl.BlockSpec((1, blk), lambda j: (0, j))],
        out_specs=pl.BlockSpec((rows, blk), lambda j: (0, j)),
        out_shape=jax.ShapeDtypeStruct((rows, n_out), F32),
        compiler_params=_cparams(("arbitrary",), 24),
        name="ada",
    )(c_all, w_ada_bf, b_ada)


def _inproj_kernel(x_ref, sc_ref, sh_ref, g_ref, cos_ref, sin_ref, w_ref,
                   qa_ref, ka_ref, va_ref, qr_ref, kr_ref, vr_ref, gr_ref, ga_ref, gb_ref,
                   kv_ref, *, tiles_per_batch):
    x = x_ref[...]
    xn = x * lax.rsqrt(jnp.mean(x * x, axis=-1, keepdims=True) + EPS) * g_ref[...]
    nb = (xn * (1.0 + sc_ref[0]) + sh_ref[0]).astype(BF16)

    def proj(seg):
        return jnp.dot(nb, w_ref[:, IN_OFFS[seg]:IN_OFFS[seg + 1]], preferred_element_type=F32)

    qa_ref[...] = proj(0).astype(BF16)
    ka = proj(1)
    va = proj(2)
    ka_ref[...] = ka.astype(BF16)
    va_ref[...] = va.astype(BF16)

    @pl.when(pl.program_id(0) % tiles_per_batch == tiles_per_batch - 1)
    def _():
        kv_ref[:, :D_ATT] = ka
        kv_ref[:, D_ATT:] = va

    cos = jnp.tile(cos_ref[...], (1, N_HEADS))
    sin = jnp.tile(sin_ref[...], (1, N_HEADS))
    first_half = (lax.broadcasted_iota(I32, (1, D_RET_K), 1) % KEY_DIM_R) < (KEY_DIM_R // 2)

    def rotary(t):
        partner = jnp.where(first_half, pltpu.roll(t, D_RET_K - KEY_DIM_R // 2, 1),
                            pltpu.roll(t, KEY_DIM_R // 2, 1))
        return t * cos + partner * sin

    qr_ref[...] = rotary(proj(3)).astype(BF16)
    kr_ref[...] = (rotary(proj(4)) * (KEY_DIM_R ** -0.5)).astype(BF16)
    vr_ref[...] = proj(5).astype(BF16)
    gr_ref[...] = proj(6).astype(BF16)
    ga_ref[...] = proj(7).astype(BF16)
    gb_ref[...] = proj(8).astype(BF16)


def _inproj(x2d, sc, sh, g, cos_t, sin_t, w_in_bf, tiles_per_batch):
    n = x2d.shape[0]
    tm = ROW_TILE
    n_tiles = n // tm
    n_batches = n_tiles // tiles_per_batch
    mod_rows = sc.shape[1]
    pos_tiles = cos_t.shape[0] // tm

    def row_spec(width):
        return pl.BlockSpec((tm, width), lambda i: (i, 0))

    mod_spec = pl.BlockSpec((1, mod_rows, D_MODEL), lambda i: (i // tiles_per_batch, 0, 0))
    pos_spec = pl.BlockSpec((tm, KEY_DIM_R), lambda i: (i % pos_tiles, 0))
    out_widths = (D_ATT, D_ATT, D_ATT, D_RET_K, D_RET_K, D_RET_V, D_RET_V, D_MODEL, D_MODEL)
    out_shape = [jax.ShapeDtypeStruct((n, w), BF16) for w in out_widths]
    out_shape.append(jax.ShapeDtypeStruct((n_batches * tm, 2 * D_ATT), F32))
    out_specs = [row_spec(w) for w in out_widths]
    out_specs.append(pl.BlockSpec((tm, 2 * D_ATT), lambda i: (i // tiles_per_batch, 0)))
    return pl.pallas_call(
        functools.partial(_inproj_kernel, tiles_per_batch=tiles_per_batch),
        grid=(n_tiles,),
        in_specs=[row_spec(D_MODEL), mod_spec, mod_spec,
                  pl.BlockSpec((1, D_MODEL), lambda i: (0, 0)),
                  pos_spec, pos_spec,
                  pl.BlockSpec((D_MODEL, D_IN), lambda i: (0, 0))],
        out_specs=out_specs,
        out_shape=out_shape,
        compiler_params=_cparams(("arbitrary",), 56),
        name="inproj",
    )(x2d, sc, sh, g, cos_t, sin_t, w_in_bf)


def _softmax_pv(s, v_parts):
    m = functools.reduce(jnp.maximum, [jnp.max(t, axis=-1, keepdims=True) for t in s])
    ps = [jnp.exp(t - m) for t in s]
    l = functools.reduce(jnp.add, [jnp.sum(p, axis=-1, keepdims=True) for p in ps])
    o = functools.reduce(jnp.add, [jnp.dot(p.astype(BF16), v, preferred_element_type=F32)
                                   for p, v in zip(ps, v_parts)])
    return o / l


def _attn_prompt_kernel(q_ref, k0_ref, k1_ref, k2_ref, v0_ref, v1_ref, v2_ref, bias_ref, o_ref):
    j = pl.program_id(1)
    q = q_ref[...]
    k = jnp.concatenate([k0_ref[...], k1_ref[...], k2_ref[...]], axis=0)
    v = jnp.concatenate([v0_ref[...], v1_ref[...], v2_ref[...]], axis=0)
    n_keys = k.shape[0]
    key_block = lax.broadcasted_iota(I32, (1, n_keys), 1) // ATT_QB
    before_start = jnp.where(key_block < 2 - j, NEG_BIG, 0.0)
    outs = []
    for h in range(N_HEADS):
        sl = slice(h * HEAD_DIM_A, (h + 1) * HEAD_DIM_A)
        qh = (q[:, sl].astype(F32) * (HEAD_DIM_A ** -0.5)).astype(BF16)
        s = lax.dot_general(qh, k[:, sl], (((1,), (1,)), ((), ())), preferred_element_type=F32)
        s = s + bias_ref[h] + before_start
        outs.append(_softmax_pv([s], [v[:, sl]]))
    o_ref[...] = jnp.concatenate(outs, axis=1).astype(BF16)


def _attn_prompt(q, k, v, bias_full, batch, seq):
    qb = ATT_QB
    nq = seq // qb

    def q_map(b, j):
        return (b * nq + j, 0)

    def kv_map(back):
        return lambda b, j: (b * nq + jnp.maximum(j - back, 0), 0)

    blk = lambda m: pl.BlockSpec((qb, D_ATT), m)
    return pl.pallas_call(
        _attn_prompt_kernel,
        grid=(batch, nq),
        in_specs=[blk(q_map), blk(kv_map(2)), blk(kv_map(1)), blk(kv_map(0)),
                  blk(kv_map(2)), blk(kv_map(1)), blk(kv_map(0)),
                  pl.BlockSpec(bias_full.shape, lambda b, j: (0, 0, 0))],
        out_specs=blk(q_map),
        out_shape=jax.ShapeDtypeStruct((batch * seq, D_ATT), BF16),
        compiler_params=_cparams(("parallel", "arbitrary"), 40),
        name="attn_prompt",
    )(q, k, k, k, v, v, v, bias_full)


SAMPLE_ATT_BATCHES = 2


def _attn_sample_kernel(q_ref, kn_ref, vn_ref, ck_ref, cv_ref, bc_ref, bn_ref, o_ref, *, t_new):
    nt = (((1,), (1,)), ((), ()))
    for b in range(SAMPLE_ATT_BATCHES):
        rows = slice(b * t_new, (b + 1) * t_new)
        q = q_ref[rows, :]
        kn = kn_ref[rows, :]
        vn = vn_ref[rows, :]
        outs = []
        for h in range(N_HEADS):
            sl = slice(h * HEAD_DIM_A, (h + 1) * HEAD_DIM_A)
            qh = (q[:, sl].astype(F32) * (HEAD_DIM_A ** -0.5)).astype(BF16)
            kc_t = ck_ref[b, h].astype(BF16)
            vc_t = cv_ref[b, h].astype(BF16)
            s_c = jnp.dot(qh, kc_t, preferred_element_type=F32) + bc_ref[h]
            s_n = lax.dot_general(qh, kn[:, sl], nt, preferred_element_type=F32) + bn_ref[h]
            m = jnp.maximum(jnp.max(s_c, axis=-1, keepdims=True), jnp.max(s_n, axis=-1, keepdims=True))
            p_c = jnp.exp(s_c - m)
            p_n = jnp.exp(s_n - m)
            l = jnp.sum(p_c, axis=-1, keepdims=True) + jnp.sum(p_n, axis=-1, keepdims=True)
            o = (lax.dot_general(p_c.astype(BF16), vc_t, nt, preferred_element_type=F32)
                 + jnp.dot(p_n.astype(BF16), vn[:, sl], preferred_element_type=F32))
            outs.append(o / l)
        o_ref[rows, :] = jnp.concatenate(outs, axis=1).astype(BF16)


def _attn_sample(q, k, v, cache_k_t, cache_v_t, bias_cache, bias_new, batch, t_new, cache_len):
    nb = SAMPLE_ATT_BATCHES
    blk = pl.BlockSpec((nb * t_new, D_ATT), lambda b: (b, 0))
    cblk = pl.BlockSpec((None, nb, N_HEADS, HEAD_DIM_A, cache_len), lambda b: (0, b, 0, 0, 0))
    return pl.pallas_call(
        functools.partial(_attn_sample_kernel, t_new=t_new),
        grid=(batch // nb,),
        in_specs=[blk, blk, blk, cblk, cblk,
                  pl.BlockSpec(bias_cache.shape, lambda b: (0, 0, 0)),
                  pl.BlockSpec(bias_new.shape, lambda b: (0, 0, 0))],
        out_specs=blk,
        out_shape=jax.ShapeDtypeStruct((batch * t_new, D_ATT), BF16),
        compiler_params=_cparams(("arbitrary",), 40),
        name="attn_sample",
    )(q, k, v, cache_k_t, cache_v_t, bias_cache, bias_new)


def _ret_kernel(q_ref, k_ref, v_ref, g_ref, s0_ref, dmask_ref, qdec_ref, kdec_ref, sdec_ref,
                y_ref, sout_ref, state_ref):
    c = pl.program_id(1)

    @pl.when(c == 0)
    def _():
        state_ref[...] = s0_ref[0]

    q = q_ref[...]
    k = k_ref[...]
    v = v_ref[...]
    g = g_ref[...]
    outs = []
    for h in range(N_HEADS):
        ks = slice(h * KEY_DIM_R, (h + 1) * KEY_DIM_R)
        vs = slice(h * VAL_DIM_R, (h + 1) * VAL_DIM_R)
        qh, kh, vh = q[:, ks], k[:, ks], v[:, vs]
        scores = lax.dot_general(qh, kh, (((1,), (1,)), ((), ())), preferred_element_type=F32)
        inner = jnp.dot((scores * dmask_ref[h]).astype(BF16), vh, preferred_element_type=F32)
        state = state_ref[h]
        cross = jnp.dot(qh, state.astype(BF16), preferred_element_type=F32) * qdec_ref[h]
        o = inner + cross
        v_dec = (vh.astype(F32) * kdec_ref[h]).astype(BF16)
        state_ref[h] = sdec_ref[h] * state + lax.dot_general(
            kh, v_dec, (((0,), (0,)), ((), ())), preferred_element_type=F32)
        on = o * lax.rsqrt(jnp.mean(o * o, axis=-1, keepdims=True) + EPS)
        outs.append(on * _silu(g[:, vs].astype(F32)))
    y_ref[...] = jnp.concatenate(outs, axis=1).astype(BF16)

    @pl.when(c == pl.num_programs(1) - 1)
    def _():
        sout_ref[0] = state_ref[...]


def _ret_tables(chunk):
    log_g = jnp.log(1.0 - jnp.exp2(-5.0 - jnp.arange(N_HEADS, dtype=F32)))
    i = jnp.arange(chunk, dtype=F32)
    diff = i[:, None] - i[None, :]
    dmask = jnp.where(diff >= 0, jnp.exp(log_g[:, None, None] * jnp.maximum(diff, 0.0)), 0.0)
    qdec = jnp.exp(log_g[:, None] * (i + 1.0))
    kdec = jnp.exp(log_g[:, None] * (chunk - 1.0 - i))
    sdec = jnp.exp(log_g * chunk)
    bc = lambda t: jnp.broadcast_to(t[:, :, None], (N_HEADS, t.shape[1], VAL_DIM_R)).astype(F32)
    sdec_t = jnp.broadcast_to(sdec[:, None, None], (N_HEADS, 1, VAL_DIM_R)).astype(F32)
    return dmask.astype(F32), bc(qdec), bc(kdec), sdec_t


def _retention(q, k, v, gate, state0, batch, seq, chunk):
    nc = seq // chunk
    dmask, qdec, kdec, sdec = _ret_tables(chunk)
    row = lambda w: pl.BlockSpec((chunk, w), lambda b, c: (b * nc + c, 0))
    const = lambda a: pl.BlockSpec(a.shape, lambda b, c: (0,) * a.ndim)
    st_spec = pl.BlockSpec((1, N_HEADS, KEY_DIM_R, VAL_DIM_R), lambda b, c: (b, 0, 0, 0))
    return pl.pallas_call(
        _ret_kernel,
        grid=(batch, nc),
        in_specs=[row(D_RET_K), row(D_RET_K), row(D_RET_V), row(D_RET_V), st_spec,
                  const(dmask), const(qdec), const(kdec), const(sdec)],
        out_specs=[row(D_RET_V), st_spec],
        out_shape=[jax.ShapeDtypeStruct((batch * seq, D_RET_V), BF16),
                   jax.ShapeDtypeStruct((batch, N_HEADS, KEY_DIM_R, VAL_DIM_R), F32)],
        scratch_shapes=[pltpu.VMEM((N_HEADS, KEY_DIM_R, VAL_DIM_R), F32)],
        compiler_params=_cparams(("parallel", "arbitrary"), 32),
        name="retention",
    )(q, k, v, gate, state0, dmask, qdec, kdec, sdec)


def _outproj_kernel(x_ref, oa_ref, yr_ref, ga_ref, gb_ref, g1_ref, sc2_ref, sh2_ref, g2_ref, n2g_ref,
                    woa_ref, wor_ref, wout_ref, wrt_ref, wsg_ref, wsu_ref, wsd_ref, *rest):
    h_ref, n2p_ref, s_ref = rest[-3:]
    ya = jnp.dot(oa_ref[...], woa_ref[...], preferred_element_type=F32)
    yr = jnp.dot(yr_ref[...], wor_ref[...], preferred_element_type=F32)
    merged = (jax.nn.sigmoid(ga_ref[...].astype(F32)) * ya
              + jax.nn.sigmoid(gb_ref[...].astype(F32)) * yr)
    mix = jnp.dot(merged.astype(BF16), wout_ref[...], preferred_element_type=F32)
    h = x_ref[...] + g1_ref[0] * mix
    hn = h * lax.rsqrt(jnp.mean(h * h, axis=-1, keepdims=True) + EPS) * n2g_ref[...]
    n2 = hn * (1.0 + sc2_ref[0]) + sh2_ref[0]
    n2b = n2.astype(BF16)
    s_ref[...] = jax.nn.sigmoid(lax.dot_general(wrt_ref[...], n2b, (((1,), (1,)), ((), ())),
                                                preferred_element_type=F32))
    hid = _silu(jnp.dot(n2b, wsg_ref[...], preferred_element_type=F32)) * jnp.dot(
        n2b, wsu_ref[...], preferred_element_type=F32)
    shared = jnp.dot(hid.astype(BF16), wsd_ref[...], preferred_element_type=F32)
    h_ref[...] = h + g2_ref[0] * shared
    half = D_MODEL // 2
    n2p_ref[...] = _pack_bf16_pair(n2[:, :half], n2[:, half:])


def _outproj(x2d, oa, yr_in, ga, gb, g1, sc2, sh2, g2, n2g, weights, tiles_per_batch,
             all_tokens, token_offset, carried=None):
    n = x2d.shape[0]
    tm = ROW_TILE
    off = token_offset // tm
    mod_rows = g1.shape[1]
    row = lambda w: pl.BlockSpec((tm, w), lambda i: (i, 0))
    mod_spec = pl.BlockSpec((1, mod_rows, D_MODEL), lambda i: (i // tiles_per_batch, 0, 0))
    const = lambda a: pl.BlockSpec(a.shape, lambda i: (0,) * a.ndim)
    in_specs = [row(D_MODEL), row(D_ATT), row(D_RET_V), row(D_MODEL), row(D_MODEL),
                mod_spec, mod_spec, mod_spec, mod_spec, const(n2g)] + [const(w) for w in weights]
    args = [x2d, oa, yr_in, ga, gb, g1, sc2, sh2, g2, n2g, *weights]
    aliases = {}
    if carried is not None:
        aliases = {len(args): 1, len(args) + 1: 2}
        in_specs += [pl.BlockSpec(memory_space=pl.ANY)] * 2
        args += list(carried)
    return pl.pallas_call(
        _outproj_kernel,
        grid=(n // tm,),
        in_specs=in_specs,
        out_specs=[row(D_MODEL),
                   pl.BlockSpec((tm, D_MODEL // 2), lambda i: (i + off, 0)),
                   pl.BlockSpec((N_EXPERTS, tm), lambda i: (0, i + off))],
        out_shape=[jax.ShapeDtypeStruct((n, D_MODEL), F32),
                   jax.ShapeDtypeStruct((all_tokens, D_MODEL // 2), U32),
                   jax.ShapeDtypeStruct((N_EXPERTS, all_tokens), F32)],
        input_output_aliases=aliases,
        compiler_params=_cparams(("arbitrary",), 48),
        name="outproj",
    )(*args)


def _route_kernel(s_ref, b_ref, idx_ref, w_ref, rank_ref, cnt_ref, run_ref, tri_ref):
    step = pl.program_id(0)
    t = s_ref.shape[1]

    @pl.when(step == 0)
    def _():
        run_ref[...] = jnp.zeros_like(run_ref)
        r = lax.broadcasted_iota(I32, (t, t), 0)
        c = lax.broadcasted_iota(I32, (t, t), 1)
        tri_ref[...] = jnp.where(r < c, 1.0, 0.0).astype(BF16)

    s = s_ref[...]
    sel = s + b_ref[...]
    row_f = lax.broadcasted_iota(I32, (N_EXPERTS, t), 0).astype(F32)

    def first_argmax(vals, rows):
        m = jnp.max(vals, axis=0, keepdims=True)
        pos = jnp.min(jnp.where(vals == m, rows, float(N_EXPERTS)), axis=0, keepdims=True)
        return m, pos

    gscore = []
    group_row = lax.broadcasted_iota(I32, (GROUP_SIZE, t), 0).astype(F32)
    for g in range(N_GROUPS):
        rows = slice(g * GROUP_SIZE, (g + 1) * GROUP_SIZE)
        m1, p1 = first_argmax(sel[rows], group_row)
        m2 = jnp.max(jnp.where(group_row == p1, -jnp.inf, sel[rows]), axis=0, keepdims=True)
        gscore.append(m1 + m2)
    cand_parts = []
    for g in range(N_GROUPS):
        rows = slice(g * GROUP_SIZE, (g + 1) * GROUP_SIZE)
        beaten_by = jnp.zeros((1, t), F32)
        for o in range(N_GROUPS):
            if o == g:
                continue
            wins = (gscore[o] > gscore[g]) if o > g else (gscore[o] >= gscore[g])
            beaten_by = beaten_by + jnp.where(wins, 1.0, 0.0)
        cand_parts.append(jnp.where(beaten_by < TOPK_GROUPS, sel[rows], -jnp.inf))
    cand = jnp.concatenate(cand_parts, axis=0)

    picked = jnp.zeros((N_EXPERTS, t), F32)
    idx_rows, w_rows = [], []
    for _ in range(TOP_K):
        _, pos = first_argmax(cand, row_f)
        hit = row_f == pos
        w_rows.append(jnp.sum(jnp.where(hit, s, 0.0), axis=0, keepdims=True))
        idx_rows.append(pos)
        picked = jnp.where(hit, 1.0, picked)
        cand = jnp.where(hit, -jnp.inf, cand)
    w_sum = functools.reduce(jnp.add, w_rows)

    before = jnp.dot(picked.astype(BF16), tri_ref[...], preferred_element_type=F32) + run_ref[...]
    run_ref[...] = run_ref[...] + jnp.sum(picked, axis=1, keepdims=True)
    rank_rows = [jnp.sum(jnp.where(row_f == idx_rows[kk], before, 0.0), axis=0, keepdims=True)
                 for kk in range(TOP_K)]

    idx_ref[...] = jnp.concatenate(idx_rows, axis=0).astype(I32)
    w_ref[...] = jnp.concatenate([w / w_sum * ROUTED_SCALE for w in w_rows], axis=0)
    rank_ref[...] = jnp.concatenate(rank_rows, axis=0).astype(I32)

    @pl.when(step == pl.num_programs(0) - 1)
    def _():
        cnt_ref[...] = run_ref[...].astype(I32)


def _route(scores_t, b_col):
    n = scores_t.shape[1]
    t = ROUTE_TILE
    col = pl.BlockSpec((TOP_K, t), lambda i: (0, i))
    const = pl.BlockSpec((N_EXPERTS, t), lambda i: (0, 0))
    return pl.pallas_call(
        _route_kernel,
        grid=(n // t,),
        in_specs=[pl.BlockSpec((N_EXPERTS, t), lambda i: (0, i)), const],
        out_specs=[col, col, col, const],
        out_shape=[jax.ShapeDtypeStruct((TOP_K, n), I32),
                   jax.ShapeDtypeStruct((TOP_K, n), F32),
                   jax.ShapeDtypeStruct((TOP_K, n), I32),
                   jax.ShapeDtypeStruct((N_EXPERTS, t), I32)],
        scratch_shapes=[pltpu.VMEM((N_EXPERTS, t), F32), pltpu.VMEM((t, t), BF16)],
        compiler_params=_cparams(("arbitrary",), 32),
        name="route",
    )(scores_t, b_col)


def _dest_kernel(idx_ref, rank_ref, start_ref, dest_ref):
    t = idx_ref.shape[1]
    row = lax.broadcasted_iota(I32, (N_EXPERTS, t), 0)
    starts = start_ref[...]
    base = [jnp.sum(jnp.where(row == idx_ref[kk:kk + 1, :], starts, 0.0), axis=0, keepdims=True)
            for kk in range(TOP_K)]
    dest_ref[...] = jnp.concatenate(base, axis=0).astype(I32) + rank_ref[...]


def _dest(idx_t, rank_t, starts_col):
    n = idx_t.shape[1]
    t = ROUTE_TILE
    col = pl.BlockSpec((TOP_K, t), lambda i: (0, i))
    return pl.pallas_call(
        _dest_kernel,
        grid=(n // t,),
        in_specs=[col, col, pl.BlockSpec((N_EXPERTS, t), lambda i: (0, 0))],
        out_specs=col,
        out_shape=jax.ShapeDtypeStruct((TOP_K, n), I32),
        compiler_params=_cparams(("arbitrary",), 32),
        name="dest",
    )(idx_t, rank_t, starts_col)


def _sc_mesh():
    return plsc.VectorSubcoreMesh(core_axis_name="core", subcore_axis_name="subcore",
                                  num_cores=V7X_SC_CORES, num_subcores=V7X_SC_SUBCORES)


def _sc_dispatch(n2p, dest_kmajor, pad_rows, n_out_rows):
    n, width = n2p.shape
    rows = SC_GATHER_ROWS
    n_workers = V7X_SC_CORES * V7X_SC_SUBCORES
    src_chunks = n // rows
    items = dest_kmajor.shape[0] // rows
    per_worker = items // n_workers
    pad_per_worker = pad_rows.shape[0] // rows // n_workers
    assert src_chunks * rows == n and per_worker * n_workers == items and per_worker % 2 == 0
    assert pad_per_worker * n_workers * rows == pad_rows.shape[0]
    idx3 = dest_kmajor.reshape(n_workers, per_worker, rows)
    pad3 = pad_rows.reshape(n_workers, pad_per_worker, rows)
    zeros = jnp.zeros((rows, width), n2p.dtype)

    def body(src_hbm, idx_hbm, pad_hbm, zero_hbm, out_hbm, idx_v, pad_v, rows_v, load_sem, scat_sem):
        worker = lax.axis_index("subcore") * V7X_SC_CORES + lax.axis_index("core")
        pltpu.sync_copy(idx_hbm.at[worker], idx_v)
        pltpu.sync_copy(pad_hbm.at[worker], pad_v)
        pltpu.sync_copy(zero_hbm, rows_v.at[0])

        def zero_fill(c):
            return pltpu.make_async_copy(rows_v.at[0], out_hbm.at[pad_v.at[c]], scat_sem.at[0])

        @pl.loop(0, pad_per_worker)
        def _(c):
            zero_fill(c).start()

        @pl.loop(0, pad_per_worker)
        def _(c):
            zero_fill(c).wait()

        def load(c, b):
            chunk = lax.rem(worker * per_worker + c, src_chunks)
            off = pl.multiple_of(chunk * rows, rows)
            return pltpu.make_async_copy(src_hbm.at[pl.ds(off, rows)], rows_v.at[b], load_sem.at[b])

        def scatter(c, b):
            return pltpu.make_async_copy(rows_v.at[b], out_hbm.at[idx_v.at[c]], scat_sem.at[b])

        load(0, 0).start()

        @pl.loop(0, per_worker, step=2)
        def _(c0):
            for b in range(2):
                c = c0 + b
                load(c, b).wait()

                @pl.when(c >= 1)
                def _():
                    scatter(c - 1, 1 - b).wait()

                @pl.when(c + 1 < per_worker)
                def _():
                    load(c + 1, 1 - b).start()

                scatter(c, b).start()

        scatter(per_worker - 1, (per_worker - 1) % 2).wait()

    return pl.kernel(
        body, mesh=_sc_mesh(),
        out_type=jax.ShapeDtypeStruct((n_out_rows, width), n2p.dtype),
        scratch_types=[pltpu.VMEM((per_worker, rows), I32),
                       pltpu.VMEM((pad_per_worker, rows), I32),
                       pltpu.VMEM((2, rows, width), n2p.dtype),
                       pltpu.SemaphoreType.DMA((2,)),
                       pltpu.SemaphoreType.DMA((2,))],
        name="sc_dispatch",
    )(n2p, idx3, pad3, zeros)


def _ffn_kernel(texp_ref, ntiles_ref, eslot_ref, enext_ref, nhalf_ref, xs_ref, wg_hbm, wu_hbm, wd_hbm,
                ys_ref, wg_buf, wu_buf, wd_buf, wgu_bf, wd_bf, sems):
    g = pl.program_id(0)

    def weight_copies(e, slot):
        return (pltpu.make_async_copy(wg_hbm.at[e], wg_buf.at[slot], sems.at[slot, 0]),
                pltpu.make_async_copy(wu_hbm.at[e], wu_buf.at[slot], sems.at[slot, 1]),
                pltpu.make_async_copy(wd_hbm.at[e], wd_buf.at[slot], sems.at[slot, 2]))

    @pl.when(g < ntiles_ref[0])
    def _():
        e = texp_ref[g]
        changed = jnp.logical_or(g == 0, texp_ref[jnp.maximum(g - 1, 0)] != e)

        @pl.when(changed)
        def _():
            slot = eslot_ref[e]

            def fetch_ahead(first_hop, hops, target_slot):
                ahead = first_hop
                for _ in range(hops - 1):
                    ahead = jnp.where(ahead >= 0, enext_ref[jnp.maximum(ahead, 0)], -1)

                @pl.when(ahead >= 0)
                def _():
                    for c in weight_copies(ahead, target_slot):
                        c.start()

            @pl.when(g == 0)
            def _():
                for c in weight_copies(e, slot):
                    c.start()
                for hops in range(1, FFN_WEIGHT_SLOTS - 1):
                    fetch_ahead(enext_ref[e], hops, lax.rem(slot + hops, FFN_WEIGHT_SLOTS))

            fetch_ahead(enext_ref[e], FFN_WEIGHT_SLOTS - 1,
                        lax.rem(slot + FFN_WEIGHT_SLOTS - 1, FFN_WEIGHT_SLOTS))
            for c in weight_copies(e, slot):
                c.wait()

            wgu_bf[:, :D_EXPERT] = wg_buf[slot].astype(BF16)
            wgu_bf[:, D_EXPERT:] = wu_buf[slot].astype(BF16)
            wd_bf[...] = wd_buf[slot].astype(BF16)

        def expert_rows(rows):
            lo, hi = _unpack_bf16_pair(xs_ref[rows, :])
            x = jnp.concatenate([lo, hi], axis=1).astype(BF16)
            gu = jnp.dot(x, wgu_bf[...], preferred_element_type=F32)
            hid = (_silu(gu[:, :D_EXPERT]) * gu[:, D_EXPERT:]).astype(BF16)
            y = jnp.dot(hid, wd_bf[...], preferred_element_type=F32)
            half = D_MODEL // 2
            ys_ref[rows, :] = _pack_bf16_pair(y[:, :half], y[:, half:])

        for n_groups in FFN_TILE_SIZES:

            @pl.when(nhalf_ref[g] == n_groups)
            def _(n_groups=n_groups):
                used = n_groups * FFN_FINE
                expert_rows(slice(0, used))
                if used < FFN_TILE:
                    ys_ref[used:, :] = jnp.zeros((FFN_TILE - used, ys_ref.shape[1]), U32)


def _ffn(tile_expert, n_tiles, expert_slot, expert_next, tile_halves, xs, w_gate, w_up, w_down):
    rows, width = xs.shape
    m = FFN_TILE
    max_tiles = tile_expert.shape[0]
    row_map = lambda g, te, nt, es, en, nh: (jnp.minimum(g, nt[0] - 1), 0)
    hbm = pl.BlockSpec(memory_space=pl.ANY)
    grid_spec = pltpu.PrefetchScalarGridSpec(
        num_scalar_prefetch=5,
        grid=(max_tiles,),
        in_specs=[pl.BlockSpec((m, width), row_map), hbm, hbm, hbm],
        out_specs=pl.BlockSpec((m, width), row_map),
        scratch_shapes=[pltpu.VMEM((FFN_WEIGHT_SLOTS, D_MODEL, D_EXPERT), F32),
                        pltpu.VMEM((FFN_WEIGHT_SLOTS, D_MODEL, D_EXPERT), F32),
                        pltpu.VMEM((FFN_WEIGHT_SLOTS, D_EXPERT, D_MODEL), F32),
                        pltpu.VMEM((D_MODEL, 2 * D_EXPERT), BF16),
                        pltpu.VMEM((D_EXPERT, D_MODEL), BF16),
                        pltpu.SemaphoreType.DMA((FFN_WEIGHT_SLOTS, 3))],
    )
    return pl.pallas_call(
        _ffn_kernel,
        grid_spec=grid_spec,
        out_shape=jax.ShapeDtypeStruct((rows, width), U32),
        compiler_params=_cparams(("arbitrary",), 32),
        name="ffn",
    )(tile_expert, n_tiles, expert_slot, expert_next, tile_halves, xs, w_gate, w_up, w_down)


def _ffn_plan(counts, n_assign):
    m = FFN_TILE
    max_tiles = n_assign // m + N_EXPERTS
    padded = ((counts + m - 1) // m) * m
    pend = jnp.cumsum(padded).astype(I32)
    pstart = pend - padded
    n_tiles = pend[-1:] // m
    g = jnp.minimum(jnp.arange(max_tiles, dtype=I32), n_tiles - 1)
    tile_expert = jnp.sum((pend[None, :] <= (g * m)[:, None]).astype(I32), axis=1)
    tile_expert = jnp.minimum(tile_expert, N_EXPERTS - 1)
    full_rows = jnp.maximum(padded - m, 0)
    last_rows = counts - full_rows
    round_to = lambda v, q: ((v + q - 1) // q) * q
    last_rows = jnp.where(last_rows > FFN_FINE_FROM, round_to(last_rows, FFN_FINE),
                          round_to(last_rows, FFN_QUANTUM))
    vend = pstart + full_rows + last_rows
    own = tile_expert[:, None] == jnp.arange(N_EXPERTS, dtype=I32)[None, :]
    tile_vend = jnp.sum(jnp.where(own, vend[None, :], 0), axis=1)
    tile_halves = (jnp.clip(tile_vend - g * m, 0, m) // FFN_FINE).astype(I32)
    used = counts > 0
    expert_slot = ((jnp.cumsum(used.astype(I32)) - 1) % FFN_WEIGHT_SLOTS).astype(I32)
    ids = jnp.where(used, jnp.arange(N_EXPERTS, dtype=I32), N_EXPERTS)
    first_used_from = lax.cummin(ids, axis=0, reverse=True)
    nxt = jnp.concatenate([first_used_from[1:], jnp.full((1,), N_EXPERTS, I32)])
    expert_next = jnp.where(nxt < N_EXPERTS, nxt, -1).astype(I32)
    spare_row = max_tiles * m
    j = jnp.arange(FFN_QUANTUM, dtype=I32)[None, :]
    seg_end = (pstart + counts)[:, None]
    spare = spare_row + (jnp.arange(N_EXPERTS, dtype=I32)[:, None] * FFN_QUANTUM + j) % SPARE_ROWS
    pad_rows = jnp.where(j < (vend[:, None] - seg_end), seg_end + j, spare).astype(I32).reshape(-1)
    unit = SC_GATHER_ROWS * V7X_SC_CORES * V7X_SC_SUBCORES
    extra = (-pad_rows.shape[0]) % unit
    filler = spare_row + (pad_rows.shape[0] + jnp.arange(extra, dtype=I32)) % SPARE_ROWS
    pad_rows = jnp.concatenate([pad_rows, filler])
    return (tile_expert, n_tiles, expert_slot, expert_next, tile_halves, pstart, pad_rows,
            spare_row + SPARE_ROWS)


def _sc_gather_rows(table, idx):
    n_idx = idx.shape[0]
    width = table.shape[1]
    n_workers = V7X_SC_CORES * V7X_SC_SUBCORES
    per_worker = n_idx // n_workers
    n_chunks = per_worker // SC_GATHER_ROWS
    assert per_worker * n_workers == n_idx and n_chunks * SC_GATHER_ROWS == per_worker and n_chunks % 2 == 0

    def body(table_hbm, idx_hbm, out_hbm, idx_v, rows_v, gather_sem, write_sem):
        worker = lax.axis_index("subcore") * V7X_SC_CORES + lax.axis_index("core")
        base = worker * per_worker
        pltpu.sync_copy(idx_hbm.at[pl.ds(base, per_worker)], idx_v)

        def gather(c, b):
            off = pl.multiple_of(c * SC_GATHER_ROWS, SC_GATHER_ROWS)
            return pltpu.make_async_copy(table_hbm.at[idx_v.at[pl.ds(off, SC_GATHER_ROWS)]],
                                         rows_v.at[b], gather_sem.at[b])

        def write(c, b):
            off = pl.multiple_of(c * SC_GATHER_ROWS, SC_GATHER_ROWS)
            return pltpu.make_async_copy(rows_v.at[b], out_hbm.at[pl.ds(base + off, SC_GATHER_ROWS)],
                                         write_sem.at[b])

        gather(0, 0).start()

        @pl.loop(0, n_chunks, step=2)
        def _(c0):
            for b in range(2):
                c = c0 + b
                gather(c, b).wait()

                @pl.when(c >= 1)
                def _():
                    write(c - 1, 1 - b).wait()

                @pl.when(c + 1 < n_chunks)
                def _():
                    gather(c + 1, 1 - b).start()

                write(c, b).start()

        write(n_chunks - 1, (n_chunks - 1) % 2).wait()

    return pl.kernel(
        body, mesh=_sc_mesh(),
        out_type=jax.ShapeDtypeStruct((n_idx, width), table.dtype),
        scratch_types=[pltpu.VMEM((per_worker,), I32),
                       pltpu.VMEM((2, SC_GATHER_ROWS, width), table.dtype),
                       pltpu.SemaphoreType.DMA((2,)),
                       pltpu.SemaphoreType.DMA((2,))],
        name="sc_gather_rows",
    )(table, idx)


def _combine_kernel(w_ref, h_ref, g2_ref, nf_ref, yk_ref, y_ref):
    t = h_ref.shape[0]
    w = w_ref[...]
    acc_lo = jnp.zeros((t, D_MODEL // 2), F32)
    acc_hi = jnp.zeros((t, D_MODEL // 2), F32)
    for kk in range(TOP_K):
        lo, hi = _unpack_bf16_pair(yk_ref[kk])
        wk = w[:, kk:kk + 1]
        acc_lo = acc_lo + wk * lo
        acc_hi = acc_hi + wk * hi
    out = h_ref[...] + g2_ref[0] * jnp.concatenate([acc_lo, acc_hi], axis=1)
    y_ref[...] = out * lax.rsqrt(jnp.mean(out * out, axis=-1, keepdims=True) + EPS) * nf_ref[...]


def _combine(w, h2, g2, normf, y_by_k, row_offset, tiles_per_batch):
    n = h2.shape[0]
    t = MOVE_TILE
    off = row_offset // t
    mod_rows = g2.shape[1]
    mod_tiles = max(ROW_TILE // t, 1) * tiles_per_batch if mod_rows == 1 else n // t
    return pl.pallas_call(
        _combine_kernel,
        grid=(n // t,),
        in_specs=[pl.BlockSpec((t, LANES), lambda i: (i + off, 0)),
                  pl.BlockSpec((t, D_MODEL), lambda i: (i, 0)),
                  pl.BlockSpec((1, mod_rows if mod_rows == 1 else t, D_MODEL),
                               (lambda i: (i // mod_tiles, 0, 0)) if mod_rows == 1
                               else (lambda i: (0, i, 0))),
                  pl.BlockSpec((1, D_MODEL), lambda i: (0, 0)),
                  pl.BlockSpec((TOP_K, t, D_MODEL // 2), lambda i: (0, i + off, 0))],
        out_specs=pl.BlockSpec((t, D_MODEL), lambda i: (i, 0)),
        out_shape=jax.ShapeDtypeStruct((n, D_MODEL), F32),
        compiler_params=_cparams(("arbitrary",), 40),
        name="combine",
    )(w, h2, g2, normf, y_by_k)


def _rotary_tables(pos):
    half = KEY_DIM_R // 2
    inv_freq = ROPE_BASE ** (-jnp.arange(half, dtype=F32) / half)
    ang = pos[:, None] * inv_freq[None, :]
    cos = jnp.cos(ang)
    sin = jnp.sin(ang)
    cos_t = jnp.concatenate([cos, cos], axis=1)
    sin_t = jnp.concatenate([-sin, sin], axis=1)
    return cos_t.astype(F32), sin_t.astype(F32)


def _rel_bias_table(rel_bias, n_rows, n_cols, q_offset):
    heads = rel_bias.shape[0]
    n_diag = n_rows + n_cols - 1
    dist = q_offset + (n_rows - 1) - np.arange(n_diag)
    idx = np.clip(dist, -REL_CLIP, REL_CLIP) + REL_CLIP
    n_hi = int(np.sum(dist > REL_CLIP))
    n_lo = int(np.sum(dist < -REL_CLIP))
    mid = rel_bias[:, int(idx[n_diag - n_lo - 1]):int(idx[n_hi]) + 1][:, ::-1]
    diag = jnp.concatenate([jnp.broadcast_to(rel_bias[:, 2 * REL_CLIP:], (heads, n_hi)), mid,
                            jnp.broadcast_to(rel_bias[:, :1], (heads, n_lo))], axis=1)
    period = n_diag + 1
    v = jnp.roll(jnp.pad(diag, ((0, 0), (0, 1))), -(n_rows - 1), axis=1)
    skew = jnp.tile(v, (1, n_rows))[:, :n_rows * (period - 1)].reshape(heads, n_rows, period - 1)
    return skew[:, :, :n_cols].astype(F32)


def _prompt_bias(rel_bias):
    n_cols = ATT_QB + ATT_WINDOW
    r = np.arange(ATT_QB)[:, None]
    c = np.arange(n_cols)[None, :]
    band = c - (r // CHUNK) * CHUNK
    valid = (band >= 0) & (band < ATT_WINDOW + CHUNK)
    table = _rel_bias_table(rel_bias, ATT_QB, n_cols, ATT_WINDOW)
    return jnp.where(jnp.asarray(valid)[None], table, NEG_BIG)


def _sample_bias(rel_bias, t_new, cache_len):
    b = _rel_bias_table(rel_bias, t_new, cache_len + t_new, cache_len)
    return b[:, :, :cache_len], b[:, :, cache_len:]


def _mod_parts(mod, rows_each):
    parts = jnp.split(mod, 6, axis=-1)
    if rows_each == 1:
        return [p[:, None, :] for p in parts]
    return [jnp.repeat(p, rows_each, axis=0)[None] for p in parts]


def kernel(x_prompt, x_sample, cache_attn_k, cache_attn_v, state_ret, c_prompt, c_sample,
           norm1_g, norm2_g, w_ada, b_ada, w_in, rel_bias, w_o_attn, w_o_ret, w_out,
           w_router, b_router, w_exp_gate, w_exp_up, w_exp_down, w_sh_gate, w_sh_up, w_sh_down,
           normf_g):
    batch, seq, d = x_prompt.shape
    dec_batch, dec_seq, _ = x_sample.shape
    depth = w_in.shape[0]
    assert depth == 1 and d == D_MODEL
    assert seq % ROW_TILE == 0 and dec_batch * dec_seq == ROW_TILE and ROW_TILE == ATT_WINDOW
    cache_len = cache_attn_k.shape[2]
    n_p = batch * seq
    n_s = dec_batch * dec_seq
    tpb = seq // ROW_TILE
    l = 0

    bf = lambda a: a.astype(BF16)
    c_all = jnp.concatenate([c_prompt, c_sample], axis=0)
    pad = (-c_all.shape[0]) % 8
    c_all = jnp.pad(c_all, ((0, pad), (0, 0)))
    mod = _ada(c_all, bf(w_ada[l]), b_ada[l][None, :])
    mod_p = _mod_parts(mod[:batch], 1)
    mod_s = _mod_parts(mod[batch:batch + dec_batch], dec_seq)

    w_in_bf = bf(w_in[l])
    n1g = norm1_g[l][None, :]
    n2g = norm2_g[l][None, :]
    dense_w = [bf(w_o_attn[l]), bf(w_o_ret[l]), bf(w_out[l]), bf(w_router[l]).T,
               bf(w_sh_gate[l]), bf(w_sh_up[l]), bf(w_sh_down[l])]

    xp = x_prompt.reshape(n_p, d)
    xs_ = x_sample.reshape(n_s, d)
    cos_p, sin_p = _rotary_tables(jnp.arange(seq, dtype=F32))
    pos_s = PAST_LEN + jnp.arange(dec_seq, dtype=F32)
    cos_s, sin_s = _rotary_tables(jnp.tile(pos_s, dec_batch))

    (qa, ka, va, qr, kr, vr, gr, ga, gb, kv_p) = _inproj(
        xp, mod_p[1], mod_p[0], n1g, cos_p, sin_p, w_in_bf, tpb)
    oa = _attn_prompt(qa, ka, va, _prompt_bias(rel_bias[l]), batch, seq)
    zero_state = jnp.zeros((batch, N_HEADS, KEY_DIM_R, VAL_DIM_R), F32)
    yr_in, state_p = _retention(qr, kr, vr, gr, zero_state, batch, seq, RET_CHUNK)
    h_p, n2p_p, s_p = _outproj(xp, oa, yr_in, ga, gb, mod_p[2], mod_p[4], mod_p[3], mod_p[5], n2g,
                               dense_w, tpb, n_p + n_s, 0)

    (qa_s, ka_s, va_s, qr_s, kr_s, vr_s, gr_s, ga_s, gb_s, kv_s) = _inproj(
        xs_, mod_s[1], mod_s[0], n1g, cos_s, sin_s, w_in_bf, 1)
    bias_c, bias_n = _sample_bias(rel_bias[l], dec_seq, cache_len)
    to_keys_minor = lambda c: jnp.transpose(c, (0, 1, 3, 4, 2))
    oa_s = _attn_sample(qa_s, ka_s, va_s, to_keys_minor(cache_attn_k), to_keys_minor(cache_attn_v),
                        bias_c, bias_n, dec_batch, dec_seq, cache_len)
    yr_in_s, state_s = _retention(qr_s, kr_s, vr_s, gr_s, state_ret[l], dec_batch, dec_seq, dec_seq)
    h_s, n2p, scores_t = _outproj(xs_, oa_s, yr_in_s, ga_s, gb_s, mod_s[2], mod_s[4], mod_s[3],
                                  mod_s[5], n2g, dense_w, 1, n_p + n_s, n_p, carried=(n2p_p, s_p))

    lanes_of = lambda v: jnp.broadcast_to(v[:, None], (N_EXPERTS, ROUTE_TILE))
    idx_t, w_t, rank_t, counts = _route(scores_t, lanes_of(b_router[l]))
    (tile_expert, n_tiles, expert_slot, expert_next, tile_halves, pstart, pad_rows,
     n_sorted_rows) = _ffn_plan(counts[:, 0], (n_p + n_s) * TOP_K)
    dest_t = _dest(idx_t, rank_t, lanes_of(pstart.astype(F32)))
    dest_kmajor = dest_t.reshape(-1)
    w_route = jnp.pad(w_t.T, ((0, 0), (0, LANES - TOP_K)))
    xs_sorted = _sc_dispatch(n2p, dest_kmajor, pad_rows, n_sorted_rows)
    ys_sorted = _ffn(tile_expert, n_tiles, expert_slot, expert_next, tile_halves, xs_sorted,
                     w_exp_gate[l], w_exp_up[l], w_exp_down[l])
    nf = normf_g[None, :]
    y_by_k = _sc_gather_rows(ys_sorted, dest_kmajor).reshape(TOP_K, n_p + n_s, d // 2)
    y_p = _combine(w_route, h_p, mod_p[5], nf, y_by_k, 0, tpb)
    y_s = _combine(w_route, h_s, mod_s[5], nf, y_by_k, n_p, 1)

    keep = min(ATT_WINDOW, seq)
    kv_p = kv_p.reshape(batch, ROW_TILE, 2, N_HEADS, HEAD_DIM_A)[:, ROW_TILE - keep:]
    kv_s = kv_s.reshape(dec_batch, dec_seq, 2, N_HEADS, HEAD_DIM_A)
    return (y_p.reshape(batch, seq, d), y_s.reshape(dec_batch, dec_seq, d),
            kv_p[:, :, 0][None], kv_p[:, :, 1][None], state_p[None],
            kv_s[:, :, 0][None], kv_s[:, :, 1][None], state_s[None])
```

```python
import functools

import numpy as np
import jax
import jax.numpy as jnp
from jax import lax
from jax.experimental import pallas as pl
from jax.experimental.pallas import tpu as pltpu
from jax.experimental.pallas import tpu_sc as plsc

F32 = jnp.float32
BF16 = jnp.bfloat16
I32 = jnp.int32
U32 = jnp.uint32

D_MODEL = 1024
PAST_LEN = 4096
CHUNK = 64
N_LEFT_CHUNKS = 8
ATT_WINDOW = N_LEFT_CHUNKS * CHUNK
N_HEADS = 8
HEAD_DIM_A = 64
D_ATT = N_HEADS * HEAD_DIM_A
REL_CLIP = 128
KEY_DIM_R = 64
VAL_DIM_R = 128
D_RET_K = N_HEADS * KEY_DIM_R
D_RET_V = N_HEADS * VAL_DIM_R
ROPE_BASE = 10000.0
N_EXPERTS = 256
TOP_K = 8
N_GROUPS = 8
GROUP_SIZE = N_EXPERTS // N_GROUPS
TOPK_GROUPS = 4
D_EXPERT = 256
ROUTED_SCALE = 2.5
EPS = 1e-6
IN_WIDTHS = (D_ATT, D_ATT, D_ATT, D_RET_K, D_RET_K, D_RET_V, D_RET_V, D_MODEL, D_MODEL)
IN_OFFS = tuple(int(v) for v in np.cumsum((0,) + IN_WIDTHS))
D_IN = IN_OFFS[-1]

NEG_BIG = -1e30
LANES = 128
V7X_VMEM_BYTES = 64 * 1024 * 1024
V7X_SC_CORES = 2
V7X_SC_SUBCORES = 16
SC_GATHER_ROWS = 96
SPARE_ROWS = 8192

ROW_TILE = 512
ATT_QB = 256
RET_CHUNK = 256
ROUTE_TILE = 512
MOVE_TILE = 512
FFN_QUANTUM = 128
FFN_FINE = 64
FFN_FINE_FROM = 384
FFN_TILE = 5 * FFN_QUANTUM
FFN_TILE_SIZES = tuple(n for n in range(1, FFN_TILE // FFN_FINE + 1)
                       if n * FFN_FINE > FFN_FINE_FROM or (n * FFN_FINE) % FFN_QUANTUM == 0)
FFN_WEIGHT_SLOTS = 3
VMEM_RESERVE_BYTES = 6 << 20


def _cparams(semantics, vmem_mb):
    return pltpu.CompilerParams(dimension_semantics=semantics,
                                vmem_limit_bytes=min(vmem_mb << 20, V7X_VMEM_BYTES - VMEM_RESERVE_BYTES))


def _silu(x):
    return x * jax.nn.sigmoid(x)


def _pack_bf16_pair(lo, hi):
    return pltpu.bitcast(pltpu.pack_elementwise([lo, hi], packed_dtype=BF16), U32)


def _unpack_bf16_pair(u):
    words = pltpu.bitcast(u, I32)
    lo = pltpu.unpack_elementwise(words, index=0, packed_dtype=BF16, unpacked_dtype=F32)
    hi = pltpu.unpack_elementwise(words, index=1, packed_dtype=BF16, unpacked_dtype=F32)
    return lo, hi


def _ada_kernel(c_ref, w_ref, b_ref, o_ref):
    sc = _silu(c_ref[...]).astype(BF16)
    o_ref[...] = jnp.dot(sc, w_ref[...], preferred_element_type=F32) + b_ref[...]


def _ada(c_all, w_ada_bf, b_ada):
    rows = c_all.shape[0]
    n_out = w_ada_bf.shape[1]
    blk = D_MODEL
    return pl.pallas_call(
        _ada_kernel,
        grid=(n_out // blk,),
        in_specs=[pl.BlockSpec((rows, D_MODEL), lambda j: (0, 0)),
                  pl.BlockSpec((D_MODEL, blk), lambda j: (0, j)),
                  pl.BlockSpec((1, blk), lambda j: (0, j))],
        out_specs=pl.BlockSpec((rows, blk), lambda j: (0, j)),
        out_shape=jax.ShapeDtypeStruct((rows, n_out), F32),
        compiler_params=_cparams(("arbitrary",), 24),
        name="ada",
    )(c_all, w_ada_bf, b_ada)


def _inproj_kernel(x_ref, sc_ref, sh_ref, g_ref, cos_ref, sin_ref, w_ref,
                   qa_ref, ka_ref, va_ref, qr_ref, kr_ref, vr_ref, gr_ref, ga_ref, gb_ref,
                   kv_ref, *, tiles_per_batch):
    x = x_ref[...]
    xn = x * lax.rsqrt(jnp.mean(x * x, axis=-1, keepdims=True) + EPS) * g_ref[...]
    nb = (xn * (1.0 + sc_ref[0]) + sh_ref[0]).astype(BF16)

    def proj(seg):
        return jnp.dot(nb, w_ref[:, IN_OFFS[seg]:IN_OFFS[seg + 1]], preferred_element_type=F32)

    qa_ref[...] = proj(0).astype(BF16)
    ka = proj(1)
    va = proj(2)
    ka_ref[...] = ka.astype(BF16)
    va_ref[...] = va.astype(BF16)

    @pl.when(pl.program_id(0) % tiles_per_batch == tiles_per_batch - 1)
    def _():
        kv_ref[:, :D_ATT] = ka
        kv_ref[:, D_ATT:] = va

    cos = jnp.tile(cos_ref[...], (1, N_HEADS))
    sin = jnp.tile(sin_ref[...], (1, N_HEADS))
    first_half = (lax.broadcasted_iota(I32, (1, D_RET_K), 1) % KEY_DIM_R) < (KEY_DIM_R // 2)

    def rotary(t):
        partner = jnp.where(first_half, pltpu.roll(t, D_RET_K - KEY_DIM_R // 2, 1),
                            pltpu.roll(t, KEY_DIM_R // 2, 1))
        return t * cos + partner * sin

    qr_ref[...] = rotary(proj(3)).astype(BF16)
    kr_ref[...] = (rotary(proj(4)) * (KEY_DIM_R ** -0.5)).astype(BF16)
    vr_ref[...] = proj(5).astype(BF16)
    gr_ref[...] = proj(6).astype(BF16)
    ga_ref[...] = proj(7).astype(BF16)
    gb_ref[...] = proj(8).astype(BF16)


def _inproj(x2d, sc, sh, g, cos_t, sin_t, w_in_bf, tiles_per_batch):
    n = x2d.shape[0]
    tm = ROW_TILE
    n_tiles = n // tm
    n_batches = n_tiles // tiles_per_batch
    mod_rows = sc.shape[1]
    pos_tiles = cos_t.shape[0] // tm

    def row_spec(width):
        return pl.BlockSpec((tm, width), lambda i: (i, 0))

    mod_spec = pl.BlockSpec((1, mod_rows, D_MODEL), lambda i: (i // tiles_per_batch, 0, 0))
    pos_spec = pl.BlockSpec((tm, KEY_DIM_R), lambda i: (i % pos_tiles, 0))
    out_widths = (D_ATT, D_ATT, D_ATT, D_RET_K, D_RET_K, D_RET_V, D_RET_V, D_MODEL, D_MODEL)
    out_shape = [jax.ShapeDtypeStruct((n, w), BF16) for w in out_widths]
    out_shape.append(jax.ShapeDtypeStruct((n_batches * tm, 2 * D_ATT), F32))
    out_specs = [row_spec(w) for w in out_widths]
    out_specs.append(pl.BlockSpec((tm, 2 * D_ATT), lambda i: (i // tiles_per_batch, 0)))
    return pl.pallas_call(
        functools.partial(_inproj_kernel, tiles_per_batch=tiles_per_batch),
        grid=(n_tiles,),
        in_specs=[row_spec(D_MODEL), mod_spec, mod_spec,
                  pl.BlockSpec((1, D_MODEL), lambda i: (0, 0)),
                  pos_spec, pos_spec,
                  pl.BlockSpec((D_MODEL, D_IN), lambda i: (0, 0))],
        out_specs=out_specs,
        out_shape=out_shape,
        compiler_params=_cparams(("arbitrary",), 56),
        name="inproj",
    )(x2d, sc, sh, g, cos_t, sin_t, w_in_bf)


def _softmax_pv(s, v_parts):
    m = functools.reduce(jnp.maximum, [jnp.max(t, axis=-1, keepdims=True) for t in s])
    ps = [jnp.exp(t - m) for t in s]
    l = functools.reduce(jnp.add, [jnp.sum(p, axis=-1, keepdims=True) for p in ps])
    o = functools.reduce(jnp.add, [jnp.dot(p.astype(BF16), v, preferred_element_type=F32)
                                   for p, v in zip(ps, v_parts)])
    return o / l


def _attn_prompt_kernel(q_ref, k0_ref, k1_ref, k2_ref, v0_ref, v1_ref, v2_ref, bias_ref, o_ref):
    j = pl.program_id(1)
    q = q_ref[...]
    k = jnp.concatenate([k0_ref[...], k1_ref[...], k2_ref[...]], axis=0)
    v = jnp.concatenate([v0_ref[...], v1_ref[...], v2_ref[...]], axis=0)
    n_keys = k.shape[0]
    key_block = lax.broadcasted_iota(I32, (1, n_keys), 1) // ATT_QB
    before_start = jnp.where(key_block < 2 - j, NEG_BIG, 0.0)
    outs = []
    for h in range(N_HEADS):
        sl = slice(h * HEAD_DIM_A, (h + 1) * HEAD_DIM_A)
        qh = (q[:, sl].astype(F32) * (HEAD_DIM_A ** -0.5)).astype(BF16)
        s = lax.dot_general(qh, k[:, sl], (((1,), (1,)), ((), ())), preferred_element_type=F32)
        s = s + bias_ref[h] + before_start
        outs.append(_softmax_pv([s], [v[:, sl]]))
    o_ref[...] = jnp.concatenate(outs, axis=1).astype(BF16)


def _attn_prompt(q, k, v, bias_full, batch, seq):
    qb = ATT_QB
    nq = seq // qb

    def q_map(b, j):
        return (b * nq + j, 0)

    def kv_map(back):
        return lambda b, j: (b * nq + jnp.maximum(j - back, 0), 0)

    blk = lambda m: pl.BlockSpec((qb, D_ATT), m)
    return pl.pallas_call(
        _attn_prompt_kernel,
        grid=(batch, nq),
        in_specs=[blk(q_map), blk(kv_map(2)), blk(kv_map(1)), blk(kv_map(0)),
                  blk(kv_map(2)), blk(kv_map(1)), blk(kv_map(0)),
                  pl.BlockSpec(bias_full.shape, lambda b, j: (0, 0, 0))],
        out_specs=blk(q_map),
        out_shape=jax.ShapeDtypeStruct((batch * seq, D_ATT), BF16),
        compiler_params=_cparams(("parallel", "arbitrary"), 40),
        name="attn_prompt",
    )(q, k, k, k, v, v, v, bias_full)


SAMPLE_ATT_BATCHES = 4


def _attn_sample_kernel(q_ref, kn_ref, vn_ref, ck_ref, cv_ref, bc_ref, bn_ref, o_ref, *, t_new):
    nt = (((1,), (1,)), ((), ()))
    for b in range(SAMPLE_ATT_BATCHES):
        rows = slice(b * t_new, (b + 1) * t_new)
        q = q_ref[rows, :]
        kn = kn_ref[rows, :]
        vn = vn_ref[rows, :]
        outs = []
        for h in range(N_HEADS):
            sl = slice(h * HEAD_DIM_A, (h + 1) * HEAD_DIM_A)
            qh = (q[:, sl].astype(F32) * (HEAD_DIM_A ** -0.5)).astype(BF16)
            kc_t = ck_ref[b, h].astype(BF16)
            vc_t = cv_ref[b, h].astype(BF16)
            s_c = jnp.dot(qh, kc_t, preferred_element_type=F32) + bc_ref[h]
            s_n = lax.dot_general(qh, kn[:, sl], nt, preferred_element_type=F32) + bn_ref[h]
            m = jnp.maximum(jnp.max(s_c, axis=-1, keepdims=True), jnp.max(s_n, axis=-1, keepdims=True))
            p_c = jnp.exp(s_c - m)
            p_n = jnp.exp(s_n - m)
            l = jnp.sum(p_c, axis=-1, keepdims=True) + jnp.sum(p_n, axis=-1, keepdims=True)
            o = (lax.dot_general(p_c.astype(BF16), vc_t, nt, preferred_element_type=F32)
                 + jnp.dot(p_n.astype(BF16), vn[:, sl], preferred_element_type=F32))
            outs.append(o / l)
        o_ref[rows, :] = jnp.concatenate(outs, axis=1).astype(BF16)


def _attn_sample(q, k, v, cache_k_t, cache_v_t, bias_cache, bias_new, batch, t_new, cache_len):
    nb = SAMPLE_ATT_BATCHES
    blk = pl.BlockSpec((nb * t_new, D_ATT), lambda b: (b, 0))
    cblk = pl.BlockSpec((None, nb, N_HEADS, HEAD_DIM_A, cache_len), lambda b: (0, b, 0, 0, 0))
    return pl.pallas_call(
        functools.partial(_attn_sample_kernel, t_new=t_new),
        grid=(batch // nb,),
        in_specs=[blk, blk, blk, cblk, cblk,
                  pl.BlockSpec(bias_cache.shape, lambda b: (0, 0, 0)),
                  pl.BlockSpec(bias_new.shape, lambda b: (0, 0, 0))],
        out_specs=blk,
        out_shape=jax.ShapeDtypeStruct((batch * t_new, D_ATT), BF16),
        compiler_params=_cparams(("arbitrary",), 40),
        name="attn_sample",
    )(q, k, v, cache_k_t, cache_v_t, bias_cache, bias_new)


def _ret_kernel(q_ref, k_ref, v_ref, g_ref, s0_ref, dmask_ref, qdec_ref, kdec_ref, sdec_ref,
                y_ref, sout_ref, state_ref):
    c = pl.program_id(1)

    @pl.when(c == 0)
    def _():
        state_ref[...] = s0_ref[0]

    q = q_ref[...]
    k = k_ref[...]
    v = v_ref[...]
    g = g_ref[...]
    outs = []
    for h in range(N_HEADS):
        ks = slice(h * KEY_DIM_R, (h + 1) * KEY_DIM_R)
        vs = slice(h * VAL_DIM_R, (h + 1) * VAL_DIM_R)
        qh, kh, vh = q[:, ks], k[:, ks], v[:, vs]
        scores = lax.dot_general(qh, kh, (((1,), (1,)), ((), ())), preferred_element_type=F32)
        inner = jnp.dot((scores * dmask_ref[h]).astype(BF16), vh, preferred_element_type=F32)
        state = state_ref[h]
        cross = jnp.dot(qh, state.astype(BF16), preferred_element_type=F32) * qdec_ref[h]
        o = inner + cross
        v_dec = (vh.astype(F32) * kdec_ref[h]).astype(BF16)
        state_ref[h] = sdec_ref[h] * state + lax.dot_general(
            kh, v_dec, (((0,), (0,)), ((), ())), preferred_element_type=F32)
        on = o * lax.rsqrt(jnp.mean(o * o, axis=-1, keepdims=True) + EPS)
        outs.append(on * _silu(g[:, vs].astype(F32)))
    y_ref[...] = jnp.concatenate(outs, axis=1).astype(BF16)

    @pl.when(c == pl.num_programs(1) - 1)
    def _():
        sout_ref[0] = state_ref[...]


def _ret_tables(chunk):
    log_g = jnp.log(1.0 - jnp.exp2(-5.0 - jnp.arange(N_HEADS, dtype=F32)))
    i = jnp.arange(chunk, dtype=F32)
    diff = i[:, None] - i[None, :]
    dmask = jnp.where(diff >= 0, jnp.exp(log_g[:, None, None] * jnp.maximum(diff, 0.0)), 0.0)
    qdec = jnp.exp(log_g[:, None] * (i + 1.0))
    kdec = jnp.exp(log_g[:, None] * (chunk - 1.0 - i))
    sdec = jnp.exp(log_g * chunk)
    bc = lambda t: jnp.broadcast_to(t[:, :, None], (N_HEADS, t.shape[1], VAL_DIM_R)).astype(F32)
    sdec_t = jnp.broadcast_to(sdec[:, None, None], (N_HEADS, 1, VAL_DIM_R)).astype(F32)
    return dmask.astype(F32), bc(qdec), bc(kdec), sdec_t


def _retention(q, k, v, gate, state0, batch, seq, chunk):
    nc = seq // chunk
    dmask, qdec, kdec, sdec = _ret_tables(chunk)
    row = lambda w: pl.BlockSpec((chunk, w), lambda b, c: (b * nc + c, 0))
    const = lambda a: pl.BlockSpec(a.shape, lambda b, c: (0,) * a.ndim)
    st_spec = pl.BlockSpec((1, N_HEADS, KEY_DIM_R, VAL_DIM_R), lambda b, c: (b, 0, 0, 0))
    return pl.pallas_call(
        _ret_kernel,
        grid=(batch, nc),
        in_specs=[row(D_RET_K), row(D_RET_K), row(D_RET_V), row(D_RET_V), st_spec,
                  const(dmask), const(qdec), const(kdec), const(sdec)],
        out_specs=[row(D_RET_V), st_spec],
        out_shape=[jax.ShapeDtypeStruct((batch * seq, D_RET_V), BF16),
                   jax.ShapeDtypeStruct((batch, N_HEADS, KEY_DIM_R, VAL_DIM_R), F32)],
        scratch_shapes=[pltpu.VMEM((N_HEADS, KEY_DIM_R, VAL_DIM_R), F32)],
        compiler_params=_cparams(("parallel", "arbitrary"), 32),
        name="retention",
    )(q, k, v, gate, state0, dmask, qdec, kdec, sdec)


def _outproj_kernel(x_ref, oa_ref, yr_ref, ga_ref, gb_ref, g1_ref, sc2_ref, sh2_ref, g2_ref, n2g_ref,
                    woa_ref, wor_ref, wout_ref, wrt_ref, wsg_ref, wsu_ref, wsd_ref, *rest):
    h_ref, n2p_ref, s_ref = rest[-3:]
    ya = jnp.dot(oa_ref[...], woa_ref[...], preferred_element_type=F32)
    yr = jnp.dot(yr_ref[...], wor_ref[...], preferred_element_type=F32)
    merged = (jax.nn.sigmoid(ga_ref[...].astype(F32)) * ya
              + jax.nn.sigmoid(gb_ref[...].astype(F32)) * yr)
    mix = jnp.dot(merged.astype(BF16), wout_ref[...], preferred_element_type=F32)
    h = x_ref[...] + g1_ref[0] * mix
    hn = h * lax.rsqrt(jnp.mean(h * h, axis=-1, keepdims=True) + EPS) * n2g_ref[...]
    n2 = hn * (1.0 + sc2_ref[0]) + sh2_ref[0]
    n2b = n2.astype(BF16)
    s_ref[...] = jax.nn.sigmoid(lax.dot_general(wrt_ref[...], n2b, (((1,), (1,)), ((), ())),
                                                preferred_element_type=F32))
    hid = _silu(jnp.dot(n2b, wsg_ref[...], preferred_element_type=F32)) * jnp.dot(
        n2b, wsu_ref[...], preferred_element_type=F32)
    shared = jnp.dot(hid.astype(BF16), wsd_ref[...], preferred_element_type=F32)
    h_ref[...] = h + g2_ref[0] * shared
    half = D_MODEL // 2
    n2p_ref[...] = _pack_bf16_pair(n2[:, :half], n2[:, half:])


def _outproj(x2d, oa, yr_in, ga, gb, g1, sc2, sh2, g2, n2g, weights, tiles_per_batch,
             all_tokens, token_offset, carried=None):
    n = x2d.shape[0]
    tm = ROW_TILE
    off = token_offset // tm
    mod_rows = g1.shape[1]
    row = lambda w: pl.BlockSpec((tm, w), lambda i: (i, 0))
    mod_spec = pl.BlockSpec((1, mod_rows, D_MODEL), lambda i: (i // tiles_per_batch, 0, 0))
    const = lambda a: pl.BlockSpec(a.shape, lambda i: (0,) * a.ndim)
    in_specs = [row(D_MODEL), row(D_ATT), row(D_RET_V), row(D_MODEL), row(D_MODEL),
                mod_spec, mod_spec, mod_spec, mod_spec, const(n2g)] + [const(w) for w in weights]
    args = [x2d, oa, yr_in, ga, gb, g1, sc2, sh2, g2, n2g, *weights]
    aliases = {}
    if carried is not None:
        aliases = {len(args): 1, len(args) + 1: 2}
        in_specs += [pl.BlockSpec(memory_space=pl.ANY)] * 2
        args += list(carried)
    return pl.pallas_call(
        _outproj_kernel,
        grid=(n // tm,),
        in_specs=in_specs,
        out_specs=[row(D_MODEL),
                   pl.BlockSpec((tm, D_MODEL // 2), lambda i: (i + off, 0)),
                   pl.BlockSpec((N_EXPERTS, tm), lambda i: (0, i + off))],
        out_shape=[jax.ShapeDtypeStruct((n, D_MODEL), F32),
                   jax.ShapeDtypeStruct((all_tokens, D_MODEL // 2), U32),
                   jax.ShapeDtypeStruct((N_EXPERTS, all_tokens), F32)],
        input_output_aliases=aliases,
        compiler_params=_cparams(("arbitrary",), 48),
        name="outproj",
    )(*args)


def _route_kernel(s_ref, b_ref, idx_ref, w_ref, rank_ref, cnt_ref, run_ref, tri_ref):
    step = pl.program_id(0)
    t = s_ref.shape[1]

    @pl.when(step == 0)
    def _():
        run_ref[...] = jnp.zeros_like(run_ref)
        r = lax.broadcasted_iota(I32, (t, t), 0)
        c = lax.broadcasted_iota(I32, (t, t), 1)
        tri_ref[...] = jnp.where(r < c, 1.0, 0.0).astype(BF16)

    s = s_ref[...]
    sel = s + b_ref[...]
    row_f = lax.broadcasted_iota(I32, (N_EXPERTS, t), 0).astype(F32)

    def first_argmax(vals, rows):
        m = jnp.max(vals, axis=0, keepdims=True)
        pos = jnp.min(jnp.where(vals == m, rows, float(N_EXPERTS)), axis=0, keepdims=True)
        return m, pos

    gscore = []
    group_row = lax.broadcasted_iota(I32, (GROUP_SIZE, t), 0).astype(F32)
    for g in range(N_GROUPS):
        rows = slice(g * GROUP_SIZE, (g + 1) * GROUP_SIZE)
        m1, p1 = first_argmax(sel[rows], group_row)
        m2 = jnp.max(jnp.where(group_row == p1, -jnp.inf, sel[rows]), axis=0, keepdims=True)
        gscore.append(m1 + m2)
    cand_parts = []
    for g in range(N_GROUPS):
        rows = slice(g * GROUP_SIZE, (g + 1) * GROUP_SIZE)
        beaten_by = jnp.zeros((1, t), F32)
        for o in range(N_GROUPS):
            if o == g:
                continue
            wins = (gscore[o] > gscore[g]) if o > g else (gscore[o] >= gscore[g])
            beaten_by = beaten_by + jnp.where(wins, 1.0, 0.0)
        cand_parts.append(jnp.where(beaten_by < TOPK_GROUPS, sel[rows], -jnp.inf))
    cand = jnp.concatenate(cand_parts, axis=0)

    picked = jnp.zeros((N_EXPERTS, t), F32)
    idx_rows, w_rows = [], []
    for _ in range(TOP_K):
        _, pos = first_argmax(cand, row_f)
        hit = row_f == pos
        w_rows.append(jnp.sum(jnp.where(hit, s, 0.0), axis=0, keepdims=True))
        idx_rows.append(pos)
        picked = jnp.where(hit, 1.0, picked)
        cand = jnp.where(hit, -jnp.inf, cand)
    w_sum = functools.reduce(jnp.add, w_rows)

    before = jnp.dot(picked.astype(BF16), tri_ref[...], preferred_element_type=F32) + run_ref[...]
    run_ref[...] = run_ref[...] + jnp.sum(picked, axis=1, keepdims=True)
    rank_rows = [jnp.sum(jnp.where(row_f == idx_rows[kk], before, 0.0), axis=0, keepdims=True)
                 for kk in range(TOP_K)]

    idx_ref[...] = jnp.concatenate(idx_rows, axis=0).astype(I32)
    w_ref[...] = jnp.concatenate([w / w_sum * ROUTED_SCALE for w in w_rows], axis=0)
    rank_ref[...] = jnp.concatenate(rank_rows, axis=0).astype(I32)

    @pl.when(step == pl.num_programs(0) - 1)
    def _():
        cnt_ref[...] = run_ref[...].astype(I32)


def _route(scores_t, b_col):
    n = scores_t.shape[1]
    t = ROUTE_TILE
    col = pl.BlockSpec((TOP_K, t), lambda i: (0, i))
    const = pl.BlockSpec((N_EXPERTS, t), lambda i: (0, 0))
    return pl.pallas_call(
        _route_kernel,
        grid=(n // t,),
        in_specs=[pl.BlockSpec((N_EXPERTS, t), lambda i: (0, i)), const],
        out_specs=[col, col, col, const],
        out_shape=[jax.ShapeDtypeStruct((TOP_K, n), I32),
                   jax.ShapeDtypeStruct((TOP_K, n), F32),
                   jax.ShapeDtypeStruct((TOP_K, n), I32),
                   jax.ShapeDtypeStruct((N_EXPERTS, t), I32)],
        scratch_shapes=[pltpu.VMEM((N_EXPERTS, t), F32), pltpu.VMEM((t, t), BF16)],
        compiler_params=_cparams(("arbitrary",), 32),
        name="route",
    )(scores_t, b_col)


def _dest_kernel(idx_ref, rank_ref, start_ref, dest_ref):
    t = idx_ref.shape[1]
    row = lax.broadcasted_iota(I32, (N_EXPERTS, t), 0)
    starts = start_ref[...]
    base = [jnp.sum(jnp.where(row == idx_ref[kk:kk + 1, :], starts, 0.0), axis=0, keepdims=True)
            for kk in range(TOP_K)]
    dest_ref[...] = jnp.concatenate(base, axis=0).astype(I32) + rank_ref[...]


def _dest(idx_t, rank_t, starts_col):
    n = idx_t.shape[1]
    t = ROUTE_TILE
    col = pl.BlockSpec((TOP_K, t), lambda i: (0, i))
    return pl.pallas_call(
        _dest_kernel,
        grid=(n // t,),
        in_specs=[col, col, pl.BlockSpec((N_EXPERTS, t), lambda i: (0, 0))],
        out_specs=col,
        out_shape=jax.ShapeDtypeStruct((TOP_K, n), I32),
        compiler_params=_cparams(("arbitrary",), 32),
        name="dest",
    )(idx_t, rank_t, starts_col)


def _sc_mesh():
    return plsc.VectorSubcoreMesh(core_axis_name="core", subcore_axis_name="subcore",
                                  num_cores=V7X_SC_CORES, num_subcores=V7X_SC_SUBCORES)


def _sc_dispatch(n2p, dest_kmajor, pad_rows, n_out_rows):
    n, width = n2p.shape
    rows = SC_GATHER_ROWS
    n_workers = V7X_SC_CORES * V7X_SC_SUBCORES
    src_chunks = n // rows
    items = dest_kmajor.shape[0] // rows
    per_worker = items // n_workers
    pad_per_worker = pad_rows.shape[0] // rows // n_workers
    assert src_chunks * rows == n and per_worker * n_workers == items and per_worker % 2 == 0
    assert pad_per_worker * n_workers * rows == pad_rows.shape[0]
    idx3 = dest_kmajor.reshape(n_workers, per_worker, rows)
    pad3 = pad_rows.reshape(n_workers, pad_per_worker, rows)
    zeros = jnp.zeros((rows, width), n2p.dtype)

    def body(src_hbm, idx_hbm, pad_hbm, zero_hbm, out_hbm, idx_v, pad_v, rows_v, load_sem, scat_sem):
        worker = lax.axis_index("subcore") * V7X_SC_CORES + lax.axis_index("core")
        pltpu.sync_copy(idx_hbm.at[worker], idx_v)
        pltpu.sync_copy(pad_hbm.at[worker], pad_v)
        pltpu.sync_copy(zero_hbm, rows_v.at[0])

        def zero_fill(c):
            return pltpu.make_async_copy(rows_v.at[0], out_hbm.at[pad_v.at[c]], scat_sem.at[0])

        @pl.loop(0, pad_per_worker)
        def _(c):
            zero_fill(c).start()

        @pl.loop(0, pad_per_worker)
        def _(c):
            zero_fill(c).wait()

        def load(c, b):
            chunk = lax.rem(worker * per_worker + c, src_chunks)
            off = pl.multiple_of(chunk * rows, rows)
            return pltpu.make_async_copy(src_hbm.at[pl.ds(off, rows)], rows_v.at[b], load_sem.at[b])

        def scatter(c, b):
            return pltpu.make_async_copy(rows_v.at[b], out_hbm.at[idx_v.at[c]], scat_sem.at[b])

        load(0, 0).start()

        @pl.loop(0, per_worker, step=2)
        def _(c0):
            for b in range(2):
                c = c0 + b
                load(c, b).wait()

                @pl.when(c >= 1)
                def _():
                    scatter(c - 1, 1 - b).wait()

                @pl.when(c + 1 < per_worker)
                def _():
                    load(c + 1, 1 - b).start()

                scatter(c, b).start()

        scatter(per_worker - 1, (per_worker - 1) % 2).wait()

    return pl.kernel(
        body, mesh=_sc_mesh(),
        out_type=jax.ShapeDtypeStruct((n_out_rows, width), n2p.dtype),
        scratch_types=[pltpu.VMEM((per_worker, rows), I32),
                       pltpu.VMEM((pad_per_worker, rows), I32),
                       pltpu.VMEM((2, rows, width), n2p.dtype),
                       pltpu.SemaphoreType.DMA((2,)),
                       pltpu.SemaphoreType.DMA((2,))],
        name="sc_dispatch",
    )(n2p, idx3, pad3, zeros)


def _ffn_kernel(texp_ref, ntiles_ref, eslot_ref, enext_ref, nhalf_ref, xs_ref, wg_hbm, wu_hbm, wd_hbm,
                ys_ref, wg_buf, wu_buf, wd_buf, wgu_bf, wd_bf, sems):
    g = pl.program_id(0)

    def weight_copies(e, slot):
        return (pltpu.make_async_copy(wg_hbm.at[e], wg_buf.at[slot], sems.at[slot, 0]),
                pltpu.make_async_copy(wu_hbm.at[e], wu_buf.at[slot], sems.at[slot, 1]),
                pltpu.make_async_copy(wd_hbm.at[e], wd_buf.at[slot], sems.at[slot, 2]))

    @pl.when(g < ntiles_ref[0])
    def _():
        e = texp_ref[g]
        changed = jnp.logical_or(g == 0, texp_ref[jnp.maximum(g - 1, 0)] != e)

        @pl.when(changed)
        def _():
            slot = eslot_ref[e]

            def fetch_ahead(first_hop, hops, target_slot):
                ahead = first_hop
                for _ in range(hops - 1):
                    ahead = jnp.where(ahead >= 0, enext_ref[jnp.maximum(ahead, 0)], -1)

                @pl.when(ahead >= 0)
                def _():
                    for c in weight_copies(ahead, target_slot):
                        c.start()

            @pl.when(g == 0)
            def _():
                for c in weight_copies(e, slot):
                    c.start()
                for hops in range(1, FFN_WEIGHT_SLOTS - 1):
                    fetch_ahead(enext_ref[e], hops, lax.rem(slot + hops, FFN_WEIGHT_SLOTS))

            fetch_ahead(enext_ref[e], FFN_WEIGHT_SLOTS - 1,
                        lax.rem(slot + FFN_WEIGHT_SLOTS - 1, FFN_WEIGHT_SLOTS))
            for c in weight_copies(e, slot):
                c.wait()

            wgu_bf[:, :D_EXPERT] = wg_buf[slot].astype(BF16)
            wgu_bf[:, D_EXPERT:] = wu_buf[slot].astype(BF16)
            wd_bf[...] = wd_buf[slot].astype(BF16)

        def expert_rows(rows):
            lo, hi = _unpack_bf16_pair(xs_ref[rows, :])
            x = jnp.concatenate([lo, hi], axis=1).astype(BF16)
            gu = jnp.dot(x, wgu_bf[...], preferred_element_type=F32)
            hid = (_silu(gu[:, :D_EXPERT]) * gu[:, D_EXPERT:]).astype(BF16)
            y = jnp.dot(hid, wd_bf[...], preferred_element_type=F32)
            half = D_MODEL // 2
            ys_ref[rows, :] = _pack_bf16_pair(y[:, :half], y[:, half:])

        for n_groups in FFN_TILE_SIZES:

            @pl.when(nhalf_ref[g] == n_groups)
            def _(n_groups=n_groups):
                used = n_groups * FFN_FINE
                expert_rows(slice(0, used))
                if used < FFN_TILE:
                    ys_ref[used:, :] = jnp.zeros((FFN_TILE - used, ys_ref.shape[1]), U32)


def _ffn(tile_expert, n_tiles, expert_slot, expert_next, tile_halves, xs, w_gate, w_up, w_down):
    rows, width = xs.shape
    m = FFN_TILE
    max_tiles = tile_expert.shape[0]
    row_map = lambda g, te, nt, es, en, nh: (jnp.minimum(g, nt[0] - 1), 0)
    hbm = pl.BlockSpec(memory_space=pl.ANY)
    grid_spec = pltpu.PrefetchScalarGridSpec(
        num_scalar_prefetch=5,
        grid=(max_tiles,),
        in_specs=[pl.BlockSpec((m, width), row_map), hbm, hbm, hbm],
        out_specs=pl.BlockSpec((m, width), row_map),
        scratch_shapes=[pltpu.VMEM((FFN_WEIGHT_SLOTS, D_MODEL, D_EXPERT), F32),
                        pltpu.VMEM((FFN_WEIGHT_SLOTS, D_MODEL, D_EXPERT), F32),
                        pltpu.VMEM((FFN_WEIGHT_SLOTS, D_EXPERT, D_MODEL), F32),
                        pltpu.VMEM((D_MODEL, 2 * D_EXPERT), BF16),
                        pltpu.VMEM((D_EXPERT, D_MODEL), BF16),
                        pltpu.SemaphoreType.DMA((FFN_WEIGHT_SLOTS, 3))],
    )
    return pl.pallas_call(
        _ffn_kernel,
        grid_spec=grid_spec,
        out_shape=jax.ShapeDtypeStruct((rows, width), U32),
        compiler_params=_cparams(("arbitrary",), 32),
        name="ffn",
    )(tile_expert, n_tiles, expert_slot, expert_next, tile_halves, xs, w_gate, w_up, w_down)


def _ffn_plan(counts, n_assign):
    m = FFN_TILE
    max_tiles = n_assign // m + N_EXPERTS
    padded = ((counts + m - 1) // m) * m
    pend = jnp.cumsum(padded).astype(I32)
    pstart = pend - padded
    n_tiles = pend[-1:] // m
    g = jnp.minimum(jnp.arange(max_tiles, dtype=I32), n_tiles - 1)
    tile_expert = jnp.sum((pend[None, :] <= (g * m)[:, None]).astype(I32), axis=1)
    tile_expert = jnp.minimum(tile_expert, N_EXPERTS - 1)
    full_rows = jnp.maximum(padded - m, 0)
    last_rows = counts - full_rows
    round_to = lambda v, q: ((v + q - 1) // q) * q
    last_rows = jnp.where(last_rows > FFN_FINE_FROM, round_to(last_rows, FFN_FINE),
                          round_to(last_rows, FFN_QUANTUM))
    vend = pstart + full_rows + last_rows
    own = tile_expert[:, None] == jnp.arange(N_EXPERTS, dtype=I32)[None, :]
    tile_vend = jnp.sum(jnp.where(own, vend[None, :], 0), axis=1)
    tile_halves = (jnp.clip(tile_vend - g * m, 0, m) // FFN_FINE).astype(I32)
    used = counts > 0
    expert_slot = ((jnp.cumsum(used.astype(I32)) - 1) % FFN_WEIGHT_SLOTS).astype(I32)
    ids = jnp.where(used, jnp.arange(N_EXPERTS, dtype=I32), N_EXPERTS)
    first_used_from = lax.cummin(ids, axis=0, reverse=True)
    nxt = jnp.concatenate([first_used_from[1:], jnp.full((1,), N_EXPERTS, I32)])
    expert_next = jnp.where(nxt < N_EXPERTS, nxt, -1).astype(I32)
    spare_row = max_tiles * m
    j = jnp.arange(FFN_QUANTUM, dtype=I32)[None, :]
    seg_end = (pstart + counts)[:, None]
    spare = spare_row + (jnp.arange(N_EXPERTS, dtype=I32)[:, None] * FFN_QUANTUM + j) % SPARE_ROWS
    pad_rows = jnp.where(j < (vend[:, None] - seg_end), seg_end + j, spare).astype(I32).reshape(-1)
    unit = SC_GATHER_ROWS * V7X_SC_CORES * V7X_SC_SUBCORES
    extra = (-pad_rows.shape[0]) % unit
    filler = spare_row + (pad_rows.shape[0] + jnp.arange(extra, dtype=I32)) % SPARE_ROWS
    pad_rows = jnp.concatenate([pad_rows, filler])
    return (tile_expert, n_tiles, expert_slot, expert_next, tile_halves, pstart, pad_rows,
            spare_row + SPARE_ROWS)


def _sc_gather_rows(table, idx):
    n_idx = idx.shape[0]
    width = table.shape[1]
    n_workers = V7X_SC_CORES * V7X_SC_SUBCORES
    per_worker = n_idx // n_workers
    n_chunks = per_worker // SC_GATHER_ROWS
    assert per_worker * n_workers == n_idx and n_chunks * SC_GATHER_ROWS == per_worker and n_chunks % 2 == 0

    def body(table_hbm, idx_hbm, out_hbm, idx_v, rows_v, gather_sem, write_sem):
        worker = lax.axis_index("subcore") * V7X_SC_CORES + lax.axis_index("core")
        base = worker * per_worker
        pltpu.sync_copy(idx_hbm.at[pl.ds(base, per_worker)], idx_v)

        def gather(c, b):
            off = pl.multiple_of(c * SC_GATHER_ROWS, SC_GATHER_ROWS)
            return pltpu.make_async_copy(table_hbm.at[idx_v.at[pl.ds(off, SC_GATHER_ROWS)]],
                                         rows_v.at[b], gather_sem.at[b])

        def write(c, b):
            off = pl.multiple_of(c * SC_GATHER_ROWS, SC_GATHER_ROWS)
            return pltpu.make_async_copy(rows_v.at[b], out_hbm.at[pl.ds(base + off, SC_GATHER_ROWS)],
                                         write_sem.at[b])

        gather(0, 0).start()

        @pl.loop(0, n_chunks, step=2)
        def _(c0):
            for b in range(2):
                c = c0 + b
                gather(c, b).wait()

                @pl.when(c >= 1)
                def _():
                    write(c - 1, 1 - b).wait()

                @pl.when(c + 1 < n_chunks)
                def _():
                    gather(c + 1, 1 - b).start()

                write(c, b).start()

        write(n_chunks - 1, (n_chunks - 1) % 2).wait()

    return pl.kernel(
        body, mesh=_sc_mesh(),
        out_type=jax.ShapeDtypeStruct((n_idx, width), table.dtype),
        scratch_types=[pltpu.VMEM((per_worker,), I32),
                       pltpu.VMEM((2, SC_GATHER_ROWS, width), table.dtype),
                       pltpu.SemaphoreType.DMA((2,)),
                       pltpu.SemaphoreType.DMA((2,))],
        name="sc_gather_rows",
    )(table, idx)


def _combine_kernel(w_ref, h_ref, g2_ref, nf_ref, yk_ref, y_ref):
    t = h_ref.shape[0]
    w = w_ref[...]
    acc_lo = jnp.zeros((t, D_MODEL // 2), F32)
    acc_hi = jnp.zeros((t, D_MODEL // 2), F32)
    for kk in range(TOP_K):
        lo, hi = _unpack_bf16_pair(yk_ref[kk])
        wk = w[:, kk:kk + 1]
        acc_lo = acc_lo + wk * lo
        acc_hi = acc_hi + wk * hi
    out = h_ref[...] + g2_ref[0] * jnp.concatenate([acc_lo, acc_hi], axis=1)
    y_ref[...] = out * lax.rsqrt(jnp.mean(out * out, axis=-1, keepdims=True) + EPS) * nf_ref[...]


def _combine(w, h2, g2, normf, y_by_k, row_offset, tiles_per_batch):
    n = h2.shape[0]
    t = MOVE_TILE
    off = row_offset // t
    mod_rows = g2.shape[1]
    mod_tiles = max(ROW_TILE // t, 1) * tiles_per_batch if mod_rows == 1 else n // t
    return pl.pallas_call(
        _combine_kernel,
        grid=(n // t,),
        in_specs=[pl.BlockSpec((t, LANES), lambda i: (i + off, 0)),
                  pl.BlockSpec((t, D_MODEL), lambda i: (i, 0)),
                  pl.BlockSpec((1, mod_rows if mod_rows == 1 else t, D_MODEL),
                               (lambda i: (i // mod_tiles, 0, 0)) if mod_rows == 1
                               else (lambda i: (0, i, 0))),
                  pl.BlockSpec((1, D_MODEL), lambda i: (0, 0)),
                  pl.BlockSpec((TOP_K, t, D_MODEL // 2), lambda i: (0, i + off, 0))],
        out_specs=pl.BlockSpec((t, D_MODEL), lambda i: (i, 0)),
        out_shape=jax.ShapeDtypeStruct((n, D_MODEL), F32),
        compiler_params=_cparams(("arbitrary",), 40),
        name="combine",
    )(w, h2, g2, normf, y_by_k)


def _rotary_tables(pos):
    half = KEY_DIM_R // 2
    inv_freq = ROPE_BASE ** (-jnp.arange(half, dtype=F32) / half)
    ang = pos[:, None] * inv_freq[None, :]
    cos = jnp.cos(ang)
    sin = jnp.sin(ang)
    cos_t = jnp.concatenate([cos, cos], axis=1)
    sin_t = jnp.concatenate([-sin, sin], axis=1)
    return cos_t.astype(F32), sin_t.astype(F32)


def _rel_bias_table(rel_bias, n_rows, n_cols, q_offset):
    heads = rel_bias.shape[0]
    n_diag = n_rows + n_cols - 1
    dist = q_offset + (n_rows - 1) - np.arange(n_diag)
    idx = np.clip(dist, -REL_CLIP, REL_CLIP) + REL_CLIP
    n_hi = int(np.sum(dist > REL_CLIP))
    n_lo = int(np.sum(dist < -REL_CLIP))
    mid = rel_bias[:, int(idx[n_diag - n_lo - 1]):int(idx[n_hi]) + 1][:, ::-1]
    diag = jnp.concatenate([jnp.broadcast_to(rel_bias[:, 2 * REL_CLIP:], (heads, n_hi)), mid,
                            jnp.broadcast_to(rel_bias[:, :1], (heads, n_lo))], axis=1)
    period = n_diag + 1
    v = jnp.roll(jnp.pad(diag, ((0, 0), (0, 1))), -(n_rows - 1), axis=1)
    skew = jnp.tile(v, (1, n_rows))[:, :n_rows * (period - 1)].reshape(heads, n_rows, period - 1)
    return skew[:, :, :n_cols].astype(F32)


def _prompt_bias(rel_bias):
    n_cols = ATT_QB + ATT_WINDOW
    r = np.arange(ATT_QB)[:, None]
    c = np.arange(n_cols)[None, :]
    band = c - (r // CHUNK) * CHUNK
    valid = (band >= 0) & (band < ATT_WINDOW + CHUNK)
    table = _rel_bias_table(rel_bias, ATT_QB, n_cols, ATT_WINDOW)
    return jnp.where(jnp.asarray(valid)[None], table, NEG_BIG)


def _sample_bias(rel_bias, t_new, cache_len):
    b = _rel_bias_table(rel_bias, t_new, cache_len + t_new, cache_len)
    return b[:, :, :cache_len], b[:, :, cache_len:]


def _mod_parts(mod, rows_each):
    parts = jnp.split(mod, 6, axis=-1)
    if rows_each == 1:
        return [p[:, None, :] for p in parts]
    return [jnp.repeat(p, rows_each, axis=0)[None] for p in parts]


def kernel(x_prompt, x_sample, cache_attn_k, cache_attn_v, state_ret, c_prompt, c_sample,
           norm1_g, norm2_g, w_ada, b_ada, w_in, rel_bias, w_o_attn, w_o_ret, w_out,
           w_router, b_router, w_exp_gate, w_exp_up, w_exp_down, w_sh_gate, w_sh_up, w_sh_down,
           normf_g):
    batch, seq, d = x_prompt.shape
    dec_batch, dec_seq, _ = x_sample.shape
    depth = w_in.shape[0]
    assert depth == 1 and d == D_MODEL
    assert seq % ROW_TILE == 0 and dec_batch * dec_seq == ROW_TILE and ROW_TILE == ATT_WINDOW
    cache_len = cache_attn_k.shape[2]
    n_p = batch * seq
    n_s = dec_batch * dec_seq
    tpb = seq // ROW_TILE
    l = 0

    bf = lambda a: a.astype(BF16)
    c_all = jnp.concatenate([c_prompt, c_sample], axis=0)
    pad = (-c_all.shape[0]) % 8
    c_all = jnp.pad(c_all, ((0, pad), (0, 0)))
    mod = _ada(c_all, bf(w_ada[l]), b_ada[l][None, :])
    mod_p = _mod_parts(mod[:batch], 1)
    mod_s = _mod_parts(mod[batch:batch + dec_batch], dec_seq)

    w_in_bf = bf(w_in[l])
    n1g = norm1_g[l][None, :]
    n2g = norm2_g[l][None, :]
    dense_w = [bf(w_o_attn[l]), bf(w_o_ret[l]), bf(w_out[l]), bf(w_router[l]).T,
               bf(w_sh_gate[l]), bf(w_sh_up[l]), bf(w_sh_down[l])]

    xp = x_prompt.reshape(n_p, d)
    xs_ = x_sample.reshape(n_s, d)
    cos_p, sin_p = _rotary_tables(jnp.arange(seq, dtype=F32))
    pos_s = PAST_LEN + jnp.arange(dec_seq, dtype=F32)
    cos_s, sin_s = _rotary_tables(jnp.tile(pos_s, dec_batch))

    (qa, ka, va, qr, kr, vr, gr, ga, gb, kv_p) = _inproj(
        xp, mod_p[1], mod_p[0], n1g, cos_p, sin_p, w_in_bf, tpb)
    oa = _attn_prompt(qa, ka, va, _prompt_bias(rel_bias[l]), batch, seq)
    zero_state = jnp.zeros((batch, N_HEADS, KEY_DIM_R, VAL_DIM_R), F32)
    yr_in, state_p = _retention(qr, kr, vr, gr, zero_state, batch, seq, RET_CHUNK)
    h_p, n2p_p, s_p = _outproj(xp, oa, yr_in, ga, gb, mod_p[2], mod_p[4], mod_p[3], mod_p[5], n2g,
                               dense_w, tpb, n_p + n_s, 0)

    (qa_s, ka_s, va_s, qr_s, kr_s, vr_s, gr_s, ga_s, gb_s, kv_s) = _inproj(
        xs_, mod_s[1], mod_s[0], n1g, cos_s, sin_s, w_in_bf, 1)
    bias_c, bias_n = _sample_bias(rel_bias[l], dec_seq, cache_len)
    to_keys_minor = lambda c: jnp.transpose(c, (0, 1, 3, 4, 2))
    oa_s = _attn_sample(qa_s, ka_s, va_s, to_keys_minor(cache_attn_k), to_keys_minor(cache_attn_v),
                        bias_c, bias_n, dec_batch, dec_seq, cache_len)
    yr_in_s, state_s = _retention(qr_s, kr_s, vr_s, gr_s, state_ret[l], dec_batch, dec_seq, dec_seq)
    h_s, n2p, scores_t = _outproj(xs_, oa_s, yr_in_s, ga_s, gb_s, mod_s[2], mod_s[4], mod_s[3],
                                  mod_s[5], n2g, dense_w, 1, n_p + n_s, n_p, carried=(n2p_p, s_p))

    lanes_of = lambda v: jnp.broadcast_to(v[:, None], (N_EXPERTS, ROUTE_TILE))
    idx_t, w_t, rank_t, counts = _route(scores_t, lanes_of(b_router[l]))
    (tile_expert, n_tiles, expert_slot, expert_next, tile_halves, pstart, pad_rows,
     n_sorted_rows) = _ffn_plan(counts[:, 0], (n_p + n_s) * TOP_K)
    dest_t = _dest(idx_t, rank_t, lanes_of(pstart.astype(F32)))
    dest_kmajor = dest_t.reshape(-1)
    w_route = jnp.pad(w_t.T, ((0, 0), (0, LANES - TOP_K)))
    xs_sorted = _sc_dispatch(n2p, dest_kmajor, pad_rows, n_sorted_rows)
    ys_sorted = _ffn(tile_expert, n_tiles, expert_slot, expert_next, tile_halves, xs_sorted,
                     w_exp_gate[l], w_exp_up[l], w_exp_down[l])
    nf = normf_g[None, :]
    y_by_k = _sc_gather_rows(ys_sorted, dest_kmajor).reshape(TOP_K, n_p + n_s, d // 2)
    y_p = _combine(w_route, h_p, mod_p[5], nf, y_by_k, 0, tpb)
    y_s = _combine(w_route, h_s, mod_s[5], nf, y_by_k, n_p, 1)

    keep = min(ATT_WINDOW, seq)
    kv_p = kv_p.reshape(batch, ROW_TILE, 2, N_HEADS, HEAD_DIM_A)[:, ROW_TILE - keep:]
    kv_s = kv_s.reshape(dec_batch, dec_seq, 2, N_HEADS, HEAD_DIM_A)
    return (y_p.reshape(batch, seq, d), y_s.reshape(dec_batch, dec_seq, d),
            kv_p[:, :, 0][None], kv_p[:, :, 1][None], state_p[None],
            kv_s[:, :, 0][None], kv_s[:, :, 1][None], state_s[None])
```

```python
import functools

import numpy as np
import jax
import jax.numpy as jnp
from jax import lax
from jax.experimental import pallas as pl
from jax.experimental.pallas import tpu as pltpu
from jax.experimental.pallas import tpu_sc as plsc

F32 = jnp.float32
BF16 = jnp.bfloat16
I32 = jnp.int32
U32 = jnp.uint32

D_MODEL = 1024
PAST_LEN = 4096
CHUNK = 64
N_LEFT_CHUNKS = 8
ATT_WINDOW = N_LEFT_CHUNKS * CHUNK
N_HEADS = 8
HEAD_DIM_A = 64
D_ATT = N_HEADS * HEAD_DIM_A
REL_CLIP = 128
KEY_DIM_R = 64
VAL_DIM_R = 128
D_RET_K = N_HEADS * KEY_DIM_R
D_RET_V = N_HEADS * VAL_DIM_R
ROPE_BASE = 10000.0
N_EXPERTS = 256
TOP_K = 8
N_GROUPS = 8
GROUP_SIZE = N_EXPERTS // N_GROUPS
TOPK_GROUPS = 4
D_EXPERT = 256
ROUTED_SCALE = 2.5
EPS = 1e-6
IN_WIDTHS = (D_ATT, D_ATT, D_ATT, D_RET_K, D_RET_K, D_RET_V, D_RET_V, D_MODEL, D_MODEL)
IN_OFFS = tuple(int(v) for v in np.cumsum((0,) + IN_WIDTHS))
D_IN = IN_OFFS[-1]

NEG_BIG = -1e30
V7X_VMEM_BYTES = 64 * 1024 * 1024
V7X_SC_CORES = 2
V7X_SC_SUBCORES = 16
SC_GATHER_ROWS = 96
SPARE_ROWS = 8192

ROW_TILE = 512
ATT_QB = 256
RET_CHUNK = 256
ROUTE_TILE = 512
MOVE_TILE = 512
FFN_QUANTUM = 128
FFN_FINE = 64
FFN_FINE_FROM = 384
FFN_TILE = 5 * FFN_QUANTUM
FFN_TILE_SIZES = tuple(n for n in range(1, FFN_TILE // FFN_FINE + 1)
                       if n * FFN_FINE > FFN_FINE_FROM or (n * FFN_FINE) % FFN_QUANTUM == 0)
FFN_WEIGHT_SLOTS = 3
VMEM_RESERVE_BYTES = 6 << 20


def _cparams(semantics, vmem_mb):
    return pltpu.CompilerParams(dimension_semantics=semantics,
                                vmem_limit_bytes=min(vmem_mb << 20, V7X_VMEM_BYTES - VMEM_RESERVE_BYTES))


def _silu(x):
    return x * jax.nn.sigmoid(x)


def _pack_bf16_pair(lo, hi):
    return pltpu.bitcast(pltpu.pack_elementwise([lo, hi], packed_dtype=BF16), U32)


def _unpack_bf16_pair(u):
    words = pltpu.bitcast(u, I32)
    lo = pltpu.unpack_elementwise(words, index=0, packed_dtype=BF16, unpacked_dtype=F32)
    hi = pltpu.unpack_elementwise(words, index=1, packed_dtype=BF16, unpacked_dtype=F32)
    return lo, hi


def _ada_kernel(c_ref, w_ref, b_ref, o_ref):
    sc = _silu(c_ref[...]).astype(BF16)
    o_ref[...] = jnp.dot(sc, w_ref[...], preferred_element_type=F32) + b_ref[...]


def _ada(c_all, w_ada_bf, b_ada):
    rows = c_all.shape[0]
    n_out = w_ada_bf.shape[1]
    blk = D_MODEL
    return pl.pallas_call(
        _ada_kernel,
        grid=(n_out // blk,),
        in_specs=[pl.BlockSpec((rows, D_MODEL), lambda j: (0, 0)),
                  pl.BlockSpec((D_MODEL, blk), lambda j: (0, j)),
                  pl.BlockSpec((1, blk), lambda j: (0, j))],
        out_specs=pl.BlockSpec((rows, blk), lambda j: (0, j)),
        out_shape=jax.ShapeDtypeStruct((rows, n_out), F32),
        compiler_params=_cparams(("arbitrary",), 24),
        name="ada",
    )(c_all, w_ada_bf, b_ada)


def _inproj_kernel(x_ref, sc_ref, sh_ref, g_ref, cos_ref, sin_ref, w_ref,
                   qa_ref, ka_ref, va_ref, qr_ref, kr_ref, vr_ref, gr_ref, ga_ref, gb_ref,
                   kv_ref, *, tiles_per_batch):
    x = x_ref[...]
    xn = x * lax.rsqrt(jnp.mean(x * x, axis=-1, keepdims=True) + EPS) * g_ref[...]
    nb = (xn * (1.0 + sc_ref[0]) + sh_ref[0]).astype(BF16)

    def proj(seg):
        return jnp.dot(nb, w_ref[:, IN_OFFS[seg]:IN_OFFS[seg + 1]], preferred_element_type=F32)

    qa_ref[...] = proj(0).astype(BF16)
    ka = proj(1)
    va = proj(2)
    ka_ref[...] = ka.astype(BF16)
    va_ref[...] = va.astype(BF16)

    @pl.when(pl.program_id(0) % tiles_per_batch == tiles_per_batch - 1)
    def _():
        kv_ref[:, :D_ATT] = ka
        kv_ref[:, D_ATT:] = va

    cos = jnp.tile(cos_ref[...], (1, N_HEADS))
    sin = jnp.tile(sin_ref[...], (1, N_HEADS))
    first_half = (lax.broadcasted_iota(I32, (1, D_RET_K), 1) % KEY_DIM_R) < (KEY_DIM_R // 2)

    def rotary(t):
        partner = jnp.where(first_half, pltpu.roll(t, D_RET_K - KEY_DIM_R // 2, 1),
                            pltpu.roll(t, KEY_DIM_R // 2, 1))
        return t * cos + partner * sin

    qr_ref[...] = rotary(proj(3)).astype(BF16)
    kr_ref[...] = (rotary(proj(4)) * (KEY_DIM_R ** -0.5)).astype(BF16)
    vr_ref[...] = proj(5).astype(BF16)
    gr_ref[...] = proj(6).astype(BF16)
    ga_ref[...] = proj(7).astype(BF16)
    gb_ref[...] = proj(8).astype(BF16)


def _inproj(x2d, sc, sh, g, cos_t, sin_t, w_in_bf, tiles_per_batch):
    n = x2d.shape[0]
    tm = ROW_TILE
    n_tiles = n // tm
    n_batches = n_tiles // tiles_per_batch
    mod_rows = sc.shape[1]
    pos_tiles = cos_t.shape[0] // tm

    def row_spec(width):
        return pl.BlockSpec((tm, width), lambda i: (i, 0))

    mod_spec = pl.BlockSpec((1, mod_rows, D_MODEL), lambda i: (i // tiles_per_batch, 0, 0))
    pos_spec = pl.BlockSpec((tm, KEY_DIM_R), lambda i: (i % pos_tiles, 0))
    out_widths = (D_ATT, D_ATT, D_ATT, D_RET_K, D_RET_K, D_RET_V, D_RET_V, D_MODEL, D_MODEL)
    out_shape = [jax.ShapeDtypeStruct((n, w), BF16) for w in out_widths]
    out_shape.append(jax.ShapeDtypeStruct((n_batches * tm, 2 * D_ATT), F32))
    out_specs = [row_spec(w) for w in out_widths]
    out_specs.append(pl.BlockSpec((tm, 2 * D_ATT), lambda i: (i // tiles_per_batch, 0)))
    return pl.pallas_call(
        functools.partial(_inproj_kernel, tiles_per_batch=tiles_per_batch),
        grid=(n_tiles,),
        in_specs=[row_spec(D_MODEL), mod_spec, mod_spec,
                  pl.BlockSpec((1, D_MODEL), lambda i: (0, 0)),
                  pos_spec, pos_spec,
                  pl.BlockSpec((D_MODEL, D_IN), lambda i: (0, 0))],
        out_specs=out_specs,
        out_shape=out_shape,
        compiler_params=_cparams(("arbitrary",), 56),
        name="inproj",
    )(x2d, sc, sh, g, cos_t, sin_t, w_in_bf)


def _softmax_pv(s, v_parts):
    m = functools.reduce(jnp.maximum, [jnp.max(t, axis=-1, keepdims=True) for t in s])
    ps = [jnp.exp(t - m) for t in s]
    l = functools.reduce(jnp.add, [jnp.sum(p, axis=-1, keepdims=True) for p in ps])
    o = functools.reduce(jnp.add, [jnp.dot(p.astype(BF16), v, preferred_element_type=F32)
                                   for p, v in zip(ps, v_parts)])
    return o / l


def _attn_prompt_kernel(q_ref, k0_ref, k1_ref, k2_ref, v0_ref, v1_ref, v2_ref, bias_ref, o_ref):
    j = pl.program_id(1)
    q = q_ref[...]
    k = jnp.concatenate([k0_ref[...], k1_ref[...], k2_ref[...]], axis=0)
    v = jnp.concatenate([v0_ref[...], v1_ref[...], v2_ref[...]], axis=0)
    n_keys = k.shape[0]
    key_block = lax.broadcasted_iota(I32, (1, n_keys), 1) // ATT_QB
    before_start = jnp.where(key_block < 2 - j, NEG_BIG, 0.0)
    outs = []
    for h in range(N_HEADS):
        sl = slice(h * HEAD_DIM_A, (h + 1) * HEAD_DIM_A)
        qh = (q[:, sl].astype(F32) * (HEAD_DIM_A ** -0.5)).astype(BF16)
        s = lax.dot_general(qh, k[:, sl], (((1,), (1,)), ((), ())), preferred_element_type=F32)
        s = s + bias_ref[h] + before_start
        outs.append(_softmax_pv([s], [v[:, sl]]))
    o_ref[...] = jnp.concatenate(outs, axis=1).astype(BF16)


def _attn_prompt(q, k, v, bias_full, batch, seq):
    qb = ATT_QB
    nq = seq // qb

    def q_map(b, j):
        return (b * nq + j, 0)

    def kv_map(back):
        return lambda b, j: (b * nq + jnp.maximum(j - back, 0), 0)

    blk = lambda m: pl.BlockSpec((qb, D_ATT), m)
    return pl.pallas_call(
        _attn_prompt_kernel,
        grid=(batch, nq),
        in_specs=[blk(q_map), blk(kv_map(2)), blk(kv_map(1)), blk(kv_map(0)),
                  blk(kv_map(2)), blk(kv_map(1)), blk(kv_map(0)),
                  pl.BlockSpec(bias_full.shape, lambda b, j: (0, 0, 0))],
        out_specs=blk(q_map),
        out_shape=jax.ShapeDtypeStruct((batch * seq, D_ATT), BF16),
        compiler_params=_cparams(("parallel", "arbitrary"), 40),
        name="attn_prompt",
    )(q, k, k, k, v, v, v, bias_full)


SAMPLE_ATT_BATCHES = 2


def _attn_sample_kernel(q_ref, kn_ref, vn_ref, ck_ref, cv_ref, bc_ref, bn_ref, o_ref, *, t_new):
    nt = (((1,), (1,)), ((), ()))
    for b in range(SAMPLE_ATT_BATCHES):
        rows = slice(b * t_new, (b + 1) * t_new)
        q = q_ref[rows, :]
        kn = kn_ref[rows, :]
        vn = vn_ref[rows, :]
        outs = []
        for h in range(N_HEADS):
            sl = slice(h * HEAD_DIM_A, (h + 1) * HEAD_DIM_A)
            qh = (q[:, sl].astype(F32) * (HEAD_DIM_A ** -0.5)).astype(BF16)
            kc_t = ck_ref[b, h].astype(BF16)
            vc_t = cv_ref[b, h].astype(BF16)
            s_c = jnp.dot(qh, kc_t, preferred_element_type=F32) + bc_ref[h]
            s_n = lax.dot_general(qh, kn[:, sl], nt, preferred_element_type=F32) + bn_ref[h]
            m = jnp.maximum(jnp.max(s_c, axis=-1, keepdims=True), jnp.max(s_n, axis=-1, keepdims=True))
            p_c = jnp.exp(s_c - m)
            p_n = jnp.exp(s_n - m)
            l = jnp.sum(p_c, axis=-1, keepdims=True) + jnp.sum(p_n, axis=-1, keepdims=True)
            o = (lax.dot_general(p_c.astype(BF16), vc_t, nt, preferred_element_type=F32)
                 + jnp.dot(p_n.astype(BF16), vn[:, sl], preferred_element_type=F32))
            outs.append(o / l)
        o_ref[rows, :] = jnp.concatenate(outs, axis=1).astype(BF16)


def _attn_sample(q, k, v, cache_k_t, cache_v_t, bias_cache, bias_new, batch, t_new, cache_len):
    nb = SAMPLE_ATT_BATCHES
    blk = pl.BlockSpec((nb * t_new, D_ATT), lambda b: (b, 0))
    cblk = pl.BlockSpec((None, nb, N_HEADS, HEAD_DIM_A, cache_len), lambda b: (0, b, 0, 0, 0))
    return pl.pallas_call(
        functools.partial(_attn_sample_kernel, t_new=t_new),
        grid=(batch // nb,),
        in_specs=[blk, blk, blk, cblk, cblk,
                  pl.BlockSpec(bias_cache.shape, lambda b: (0, 0, 0)),
                  pl.BlockSpec(bias_new.shape, lambda b: (0, 0, 0))],
        out_specs=blk,
        out_shape=jax.ShapeDtypeStruct((batch * t_new, D_ATT), BF16),
        compiler_params=_cparams(("arbitrary",), 40),
        name="attn_sample",
    )(q, k, v, cache_k_t, cache_v_t, bias_cache, bias_new)


def _ret_kernel(q_ref, k_ref, v_ref, g_ref, s0_ref, dmask_ref, qdec_ref, kdec_ref, sdec_ref,
                y_ref, sout_ref, state_ref):
    c = pl.program_id(1)

    @pl.when(c == 0)
    def _():
        state_ref[...] = s0_ref[0]

    q = q_ref[...]
    k = k_ref[...]
    v = v_ref[...]
    g = g_ref[...]
    outs = []
    for h in range(N_HEADS):
        ks = slice(h * KEY_DIM_R, (h + 1) * KEY_DIM_R)
        vs = slice(h * VAL_DIM_R, (h + 1) * VAL_DIM_R)
        qh, kh, vh = q[:, ks], k[:, ks], v[:, vs]
        scores = lax.dot_general(qh, kh, (((1,), (1,)), ((), ())), preferred_element_type=F32)
        inner = jnp.dot((scores * dmask_ref[h]).astype(BF16), vh, preferred_element_type=F32)
        state = state_ref[h]
        cross = jnp.dot(qh, state.astype(BF16), preferred_element_type=F32) * qdec_ref[h]
        o = inner + cross
        v_dec = (vh.astype(F32) * kdec_ref[h]).astype(BF16)
        state_ref[h] = sdec_ref[h] * state + lax.dot_general(
            kh, v_dec, (((0,), (0,)), ((), ())), preferred_element_type=F32)
        on = o * lax.rsqrt(jnp.mean(o * o, axis=-1, keepdims=True) + EPS)
        outs.append(on * _silu(g[:, vs].astype(F32)))
    y_ref[...] = jnp.concatenate(outs, axis=1).astype(BF16)

    @pl.when(c == pl.num_programs(1) - 1)
    def _():
        sout_ref[0] = state_ref[...]


def _ret_tables(chunk):
    log_g = jnp.log(1.0 - jnp.exp2(-5.0 - jnp.arange(N_HEADS, dtype=F32)))
    i = jnp.arange(chunk, dtype=F32)
    diff = i[:, None] - i[None, :]
    dmask = jnp.where(diff >= 0, jnp.exp(log_g[:, None, None] * jnp.maximum(diff, 0.0)), 0.0)
    qdec = jnp.exp(log_g[:, None] * (i + 1.0))
    kdec = jnp.exp(log_g[:, None] * (chunk - 1.0 - i))
    sdec = jnp.exp(log_g * chunk)
    bc = lambda t: jnp.broadcast_to(t[:, :, None], (N_HEADS, t.shape[1], VAL_DIM_R)).astype(F32)
    sdec_t = jnp.broadcast_to(sdec[:, None, None], (N_HEADS, 1, VAL_DIM_R)).astype(F32)
    return dmask.astype(F32), bc(qdec), bc(kdec), sdec_t


def _retention(q, k, v, gate, state0, batch, seq, chunk):
    nc = seq // chunk
    dmask, qdec, kdec, sdec = _ret_tables(chunk)
    row = lambda w: pl.BlockSpec((chunk, w), lambda b, c: (b * nc + c, 0))
    const = lambda a: pl.BlockSpec(a.shape, lambda b, c: (0,) * a.ndim)
    st_spec = pl.BlockSpec((1, N_HEADS, KEY_DIM_R, VAL_DIM_R), lambda b, c: (b, 0, 0, 0))
    return pl.pallas_call(
        _ret_kernel,
        grid=(batch, nc),
        in_specs=[row(D_RET_K), row(D_RET_K), row(D_RET_V), row(D_RET_V), st_spec,
                  const(dmask), const(qdec), const(kdec), const(sdec)],
        out_specs=[row(D_RET_V), st_spec],
        out_shape=[jax.ShapeDtypeStruct((batch * seq, D_RET_V), BF16),
                   jax.ShapeDtypeStruct((batch, N_HEADS, KEY_DIM_R, VAL_DIM_R), F32)],
        scratch_shapes=[pltpu.VMEM((N_HEADS, KEY_DIM_R, VAL_DIM_R), F32)],
        compiler_params=_cparams(("parallel", "arbitrary"), 32),
        name="retention",
    )(q, k, v, gate, state0, dmask, qdec, kdec, sdec)


def _outproj_kernel(x_ref, oa_ref, yr_ref, ga_ref, gb_ref, g1_ref, sc2_ref, sh2_ref, g2_ref, n2g_ref,
                    woa_ref, wor_ref, wout_ref, wrt_ref, wsg_ref, wsu_ref, wsd_ref, *rest):
    h_ref, n2p_ref, s_ref = rest[-3:]
    ya = jnp.dot(oa_ref[...], woa_ref[...], preferred_element_type=F32)
    yr = jnp.dot(yr_ref[...], wor_ref[...], preferred_element_type=F32)
    merged = (jax.nn.sigmoid(ga_ref[...].astype(F32)) * ya
              + jax.nn.sigmoid(gb_ref[...].astype(F32)) * yr)
    mix = jnp.dot(merged.astype(BF16), wout_ref[...], preferred_element_type=F32)
    h = x_ref[...] + g1_ref[0] * mix
    hn = h * lax.rsqrt(jnp.mean(h * h, axis=-1, keepdims=True) + EPS) * n2g_ref[...]
    n2 = hn * (1.0 + sc2_ref[0]) + sh2_ref[0]
    n2b = n2.astype(BF16)
    s_ref[...] = jax.nn.sigmoid(lax.dot_general(wrt_ref[...], n2b, (((1,), (1,)), ((), ())),
                                                preferred_element_type=F32))
    hid = _silu(jnp.dot(n2b, wsg_ref[...], preferred_element_type=F32)) * jnp.dot(
        n2b, wsu_ref[...], preferred_element_type=F32)
    shared = jnp.dot(hid.astype(BF16), wsd_ref[...], preferred_element_type=F32)
    h_ref[...] = h + g2_ref[0] * shared
    half = D_MODEL // 2
    n2p_ref[...] = _pack_bf16_pair(n2[:, :half], n2[:, half:])


def _outproj(x2d, oa, yr_in, ga, gb, g1, sc2, sh2, g2, n2g, weights, tiles_per_batch,
             all_tokens, token_offset, carried=None):
    n = x2d.shape[0]
    tm = ROW_TILE
    off = token_offset // tm
    mod_rows = g1.shape[1]
    row = lambda w: pl.BlockSpec((tm, w), lambda i: (i, 0))
    mod_spec = pl.BlockSpec((1, mod_rows, D_MODEL), lambda i: (i // tiles_per_batch, 0, 0))
    const = lambda a: pl.BlockSpec(a.shape, lambda i: (0,) * a.ndim)
    in_specs = [row(D_MODEL), row(D_ATT), row(D_RET_V), row(D_MODEL), row(D_MODEL),
                mod_spec, mod_spec, mod_spec, mod_spec, const(n2g)] + [const(w) for w in weights]
    args = [x2d, oa, yr_in, ga, gb, g1, sc2, sh2, g2, n2g, *weights]
    aliases = {}
    if carried is not None:
        aliases = {len(args): 1, len(args) + 1: 2}
        in_specs += [pl.BlockSpec(memory_space=pl.ANY)] * 2
        args += list(carried)
    return pl.pallas_call(
        _outproj_kernel,
        grid=(n // tm,),
        in_specs=in_specs,
        out_specs=[row(D_MODEL),
                   pl.BlockSpec((tm, D_MODEL // 2), lambda i: (i + off, 0)),
                   pl.BlockSpec((N_EXPERTS, tm), lambda i: (0, i + off))],
        out_shape=[jax.ShapeDtypeStruct((n, D_MODEL), F32),
                   jax.ShapeDtypeStruct((all_tokens, D_MODEL // 2), U32),
                   jax.ShapeDtypeStruct((N_EXPERTS, all_tokens), F32)],
        input_output_aliases=aliases,
        compiler_params=_cparams(("arbitrary",), 48),
        name="outproj",
    )(*args)


def _route_kernel(s_ref, b_ref, idx_ref, w_ref, rank_ref, cnt_ref, run_ref, tri_ref):
    step = pl.program_id(0)
    t = s_ref.shape[1]

    @pl.when(step == 0)
    def _():
        run_ref[...] = jnp.zeros_like(run_ref)
        r = lax.broadcasted_iota(I32, (t, t), 0)
        c = lax.broadcasted_iota(I32, (t, t), 1)
        tri_ref[...] = jnp.where(r < c, 1.0, 0.0).astype(BF16)

    s = s_ref[...]
    sel = s + b_ref[...]
    row_f = lax.broadcasted_iota(I32, (N_EXPERTS, t), 0).astype(F32)

    def first_argmax(vals, rows):
        m = jnp.max(vals, axis=0, keepdims=True)
        pos = jnp.min(jnp.where(vals == m, rows, float(N_EXPERTS)), axis=0, keepdims=True)
        return m, pos

    gscore = []
    group_row = lax.broadcasted_iota(I32, (GROUP_SIZE, t), 0).astype(F32)
    for g in range(N_GROUPS):
        rows = slice(g * GROUP_SIZE, (g + 1) * GROUP_SIZE)
        m1, p1 = first_argmax(sel[rows], group_row)
        m2 = jnp.max(jnp.where(group_row == p1, -jnp.inf, sel[rows]), axis=0, keepdims=True)
        gscore.append(m1 + m2)
    cand_parts = []
    for g in range(N_GROUPS):
        rows = slice(g * GROUP_SIZE, (g + 1) * GROUP_SIZE)
        beaten_by = jnp.zeros((1, t), F32)
        for o in range(N_GROUPS):
            if o == g:
                continue
            wins = (gscore[o] > gscore[g]) if o > g else (gscore[o] >= gscore[g])
            beaten_by = beaten_by + jnp.where(wins, 1.0, 0.0)
        cand_parts.append(jnp.where(beaten_by < TOPK_GROUPS, sel[rows], -jnp.inf))
    cand = jnp.concatenate(cand_parts, axis=0)

    picked = jnp.zeros((N_EXPERTS, t), F32)
    idx_rows, w_rows = [], []
    for _ in range(TOP_K):
        _, pos = first_argmax(cand, row_f)
        hit = row_f == pos
        w_rows.append(jnp.sum(jnp.where(hit, s, 0.0), axis=0, keepdims=True))
        idx_rows.append(pos)
        picked = jnp.where(hit, 1.0, picked)
        cand = jnp.where(hit, -jnp.inf, cand)
    w_sum = functools.reduce(jnp.add, w_rows)

    before = jnp.dot(picked.astype(BF16), tri_ref[...], preferred_element_type=F32) + run_ref[...]
    run_ref[...] = run_ref[...] + jnp.sum(picked, axis=1, keepdims=True)
    rank_rows = [jnp.sum(jnp.where(row_f == idx_rows[kk], before, 0.0), axis=0, keepdims=True)
                 for kk in range(TOP_K)]

    idx_ref[...] = jnp.concatenate(idx_rows, axis=0).astype(I32)
    w_ref[...] = jnp.concatenate([w / w_sum * ROUTED_SCALE for w in w_rows], axis=0).T
    rank_ref[...] = jnp.concatenate(rank_rows, axis=0).astype(I32)

    @pl.when(step == pl.num_programs(0) - 1)
    def _():
        cnt_ref[...] = run_ref[...].astype(I32)


def _route(scores_t, b_col):
    n = scores_t.shape[1]
    t = ROUTE_TILE
    col = pl.BlockSpec((TOP_K, t), lambda i: (0, i))
    const = pl.BlockSpec((N_EXPERTS, t), lambda i: (0, 0))
    return pl.pallas_call(
        _route_kernel,
        grid=(n // t,),
        in_specs=[pl.BlockSpec((N_EXPERTS, t), lambda i: (0, i)), const],
        out_specs=[col, pl.BlockSpec((t, TOP_K), lambda i: (i, 0)), col, const],
        out_shape=[jax.ShapeDtypeStruct((TOP_K, n), I32),
                   jax.ShapeDtypeStruct((n, TOP_K), F32),
                   jax.ShapeDtypeStruct((TOP_K, n), I32),
                   jax.ShapeDtypeStruct((N_EXPERTS, t), I32)],
        scratch_shapes=[pltpu.VMEM((N_EXPERTS, t), F32), pltpu.VMEM((t, t), BF16)],
        compiler_params=_cparams(("arbitrary",), 32),
        name="route",
    )(scores_t, b_col)


def _dest_kernel(idx_ref, rank_ref, start_ref, dest_ref):
    t = idx_ref.shape[1]
    row = lax.broadcasted_iota(I32, (N_EXPERTS, t), 0)
    starts = start_ref[...]
    base = [jnp.sum(jnp.where(row == idx_ref[kk:kk + 1, :], starts, 0.0), axis=0, keepdims=True)
            for kk in range(TOP_K)]
    dest_ref[...] = jnp.concatenate(base, axis=0).astype(I32) + rank_ref[...]


def _dest(idx_t, rank_t, starts_col):
    n = idx_t.shape[1]
    t = ROUTE_TILE
    col = pl.BlockSpec((TOP_K, t), lambda i: (0, i))
    return pl.pallas_call(
        _dest_kernel,
        grid=(n // t,),
        in_specs=[col, col, pl.BlockSpec((N_EXPERTS, t), lambda i: (0, 0))],
        out_specs=col,
        out_shape=jax.ShapeDtypeStruct((TOP_K, n), I32),
        compiler_params=_cparams(("arbitrary",), 32),
        name="dest",
    )(idx_t, rank_t, starts_col)


def _sc_mesh():
    return plsc.VectorSubcoreMesh(core_axis_name="core", subcore_axis_name="subcore",
                                  num_cores=V7X_SC_CORES, num_subcores=V7X_SC_SUBCORES)


def _sc_dispatch(n2p, dest_kmajor, pad_rows, n_out_rows):
    n, width = n2p.shape
    rows = SC_GATHER_ROWS
    n_workers = V7X_SC_CORES * V7X_SC_SUBCORES
    src_chunks = n // rows
    items = dest_kmajor.shape[0] // rows
    per_worker = items // n_workers
    pad_per_worker = pad_rows.shape[0] // rows // n_workers
    assert src_chunks * rows == n and per_worker * n_workers == items and per_worker % 2 == 0
    assert pad_per_worker * n_workers * rows == pad_rows.shape[0]
    idx3 = dest_kmajor.reshape(n_workers, per_worker, rows)
    pad3 = pad_rows.reshape(n_workers, pad_per_worker, rows)
    zeros = jnp.zeros((rows, width), n2p.dtype)

    def body(src_hbm, idx_hbm, pad_hbm, zero_hbm, out_hbm, idx_v, pad_v, rows_v, load_sem, scat_sem):
        worker = lax.axis_index("subcore") * V7X_SC_CORES + lax.axis_index("core")
        pltpu.sync_copy(idx_hbm.at[worker], idx_v)
        pltpu.sync_copy(pad_hbm.at[worker], pad_v)
        pltpu.sync_copy(zero_hbm, rows_v.at[0])

        def zero_fill(c):
            return pltpu.make_async_copy(rows_v.at[0], out_hbm.at[pad_v.at[c]], scat_sem.at[0])

        @pl.loop(0, pad_per_worker)
        def _(c):
            zero_fill(c).start()

        @pl.loop(0, pad_per_worker)
        def _(c):
            zero_fill(c).wait()

        def load(c, b):
            chunk = lax.rem(worker * per_worker + c, src_chunks)
            off = pl.multiple_of(chunk * rows, rows)
            return pltpu.make_async_copy(src_hbm.at[pl.ds(off, rows)], rows_v.at[b], load_sem.at[b])

        def scatter(c, b):
            return pltpu.make_async_copy(rows_v.at[b], out_hbm.at[idx_v.at[c]], scat_sem.at[b])

        load(0, 0).start()

        @pl.loop(0, per_worker, step=2)
        def _(c0):
            for b in range(2):
                c = c0 + b
                load(c, b).wait()

                @pl.when(c >= 1)
                def _():
                    scatter(c - 1, 1 - b).wait()

                @pl.when(c + 1 < per_worker)
                def _():
                    load(c + 1, 1 - b).start()

                scatter(c, b).start()

        scatter(per_worker - 1, (per_worker - 1) % 2).wait()

    return pl.kernel(
        body, mesh=_sc_mesh(),
        out_type=jax.ShapeDtypeStruct((n_out_rows, width), n2p.dtype),
        scratch_types=[pltpu.VMEM((per_worker, rows), I32),
                       pltpu.VMEM((pad_per_worker, rows), I32),
                       pltpu.VMEM((2, rows, width), n2p.dtype),
                       pltpu.SemaphoreType.DMA((2,)),
                       pltpu.SemaphoreType.DMA((2,))],
        name="sc_dispatch",
    )(n2p, idx3, pad3, zeros)


def _ffn_kernel(texp_ref, ntiles_ref, eslot_ref, enext_ref, nhalf_ref, xs_ref, wg_hbm, wu_hbm, wd_hbm,
                ys_ref, wg_buf, wu_buf, wd_buf, wgu_bf, wd_bf, sems):
    g = pl.program_id(0)

    def weight_copies(e, slot):
        return (pltpu.make_async_copy(wg_hbm.at[e], wg_buf.at[slot], sems.at[slot, 0]),
                pltpu.make_async_copy(wu_hbm.at[e], wu_buf.at[slot], sems.at[slot, 1]),
                pltpu.make_async_copy(wd_hbm.at[e], wd_buf.at[slot], sems.at[slot, 2]))

    @pl.when(g < ntiles_ref[0])
    def _():
        e = texp_ref[g]
        changed = jnp.logical_or(g == 0, texp_ref[jnp.maximum(g - 1, 0)] != e)

        @pl.when(changed)
        def _():
            slot = eslot_ref[e]

            def fetch_ahead(first_hop, hops, target_slot):
                ahead = first_hop
                for _ in range(hops - 1):
                    ahead = jnp.where(ahead >= 0, enext_ref[jnp.maximum(ahead, 0)], -1)

                @pl.when(ahead >= 0)
                def _():
                    for c in weight_copies(ahead, target_slot):
                        c.start()

            @pl.when(g == 0)
            def _():
                for c in weight_copies(e, slot):
                    c.start()
                for hops in range(1, FFN_WEIGHT_SLOTS - 1):
                    fetch_ahead(enext_ref[e], hops, lax.rem(slot + hops, FFN_WEIGHT_SLOTS))

            fetch_ahead(enext_ref[e], FFN_WEIGHT_SLOTS - 1,
                        lax.rem(slot + FFN_WEIGHT_SLOTS - 1, FFN_WEIGHT_SLOTS))
            for c in weight_copies(e, slot):
                c.wait()

            wgu_bf[:, :D_EXPERT] = wg_buf[slot].astype(BF16)
            wgu_bf[:, D_EXPERT:] = wu_buf[slot].astype(BF16)
            wd_bf[...] = wd_buf[slot].astype(BF16)

        def expert_rows(rows):
            lo, hi = _unpack_bf16_pair(xs_ref[rows, :])
            x = jnp.concatenate([lo, hi], axis=1).astype(BF16)
            gu = jnp.dot(x, wgu_bf[...], preferred_element_type=F32)
            hid = (_silu(gu[:, :D_EXPERT]) * gu[:, D_EXPERT:]).astype(BF16)
            y = jnp.dot(hid, wd_bf[...], preferred_element_type=F32)
            half = D_MODEL // 2
            ys_ref[rows, :] = _pack_bf16_pair(y[:, :half], y[:, half:])

        for n_groups in FFN_TILE_SIZES:

            @pl.when(nhalf_ref[g] == n_groups)
            def _(n_groups=n_groups):
                used = n_groups * FFN_FINE
                expert_rows(slice(0, used))
                if used < FFN_TILE:
                    ys_ref[used:, :] = jnp.zeros((FFN_TILE - used, ys_ref.shape[1]), U32)


def _ffn(tile_expert, n_tiles, expert_slot, expert_next, tile_halves, xs, w_gate, w_up, w_down):
    rows, width = xs.shape
    m = FFN_TILE
    max_tiles = tile_expert.shape[0]
    row_map = lambda g, te, nt, es, en, nh: (jnp.minimum(g, nt[0] - 1), 0)
    hbm = pl.BlockSpec(memory_space=pl.ANY)
    grid_spec = pltpu.PrefetchScalarGridSpec(
        num_scalar_prefetch=5,
        grid=(max_tiles,),
        in_specs=[pl.BlockSpec((m, width), row_map), hbm, hbm, hbm],
        out_specs=pl.BlockSpec((m, width), row_map),
        scratch_shapes=[pltpu.VMEM((FFN_WEIGHT_SLOTS, D_MODEL, D_EXPERT), F32),
                        pltpu.VMEM((FFN_WEIGHT_SLOTS, D_MODEL, D_EXPERT), F32),
                        pltpu.VMEM((FFN_WEIGHT_SLOTS, D_EXPERT, D_MODEL), F32),
                        pltpu.VMEM((D_MODEL, 2 * D_EXPERT), BF16),
                        pltpu.VMEM((D_EXPERT, D_MODEL), BF16),
                        pltpu.SemaphoreType.DMA((FFN_WEIGHT_SLOTS, 3))],
    )
    return pl.pallas_call(
        _ffn_kernel,
        grid_spec=grid_spec,
        out_shape=jax.ShapeDtypeStruct((rows, width), U32),
        compiler_params=_cparams(("arbitrary",), 32),
        name="ffn",
    )(tile_expert, n_tiles, expert_slot, expert_next, tile_halves, xs, w_gate, w_up, w_down)


def _ffn_plan(counts, n_assign):
    m = FFN_TILE
    max_tiles = n_assign // m + N_EXPERTS
    padded = ((counts + m - 1) // m) * m
    pend = jnp.cumsum(padded).astype(I32)
    pstart = pend - padded
    n_tiles = pend[-1:] // m
    g = jnp.minimum(jnp.arange(max_tiles, dtype=I32), n_tiles - 1)
    tile_expert = jnp.sum((pend[None, :] <= (g * m)[:, None]).astype(I32), axis=1)
    tile_expert = jnp.minimum(tile_expert, N_EXPERTS - 1)
    full_rows = jnp.maximum(padded - m, 0)
    last_rows = counts - full_rows
    round_to = lambda v, q: ((v + q - 1) // q) * q
    last_rows = jnp.where(last_rows > FFN_FINE_FROM, round_to(last_rows, FFN_FINE),
                          round_to(last_rows, FFN_QUANTUM))
    vend = pstart + full_rows + last_rows
    own = tile_expert[:, None] == jnp.arange(N_EXPERTS, dtype=I32)[None, :]
    tile_vend = jnp.sum(jnp.where(own, vend[None, :], 0), axis=1)
    tile_halves = (jnp.clip(tile_vend - g * m, 0, m) // FFN_FINE).astype(I32)
    used = counts > 0
    expert_slot = ((jnp.cumsum(used.astype(I32)) - 1) % FFN_WEIGHT_SLOTS).astype(I32)
    ids = jnp.where(used, jnp.arange(N_EXPERTS, dtype=I32), N_EXPERTS)
    first_used_from = lax.cummin(ids, axis=0, reverse=True)
    nxt = jnp.concatenate([first_used_from[1:], jnp.full((1,), N_EXPERTS, I32)])
    expert_next = jnp.where(nxt < N_EXPERTS, nxt, -1).astype(I32)
    spare_row = max_tiles * m
    j = jnp.arange(FFN_QUANTUM, dtype=I32)[None, :]
    seg_end = (pstart + counts)[:, None]
    spare = spare_row + (jnp.arange(N_EXPERTS, dtype=I32)[:, None] * FFN_QUANTUM + j) % SPARE_ROWS
    pad_rows = jnp.where(j < (vend[:, None] - seg_end), seg_end + j, spare).astype(I32).reshape(-1)
    unit = SC_GATHER_ROWS * V7X_SC_CORES * V7X_SC_SUBCORES
    extra = (-pad_rows.shape[0]) % unit
    filler = spare_row + (pad_rows.shape[0] + jnp.arange(extra, dtype=I32)) % SPARE_ROWS
    pad_rows = jnp.concatenate([pad_rows, filler])
    return (tile_expert, n_tiles, expert_slot, expert_next, tile_halves, pstart, pad_rows,
            spare_row + SPARE_ROWS)


def _sc_gather_rows(table, idx):
    n_idx = idx.shape[0]
    width = table.shape[1]
    n_workers = V7X_SC_CORES * V7X_SC_SUBCORES
    per_worker = n_idx // n_workers
    n_chunks = per_worker // SC_GATHER_ROWS
    assert per_worker * n_workers == n_idx and n_chunks * SC_GATHER_ROWS == per_worker and n_chunks % 2 == 0

    def body(table_hbm, idx_hbm, out_hbm, idx_v, rows_v, gather_sem, write_sem):
        worker = lax.axis_index("subcore") * V7X_SC_CORES + lax.axis_index("core")
        base = worker * per_worker
        pltpu.sync_copy(idx_hbm.at[pl.ds(base, per_worker)], idx_v)

        def gather(c, b):
            off = pl.multiple_of(c * SC_GATHER_ROWS, SC_GATHER_ROWS)
            return pltpu.make_async_copy(table_hbm.at[idx_v.at[pl.ds(off, SC_GATHER_ROWS)]],
                                         rows_v.at[b], gather_sem.at[b])

        def write(c, b):
            off = pl.multiple_of(c * SC_GATHER_ROWS, SC_GATHER_ROWS)
            return pltpu.make_async_copy(rows_v.at[b], out_hbm.at[pl.ds(base + off, SC_GATHER_ROWS)],
                                         write_sem.at[b])

        gather(0, 0).start()

        @pl.loop(0, n_chunks, step=2)
        def _(c0):
            for b in range(2):
                c = c0 + b
                gather(c, b).wait()

                @pl.when(c >= 1)
                def _():
                    write(c - 1, 1 - b).wait()

                @pl.when(c + 1 < n_chunks)
                def _():
                    gather(c + 1, 1 - b).start()

                write(c, b).start()

        write(n_chunks - 1, (n_chunks - 1) % 2).wait()

    return pl.kernel(
        body, mesh=_sc_mesh(),
        out_type=jax.ShapeDtypeStruct((n_idx, width), table.dtype),
        scratch_types=[pltpu.VMEM((per_worker,), I32),
                       pltpu.VMEM((2, SC_GATHER_ROWS, width), table.dtype),
                       pltpu.SemaphoreType.DMA((2,)),
                       pltpu.SemaphoreType.DMA((2,))],
        name="sc_gather_rows",
    )(table, idx)


def _combine_kernel(w_ref, h_ref, g2_ref, nf_ref, yk_ref, y_ref):
    t = h_ref.shape[0]
    w = w_ref[...]
    acc_lo = jnp.zeros((t, D_MODEL // 2), F32)
    acc_hi = jnp.zeros((t, D_MODEL // 2), F32)
    for kk in range(TOP_K):
        lo, hi = _unpack_bf16_pair(yk_ref[kk])
        wk = w[:, kk:kk + 1]
        acc_lo = acc_lo + wk * lo
        acc_hi = acc_hi + wk * hi
    out = h_ref[...] + g2_ref[0] * jnp.concatenate([acc_lo, acc_hi], axis=1)
    y_ref[...] = out * lax.rsqrt(jnp.mean(out * out, axis=-1, keepdims=True) + EPS) * nf_ref[...]


def _combine(w, h2, g2, normf, y_by_k, row_offset, tiles_per_batch):
    n = h2.shape[0]
    t = MOVE_TILE
    off = row_offset // t
    mod_rows = g2.shape[1]
    mod_tiles = max(ROW_TILE // t, 1) * tiles_per_batch if mod_rows == 1 else n // t
    return pl.pallas_call(
        _combine_kernel,
        grid=(n // t,),
        in_specs=[pl.BlockSpec((t, TOP_K), lambda i: (i + off, 0)),
                  pl.BlockSpec((t, D_MODEL), lambda i: (i, 0)),
                  pl.BlockSpec((1, mod_rows if mod_rows == 1 else t, D_MODEL),
                               (lambda i: (i // mod_tiles, 0, 0)) if mod_rows == 1
                               else (lambda i: (0, i, 0))),
                  pl.BlockSpec((1, D_MODEL), lambda i: (0, 0)),
                  pl.BlockSpec((TOP_K, t, D_MODEL // 2), lambda i: (0, i + off, 0))],
        out_specs=pl.BlockSpec((t, D_MODEL), lambda i: (i, 0)),
        out_shape=jax.ShapeDtypeStruct((n, D_MODEL), F32),
        compiler_params=_cparams(("arbitrary",), 40),
        name="combine",
    )(w, h2, g2, normf, y_by_k)


def _rotary_tables(pos):
    half = KEY_DIM_R // 2
    inv_freq = ROPE_BASE ** (-jnp.arange(half, dtype=F32) / half)
    ang = pos[:, None] * inv_freq[None, :]
    cos = jnp.cos(ang)
    sin = jnp.sin(ang)
    cos_t = jnp.concatenate([cos, cos], axis=1)
    sin_t = jnp.concatenate([-sin, sin], axis=1)
    return cos_t.astype(F32), sin_t.astype(F32)


def _rel_bias_table(rel_bias, n_rows, n_cols, q_offset):
    heads = rel_bias.shape[0]
    n_diag = n_rows + n_cols - 1
    dist = q_offset + (n_rows - 1) - np.arange(n_diag)
    idx = np.clip(dist, -REL_CLIP, REL_CLIP) + REL_CLIP
    n_hi = int(np.sum(dist > REL_CLIP))
    n_lo = int(np.sum(dist < -REL_CLIP))
    mid = rel_bias[:, int(idx[n_diag - n_lo - 1]):int(idx[n_hi]) + 1][:, ::-1]
    diag = jnp.concatenate([jnp.broadcast_to(rel_bias[:, 2 * REL_CLIP:], (heads, n_hi)), mid,
                            jnp.broadcast_to(rel_bias[:, :1], (heads, n_lo))], axis=1)
    period = n_diag + 1
    v = jnp.roll(jnp.pad(diag, ((0, 0), (0, 1))), -(n_rows - 1), axis=1)
    skew = jnp.tile(v, (1, n_rows))[:, :n_rows * (period - 1)].reshape(heads, n_rows, period - 1)
    return skew[:, :, :n_cols].astype(F32)


def _prompt_bias(rel_bias):
    n_cols = ATT_QB + ATT_WINDOW
    r = np.arange(ATT_QB)[:, None]
    c = np.arange(n_cols)[None, :]
    band = c - (r // CHUNK) * CHUNK
    valid = (band >= 0) & (band < ATT_WINDOW + CHUNK)
    table = _rel_bias_table(rel_bias, ATT_QB, n_cols, ATT_WINDOW)
    return jnp.where(jnp.asarray(valid)[None], table, NEG_BIG)


def _sample_bias(rel_bias, t_new, cache_len):
    b = _rel_bias_table(rel_bias, t_new, cache_len + t_new, cache_len)
    return b[:, :, :cache_len], b[:, :, cache_len:]


def _mod_parts(mod, rows_each):
    parts = jnp.split(mod, 6, axis=-1)
    if rows_each == 1:
        return [p[:, None, :] for p in parts]
    return [jnp.repeat(p, rows_each, axis=0)[None] for p in parts]


def kernel(x_prompt, x_sample, cache_attn_k, cache_attn_v, state_ret, c_prompt, c_sample,
           norm1_g, norm2_g, w_ada, b_ada, w_in, rel_bias, w_o_attn, w_o_ret, w_out,
           w_router, b_router, w_exp_gate, w_exp_up, w_exp_down, w_sh_gate, w_sh_up, w_sh_down,
           normf_g):
    batch, seq, d = x_prompt.shape
    dec_batch, dec_seq, _ = x_sample.shape
    depth = w_in.shape[0]
    assert depth == 1 and d == D_MODEL
    assert seq % ROW_TILE == 0 and dec_batch * dec_seq == ROW_TILE and ROW_TILE == ATT_WINDOW
    cache_len = cache_attn_k.shape[2]
    n_p = batch * seq
    n_s = dec_batch * dec_seq
    tpb = seq // ROW_TILE
    l = 0

    bf = lambda a: a.astype(BF16)
    c_all = jnp.concatenate([c_prompt, c_sample], axis=0)
    pad = (-c_all.shape[0]) % 8
    c_all = jnp.pad(c_all, ((0, pad), (0, 0)))
    mod = _ada(c_all, bf(w_ada[l]), b_ada[l][None, :])
    mod_p = _mod_parts(mod[:batch], 1)
    mod_s = _mod_parts(mod[batch:batch + dec_batch], dec_seq)

    w_in_bf = bf(w_in[l])
    n1g = norm1_g[l][None, :]
    n2g = norm2_g[l][None, :]
    dense_w = [bf(w_o_attn[l]), bf(w_o_ret[l]), bf(w_out[l]), bf(w_router[l]).T,
               bf(w_sh_gate[l]), bf(w_sh_up[l]), bf(w_sh_down[l])]

    xp = x_prompt.reshape(n_p, d)
    xs_ = x_sample.reshape(n_s, d)
    cos_p, sin_p = _rotary_tables(jnp.arange(seq, dtype=F32))
    pos_s = PAST_LEN + jnp.arange(dec_seq, dtype=F32)
    cos_s, sin_s = _rotary_tables(jnp.tile(pos_s, dec_batch))

    (qa, ka, va, qr, kr, vr, gr, ga, gb, kv_p) = _inproj(
        xp, mod_p[1], mod_p[0], n1g, cos_p, sin_p, w_in_bf, tpb)
    oa = _attn_prompt(qa, ka, va, _prompt_bias(rel_bias[l]), batch, seq)
    zero_state = jnp.zeros((batch, N_HEADS, KEY_DIM_R, VAL_DIM_R), F32)
    yr_in, state_p = _retention(qr, kr, vr, gr, zero_state, batch, seq, RET_CHUNK)
    h_p, n2p_p, s_p = _outproj(xp, oa, yr_in, ga, gb, mod_p[2], mod_p[4], mod_p[3], mod_p[5], n2g,
                               dense_w, tpb, n_p + n_s, 0)

    (qa_s, ka_s, va_s, qr_s, kr_s, vr_s, gr_s, ga_s, gb_s, kv_s) = _inproj(
        xs_, mod_s[1], mod_s[0], n1g, cos_s, sin_s, w_in_bf, 1)
    bias_c, bias_n = _sample_bias(rel_bias[l], dec_seq, cache_len)
    to_keys_minor = lambda c: jnp.transpose(c, (0, 1, 3, 4, 2))
    oa_s = _attn_sample(qa_s, ka_s, va_s, to_keys_minor(cache_attn_k), to_keys_minor(cache_attn_v),
                        bias_c, bias_n, dec_batch, dec_seq, cache_len)
    yr_in_s, state_s = _retention(qr_s, kr_s, vr_s, gr_s, state_ret[l], dec_batch, dec_seq, dec_seq)
    h_s, n2p, scores_t = _outproj(xs_, oa_s, yr_in_s, ga_s, gb_s, mod_s[2], mod_s[4], mod_s[3],
                                  mod_s[5], n2g, dense_w, 1, n_p + n_s, n_p, carried=(n2p_p, s_p))

    lanes_of = lambda v: jnp.broadcast_to(v[:, None], (N_EXPERTS, ROUTE_TILE))
    idx_t, w_t, rank_t, counts = _route(scores_t, lanes_of(b_router[l]))
    (tile_expert, n_tiles, expert_slot, expert_next, tile_halves, pstart, pad_rows,
     n_sorted_rows) = _ffn_plan(counts[:, 0], (n_p + n_s) * TOP_K)
    dest_t = _dest(idx_t, rank_t, lanes_of(pstart.astype(F32)))
    dest_kmajor = dest_t.reshape(-1)
    w_route = w_t
    xs_sorted = _sc_dispatch(n2p, dest_kmajor, pad_rows, n_sorted_rows)
    ys_sorted = _ffn(tile_expert, n_tiles, expert_slot, expert_next, tile_halves, xs_sorted,
                     w_exp_gate[l], w_exp_up[l], w_exp_down[l])
    nf = normf_g[None, :]
    y_by_k = _sc_gather_rows(ys_sorted, dest_kmajor).reshape(TOP_K, n_p + n_s, d // 2)
    y_p = _combine(w_route, h_p, mod_p[5], nf, y_by_k, 0, tpb)
    y_s = _combine(w_route, h_s, mod_s[5], nf, y_by_k, n_p, 1)

    keep = min(ATT_WINDOW, seq)
    kv_p = kv_p.reshape(batch, ROW_TILE, 2, N_HEADS, HEAD_DIM_A)[:, ROW_TILE - keep:]
    kv_s = kv_s.reshape(dec_batch, dec_seq, 2, N_HEADS, HEAD_DIM_A)
    return (y_p.reshape(batch, seq, d), y_s.reshape(dec_batch, dec_seq, d),
            kv_p[:, :, 0][None], kv_p[:, :, 1][None], state_p[None],
            kv_s[:, :, 0][None], kv_s[:, :, 1][None], state_s[None])
```

```python
import functools

import numpy as np
import jax
import jax.numpy as jnp
from jax import lax
from jax.experimental import pallas as pl
from jax.experimental.pallas import tpu as pltpu
from jax.experimental.pallas import tpu_sc as plsc

F32 = jnp.float32
BF16 = jnp.bfloat16
I32 = jnp.int32
U32 = jnp.uint32

D_MODEL = 1024
PAST_LEN = 4096
CHUNK = 64
N_LEFT_CHUNKS = 8
ATT_WINDOW = N_LEFT_CHUNKS * CHUNK
N_HEADS = 8
HEAD_DIM_A = 64
D_ATT = N_HEADS * HEAD_DIM_A
REL_CLIP = 128
KEY_DIM_R = 64
VAL_DIM_R = 128
D_RET_K = N_HEADS * KEY_DIM_R
D_RET_V = N_HEADS * VAL_DIM_R
ROPE_BASE = 10000.0
N_EXPERTS = 256
TOP_K = 8
N_GROUPS = 8
GROUP_SIZE = N_EXPERTS // N_GROUPS
TOPK_GROUPS = 4
D_EXPERT = 256
ROUTED_SCALE = 2.5
EPS = 1e-6
IN_WIDTHS = (D_ATT, D_ATT, D_ATT, D_RET_K, D_RET_K, D_RET_V, D_RET_V, D_MODEL, D_MODEL)
IN_OFFS = tuple(int(v) for v in np.cumsum((0,) + IN_WIDTHS))
D_IN = IN_OFFS[-1]

NEG_BIG = -1e30
V7X_VMEM_BYTES = 64 * 1024 * 1024
V7X_SC_CORES = 2
V7X_SC_SUBCORES = 16
SC_GATHER_ROWS = 96
SPARE_ROWS = 8192

ROW_TILE = 512
ATT_QB = 256
RET_CHUNK = 256
ROUTE_TILE = 512
MOVE_TILE = 512
FFN_QUANTUM = 128
FFN_FINE = 64
FFN_FINE_FROM = 384
FFN_TILE = 5 * FFN_QUANTUM
FFN_TILE_SIZES = tuple(n for n in range(1, FFN_TILE // FFN_FINE + 1)
                       if n * FFN_FINE > FFN_FINE_FROM or (n * FFN_FINE) % FFN_QUANTUM == 0)
FFN_WEIGHT_SLOTS = 3
VMEM_RESERVE_BYTES = 6 << 20


def _cparams(semantics, vmem_mb):
    return pltpu.CompilerParams(dimension_semantics=semantics,
                                vmem_limit_bytes=min(vmem_mb << 20, V7X_VMEM_BYTES - VMEM_RESERVE_BYTES))


def _silu(x):
    return x * jax.nn.sigmoid(x)


def _pack_bf16_pair(lo, hi):
    return pltpu.bitcast(pltpu.pack_elementwise([lo, hi], packed_dtype=BF16), U32)


def _unpack_bf16_pair(u):
    words = pltpu.bitcast(u, I32)
    lo = pltpu.unpack_elementwise(words, index=0, packed_dtype=BF16, unpacked_dtype=F32)
    hi = pltpu.unpack_elementwise(words, index=1, packed_dtype=BF16, unpacked_dtype=F32)
    return lo, hi


def _ada_kernel(c_ref, w_ref, b_ref, o_ref):
    sc = _silu(c_ref[...]).astype(BF16)
    o_ref[...] = jnp.dot(sc, w_ref[...], preferred_element_type=F32) + b_ref[...]


def _ada(c_all, w_ada_bf, b_ada):
    rows = c_all.shape[0]
    n_out = w_ada_bf.shape[1]
    blk = D_MODEL
    return pl.pallas_call(
        _ada_kernel,
        grid=(n_out // blk,),
        in_specs=[pl.BlockSpec((rows, D_MODEL), lambda j: (0, 0)),
                  pl.BlockSpec((D_MODEL, blk), lambda j: (0, j)),
                  pl.BlockSpec((1, blk), lambda j: (0, j))],
        out_specs=pl.BlockSpec((rows, blk), lambda j: (0, j)),
        out_shape=jax.ShapeDtypeStruct((rows, n_out), F32),
        compiler_params=_cparams(("arbitrary",), 24),
        name="ada",
    )(c_all, w_ada_bf, b_ada)


def _inproj_kernel(x_ref, sc_ref, sh_ref, g_ref, cos_ref, sin_ref, w_ref,
                   qa_ref, ka_ref, va_ref, qr_ref, kr_ref, vr_ref, gr_ref, ga_ref, gb_ref,
                   kv_ref, *, tiles_per_batch):
    x = x_ref[...]
    xn = x * lax.rsqrt(jnp.mean(x * x, axis=-1, keepdims=True) + EPS) * g_ref[...]
    nb = (xn * (1.0 + sc_ref[0]) + sh_ref[0]).astype(BF16)

    def proj(seg):
        return jnp.dot(nb, w_ref[:, IN_OFFS[seg]:IN_OFFS[seg + 1]], preferred_element_type=F32)

    qa_ref[...] = proj(0).astype(BF16)
    ka = proj(1)
    va = proj(2)
    ka_ref[...] = ka.astype(BF16)
    va_ref[...] = va.astype(BF16)

    @pl.when(pl.program_id(0) % tiles_per_batch == tiles_per_batch - 1)
    def _():
        kv_ref[:, :D_ATT] = ka
        kv_ref[:, D_ATT:] = va

    cos = jnp.tile(cos_ref[...], (1, N_HEADS))
    sin = jnp.tile(sin_ref[...], (1, N_HEADS))
    first_half = (lax.broadcasted_iota(I32, (1, D_RET_K), 1) % KEY_DIM_R) < (KEY_DIM_R // 2)

    def rotary(t):
        partner = jnp.where(first_half, pltpu.roll(t, D_RET_K - KEY_DIM_R // 2, 1),
                            pltpu.roll(t, KEY_DIM_R // 2, 1))
        return t * cos + partner * sin

    qr_ref[...] = rotary(proj(3)).astype(BF16)
    kr_ref[...] = (rotary(proj(4)) * (KEY_DIM_R ** -0.5)).astype(BF16)
    vr_ref[...] = proj(5).astype(BF16)
    gr_ref[...] = proj(6).astype(BF16)
    ga_ref[...] = proj(7).astype(BF16)
    gb_ref[...] = proj(8).astype(BF16)


def _inproj(x2d, sc, sh, g, cos_t, sin_t, w_in_bf, tiles_per_batch):
    n = x2d.shape[0]
    tm = ROW_TILE
    n_tiles = n // tm
    n_batches = n_tiles // tiles_per_batch
    mod_rows = sc.shape[1]
    pos_tiles = cos_t.shape[0] // tm

    def row_spec(width):
        return pl.BlockSpec((tm, width), lambda i: (i, 0))

    mod_spec = pl.BlockSpec((1, mod_rows, D_MODEL), lambda i: (i // tiles_per_batch, 0, 0))
    pos_spec = pl.BlockSpec((tm, KEY_DIM_R), lambda i: (i % pos_tiles, 0))
    out_widths = (D_ATT, D_ATT, D_ATT, D_RET_K, D_RET_K, D_RET_V, D_RET_V, D_MODEL, D_MODEL)
    out_shape = [jax.ShapeDtypeStruct((n, w), BF16) for w in out_widths]
    out_shape.append(jax.ShapeDtypeStruct((n_batches * tm, 2 * D_ATT), F32))
    out_specs = [row_spec(w) for w in out_widths]
    out_specs.append(pl.BlockSpec((tm, 2 * D_ATT), lambda i: (i // tiles_per_batch, 0)))
    return pl.pallas_call(
        functools.partial(_inproj_kernel, tiles_per_batch=tiles_per_batch),
        grid=(n_tiles,),
        in_specs=[row_spec(D_MODEL), mod_spec, mod_spec,
                  pl.BlockSpec((1, D_MODEL), lambda i: (0, 0)),
                  pos_spec, pos_spec,
                  pl.BlockSpec((D_MODEL, D_IN), lambda i: (0, 0))],
        out_specs=out_specs,
        out_shape=out_shape,
        compiler_params=_cparams(("arbitrary",), 56),
        name="inproj",
    )(x2d, sc, sh, g, cos_t, sin_t, w_in_bf)


def _softmax_pv(s, v_parts):
    m = functools.reduce(jnp.maximum, [jnp.max(t, axis=-1, keepdims=True) for t in s])
    ps = [jnp.exp(t - m) for t in s]
    l = functools.reduce(jnp.add, [jnp.sum(p, axis=-1, keepdims=True) for p in ps])
    o = functools.reduce(jnp.add, [jnp.dot(p.astype(BF16), v, preferred_element_type=F32)
                                   for p, v in zip(ps, v_parts)])
    return o / l


def _attn_prompt_kernel(q_ref, k0_ref, k1_ref, k2_ref, v0_ref, v1_ref, v2_ref, bias_ref, o_ref):
    j = pl.program_id(1)
    q = q_ref[...]
    k = jnp.concatenate([k0_ref[...], k1_ref[...], k2_ref[...]], axis=0)
    v = jnp.concatenate([v0_ref[...], v1_ref[...], v2_ref[...]], axis=0)
    n_keys = k.shape[0]
    key_block = lax.broadcasted_iota(I32, (1, n_keys), 1) // ATT_QB
    before_start = jnp.where(key_block < 2 - j, NEG_BIG, 0.0)
    outs = []
    for h in range(N_HEADS):
        sl = slice(h * HEAD_DIM_A, (h + 1) * HEAD_DIM_A)
        qh = (q[:, sl].astype(F32) * (HEAD_DIM_A ** -0.5)).astype(BF16)
        s = lax.dot_general(qh, k[:, sl], (((1,), (1,)), ((), ())), preferred_element_type=F32)
        s = s + bias_ref[h] + before_start
        outs.append(_softmax_pv([s], [v[:, sl]]))
    o_ref[...] = jnp.concatenate(outs, axis=1).astype(BF16)


def _attn_prompt(q, k, v, bias_full, batch, seq):
    qb = ATT_QB
    nq = seq // qb

    def q_map(b, j):
        return (b * nq + j, 0)

    def kv_map(back):
        return lambda b, j: (b * nq + jnp.maximum(j - back, 0), 0)

    blk = lambda m: pl.BlockSpec((qb, D_ATT), m)
    return pl.pallas_call(
        _attn_prompt_kernel,
        grid=(batch, nq),
        in_specs=[blk(q_map), blk(kv_map(2)), blk(kv_map(1)), blk(kv_map(0)),
                  blk(kv_map(2)), blk(kv_map(1)), blk(kv_map(0)),
                  pl.BlockSpec(bias_full.shape, lambda b, j: (0, 0, 0))],
        out_specs=blk(q_map),
        out_shape=jax.ShapeDtypeStruct((batch * seq, D_ATT), BF16),
        compiler_params=_cparams(("parallel", "arbitrary"), 40),
        name="attn_prompt",
    )(q, k, k, k, v, v, v, bias_full)


SAMPLE_ATT_BATCHES = 2


def _attn_sample_kernel(q_ref, kn_ref, vn_ref, ck_ref, cv_ref, bc_ref, bn_ref, o_ref, *, t_new):
    nt = (((1,), (1,)), ((), ()))
    for b in range(SAMPLE_ATT_BATCHES):
        rows = slice(b * t_new, (b + 1) * t_new)
        q = q_ref[rows, :]
        kn = kn_ref[rows, :]
        vn = vn_ref[rows, :]
        outs = []
        for h in range(N_HEADS):
            sl = slice(h * HEAD_DIM_A, (h + 1) * HEAD_DIM_A)
            qh = (q[:, sl].astype(F32) * (HEAD_DIM_A ** -0.5)).astype(BF16)
            kc_t = ck_ref[b, h].astype(BF16)
            vc_t = cv_ref[b, h].astype(BF16)
            s_c = jnp.dot(qh, kc_t, preferred_element_type=F32) + bc_ref[h]
            s_n = lax.dot_general(qh, kn[:, sl], nt, preferred_element_type=F32) + bn_ref[h]
            m = jnp.maximum(jnp.max(s_c, axis=-1, keepdims=True), jnp.max(s_n, axis=-1, keepdims=True))
            p_c = jnp.exp(s_c - m)
            p_n = jnp.exp(s_n - m)
            l = jnp.sum(p_c, axis=-1, keepdims=True) + jnp.sum(p_n, axis=-1, keepdims=True)
            o = (lax.dot_general(p_c.astype(BF16), vc_t, nt, preferred_element_type=F32)
                 + jnp.dot(p_n.astype(BF16), vn[:, sl], preferred_element_type=F32))
            outs.append(o / l)
        o_ref[rows, :] = jnp.concatenate(outs, axis=1).astype(BF16)


def _attn_sample(q, k, v, cache_k_t, cache_v_t, bias_cache, bias_new, batch, t_new, cache_len):
    nb = SAMPLE_ATT_BATCHES
    blk = pl.BlockSpec((nb * t_new, D_ATT), lambda b: (b, 0))
    cblk = pl.BlockSpec((None, nb, N_HEADS, HEAD_DIM_A, cache_len), lambda b: (0, b, 0, 0, 0))
    return pl.pallas_call(
        functools.partial(_attn_sample_kernel, t_new=t_new),
        grid=(batch // nb,),
        in_specs=[blk, blk, blk, cblk, cblk,
                  pl.BlockSpec(bias_cache.shape, lambda b: (0, 0, 0)),
                  pl.BlockSpec(bias_new.shape, lambda b: (0, 0, 0))],
        out_specs=blk,
        out_shape=jax.ShapeDtypeStruct((batch * t_new, D_ATT), BF16),
        compiler_params=_cparams(("arbitrary",), 40),
        name="attn_sample",
    )(q, k, v, cache_k_t, cache_v_t, bias_cache, bias_new)


def _ret_kernel(q_ref, k_ref, v_ref, g_ref, s0_ref, dmask_ref, qdec_ref, kdec_ref, sdec_ref,
                y_ref, sout_ref, state_ref):
    c = pl.program_id(1)

    @pl.when(c == 0)
    def _():
        state_ref[...] = s0_ref[0]

    q = q_ref[...]
    k = k_ref[...]
    v = v_ref[...]
    g = g_ref[...]
    outs = []
    for h in range(N_HEADS):
        ks = slice(h * KEY_DIM_R, (h + 1) * KEY_DIM_R)
        vs = slice(h * VAL_DIM_R, (h + 1) * VAL_DIM_R)
        qh, kh, vh = q[:, ks], k[:, ks], v[:, vs]
        scores = lax.dot_general(qh, kh, (((1,), (1,)), ((), ())), preferred_element_type=F32)
        inner = jnp.dot((scores * dmask_ref[h]).astype(BF16), vh, preferred_element_type=F32)
        state = state_ref[h]
        cross = jnp.dot(qh, state.astype(BF16), preferred_element_type=F32) * qdec_ref[h]
        o = inner + cross
        v_dec = (vh.astype(F32) * kdec_ref[h]).astype(BF16)
        state_ref[h] = sdec_ref[h] * state + lax.dot_general(
            kh, v_dec, (((0,), (0,)), ((), ())), preferred_element_type=F32)
        on = o * lax.rsqrt(jnp.mean(o * o, axis=-1, keepdims=True) + EPS)
        outs.append(on * _silu(g[:, vs].astype(F32)))
    y_ref[...] = jnp.concatenate(outs, axis=1).astype(BF16)

    @pl.when(c == pl.num_programs(1) - 1)
    def _():
        sout_ref[0] = state_ref[...]


def _ret_tables(chunk):
    log_g = jnp.log(1.0 - jnp.exp2(-5.0 - jnp.arange(N_HEADS, dtype=F32)))
    i = jnp.arange(chunk, dtype=F32)
    diff = i[:, None] - i[None, :]
    dmask = jnp.where(diff >= 0, jnp.exp(log_g[:, None, None] * jnp.maximum(diff, 0.0)), 0.0)
    qdec = jnp.exp(log_g[:, None] * (i + 1.0))
    kdec = jnp.exp(log_g[:, None] * (chunk - 1.0 - i))
    sdec = jnp.exp(log_g * chunk)
    bc = lambda t: jnp.broadcast_to(t[:, :, None], (N_HEADS, t.shape[1], VAL_DIM_R)).astype(F32)
    sdec_t = jnp.broadcast_to(sdec[:, None, None], (N_HEADS, 1, VAL_DIM_R)).astype(F32)
    return dmask.astype(F32), bc(qdec), bc(kdec), sdec_t


def _retention(q, k, v, gate, state0, batch, seq, chunk):
    nc = seq // chunk
    dmask, qdec, kdec, sdec = _ret_tables(chunk)
    row = lambda w: pl.BlockSpec((chunk, w), lambda b, c: (b * nc + c, 0))
    const = lambda a: pl.BlockSpec(a.shape, lambda b, c: (0,) * a.ndim)
    st_spec = pl.BlockSpec((1, N_HEADS, KEY_DIM_R, VAL_DIM_R), lambda b, c: (b, 0, 0, 0))
    return pl.pallas_call(
        _ret_kernel,
        grid=(batch, nc),
        in_specs=[row(D_RET_K), row(D_RET_K), row(D_RET_V), row(D_RET_V), st_spec,
                  const(dmask), const(qdec), const(kdec), const(sdec)],
        out_specs=[row(D_RET_V), st_spec],
        out_shape=[jax.ShapeDtypeStruct((batch * seq, D_RET_V), BF16),
                   jax.ShapeDtypeStruct((batch, N_HEADS, KEY_DIM_R, VAL_DIM_R), F32)],
        scratch_shapes=[pltpu.VMEM((N_HEADS, KEY_DIM_R, VAL_DIM_R), F32)],
        compiler_params=_cparams(("parallel", "arbitrary"), 32),
        name="retention",
    )(q, k, v, gate, state0, dmask, qdec, kdec, sdec)


def _outproj_kernel(x_ref, oa_ref, yr_ref, ga_ref, gb_ref, g1_ref, sc2_ref, sh2_ref, g2_ref, n2g_ref,
                    woa_ref, wor_ref, wout_ref, wrt_ref, *rest):
    h_ref, n2p_ref, s_ref = rest[-3:]
    ya = jnp.dot(oa_ref[...], woa_ref[...], preferred_element_type=F32)
    yr = jnp.dot(yr_ref[...], wor_ref[...], preferred_element_type=F32)
    merged = (jax.nn.sigmoid(ga_ref[...].astype(F32)) * ya
              + jax.nn.sigmoid(gb_ref[...].astype(F32)) * yr)
    mix = jnp.dot(merged.astype(BF16), wout_ref[...], preferred_element_type=F32)
    h = x_ref[...] + g1_ref[0] * mix
    hn = h * lax.rsqrt(jnp.mean(h * h, axis=-1, keepdims=True) + EPS) * n2g_ref[...]
    n2 = hn * (1.0 + sc2_ref[0]) + sh2_ref[0]
    n2b = n2.astype(BF16)
    s_ref[...] = jax.nn.sigmoid(lax.dot_general(wrt_ref[...], n2b, (((1,), (1,)), ((), ())),
                                                preferred_element_type=F32))
    h_ref[...] = h
    half = D_MODEL // 2
    n2p_ref[...] = _pack_bf16_pair(n2[:, :half], n2[:, half:])


def _shared_kernel(n2p_ref, h_ref, g2_ref, wsg_ref, wsu_ref, wsd_ref, o_ref):
    lo, hi = _unpack_bf16_pair(n2p_ref[...])
    n2b = jnp.concatenate([lo, hi], axis=1).astype(BF16)
    hid = _silu(jnp.dot(n2b, wsg_ref[...], preferred_element_type=F32)) * jnp.dot(
        n2b, wsu_ref[...], preferred_element_type=F32)
    shared = jnp.dot(hid.astype(BF16), wsd_ref[...], preferred_element_type=F32)
    o_ref[...] = h_ref[...] + g2_ref[0] * shared


def _shared_expert(n2p_all, h, g2, weights, token_offset, tiles_per_batch):
    n = h.shape[0]
    tm = ROW_TILE
    off = token_offset // tm
    mod_rows = g2.shape[1]
    const = lambda a: pl.BlockSpec(a.shape, lambda i: (0,) * a.ndim)
    return pl.pallas_call(
        _shared_kernel,
        grid=(n // tm,),
        in_specs=[pl.BlockSpec((tm, D_MODEL // 2), lambda i: (i + off, 0)),
                  pl.BlockSpec((tm, D_MODEL), lambda i: (i, 0)),
                  pl.BlockSpec((1, mod_rows, D_MODEL), lambda i: (i // tiles_per_batch, 0, 0))]
                 + [const(w) for w in weights],
        out_specs=pl.BlockSpec((tm, D_MODEL), lambda i: (i, 0)),
        out_shape=jax.ShapeDtypeStruct((n, D_MODEL), F32),
        compiler_params=_cparams(("arbitrary",), 32),
        name="shared_expert",
    )(n2p_all, h, g2, *weights)


def _outproj(x2d, oa, yr_in, ga, gb, g1, sc2, sh2, g2, n2g, weights, tiles_per_batch,
             all_tokens, token_offset, carried=None):
    n = x2d.shape[0]
    tm = ROW_TILE
    off = token_offset // tm
    mod_rows = g1.shape[1]
    row = lambda w: pl.BlockSpec((tm, w), lambda i: (i, 0))
    mod_spec = pl.BlockSpec((1, mod_rows, D_MODEL), lambda i: (i // tiles_per_batch, 0, 0))
    const = lambda a: pl.BlockSpec(a.shape, lambda i: (0,) * a.ndim)
    in_specs = [row(D_MODEL), row(D_ATT), row(D_RET_V), row(D_MODEL), row(D_MODEL),
                mod_spec, mod_spec, mod_spec, mod_spec, const(n2g)] + [const(w) for w in weights]
    args = [x2d, oa, yr_in, ga, gb, g1, sc2, sh2, g2, n2g, *weights]
    aliases = {}
    if carried is not None:
        aliases = {len(args): 1, len(args) + 1: 2}
        in_specs += [pl.BlockSpec(memory_space=pl.ANY)] * 2
        args += list(carried)
    return pl.pallas_call(
        _outproj_kernel,
        grid=(n // tm,),
        in_specs=in_specs,
        out_specs=[row(D_MODEL),
                   pl.BlockSpec((tm, D_MODEL // 2), lambda i: (i + off, 0)),
                   pl.BlockSpec((N_EXPERTS, tm), lambda i: (0, i + off))],
        out_shape=[jax.ShapeDtypeStruct((n, D_MODEL), F32),
                   jax.ShapeDtypeStruct((all_tokens, D_MODEL // 2), U32),
                   jax.ShapeDtypeStruct((N_EXPERTS, all_tokens), F32)],
        input_output_aliases=aliases,
        compiler_params=_cparams(("arbitrary",), 48),
        name="outproj",
    )(*args)


def _route_kernel(s_ref, b_ref, idx_ref, w_ref, rank_ref, cnt_ref, run_ref, tri_ref):
    step = pl.program_id(0)
    t = s_ref.shape[1]

    @pl.when(step == 0)
    def _():
        run_ref[...] = jnp.zeros_like(run_ref)
        r = lax.broadcasted_iota(I32, (t, t), 0)
        c = lax.broadcasted_iota(I32, (t, t), 1)
        tri_ref[...] = jnp.where(r < c, 1.0, 0.0).astype(BF16)

    s = s_ref[...]
    sel = s + b_ref[...]
    row_f = lax.broadcasted_iota(I32, (N_EXPERTS, t), 0).astype(F32)

    def first_argmax(vals, rows):
        m = jnp.max(vals, axis=0, keepdims=True)
        pos = jnp.min(jnp.where(vals == m, rows, float(N_EXPERTS)), axis=0, keepdims=True)
        return m, pos

    gscore = []
    group_row = lax.broadcasted_iota(I32, (GROUP_SIZE, t), 0).astype(F32)
    for g in range(N_GROUPS):
        rows = slice(g * GROUP_SIZE, (g + 1) * GROUP_SIZE)
        m1, p1 = first_argmax(sel[rows], group_row)
        m2 = jnp.max(jnp.where(group_row == p1, -jnp.inf, sel[rows]), axis=0, keepdims=True)
        gscore.append(m1 + m2)
    cand_parts = []
    for g in range(N_GROUPS):
        rows = slice(g * GROUP_SIZE, (g + 1) * GROUP_SIZE)
        beaten_by = jnp.zeros((1, t), F32)
        for o in range(N_GROUPS):
            if o == g:
                continue
            wins = (gscore[o] > gscore[g]) if o > g else (gscore[o] >= gscore[g])
            beaten_by = beaten_by + jnp.where(wins, 1.0, 0.0)
        cand_parts.append(jnp.where(beaten_by < TOPK_GROUPS, sel[rows], -jnp.inf))
    cand = jnp.concatenate(cand_parts, axis=0)

    picked = jnp.zeros((N_EXPERTS, t), F32)
    idx_rows, w_rows = [], []
    for _ in range(TOP_K):
        _, pos = first_argmax(cand, row_f)
        hit = row_f == pos
        w_rows.append(jnp.sum(jnp.where(hit, s, 0.0), axis=0, keepdims=True))
        idx_rows.append(pos)
        picked = jnp.where(hit, 1.0, picked)
        cand = jnp.where(hit, -jnp.inf, cand)
    w_sum = functools.reduce(jnp.add, w_rows)

    before = jnp.dot(picked.astype(BF16), tri_ref[...], preferred_element_type=F32) + run_ref[...]
    run_ref[...] = run_ref[...] + jnp.sum(picked, axis=1, keepdims=True)
    rank_rows = [jnp.sum(jnp.where(row_f == idx_rows[kk], before, 0.0), axis=0, keepdims=True)
                 for kk in range(TOP_K)]

    idx_ref[...] = jnp.concatenate(idx_rows, axis=0).astype(I32)
    w_ref[...] = jnp.concatenate([w / w_sum * ROUTED_SCALE for w in w_rows], axis=0).T
    rank_ref[...] = jnp.concatenate(rank_rows, axis=0).astype(I32)

    @pl.when(step == pl.num_programs(0) - 1)
    def _():
        cnt_ref[...] = run_ref[...].astype(I32)


def _route(scores_t, b_col):
    n = scores_t.shape[1]
    t = ROUTE_TILE
    col = pl.BlockSpec((TOP_K, t), lambda i: (0, i))
    const = pl.BlockSpec((N_EXPERTS, t), lambda i: (0, 0))
    return pl.pallas_call(
        _route_kernel,
        grid=(n // t,),
        in_specs=[pl.BlockSpec((N_EXPERTS, t), lambda i: (0, i)), const],
        out_specs=[col, pl.BlockSpec((t, TOP_K), lambda i: (i, 0)), col, const],
        out_shape=[jax.ShapeDtypeStruct((TOP_K, n), I32),
                   jax.ShapeDtypeStruct((n, TOP_K), F32),
                   jax.ShapeDtypeStruct((TOP_K, n), I32),
                   jax.ShapeDtypeStruct((N_EXPERTS, t), I32)],
        scratch_shapes=[pltpu.VMEM((N_EXPERTS, t), F32), pltpu.VMEM((t, t), BF16)],
        compiler_params=_cparams(("arbitrary",), 32),
        name="route",
    )(scores_t, b_col)


def _dest_kernel(idx_ref, rank_ref, start_ref, dest_ref):
    t = idx_ref.shape[1]
    row = lax.broadcasted_iota(I32, (N_EXPERTS, t), 0)
    starts = start_ref[...]
    base = [jnp.sum(jnp.where(row == idx_ref[kk:kk + 1, :], starts, 0.0), axis=0, keepdims=True)
            for kk in range(TOP_K)]
    dest_ref[...] = jnp.concatenate(base, axis=0).astype(I32) + rank_ref[...]


def _dest(idx_t, rank_t, starts_col):
    n = idx_t.shape[1]
    t = ROUTE_TILE
    col = pl.BlockSpec((TOP_K, t), lambda i: (0, i))
    return pl.pallas_call(
        _dest_kernel,
        grid=(n // t,),
        in_specs=[col, col, pl.BlockSpec((N_EXPERTS, t), lambda i: (0, 0))],
        out_specs=col,
        out_shape=jax.ShapeDtypeStruct((TOP_K, n), I32),
        compiler_params=_cparams(("arbitrary",), 32),
        name="dest",
    )(idx_t, rank_t, starts_col)


def _sc_mesh():
    return plsc.VectorSubcoreMesh(core_axis_name="core", subcore_axis_name="subcore",
                                  num_cores=V7X_SC_CORES, num_subcores=V7X_SC_SUBCORES)


def _sc_dispatch(n2p, dest_kmajor, pad_rows, n_out_rows):
    n, width = n2p.shape
    rows = SC_GATHER_ROWS
    n_workers = V7X_SC_CORES * V7X_SC_SUBCORES
    src_chunks = n // rows
    items = dest_kmajor.shape[0] // rows
    per_worker = items // n_workers
    pad_per_worker = pad_rows.shape[0] // rows // n_workers
    assert src_chunks * rows == n and per_worker * n_workers == items and per_worker % 2 == 0
    assert pad_per_worker * n_workers * rows == pad_rows.shape[0]
    idx3 = dest_kmajor.reshape(n_workers, per_worker, rows)
    pad3 = pad_rows.reshape(n_workers, pad_per_worker, rows)
    zeros = jnp.zeros((rows, width), n2p.dtype)

    def body(src_hbm, idx_hbm, pad_hbm, zero_hbm, out_hbm, idx_v, pad_v, rows_v, load_sem, scat_sem):
        worker = lax.axis_index("subcore") * V7X_SC_CORES + lax.axis_index("core")
        pltpu.sync_copy(idx_hbm.at[worker], idx_v)
        pltpu.sync_copy(pad_hbm.at[worker], pad_v)
        pltpu.sync_copy(zero_hbm, rows_v.at[0])

        def zero_fill(c):
            return pltpu.make_async_copy(rows_v.at[0], out_hbm.at[pad_v.at[c]], scat_sem.at[0])

        @pl.loop(0, pad_per_worker)
        def _(c):
            zero_fill(c).start()

        @pl.loop(0, pad_per_worker)
        def _(c):
            zero_fill(c).wait()

        def load(c, b):
            chunk = lax.rem(worker * per_worker + c, src_chunks)
            off = pl.multiple_of(chunk * rows, rows)
            return pltpu.make_async_copy(src_hbm.at[pl.ds(off, rows)], rows_v.at[b], load_sem.at[b])

        def scatter(c, b):
            return pltpu.make_async_copy(rows_v.at[b], out_hbm.at[idx_v.at[c]], scat_sem.at[b])

        load(0, 0).start()

        @pl.loop(0, per_worker, step=2)
        def _(c0):
            for b in range(2):
                c = c0 + b
                load(c, b).wait()

                @pl.when(c >= 1)
                def _():
                    scatter(c - 1, 1 - b).wait()

                @pl.when(c + 1 < per_worker)
                def _():
                    load(c + 1, 1 - b).start()

                scatter(c, b).start()

        scatter(per_worker - 1, (per_worker - 1) % 2).wait()

    return pl.kernel(
        body, mesh=_sc_mesh(),
        out_type=jax.ShapeDtypeStruct((n_out_rows, width), n2p.dtype),
        scratch_types=[pltpu.VMEM((per_worker, rows), I32),
                       pltpu.VMEM((pad_per_worker, rows), I32),
                       pltpu.VMEM((2, rows, width), n2p.dtype),
                       pltpu.SemaphoreType.DMA((2,)),
                       pltpu.SemaphoreType.DMA((2,))],
        name="sc_dispatch",
    )(n2p, idx3, pad3, zeros)


def _ffn_kernel(texp_ref, ntiles_ref, eslot_ref, enext_ref, nhalf_ref, xs_ref, wg_hbm, wu_hbm, wd_hbm,
                ys_ref, wg_buf, wu_buf, wd_buf, wgu_bf, wd_bf, sems):
    g = pl.program_id(0)

    def weight_copies(e, slot):
        return (pltpu.make_async_copy(wg_hbm.at[e], wg_buf.at[slot], sems.at[slot, 0]),
                pltpu.make_async_copy(wu_hbm.at[e], wu_buf.at[slot], sems.at[slot, 1]),
                pltpu.make_async_copy(wd_hbm.at[e], wd_buf.at[slot], sems.at[slot, 2]))

    @pl.when(g < ntiles_ref[0])
    def _():
        e = texp_ref[g]
        changed = jnp.logical_or(g == 0, texp_ref[jnp.maximum(g - 1, 0)] != e)

        @pl.when(changed)
        def _():
            slot = eslot_ref[e]

            def fetch_ahead(first_hop, hops, target_slot):
                ahead = first_hop
                for _ in range(hops - 1):
                    ahead = jnp.where(ahead >= 0, enext_ref[jnp.maximum(ahead, 0)], -1)

                @pl.when(ahead >= 0)
                def _():
                    for c in weight_copies(ahead, target_slot):
                        c.start()

            @pl.when(g == 0)
            def _():
                for c in weight_copies(e, slot):
                    c.start()
                for hops in range(1, FFN_WEIGHT_SLOTS - 1):
                    fetch_ahead(enext_ref[e], hops, lax.rem(slot + hops, FFN_WEIGHT_SLOTS))

            fetch_ahead(enext_ref[e], FFN_WEIGHT_SLOTS - 1,
                        lax.rem(slot + FFN_WEIGHT_SLOTS - 1, FFN_WEIGHT_SLOTS))
            for c in weight_copies(e, slot):
                c.wait()

            wgu_bf[:, :D_EXPERT] = wg_buf[slot].astype(BF16)
            wgu_bf[:, D_EXPERT:] = wu_buf[slot].astype(BF16)
            wd_bf[...] = wd_buf[slot].astype(BF16)

        def expert_rows(rows):
            lo, hi = _unpack_bf16_pair(xs_ref[rows, :])
            x = jnp.concatenate([lo, hi], axis=1).astype(BF16)
            gu = jnp.dot(x, wgu_bf[...], preferred_element_type=F32)
            hid = (_silu(gu[:, :D_EXPERT]) * gu[:, D_EXPERT:]).astype(BF16)
            y = jnp.dot(hid, wd_bf[...], preferred_element_type=F32)
            half = D_MODEL // 2
            ys_ref[rows, :] = _pack_bf16_pair(y[:, :half], y[:, half:])

        for n_groups in FFN_TILE_SIZES:

            @pl.when(nhalf_ref[g] == n_groups)
            def _(n_groups=n_groups):
                used = n_groups * FFN_FINE
                expert_rows(slice(0, used))
                if used < FFN_TILE:
                    ys_ref[used:, :] = jnp.zeros((FFN_TILE - used, ys_ref.shape[1]), U32)


def _ffn(tile_expert, n_tiles, expert_slot, expert_next, tile_halves, xs, w_gate, w_up, w_down):
    rows, width = xs.shape
    m = FFN_TILE
    max_tiles = tile_expert.shape[0]
    row_map = lambda g, te, nt, es, en, nh: (jnp.minimum(g, nt[0] - 1), 0)
    hbm = pl.BlockSpec(memory_space=pl.ANY)
    grid_spec = pltpu.PrefetchScalarGridSpec(
        num_scalar_prefetch=5,
        grid=(max_tiles,),
        in_specs=[pl.BlockSpec((m, width), row_map), hbm, hbm, hbm],
        out_specs=pl.BlockSpec((m, width), row_map),
        scratch_shapes=[pltpu.VMEM((FFN_WEIGHT_SLOTS, D_MODEL, D_EXPERT), F32),
                        pltpu.VMEM((FFN_WEIGHT_SLOTS, D_MODEL, D_EXPERT), F32),
                        pltpu.VMEM((FFN_WEIGHT_SLOTS, D_EXPERT, D_MODEL), F32),
                        pltpu.VMEM((D_MODEL, 2 * D_EXPERT), BF16),
                        pltpu.VMEM((D_EXPERT, D_MODEL), BF16),
                        pltpu.SemaphoreType.DMA((FFN_WEIGHT_SLOTS, 3))],
    )
    return pl.pallas_call(
        _ffn_kernel,
        grid_spec=grid_spec,
        out_shape=jax.ShapeDtypeStruct((rows, width), U32),
        compiler_params=_cparams(("arbitrary",), 32),
        name="ffn",
    )(tile_expert, n_tiles, expert_slot, expert_next, tile_halves, xs, w_gate, w_up, w_down)


def _ffn_plan(counts, n_assign):
    m = FFN_TILE
    max_tiles = n_assign // m + N_EXPERTS
    padded = ((counts + m - 1) // m) * m
    pend = jnp.cumsum(padded).astype(I32)
    pstart = pend - padded
    n_tiles = pend[-1:] // m
    g = jnp.minimum(jnp.arange(max_tiles, dtype=I32), n_tiles - 1)
    tile_expert = jnp.sum((pend[None, :] <= (g * m)[:, None]).astype(I32), axis=1)
    tile_expert = jnp.minimum(tile_expert, N_EXPERTS - 1)
    full_rows = jnp.maximum(padded - m, 0)
    last_rows = counts - full_rows
    round_to = lambda v, q: ((v + q - 1) // q) * q
    last_rows = jnp.where(last_rows > FFN_FINE_FROM, round_to(last_rows, FFN_FINE),
                          round_to(last_rows, FFN_QUANTUM))
    vend = pstart + full_rows + last_rows
    own = tile_expert[:, None] == jnp.arange(N_EXPERTS, dtype=I32)[None, :]
    tile_vend = jnp.sum(jnp.where(own, vend[None, :], 0), axis=1)
    tile_halves = (jnp.clip(tile_vend - g * m, 0, m) // FFN_FINE).astype(I32)
    used = counts > 0
    expert_slot = ((jnp.cumsum(used.astype(I32)) - 1) % FFN_WEIGHT_SLOTS).astype(I32)
    ids = jnp.where(used, jnp.arange(N_EXPERTS, dtype=I32), N_EXPERTS)
    first_used_from = lax.cummin(ids, axis=0, reverse=True)
    nxt = jnp.concatenate([first_used_from[1:], jnp.full((1,), N_EXPERTS, I32)])
    expert_next = jnp.where(nxt < N_EXPERTS, nxt, -1).astype(I32)
    spare_row = max_tiles * m
    j = jnp.arange(FFN_QUANTUM, dtype=I32)[None, :]
    seg_end = (pstart + counts)[:, None]
    spare = spare_row + (jnp.arange(N_EXPERTS, dtype=I32)[:, None] * FFN_QUANTUM + j) % SPARE_ROWS
    pad_rows = jnp.where(j < (vend[:, None] - seg_end), seg_end + j, spare).astype(I32).reshape(-1)
    unit = SC_GATHER_ROWS * V7X_SC_CORES * V7X_SC_SUBCORES
    extra = (-pad_rows.shape[0]) % unit
    filler = spare_row + (pad_rows.shape[0] + jnp.arange(extra, dtype=I32)) % SPARE_ROWS
    pad_rows = jnp.concatenate([pad_rows, filler])
    return (tile_expert, n_tiles, expert_slot, expert_next, tile_halves, pstart, pad_rows,
            spare_row + SPARE_ROWS)


def _sc_gather_rows(table, idx):
    n_idx = idx.shape[0]
    width = table.shape[1]
    n_workers = V7X_SC_CORES * V7X_SC_SUBCORES
    per_worker = n_idx // n_workers
    n_chunks = per_worker // SC_GATHER_ROWS
    assert per_worker * n_workers == n_idx and n_chunks * SC_GATHER_ROWS == per_worker and n_chunks % 2 == 0

    def body(table_hbm, idx_hbm, out_hbm, idx_v, rows_v, gather_sem, write_sem):
        worker = lax.axis_index("subcore") * V7X_SC_CORES + lax.axis_index("core")
        base = worker * per_worker
        pltpu.sync_copy(idx_hbm.at[pl.ds(base, per_worker)], idx_v)

        def gather(c, b):
            off = pl.multiple_of(c * SC_GATHER_ROWS, SC_GATHER_ROWS)
            return pltpu.make_async_copy(table_hbm.at[idx_v.at[pl.ds(off, SC_GATHER_ROWS)]],
                                         rows_v.at[b], gather_sem.at[b])

        def write(c, b):
            off = pl.multiple_of(c * SC_GATHER_ROWS, SC_GATHER_ROWS)
            return pltpu.make_async_copy(rows_v.at[b], out_hbm.at[pl.ds(base + off, SC_GATHER_ROWS)],
                                         write_sem.at[b])

        gather(0, 0).start()

        @pl.loop(0, n_chunks, step=2)
        def _(c0):
            for b in range(2):
                c = c0 + b
                gather(c, b).wait()

                @pl.when(c >= 1)
                def _():
                    write(c - 1, 1 - b).wait()

                @pl.when(c + 1 < n_chunks)
                def _():
                    gather(c + 1, 1 - b).start()

                write(c, b).start()

        write(n_chunks - 1, (n_chunks - 1) % 2).wait()

    return pl.kernel(
        body, mesh=_sc_mesh(),
        out_type=jax.ShapeDtypeStruct((n_idx, width), table.dtype),
        scratch_types=[pltpu.VMEM((per_worker,), I32),
                       pltpu.VMEM((2, SC_GATHER_ROWS, width), table.dtype),
                       pltpu.SemaphoreType.DMA((2,)),
                       pltpu.SemaphoreType.DMA((2,))],
        name="sc_gather_rows",
    )(table, idx)


def _combine_kernel(w_ref, h_ref, g2_ref, nf_ref, yk_ref, y_ref):
    t = h_ref.shape[0]
    w = w_ref[...]
    acc_lo = jnp.zeros((t, D_MODEL // 2), F32)
    acc_hi = jnp.zeros((t, D_MODEL // 2), F32)
    for kk in range(TOP_K):
        lo, hi = _unpack_bf16_pair(yk_ref[kk])
        wk = w[:, kk:kk + 1]
        acc_lo = acc_lo + wk * lo
        acc_hi = acc_hi + wk * hi
    out = h_ref[...] + g2_ref[0] * jnp.concatenate([acc_lo, acc_hi], axis=1)
    y_ref[...] = out * lax.rsqrt(jnp.mean(out * out, axis=-1, keepdims=True) + EPS) * nf_ref[...]


def _combine(w, h2, g2, normf, y_by_k, row_offset, tiles_per_batch):
    n = h2.shape[0]
    t = MOVE_TILE
    off = row_offset // t
    mod_rows = g2.shape[1]
    mod_tiles = max(ROW_TILE // t, 1) * tiles_per_batch if mod_rows == 1 else n // t
    return pl.pallas_call(
        _combine_kernel,
        grid=(n // t,),
        in_specs=[pl.BlockSpec((t, TOP_K), lambda i: (i + off, 0)),
                  pl.BlockSpec((t, D_MODEL), lambda i: (i, 0)),
                  pl.BlockSpec((1, mod_rows if mod_rows == 1 else t, D_MODEL),
                               (lambda i: (i // mod_tiles, 0, 0)) if mod_rows == 1
                               else (lambda i: (0, i, 0))),
                  pl.BlockSpec((1, D_MODEL), lambda i: (0, 0)),
                  pl.BlockSpec((TOP_K, t, D_MODEL // 2), lambda i: (0, i + off, 0))],
        out_specs=pl.BlockSpec((t, D_MODEL), lambda i: (i, 0)),
        out_shape=jax.ShapeDtypeStruct((n, D_MODEL), F32),
        compiler_params=_cparams(("arbitrary",), 40),
        name="combine",
    )(w, h2, g2, normf, y_by_k)


def _rotary_tables(pos):
    half = KEY_DIM_R // 2
    inv_freq = ROPE_BASE ** (-jnp.arange(half, dtype=F32) / half)
    ang = pos[:, None] * inv_freq[None, :]
    cos = jnp.cos(ang)
    sin = jnp.sin(ang)
    cos_t = jnp.concatenate([cos, cos], axis=1)
    sin_t = jnp.concatenate([-sin, sin], axis=1)
    return cos_t.astype(F32), sin_t.astype(F32)


def _rel_bias_table(rel_bias, n_rows, n_cols, q_offset):
    heads = rel_bias.shape[0]
    n_diag = n_rows + n_cols - 1
    dist = q_offset + (n_rows - 1) - np.arange(n_diag)
    idx = np.clip(dist, -REL_CLIP, REL_CLIP) + REL_CLIP
    n_hi = int(np.sum(dist > REL_CLIP))
    n_lo = int(np.sum(dist < -REL_CLIP))
    mid = rel_bias[:, int(idx[n_diag - n_lo - 1]):int(idx[n_hi]) + 1][:, ::-1]
    diag = jnp.concatenate([jnp.broadcast_to(rel_bias[:, 2 * REL_CLIP:], (heads, n_hi)), mid,
                            jnp.broadcast_to(rel_bias[:, :1], (heads, n_lo))], axis=1)
    period = n_diag + 1
    v = jnp.roll(jnp.pad(diag, ((0, 0), (0, 1))), -(n_rows - 1), axis=1)
    skew = jnp.tile(v, (1, n_rows))[:, :n_rows * (period - 1)].reshape(heads, n_rows, period - 1)
    return skew[:, :, :n_cols].astype(F32)


def _prompt_bias(rel_bias):
    n_cols = ATT_QB + ATT_WINDOW
    r = np.arange(ATT_QB)[:, None]
    c = np.arange(n_cols)[None, :]
    band = c - (r // CHUNK) * CHUNK
    valid = (band >= 0) & (band < ATT_WINDOW + CHUNK)
    table = _rel_bias_table(rel_bias, ATT_QB, n_cols, ATT_WINDOW)
    return jnp.where(jnp.asarray(valid)[None], table, NEG_BIG)


def _sample_bias(rel_bias, t_new, cache_len):
    b = _rel_bias_table(rel_bias, t_new, cache_len + t_new, cache_len)
    return b[:, :, :cache_len], b[:, :, cache_len:]


def _mod_parts(mod, rows_each):
    parts = jnp.split(mod, 6, axis=-1)
    if rows_each == 1:
        return [p[:, None, :] for p in parts]
    return [jnp.repeat(p, rows_each, axis=0)[None] for p in parts]


def kernel(x_prompt, x_sample, cache_attn_k, cache_attn_v, state_ret, c_prompt, c_sample,
           norm1_g, norm2_g, w_ada, b_ada, w_in, rel_bias, w_o_attn, w_o_ret, w_out,
           w_router, b_router, w_exp_gate, w_exp_up, w_exp_down, w_sh_gate, w_sh_up, w_sh_down,
           normf_g):
    batch, seq, d = x_prompt.shape
    dec_batch, dec_seq, _ = x_sample.shape
    depth = w_in.shape[0]
    assert depth == 1 and d == D_MODEL
    assert seq % ROW_TILE == 0 and dec_batch * dec_seq == ROW_TILE and ROW_TILE == ATT_WINDOW
    cache_len = cache_attn_k.shape[2]
    n_p = batch * seq
    n_s = dec_batch * dec_seq
    tpb = seq // ROW_TILE
    l = 0

    bf = lambda a: a.astype(BF16)
    c_all = jnp.concatenate([c_prompt, c_sample], axis=0)
    pad = (-c_all.shape[0]) % 8
    c_all = jnp.pad(c_all, ((0, pad), (0, 0)))
    mod = _ada(c_all, bf(w_ada[l]), b_ada[l][None, :])
    mod_p = _mod_parts(mod[:batch], 1)
    mod_s = _mod_parts(mod[batch:batch + dec_batch], dec_seq)

    w_in_bf = bf(w_in[l])
    n1g = norm1_g[l][None, :]
    n2g = norm2_g[l][None, :]
    dense_w = [bf(w_o_attn[l]), bf(w_o_ret[l]), bf(w_out[l]), bf(w_router[l]).T,
               bf(w_sh_gate[l]), bf(w_sh_up[l]), bf(w_sh_down[l])]

    xp = x_prompt.reshape(n_p, d)
    xs_ = x_sample.reshape(n_s, d)
    cos_p, sin_p = _rotary_tables(jnp.arange(seq, dtype=F32))
    pos_s = PAST_LEN + jnp.arange(dec_seq, dtype=F32)
    cos_s, sin_s = _rotary_tables(jnp.tile(pos_s, dec_batch))

    (qa, ka, va, qr, kr, vr, gr, ga, gb, kv_p) = _inproj(
        xp, mod_p[1], mod_p[0], n1g, cos_p, sin_p, w_in_bf, tpb)
    oa = _attn_prompt(qa, ka, va, _prompt_bias(rel_bias[l]), batch, seq)
    zero_state = jnp.zeros((batch, N_HEADS, KEY_DIM_R, VAL_DIM_R), F32)
    yr_in, state_p = _retention(qr, kr, vr, gr, zero_state, batch, seq, RET_CHUNK)
    h_p, n2p_p, s_p = _outproj(xp, oa, yr_in, ga, gb, mod_p[2], mod_p[4], mod_p[3], mod_p[5], n2g,
                               dense_w[:4], tpb, n_p + n_s, 0)

    (qa_s, ka_s, va_s, qr_s, kr_s, vr_s, gr_s, ga_s, gb_s, kv_s) = _inproj(
        xs_, mod_s[1], mod_s[0], n1g, cos_s, sin_s, w_in_bf, 1)
    bias_c, bias_n = _sample_bias(rel_bias[l], dec_seq, cache_len)
    to_keys_minor = lambda c: jnp.transpose(c, (0, 1, 3, 4, 2))
    oa_s = _attn_sample(qa_s, ka_s, va_s, to_keys_minor(cache_attn_k), to_keys_minor(cache_attn_v),
                        bias_c, bias_n, dec_batch, dec_seq, cache_len)
    yr_in_s, state_s = _retention(qr_s, kr_s, vr_s, gr_s, state_ret[l], dec_batch, dec_seq, dec_seq)
    h_s, n2p, scores_t = _outproj(xs_, oa_s, yr_in_s, ga_s, gb_s, mod_s[2], mod_s[4], mod_s[3],
                                  mod_s[5], n2g, dense_w[:4], 1, n_p + n_s, n_p, carried=(n2p_p, s_p))

    lanes_of = lambda v: jnp.broadcast_to(v[:, None], (N_EXPERTS, ROUTE_TILE))
    idx_t, w_t, rank_t, counts = _route(scores_t, lanes_of(b_router[l]))
    (tile_expert, n_tiles, expert_slot, expert_next, tile_halves, pstart, pad_rows,
     n_sorted_rows) = _ffn_plan(counts[:, 0], (n_p + n_s) * TOP_K)
    dest_t = _dest(idx_t, rank_t, lanes_of(pstart.astype(F32)))
    dest_kmajor = dest_t.reshape(-1)
    w_route = w_t
    xs_sorted = _sc_dispatch(n2p, dest_kmajor, pad_rows, n_sorted_rows)
    h_p = _shared_expert(n2p, h_p, mod_p[5], dense_w[4:], 0, tpb)
    h_s = _shared_expert(n2p, h_s, mod_s[5], dense_w[4:], n_p, 1)
    ys_sorted = _ffn(tile_expert, n_tiles, expert_slot, expert_next, tile_halves, xs_sorted,
                     w_exp_gate[l], w_exp_up[l], w_exp_down[l])
    nf = normf_g[None, :]
    y_by_k = _sc_gather_rows(ys_sorted, dest_kmajor).reshape(TOP_K, n_p + n_s, d // 2)
    y_p = _combine(w_route, h_p, mod_p[5], nf, y_by_k, 0, tpb)
    y_s = _combine(w_route, h_s, mod_s[5], nf, y_by_k, n_p, 1)

    keep = min(ATT_WINDOW, seq)
    kv_p = kv_p.reshape(batch, ROW_TILE, 2, N_HEADS, HEAD_DIM_A)[:, ROW_TILE - keep:]
    kv_s = kv_s.reshape(dec_batch, dec_seq, 2, N_HEADS, HEAD_DIM_A)
    return (y_p.reshape(batch, seq, d), y_s.reshape(dec_batch, dec_seq, d),
            kv_p[:, :, 0][None], kv_p[:, :, 1][None], state_p[None],
            kv_s[:, :, 0][None], kv_s[:, :, 1][None], state_s[None])
```

```python
import functools

import numpy as np
import jax
import jax.numpy as jnp
from jax import lax
from jax.experimental import pallas as pl
from jax.experimental.pallas import tpu as pltpu
from jax.experimental.pallas import tpu_sc as plsc

F32 = jnp.float32
BF16 = jnp.bfloat16
I32 = jnp.int32
U32 = jnp.uint32

D_MODEL = 1024
PAST_LEN = 4096
CHUNK = 64
N_LEFT_CHUNKS = 8
ATT_WINDOW = N_LEFT_CHUNKS * CHUNK
N_HEADS = 8
HEAD_DIM_A = 64
D_ATT = N_HEADS * HEAD_DIM_A
REL_CLIP = 128
KEY_DIM_R = 64
VAL_DIM_R = 128
D_RET_K = N_HEADS * KEY_DIM_R
D_RET_V = N_HEADS * VAL_DIM_R
ROPE_BASE = 10000.0
N_EXPERTS = 256
TOP_K = 8
N_GROUPS = 8
GROUP_SIZE = N_EXPERTS // N_GROUPS
TOPK_GROUPS = 4
D_EXPERT = 256
ROUTED_SCALE = 2.5
EPS = 1e-6
IN_WIDTHS = (D_ATT, D_ATT, D_ATT, D_RET_K, D_RET_K, D_RET_V, D_RET_V, D_MODEL, D_MODEL)
IN_OFFS = tuple(int(v) for v in np.cumsum((0,) + IN_WIDTHS))
D_IN = IN_OFFS[-1]

NEG_BIG = -1e30
V7X_VMEM_BYTES = 64 * 1024 * 1024
V7X_SC_CORES = 2
V7X_SC_SUBCORES = 16
SC_GATHER_ROWS = 96
SPARE_ROWS = 8192

ROW_TILE = 512
ATT_QB = 256
RET_CHUNK = 256
ROUTE_TILE = 512
MOVE_TILE = 512
FFN_QUANTUM = 128
FFN_FINE = 64
FFN_FINE_FROM = 384
FFN_TILE = 5 * FFN_QUANTUM
FFN_TILE_SIZES = tuple(n for n in range(1, FFN_TILE // FFN_FINE + 1)
                       if n * FFN_FINE > FFN_FINE_FROM or (n * FFN_FINE) % FFN_QUANTUM == 0)
FFN_WEIGHT_SLOTS = 3
VMEM_RESERVE_BYTES = 6 << 20


def _cparams(semantics, vmem_mb):
    return pltpu.CompilerParams(dimension_semantics=semantics,
                                vmem_limit_bytes=min(vmem_mb << 20, V7X_VMEM_BYTES - VMEM_RESERVE_BYTES))


def _silu(x):
    return x * jax.nn.sigmoid(x)


def _pack_bf16_pair(lo, hi):
    return pltpu.bitcast(pltpu.pack_elementwise([lo, hi], packed_dtype=BF16), U32)


def _unpack_bf16_pair(u):
    words = pltpu.bitcast(u, I32)
    lo = pltpu.unpack_elementwise(words, index=0, packed_dtype=BF16, unpacked_dtype=F32)
    hi = pltpu.unpack_elementwise(words, index=1, packed_dtype=BF16, unpacked_dtype=F32)
    return lo, hi


def _ada_kernel(c_ref, w_ref, b_ref, o_ref):
    sc = _silu(c_ref[...]).astype(BF16)
    o_ref[...] = jnp.dot(sc, w_ref[...], preferred_element_type=F32) + b_ref[...]


def _ada(c_all, w_ada_bf, b_ada):
    rows = c_all.shape[0]
    n_out = w_ada_bf.shape[1]
    blk = D_MODEL
    return pl.pallas_call(
        _ada_kernel,
        grid=(n_out // blk,),
        in_specs=[pl.BlockSpec((rows, D_MODEL), lambda j: (0, 0)),
                  pl.BlockSpec((D_MODEL, blk), lambda j: (0, j)),
                  pl.BlockSpec((1, blk), lambda j: (0, j))],
        out_specs=pl.BlockSpec((rows, blk), lambda j: (0, j)),
        out_shape=jax.ShapeDtypeStruct((rows, n_out), F32),
        compiler_params=_cparams(("arbitrary",), 24),
        name="ada",
    )(c_all, w_ada_bf, b_ada)


def _inproj_kernel(x_ref, sc_ref, sh_ref, g_ref, cos_ref, sin_ref, w_ref,
                   qa_ref, ka_ref, va_ref, qr_ref, kr_ref, vr_ref, gr_ref, ga_ref, gb_ref,
                   kv_ref, *, tiles_per_batch):
    x = x_ref[...]
    xn = x * lax.rsqrt(jnp.mean(x * x, axis=-1, keepdims=True) + EPS) * g_ref[...]
    nb = (xn * (1.0 + sc_ref[0]) + sh_ref[0]).astype(BF16)

    def proj(seg):
        return jnp.dot(nb, w_ref[:, IN_OFFS[seg]:IN_OFFS[seg + 1]], preferred_element_type=F32)

    qa_ref[...] = proj(0).astype(BF16)
    ka = proj(1)
    va = proj(2)
    ka_ref[...] = ka.astype(BF16)
    va_ref[...] = va.astype(BF16)

    @pl.when(pl.program_id(0) % tiles_per_batch == tiles_per_batch - 1)
    def _():
        kv_ref[:, :D_ATT] = ka
        kv_ref[:, D_ATT:] = va

    cos = jnp.tile(cos_ref[...], (1, N_HEADS))
    sin = jnp.tile(sin_ref[...], (1, N_HEADS))
    first_half = (lax.broadcasted_iota(I32, (1, D_RET_K), 1) % KEY_DIM_R) < (KEY_DIM_R // 2)

    def rotary(t):
        partner = jnp.where(first_half, pltpu.roll(t, D_RET_K - KEY_DIM_R // 2, 1),
                            pltpu.roll(t, KEY_DIM_R // 2, 1))
        return t * cos + partner * sin

    qr_ref[...] = rotary(proj(3)).astype(BF16)
    kr_ref[...] = (rotary(proj(4)) * (KEY_DIM_R ** -0.5)).astype(BF16)
    vr_ref[...] = proj(5).astype(BF16)
    gr_ref[...] = proj(6).astype(BF16)
    ga_ref[...] = proj(7).astype(BF16)
    gb_ref[...] = proj(8).astype(BF16)


def _inproj(x2d, sc, sh, g, cos_t, sin_t, w_in_bf, tiles_per_batch):
    n = x2d.shape[0]
    tm = ROW_TILE
    n_tiles = n // tm
    n_batches = n_tiles // tiles_per_batch
    mod_rows = sc.shape[1]
    pos_tiles = cos_t.shape[0] // tm

    def row_spec(width):
        return pl.BlockSpec((tm, width), lambda i: (i, 0))

    mod_spec = pl.BlockSpec((1, mod_rows, D_MODEL), lambda i: (i // tiles_per_batch, 0, 0))
    pos_spec = pl.BlockSpec((tm, KEY_DIM_R), lambda i: (i % pos_tiles, 0))
    out_widths = (D_ATT, D_ATT, D_ATT, D_RET_K, D_RET_K, D_RET_V, D_RET_V, D_MODEL, D_MODEL)
    out_shape = [jax.ShapeDtypeStruct((n, w), BF16) for w in out_widths]
    out_shape.append(jax.ShapeDtypeStruct((n_batches * tm, 2 * D_ATT), F32))
    out_specs = [row_spec(w) for w in out_widths]
    out_specs.append(pl.BlockSpec((tm, 2 * D_ATT), lambda i: (i // tiles_per_batch, 0)))
    return pl.pallas_call(
        functools.partial(_inproj_kernel, tiles_per_batch=tiles_per_batch),
        grid=(n_tiles,),
        in_specs=[row_spec(D_MODEL), mod_spec, mod_spec,
                  pl.BlockSpec((1, D_MODEL), lambda i: (0, 0)),
                  pos_spec, pos_spec,
                  pl.BlockSpec((D_MODEL, D_IN), lambda i: (0, 0))],
        out_specs=out_specs,
        out_shape=out_shape,
        compiler_params=_cparams(("arbitrary",), 56),
        name="inproj",
    )(x2d, sc, sh, g, cos_t, sin_t, w_in_bf)


def _softmax_pv(s, v_parts):
    m = functools.reduce(jnp.maximum, [jnp.max(t, axis=-1, keepdims=True) for t in s])
    ps = [jnp.exp(t - m) for t in s]
    l = functools.reduce(jnp.add, [jnp.sum(p, axis=-1, keepdims=True) for p in ps])
    o = functools.reduce(jnp.add, [jnp.dot(p.astype(BF16), v, preferred_element_type=F32)
                                   for p, v in zip(ps, v_parts)])
    return o / l


def _attn_prompt_kernel(q_ref, k0_ref, k1_ref, k2_ref, v0_ref, v1_ref, v2_ref, bias_ref, o_ref):
    j = pl.program_id(1)
    q = q_ref[...]
    k = jnp.concatenate([k0_ref[...], k1_ref[...], k2_ref[...]], axis=0)
    v = jnp.concatenate([v0_ref[...], v1_ref[...], v2_ref[...]], axis=0)
    n_keys = k.shape[0]

    def attend(before_start):
        outs = []
        for h in range(N_HEADS):
            sl = slice(h * HEAD_DIM_A, (h + 1) * HEAD_DIM_A)
            qh = (q[:, sl].astype(F32) * (HEAD_DIM_A ** -0.5)).astype(BF16)
            s = lax.dot_general(qh, k[:, sl], (((1,), (1,)), ((), ())), preferred_element_type=F32)
            s = s + bias_ref[h]
            if before_start is not None:
                s = s + before_start
            outs.append(_softmax_pv([s], [v[:, sl]]))
        o_ref[...] = jnp.concatenate(outs, axis=1).astype(BF16)

    @pl.when(j >= 2)
    def _():
        attend(None)

    @pl.when(j < 2)
    def _():
        key_block = lax.broadcasted_iota(I32, (1, n_keys), 1) // ATT_QB
        attend(jnp.where(key_block < 2 - j, NEG_BIG, 0.0))


def _attn_prompt(q, k, v, bias_full, batch, seq):
    qb = ATT_QB
    nq = seq // qb

    def q_map(b, j):
        return (b * nq + j, 0)

    def kv_map(back):
        return lambda b, j: (b * nq + jnp.maximum(j - back, 0), 0)

    blk = lambda m: pl.BlockSpec((qb, D_ATT), m)
    return pl.pallas_call(
        _attn_prompt_kernel,
        grid=(batch, nq),
        in_specs=[blk(q_map), blk(kv_map(2)), blk(kv_map(1)), blk(kv_map(0)),
                  blk(kv_map(2)), blk(kv_map(1)), blk(kv_map(0)),
                  pl.BlockSpec(bias_full.shape, lambda b, j: (0, 0, 0))],
        out_specs=blk(q_map),
        out_shape=jax.ShapeDtypeStruct((batch * seq, D_ATT), BF16),
        compiler_params=_cparams(("parallel", "arbitrary"), 40),
        name="attn_prompt",
    )(q, k, k, k, v, v, v, bias_full)


SAMPLE_ATT_BATCHES = 2


def _attn_sample_kernel(q_ref, kn_ref, vn_ref, ck_ref, cv_ref, bc_ref, bn_ref, o_ref, *, t_new):
    nt = (((1,), (1,)), ((), ()))
    for b in range(SAMPLE_ATT_BATCHES):
        rows = slice(b * t_new, (b + 1) * t_new)
        q = q_ref[rows, :]
        kn = kn_ref[rows, :]
        vn = vn_ref[rows, :]
        outs = []
        for h in range(N_HEADS):
            sl = slice(h * HEAD_DIM_A, (h + 1) * HEAD_DIM_A)
            qh = (q[:, sl].astype(F32) * (HEAD_DIM_A ** -0.5)).astype(BF16)
            kc_t = ck_ref[b, h].astype(BF16)
            vc_t = cv_ref[b, h].astype(BF16)
            s_c = jnp.dot(qh, kc_t, preferred_element_type=F32) + bc_ref[h]
            s_n = lax.dot_general(qh, kn[:, sl], nt, preferred_element_type=F32) + bn_ref[h]
            m = jnp.maximum(jnp.max(s_c, axis=-1, keepdims=True), jnp.max(s_n, axis=-1, keepdims=True))
            p_c = jnp.exp(s_c - m)
            p_n = jnp.exp(s_n - m)
            l = jnp.sum(p_c, axis=-1, keepdims=True) + jnp.sum(p_n, axis=-1, keepdims=True)
            o = (lax.dot_general(p_c.astype(BF16), vc_t, nt, preferred_element_type=F32)
                 + jnp.dot(p_n.astype(BF16), vn[:, sl], preferred_element_type=F32))
            outs.append(o / l)
        o_ref[rows, :] = jnp.concatenate(outs, axis=1).astype(BF16)


def _attn_sample(q, k, v, cache_k_t, cache_v_t, bias_cache, bias_new, batch, t_new, cache_len):
    nb = SAMPLE_ATT_BATCHES
    blk = pl.BlockSpec((nb * t_new, D_ATT), lambda b: (b, 0))
    cblk = pl.BlockSpec((None, nb, N_HEADS, HEAD_DIM_A, cache_len), lambda b: (0, b, 0, 0, 0))
    return pl.pallas_call(
        functools.partial(_attn_sample_kernel, t_new=t_new),
        grid=(batch // nb,),
        in_specs=[blk, blk, blk, cblk, cblk,
                  pl.BlockSpec(bias_cache.shape, lambda b: (0, 0, 0)),
                  pl.BlockSpec(bias_new.shape, lambda b: (0, 0, 0))],
        out_specs=blk,
        out_shape=jax.ShapeDtypeStruct((batch * t_new, D_ATT), BF16),
        compiler_params=_cparams(("arbitrary",), 40),
        name="attn_sample",
    )(q, k, v, cache_k_t, cache_v_t, bias_cache, bias_new)


def _ret_kernel(q_ref, k_ref, v_ref, g_ref, s0_ref, dmask_ref, qdec_ref, kdec_ref, sdec_ref,
                y_ref, sout_ref, state_ref):
    c = pl.program_id(1)

    @pl.when(c == 0)
    def _():
        state_ref[...] = s0_ref[0]

    q = q_ref[...]
    k = k_ref[...]
    v = v_ref[...]
    g = g_ref[...]
    outs = []
    for h in range(N_HEADS):
        ks = slice(h * KEY_DIM_R, (h + 1) * KEY_DIM_R)
        vs = slice(h * VAL_DIM_R, (h + 1) * VAL_DIM_R)
        qh, kh, vh = q[:, ks], k[:, ks], v[:, vs]
        scores = lax.dot_general(qh, kh, (((1,), (1,)), ((), ())), preferred_element_type=F32)
        inner = jnp.dot((scores * dmask_ref[h]).astype(BF16), vh, preferred_element_type=F32)
        state = state_ref[h]
        cross = jnp.dot(qh, state.astype(BF16), preferred_element_type=F32) * qdec_ref[h]
        o = inner + cross
        v_dec = (vh.astype(F32) * kdec_ref[h]).astype(BF16)
        state_ref[h] = sdec_ref[h] * state + lax.dot_general(
            kh, v_dec, (((0,), (0,)), ((), ())), preferred_element_type=F32)
        on = o * lax.rsqrt(jnp.mean(o * o, axis=-1, keepdims=True) + EPS)
        outs.append(on * _silu(g[:, vs].astype(F32)))
    y_ref[...] = jnp.concatenate(outs, axis=1).astype(BF16)

    @pl.when(c == pl.num_programs(1) - 1)
    def _():
        sout_ref[0] = state_ref[...]


def _ret_tables(chunk):
    log_g = jnp.log(1.0 - jnp.exp2(-5.0 - jnp.arange(N_HEADS, dtype=F32)))
    i = jnp.arange(chunk, dtype=F32)
    diff = i[:, None] - i[None, :]
    dmask = jnp.where(diff >= 0, jnp.exp(log_g[:, None, None] * jnp.maximum(diff, 0.0)), 0.0)
    qdec = jnp.exp(log_g[:, None] * (i + 1.0))
    kdec = jnp.exp(log_g[:, None] * (chunk - 1.0 - i))
    sdec = jnp.exp(log_g * chunk)
    bc = lambda t: jnp.broadcast_to(t[:, :, None], (N_HEADS, t.shape[1], VAL_DIM_R)).astype(F32)
    sdec_t = jnp.broadcast_to(sdec[:, None, None], (N_HEADS, 1, VAL_DIM_R)).astype(F32)
    return dmask.astype(F32), bc(qdec), bc(kdec), sdec_t


def _retention(q, k, v, gate, state0, batch, seq, chunk):
    nc = seq // chunk
    dmask, qdec, kdec, sdec = _ret_tables(chunk)
    row = lambda w: pl.BlockSpec((chunk, w), lambda b, c: (b * nc + c, 0))
    const = lambda a: pl.BlockSpec(a.shape, lambda b, c: (0,) * a.ndim)
    st_spec = pl.BlockSpec((1, N_HEADS, KEY_DIM_R, VAL_DIM_R), lambda b, c: (b, 0, 0, 0))
    return pl.pallas_call(
        _ret_kernel,
        grid=(batch, nc),
        in_specs=[row(D_RET_K), row(D_RET_K), row(D_RET_V), row(D_RET_V), st_spec,
                  const(dmask), const(qdec), const(kdec), const(sdec)],
        out_specs=[row(D_RET_V), st_spec],
        out_shape=[jax.ShapeDtypeStruct((batch * seq, D_RET_V), BF16),
                   jax.ShapeDtypeStruct((batch, N_HEADS, KEY_DIM_R, VAL_DIM_R), F32)],
        scratch_shapes=[pltpu.VMEM((N_HEADS, KEY_DIM_R, VAL_DIM_R), F32)],
        compiler_params=_cparams(("parallel", "arbitrary"), 32),
        name="retention",
    )(q, k, v, gate, state0, dmask, qdec, kdec, sdec)


def _outproj_kernel(x_ref, oa_ref, yr_ref, ga_ref, gb_ref, g1_ref, sc2_ref, sh2_ref, g2_ref, n2g_ref,
                    woa_ref, wor_ref, wout_ref, wrt_ref, wsg_ref, wsu_ref, wsd_ref, *rest):
    h_ref, n2p_ref, s_ref = rest[-3:]
    ya = jnp.dot(oa_ref[...], woa_ref[...], preferred_element_type=F32)
    yr = jnp.dot(yr_ref[...], wor_ref[...], preferred_element_type=F32)
    merged = (jax.nn.sigmoid(ga_ref[...].astype(F32)) * ya
              + jax.nn.sigmoid(gb_ref[...].astype(F32)) * yr)
    mix = jnp.dot(merged.astype(BF16), wout_ref[...], preferred_element_type=F32)
    h = x_ref[...] + g1_ref[0] * mix
    hn = h * lax.rsqrt(jnp.mean(h * h, axis=-1, keepdims=True) + EPS) * n2g_ref[...]
    n2 = hn * (1.0 + sc2_ref[0]) + sh2_ref[0]
    n2b = n2.astype(BF16)
    s_ref[...] = jax.nn.sigmoid(lax.dot_general(wrt_ref[...], n2b, (((1,), (1,)), ((), ())),
                                                preferred_element_type=F32))
    hid = _silu(jnp.dot(n2b, wsg_ref[...], preferred_element_type=F32)) * jnp.dot(
        n2b, wsu_ref[...], preferred_element_type=F32)
    shared = jnp.dot(hid.astype(BF16), wsd_ref[...], preferred_element_type=F32)
    h_ref[...] = h + g2_ref[0] * shared
    half = D_MODEL // 2
    n2p_ref[...] = _pack_bf16_pair(n2[:, :half], n2[:, half:])


def _outproj(x2d, oa, yr_in, ga, gb, g1, sc2, sh2, g2, n2g, weights, tiles_per_batch,
             all_tokens, token_offset, carried=None):
    n = x2d.shape[0]
    tm = ROW_TILE
    off = token_offset // tm
    mod_rows = g1.shape[1]
    row = lambda w: pl.BlockSpec((tm, w), lambda i: (i, 0))
    mod_spec = pl.BlockSpec((1, mod_rows, D_MODEL), lambda i: (i // tiles_per_batch, 0, 0))
    const = lambda a: pl.BlockSpec(a.shape, lambda i: (0,) * a.ndim)
    in_specs = [row(D_MODEL), row(D_ATT), row(D_RET_V), row(D_MODEL), row(D_MODEL),
                mod_spec, mod_spec, mod_spec, mod_spec, const(n2g)] + [const(w) for w in weights]
    args = [x2d, oa, yr_in, ga, gb, g1, sc2, sh2, g2, n2g, *weights]
    aliases = {}
    if carried is not None:
        aliases = {len(args): 1, len(args) + 1: 2}
        in_specs += [pl.BlockSpec(memory_space=pl.ANY)] * 2
        args += list(carried)
    return pl.pallas_call(
        _outproj_kernel,
        grid=(n // tm,),
        in_specs=in_specs,
        out_specs=[row(D_MODEL),
                   pl.BlockSpec((tm, D_MODEL // 2), lambda i: (i + off, 0)),
                   pl.BlockSpec((N_EXPERTS, tm), lambda i: (0, i + off))],
        out_shape=[jax.ShapeDtypeStruct((n, D_MODEL), F32),
                   jax.ShapeDtypeStruct((all_tokens, D_MODEL // 2), U32),
                   jax.ShapeDtypeStruct((N_EXPERTS, all_tokens), F32)],
        input_output_aliases=aliases,
        compiler_params=_cparams(("arbitrary",), 48),
        name="outproj",
    )(*args)


def _route_kernel(s_ref, b_ref, idx_ref, w_ref, rank_ref, cnt_ref, run_ref, tri_ref):
    step = pl.program_id(0)
    t = s_ref.shape[1]

    @pl.when(step == 0)
    def _():
        run_ref[...] = jnp.zeros_like(run_ref)
        r = lax.broadcasted_iota(I32, (t, t), 0)
        c = lax.broadcasted_iota(I32, (t, t), 1)
        tri_ref[...] = jnp.where(r < c, 1.0, 0.0).astype(BF16)

    s = s_ref[...]
    sel = s + b_ref[...]
    row_f = lax.broadcasted_iota(I32, (N_EXPERTS, t), 0).astype(F32)

    def first_argmax(vals, rows):
        m = jnp.max(vals, axis=0, keepdims=True)
        pos = jnp.min(jnp.where(vals == m, rows, float(N_EXPERTS)), axis=0, keepdims=True)
        return m, pos

    gscore = []
    group_row = lax.broadcasted_iota(I32, (GROUP_SIZE, t), 0).astype(F32)
    for g in range(N_GROUPS):
        rows = slice(g * GROUP_SIZE, (g + 1) * GROUP_SIZE)
        m1, p1 = first_argmax(sel[rows], group_row)
        m2 = jnp.max(jnp.where(group_row == p1, -jnp.inf, sel[rows]), axis=0, keepdims=True)
        gscore.append(m1 + m2)
    cand_parts = []
    for g in range(N_GROUPS):
        rows = slice(g * GROUP_SIZE, (g + 1) * GROUP_SIZE)
        beaten_by = jnp.zeros((1, t), F32)
        for o in range(N_GROUPS):
            if o == g:
                continue
            wins = (gscore[o] > gscore[g]) if o > g else (gscore[o] >= gscore[g])
            beaten_by = beaten_by + jnp.where(wins, 1.0, 0.0)
        cand_parts.append(jnp.where(beaten_by < TOPK_GROUPS, sel[rows], -jnp.inf))
    cand = jnp.concatenate(cand_parts, axis=0)

    picked = jnp.zeros((N_EXPERTS, t), F32)
    idx_rows, w_rows = [], []
    for _ in range(TOP_K):
        _, pos = first_argmax(cand, row_f)
        hit = row_f == pos
        w_rows.append(jnp.sum(jnp.where(hit, s, 0.0), axis=0, keepdims=True))
        idx_rows.append(pos)
        picked = jnp.where(hit, 1.0, picked)
        cand = jnp.where(hit, -jnp.inf, cand)
    w_sum = functools.reduce(jnp.add, w_rows)

    before = jnp.dot(picked.astype(BF16), tri_ref[...], preferred_element_type=F32) + run_ref[...]
    run_ref[...] = run_ref[...] + jnp.sum(picked, axis=1, keepdims=True)
    rank_rows = [jnp.sum(jnp.where(row_f == idx_rows[kk], before, 0.0), axis=0, keepdims=True)
                 for kk in range(TOP_K)]

    idx_ref[...] = jnp.concatenate(idx_rows, axis=0).astype(I32)
    w_ref[...] = jnp.concatenate([w / w_sum * ROUTED_SCALE for w in w_rows], axis=0).T
    rank_ref[...] = jnp.concatenate(rank_rows, axis=0).astype(I32)

    @pl.when(step == pl.num_programs(0) - 1)
    def _():
        cnt_ref[...] = run_ref[...].astype(I32)


def _route(scores_t, b_col):
    n = scores_t.shape[1]
    t = ROUTE_TILE
    col = pl.BlockSpec((TOP_K, t), lambda i: (0, i))
    const = pl.BlockSpec((N_EXPERTS, t), lambda i: (0, 0))
    return pl.pallas_call(
        _route_kernel,
        grid=(n // t,),
        in_specs=[pl.BlockSpec((N_EXPERTS, t), lambda i: (0, i)), const],
        out_specs=[col, pl.BlockSpec((t, TOP_K), lambda i: (i, 0)), col, const],
        out_shape=[jax.ShapeDtypeStruct((TOP_K, n), I32),
                   jax.ShapeDtypeStruct((n, TOP_K), F32),
                   jax.ShapeDtypeStruct((TOP_K, n), I32),
                   jax.ShapeDtypeStruct((N_EXPERTS, t), I32)],
        scratch_shapes=[pltpu.VMEM((N_EXPERTS, t), F32), pltpu.VMEM((t, t), BF16)],
        compiler_params=_cparams(("arbitrary",), 32),
        name="route",
    )(scores_t, b_col)


def _dest_kernel(idx_ref, rank_ref, start_ref, dest_ref):
    t = idx_ref.shape[1]
    row = lax.broadcasted_iota(I32, (N_EXPERTS, t), 0)
    starts = start_ref[...]
    base = [jnp.sum(jnp.where(row == idx_ref[kk:kk + 1, :], starts, 0.0), axis=0, keepdims=True)
            for kk in range(TOP_K)]
    dest_ref[...] = jnp.concatenate(base, axis=0).astype(I32) + rank_ref[...]


def _dest(idx_t, rank_t, starts_col):
    n = idx_t.shape[1]
    t = ROUTE_TILE
    col = pl.BlockSpec((TOP_K, t), lambda i: (0, i))
    return pl.pallas_call(
        _dest_kernel,
        grid=(n // t,),
        in_specs=[col, col, pl.BlockSpec((N_EXPERTS, t), lambda i: (0, 0))],
        out_specs=col,
        out_shape=jax.ShapeDtypeStruct((TOP_K, n), I32),
        compiler_params=_cparams(("arbitrary",), 32),
        name="dest",
    )(idx_t, rank_t, starts_col)


def _sc_mesh():
    return plsc.VectorSubcoreMesh(core_axis_name="core", subcore_axis_name="subcore",
                                  num_cores=V7X_SC_CORES, num_subcores=V7X_SC_SUBCORES)


def _sc_dispatch(n2p, dest_kmajor, pad_rows, n_out_rows):
    n, width = n2p.shape
    rows = SC_GATHER_ROWS
    n_workers = V7X_SC_CORES * V7X_SC_SUBCORES
    src_chunks = n // rows
    items = dest_kmajor.shape[0] // rows
    per_worker = items // n_workers
    pad_per_worker = pad_rows.shape[0] // rows // n_workers
    assert src_chunks * rows == n and per_worker * n_workers == items and per_worker % 2 == 0
    assert pad_per_worker * n_workers * rows == pad_rows.shape[0]
    idx3 = dest_kmajor.reshape(n_workers, per_worker, rows)
    pad3 = pad_rows.reshape(n_workers, pad_per_worker, rows)
    zeros = jnp.zeros((rows, width), n2p.dtype)

    def body(src_hbm, idx_hbm, pad_hbm, zero_hbm, out_hbm, idx_v, pad_v, rows_v, load_sem, scat_sem):
        worker = lax.axis_index("subcore") * V7X_SC_CORES + lax.axis_index("core")
        pltpu.sync_copy(idx_hbm.at[worker], idx_v)
        pltpu.sync_copy(pad_hbm.at[worker], pad_v)
        pltpu.sync_copy(zero_hbm, rows_v.at[0])

        def zero_fill(c):
            return pltpu.make_async_copy(rows_v.at[0], out_hbm.at[pad_v.at[c]], scat_sem.at[0])

        @pl.loop(0, pad_per_worker)
        def _(c):
            zero_fill(c).start()

        @pl.loop(0, pad_per_worker)
        def _(c):
            zero_fill(c).wait()

        def load(c, b):
            chunk = lax.rem(worker * per_worker + c, src_chunks)
            off = pl.multiple_of(chunk * rows, rows)
            return pltpu.make_async_copy(src_hbm.at[pl.ds(off, rows)], rows_v.at[b], load_sem.at[b])

        def scatter(c, b):
            return pltpu.make_async_copy(rows_v.at[b], out_hbm.at[idx_v.at[c]], scat_sem.at[b])

        load(0, 0).start()

        @pl.loop(0, per_worker, step=2)
        def _(c0):
            for b in range(2):
                c = c0 + b
                load(c, b).wait()

                @pl.when(c >= 1)
                def _():
                    scatter(c - 1, 1 - b).wait()

                @pl.when(c + 1 < per_worker)
                def _():
                    load(c + 1, 1 - b).start()

                scatter(c, b).start()

        scatter(per_worker - 1, (per_worker - 1) % 2).wait()

    return pl.kernel(
        body, mesh=_sc_mesh(),
        out_type=jax.ShapeDtypeStruct((n_out_rows, width), n2p.dtype),
        scratch_types=[pltpu.VMEM((per_worker, rows), I32),
                       pltpu.VMEM((pad_per_worker, rows), I32),
                       pltpu.VMEM((2, rows, width), n2p.dtype),
                       pltpu.SemaphoreType.DMA((2,)),
                       pltpu.SemaphoreType.DMA((2,))],
        name="sc_dispatch",
    )(n2p, idx3, pad3, zeros)


def _ffn_kernel(texp_ref, ntiles_ref, eslot_ref, enext_ref, nhalf_ref, xs_ref, wg_hbm, wu_hbm, wd_hbm,
                ys_ref, wg_buf, wu_buf, wd_buf, wgu_bf, wd_bf, sems):
    g = pl.program_id(0)

    def weight_copies(e, slot):
        return (pltpu.make_async_copy(wg_hbm.at[e], wg_buf.at[slot], sems.at[slot, 0]),
                pltpu.make_async_copy(wu_hbm.at[e], wu_buf.at[slot], sems.at[slot, 1]),
                pltpu.make_async_copy(wd_hbm.at[e], wd_buf.at[slot], sems.at[slot, 2]))

    @pl.when(g < ntiles_ref[0])
    def _():
        e = texp_ref[g]
        changed = jnp.logical_or(g == 0, texp_ref[jnp.maximum(g - 1, 0)] != e)

        @pl.when(changed)
        def _():
            slot = eslot_ref[e]

            def fetch_ahead(first_hop, hops, target_slot):
                ahead = first_hop
                for _ in range(hops - 1):
                    ahead = jnp.where(ahead >= 0, enext_ref[jnp.maximum(ahead, 0)], -1)

                @pl.when(ahead >= 0)
                def _():
                    for c in weight_copies(ahead, target_slot):
                        c.start()

            @pl.when(g == 0)
            def _():
                for c in weight_copies(e, slot):
                    c.start()
                for hops in range(1, FFN_WEIGHT_SLOTS - 1):
                    fetch_ahead(enext_ref[e], hops, lax.rem(slot + hops, FFN_WEIGHT_SLOTS))

            fetch_ahead(enext_ref[e], FFN_WEIGHT_SLOTS - 1,
                        lax.rem(slot + FFN_WEIGHT_SLOTS - 1, FFN_WEIGHT_SLOTS))
            for c in weight_copies(e, slot):
                c.wait()

            wgu_bf[:, :D_EXPERT] = wg_buf[slot].astype(BF16)
            wgu_bf[:, D_EXPERT:] = wu_buf[slot].astype(BF16)
            wd_bf[...] = wd_buf[slot].astype(BF16)

        def expert_rows(rows):
            lo, hi = _unpack_bf16_pair(xs_ref[rows, :])
            x = jnp.concatenate([lo, hi], axis=1).astype(BF16)
            gu = jnp.dot(x, wgu_bf[...], preferred_element_type=F32)
            hid = (_silu(gu[:, :D_EXPERT]) * gu[:, D_EXPERT:]).astype(BF16)
            y = jnp.dot(hid, wd_bf[...], preferred_element_type=F32)
            half = D_MODEL // 2
            ys_ref[rows, :] = _pack_bf16_pair(y[:, :half], y[:, half:])

        for n_groups in FFN_TILE_SIZES:

            @pl.when(nhalf_ref[g] == n_groups)
            def _(n_groups=n_groups):
                used = n_groups * FFN_FINE
                expert_rows(slice(0, used))
                if used < FFN_TILE:
                    ys_ref[used:, :] = jnp.zeros((FFN_TILE - used, ys_ref.shape[1]), U32)


def _ffn(tile_expert, n_tiles, expert_slot, expert_next, tile_halves, xs, w_gate, w_up, w_down):
    rows, width = xs.shape
    m = FFN_TILE
    max_tiles = tile_expert.shape[0]
    row_map = lambda g, te, nt, es, en, nh: (jnp.minimum(g, nt[0] - 1), 0)
    hbm = pl.BlockSpec(memory_space=pl.ANY)
    grid_spec = pltpu.PrefetchScalarGridSpec(
        num_scalar_prefetch=5,
        grid=(max_tiles,),
        in_specs=[pl.BlockSpec((m, width), row_map), hbm, hbm, hbm],
        out_specs=pl.BlockSpec((m, width), row_map),
        scratch_shapes=[pltpu.VMEM((FFN_WEIGHT_SLOTS, D_MODEL, D_EXPERT), F32),
                        pltpu.VMEM((FFN_WEIGHT_SLOTS, D_MODEL, D_EXPERT), F32),
                        pltpu.VMEM((FFN_WEIGHT_SLOTS, D_EXPERT, D_MODEL), F32),
                        pltpu.VMEM((D_MODEL, 2 * D_EXPERT), BF16),
                        pltpu.VMEM((D_EXPERT, D_MODEL), BF16),
                        pltpu.SemaphoreType.DMA((FFN_WEIGHT_SLOTS, 3))],
    )
    return pl.pallas_call(
        _ffn_kernel,
        grid_spec=grid_spec,
        out_shape=jax.ShapeDtypeStruct((rows, width), U32),
        compiler_params=_cparams(("arbitrary",), 32),
        name="ffn",
    )(tile_expert, n_tiles, expert_slot, expert_next, tile_halves, xs, w_gate, w_up, w_down)


def _ffn_plan(counts, n_assign):
    m = FFN_TILE
    max_tiles = n_assign // m + N_EXPERTS
    padded = ((counts + m - 1) // m) * m
    pend = jnp.cumsum(padded).astype(I32)
    pstart = pend - padded
    n_tiles = pend[-1:] // m
    g = jnp.minimum(jnp.arange(max_tiles, dtype=I32), n_tiles - 1)
    tile_expert = jnp.sum((pend[None, :] <= (g * m)[:, None]).astype(I32), axis=1)
    tile_expert = jnp.minimum(tile_expert, N_EXPERTS - 1)
    full_rows = jnp.maximum(padded - m, 0)
    last_rows = counts - full_rows
    round_to = lambda v, q: ((v + q - 1) // q) * q
    last_rows = jnp.where(last_rows > FFN_FINE_FROM, round_to(last_rows, FFN_FINE),
                          round_to(last_rows, FFN_QUANTUM))
    vend = pstart + full_rows + last_rows
    own = tile_expert[:, None] == jnp.arange(N_EXPERTS, dtype=I32)[None, :]
    tile_vend = jnp.sum(jnp.where(own, vend[None, :], 0), axis=1)
    tile_halves = (jnp.clip(tile_vend - g * m, 0, m) // FFN_FINE).astype(I32)
    used = counts > 0
    expert_slot = ((jnp.cumsum(used.astype(I32)) - 1) % FFN_WEIGHT_SLOTS).astype(I32)
    ids = jnp.where(used, jnp.arange(N_EXPERTS, dtype=I32), N_EXPERTS)
    first_used_from = lax.cummin(ids, axis=0, reverse=True)
    nxt = jnp.concatenate([first_used_from[1:], jnp.full((1,), N_EXPERTS, I32)])
    expert_next = jnp.where(nxt < N_EXPERTS, nxt, -1).astype(I32)
    spare_row = max_tiles * m
    j = jnp.arange(FFN_QUANTUM, dtype=I32)[None, :]
    seg_end = (pstart + counts)[:, None]
    spare = spare_row + (jnp.arange(N_EXPERTS, dtype=I32)[:, None] * FFN_QUANTUM + j) % SPARE_ROWS
    pad_rows = jnp.where(j < (vend[:, None] - seg_end), seg_end + j, spare).astype(I32).reshape(-1)
    unit = SC_GATHER_ROWS * V7X_SC_CORES * V7X_SC_SUBCORES
    extra = (-pad_rows.shape[0]) % unit
    filler = spare_row + (pad_rows.shape[0] + jnp.arange(extra, dtype=I32)) % SPARE_ROWS
    pad_rows = jnp.concatenate([pad_rows, filler])
    return (tile_expert, n_tiles, expert_slot, expert_next, tile_halves, pstart, pad_rows,
            spare_row + SPARE_ROWS)


def _sc_gather_rows(table, idx):
    n_idx = idx.shape[0]
    width = table.shape[1]
    n_workers = V7X_SC_CORES * V7X_SC_SUBCORES
    per_worker = n_idx // n_workers
    n_chunks = per_worker // SC_GATHER_ROWS
    assert per_worker * n_workers == n_idx and n_chunks * SC_GATHER_ROWS == per_worker and n_chunks % 2 == 0

    def body(table_hbm, idx_hbm, out_hbm, idx_v, rows_v, gather_sem, write_sem):
        worker = lax.axis_index("subcore") * V7X_SC_CORES + lax.axis_index("core")
        base = worker * per_worker
        pltpu.sync_copy(idx_hbm.at[pl.ds(base, per_worker)], idx_v)

        def gather(c, b):
            off = pl.multiple_of(c * SC_GATHER_ROWS, SC_GATHER_ROWS)
            return pltpu.make_async_copy(table_hbm.at[idx_v.at[pl.ds(off, SC_GATHER_ROWS)]],
                                         rows_v.at[b], gather_sem.at[b])

        def write(c, b):
            off = pl.multiple_of(c * SC_GATHER_ROWS, SC_GATHER_ROWS)
            return pltpu.make_async_copy(rows_v.at[b], out_hbm.at[pl.ds(base + off, SC_GATHER_ROWS)],
                                         write_sem.at[b])

        gather(0, 0).start()

        @pl.loop(0, n_chunks, step=2)
        def _(c0):
            for b in range(2):
                c = c0 + b
                gather(c, b).wait()

                @pl.when(c >= 1)
                def _():
                    write(c - 1, 1 - b).wait()

                @pl.when(c + 1 < n_chunks)
                def _():
                    gather(c + 1, 1 - b).start()

                write(c, b).start()

        write(n_chunks - 1, (n_chunks - 1) % 2).wait()

    return pl.kernel(
        body, mesh=_sc_mesh(),
        out_type=jax.ShapeDtypeStruct((n_idx, width), table.dtype),
        scratch_types=[pltpu.VMEM((per_worker,), I32),
                       pltpu.VMEM((2, SC_GATHER_ROWS, width), table.dtype),
                       pltpu.SemaphoreType.DMA((2,)),
                       pltpu.SemaphoreType.DMA((2,))],
        name="sc_gather_rows",
    )(table, idx)


def _combine_kernel(w_ref, h_ref, g2_ref, nf_ref, yk_ref, y_ref):
    t = h_ref.shape[0]
    w = w_ref[...]
    acc_lo = jnp.zeros((t, D_MODEL // 2), F32)
    acc_hi = jnp.zeros((t, D_MODEL // 2), F32)
    for kk in range(TOP_K):
        lo, hi = _unpack_bf16_pair(yk_ref[kk])
        wk = w[:, kk:kk + 1]
        acc_lo = acc_lo + wk * lo
        acc_hi = acc_hi + wk * hi
    out = h_ref[...] + g2_ref[0] * jnp.concatenate([acc_lo, acc_hi], axis=1)
    y_ref[...] = out * lax.rsqrt(jnp.mean(out * out, axis=-1, keepdims=True) + EPS) * nf_ref[...]


def _combine(w, h2, g2, normf, y_by_k, row_offset, tiles_per_batch):
    n = h2.shape[0]
    t = MOVE_TILE
    off = row_offset // t
    mod_rows = g2.shape[1]
    mod_tiles = max(ROW_TILE // t, 1) * tiles_per_batch if mod_rows == 1 else n // t
    return pl.pallas_call(
        _combine_kernel,
        grid=(n // t,),
        in_specs=[pl.BlockSpec((t, TOP_K), lambda i: (i + off, 0)),
                  pl.BlockSpec((t, D_MODEL), lambda i: (i, 0)),
                  pl.BlockSpec((1, mod_rows if mod_rows == 1 else t, D_MODEL),
                               (lambda i: (i // mod_tiles, 0, 0)) if mod_rows == 1
                               else (lambda i: (0, i, 0))),
                  pl.BlockSpec((1, D_MODEL), lambda i: (0, 0)),
                  pl.BlockSpec((TOP_K, t, D_MODEL // 2), lambda i: (0, i + off, 0))],
        out_specs=pl.BlockSpec((t, D_MODEL), lambda i: (i, 0)),
        out_shape=jax.ShapeDtypeStruct((n, D_MODEL), F32),
        compiler_params=_cparams(("arbitrary",), 40),
        name="combine",
    )(w, h2, g2, normf, y_by_k)


def _rotary_tables(pos):
    half = KEY_DIM_R // 2
    inv_freq = ROPE_BASE ** (-jnp.arange(half, dtype=F32) / half)
    ang = pos[:, None] * inv_freq[None, :]
    cos = jnp.cos(ang)
    sin = jnp.sin(ang)
    cos_t = jnp.concatenate([cos, cos], axis=1)
    sin_t = jnp.concatenate([-sin, sin], axis=1)
    return cos_t.astype(F32), sin_t.astype(F32)


def _rel_bias_table(rel_bias, n_rows, n_cols, q_offset):
    heads = rel_bias.shape[0]
    n_diag = n_rows + n_cols - 1
    dist = q_offset + (n_rows - 1) - np.arange(n_diag)
    idx = np.clip(dist, -REL_CLIP, REL_CLIP) + REL_CLIP
    n_hi = int(np.sum(dist > REL_CLIP))
    n_lo = int(np.sum(dist < -REL_CLIP))
    mid = rel_bias[:, int(idx[n_diag - n_lo - 1]):int(idx[n_hi]) + 1][:, ::-1]
    diag = jnp.concatenate([jnp.broadcast_to(rel_bias[:, 2 * REL_CLIP:], (heads, n_hi)), mid,
                            jnp.broadcast_to(rel_bias[:, :1], (heads, n_lo))], axis=1)
    period = n_diag + 1
    v = jnp.roll(jnp.pad(diag, ((0, 0), (0, 1))), -(n_rows - 1), axis=1)
    skew = jnp.tile(v, (1, n_rows))[:, :n_rows * (period - 1)].reshape(heads, n_rows, period - 1)
    return skew[:, :, :n_cols].astype(F32)


def _prompt_bias(rel_bias):
    n_cols = ATT_QB + ATT_WINDOW
    r = np.arange(ATT_QB)[:, None]
    c = np.arange(n_cols)[None, :]
    band = c - (r // CHUNK) * CHUNK
    valid = (band >= 0) & (band < ATT_WINDOW + CHUNK)
    table = _rel_bias_table(rel_bias, ATT_QB, n_cols, ATT_WINDOW)
    return jnp.where(jnp.asarray(valid)[None], table, NEG_BIG)


def _sample_bias(rel_bias, t_new, cache_len):
    b = _rel_bias_table(rel_bias, t_new, cache_len + t_new, cache_len)
    return b[:, :, :cache_len], b[:, :, cache_len:]


def _mod_parts(mod, rows_each):
    parts = jnp.split(mod, 6, axis=-1)
    if rows_each == 1:
        return [p[:, None, :] for p in parts]
    return [jnp.repeat(p, rows_each, axis=0)[None] for p in parts]


def kernel(x_prompt, x_sample, cache_attn_k, cache_attn_v, state_ret, c_prompt, c_sample,
           norm1_g, norm2_g, w_ada, b_ada, w_in, rel_bias, w_o_attn, w_o_ret, w_out,
           w_router, b_router, w_exp_gate, w_exp_up, w_exp_down, w_sh_gate, w_sh_up, w_sh_down,
           normf_g):
    batch, seq, d = x_prompt.shape
    dec_batch, dec_seq, _ = x_sample.shape
    depth = w_in.shape[0]
    assert depth == 1 and d == D_MODEL
    assert seq % ROW_TILE == 0 and dec_batch * dec_seq == ROW_TILE and ROW_TILE == ATT_WINDOW
    cache_len = cache_attn_k.shape[2]
    n_p = batch * seq
    n_s = dec_batch * dec_seq
    tpb = seq // ROW_TILE
    l = 0

    bf = lambda a: a.astype(BF16)
    c_all = jnp.concatenate([c_prompt, c_sample], axis=0)
    pad = (-c_all.shape[0]) % 8
    c_all = jnp.pad(c_all, ((0, pad), (0, 0)))
    mod = _ada(c_all, bf(w_ada[l]), b_ada[l][None, :])
    mod_p = _mod_parts(mod[:batch], 1)
    mod_s = _mod_parts(mod[batch:batch + dec_batch], dec_seq)

    w_in_bf = bf(w_in[l])
    n1g = norm1_g[l][None, :]
    n2g = norm2_g[l][None, :]
    dense_w = [bf(w_o_attn[l]), bf(w_o_ret[l]), bf(w_out[l]), bf(w_router[l]).T,
               bf(w_sh_gate[l]), bf(w_sh_up[l]), bf(w_sh_down[l])]

    xp = x_prompt.reshape(n_p, d)
    xs_ = x_sample.reshape(n_s, d)
    cos_p, sin_p = _rotary_tables(jnp.arange(seq, dtype=F32))
    pos_s = PAST_LEN + jnp.arange(dec_seq, dtype=F32)
    cos_s, sin_s = _rotary_tables(jnp.tile(pos_s, dec_batch))

    (qa, ka, va, qr, kr, vr, gr, ga, gb, kv_p) = _inproj(
        xp, mod_p[1], mod_p[0], n1g, cos_p, sin_p, w_in_bf, tpb)
    oa = _attn_prompt(qa, ka, va, _prompt_bias(rel_bias[l]), batch, seq)
    zero_state = jnp.zeros((batch, N_HEADS, KEY_DIM_R, VAL_DIM_R), F32)
    yr_in, state_p = _retention(qr, kr, vr, gr, zero_state, batch, seq, RET_CHUNK)
    h_p, n2p_p, s_p = _outproj(xp, oa, yr_in, ga, gb, mod_p[2], mod_p[4], mod_p[3], mod_p[5], n2g,
                               dense_w, tpb, n_p + n_s, 0)

    (qa_s, ka_s, va_s, qr_s, kr_s, vr_s, gr_s, ga_s, gb_s, kv_s) = _inproj(
        xs_, mod_s[1], mod_s[0], n1g, cos_s, sin_s, w_in_bf, 1)
    bias_c, bias_n = _sample_bias(rel_bias[l], dec_seq, cache_len)
    to_keys_minor = lambda c: jnp.transpose(c, (0, 1, 3, 4, 2))
    oa_s = _attn_sample(qa_s, ka_s, va_s, to_keys_minor(cache_attn_k), to_keys_minor(cache_attn_v),
                        bias_c, bias_n, dec_batch, dec_seq, cache_len)
    yr_in_s, state_s = _retention(qr_s, kr_s, vr_s, gr_s, state_ret[l], dec_batch, dec_seq, dec_seq)
    h_s, n2p, scores_t = _outproj(xs_, oa_s, yr_in_s, ga_s, gb_s, mod_s[2], mod_s[4], mod_s[3],
                                  mod_s[5], n2g, dense_w, 1, n_p + n_s, n_p, carried=(n2p_p, s_p))

    lanes_of = lambda v: jnp.broadcast_to(v[:, None], (N_EXPERTS, ROUTE_TILE))
    idx_t, w_t, rank_t, counts = _route(scores_t, lanes_of(b_router[l]))
    (tile_expert, n_tiles, expert_slot, expert_next, tile_halves, pstart, pad_rows,
     n_sorted_rows) = _ffn_plan(counts[:, 0], (n_p + n_s) * TOP_K)
    dest_t = _dest(idx_t, rank_t, lanes_of(pstart.astype(F32)))
    dest_kmajor = dest_t.reshape(-1)
    w_route = w_t
    xs_sorted = _sc_dispatch(n2p, dest_kmajor, pad_rows, n_sorted_rows)
    ys_sorted = _ffn(tile_expert, n_tiles, expert_slot, expert_next, tile_halves, xs_sorted,
                     w_exp_gate[l], w_exp_up[l], w_exp_down[l])
    nf = normf_g[None, :]
    y_by_k = _sc_gather_rows(ys_sorted, dest_kmajor).reshape(TOP_K, n_p + n_s, d // 2)
    y_p = _combine(w_route, h_p, mod_p[5], nf, y_by_k, 0, tpb)
    y_s = _combine(w_route, h_s, mod_s[5], nf, y_by_k, n_p, 1)

    keep = min(ATT_WINDOW, seq)
    kv_p = kv_p.reshape(batch, ROW_TILE, 2, N_HEADS, HEAD_DIM_A)[:, ROW_TILE - keep:]
    kv_s = kv_s.reshape(dec_batch, dec_seq, 2, N_HEADS, HEAD_DIM_A)
    return (y_p.reshape(batch, seq, d), y_s.reshape(dec_batch, dec_seq, d),
            kv_p[:, :, 0][None], kv_p[:, :, 1][None], state_p[None],
            kv_s[:, :, 0][None], kv_s[:, :, 1][None], state_s[None])
```
